```python
import math
import numpy as np
import jax
import jax.numpy as jnp
from jax import lax

D_MODEL = 2048
BATCH = 4
SEQ = 4096
DEPTH = 1

HEAD_DIM = 128
MIX_WIDTH = D_MODEL
N_HEADS_NSA = MIX_WIDTH // 2 // HEAD_DIM
N_KV_NSA = 2
N_HEADS_GDN = MIX_WIDTH // 2 // HEAD_DIM
ROT_DIM = HEAD_DIM // 4
ROPE_THETA = 500000.0
CMP_LEN = 32
CMP_STRIDE = 16
SLC_LEN = 64
SLC_TOP = 16
WINDOW = 512
WIN_QBLOCK = 128
SLC_QBLOCK = 64
N_NSA_BRANCH = 3
CONV_WIDTH = 4
GDN_CHUNK = 64
N_GROUPS = 8
EXPERTS_PER_GROUP = 8
N_EXPERTS = N_GROUPS * EXPERTS_PER_GROUP
TOP_K_IN_GROUP = 2
D_EXPERT = D_MODEL // 4
MOE_ROW_BLOCK = 256
EPS = 1e-6

NSA_Q_COLS = N_HEADS_NSA * HEAD_DIM
NSA_KV_COLS = N_NSA_BRANCH * 2 * N_KV_NSA * HEAD_DIM
NSA_GATE_COLS = N_NSA_BRANCH * N_HEADS_NSA
GDN_QKV_COLS = 3 * N_HEADS_GDN * HEAD_DIM
GDN_BETA_COLS = N_HEADS_GDN
GDN_DECAY_COLS = N_HEADS_GDN
GDN_GATE_COLS = N_HEADS_GDN * HEAD_DIM
IN_SPLITS = (NSA_Q_COLS, NSA_KV_COLS, NSA_GATE_COLS, GDN_QKV_COLS, GDN_BETA_COLS, GDN_DECAY_COLS, GDN_GATE_COLS)
D_IN_PROJ = sum(IN_SPLITS)

kernel_name = "hymba_nsa_gdn_hier_moe_block"


def rmsnorm(x, w):
    xf = x.astype(jnp.float32)
    y = xf * lax.rsqrt(jnp.mean(xf * xf, axis=-1, keepdims=True) + EPS)
    return (y * w.astype(jnp.float32)).astype(x.dtype)


def l2norm(x):
    return x * lax.rsqrt(jnp.sum(x * x, axis=-1, keepdims=True) + EPS)


def masked_softmax(s, mask):
    s = jnp.where(mask, s.astype(jnp.float32), -jnp.inf)
    m = jnp.max(s, axis=-1, keepdims=True)
    m = jnp.where(jnp.isfinite(m), m, 0.0)
    p = jnp.exp(s - m)
    return p / jnp.maximum(jnp.sum(p, axis=-1, keepdims=True), 1e-30)


def rope_tables(positions):
    inv_freq = ROPE_THETA ** (-jnp.arange(0, ROT_DIM, 2, dtype=jnp.float32) / ROT_DIM)
    ang = positions.astype(jnp.float32)[..., None] * inv_freq
    return jnp.cos(ang)[:, :, None, :], jnp.sin(ang)[:, :, None, :]


def partial_rope(x, cos, sin):
    half = ROT_DIM // 2
    x1, x2, xp = x[..., :half], x[..., half:ROT_DIM], x[..., ROT_DIM:]
    rot = jnp.concatenate([x1 * cos - x2 * sin, x1 * sin + x2 * cos], axis=-1)
    return jnp.concatenate([rot.astype(x.dtype), xp], axis=-1)


def nsa_compress(a, w, pe):
    T = a.shape[1]
    n_cmp = (T - CMP_LEN) // CMP_STRIDE + 1
    idx = np.arange(n_cmp)[:, None] * CMP_STRIDE + np.arange(CMP_LEN)[None, :]
    blocks = a[:, idx]
    return jnp.einsum('bnlgd,lde->bgne', blocks + pe[None, None, :, None, :], w)


def nsa_compressed(qg, k, v, wk, pek, wv, pev):
    B, G, R, T, D = qg.shape
    kc, vc = nsa_compress(k, wk, pek), nsa_compress(v, wv, pev)
    n_cmp = kc.shape[2]
    s = jnp.einsum('bgrtd,bgnd->bgrtn', qg, kc) * (HEAD_DIM ** -0.5)
    blk_end = jnp.arange(n_cmp) * CMP_STRIDE + CMP_LEN - 1
    mask = blk_end[None, :] <= jnp.arange(T)[:, None]
    p = masked_softmax(s, mask)
    o = jnp.einsum('bgrtn,bgnd->bgrtd', p, vc)
    return o, p


def nsa_selected(qg, k, v, p_cmp):
    B, G, R, T, D = qg.shape
    n_cmp = p_cmp.shape[-1]
    n_slc = T // SLC_LEN
    n_sel = min(SLC_TOP, n_slc)
    ci = np.arange(n_cmp)[:, None] * CMP_STRIDE
    sj = np.arange(n_slc)[None, :] * SLC_LEN
    agg = jnp.asarray(((ci < sj + SLC_LEN) & (ci + CMP_LEN > sj)).astype(np.float32))
    imp = jnp.einsum('bgrtn,nm->bgtm', p_cmp, agg)
    t = jnp.arange(T)[:, None]
    j = jnp.arange(n_slc)[None, :]
    cur = t // SLC_LEN
    valid = j <= cur
    forced = (j == 0) | (j == cur) | (j == cur - 1)
    imp = jnp.where(forced, jnp.inf, jnp.where(valid, imp, -jnp.inf))
    _, sel = lax.top_k(imp, n_sel)

    kb = k.transpose(0, 2, 1, 3).reshape(B, G, n_slc, SLC_LEN, D)
    vb = v.transpose(0, 2, 1, 3).reshape(B, G, n_slc, SLC_LEN, D)
    nqb = T // SLC_QBLOCK
    q_blocks = qg.reshape(B, G, R, nqb, SLC_QBLOCK, D).transpose(3, 0, 1, 2, 4, 5)
    sel_blocks = sel.reshape(B, G, nqb, SLC_QBLOCK, n_sel).transpose(2, 0, 1, 3, 4)
    t_blocks = jnp.arange(T).reshape(nqb, SLC_QBLOCK)
    bi = jnp.arange(B)[:, None, None, None]
    gi = jnp.arange(G)[None, :, None, None]

    def one_block(args):
        qb, sb, tb = args
        ks = kb[bi, gi, sb].reshape(B, G, SLC_QBLOCK, n_sel * SLC_LEN, D)
        vs = vb[bi, gi, sb].reshape(B, G, SLC_QBLOCK, n_sel * SLC_LEN, D)
        kpos = (sb[..., None] * SLC_LEN + jnp.arange(SLC_LEN)).reshape(B, G, SLC_QBLOCK, n_sel * SLC_LEN)
        mask = (kpos <= tb[None, None, :, None])[:, :, None]
        s = jnp.einsum('bgrqd,bgqkd->bgrqk', qb, ks) * (HEAD_DIM ** -0.5)
        p = masked_softmax(s, mask)
        return jnp.einsum('bgrqk,bgqkd->bgrqd', p, vs)

    o = lax.map(one_block, (q_blocks, sel_blocks, t_blocks))
    return o.transpose(1, 2, 3, 0, 4, 5).reshape(B, G, R, T, D)


def nsa_window(qg, k, v):
    B, G, R, T, D = qg.shape
    nb = T // WIN_QBLOCK
    n_band = WINDOW // WIN_QBLOCK + 1

    def band(a):
        a = jnp.pad(a.transpose(0, 2, 1, 3), ((0, 0), (0, 0), (WINDOW, 0), (0, 0)))
        a = a.reshape(B, G, nb + n_band - 1, WIN_QBLOCK, D)
        return jnp.concatenate([a[:, :, i:i + nb] for i in range(n_band)], axis=3)

    kb, vb = band(k), band(v)
    qb = qg.reshape(B, G, R, nb, WIN_QBLOCK, D)
    qi = jnp.arange(WIN_QBLOCK)[:, None]
    kj = jnp.arange(n_band * WIN_QBLOCK)[None, :]
    dist = WINDOW + qi - kj
    kpos = jnp.arange(nb)[:, None, None] * WIN_QBLOCK - WINDOW + kj[None]
    mask = (dist >= 0) & (dist < WINDOW) & (kpos >= 0)
    s = jnp.einsum('bgrnqd,bgnkd->bgrnqk', qb, kb) * (HEAD_DIM ** -0.5)
    p = masked_softmax(s, mask)
    o = jnp.einsum('bgrnqk,bgnkd->bgrnqd', p, vb)
    return o.reshape(B, G, R, T, D)


def nsa_mixer(q_raw, kv_raw, gate_raw, cos, sin, cmp_wk, cmp_pek, cmp_wv, cmp_pev):
    B, T, _ = q_raw.shape
    H, G, D = N_HEADS_NSA, N_KV_NSA, HEAD_DIM
    R = H // G
    q = partial_rope(q_raw.reshape(B, T, H, D), cos, sin)
    kv = kv_raw.reshape(B, T, N_NSA_BRANCH, 2, G, D)
    k_c, v_c = partial_rope(kv[:, :, 0, 0], cos, sin), kv[:, :, 0, 1]
    k_s, v_s = partial_rope(kv[:, :, 1, 0], cos, sin), kv[:, :, 1, 1]
    k_w, v_w = partial_rope(kv[:, :, 2, 0], cos, sin), kv[:, :, 2, 1]
    qg = q.reshape(B, T, G, R, D).transpose(0, 2, 3, 1, 4)
    o_c, p_c = nsa_compressed(qg, k_c, v_c, cmp_wk, cmp_pek, cmp_wv, cmp_pev)
    o_s = nsa_selected(qg, k_s, v_s, p_c)
    o_w = nsa_window(qg, k_w, v_w)
    to_bthd = lambda o: o.transpose(0, 3, 1, 2, 4).reshape(B, T, H, D)
    gates = jax.nn.sigmoid(gate_raw.astype(jnp.float32)).reshape(B, T, H, N_NSA_BRANCH)
    o = (gates[..., 0:1] * to_bthd(o_c) + gates[..., 1:2] * to_bthd(o_s)
         + gates[..., 2:3] * to_bthd(o_w))
    return o.reshape(B, T, H * D).astype(q_raw.dtype)


def causal_depthwise_conv(x, w):
    C = x.shape[-1]
    return lax.conv_general_dilated(x, w[:, None, :].astype(x.dtype), window_strides=(1,),
                                    padding=[(CONV_WIDTH - 1, 0)],
                                    dimension_numbers=('NWC', 'WIO', 'NWC'),
                                    feature_group_count=C)


def chunked_gated_delta(q, k, v, g, beta):
    B, H, T, Dk = q.shape
    Dv = v.shape[-1]
    C = GDN_CHUNK
    N = T // C
    q = q * (Dk ** -0.5)
    kb = k * beta[..., None]
    vb = v * beta[..., None]
    rs = lambda a: a.reshape(B, H, N, C, a.shape[-1])
    q, k, v, kb, vb = rs(q), rs(k), rs(v), rs(kb), rs(vb)
    g = jnp.cumsum(g.reshape(B, H, N, C), axis=-1)
    tril = jnp.tril(jnp.ones((C, C), dtype=bool))
    strict = jnp.tril(jnp.ones((C, C), dtype=bool), -1)
    decay = jnp.exp(jnp.where(tril, g[..., :, None] - g[..., None, :], -jnp.inf))
    lower = jnp.where(strict, jnp.einsum('bhncd,bhnsd->bhncs', kb, k) * decay, 0.0)
    eye = jnp.eye(C, dtype=q.dtype)
    t_inv = lax.linalg.triangular_solve(eye + lower, jnp.broadcast_to(eye, lower.shape),
                                        left_side=True, lower=True, unit_diagonal=True)
    u = t_inv @ vb
    w = t_inv @ (kb * jnp.exp(g)[..., None])
    attn = jnp.where(tril, jnp.einsum('bhncd,bhnsd->bhncs', q, k) * decay, 0.0)
    q_dec = q * jnp.exp(g)[..., None]
    k_dec = k * jnp.exp(g[..., -1:] - g)[..., None]
    g_last = jnp.exp(g[..., -1])

    def step(S, xs):
        u_i, w_i, a_i, q_i, k_i, gl_i = xs
        v_new = u_i - w_i @ S
        o_i = q_i @ S + a_i @ v_new
        S = S * gl_i[..., None, None] + jnp.swapaxes(k_i, -1, -2) @ v_new
        return S, o_i

    front = lambda a: jnp.moveaxis(a, 2, 0)
    S0 = jnp.zeros((B, H, Dk, Dv), q.dtype)
    _, o = lax.scan(step, S0, (front(u), front(w), front(attn), front(q_dec), front(k_dec), front(g_last)))
    return o.transpose(1, 2, 0, 3, 4).reshape(B, H, T, Dv)


def gdn_mixer(qkv_raw, b_raw, a_raw, z_raw, conv_w, a_log, dt_bias, norm_w):
    B, T, _ = qkv_raw.shape
    H, D = N_HEADS_GDN, HEAD_DIM
    qkv = jax.nn.silu(causal_depthwise_conv(qkv_raw, conv_w)).astype(jnp.float32)
    q, k, v = [a.reshape(B, T, H, D).transpose(0, 2, 1, 3) for a in jnp.split(qkv, 3, axis=-1)]
    q, k = l2norm(q), l2norm(k)
    beta = jax.nn.sigmoid(b_raw.astype(jnp.float32)).transpose(0, 2, 1)
    g = (-jnp.exp(a_log.astype(jnp.float32))
         * jax.nn.softplus(a_raw.astype(jnp.float32) + dt_bias.astype(jnp.float32))).transpose(0, 2, 1)
    o = chunked_gated_delta(q, k, v, g, beta).transpose(0, 2, 1, 3)
    o = rmsnorm(o, norm_w) * jax.nn.silu(z_raw.reshape(B, T, H, D).astype(jnp.float32))
    return o.reshape(B, T, H * D).astype(qkv_raw.dtype)


def hier_moe(h, wr_grp, br_grp, wr_exp, br_exp, w_gate, w_up, w_down):
    B, T, D = h.shape
    N = B * T
    hf = h.reshape(N, D)
    grp_logits = (hf @ wr_grp).astype(jnp.float32) + br_grp.astype(jnp.float32)
    grp_prob = jax.nn.softmax(grp_logits, axis=-1)
    grp = jnp.argmax(grp_logits, axis=-1)
    p_grp = jnp.take_along_axis(grp_prob, grp[:, None], axis=-1)[:, 0]
    exp_logits = ((hf @ wr_exp).astype(jnp.float32) + br_exp.astype(jnp.float32)).reshape(N, N_GROUPS, EXPERTS_PER_GROUP)
    in_grp = jnp.take_along_axis(exp_logits, grp[:, None, None], axis=1)[:, 0]
    top_p, top_i = lax.top_k(jax.nn.softmax(in_grp, axis=-1), TOP_K_IN_GROUP)
    weights = top_p / jnp.sum(top_p, axis=-1, keepdims=True) * p_grp[:, None]
    expert = grp[:, None] * EXPERTS_PER_GROUP + top_i

    M = N * TOP_K_IN_GROUP
    Rb = MOE_ROW_BLOCK
    e_flat = expert.reshape(M)
    w_flat = weights.reshape(M)
    tok_flat = jnp.repeat(jnp.arange(N, dtype=jnp.int32), TOP_K_IN_GROUP)
    order = jnp.argsort(e_flat)
    e_sorted = e_flat[order]
    counts = jnp.zeros((N_EXPERTS,), jnp.int32).at[e_flat].add(1)
    padded = (counts + Rb - 1) // Rb * Rb
    pad_end = jnp.cumsum(padded)
    pad_start = pad_end - padded
    start = jnp.cumsum(counts) - counts
    dest = pad_start[e_sorted] + (jnp.arange(M, dtype=jnp.int32) - start[e_sorted])
    n_blocks = (M + N_EXPERTS * (Rb - 1) + Rb - 1) // Rb
    P = n_blocks * Rb
    row_tok = jnp.full((P,), N, jnp.int32).at[dest].set(tok_flat[order])
    row_w = jnp.zeros((P,), jnp.float32).at[dest].set(w_flat[order])
    blk_expert = jnp.minimum(jnp.searchsorted(pad_end, jnp.arange(n_blocks, dtype=jnp.int32) * Rb, side='right'),
                             N_EXPERTS - 1)
    h_pad = jnp.concatenate([hf, jnp.zeros((1, D), hf.dtype)], axis=0)

    def expert_block(args):
        tok, e, wt = args
        xb = h_pad[tok]
        y = (jax.nn.silu(xb @ w_gate[e]) * (xb @ w_up[e])) @ w_down[e]
        return y * wt[:, None].astype(y.dtype)

    y = lax.map(expert_block, (row_tok.reshape(n_blocks, Rb), blk_expert, row_w.reshape(n_blocks, Rb)))
    out = jnp.zeros((N + 1, D), y.dtype).at[row_tok].add(y.reshape(P, D))[:N]
    return out.reshape(B, T, D).astype(h.dtype)


def setup_inputs(seed: int = 0) -> dict:
    key = jax.random.key(seed)
    ks = jax.random.split(key, 24)
    f32 = jnp.float32
    nrm = lambda k, shape, scale: jax.random.normal(k, shape, f32) * scale
    L = DEPTH
    x = nrm(ks[0], (BATCH, SEQ, D_MODEL), 1.0)
    positions = jnp.broadcast_to(jnp.arange(SEQ, dtype=jnp.int32), (BATCH, SEQ))
    attn_norm_w = 1.0 + nrm(ks[1], (L, D_MODEL), 0.02)
    w_in = nrm(ks[2], (L, D_MODEL, D_IN_PROJ), D_MODEL ** -0.5)
    cmp_wk = nrm(ks[3], (L, CMP_LEN, HEAD_DIM, HEAD_DIM), (CMP_LEN * HEAD_DIM) ** -0.5)
    cmp_pek = nrm(ks[4], (L, CMP_LEN, HEAD_DIM), 0.1)
    cmp_wv = nrm(ks[5], (L, CMP_LEN, HEAD_DIM, HEAD_DIM), (CMP_LEN * HEAD_DIM) ** -0.5)
    cmp_pev = nrm(ks[6], (L, CMP_LEN, HEAD_DIM), 0.1)
    gdn_conv_w = nrm(ks[7], (L, CONV_WIDTH, GDN_QKV_COLS), CONV_WIDTH ** -0.5)
    gdn_a_log = jnp.log(jax.random.uniform(ks[8], (L, N_HEADS_GDN), f32, 1.0, 16.0))
    dt = jnp.exp(jax.random.uniform(ks[9], (L, N_HEADS_GDN), f32, math.log(1e-3), math.log(1e-1)))
    gdn_dt_bias = dt + jnp.log(-jnp.expm1(-dt))
    gdn_norm_w = 1.0 + nrm(ks[10], (L, HEAD_DIM), 0.02)
    w_out = nrm(ks[11], (L, MIX_WIDTH, D_MODEL), MIX_WIDTH ** -0.5)
    ffn_norm_w = 1.0 + nrm(ks[12], (L, D_MODEL), 0.02)
    router_group_w = nrm(ks[13], (L, D_MODEL, N_GROUPS), D_MODEL ** -0.5)
    router_group_b = nrm(ks[14], (L, N_GROUPS), 0.01)
    router_expert_w = nrm(ks[15], (L, D_MODEL, N_EXPERTS), D_MODEL ** -0.5)
    router_expert_b = nrm(ks[16], (L, N_EXPERTS), 0.01)
    moe_w_gate = nrm(ks[17], (L, N_EXPERTS, D_MODEL, D_EXPERT), D_MODEL ** -0.5)
    moe_w_up = nrm(ks[18], (L, N_EXPERTS, D_MODEL, D_EXPERT), D_MODEL ** -0.5)
    moe_w_down = nrm(ks[19], (L, N_EXPERTS, D_EXPERT, D_MODEL), D_EXPERT ** -0.5)
    final_norm_w = 1.0 + nrm(ks[20], (D_MODEL,), 0.02)
    return {"x": x, "positions": positions, "attn_norm_w": attn_norm_w, "w_in": w_in,
            "cmp_wk": cmp_wk, "cmp_pek": cmp_pek, "cmp_wv": cmp_wv, "cmp_pev": cmp_pev,
            "gdn_conv_w": gdn_conv_w, "gdn_a_log": gdn_a_log, "gdn_dt_bias": gdn_dt_bias,
            "gdn_norm_w": gdn_norm_w, "w_out": w_out, "ffn_norm_w": ffn_norm_w,
            "router_group_w": router_group_w, "router_group_b": router_group_b,
            "router_expert_w": router_expert_w, "router_expert_b": router_expert_b,
            "moe_w_gate": moe_w_gate, "moe_w_up": moe_w_up, "moe_w_down": moe_w_down,
            "final_norm_w": final_norm_w}


def reference(x, positions, attn_norm_w, w_in, cmp_wk, cmp_pek, cmp_wv, cmp_pev,
              gdn_conv_w, gdn_a_log, gdn_dt_bias, gdn_norm_w, w_out, ffn_norm_w,
              router_group_w, router_group_b, router_expert_w, router_expert_b,
              moe_w_gate, moe_w_up, moe_w_down, final_norm_w):
    cos, sin = rope_tables(positions)
    offsets = []
    acc = 0
    for n in IN_SPLITS[:-1]:
        acc += n
        offsets.append(acc)
    h = x
    for l in range(DEPTH):
        hn = rmsnorm(h, attn_norm_w[l])
        proj = hn @ w_in[l]
        nsa_q, nsa_kv, nsa_gate, gdn_qkv, gdn_b, gdn_a, gdn_z = jnp.split(proj, offsets, axis=-1)
        o_a = nsa_mixer(nsa_q, nsa_kv, nsa_gate, cos, sin, cmp_wk[l], cmp_pek[l], cmp_wv[l], cmp_pev[l])
        o_b = gdn_mixer(gdn_qkv, gdn_b, gdn_a, gdn_z, gdn_conv_w[l], gdn_a_log[l], gdn_dt_bias[l], gdn_norm_w[l])
        h = h + jnp.concatenate([o_a, o_b], axis=-1) @ w_out[l]
        h = h + hier_moe(rmsnorm(h, ffn_norm_w[l]), router_group_w[l], router_group_b[l],
                         router_expert_w[l], router_expert_b[l],
                         moe_w_gate[l], moe_w_up[l], moe_w_down[l])
    return rmsnorm(h, final_norm_w)
```

```python
import functools

import numpy as np
import jax
import jax.numpy as jnp
from jax import lax
from jax.experimental import pallas as pl
from jax.experimental.pallas import tpu as pltpu

F32 = jnp.float32
BF16 = jnp.bfloat16

HEAD_DIM = 128
N_HEADS_NSA = 8
N_KV_NSA = 2
N_REP = N_HEADS_NSA // N_KV_NSA
N_HEADS_GDN = 8
ROT_DIM = 32
ROPE_THETA = 500000.0
CMP_LEN = 32
CMP_STRIDE = 16
SLC_LEN = 64
SLC_TOP = 16
WINDOW = 512
CONV_WIDTH = 4
GDN_CHUNK = 64
N_GROUPS = 8
EXPERTS_PER_GROUP = 8
N_EXPERTS = 64
MOE_ROW_BLOCK = 256
EPS = 1e-6
LANE = 128

CB_Q = 0
CB_KV = 8
CB_GQKV = 20
CB_Z = 44
CB_SMALL = 52
N_CB = 54
SMALL_BETA = 24
SMALL_DECAY = 32

NEG_BIAS = -32768.0
VMEM_LIMIT = 56 * 1024 * 1024


def _cparams(sem):
    return pltpu.CompilerParams(dimension_semantics=sem, vmem_limit_bytes=VMEM_LIMIT)


def _mm(a, b):
    return jnp.dot(a.astype(BF16), b.astype(BF16), preferred_element_type=F32)


def _mm_nt(a, b):
    return lax.dot_general(a.astype(BF16), b.astype(BF16), (((1,), (1,)), ((), ())),
                           preferred_element_type=F32)


def _rope(x, c, sa, sb):
    return (x * c + pltpu.roll(x, LANE - ROT_DIM // 2, 1) * sa
            + pltpu.roll(x, ROT_DIM // 2, 1) * sb)


def _sigmoid(x):
    return 1.0 / (1.0 + jnp.exp(-x))


def _in_proj_kernel(x_ref, nw_ref, w_ref, o_ref, hn_ref):
    @pl.when(pl.program_id(1) == 0)
    def _():
        x = x_ref[...]
        ms = jnp.mean(x * x, axis=-1, keepdims=True)
        hn_ref[...] = (x * lax.rsqrt(ms + EPS) * nw_ref[...]).astype(BF16)
    o_ref[...] = jnp.dot(hn_ref[...], w_ref[...], preferred_element_type=F32)


def _in_proj(x2, norm_w, w):
    n, d = x2.shape
    ncol = w.shape[1]
    tm = min(512, n)
    tn = 768
    return pl.pallas_call(
        _in_proj_kernel,
        grid=(n // tm, ncol // tn),
        in_specs=[pl.BlockSpec((tm, d), lambda i, j: (i, 0)),
                  pl.BlockSpec((1, d), lambda i, j: (0, 0)),
                  pl.BlockSpec((d, tn), lambda i, j: (0, j))],
        out_specs=pl.BlockSpec((tm, tn), lambda i, j: (i, j)),
        out_shape=jax.ShapeDtypeStruct((n, ncol), F32),
        scratch_shapes=[pltpu.VMEM((tm, d), BF16)],
        compiler_params=_cparams(("parallel", "arbitrary")),
        name="in_proj",
    )(x2, norm_w.reshape(1, d), w)


def _compress_kernel(a_ref, c_ref, sa_ref, sb_ref, w_ref, pe_ref, o_ref, xs_ref):
    kv = pl.program_id(1)
    t = a_ref.shape[1]
    nc = t // CMP_STRIDE
    x = a_ref[0]
    xr = _rope(x, c_ref[0], sa_ref[0], sb_ref[0])
    x = jnp.where(kv == 0, xr, x)
    xs_ref[pl.ds(0, t), :] = x
    xs_ref[pl.ds(t, CMP_STRIDE), :] = jnp.zeros((CMP_STRIDE, LANE), F32)
    acc = jnp.zeros((nc, LANE), F32)
    for l in range(CMP_LEN):
        rows = xs_ref[pl.ds(l, nc, stride=CMP_STRIDE), :] + pe_ref[0, pl.ds(l, 1), :]
        acc = acc + _mm(rows, w_ref[0, l])
    o_ref[0, 0, 0] = acc.astype(BF16)


def _nsa_compress(proj3, tabs, cmp_w, cmp_pe):
    b, t, _ = proj3.shape
    g = N_KV_NSA
    nc = t // CMP_STRIDE
    tab_spec = pl.BlockSpec((1, t, LANE), lambda bi, kv, gi: (bi, 0, 0))
    return pl.pallas_call(
        _compress_kernel,
        grid=(b, 2, g),
        in_specs=[pl.BlockSpec((1, t, LANE), lambda bi, kv, gi: (bi, 0, CB_KV + kv * g + gi)),
                  tab_spec, tab_spec, tab_spec,
                  pl.BlockSpec((1, CMP_LEN, LANE, LANE), lambda bi, kv, gi: (kv, 0, 0, 0)),
                  pl.BlockSpec((1, CMP_LEN, LANE), lambda bi, kv, gi: (kv, 0, 0))],
        out_specs=pl.BlockSpec((1, 1, 1, nc, LANE), lambda bi, kv, gi: (bi, kv, gi, 0, 0)),
        out_shape=jax.ShapeDtypeStruct((b, 2, g, nc, LANE), BF16),
        scratch_shapes=[pltpu.VMEM((t + CMP_STRIDE, LANE), F32)],
        compiler_params=_cparams(("parallel", "arbitrary", "arbitrary")),
        name="nsa_compress",
    )(proj3, *tabs, cmp_w, cmp_pe)


def _load_q(q_ref, c, sa, sb):
    scale = HEAD_DIM ** -0.5
    qs = [(_rope(q_ref[0, :, r * LANE:(r + 1) * LANE], c, sa, sb) * scale).astype(BF16)
          for r in range(N_REP)]
    return jnp.concatenate(qs, axis=0)


def _store_gated(o_ref, o, gate_ref, g, branch, tq):
    gt = gate_ref[0]
    for r in range(N_REP):
        col = ((g * N_REP + r) * 3 + branch)
        lane = lax.broadcasted_iota(jnp.int32, gt.shape, 1)
        gcol = jnp.sum(jnp.where(lane == col, gt, 0.0), axis=1, keepdims=True)
        o_ref[0, :, r * LANE:(r + 1) * LANE] = (o[r * tq:(r + 1) * tq] * _sigmoid(gcol)).astype(o_ref.dtype)


def _cmp_attn_kernel(q_ref, c_ref, sa_ref, sb_ref, kc_ref, vc_ref, gate_ref, o_ref, sel_ref, *, n_slc):
    g = pl.program_id(1)
    i = pl.program_id(2)
    tq = q_ref.shape[1]
    nc = kc_ref.shape[3]
    q4 = _load_q(q_ref, c_ref[0], sa_ref[0], sb_ref[0])
    s = _mm_nt(q4, kc_ref[0, 0, 0])
    row = lax.broadcasted_iota(jnp.int32, s.shape, 0)
    n = lax.broadcasted_iota(jnp.int32, s.shape, 1)
    tpos = i * tq + (row & (tq - 1))
    mask = (n * CMP_STRIDE + (CMP_LEN - 1) <= tpos) & (n < nc - 1)
    sm = jnp.where(mask, s, -1e30)
    m = jnp.max(sm, axis=1, keepdims=True)
    p = jnp.where(mask, jnp.exp(sm - m), 0.0)
    l = jnp.sum(p, axis=1, keepdims=True)
    p = p / jnp.maximum(l, 1e-30)
    o = _mm(p, vc_ref[0, 0, 0])
    _store_gated(o_ref, o, gate_ref, g, 0, tq)

    ps = p[0:tq]
    for r in range(1, N_REP):
        ps = ps + p[r * tq:(r + 1) * tq]
    cn = lax.broadcasted_iota(jnp.int32, (nc, LANE), 0)
    cj = lax.broadcasted_iota(jnp.int32, (nc, LANE), 1)
    ratio = SLC_LEN // CMP_STRIDE
    agg = ((cn >= ratio * cj - (CMP_LEN // CMP_STRIDE - 1)) & (cn < ratio * cj + ratio)
           & (cn < nc - 1) & (cj < n_slc))
    agg = jnp.where(agg, 1.0, 0.0).astype(BF16)
    ps_hi = ps.astype(BF16)
    ps_lo = (ps - ps_hi.astype(F32)).astype(BF16)
    imp = (jnp.dot(ps_hi, agg, preferred_element_type=F32)
           + jnp.dot(ps_lo, agg, preferred_element_type=F32))

    j = lax.broadcasted_iota(jnp.int32, (tq, LANE), 1)
    tt = i * tq + lax.broadcasted_iota(jnp.int32, (tq, LANE), 0)
    cur = tt // SLC_LEN
    valid = j <= cur
    forced = (j == 0) | (j == cur) | (j == cur - 1)
    vals = jnp.where(forced, 1e30, jnp.where(valid, imp, -1.0))
    sel = jnp.zeros((tq, LANE), F32)
    for _ in range(min(SLC_TOP, n_slc)):
        mx = jnp.max(vals, axis=1, keepdims=True)
        idx = jnp.min(jnp.where(vals == mx, j, LANE), axis=1, keepdims=True)
        pick = j == idx
        sel = jnp.where(pick, 1.0, sel)
        vals = jnp.where(pick, -1.0, vals)
    sel_ref[0, 0] = jnp.where(sel > 0.5, 0.0, NEG_BIAS).astype(BF16)


def _nsa_cmp_attn(proj3, tabs, kvc, tq=128):
    b, t, _ = proj3.shape
    g = N_KV_NSA
    nc = t // CMP_STRIDE
    n_slc = t // SLC_LEN
    rw = N_REP * LANE
    tab_spec = pl.BlockSpec((1, tq, LANE), lambda bi, gi, i: (bi, i, 0))
    return pl.pallas_call(
        functools.partial(_cmp_attn_kernel, n_slc=n_slc),
        grid=(b, g, t // tq),
        in_specs=[pl.BlockSpec((1, tq, rw), lambda bi, gi, i: (bi, i, gi)),
                  tab_spec, tab_spec, tab_spec,
                  pl.BlockSpec((1, 1, 1, nc, LANE), lambda bi, gi, i: (bi, 0, gi, 0, 0)),
                  pl.BlockSpec((1, 1, 1, nc, LANE), lambda bi, gi, i: (bi, 1, gi, 0, 0)),
                  pl.BlockSpec((1, tq, LANE), lambda bi, gi, i: (bi, i, CB_SMALL))],
        out_specs=[pl.BlockSpec((1, tq, rw), lambda bi, gi, i: (bi, i, gi)),
                   pl.BlockSpec((1, 1, tq, LANE), lambda bi, gi, i: (bi, gi, i, 0))],
        out_shape=[jax.ShapeDtypeStruct((b, t, N_HEADS_NSA * LANE), BF16),
                   jax.ShapeDtypeStruct((b, g, t, LANE), BF16)],
        compiler_params=_cparams(("parallel", "parallel", "parallel")),
        name="nsa_cmp",
    )(proj3, *tabs, kvc, kvc, proj3)


def _sel_attn_kernel(q_ref, c_ref, sa_ref, sb_ref, cf_ref, saf_ref, sbf_ref, k_ref, v_ref,
                     sel_ref, gate_ref, o_ref, ka_ref, vs_ref, m_ref, l_ref, acc_ref, *, tk):
    g = pl.program_id(1)
    i = pl.program_id(2)
    tq = q_ref.shape[1]
    t = k_ref.shape[1]
    rows = N_REP * tq

    @pl.when(i == 0)
    def _():
        kr = _rope(k_ref[0], cf_ref[0], saf_ref[0], sbf_ref[0])
        ka_ref[:, 0:LANE] = kr.astype(BF16)
        pos = lax.broadcasted_iota(jnp.int32, (t, LANE), 0)
        lane = lax.broadcasted_iota(jnp.int32, (t, LANE), 1)
        ka_ref[:, LANE:2 * LANE] = jnp.where(lane == pos // SLC_LEN, 1.0, 0.0).astype(BF16)
        vs_ref[...] = v_ref[0].astype(BF16)

    q4 = _load_q(q_ref, c_ref[0], sa_ref[0], sb_ref[0])
    bias = sel_ref[0, 0]
    qa = jnp.concatenate([q4, jnp.concatenate([bias] * N_REP, axis=0)], axis=1)

    m_ref[...] = jnp.full((rows, 1), -1e30, F32)
    l_ref[...] = jnp.zeros((rows, 1), F32)
    acc_ref[...] = jnp.zeros((rows, LANE), F32)

    def step(kt, causal):
        k0 = pl.multiple_of(kt * tk, tk)
        s = _mm_nt(qa, ka_ref[pl.ds(k0, tk), :])
        if causal:
            r = lax.broadcasted_iota(jnp.int32, s.shape, 0)
            kp = k0 + lax.broadcasted_iota(jnp.int32, s.shape, 1)
            s = jnp.where(kp <= i * tq + (r & (tq - 1)), s, -1e30)
        m_old = m_ref[...]
        m_new = jnp.maximum(m_old, jnp.max(s, axis=1, keepdims=True))
        alpha = jnp.exp(m_old - m_new)
        p = jnp.exp(s - m_new)
        l_ref[...] = alpha * l_ref[...] + jnp.sum(p, axis=1, keepdims=True)
        acc_ref[...] = alpha * acc_ref[...] + _mm(p, vs_ref[pl.ds(k0, tk), :])
        m_ref[...] = m_new

    diag = (i * tq) // tk

    def body(kt, carry):
        step(kt, False)
        return carry

    lax.fori_loop(0, diag, body, 0)
    step(diag, True)
    o = acc_ref[...] / l_ref[...]
    _store_gated(o_ref, o, gate_ref, g, 1, tq)


def _nsa_sel_attn(proj3, tabs, selbias, tq=128, tk=512):
    b, t, _ = proj3.shape
    g = N_KV_NSA
    rw = N_REP * LANE
    tk = min(tk, t)
    tab_spec = pl.BlockSpec((1, tq, LANE), lambda bi, gi, i: (bi, i, 0))
    tabf_spec = pl.BlockSpec((1, t, LANE), lambda bi, gi, i: (bi, 0, 0))
    kcb = CB_KV + 1 * 2 * g
    return pl.pallas_call(
        functools.partial(_sel_attn_kernel, tk=tk),
        grid=(b, g, t // tq),
        in_specs=[pl.BlockSpec((1, tq, rw), lambda bi, gi, i: (bi, i, gi)),
                  tab_spec, tab_spec, tab_spec, tabf_spec, tabf_spec, tabf_spec,
                  pl.BlockSpec((1, t, LANE), lambda bi, gi, i: (bi, 0, kcb + gi)),
                  pl.BlockSpec((1, t, LANE), lambda bi, gi, i: (bi, 0, kcb + g + gi)),
                  pl.BlockSpec((1, 1, tq, LANE), lambda bi, gi, i: (bi, gi, i, 0)),
                  pl.BlockSpec((1, tq, LANE), lambda bi, gi, i: (bi, i, CB_SMALL))],
        out_specs=pl.BlockSpec((1, tq, rw), lambda bi, gi, i: (bi, i, gi)),
        out_shape=jax.ShapeDtypeStruct((b, t, N_HEADS_NSA * LANE), BF16),
        scratch_shapes=[pltpu.VMEM((t, 2 * LANE), BF16), pltpu.VMEM((t, LANE), BF16),
                        pltpu.VMEM((N_REP * tq, 1), F32), pltpu.VMEM((N_REP * tq, 1), F32),
                        pltpu.VMEM((N_REP * tq, LANE), F32)],
        compiler_params=_cparams(("parallel", "parallel", "arbitrary")),
        name="nsa_sel",
    )(proj3, *tabs, *tabs, proj3, proj3, selbias, proj3)


def _win_attn_kernel(q_ref, c_ref, sa_ref, sb_ref, cf_ref, saf_ref, sbf_ref, k_ref, v_ref,
                     gate_ref, o_ref, ks_ref, vs_ref, *, span):
    g = pl.program_id(1)
    i = pl.program_id(2)
    tq = q_ref.shape[1]

    @pl.when(i == 0)
    def _():
        ks_ref[...] = _rope(k_ref[0], cf_ref[0], saf_ref[0], sbf_ref[0]).astype(BF16)
        vs_ref[...] = v_ref[0].astype(BF16)

    q4 = _load_q(q_ref, c_ref[0], sa_ref[0], sb_ref[0])
    k0 = pl.multiple_of(jnp.maximum(i * tq - WINDOW, 0), tq)
    s = _mm_nt(q4, ks_ref[pl.ds(k0, span), :])
    r = lax.broadcasted_iota(jnp.int32, s.shape, 0)
    kp = k0 + lax.broadcasted_iota(jnp.int32, s.shape, 1)
    dist = i * tq + (r & (tq - 1)) - kp
    mask = (dist >= 0) & (dist < WINDOW)
    sm = jnp.where(mask, s, -1e30)
    m = jnp.max(sm, axis=1, keepdims=True)
    p = jnp.where(mask, jnp.exp(sm - m), 0.0)
    l = jnp.sum(p, axis=1, keepdims=True)
    o = _mm(p, vs_ref[pl.ds(k0, span), :]) / jnp.maximum(l, 1e-30)
    _store_gated(o_ref, o, gate_ref, g, 2, tq)


def _nsa_win_attn(proj3, tabs, tq=128):
    b, t, _ = proj3.shape
    g = N_KV_NSA
    rw = N_REP * LANE
    span = min(WINDOW + tq, t)
    tab_spec = pl.BlockSpec((1, tq, LANE), lambda bi, gi, i: (bi, i, 0))
    tabf_spec = pl.BlockSpec((1, t, LANE), lambda bi, gi, i: (bi, 0, 0))
    kcb = CB_KV + 2 * 2 * g
    return pl.pallas_call(
        functools.partial(_win_attn_kernel, span=span),
        grid=(b, g, t // tq),
        in_specs=[pl.BlockSpec((1, tq, rw), lambda bi, gi, i: (bi, i, gi)),
                  tab_spec, tab_spec, tab_spec, tabf_spec, tabf_spec, tabf_spec,
                  pl.BlockSpec((1, t, LANE), lambda bi, gi, i: (bi, 0, kcb + gi)),
                  pl.BlockSpec((1, t, LANE), lambda bi, gi, i: (bi, 0, kcb + g + gi)),
                  pl.BlockSpec((1, tq, LANE), lambda bi, gi, i: (bi, i, CB_SMALL))],
        out_specs=pl.BlockSpec((1, tq, rw), lambda bi, gi, i: (bi, i, gi)),
        out_shape=jax.ShapeDtypeStruct((b, t, N_HEADS_NSA * LANE), BF16),
        scratch_shapes=[pltpu.VMEM((t, LANE), BF16), pltpu.VMEM((t, LANE), BF16)],
        compiler_params=_cparams(("parallel", "parallel", "arbitrary")),
        name="nsa_win",
    )(proj3, *tabs, *tabs, proj3, proj3, proj3)


def _gdn_kernel(q_ref, k_ref, v_ref, z_ref, small_ref, cwq_ref, cwk_ref, cwv_ref, alog_ref, dtb_ref,
                nw_ref, o_ref, q_s, k_s, kb_s, vb_s, gc_s, at_s, gl_s, kdt_s):
    h = pl.program_id(1)
    t = q_ref.shape[1]
    c = GDN_CHUNK
    nchunk = t // c
    rowi = lax.broadcasted_iota(jnp.int32, (t, LANE), 0)
    lanei = lax.broadcasted_iota(jnp.int32, (t, LANE), 1)

    def conv_silu(x, cw):
        y = x * cw[CONV_WIDTH - 1:CONV_WIDTH]
        for sft in range(1, CONV_WIDTH):
            xs = jnp.where(rowi >= sft, pltpu.roll(x, sft, 0), 0.0)
            y = y + xs * cw[CONV_WIDTH - 1 - sft:CONV_WIDTH - sft]
        return y * _sigmoid(y)

    def l2n(x):
        return x * lax.rsqrt(jnp.sum(x * x, axis=1, keepdims=True) + EPS)

    q = l2n(conv_silu(q_ref[0], cwq_ref[...])) * (HEAD_DIM ** -0.5)
    k = l2n(conv_silu(k_ref[0], cwk_ref[...]))
    v = conv_silu(v_ref[0], cwv_ref[...])

    small = small_ref[0]
    b_raw = jnp.sum(jnp.where(lanei == SMALL_BETA + h, small, 0.0), axis=1, keepdims=True)
    a_raw = jnp.sum(jnp.where(lanei == SMALL_DECAY + h, small, 0.0), axis=1, keepdims=True)
    beta = jnp.broadcast_to(_sigmoid(b_raw), (t, LANE))
    xa = jnp.broadcast_to(a_raw, (t, LANE)) + dtb_ref[0]
    softplus = jnp.maximum(xa, 0.0) + jnp.log(1.0 + jnp.exp(-jnp.abs(xa)))
    gstep = -jnp.exp(alog_ref[0]) * softplus
    gc = gstep
    sft = 1
    while sft < c:
        gc = gc + jnp.where((rowi & (c - 1)) >= sft, pltpu.roll(gc, sft, 0), 0.0)
        sft *= 2
    q_s[...] = q
    k_s[...] = k
    kb_s[...] = k * beta
    vb_s[...] = v * beta
    gc_s[...] = gc

    ci = lax.broadcasted_iota(jnp.int32, (c, c), 0)
    cj = lax.broadcasted_iota(jnp.int32, (c, c), 1)
    tril = cj <= ci
    strict = cj < ci
    eye = jnp.where(ci == cj, 1.0, 0.0)
    c2 = 2 * c

    def prep(n2, carry):
        r2 = pl.multiple_of(n2 * c2, c2)
        gc2 = gc_s[pl.ds(r2, c2), :]
        gct = gc2.T
        kd2 = []
        for hlf in range(2):
            r0 = r2 + hlf * c
            lo = hlf * c
            qn = q_s[pl.ds(r0, c), :]
            kn = k_s[pl.ds(r0, c), :]
            kbn = kb_s[pl.ds(r0, c), :]
            vbn = vb_s[pl.ds(r0, c), :]
            gcn = gc2[lo:lo + c]
            diff = gcn[:, 0:c] - gct[0:1, lo:lo + c]
            decay = jnp.exp(jnp.where(tril, diff, -1e30))
            low = jnp.where(strict, _mm_nt(kbn, kn) * decay, 0.0)
            y = -low
            pm = eye + y
            pw = 1
            while 2 * pw < c:
                y = _mm(y, y)
                pm = pm + _mm(pm, y)
                pw *= 2
            egn = jnp.exp(gcn)
            vb_s[pl.ds(r0, c), :] = _mm(pm, vbn)
            kb_s[pl.ds(r0, c), :] = _mm(pm, kbn * egn)
            at_s[pl.ds(r0, c), :] = jnp.where(tril, _mm_nt(qn, kn) * decay, 0.0)
            q_s[pl.ds(r0, c), :] = qn * egn
            gl = gcn[c - 1:c, :]
            gl_s[pl.ds(2 * n2 + hlf, 1), :] = jnp.exp(gl)
            kd2.append(kn * jnp.exp(gl - gcn))
        kdt_s[n2] = jnp.concatenate(kd2, axis=0).T
        return carry

    lax.fori_loop(0, nchunk // 2, prep, 0)

    def scan(n2, s):
        r2 = pl.multiple_of(n2 * c2, c2)
        kdt = kdt_s[n2]
        for hlf in range(2):
            r0 = r2 + hlf * c
            lo = hlf * c
            vnew = vb_s[pl.ds(r0, c), :] - _mm(kb_s[pl.ds(r0, c), :], s)
            o = _mm(q_s[pl.ds(r0, c), :], s) + _mm(at_s[pl.ds(r0, c), :], vnew)
            s = s * gl_s[pl.ds(2 * n2 + hlf, 1), :] + _mm(kdt[:, lo:lo + c], vnew)
            vb_s[pl.ds(r0, c), :] = o
        return s

    lax.fori_loop(0, nchunk // 2, scan, jnp.zeros((LANE, LANE), F32))

    o = vb_s[...]
    on = o * lax.rsqrt(jnp.mean(o * o, axis=1, keepdims=True) + EPS) * nw_ref[...]
    z = z_ref[0]
    o_ref[0] = (on * (z * _sigmoid(z))).astype(o_ref.dtype)


def _gdn(proj3, conv_w, a_log, dt_bias, norm_w):
    b, t, _ = proj3.shape
    hh = N_HEADS_GDN
    c = GDN_CHUNK
    col = lambda cb: pl.BlockSpec((1, t, LANE), lambda bi, hi: (bi, 0, cb + hi))
    cw = lambda off: pl.BlockSpec((CONV_WIDTH, LANE), lambda bi, hi: (0, off + hi))
    hrow = pl.BlockSpec((1, 1, LANE), lambda bi, hi: (hi, 0, 0))
    alog_b = jnp.broadcast_to(a_log.astype(F32)[:, None, None], (hh, 1, LANE))
    dtb_b = jnp.broadcast_to(dt_bias.astype(F32)[:, None, None], (hh, 1, LANE))
    big = pltpu.VMEM((t, LANE), F32)
    return pl.pallas_call(
        _gdn_kernel,
        grid=(b, hh),
        in_specs=[col(CB_GQKV), col(CB_GQKV + hh), col(CB_GQKV + 2 * hh), col(CB_Z),
                  pl.BlockSpec((1, t, LANE), lambda bi, hi: (bi, 0, CB_SMALL)),
                  cw(0), cw(hh), cw(2 * hh), hrow, hrow,
                  pl.BlockSpec((1, LANE), lambda bi, hi: (0, 0))],
        out_specs=pl.BlockSpec((1, t, LANE), lambda bi, hi: (bi, 0, hi)),
        out_shape=jax.ShapeDtypeStruct((b, t, hh * LANE), BF16),
        scratch_shapes=[big, big, big, big, big,
                        pltpu.VMEM((t, c), F32),
                        pltpu.VMEM((t // c, LANE), F32),
                        pltpu.VMEM((t // (2 * c), LANE, 2 * c), F32)],
        compiler_params=_cparams(("parallel", "parallel")),
        name="gdn",
    )(proj3, proj3, proj3, proj3, proj3, conv_w, conv_w, conv_w, alog_b, dtb_b,
      norm_w.reshape(1, LANE))


def _split3(a):
    hi = a.astype(BF16)
    lo = (a - hi.astype(F32)).astype(BF16)
    return hi, lo


def _out_proj_kernel(oc_ref, os_ref, ow_ref, ob_ref, x_ref, wo_ref, fw_ref, wr_ref, br_ref,
                     h_ref, hn_ref, ids_ref, wts_ref):
    half = oc_ref.shape[1]
    oa = (oc_ref[...].astype(F32) + os_ref[...].astype(F32) + ow_ref[...].astype(F32)).astype(BF16)
    h1 = (x_ref[...] + jnp.dot(oa, wo_ref[0:half, :], preferred_element_type=F32)
          + jnp.dot(ob_ref[...], wo_ref[half:2 * half, :], preferred_element_type=F32))
    h_ref[...] = h1
    hn = h1 * lax.rsqrt(jnp.mean(h1 * h1, axis=1, keepdims=True) + EPS) * fw_ref[...]
    hn_ref[...] = hn

    a_hi, a_lo = _split3(hn)
    w_hi, w_lo = _split3(wr_ref[...])
    dot = lambda a, b: jnp.dot(a, b, preferred_element_type=F32)
    logits = dot(a_hi, w_hi) + dot(a_hi, w_lo) + dot(a_lo, w_hi) + br_ref[...]
    lane = lax.broadcasted_iota(jnp.int32, logits.shape, 1)
    big = 1e30
    is_g = lane < N_GROUPS
    lg = jnp.where(is_g, logits, -big)
    gm = jnp.max(lg, axis=1, keepdims=True)
    grp = jnp.min(jnp.where(lg == gm, lane, LANE), axis=1, keepdims=True)
    p_grp = 1.0 / jnp.sum(jnp.where(is_g, jnp.exp(lg - gm), 0.0), axis=1, keepdims=True)
    e_id = lane - N_GROUPS
    in_g = (e_id >= 0) & (e_id < N_EXPERTS) & ((e_id // EXPERTS_PER_GROUP) == grp)
    le = jnp.where(in_g, logits, -big)
    em = jnp.max(le, axis=1, keepdims=True)
    pe = jnp.where(in_g, jnp.exp(le - em), 0.0)
    pe = pe / jnp.sum(pe, axis=1, keepdims=True)
    pm = jnp.where(in_g, pe, -1.0)
    p1 = jnp.max(pm, axis=1, keepdims=True)
    i1 = jnp.min(jnp.where(pm == p1, lane, LANE), axis=1, keepdims=True)
    pm2 = jnp.where(lane == i1, -1.0, pm)
    p2 = jnp.max(pm2, axis=1, keepdims=True)
    i2 = jnp.min(jnp.where(pm2 == p2, lane, LANE), axis=1, keepdims=True)
    den = p1 + p2
    ids_ref[...] = jnp.where(lane == 0, i1 - N_GROUPS, jnp.where(lane == 1, i2 - N_GROUPS, 0))
    wts_ref[...] = jnp.where(lane == 0, p1 / den * p_grp, jnp.where(lane == 1, p2 / den * p_grp, 0.0))


def _out_proj(oc, os_, ow, ob, x2, w_out, ffn_w, wr, br):
    n, d = x2.shape
    half = oc.shape[1]
    tm = min(256, n)
    row = lambda w: pl.BlockSpec((tm, w), lambda i: (i, 0))
    full = lambda a: pl.BlockSpec(a.shape, lambda i: (0,) * a.ndim)
    fw = ffn_w.reshape(1, d)
    return pl.pallas_call(
        _out_proj_kernel,
        grid=(n // tm,),
        in_specs=[row(half), row(half), row(half), row(half), row(d), full(w_out), full(fw),
                  full(wr), full(br)],
        out_specs=[row(d), row(d), row(LANE), row(LANE)],
        out_shape=[jax.ShapeDtypeStruct((n, d), F32), jax.ShapeDtypeStruct((n, d), F32),
                   jax.ShapeDtypeStruct((n, LANE), jnp.int32), jax.ShapeDtypeStruct((n, LANE), F32)],
        compiler_params=_cparams(("parallel",)),
        name="out_proj",
    )(oc, os_, ow, ob, x2, w_out, fw, wr, br)


def _moe_kernel(bexp_ref, bact_ref, rtok_ref, rslot_ref, hn_hbm, roww_ref, wg_ref, wu_ref, wd_ref,
                y_hbm, xbuf, ybuf, wgb, wub, wdb, cached, gsem, ssem):
    i = pl.program_id(0)
    rb = xbuf.shape[0]

    nv = bact_ref[i]

    @pl.when(i == 0)
    def _():
        cached[0] = -1

    @pl.when(nv > 0)
    def _():
        def gather(r, carry):
            tok = rtok_ref[0, 0, r]
            pltpu.make_async_copy(hn_hbm.at[pl.ds(tok, 1), :], xbuf.at[pl.ds(r, 1), :], gsem).start()
            return carry
        lax.fori_loop(0, rb, gather, 0)

        e = bexp_ref[i]

        @pl.when(cached[0] != e)
        def _():
            wgb[...] = wg_ref[0].astype(BF16)
            wub[...] = wu_ref[0].astype(BF16)
            wdb[...] = wd_ref[0].astype(BF16)
            cached[0] = e

        pltpu.make_async_copy(hn_hbm.at[pl.ds(0, rb), :], xbuf, gsem).wait()
        x = xbuf[...].astype(BF16)
        hg = jnp.dot(x, wgb[...], preferred_element_type=F32)
        hu = jnp.dot(x, wub[...], preferred_element_type=F32)
        act = (hg * _sigmoid(hg) * hu).astype(BF16)
        ybuf[...] = jnp.dot(act, wdb[...], preferred_element_type=F32) * roww_ref[...]

        def scatter(r, carry):
            slot = rslot_ref[0, 0, r]
            pltpu.make_async_copy(ybuf.at[pl.ds(r, 1), :], y_hbm.at[pl.ds(slot, 1), :], ssem).start()
            return carry
        lax.fori_loop(0, nv, scatter, 0)

        def drain(r, carry):
            pltpu.make_async_copy(ybuf.at[pl.ds(0, 1), :], y_hbm.at[pl.ds(0, 1), :], ssem).wait()
            return carry
        lax.fori_loop(0, nv, drain, 0)


def _moe(hn, bexp, bact, rtok, rslot, roww, w_gate, w_up, w_down, n_slots):
    n, d = hn.shape
    nb = bexp.shape[0]
    rb = MOE_ROW_BLOCK
    de = w_gate.shape[2]
    smem_rows = pl.BlockSpec((1, 1, rb), lambda i, be, ba: (i, 0, 0), memory_space=pltpu.SMEM)
    grid_spec = pltpu.PrefetchScalarGridSpec(
        num_scalar_prefetch=2,
        grid=(nb,),
        in_specs=[smem_rows, smem_rows,
                  pl.BlockSpec(memory_space=pl.ANY),
                  pl.BlockSpec((rb, 1), lambda i, be, ba: (i, 0)),
                  pl.BlockSpec((1, d, de), lambda i, be, ba: (be[i], 0, 0)),
                  pl.BlockSpec((1, d, de), lambda i, be, ba: (be[i], 0, 0)),
                  pl.BlockSpec((1, de, d), lambda i, be, ba: (be[i], 0, 0))],
        out_specs=pl.BlockSpec(memory_space=pl.ANY),
        scratch_shapes=[pltpu.VMEM((rb, d), F32), pltpu.VMEM((rb, d), F32),
                        pltpu.VMEM((d, de), BF16), pltpu.VMEM((d, de), BF16), pltpu.VMEM((de, d), BF16),
                        pltpu.SMEM((1,), jnp.int32),
                        pltpu.SemaphoreType.DMA(()), pltpu.SemaphoreType.DMA(())])
    return pl.pallas_call(
        _moe_kernel,
        grid_spec=grid_spec,
        out_shape=jax.ShapeDtypeStruct((n_slots, d), F32),
        compiler_params=_cparams(("arbitrary",)),
        name="moe",
    )(bexp, bact, rtok, rslot, hn, roww, w_gate, w_up, w_down)


def _dispatch(ids, wts, n):
    k = 2
    m = n * k
    rb = MOE_ROW_BLOCK
    e_flat = ids[:, :k].reshape(m)
    w_flat = wts[:, :k].reshape(m)
    onehot = (e_flat[:, None] == jnp.arange(N_EXPERTS, dtype=jnp.int32)[None, :]).astype(jnp.int32)
    csum = jnp.cumsum(onehot, axis=0)
    counts = csum[-1]
    rank = jnp.take_along_axis(csum, e_flat[:, None], axis=1)[:, 0] - 1
    padded = (counts + rb - 1) // rb * rb
    pad_end = jnp.cumsum(padded)
    pad_start = pad_end - padded
    dest = pad_start[e_flat] + rank
    nb = (m + N_EXPERTS * (rb - 1) + rb - 1) // rb
    p = nb * rb
    row_m = jnp.full((p,), -1, jnp.int32).at[dest].set(jnp.arange(m, dtype=jnp.int32))
    row_w = jnp.zeros((p,), F32).at[dest].set(w_flat)
    real = row_m >= 0
    rtok = jnp.where(real, row_m // k, 0)
    rslot = jnp.where(real, (row_m % k) * n + row_m // k, 0)
    starts = jnp.arange(nb, dtype=jnp.int32) * rb
    bexp = jnp.minimum(jnp.searchsorted(pad_end, starts, side='right'), N_EXPERTS - 1).astype(jnp.int32)
    bact = jnp.sum(real.reshape(nb, rb).astype(jnp.int32), axis=1)
    last_e = jnp.max(jnp.where(bact > 0, bexp, 0))
    bexp = jnp.where(bact > 0, bexp, last_e)
    return (bexp, bact, rtok.reshape(nb, 1, rb), rslot.reshape(nb, 1, rb), row_w.reshape(p, 1), m)


def _combine_kernel(h_ref, y0_ref, y1_ref, fw_ref, o_ref):
    h = h_ref[...] + (y0_ref[...] + y1_ref[...])
    o_ref[...] = h * lax.rsqrt(jnp.mean(h * h, axis=1, keepdims=True) + EPS) * fw_ref[...]


def _combine(h1, y, final_w):
    n, d = h1.shape
    tm = min(256, n)
    nt = n // tm
    return pl.pallas_call(
        _combine_kernel,
        grid=(nt,),
        in_specs=[pl.BlockSpec((tm, d), lambda i: (i, 0)),
                  pl.BlockSpec((tm, d), lambda i: (i, 0)),
                  pl.BlockSpec((tm, d), lambda i: (i + nt, 0)),
                  pl.BlockSpec((1, d), lambda i: (0, 0))],
        out_specs=pl.BlockSpec((tm, d), lambda i: (i, 0)),
        out_shape=jax.ShapeDtypeStruct((n, d), F32),
        compiler_params=_cparams(("parallel",)),
        name="combine",
    )(h1, y, y, final_w.reshape(1, d))


def _rope_tables(positions):
    half = ROT_DIM // 2
    inv_freq = ROPE_THETA ** (-jnp.arange(0, ROT_DIM, 2, dtype=F32) / ROT_DIM)
    ang = positions.astype(F32)[..., None] * inv_freq
    cos, sin = jnp.cos(ang), jnp.sin(ang)
    b, t = positions.shape
    ones = jnp.ones((b, t, LANE - ROT_DIM), F32)
    zeros = jnp.zeros((b, t, LANE - half), F32)
    c = jnp.concatenate([cos, cos, ones], axis=-1)
    sa = jnp.concatenate([-sin, zeros], axis=-1)
    sb = jnp.concatenate([jnp.zeros((b, t, half), F32), sin, zeros[..., :LANE - ROT_DIM]], axis=-1)
    return c, sa, sb


def _arrange_w_in(w_in):
    d = w_in.shape[0]
    sizes = (N_HEADS_NSA * HEAD_DIM, 3 * 2 * N_KV_NSA * HEAD_DIM, 3 * N_HEADS_NSA,
             3 * N_HEADS_GDN * HEAD_DIM, N_HEADS_GDN, N_HEADS_GDN, N_HEADS_GDN * HEAD_DIM)
    offs = np.cumsum((0,) + sizes)
    seg = [w_in[:, offs[i]:offs[i + 1]] for i in range(len(sizes))]
    q, kv, gate, gqkv, gb, ga, gz = seg
    used = sum(sizes)
    pad = jnp.zeros((d, N_CB * LANE - used), w_in.dtype)
    return jnp.concatenate([q, kv, gqkv, gz, gate, gb, ga, pad], axis=1).astype(BF16)


def kernel(x, positions, attn_norm_w, w_in, cmp_wk, cmp_pek, cmp_wv, cmp_pev, gdn_conv_w, gdn_a_log,
           gdn_dt_bias, gdn_norm_w, w_out, ffn_norm_w, router_group_w, router_group_b,
           router_expert_w, router_expert_b, moe_w_gate, moe_w_up, moe_w_down, final_norm_w):
    b, t, d = x.shape
    n = b * t
    tabs = _rope_tables(positions)
    h = x.reshape(n, d)
    assert w_in.shape[0] == 1, "single-layer block only"
    for l in range(1):
        proj = _in_proj(h, attn_norm_w[l], _arrange_w_in(w_in[l]))
        proj3 = proj.reshape(b, t, N_CB * LANE)
        cmp_w = jnp.stack([cmp_wk[l], cmp_wv[l]])
        cmp_pe = jnp.stack([cmp_pek[l], cmp_pev[l]])
        kvc = _nsa_compress(proj3, tabs, cmp_w, cmp_pe)
        o_c, selbias = _nsa_cmp_attn(proj3, tabs, kvc)
        o_s = _nsa_sel_attn(proj3, tabs, selbias)
        o_w = _nsa_win_attn(proj3, tabs)
        o_b = _gdn(proj3, gdn_conv_w[l], gdn_a_log[l], gdn_dt_bias[l], gdn_norm_w[l])
        half = N_HEADS_NSA * HEAD_DIM
        wr = jnp.concatenate([router_group_w[l], router_expert_w[l],
                              jnp.zeros((d, LANE - N_GROUPS - N_EXPERTS), F32)], axis=1)
        br = jnp.concatenate([router_group_b[l], router_expert_b[l],
                              jnp.zeros((LANE - N_GROUPS - N_EXPERTS,), F32)]).reshape(1, LANE)
        h1, hn2, ids, wts = _out_proj(o_c.reshape(n, half), o_s.reshape(n, half), o_w.reshape(n, half),
                                      o_b.reshape(n, half), h, w_out[l].astype(BF16), ffn_norm_w[l], wr, br)
        bexp, bact, rtok, rslot, roww, n_slots = _dispatch(ids, wts, n)
        y = _moe(hn2, bexp, bact, rtok, rslot, roww, moe_w_gate[l], moe_w_up[l], moe_w_down[l], n_slots)
        out = _combine(h1, y, final_norm_w)
    return out.reshape(b, t, d)
```

```python
import functools

import numpy as np
import jax
import jax.numpy as jnp
from jax import lax
from jax.experimental import pallas as pl
from jax.experimental.pallas import tpu as pltpu

F32 = jnp.float32
BF16 = jnp.bfloat16

HEAD_DIM = 128
N_HEADS_NSA = 8
N_KV_NSA = 2
N_REP = N_HEADS_NSA // N_KV_NSA
N_HEADS_GDN = 8
ROT_DIM = 32
ROPE_THETA = 500000.0
CMP_LEN = 32
CMP_STRIDE = 16
SLC_LEN = 64
SLC_TOP = 16
WINDOW = 512
CONV_WIDTH = 4
GDN_CHUNK = 64
N_GROUPS = 8
EXPERTS_PER_GROUP = 8
N_EXPERTS = 64
MOE_ROW_BLOCK = 256
EPS = 1e-6
LANE = 128

CB_Q = 0
CB_KV = 8
CB_GQKV = 20
CB_Z = 44
CB_SMALL = 52
N_CB = 54
SMALL_BETA = 24
SMALL_DECAY = 32

NEG_BIAS = -32768.0
VMEM_LIMIT = 56 * 1024 * 1024


def _cparams(sem):
    return pltpu.CompilerParams(dimension_semantics=sem, vmem_limit_bytes=VMEM_LIMIT)


def _mm(a, b):
    return jnp.dot(a.astype(BF16), b.astype(BF16), preferred_element_type=F32)


def _mm_nt(a, b):
    return lax.dot_general(a.astype(BF16), b.astype(BF16), (((1,), (1,)), ((), ())),
                           preferred_element_type=F32)


def _rope(x, c, sa, sb):
    return (x * c + pltpu.roll(x, LANE - ROT_DIM // 2, 1) * sa
            + pltpu.roll(x, ROT_DIM // 2, 1) * sb)


def _sigmoid(x):
    return 1.0 / (1.0 + jnp.exp(-x))


def _lane_tiles(x):
    return [x[:, i:i + LANE] for i in range(0, x.shape[1], LANE)]


def _rowmax(x):
    return jnp.max(functools.reduce(jnp.maximum, _lane_tiles(x)), axis=1, keepdims=True)


def _rowsum(x):
    return jnp.sum(functools.reduce(jnp.add, _lane_tiles(x)), axis=1, keepdims=True)


def _in_proj_kernel(x_ref, nw_ref, w_ref, o_ref, hn_ref):
    @pl.when(pl.program_id(1) == 0)
    def _():
        x = x_ref[...]
        ms = jnp.mean(x * x, axis=-1, keepdims=True)
        hn_ref[...] = (x * lax.rsqrt(ms + EPS) * nw_ref[...]).astype(BF16)
    o_ref[...] = jnp.dot(hn_ref[...], w_ref[...], preferred_element_type=F32)


def _in_proj(x2, norm_w, w):
    n, d = x2.shape
    ncol = w.shape[1]
    tm = min(512, n)
    tn = 768
    return pl.pallas_call(
        _in_proj_kernel,
        grid=(n // tm, ncol // tn),
        in_specs=[pl.BlockSpec((tm, d), lambda i, j: (i, 0)),
                  pl.BlockSpec((1, d), lambda i, j: (0, 0)),
                  pl.BlockSpec((d, tn), lambda i, j: (0, j))],
        out_specs=pl.BlockSpec((tm, tn), lambda i, j: (i, j)),
        out_shape=jax.ShapeDtypeStruct((n, ncol), F32),
        scratch_shapes=[pltpu.VMEM((tm, d), BF16)],
        compiler_params=_cparams(("parallel", "arbitrary")),
        name="in_proj",
    )(x2, norm_w.reshape(1, d), w)


def _compress_kernel(a_ref, c_ref, sa_ref, sb_ref, w_ref, pe_ref, o_ref, xs_ref):
    kv = pl.program_id(1)
    t = a_ref.shape[1]
    nc = t // CMP_STRIDE
    x = a_ref[0]
    xr = _rope(x, c_ref[0], sa_ref[0], sb_ref[0])
    x = jnp.where(kv == 0, xr, x)
    xs_ref[pl.ds(0, t), :] = x
    xs_ref[pl.ds(t, CMP_STRIDE), :] = jnp.zeros((CMP_STRIDE, LANE), F32)
    acc = jnp.zeros((nc, LANE), F32)
    for l in range(CMP_LEN):
        rows = xs_ref[pl.ds(l, nc, stride=CMP_STRIDE), :] + pe_ref[0, pl.ds(l, 1), :]
        acc = acc + _mm(rows, w_ref[0, l])
    o_ref[0, 0, 0] = acc.astype(BF16)


def _nsa_compress(proj3, tabs, cmp_w, cmp_pe):
    b, t, _ = proj3.shape
    g = N_KV_NSA
    nc = t // CMP_STRIDE
    tab_spec = pl.BlockSpec((1, t, LANE), lambda bi, kv, gi: (bi, 0, 0))
    return pl.pallas_call(
        _compress_kernel,
        grid=(b, 2, g),
        in_specs=[pl.BlockSpec((1, t, LANE), lambda bi, kv, gi: (bi, 0, CB_KV + kv * g + gi)),
                  tab_spec, tab_spec, tab_spec,
                  pl.BlockSpec((1, CMP_LEN, LANE, LANE), lambda bi, kv, gi: (kv, 0, 0, 0)),
                  pl.BlockSpec((1, CMP_LEN, LANE), lambda bi, kv, gi: (kv, 0, 0))],
        out_specs=pl.BlockSpec((1, 1, 1, nc, LANE), lambda bi, kv, gi: (bi, kv, gi, 0, 0)),
        out_shape=jax.ShapeDtypeStruct((b, 2, g, nc, LANE), BF16),
        scratch_shapes=[pltpu.VMEM((t + CMP_STRIDE, LANE), F32)],
        compiler_params=_cparams(("parallel", "arbitrary", "arbitrary")),
        name="nsa_compress",
    )(proj3, *tabs, cmp_w, cmp_pe)


def _load_q(q_ref, c, sa, sb):
    scale = HEAD_DIM ** -0.5
    qs = [(_rope(q_ref[0, :, r * LANE:(r + 1) * LANE], c, sa, sb) * scale).astype(BF16)
          for r in range(N_REP)]
    return jnp.concatenate(qs, axis=0)


def _store_gated(o_ref, o, gate_ref, g, branch, tq):
    gt = gate_ref[0]
    for r in range(N_REP):
        col = ((g * N_REP + r) * 3 + branch)
        lane = lax.broadcasted_iota(jnp.int32, gt.shape, 1)
        gcol = jnp.sum(jnp.where(lane == col, gt, 0.0), axis=1, keepdims=True)
        o_ref[0, :, r * LANE:(r + 1) * LANE] = (o[r * tq:(r + 1) * tq] * _sigmoid(gcol)).astype(o_ref.dtype)


def _cmp_attn_kernel(q_ref, c_ref, sa_ref, sb_ref, kc_ref, vc_ref, gate_ref, o_ref, sel_ref, *, n_slc):
    g = pl.program_id(1)
    i = pl.program_id(2)
    tq = q_ref.shape[1]
    nc = kc_ref.shape[3]
    q4 = _load_q(q_ref, c_ref[0], sa_ref[0], sb_ref[0])
    s = _mm_nt(q4, kc_ref[0, 0, 0])
    row = lax.broadcasted_iota(jnp.int32, s.shape, 0)
    n = lax.broadcasted_iota(jnp.int32, s.shape, 1)
    tpos = i * tq + (row & (tq - 1))
    mask = (n * CMP_STRIDE + (CMP_LEN - 1) <= tpos) & (n < nc - 1)
    sm = jnp.where(mask, s, -1e30)
    m = _rowmax(sm)
    p = jnp.where(mask, jnp.exp(sm - m), 0.0)
    l = _rowsum(p)
    p = p / jnp.maximum(l, 1e-30)
    o = _mm(p, vc_ref[0, 0, 0])
    _store_gated(o_ref, o, gate_ref, g, 0, tq)

    ps = p[0:tq]
    for r in range(1, N_REP):
        ps = ps + p[r * tq:(r + 1) * tq]
    cn = lax.broadcasted_iota(jnp.int32, (nc, LANE), 0)
    cj = lax.broadcasted_iota(jnp.int32, (nc, LANE), 1)
    ratio = SLC_LEN // CMP_STRIDE
    agg = ((cn >= ratio * cj - (CMP_LEN // CMP_STRIDE - 1)) & (cn < ratio * cj + ratio)
           & (cn < nc - 1) & (cj < n_slc))
    agg = jnp.where(agg, 1.0, 0.0).astype(BF16)
    ps_hi = ps.astype(BF16)
    ps_lo = (ps - ps_hi.astype(F32)).astype(BF16)
    imp = (jnp.dot(ps_hi, agg, preferred_element_type=F32)
           + jnp.dot(ps_lo, agg, preferred_element_type=F32))

    j = lax.broadcasted_iota(jnp.int32, (tq, LANE), 1)
    tt = i * tq + lax.broadcasted_iota(jnp.int32, (tq, LANE), 0)
    cur = tt // SLC_LEN
    valid = j <= cur
    forced = (j == 0) | (j == cur) | (j == cur - 1)
    vals = jnp.where(forced, 1e30, jnp.where(valid, imp, -1.0))
    vt = vals.T
    jb = lax.broadcasted_iota(jnp.int32, (n_slc, tq), 0)
    vb = vt[0:n_slc]
    cnt = jnp.zeros((n_slc, tq), F32)
    for jp in range(n_slc):
        cand = vt[jp:jp + 1, :]
        ge = jnp.where(cand >= vb, 1.0, 0.0)
        gt = jnp.where(cand > vb, 1.0, 0.0)
        cnt = cnt + jnp.where(jb > jp, ge, gt)
    keep = jnp.where(cnt < float(min(SLC_TOP, n_slc)), 0.0, NEG_BIAS)
    if n_slc < LANE:
        keep = jnp.concatenate([keep, jnp.full((LANE - n_slc, tq), NEG_BIAS, F32)], axis=0)
    sel_ref[0, 0] = jnp.where(valid, keep.T, NEG_BIAS).astype(BF16)


def _nsa_cmp_attn(proj3, tabs, kvc, tq=128):
    b, t, _ = proj3.shape
    g = N_KV_NSA
    nc = t // CMP_STRIDE
    n_slc = t // SLC_LEN
    rw = N_REP * LANE
    tab_spec = pl.BlockSpec((1, tq, LANE), lambda bi, gi, i: (bi, i, 0))
    return pl.pallas_call(
        functools.partial(_cmp_attn_kernel, n_slc=n_slc),
        grid=(b, g, t // tq),
        in_specs=[pl.BlockSpec((1, tq, rw), lambda bi, gi, i: (bi, i, gi)),
                  tab_spec, tab_spec, tab_spec,
                  pl.BlockSpec((1, 1, 1, nc, LANE), lambda bi, gi, i: (bi, 0, gi, 0, 0)),
                  pl.BlockSpec((1, 1, 1, nc, LANE), lambda bi, gi, i: (bi, 1, gi, 0, 0)),
                  pl.BlockSpec((1, tq, LANE), lambda bi, gi, i: (bi, i, CB_SMALL))],
        out_specs=[pl.BlockSpec((1, tq, rw), lambda bi, gi, i: (bi, i, gi)),
                   pl.BlockSpec((1, 1, tq, LANE), lambda bi, gi, i: (bi, gi, i, 0))],
        out_shape=[jax.ShapeDtypeStruct((b, t, N_HEADS_NSA * LANE), BF16),
                   jax.ShapeDtypeStruct((b, g, t, LANE), BF16)],
        compiler_params=_cparams(("parallel", "parallel", "parallel")),
        name="nsa_cmp",
    )(proj3, *tabs, kvc, kvc, proj3)


def _sel_attn_kernel(q_ref, c_ref, sa_ref, sb_ref, cf_ref, saf_ref, sbf_ref, k_ref, v_ref,
                     sel_ref, gate_ref, o_ref, ka_ref, vs_ref, m_ref, l_ref, acc_ref, s0_ref, s1_ref,
                     *, tk):
    g = pl.program_id(1)
    i = pl.program_id(2)
    tq = q_ref.shape[1]
    t = k_ref.shape[1]
    rows = N_REP * tq

    @pl.when(i == 0)
    def _():
        kr = _rope(k_ref[0], cf_ref[0], saf_ref[0], sbf_ref[0])
        ka_ref[:, 0:LANE] = kr.astype(BF16)
        pos = lax.broadcasted_iota(jnp.int32, (t, LANE), 0)
        lane = lax.broadcasted_iota(jnp.int32, (t, LANE), 1)
        ka_ref[:, LANE:2 * LANE] = jnp.where(lane == pos // SLC_LEN, 1.0, 0.0).astype(BF16)
        vs_ref[...] = v_ref[0].astype(BF16)

    q4 = _load_q(q_ref, c_ref[0], sa_ref[0], sb_ref[0])
    bias = sel_ref[0, 0]
    qa = jnp.concatenate([q4, jnp.concatenate([bias] * N_REP, axis=0)], axis=1)

    m_ref[...] = jnp.full((rows, LANE), -1e30, F32)
    l_ref[...] = jnp.zeros((rows, LANE), F32)
    acc_ref[...] = jnp.zeros((rows, LANE), F32)

    half = tk // 2

    def scores(kt, hlf):
        k0 = pl.multiple_of(kt * tk + hlf * half, half)
        return _mm_nt(qa, ka_ref[pl.ds(k0, half), :])

    def update(s_ref, kt, hlf, causal):
        k0 = pl.multiple_of(kt * tk + hlf * half, half)
        s = s_ref[...]
        if causal:
            r = lax.broadcasted_iota(jnp.int32, s.shape, 0)
            kp = k0 + lax.broadcasted_iota(jnp.int32, s.shape, 1)
            s = jnp.where(kp <= i * tq + (r & (tq - 1)), s, -1e30)
        tiles = _lane_tiles(s)
        m_old = m_ref[...]
        m_new = jnp.maximum(m_old, _rowmax(s))
        alpha = jnp.exp(m_old - m_new)
        ps = [jnp.exp(tl - m_new) for tl in tiles]
        l_ref[...] = alpha * l_ref[...] + jnp.sum(functools.reduce(jnp.add, ps), axis=1, keepdims=True)
        p = jnp.concatenate([x.astype(BF16) for x in ps], axis=1)
        acc_ref[...] = alpha * acc_ref[...] + jnp.dot(p, vs_ref[pl.ds(k0, half), :],
                                                      preferred_element_type=F32)
        m_ref[...] = m_new

    diag = (i * tq) // tk
    s0_ref[...] = scores(0, 0)

    def body(kt, carry):
        s1_ref[...] = scores(kt, 1)
        update(s0_ref, kt, 0, False)
        s0_ref[...] = scores(kt + 1, 0)
        update(s1_ref, kt, 1, False)
        return carry

    lax.fori_loop(0, diag, body, 0)
    s1_ref[...] = scores(diag, 1)
    update(s0_ref, diag, 0, True)
    update(s1_ref, diag, 1, True)
    o = acc_ref[...] / l_ref[...]
    _store_gated(o_ref, o, gate_ref, g, 1, tq)


def _nsa_sel_attn(proj3, tabs, selbias, tq=128, tk=512):
    b, t, _ = proj3.shape
    g = N_KV_NSA
    rw = N_REP * LANE
    tk = min(tk, t)
    tab_spec = pl.BlockSpec((1, tq, LANE), lambda bi, gi, i: (bi, i, 0))
    tabf_spec = pl.BlockSpec((1, t, LANE), lambda bi, gi, i: (bi, 0, 0))
    kcb = CB_KV + 1 * 2 * g
    return pl.pallas_call(
        functools.partial(_sel_attn_kernel, tk=tk),
        grid=(b, g, t // tq),
        in_specs=[pl.BlockSpec((1, tq, rw), lambda bi, gi, i: (bi, i, gi)),
                  tab_spec, tab_spec, tab_spec, tabf_spec, tabf_spec, tabf_spec,
                  pl.BlockSpec((1, t, LANE), lambda bi, gi, i: (bi, 0, kcb + gi)),
                  pl.BlockSpec((1, t, LANE), lambda bi, gi, i: (bi, 0, kcb + g + gi)),
                  pl.BlockSpec((1, 1, tq, LANE), lambda bi, gi, i: (bi, gi, i, 0)),
                  pl.BlockSpec((1, tq, LANE), lambda bi, gi, i: (bi, i, CB_SMALL))],
        out_specs=pl.BlockSpec((1, tq, rw), lambda bi, gi, i: (bi, i, gi)),
        out_shape=jax.ShapeDtypeStruct((b, t, N_HEADS_NSA * LANE), BF16),
        scratch_shapes=[pltpu.VMEM((t, 2 * LANE), BF16), pltpu.VMEM((t, LANE), BF16),
                        pltpu.VMEM((N_REP * tq, LANE), F32), pltpu.VMEM((N_REP * tq, LANE), F32),
                        pltpu.VMEM((N_REP * tq, LANE), F32),
                        pltpu.VMEM((N_REP * tq, tk // 2), F32), pltpu.VMEM((N_REP * tq, tk // 2), F32)],
        compiler_params=_cparams(("parallel", "parallel", "arbitrary")),
        name="nsa_sel",
    )(proj3, *tabs, *tabs, proj3, proj3, selbias, proj3)


def _win_attn_kernel(q_ref, c_ref, sa_ref, sb_ref, cf_ref, saf_ref, sbf_ref, k_ref, v_ref,
                     gate_ref, o_ref, ks_ref, vs_ref, *, span):
    g = pl.program_id(1)
    i = pl.program_id(2)
    tq = q_ref.shape[1]

    @pl.when(i == 0)
    def _():
        ks_ref[...] = _rope(k_ref[0], cf_ref[0], saf_ref[0], sbf_ref[0]).astype(BF16)
        vs_ref[...] = v_ref[0].astype(BF16)

    q4 = _load_q(q_ref, c_ref[0], sa_ref[0], sb_ref[0])
    k0 = pl.multiple_of(jnp.maximum(i * tq - WINDOW, 0), tq)
    s = _mm_nt(q4, ks_ref[pl.ds(k0, span), :])
    r = lax.broadcasted_iota(jnp.int32, s.shape, 0)
    kp = k0 + lax.broadcasted_iota(jnp.int32, s.shape, 1)
    dist = i * tq + (r & (tq - 1)) - kp
    mask = (dist >= 0) & (dist < WINDOW)
    sm = jnp.where(mask, s, -1e30)
    m = _rowmax(sm)
    p = jnp.where(mask, jnp.exp(sm - m), 0.0)
    l = _rowsum(p)
    o = _mm(p, vs_ref[pl.ds(k0, span), :]) / jnp.maximum(l, 1e-30)
    _store_gated(o_ref, o, gate_ref, g, 2, tq)


def _nsa_win_attn(proj3, tabs, tq=128):
    b, t, _ = proj3.shape
    g = N_KV_NSA
    rw = N_REP * LANE
    span = min(WINDOW + tq, t)
    tab_spec = pl.BlockSpec((1, tq, LANE), lambda bi, gi, i: (bi, i, 0))
    tabf_spec = pl.BlockSpec((1, t, LANE), lambda bi, gi, i: (bi, 0, 0))
    kcb = CB_KV + 2 * 2 * g
    return pl.pallas_call(
        functools.partial(_win_attn_kernel, span=span),
        grid=(b, g, t // tq),
        in_specs=[pl.BlockSpec((1, tq, rw), lambda bi, gi, i: (bi, i, gi)),
                  tab_spec, tab_spec, tab_spec, tabf_spec, tabf_spec, tabf_spec,
                  pl.BlockSpec((1, t, LANE), lambda bi, gi, i: (bi, 0, kcb + gi)),
                  pl.BlockSpec((1, t, LANE), lambda bi, gi, i: (bi, 0, kcb + g + gi)),
                  pl.BlockSpec((1, tq, LANE), lambda bi, gi, i: (bi, i, CB_SMALL))],
        out_specs=pl.BlockSpec((1, tq, rw), lambda bi, gi, i: (bi, i, gi)),
        out_shape=jax.ShapeDtypeStruct((b, t, N_HEADS_NSA * LANE), BF16),
        scratch_shapes=[pltpu.VMEM((t, LANE), BF16), pltpu.VMEM((t, LANE), BF16)],
        compiler_params=_cparams(("parallel", "parallel", "arbitrary")),
        name="nsa_win",
    )(proj3, *tabs, *tabs, proj3, proj3, proj3)


def _gdn_kernel(q_ref, k_ref, v_ref, z_ref, small_ref, cwq_ref, cwk_ref, cwv_ref, alog_ref, dtb_ref,
                nw_ref, o_ref, q_s, k_s, kb_s, vb_s, gc_s, qp_s, op_s, gl_s, mm_s, nn_s):
    h = pl.program_id(1)
    t = q_ref.shape[1]
    c = GDN_CHUNK
    nchunk = t // c
    rowi = lax.broadcasted_iota(jnp.int32, (t, LANE), 0)
    lanei = lax.broadcasted_iota(jnp.int32, (t, LANE), 1)

    def conv_silu(x, cw):
        y = x * cw[CONV_WIDTH - 1:CONV_WIDTH]
        for sft in range(1, CONV_WIDTH):
            xs = jnp.where(rowi >= sft, pltpu.roll(x, sft, 0), 0.0)
            y = y + xs * cw[CONV_WIDTH - 1 - sft:CONV_WIDTH - sft]
        return y * _sigmoid(y)

    def l2n(x):
        return x * lax.rsqrt(jnp.sum(x * x, axis=1, keepdims=True) + EPS)

    q = l2n(conv_silu(q_ref[0], cwq_ref[...])) * (HEAD_DIM ** -0.5)
    k = l2n(conv_silu(k_ref[0], cwk_ref[...]))
    v = conv_silu(v_ref[0], cwv_ref[...])

    small = small_ref[0]
    b_raw = jnp.sum(jnp.where(lanei == SMALL_BETA + h, small, 0.0), axis=1, keepdims=True)
    a_raw = jnp.sum(jnp.where(lanei == SMALL_DECAY + h, small, 0.0), axis=1, keepdims=True)
    beta = jnp.broadcast_to(_sigmoid(b_raw), (t, LANE))
    xa = jnp.broadcast_to(a_raw, (t, LANE)) + dtb_ref[0]
    softplus = jnp.maximum(xa, 0.0) + jnp.log(1.0 + jnp.exp(-jnp.abs(xa)))
    gstep = -jnp.exp(alog_ref[0]) * softplus
    gc = gstep
    sft = 1
    while sft < c:
        gc = gc + jnp.where((rowi & (c - 1)) >= sft, pltpu.roll(gc, sft, 0), 0.0)
        sft *= 2
    q_s[...] = q
    k_s[...] = k
    kb_s[...] = k * beta
    vb_s[...] = v * beta
    gc_s[...] = gc

    c2 = 2 * c
    ci = lax.broadcasted_iota(jnp.int32, (c2, c2), 0)
    cj = lax.broadcasted_iota(jnp.int32, (c2, c2), 1)
    same = (ci // c) == (cj // c)
    tril = same & (cj <= ci)
    strict = same & (cj < ci)
    eye = jnp.where(ci == cj, 1.0, 0.0)
    first = ci < c

    pairs_per_iter = 8
    rng = range(pairs_per_iter)

    def prep(it, carry):
        n2 = [it * pairs_per_iter + p for p in rng]
        sl = [pl.ds(pl.multiple_of(n * c2, c2), c2) for n in n2]
        gc2 = [gc_s[s_, :] for s_ in sl]
        kn = [k_s[s_, :] for s_ in sl]
        kbn = [kb_s[s_, :] for s_ in sl]
        decay = [jnp.exp(jnp.where(tril, g_ - g_.T[0:1, :], -1e30)) for g_ in gc2]
        kk = [_mm_nt(kbn[p], kn[p]) for p in rng]
        y = [-jnp.where(strict, kk[p] * decay[p], 0.0) for p in rng]
        pm = [eye + y_ for y_ in y]
        pw = 1
        while 2 * pw < c:
            y = [_mm(y_, y_) for y_ in y]
            pm = [pm[p] + _mm(pm[p], y[p]) for p in rng]
            pw *= 2
        egn = [jnp.exp(g_) for g_ in gc2]
        u = [_mm(pm[p], vb_s[sl[p], :]) for p in rng]
        w = [_mm(pm[p], kbn[p] * egn[p]) for p in rng]
        qn = [q_s[s_, :] for s_ in sl]
        qk = [_mm_nt(qn[p], kn[p]) for p in rng]
        attn = [jnp.where(tril, qk[p] * decay[p], 0.0) for p in rng]
        aw = [_mm(attn[p], w[p]) for p in rng]
        au = [_mm(attn[p], u[p]) for p in rng]
        for p in rng:
            qp_s[sl[p], :] = qn[p] * egn[p] - aw[p]
            op_s[sl[p], :] = au[p]
        gl = [jnp.where(first, g_[c - 1:c, :], g_[c2 - 1:c2, :]) for g_ in gc2]
        kdt = [(kn[p] * jnp.exp(gl[p] - gc2[p])).T for p in rng]
        for p in rng:
            for hlf in range(2):
                lo = hlf * c
                mm_s[2 * n2[p] + hlf] = _mm(kdt[p][:, lo:lo + c], w[p][lo:lo + c]).astype(BF16)
                nn_s[2 * n2[p] + hlf] = _mm(kdt[p][:, lo:lo + c], u[p][lo:lo + c])
                gl_s[pl.ds(2 * n2[p] + hlf, 1), :] = jnp.exp(gc2[p][lo + c - 1:lo + c, :])
        return carry

    lax.fori_loop(0, nchunk // (2 * pairs_per_iter), prep, 0)

    def scan(it, s):
        for j in range(2):
            n = it * 2 + j
            rs = pl.ds(pl.multiple_of(n * c, c), c)
            sb = s.astype(BF16)
            o = jnp.dot(qp_s[rs, :].astype(BF16), sb, preferred_element_type=F32) + op_s[rs, :]
            s = (s * gl_s[pl.ds(n, 1), :] - jnp.dot(mm_s[n], sb, preferred_element_type=F32)
                 + nn_s[n])
            vb_s[rs, :] = o
        return s

    lax.fori_loop(0, nchunk // 2, scan, jnp.zeros((LANE, LANE), F32))

    o = vb_s[...]
    on = o * lax.rsqrt(jnp.mean(o * o, axis=1, keepdims=True) + EPS) * nw_ref[...]
    z = z_ref[0]
    o_ref[0] = (on * (z * _sigmoid(z))).astype(o_ref.dtype)


def _gdn(proj3, conv_w, a_log, dt_bias, norm_w):
    b, t, _ = proj3.shape
    hh = N_HEADS_GDN
    c = GDN_CHUNK
    col = lambda cb: pl.BlockSpec((1, t, LANE), lambda bi, hi: (bi, 0, cb + hi))
    cw = lambda off: pl.BlockSpec((CONV_WIDTH, LANE), lambda bi, hi: (0, off + hi))
    hrow = pl.BlockSpec((1, 1, LANE), lambda bi, hi: (hi, 0, 0))
    alog_b = jnp.broadcast_to(a_log.astype(F32)[:, None, None], (hh, 1, LANE))
    dtb_b = jnp.broadcast_to(dt_bias.astype(F32)[:, None, None], (hh, 1, LANE))
    big = pltpu.VMEM((t, LANE), F32)
    return pl.pallas_call(
        _gdn_kernel,
        grid=(b, hh),
        in_specs=[col(CB_GQKV), col(CB_GQKV + hh), col(CB_GQKV + 2 * hh), col(CB_Z),
                  pl.BlockSpec((1, t, LANE), lambda bi, hi: (bi, 0, CB_SMALL)),
                  cw(0), cw(hh), cw(2 * hh), hrow, hrow,
                  pl.BlockSpec((1, LANE), lambda bi, hi: (0, 0))],
        out_specs=pl.BlockSpec((1, t, LANE), lambda bi, hi: (bi, 0, hi)),
        out_shape=jax.ShapeDtypeStruct((b, t, hh * LANE), BF16),
        scratch_shapes=[big, big, big, big, big, big, big,
                        pltpu.VMEM((t // c, LANE), F32),
                        pltpu.VMEM((t // c, LANE, LANE), BF16),
                        pltpu.VMEM((t // c, LANE, LANE), F32)],
        compiler_params=_cparams(("parallel", "parallel")),
        name="gdn",
    )(proj3, proj3, proj3, proj3, proj3, conv_w, conv_w, conv_w, alog_b, dtb_b,
      norm_w.reshape(1, LANE))


def _split3(a):
    hi = a.astype(BF16)
    lo = (a - hi.astype(F32)).astype(BF16)
    return hi, lo


def _out_proj_kernel(oc_ref, os_ref, ow_ref, ob_ref, x_ref, wo_ref, fw_ref, wr_ref, br_ref,
                     h_ref, hn_ref, ids_ref, wts_ref):
    half = oc_ref.shape[1]
    oa = (oc_ref[...].astype(F32) + os_ref[...].astype(F32) + ow_ref[...].astype(F32)).astype(BF16)
    h1 = (x_ref[...] + jnp.dot(oa, wo_ref[0:half, :], preferred_element_type=F32)
          + jnp.dot(ob_ref[...], wo_ref[half:2 * half, :], preferred_element_type=F32))
    h_ref[...] = h1
    hn = h1 * lax.rsqrt(jnp.mean(h1 * h1, axis=1, keepdims=True) + EPS) * fw_ref[...]
    hn_ref[...] = hn

    a_hi, a_lo = _split3(hn)
    w_hi, w_lo = _split3(wr_ref[...])
    dot = lambda a, b: jnp.dot(a, b, preferred_element_type=F32)
    logits = dot(a_hi, w_hi) + dot(a_hi, w_lo) + dot(a_lo, w_hi) + br_ref[...]
    lane = lax.broadcasted_iota(jnp.int32, logits.shape, 1)
    big = 1e30
    is_g = lane < N_GROUPS
    lg = jnp.where(is_g, logits, -big)
    gm = jnp.max(lg, axis=1, keepdims=True)
    grp = jnp.min(jnp.where(lg == gm, lane, LANE), axis=1, keepdims=True)
    p_grp = 1.0 / jnp.sum(jnp.where(is_g, jnp.exp(lg - gm), 0.0), axis=1, keepdims=True)
    e_id = lane - N_GROUPS
    in_g = (e_id >= 0) & (e_id < N_EXPERTS) & ((e_id // EXPERTS_PER_GROUP) == grp)
    le = jnp.where(in_g, logits, -big)
    em = jnp.max(le, axis=1, keepdims=True)
    pe = jnp.where(in_g, jnp.exp(le - em), 0.0)
    pe = pe / jnp.sum(pe, axis=1, keepdims=True)
    pm = jnp.where(in_g, pe, -1.0)
    p1 = jnp.max(pm, axis=1, keepdims=True)
    i1 = jnp.min(jnp.where(pm == p1, lane, LANE), axis=1, keepdims=True)
    pm2 = jnp.where(lane == i1, -1.0, pm)
    p2 = jnp.max(pm2, axis=1, keepdims=True)
    i2 = jnp.min(jnp.where(pm2 == p2, lane, LANE), axis=1, keepdims=True)
    den = p1 + p2
    ids_ref[...] = jnp.where(lane == 0, i1 - N_GROUPS, jnp.where(lane == 1, i2 - N_GROUPS, 0))
    wts_ref[...] = jnp.where(lane == 0, p1 / den * p_grp, jnp.where(lane == 1, p2 / den * p_grp, 0.0))


def _out_proj(oc, os_, ow, ob, x2, w_out, ffn_w, wr, br):
    n, d = x2.shape
    half = oc.shape[1]
    tm = min(256, n)
    row = lambda w: pl.BlockSpec((tm, w), lambda i: (i, 0))
    full = lambda a: pl.BlockSpec(a.shape, lambda i: (0,) * a.ndim)
    fw = ffn_w.reshape(1, d)
    return pl.pallas_call(
        _out_proj_kernel,
        grid=(n // tm,),
        in_specs=[row(half), row(half), row(half), row(half), row(d), full(w_out), full(fw),
                  full(wr), full(br)],
        out_specs=[row(d), row(d), row(LANE), row(LANE)],
        out_shape=[jax.ShapeDtypeStruct((n, d), F32), jax.ShapeDtypeStruct((n, d), F32),
                   jax.ShapeDtypeStruct((n, LANE), jnp.int32), jax.ShapeDtypeStruct((n, LANE), F32)],
        compiler_params=_cparams(("parallel",)),
        name="out_proj",
    )(oc, os_, ow, ob, x2, w_out, fw, wr, br)


def _moe_kernel(bexp_ref, bact_ref, rtok_ref, rtokn_ref, rslot_ref, hn_hbm, roww_ref, wg_ref, wu_ref,
                wd_ref, y_hbm, xbuf, ybuf, wgb, wub, wdb, cached, gsem, ssem, *, n_real):
    i = pl.program_id(0)
    nb = pl.num_programs(0)
    rb = ybuf.shape[1]
    slot = i & 1
    other = 1 - slot
    active = bact_ref[i] > 0
    prev_active = jnp.logical_and(i > 0, bact_ref[jnp.maximum(i - 1, 0)] > 0)

    def gather_copy(tok, r, buf):
        return pltpu.make_async_copy(hn_hbm.at[pl.ds(tok, 1), :], xbuf.at[buf, pl.ds(r, 1), :],
                                     gsem.at[buf])

    def wait_gather(buf):
        pltpu.make_async_copy(hn_hbm.at[pl.ds(0, rb), :], xbuf.at[buf], gsem.at[buf]).wait()

    def wait_scatter(buf):
        pltpu.make_async_copy(ybuf.at[buf], y_hbm.at[pl.ds(0, rb), :], ssem.at[buf]).wait()

    @pl.when(i == 0)
    def _():
        cached[0] = -1
        ybuf[0] = jnp.zeros(ybuf.shape[1:], F32)
        spare = pltpu.make_async_copy(ybuf.at[0], y_hbm.at[pl.ds(n_real, rb), :], ssem.at[0])
        spare.start()
        spare.wait()

        def first(r, carry):
            gather_copy(rtok_ref[0, 0, r], r, 0).start()
            return carry
        lax.fori_loop(0, rb, first, 0)

    @pl.when(jnp.logical_or(i == 0, prev_active))
    def _():
        wait_gather(slot)

    @pl.when(active)
    def _():
        e = bexp_ref[i]

        @pl.when(cached[0] != e)
        def _():
            wgb[...] = wg_ref[0].astype(BF16)
            wub[...] = wu_ref[0].astype(BF16)
            wdb[...] = wd_ref[0].astype(BF16)
            cached[0] = e

        for r in range(rb):
            gather_copy(rtokn_ref[0, 0, r], r, other).start()
        x = xbuf[slot].astype(BF16)
        hg = jnp.dot(x, wgb[...], preferred_element_type=F32)
        hu = jnp.dot(x, wub[...], preferred_element_type=F32)
        act = (hg * _sigmoid(hg) * hu).astype(BF16)
        ybuf[slot] = jnp.dot(act, wdb[...], preferred_element_type=F32) * roww_ref[...]
        for r in range(rb):
            pltpu.make_async_copy(ybuf.at[slot, pl.ds(r, 1), :],
                                  y_hbm.at[pl.ds(rslot_ref[0, 0, r], 1), :], ssem.at[slot]).start()

    @pl.when(prev_active)
    def _():
        wait_scatter(other)

    @pl.when(jnp.logical_and(i == nb - 1, active))
    def _():
        wait_scatter(slot)
        wait_gather(other)


def _moe(hn, bexp, bact, rtok, rslot, roww, w_gate, w_up, w_down, n_real):
    n, d = hn.shape
    nb = bexp.shape[0]
    rb = MOE_ROW_BLOCK
    de = w_gate.shape[2]
    smem_rows = pl.BlockSpec((1, 1, rb), lambda i, be, ba: (i, 0, 0), memory_space=pltpu.SMEM)
    smem_next = pl.BlockSpec((1, 1, rb), lambda i, be, ba: (jnp.minimum(i + 1, nb - 1), 0, 0),
                             memory_space=pltpu.SMEM)
    grid_spec = pltpu.PrefetchScalarGridSpec(
        num_scalar_prefetch=2,
        grid=(nb,),
        in_specs=[smem_rows, smem_next, smem_rows,
                  pl.BlockSpec(memory_space=pl.ANY),
                  pl.BlockSpec((rb, 1), lambda i, be, ba: (i, 0)),
                  pl.BlockSpec((1, d, de), lambda i, be, ba: (be[i], 0, 0)),
                  pl.BlockSpec((1, d, de), lambda i, be, ba: (be[i], 0, 0)),
                  pl.BlockSpec((1, de, d), lambda i, be, ba: (be[i], 0, 0))],
        out_specs=pl.BlockSpec(memory_space=pl.ANY),
        scratch_shapes=[pltpu.VMEM((2, rb, d), F32), pltpu.VMEM((2, rb, d), F32),
                        pltpu.VMEM((d, de), BF16), pltpu.VMEM((d, de), BF16), pltpu.VMEM((de, d), BF16),
                        pltpu.SMEM((1,), jnp.int32),
                        pltpu.SemaphoreType.DMA((2,)), pltpu.SemaphoreType.DMA((2,))])
    return pl.pallas_call(
        functools.partial(_moe_kernel, n_real=n_real),
        grid_spec=grid_spec,
        out_shape=jax.ShapeDtypeStruct((n_real + rb, d), F32),
        compiler_params=_cparams(("arbitrary",)),
        name="moe",
    )(bexp, bact, rtok, rtok, rslot, hn, roww, w_gate, w_up, w_down)


def _dispatch(ids, wts, n):
    k = 2
    m = n * k
    rb = MOE_ROW_BLOCK
    e_flat = ids[:, :k].reshape(m)
    w_flat = wts[:, :k].reshape(m)
    onehot = (e_flat[:, None] == jnp.arange(N_EXPERTS, dtype=jnp.int32)[None, :]).astype(jnp.int32)
    csum = jnp.cumsum(onehot, axis=0)
    counts = csum[-1]
    rank = jnp.take_along_axis(csum, e_flat[:, None], axis=1)[:, 0] - 1
    padded = (counts + rb - 1) // rb * rb
    pad_end = jnp.cumsum(padded)
    pad_start = pad_end - padded
    dest = pad_start[e_flat] + rank
    nb = (m + N_EXPERTS * (rb - 1) + rb - 1) // rb
    p = nb * rb
    row_m = jnp.full((p,), -1, jnp.int32).at[dest].set(jnp.arange(m, dtype=jnp.int32))
    real = row_m >= 0
    row_w = jnp.where(real, w_flat[jnp.maximum(row_m, 0)], 0.0)
    rtok = jnp.where(real, row_m // k, 0)
    rslot = jnp.where(real, (row_m % k) * n + row_m // k, m + jnp.arange(p, dtype=jnp.int32) % rb)
    starts = jnp.arange(nb, dtype=jnp.int32) * rb
    bexp = jnp.minimum(jnp.sum((pad_end[None, :] <= starts[:, None]).astype(jnp.int32), axis=1),
                       N_EXPERTS - 1)
    bact = jnp.sum(real.reshape(nb, rb).astype(jnp.int32), axis=1)
    last_e = jnp.max(jnp.where(bact > 0, bexp, 0))
    bexp = jnp.where(bact > 0, bexp, last_e)
    return (bexp, bact, rtok.reshape(nb, 1, rb), rslot.reshape(nb, 1, rb), row_w.reshape(p, 1), m)


def _combine_kernel(h_ref, y0_ref, y1_ref, fw_ref, o_ref):
    h = h_ref[...] + (y0_ref[...] + y1_ref[...])
    o_ref[...] = h * lax.rsqrt(jnp.mean(h * h, axis=1, keepdims=True) + EPS) * fw_ref[...]


def _combine(h1, y, final_w):
    n, d = h1.shape
    tm = min(256, n)
    nt = n // tm
    return pl.pallas_call(
        _combine_kernel,
        grid=(nt,),
        in_specs=[pl.BlockSpec((tm, d), lambda i: (i, 0)),
                  pl.BlockSpec((tm, d), lambda i: (i, 0)),
                  pl.BlockSpec((tm, d), lambda i: (i + nt, 0)),
                  pl.BlockSpec((1, d), lambda i: (0, 0))],
        out_specs=pl.BlockSpec((tm, d), lambda i: (i, 0)),
        out_shape=jax.ShapeDtypeStruct((n, d), F32),
        compiler_params=_cparams(("parallel",)),
        name="combine",
    )(h1, y, y, final_w.reshape(1, d))


def _rope_tables(positions):
    half = ROT_DIM // 2
    inv_freq = ROPE_THETA ** (-jnp.arange(0, ROT_DIM, 2, dtype=F32) / ROT_DIM)
    ang = positions.astype(F32)[..., None] * inv_freq
    cos, sin = jnp.cos(ang), jnp.sin(ang)
    b, t = positions.shape
    ones = jnp.ones((b, t, LANE - ROT_DIM), F32)
    zeros = jnp.zeros((b, t, LANE - half), F32)
    c = jnp.concatenate([cos, cos, ones], axis=-1)
    sa = jnp.concatenate([-sin, zeros], axis=-1)
    sb = jnp.concatenate([jnp.zeros((b, t, half), F32), sin, zeros[..., :LANE - ROT_DIM]], axis=-1)
    return c, sa, sb


def _arrange_w_in(w_in):
    d = w_in.shape[0]
    sizes = (N_HEADS_NSA * HEAD_DIM, 3 * 2 * N_KV_NSA * HEAD_DIM, 3 * N_HEADS_NSA,
             3 * N_HEADS_GDN * HEAD_DIM, N_HEADS_GDN, N_HEADS_GDN, N_HEADS_GDN * HEAD_DIM)
    offs = np.cumsum((0,) + sizes)
    seg = [w_in[:, offs[i]:offs[i + 1]] for i in range(len(sizes))]
    q, kv, gate, gqkv, gb, ga, gz = seg
    used = sum(sizes)
    pad = jnp.zeros((d, N_CB * LANE - used), w_in.dtype)
    return jnp.concatenate([q, kv, gqkv, gz, gate, gb, ga, pad], axis=1).astype(BF16)


def kernel(x, positions, attn_norm_w, w_in, cmp_wk, cmp_pek, cmp_wv, cmp_pev, gdn_conv_w, gdn_a_log,
           gdn_dt_bias, gdn_norm_w, w_out, ffn_norm_w, router_group_w, router_group_b,
           router_expert_w, router_expert_b, moe_w_gate, moe_w_up, moe_w_down, final_norm_w):
    b, t, d = x.shape
    n = b * t
    tabs = _rope_tables(positions)
    h = x.reshape(n, d)
    assert w_in.shape[0] == 1, "single-layer block only"
    for l in range(1):
        proj = _in_proj(h, attn_norm_w[l], _arrange_w_in(w_in[l]))
        proj3 = proj.reshape(b, t, N_CB * LANE)
        cmp_w = jnp.stack([cmp_wk[l], cmp_wv[l]])
        cmp_pe = jnp.stack([cmp_pek[l], cmp_pev[l]])
        kvc = _nsa_compress(proj3, tabs, cmp_w, cmp_pe)
        o_c, selbias = _nsa_cmp_attn(proj3, tabs, kvc)
        o_s = _nsa_sel_attn(proj3, tabs, selbias)
        o_w = _nsa_win_attn(proj3, tabs)
        o_b = _gdn(proj3, gdn_conv_w[l], gdn_a_log[l], gdn_dt_bias[l], gdn_norm_w[l])
        half = N_HEADS_NSA * HEAD_DIM
        wr = jnp.concatenate([router_group_w[l], router_expert_w[l],
                              jnp.zeros((d, LANE - N_GROUPS - N_EXPERTS), F32)], axis=1)
        br = jnp.concatenate([router_group_b[l], router_expert_b[l],
                              jnp.zeros((LANE - N_GROUPS - N_EXPERTS,), F32)]).reshape(1, LANE)
        h1, hn2, ids, wts = _out_proj(o_c.reshape(n, half), o_s.reshape(n, half), o_w.reshape(n, half),
                                      o_b.reshape(n, half), h, w_out[l].astype(BF16), ffn_norm_w[l], wr, br)
        bexp, bact, rtok, rslot, roww, n_slots = _dispatch(ids, wts, n)
        y = _moe(hn2, bexp, bact, rtok, rslot, roww, moe_w_gate[l], moe_w_up[l], moe_w_down[l], n_slots)
        out = _combine(h1, y, final_norm_w)
    return out.reshape(b, t, d)
```

```python
import functools

import numpy as np
import jax
import jax.numpy as jnp
from jax import lax
from jax.experimental import pallas as pl
from jax.experimental.pallas import tpu as pltpu

F32 = jnp.float32
BF16 = jnp.bfloat16

HEAD_DIM = 128
N_HEADS_NSA = 8
N_KV_NSA = 2
N_REP = N_HEADS_NSA // N_KV_NSA
N_HEADS_GDN = 8
ROT_DIM = 32
ROPE_THETA = 500000.0
CMP_LEN = 32
CMP_STRIDE = 16
SLC_LEN = 64
SLC_TOP = 16
WINDOW = 512
CONV_WIDTH = 4
GDN_CHUNK = 64
N_GROUPS = 8
EXPERTS_PER_GROUP = 8
N_EXPERTS = 64
MOE_ROW_BLOCK = 256
EPS = 1e-6
LANE = 128

CB_Q = 0
CB_KV = 8
CB_GQKV = 20
CB_Z = 44
CB_SMALL = 52
N_CB = 54
SMALL_BETA = 24
SMALL_DECAY = 32

NEG_BIAS = -32768.0
VMEM_LIMIT = 56 * 1024 * 1024


def _cparams(sem):
    return pltpu.CompilerParams(dimension_semantics=sem, vmem_limit_bytes=VMEM_LIMIT)


def _mm(a, b):
    return jnp.dot(a.astype(BF16), b.astype(BF16), preferred_element_type=F32)


def _mm_nt(a, b):
    return lax.dot_general(a.astype(BF16), b.astype(BF16), (((1,), (1,)), ((), ())),
                           preferred_element_type=F32)


def _rope(x, c, sa, sb):
    return (x * c + pltpu.roll(x, LANE - ROT_DIM // 2, 1) * sa
            + pltpu.roll(x, ROT_DIM // 2, 1) * sb)


def _sigmoid(x):
    return 0.5 * jnp.tanh(0.5 * x) + 0.5


def _lane_tiles(x):
    return [x[:, i:i + LANE] for i in range(0, x.shape[1], LANE)]


def _store_token_major(ref, x):
    rows, d = x.shape
    seg = d // LANE
    for s in range(seg):
        ref[pl.ds(s, rows, stride=seg), :] = x[:, s * LANE:(s + 1) * LANE]


def _load_token_major(ref, rows):
    seg = ref.shape[0] // rows
    return jnp.concatenate([ref[pl.ds(s, rows, stride=seg), :] for s in range(seg)], axis=1)


def _rowmax(x):
    return jnp.max(functools.reduce(jnp.maximum, _lane_tiles(x)), axis=1, keepdims=True)


def _rowsum(x):
    return jnp.sum(functools.reduce(jnp.add, _lane_tiles(x)), axis=1, keepdims=True)


def _in_proj_kernel(x_ref, nw_ref, w_ref, o_ref, small_ref, hn_ref, *, small_off):
    j = pl.program_id(1)

    @pl.when(j == 0)
    def _():
        x = x_ref[...]
        ms = jnp.mean(x * x, axis=-1, keepdims=True)
        hn_ref[...] = (x * lax.rsqrt(ms + EPS) * nw_ref[...]).astype(BF16)
    acc = jnp.dot(hn_ref[...], w_ref[...], preferred_element_type=F32)
    o_ref[...] = acc.astype(o_ref.dtype)

    @pl.when(j == pl.num_programs(1) - 1)
    def _():
        small_ref[...] = acc[:, small_off:small_off + LANE]


def _in_proj(x2, norm_w, w):
    n, d = x2.shape
    ncol = w.shape[1]
    tm = min(1024, n)
    tn = 768
    small_off = CB_SMALL * LANE - (ncol // tn - 1) * tn
    assert 0 <= small_off <= tn - LANE
    return pl.pallas_call(
        functools.partial(_in_proj_kernel, small_off=small_off),
        grid=(n // tm, ncol // tn),
        in_specs=[pl.BlockSpec((tm, d), lambda i, j: (i, 0)),
                  pl.BlockSpec((1, d), lambda i, j: (0, 0)),
                  pl.BlockSpec((d, tn), lambda i, j: (0, j))],
        out_specs=[pl.BlockSpec((tm, tn), lambda i, j: (i, j)),
                   pl.BlockSpec((tm, LANE), lambda i, j: (i, 0))],
        out_shape=[jax.ShapeDtypeStruct((n, ncol), BF16), jax.ShapeDtypeStruct((n, LANE), F32)],
        scratch_shapes=[pltpu.VMEM((tm, d), BF16)],
        compiler_params=_cparams(("parallel", "arbitrary")),
        name="in_proj",
    )(x2, norm_w.reshape(1, d), w)


def _compress_kernel(a_ref, c_ref, sa_ref, sb_ref, w_ref, pe_ref, o_ref, xs_ref):
    kv = pl.program_id(1)
    t = a_ref.shape[1]
    nc = t // CMP_STRIDE
    x = a_ref[0].astype(F32)
    xr = _rope(x, c_ref[0], sa_ref[0], sb_ref[0])
    x = jnp.where(kv == 0, xr, x)
    xs_ref[pl.ds(0, t), :] = x
    xs_ref[pl.ds(t, CMP_STRIDE), :] = jnp.zeros((CMP_STRIDE, LANE), F32)
    acc = jnp.zeros((nc, LANE), F32)
    for l in range(CMP_LEN):
        rows = xs_ref[pl.ds(l, nc, stride=CMP_STRIDE), :] + pe_ref[0, pl.ds(l, 1), :]
        acc = acc + _mm(rows, w_ref[0, l])
    o_ref[0, 0, 0] = acc.astype(BF16)


def _nsa_compress(proj3, tabs, cmp_w, cmp_pe):
    b, t, _ = proj3.shape
    g = N_KV_NSA
    nc = t // CMP_STRIDE
    tab_spec = pl.BlockSpec((1, t, LANE), lambda bi, kv, gi: (bi, 0, 0))
    return pl.pallas_call(
        _compress_kernel,
        grid=(b, 2, g),
        in_specs=[pl.BlockSpec((1, t, LANE), lambda bi, kv, gi: (bi, 0, CB_KV + kv * g + gi)),
                  tab_spec, tab_spec, tab_spec,
                  pl.BlockSpec((1, CMP_LEN, LANE, LANE), lambda bi, kv, gi: (kv, 0, 0, 0)),
                  pl.BlockSpec((1, CMP_LEN, LANE), lambda bi, kv, gi: (kv, 0, 0))],
        out_specs=pl.BlockSpec((1, 1, 1, nc, LANE), lambda bi, kv, gi: (bi, kv, gi, 0, 0)),
        out_shape=jax.ShapeDtypeStruct((b, 2, g, nc, LANE), BF16),
        scratch_shapes=[pltpu.VMEM((t + CMP_STRIDE, LANE), F32)],
        compiler_params=_cparams(("parallel", "arbitrary", "arbitrary")),
        name="nsa_compress",
    )(proj3, *tabs, cmp_w, cmp_pe)


def _load_q(q_ref, c, sa, sb):
    scale = HEAD_DIM ** -0.5
    qs = [(_rope(q_ref[0, :, r * LANE:(r + 1) * LANE].astype(F32), c, sa, sb) * scale).astype(BF16)
          for r in range(N_REP)]
    return jnp.concatenate(qs, axis=0)


def _store_gated(o_ref, o, gate_ref, g, branch, tq):
    gt = gate_ref[0]
    for r in range(N_REP):
        col = ((g * N_REP + r) * 3 + branch)
        lane = lax.broadcasted_iota(jnp.int32, gt.shape, 1)
        gcol = jnp.sum(jnp.where(lane == col, gt, 0.0), axis=1, keepdims=True)
        o_ref[0, :, r * LANE:(r + 1) * LANE] = (o[r * tq:(r + 1) * tq] * _sigmoid(gcol)).astype(o_ref.dtype)


def _cmp_attn_kernel(q_ref, c_ref, sa_ref, sb_ref, kc_ref, vc_ref, gate_ref, o_ref, sel_ref, *, n_slc):
    g = pl.program_id(1)
    i = pl.program_id(2)
    tq = q_ref.shape[1]
    nc = kc_ref.shape[3]
    q4 = _load_q(q_ref, c_ref[0], sa_ref[0], sb_ref[0])
    s = _mm_nt(q4, kc_ref[0, 0, 0])
    row = lax.broadcasted_iota(jnp.int32, s.shape, 0)
    n = lax.broadcasted_iota(jnp.int32, s.shape, 1)
    tpos = i * tq + (row & (tq - 1))
    mask = (n * CMP_STRIDE + (CMP_LEN - 1) <= tpos) & (n < nc - 1)
    sm = jnp.where(mask, s, -1e30)
    m = _rowmax(sm)
    p = jnp.where(mask, jnp.exp(sm - m), 0.0)
    l = _rowsum(p)
    p = p / jnp.maximum(l, 1e-30)
    o = _mm(p, vc_ref[0, 0, 0])
    _store_gated(o_ref, o, gate_ref, g, 0, tq)

    ps = p[0:tq]
    for r in range(1, N_REP):
        ps = ps + p[r * tq:(r + 1) * tq]
    cn = lax.broadcasted_iota(jnp.int32, (nc, LANE), 0)
    cj = lax.broadcasted_iota(jnp.int32, (nc, LANE), 1)
    ratio = SLC_LEN // CMP_STRIDE
    agg = ((cn >= ratio * cj - (CMP_LEN // CMP_STRIDE - 1)) & (cn < ratio * cj + ratio)
           & (cn < nc - 1) & (cj < n_slc))
    agg = jnp.where(agg, 1.0, 0.0).astype(BF16)
    ps_hi = ps.astype(BF16)
    ps_lo = (ps - ps_hi.astype(F32)).astype(BF16)
    imp = (jnp.dot(ps_hi, agg, preferred_element_type=F32)
           + jnp.dot(ps_lo, agg, preferred_element_type=F32))

    j = lax.broadcasted_iota(jnp.int32, (tq, LANE), 1)
    tt = i * tq + lax.broadcasted_iota(jnp.int32, (tq, LANE), 0)
    cur = tt // SLC_LEN
    valid = j <= cur
    forced = (j == 0) | (j == cur) | (j == cur - 1)
    vals = jnp.where(forced, 1e30, jnp.where(valid, imp, -1.0))
    vt = vals.T
    jb = lax.broadcasted_iota(jnp.int32, (n_slc, tq), 0)
    vb = vt[0:n_slc]
    cnt = jnp.zeros((n_slc, tq), F32)
    for jp in range(n_slc):
        cand = vt[jp:jp + 1, :]
        ge = jnp.where(cand >= vb, 1.0, 0.0)
        gt = jnp.where(cand > vb, 1.0, 0.0)
        cnt = cnt + jnp.where(jb > jp, ge, gt)
    keep = jnp.where(cnt < float(min(SLC_TOP, n_slc)), 0.0, NEG_BIAS)
    if n_slc < LANE:
        keep = jnp.concatenate([keep, jnp.full((LANE - n_slc, tq), NEG_BIAS, F32)], axis=0)
    sel_ref[0, 0] = jnp.where(valid, keep.T, NEG_BIAS).astype(BF16)


def _nsa_cmp_attn(proj3, small3, tabs, kvc, tq=128):
    b, t, _ = proj3.shape
    g = N_KV_NSA
    nc = t // CMP_STRIDE
    n_slc = t // SLC_LEN
    rw = N_REP * LANE
    tab_spec = pl.BlockSpec((1, tq, LANE), lambda bi, gi, i: (bi, i, 0))
    return pl.pallas_call(
        functools.partial(_cmp_attn_kernel, n_slc=n_slc),
        grid=(b, g, t // tq),
        in_specs=[pl.BlockSpec((1, tq, rw), lambda bi, gi, i: (bi, i, gi)),
                  tab_spec, tab_spec, tab_spec,
                  pl.BlockSpec((1, 1, 1, nc, LANE), lambda bi, gi, i: (bi, 0, gi, 0, 0)),
                  pl.BlockSpec((1, 1, 1, nc, LANE), lambda bi, gi, i: (bi, 1, gi, 0, 0)),
                  pl.BlockSpec((1, tq, LANE), lambda bi, gi, i: (bi, i, 0))],
        out_specs=[pl.BlockSpec((1, tq, rw), lambda bi, gi, i: (bi, i, gi)),
                   pl.BlockSpec((1, 1, tq, LANE), lambda bi, gi, i: (bi, gi, i, 0))],
        out_shape=[jax.ShapeDtypeStruct((b, t, N_HEADS_NSA * LANE), BF16),
                   jax.ShapeDtypeStruct((b, g, t, LANE), BF16)],
        compiler_params=_cparams(("parallel", "parallel", "parallel")),
        name="nsa_cmp",
    )(proj3, *tabs, kvc, kvc, small3)


def _sel_attn_kernel(q_ref, c_ref, sa_ref, sb_ref, cf_ref, saf_ref, sbf_ref, k_ref, v_ref,
                     sel_ref, gate_ref, o_ref, ka_ref, vs_ref, m_ref, l_ref, acc_ref, s0_ref, s1_ref,
                     *, tk):
    g = pl.program_id(1)
    i = pl.program_id(2)
    tq = q_ref.shape[1]
    t = k_ref.shape[1]
    rows = N_REP * tq

    @pl.when(i == 0)
    def _():
        kr = _rope(k_ref[0].astype(F32), cf_ref[0], saf_ref[0], sbf_ref[0])
        ka_ref[:, 0:LANE] = kr.astype(BF16)
        pos = lax.broadcasted_iota(jnp.int32, (t, LANE), 0)
        lane = lax.broadcasted_iota(jnp.int32, (t, LANE), 1)
        ka_ref[:, LANE:2 * LANE] = jnp.where(lane == pos // SLC_LEN, 1.0, 0.0).astype(BF16)
        vs_ref[...] = v_ref[0].astype(BF16)

    q4 = _load_q(q_ref, c_ref[0], sa_ref[0], sb_ref[0])
    bias = sel_ref[0, 0]
    qa = jnp.concatenate([q4, jnp.concatenate([bias] * N_REP, axis=0)], axis=1)

    m_ref[...] = jnp.full((rows, LANE), -1e30, F32)
    l_ref[...] = jnp.zeros((rows, LANE), F32)
    acc_ref[...] = jnp.zeros((rows, LANE), F32)

    half = tk // 2

    def scores(kt, hlf):
        k0 = pl.multiple_of(kt * tk + hlf * half, half)
        return _mm_nt(qa, ka_ref[pl.ds(k0, half), :])

    def update(s_ref, kt, hlf, causal):
        k0 = pl.multiple_of(kt * tk + hlf * half, half)
        s = s_ref[...]
        if causal:
            r = lax.broadcasted_iota(jnp.int32, s.shape, 0)
            kp = k0 + lax.broadcasted_iota(jnp.int32, s.shape, 1)
            s = jnp.where(kp <= i * tq + (r & (tq - 1)), s, -1e30)
        tiles = _lane_tiles(s)
        m_old = m_ref[...]
        m_new = jnp.maximum(m_old, _rowmax(s))
        alpha = jnp.exp(m_old - m_new)
        ps = [jnp.exp(tl - m_new) for tl in tiles]
        l_ref[...] = alpha * l_ref[...] + jnp.sum(functools.reduce(jnp.add, ps), axis=1, keepdims=True)
        p = jnp.concatenate([x.astype(BF16) for x in ps], axis=1)
        acc_ref[...] = alpha * acc_ref[...] + jnp.dot(p, vs_ref[pl.ds(k0, half), :],
                                                      preferred_element_type=F32)
        m_ref[...] = m_new

    diag = (i * tq) // tk
    s0_ref[...] = scores(0, 0)

    def body(kt, carry):
        s1_ref[...] = scores(kt, 1)
        update(s0_ref, kt, 0, False)
        s0_ref[...] = scores(kt + 1, 0)
        update(s1_ref, kt, 1, False)
        return carry

    lax.fori_loop(0, diag, body, 0)
    s1_ref[...] = scores(diag, 1)
    update(s0_ref, diag, 0, True)
    update(s1_ref, diag, 1, True)
    o = acc_ref[...] / l_ref[...]
    _store_gated(o_ref, o, gate_ref, g, 1, tq)


def _nsa_sel_attn(proj3, small3, tabs, selbias, tq=128, tk=512):
    b, t, _ = proj3.shape
    g = N_KV_NSA
    rw = N_REP * LANE
    tk = min(tk, t)
    tab_spec = pl.BlockSpec((1, tq, LANE), lambda bi, gi, i: (bi, i, 0))
    tabf_spec = pl.BlockSpec((1, t, LANE), lambda bi, gi, i: (bi, 0, 0))
    kcb = CB_KV + 1 * 2 * g
    return pl.pallas_call(
        functools.partial(_sel_attn_kernel, tk=tk),
        grid=(b, g, t // tq),
        in_specs=[pl.BlockSpec((1, tq, rw), lambda bi, gi, i: (bi, i, gi)),
                  tab_spec, tab_spec, tab_spec, tabf_spec, tabf_spec, tabf_spec,
                  pl.BlockSpec((1, t, LANE), lambda bi, gi, i: (bi, 0, kcb + gi)),
                  pl.BlockSpec((1, t, LANE), lambda bi, gi, i: (bi, 0, kcb + g + gi)),
                  pl.BlockSpec((1, 1, tq, LANE), lambda bi, gi, i: (bi, gi, i, 0)),
                  pl.BlockSpec((1, tq, LANE), lambda bi, gi, i: (bi, i, 0))],
        out_specs=pl.BlockSpec((1, tq, rw), lambda bi, gi, i: (bi, i, gi)),
        out_shape=jax.ShapeDtypeStruct((b, t, N_HEADS_NSA * LANE), BF16),
        scratch_shapes=[pltpu.VMEM((t, 2 * LANE), BF16), pltpu.VMEM((t, LANE), BF16),
                        pltpu.VMEM((N_REP * tq, LANE), F32), pltpu.VMEM((N_REP * tq, LANE), F32),
                        pltpu.VMEM((N_REP * tq, LANE), F32),
                        pltpu.VMEM((N_REP * tq, tk // 2), F32), pltpu.VMEM((N_REP * tq, tk // 2), F32)],
        compiler_params=_cparams(("parallel", "parallel", "arbitrary")),
        name="nsa_sel",
    )(proj3, *tabs, *tabs, proj3, proj3, selbias, small3)


def _win_attn_kernel(q_ref, c_ref, sa_ref, sb_ref, cf_ref, saf_ref, sbf_ref, k_ref, v_ref,
                     gate_ref, o_ref, ks_ref, vs_ref, *, span):
    g = pl.program_id(1)
    i = pl.program_id(2)
    tq = q_ref.shape[1]

    t = k_ref.shape[1]

    @pl.when(i == 0)
    def _():
        zeros = jnp.zeros((WINDOW, LANE), BF16)
        ks_ref[pl.ds(0, WINDOW), :] = zeros
        vs_ref[pl.ds(0, WINDOW), :] = zeros
        ks_ref[pl.ds(WINDOW, t), :] = _rope(k_ref[0].astype(F32), cf_ref[0], saf_ref[0],
                                            sbf_ref[0]).astype(BF16)
        vs_ref[pl.ds(WINDOW, t), :] = v_ref[0].astype(BF16)

    q4 = _load_q(q_ref, c_ref[0], sa_ref[0], sb_ref[0])
    k0 = pl.multiple_of(i * tq, tq)
    s = _mm_nt(q4, ks_ref[pl.ds(k0, span), :])
    tiles = _lane_tiles(s)
    r = lax.broadcasted_iota(jnp.int32, tiles[0].shape, 0) & (tq - 1)
    c = lax.broadcasted_iota(jnp.int32, tiles[0].shape, 1)
    first_block = WINDOW // tq - i
    masked = []
    for b, tl in enumerate(tiles):
        if b == 0:
            tl = jnp.where(c > r, tl, -1e30)
        if b == len(tiles) - 1:
            tl = jnp.where(c <= r, tl, -1e30)
        else:
            tl = jnp.where(b >= first_block, tl, -1e30)
        masked.append(tl)
    m = jnp.max(functools.reduce(jnp.maximum, masked), axis=1, keepdims=True)
    ps = [jnp.exp(tl - m) for tl in masked]
    l = jnp.sum(functools.reduce(jnp.add, ps), axis=1, keepdims=True)
    p = jnp.concatenate([x.astype(BF16) for x in ps], axis=1)
    o = jnp.dot(p, vs_ref[pl.ds(k0, span), :], preferred_element_type=F32) / l
    _store_gated(o_ref, o, gate_ref, g, 2, tq)


def _nsa_win_attn(proj3, small3, tabs, tq=128):
    b, t, _ = proj3.shape
    g = N_KV_NSA
    rw = N_REP * LANE
    span = WINDOW + tq
    assert tq == LANE
    tab_spec = pl.BlockSpec((1, tq, LANE), lambda bi, gi, i: (bi, i, 0))
    tabf_spec = pl.BlockSpec((1, t, LANE), lambda bi, gi, i: (bi, 0, 0))
    kcb = CB_KV + 2 * 2 * g
    return pl.pallas_call(
        functools.partial(_win_attn_kernel, span=span),
        grid=(b, g, t // tq),
        in_specs=[pl.BlockSpec((1, tq, rw), lambda bi, gi, i: (bi, i, gi)),
                  tab_spec, tab_spec, tab_spec, tabf_spec, tabf_spec, tabf_spec,
                  pl.BlockSpec((1, t, LANE), lambda bi, gi, i: (bi, 0, kcb + gi)),
                  pl.BlockSpec((1, t, LANE), lambda bi, gi, i: (bi, 0, kcb + g + gi)),
                  pl.BlockSpec((1, tq, LANE), lambda bi, gi, i: (bi, i, 0))],
        out_specs=pl.BlockSpec((1, tq, rw), lambda bi, gi, i: (bi, i, gi)),
        out_shape=jax.ShapeDtypeStruct((b, t, N_HEADS_NSA * LANE), BF16),
        scratch_shapes=[pltpu.VMEM((t + WINDOW, LANE), BF16), pltpu.VMEM((t + WINDOW, LANE), BF16)],
        compiler_params=_cparams(("parallel", "parallel", "arbitrary")),
        name="nsa_win",
    )(proj3, *tabs, *tabs, proj3, proj3, small3)


def _gdn_kernel(q_ref, k_ref, v_ref, z_ref, small_ref, cwq_ref, cwk_ref, cwv_ref, alog_ref, dtb_ref,
                nw_ref, o_ref, q_s, k_s, kb_s, vb_s, gc_s, qp_s, op_s, gl_s, mm_s, nn_s):
    h = pl.program_id(1)
    t = q_ref.shape[1]
    c = GDN_CHUNK
    nchunk = t // c
    rowi = lax.broadcasted_iota(jnp.int32, (t, LANE), 0)
    lanei = lax.broadcasted_iota(jnp.int32, (t, LANE), 1)

    def conv_silu(x, cw):
        y = x * cw[CONV_WIDTH - 1:CONV_WIDTH]
        for sft in range(1, CONV_WIDTH):
            xs = jnp.where(rowi >= sft, pltpu.roll(x, sft, 0), 0.0)
            y = y + xs * cw[CONV_WIDTH - 1 - sft:CONV_WIDTH - sft]
        return y * _sigmoid(y)

    def l2n(x):
        return x * lax.rsqrt(jnp.sum(x * x, axis=1, keepdims=True) + EPS)

    q = l2n(conv_silu(q_ref[0].astype(F32), cwq_ref[...])) * (HEAD_DIM ** -0.5)
    k = l2n(conv_silu(k_ref[0].astype(F32), cwk_ref[...]))
    v = conv_silu(v_ref[0].astype(F32), cwv_ref[...])

    small = small_ref[0]
    b_raw = jnp.sum(jnp.where(lanei == SMALL_BETA + h, small, 0.0), axis=1, keepdims=True)
    a_raw = jnp.sum(jnp.where(lanei == SMALL_DECAY + h, small, 0.0), axis=1, keepdims=True)
    beta = jnp.broadcast_to(_sigmoid(b_raw), (t, LANE))
    xa = jnp.broadcast_to(a_raw, (t, LANE)) + dtb_ref[0]
    softplus = jnp.maximum(xa, 0.0) + jnp.log(1.0 + jnp.exp(-jnp.abs(xa)))
    gstep = -jnp.exp(alog_ref[0]) * softplus
    gc = gstep
    sft = 1
    while sft < c:
        gc = gc + jnp.where((rowi & (c - 1)) >= sft, pltpu.roll(gc, sft, 0), 0.0)
        sft *= 2
    q_s[...] = q
    k_s[...] = k
    kb_s[...] = k * beta
    vb_s[...] = v * beta
    gc_s[...] = gc

    c2 = 2 * c
    ci = lax.broadcasted_iota(jnp.int32, (c2, c2), 0)
    cj = lax.broadcasted_iota(jnp.int32, (c2, c2), 1)
    same = (ci // c) == (cj // c)
    tril = same & (cj <= ci)
    strict = same & (cj < ci)
    eye = jnp.where(ci == cj, 1.0, 0.0)
    first = ci < c

    pairs_per_iter = 8
    rng = range(pairs_per_iter)

    def prep(it, carry):
        n2 = [it * pairs_per_iter + p for p in rng]
        sl = [pl.ds(pl.multiple_of(n * c2, c2), c2) for n in n2]
        gc2 = [gc_s[s_, :] for s_ in sl]
        kn = [k_s[s_, :] for s_ in sl]
        kbn = [kb_s[s_, :] for s_ in sl]
        decay = [jnp.exp(jnp.where(tril, g_ - g_.T[0:1, :], -1e30)) for g_ in gc2]
        kk = [_mm_nt(kbn[p], kn[p]) for p in rng]
        y = [-jnp.where(strict, kk[p] * decay[p], 0.0) for p in rng]
        pm = [eye + y_ for y_ in y]
        pw = 1
        while 2 * pw < c:
            y = [_mm(y_, y_) for y_ in y]
            pm = [pm[p] + _mm(pm[p], y[p]) for p in rng]
            pw *= 2
        egn = [jnp.exp(g_) for g_ in gc2]
        u = [_mm(pm[p], vb_s[sl[p], :]) for p in rng]
        w = [_mm(pm[p], kbn[p] * egn[p]) for p in rng]
        qn = [q_s[s_, :] for s_ in sl]
        qk = [_mm_nt(qn[p], kn[p]) for p in rng]
        attn = [jnp.where(tril, qk[p] * decay[p], 0.0) for p in rng]
        aw = [_mm(attn[p], w[p]) for p in rng]
        au = [_mm(attn[p], u[p]) for p in rng]
        for p in rng:
            qp_s[sl[p], :] = qn[p] * egn[p] - aw[p]
            op_s[sl[p], :] = au[p]
        gl = [jnp.where(first, g_[c - 1:c, :], g_[c2 - 1:c2, :]) for g_ in gc2]
        kdt = [(kn[p] * jnp.exp(gl[p] - gc2[p])).T for p in rng]
        for p in rng:
            for hlf in range(2):
                lo = hlf * c
                mm_s[2 * n2[p] + hlf] = _mm(kdt[p][:, lo:lo + c], w[p][lo:lo + c]).astype(BF16)
                nn_s[2 * n2[p] + hlf] = _mm(kdt[p][:, lo:lo + c], u[p][lo:lo + c])
                gl_s[pl.ds(2 * n2[p] + hlf, 1), :] = jnp.exp(gc2[p][lo + c - 1:lo + c, :])
        return carry

    lax.fori_loop(0, nchunk // (2 * pairs_per_iter), prep, 0)

    def scan(it, s):
        for j in range(2):
            n = it * 2 + j
            rs = pl.ds(pl.multiple_of(n * c, c), c)
            sb = s.astype(BF16)
            o = jnp.dot(qp_s[rs, :].astype(BF16), sb, preferred_element_type=F32) + op_s[rs, :]
            s = (s * gl_s[pl.ds(n, 1), :] - jnp.dot(mm_s[n], sb, preferred_element_type=F32)
                 + nn_s[n])
            vb_s[rs, :] = o
        return s

    lax.fori_loop(0, nchunk // 2, scan, jnp.zeros((LANE, LANE), F32))

    o = vb_s[...]
    on = o * lax.rsqrt(jnp.mean(o * o, axis=1, keepdims=True) + EPS) * nw_ref[...]
    z = z_ref[0].astype(F32)
    o_ref[0] = (on * (z * _sigmoid(z))).astype(o_ref.dtype)


def _gdn(proj3, small3, conv_w, a_log, dt_bias, norm_w):
    b, t, _ = proj3.shape
    hh = N_HEADS_GDN
    c = GDN_CHUNK
    col = lambda cb: pl.BlockSpec((1, t, LANE), lambda bi, hi: (bi, 0, cb + hi))
    cw = lambda off: pl.BlockSpec((CONV_WIDTH, LANE), lambda bi, hi: (0, off + hi))
    hrow = pl.BlockSpec((1, 1, LANE), lambda bi, hi: (hi, 0, 0))
    alog_b = jnp.broadcast_to(a_log.astype(F32)[:, None, None], (hh, 1, LANE))
    dtb_b = jnp.broadcast_to(dt_bias.astype(F32)[:, None, None], (hh, 1, LANE))
    big = pltpu.VMEM((t, LANE), F32)
    return pl.pallas_call(
        _gdn_kernel,
        grid=(b, hh),
        in_specs=[col(CB_GQKV), col(CB_GQKV + hh), col(CB_GQKV + 2 * hh), col(CB_Z),
                  pl.BlockSpec((1, t, LANE), lambda bi, hi: (bi, 0, 0)),
                  cw(0), cw(hh), cw(2 * hh), hrow, hrow,
                  pl.BlockSpec((1, LANE), lambda bi, hi: (0, 0))],
        out_specs=pl.BlockSpec((1, t, LANE), lambda bi, hi: (bi, 0, hi)),
        out_shape=jax.ShapeDtypeStruct((b, t, hh * LANE), BF16),
        scratch_shapes=[big, big, big, big, big, big, big,
                        pltpu.VMEM((t // c, LANE), F32),
                        pltpu.VMEM((t // c, LANE, LANE), BF16),
                        pltpu.VMEM((t // c, LANE, LANE), F32)],
        compiler_params=_cparams(("parallel", "parallel")),
        name="gdn",
    )(proj3, proj3, proj3, proj3, small3, conv_w, conv_w, conv_w, alog_b, dtb_b,
      norm_w.reshape(1, LANE))


def _split3(a):
    hi = a.astype(BF16)
    lo = (a - hi.astype(F32)).astype(BF16)
    return hi, lo


def _out_proj_kernel(oc_ref, os_ref, ow_ref, ob_ref, x_ref, wo_ref, fw_ref, wr_ref, br_ref,
                     h_ref, hn_ref, ids_ref, wts_ref):
    half = oc_ref.shape[1]
    oa = (oc_ref[...].astype(F32) + os_ref[...].astype(F32) + ow_ref[...].astype(F32)).astype(BF16)
    h1 = (x_ref[...] + jnp.dot(oa, wo_ref[0:half, :], preferred_element_type=F32)
          + jnp.dot(ob_ref[...], wo_ref[half:2 * half, :], preferred_element_type=F32))
    h_ref[...] = h1
    hn = h1 * lax.rsqrt(jnp.mean(h1 * h1, axis=1, keepdims=True) + EPS) * fw_ref[...]
    _store_token_major(hn_ref, hn)

    a_hi, a_lo = _split3(hn)
    w_hi, w_lo = _split3(wr_ref[...])
    dot = lambda a, b: jnp.dot(a, b, preferred_element_type=F32)
    logits = dot(a_hi, w_hi) + dot(a_hi, w_lo) + dot(a_lo, w_hi) + br_ref[...]
    lane = lax.broadcasted_iota(jnp.int32, logits.shape, 1)
    big = 1e30
    is_g = lane < N_GROUPS
    lg = jnp.where(is_g, logits, -big)
    gm = jnp.max(lg, axis=1, keepdims=True)
    grp = jnp.min(jnp.where(lg == gm, lane, LANE), axis=1, keepdims=True)
    p_grp = 1.0 / jnp.sum(jnp.where(is_g, jnp.exp(lg - gm), 0.0), axis=1, keepdims=True)
    e_id = lane - N_GROUPS
    in_g = (e_id >= 0) & (e_id < N_EXPERTS) & ((e_id // EXPERTS_PER_GROUP) == grp)
    le = jnp.where(in_g, logits, -big)
    em = jnp.max(le, axis=1, keepdims=True)
    pe = jnp.where(in_g, jnp.exp(le - em), 0.0)
    pe = pe / jnp.sum(pe, axis=1, keepdims=True)
    pm = jnp.where(in_g, pe, -1.0)
    p1 = jnp.max(pm, axis=1, keepdims=True)
    i1 = jnp.min(jnp.where(pm == p1, lane, LANE), axis=1, keepdims=True)
    pm2 = jnp.where(lane == i1, -1.0, pm)
    p2 = jnp.max(pm2, axis=1, keepdims=True)
    i2 = jnp.min(jnp.where(pm2 == p2, lane, LANE), axis=1, keepdims=True)
    den = p1 + p2
    ids_ref[...] = jnp.where(lane == 0, i1 - N_GROUPS, jnp.where(lane == 1, i2 - N_GROUPS, 0))
    wts_ref[...] = jnp.where(lane == 0, p1 / den * p_grp, jnp.where(lane == 1, p2 / den * p_grp, 0.0))


def _out_proj(oc, os_, ow, ob, x2, w_out, ffn_w, wr, br):
    n, d = x2.shape
    half = oc.shape[1]
    tm = min(256, n)
    row = lambda w: pl.BlockSpec((tm, w), lambda i: (i, 0))
    full = lambda a: pl.BlockSpec(a.shape, lambda i: (0,) * a.ndim)
    fw = ffn_w.reshape(1, d)
    return pl.pallas_call(
        _out_proj_kernel,
        grid=(n // tm,),
        in_specs=[row(half), row(half), row(half), row(half), row(d), full(w_out), full(fw),
                  full(wr), full(br)],
        out_specs=[row(d), pl.BlockSpec((tm * (d // LANE), LANE), lambda i: (i, 0)), row(LANE), row(LANE)],
        out_shape=[jax.ShapeDtypeStruct((n, d), F32), jax.ShapeDtypeStruct((n * (d // LANE), LANE), F32),
                   jax.ShapeDtypeStruct((n, LANE), jnp.int32), jax.ShapeDtypeStruct((n, LANE), F32)],
        compiler_params=_cparams(("parallel",)),
        name="out_proj",
    )(oc, os_, ow, ob, x2, w_out, fw, wr, br)


def _moe_kernel(bexp_ref, bact_ref, rtok_ref, rtokn_ref, rslot_ref, hn_hbm, wg_ref, wu_ref,
                wd_ref, y_hbm, xbuf, ybuf, wgb, wub, wdb, cached, gsem, ssem, *, n_real):
    i = pl.program_id(0)
    nb = pl.num_programs(0)
    rb = rtok_ref.shape[2]
    seg = ybuf.shape[1] // rb
    slot = i & 1
    other = 1 - slot
    active = bact_ref[i] > 0
    prev_active = jnp.logical_and(i > 0, bact_ref[jnp.maximum(i - 1, 0)] > 0)

    def gather_copy(tok, r, buf):
        return pltpu.make_async_copy(hn_hbm.at[pl.ds(pl.multiple_of(tok, seg), seg), :],
                                     xbuf.at[buf, pl.ds(r * seg, seg), :], gsem.at[buf])

    def wait_gather(buf):
        pltpu.make_async_copy(hn_hbm.at[pl.ds(0, rb * seg), :], xbuf.at[buf], gsem.at[buf]).wait()

    def wait_scatter(buf):
        pltpu.make_async_copy(ybuf.at[buf], y_hbm.at[pl.ds(0, rb * seg), :], ssem.at[buf]).wait()

    @pl.when(i == 0)
    def _():
        cached[0] = -1
        ybuf[0] = jnp.zeros(ybuf.shape[1:], F32)
        spare = pltpu.make_async_copy(ybuf.at[0], y_hbm.at[pl.ds(n_real * seg, rb * seg), :], ssem.at[0])
        spare.start()
        spare.wait()

        def first(r, carry):
            gather_copy(rtok_ref[0, 0, r], r, 0).start()
            return carry
        lax.fori_loop(0, rb, first, 0)

    @pl.when(jnp.logical_or(i == 0, prev_active))
    def _():
        wait_gather(slot)

    @pl.when(active)
    def _():
        e = bexp_ref[i]

        @pl.when(cached[0] != e)
        def _():
            wgb[...] = wg_ref[0].astype(BF16)
            wub[...] = wu_ref[0].astype(BF16)
            wdb[...] = wd_ref[0].astype(BF16)
            cached[0] = e

        for r in range(rb):
            gather_copy(rtokn_ref[0, 0, r], r, other).start()
        x = _load_token_major(xbuf.at[slot], rb).astype(BF16)
        hg = jnp.dot(x, wgb[...], preferred_element_type=F32)
        hu = jnp.dot(x, wub[...], preferred_element_type=F32)
        act = (hg * _sigmoid(hg) * hu).astype(BF16)
        y = jnp.dot(act, wdb[...], preferred_element_type=F32)
        _store_token_major(ybuf.at[slot], y)
        for r in range(rb):
            dst = pl.multiple_of(rslot_ref[0, 0, r], seg)
            pltpu.make_async_copy(ybuf.at[slot, pl.ds(r * seg, seg), :],
                                  y_hbm.at[pl.ds(dst, seg), :], ssem.at[slot]).start()

    @pl.when(prev_active)
    def _():
        wait_scatter(other)

    @pl.when(jnp.logical_and(i == nb - 1, active))
    def _():
        wait_scatter(slot)
        wait_gather(other)


def _moe(hn, bexp, bact, rtok, rslot, w_gate, w_up, w_down, n_real):
    d, de = w_gate.shape[1], w_gate.shape[2]
    seg = d // LANE
    nb = bexp.shape[0]
    rb = MOE_ROW_BLOCK
    smem_rows = pl.BlockSpec((1, 1, rb), lambda i, be, ba: (i, 0, 0), memory_space=pltpu.SMEM)
    smem_next = pl.BlockSpec((1, 1, rb), lambda i, be, ba: (jnp.minimum(i + 1, nb - 1), 0, 0),
                             memory_space=pltpu.SMEM)
    grid_spec = pltpu.PrefetchScalarGridSpec(
        num_scalar_prefetch=2,
        grid=(nb,),
        in_specs=[smem_rows, smem_next, smem_rows,
                  pl.BlockSpec(memory_space=pl.ANY),
                  pl.BlockSpec((1, d, de), lambda i, be, ba: (be[i], 0, 0)),
                  pl.BlockSpec((1, d, de), lambda i, be, ba: (be[i], 0, 0)),
                  pl.BlockSpec((1, de, d), lambda i, be, ba: (be[i], 0, 0))],
        out_specs=pl.BlockSpec(memory_space=pl.ANY),
        scratch_shapes=[pltpu.VMEM((2, rb * seg, LANE), F32), pltpu.VMEM((2, rb * seg, LANE), F32),
                        pltpu.VMEM((d, de), BF16), pltpu.VMEM((d, de), BF16), pltpu.VMEM((de, d), BF16),
                        pltpu.SMEM((1,), jnp.int32),
                        pltpu.SemaphoreType.DMA((2,)), pltpu.SemaphoreType.DMA((2,))])
    return pl.pallas_call(
        functools.partial(_moe_kernel, n_real=n_real),
        grid_spec=grid_spec,
        out_shape=jax.ShapeDtypeStruct(((n_real + rb) * seg, LANE), F32),
        compiler_params=_cparams(("arbitrary",)),
        name="moe",
    )(bexp, bact, rtok, rtok, rslot, hn, w_gate, w_up, w_down)


def _dispatch(ids, n, seg):
    k = 2
    m = n * k
    rb = MOE_ROW_BLOCK
    e_flat = ids[:, :k].reshape(m)
    onehot = (e_flat[:, None] == jnp.arange(N_EXPERTS, dtype=jnp.int32)[None, :]).astype(jnp.int32)
    csum = jnp.cumsum(onehot, axis=0)
    counts = csum[-1]
    rank = jnp.take_along_axis(csum, e_flat[:, None], axis=1)[:, 0] - 1
    padded = (counts + rb - 1) // rb * rb
    pad_end = jnp.cumsum(padded)
    pad_start = pad_end - padded
    dest = pad_start[e_flat] + rank
    nb = (m + N_EXPERTS * (rb - 1) + rb - 1) // rb
    p = nb * rb
    row_m = jnp.full((p,), -1, jnp.int32).at[dest].set(jnp.arange(m, dtype=jnp.int32))
    real = row_m >= 0
    rtok = jnp.where(real, row_m // k, 0)
    rslot = jnp.where(real, (row_m % k) * n + row_m // k, m + jnp.arange(p, dtype=jnp.int32) % rb)
    starts = jnp.arange(nb, dtype=jnp.int32) * rb
    bexp = jnp.minimum(jnp.sum((pad_end[None, :] <= starts[:, None]).astype(jnp.int32), axis=1),
                       N_EXPERTS - 1)
    bact = jnp.sum(real.reshape(nb, rb).astype(jnp.int32), axis=1)
    last_e = jnp.max(jnp.where(bact > 0, bexp, 0))
    bexp = jnp.where(bact > 0, bexp, last_e)
    return (bexp, bact, (rtok * seg).reshape(nb, 1, rb), (rslot * seg).reshape(nb, 1, rb),
            m)


def _combine_kernel(h_ref, y0_ref, y1_ref, wts_ref, fw_ref, o_ref):
    tm = h_ref.shape[0]
    wts = wts_ref[...]
    moe = (_load_token_major(y0_ref, tm) * wts[:, 0:1] + _load_token_major(y1_ref, tm) * wts[:, 1:2])
    h = h_ref[...] + moe
    o_ref[...] = h * lax.rsqrt(jnp.mean(h * h, axis=1, keepdims=True) + EPS) * fw_ref[...]


def _combine(h1, y, wts, final_w):
    n, d = h1.shape
    seg = d // LANE
    tm = min(256, n)
    nt = n // tm
    return pl.pallas_call(
        _combine_kernel,
        grid=(nt,),
        in_specs=[pl.BlockSpec((tm, d), lambda i: (i, 0)),
                  pl.BlockSpec((tm * seg, LANE), lambda i: (i, 0)),
                  pl.BlockSpec((tm * seg, LANE), lambda i: (i + nt, 0)),
                  pl.BlockSpec((tm, LANE), lambda i: (i, 0)),
                  pl.BlockSpec((1, d), lambda i: (0, 0))],
        out_specs=pl.BlockSpec((tm, d), lambda i: (i, 0)),
        out_shape=jax.ShapeDtypeStruct((n, d), F32),
        compiler_params=_cparams(("parallel",)),
        name="combine",
    )(h1, y, y, wts, final_w.reshape(1, d))


def _rope_tables(positions):
    half = ROT_DIM // 2
    inv_freq = ROPE_THETA ** (-jnp.arange(0, ROT_DIM, 2, dtype=F32) / ROT_DIM)
    ang = positions.astype(F32)[..., None] * inv_freq
    cos, sin = jnp.cos(ang), jnp.sin(ang)
    b, t = positions.shape
    ones = jnp.ones((b, t, LANE - ROT_DIM), F32)
    zeros = jnp.zeros((b, t, LANE - half), F32)
    c = jnp.concatenate([cos, cos, ones], axis=-1)
    sa = jnp.concatenate([-sin, zeros], axis=-1)
    sb = jnp.concatenate([jnp.zeros((b, t, half), F32), sin, zeros[..., :LANE - ROT_DIM]], axis=-1)
    return c, sa, sb


def _arrange_w_in(w_in):
    d = w_in.shape[0]
    sizes = (N_HEADS_NSA * HEAD_DIM, 3 * 2 * N_KV_NSA * HEAD_DIM, 3 * N_HEADS_NSA,
             3 * N_HEADS_GDN * HEAD_DIM, N_HEADS_GDN, N_HEADS_GDN, N_HEADS_GDN * HEAD_DIM)
    offs = np.cumsum((0,) + sizes)
    seg = [w_in[:, offs[i]:offs[i + 1]] for i in range(len(sizes))]
    q, kv, gate, gqkv, gb, ga, gz = seg
    used = sum(sizes)
    pad = jnp.zeros((d, N_CB * LANE - used), w_in.dtype)
    return jnp.concatenate([q, kv, gqkv, gz, gate, gb, ga, pad], axis=1).astype(BF16)


def kernel(x, positions, attn_norm_w, w_in, cmp_wk, cmp_pek, cmp_wv, cmp_pev, gdn_conv_w, gdn_a_log,
           gdn_dt_bias, gdn_norm_w, w_out, ffn_norm_w, router_group_w, router_group_b,
           router_expert_w, router_expert_b, moe_w_gate, moe_w_up, moe_w_down, final_norm_w):
    b, t, d = x.shape
    n = b * t
    tabs = _rope_tables(positions)
    h = x.reshape(n, d)
    assert w_in.shape[0] == 1, "single-layer block only"
    for l in range(1):
        proj, small = _in_proj(h, attn_norm_w[l], _arrange_w_in(w_in[l]))
        proj3 = proj.reshape(b, t, N_CB * LANE)
        small3 = small.reshape(b, t, LANE)
        cmp_w = jnp.stack([cmp_wk[l], cmp_wv[l]])
        cmp_pe = jnp.stack([cmp_pek[l], cmp_pev[l]])
        kvc = _nsa_compress(proj3, tabs, cmp_w, cmp_pe)
        o_c, selbias = _nsa_cmp_attn(proj3, small3, tabs, kvc)
        o_s = _nsa_sel_attn(proj3, small3, tabs, selbias)
        o_w = _nsa_win_attn(proj3, small3, tabs)
        o_b = _gdn(proj3, small3, gdn_conv_w[l], gdn_a_log[l], gdn_dt_bias[l], gdn_norm_w[l])
        half = N_HEADS_NSA * HEAD_DIM
        wr = jnp.concatenate([router_group_w[l], router_expert_w[l],
                              jnp.zeros((d, LANE - N_GROUPS - N_EXPERTS), F32)], axis=1)
        br = jnp.concatenate([router_group_b[l], router_expert_b[l],
                              jnp.zeros((LANE - N_GROUPS - N_EXPERTS,), F32)]).reshape(1, LANE)
        h1, hn2, ids, wts = _out_proj(o_c.reshape(n, half), o_s.reshape(n, half), o_w.reshape(n, half),
                                      o_b.reshape(n, half), h, w_out[l].astype(BF16), ffn_norm_w[l], wr, br)
        bexp, bact, rtok, rslot, n_slots = _dispatch(ids, n, d // LANE)
        y = _moe(hn2, bexp, bact, rtok, rslot, moe_w_gate[l], moe_w_up[l], moe_w_down[l], n_slots)
        out = _combine(h1, y, wts, final_norm_w)
    return out.reshape(b, t, d)
```

```python
import functools

import numpy as np
import jax
import jax.numpy as jnp
from jax import lax
from jax.experimental import pallas as pl
from jax.experimental.pallas import tpu as pltpu

F32 = jnp.float32
BF16 = jnp.bfloat16

HEAD_DIM = 128
N_HEADS_NSA = 8
N_KV_NSA = 2
N_REP = N_HEADS_NSA // N_KV_NSA
N_HEADS_GDN = 8
ROT_DIM = 32
ROPE_THETA = 500000.0
CMP_LEN = 32
CMP_STRIDE = 16
SLC_LEN = 64
SLC_TOP = 16
WINDOW = 512
CONV_WIDTH = 4
GDN_CHUNK = 64
N_GROUPS = 8
EXPERTS_PER_GROUP = 8
N_EXPERTS = 64
MOE_ROW_BLOCK = 256
EPS = 1e-6
LANE = 128

CB_Q = 0
CB_KV = 8
CB_GQKV = 20
CB_Z = 44
CB_SMALL = 52
N_CB = 54
SMALL_BETA = 24
SMALL_DECAY = 32

NEG_BIAS = -32768.0
VMEM_LIMIT = 56 * 1024 * 1024


def _cparams(sem):
    return pltpu.CompilerParams(dimension_semantics=sem, vmem_limit_bytes=VMEM_LIMIT)


def _mm(a, b):
    return jnp.dot(a.astype(BF16), b.astype(BF16), preferred_element_type=F32)


def _mm_nt(a, b):
    return lax.dot_general(a.astype(BF16), b.astype(BF16), (((1,), (1,)), ((), ())),
                           preferred_element_type=F32)


def _rope(x, c, sa, sb):
    return (x * c + pltpu.roll(x, LANE - ROT_DIM // 2, 1) * sa
            + pltpu.roll(x, ROT_DIM // 2, 1) * sb)


def _sigmoid(x):
    return 0.5 * jnp.tanh(0.5 * x) + 0.5


def _lane_tiles(x):
    return [x[:, i:i + LANE] for i in range(0, x.shape[1], LANE)]


def _store_token_major(ref, x):
    rows, d = x.shape
    seg = d // LANE
    for s in range(seg):
        ref[pl.ds(s, rows, stride=seg), :] = x[:, s * LANE:(s + 1) * LANE]


def _load_token_major(ref, rows):
    seg = ref.shape[0] // rows
    return jnp.concatenate([ref[pl.ds(s, rows, stride=seg), :] for s in range(seg)], axis=1)


def _rowmax(x):
    return jnp.max(functools.reduce(jnp.maximum, _lane_tiles(x)), axis=1, keepdims=True)


def _rowsum(x):
    return jnp.sum(functools.reduce(jnp.add, _lane_tiles(x)), axis=1, keepdims=True)


def _in_proj_kernel(x_ref, nw_ref, w_ref, o_ref, small_ref, hn_ref, *, small_off):
    j = pl.program_id(1)

    @pl.when(j == 0)
    def _():
        x = x_ref[...]
        ms = jnp.mean(x * x, axis=-1, keepdims=True)
        hn_ref[...] = (x * lax.rsqrt(ms + EPS) * nw_ref[...]).astype(BF16)
    acc = jnp.dot(hn_ref[...], w_ref[...], preferred_element_type=F32)
    o_ref[...] = acc.astype(o_ref.dtype)

    @pl.when(j == pl.num_programs(1) - 1)
    def _():
        small_ref[...] = acc[:, small_off:small_off + LANE]


def _in_proj(x2, norm_w, w):
    n, d = x2.shape
    ncol = w.shape[1]
    tm = min(1024, n)
    tn = 768
    small_off = CB_SMALL * LANE - (ncol // tn - 1) * tn
    assert 0 <= small_off <= tn - LANE
    return pl.pallas_call(
        functools.partial(_in_proj_kernel, small_off=small_off),
        grid=(n // tm, ncol // tn),
        in_specs=[pl.BlockSpec((tm, d), lambda i, j: (i, 0)),
                  pl.BlockSpec((1, d), lambda i, j: (0, 0)),
                  pl.BlockSpec((d, tn), lambda i, j: (0, j))],
        out_specs=[pl.BlockSpec((tm, tn), lambda i, j: (i, j)),
                   pl.BlockSpec((tm, LANE), lambda i, j: (i, 0))],
        out_shape=[jax.ShapeDtypeStruct((n, ncol), BF16), jax.ShapeDtypeStruct((n, LANE), F32)],
        scratch_shapes=[pltpu.VMEM((tm, d), BF16)],
        compiler_params=_cparams(("parallel", "arbitrary")),
        name="in_proj",
    )(x2, norm_w.reshape(1, d), w)


def _compress_kernel(a_ref, c_ref, sa_ref, sb_ref, w_ref, pe_ref, o_ref, xs_ref):
    kv = pl.program_id(1)
    t = a_ref.shape[1]
    nc = t // CMP_STRIDE
    x = a_ref[0].astype(F32)
    xr = _rope(x, c_ref[0], sa_ref[0], sb_ref[0])
    x = jnp.where(kv == 0, xr, x)
    xs_ref[pl.ds(0, t), :] = x
    xs_ref[pl.ds(t, CMP_STRIDE), :] = jnp.zeros((CMP_STRIDE, LANE), F32)
    acc = jnp.zeros((nc, LANE), F32)
    for l in range(CMP_LEN):
        rows = xs_ref[pl.ds(l, nc, stride=CMP_STRIDE), :] + pe_ref[0, pl.ds(l, 1), :]
        acc = acc + _mm(rows, w_ref[0, l])
    o_ref[0, 0, 0] = acc.astype(BF16)


def _nsa_compress(proj3, tabs, cmp_w, cmp_pe):
    b, t, _ = proj3.shape
    g = N_KV_NSA
    nc = t // CMP_STRIDE
    tab_spec = pl.BlockSpec((1, t, LANE), lambda bi, kv, gi: (bi, 0, 0))
    return pl.pallas_call(
        _compress_kernel,
        grid=(b, 2, g),
        in_specs=[pl.BlockSpec((1, t, LANE), lambda bi, kv, gi: (bi, 0, CB_KV + kv * g + gi)),
                  tab_spec, tab_spec, tab_spec,
                  pl.BlockSpec((1, CMP_LEN, LANE, LANE), lambda bi, kv, gi: (kv, 0, 0, 0)),
                  pl.BlockSpec((1, CMP_LEN, LANE), lambda bi, kv, gi: (kv, 0, 0))],
        out_specs=pl.BlockSpec((1, 1, 1, nc, LANE), lambda bi, kv, gi: (bi, kv, gi, 0, 0)),
        out_shape=jax.ShapeDtypeStruct((b, 2, g, nc, LANE), BF16),
        scratch_shapes=[pltpu.VMEM((t + CMP_STRIDE, LANE), F32)],
        compiler_params=_cparams(("parallel", "arbitrary", "arbitrary")),
        name="nsa_compress",
    )(proj3, *tabs, cmp_w, cmp_pe)


def _load_q(q_ref, c, sa, sb):
    scale = HEAD_DIM ** -0.5
    qs = [(_rope(q_ref[0, :, r * LANE:(r + 1) * LANE].astype(F32), c, sa, sb) * scale).astype(BF16)
          for r in range(N_REP)]
    return jnp.concatenate(qs, axis=0)


def _store_gated(o_ref, o, gate_ref, g, branch, tq):
    gt = gate_ref[0]
    for r in range(N_REP):
        col = ((g * N_REP + r) * 3 + branch)
        lane = lax.broadcasted_iota(jnp.int32, gt.shape, 1)
        gcol = jnp.sum(jnp.where(lane == col, gt, 0.0), axis=1, keepdims=True)
        o_ref[0, :, r * LANE:(r + 1) * LANE] = (o[r * tq:(r + 1) * tq] * _sigmoid(gcol)).astype(o_ref.dtype)


def _cmp_attn_kernel(q_ref, c_ref, sa_ref, sb_ref, kc_ref, vc_ref, gate_ref, o_ref, sel_ref, *, n_slc):
    g = pl.program_id(1)
    i = pl.program_id(2)
    tq = q_ref.shape[1]
    nc = kc_ref.shape[3]
    q4 = _load_q(q_ref, c_ref[0], sa_ref[0], sb_ref[0])
    s = _mm_nt(q4, kc_ref[0, 0, 0])
    row = lax.broadcasted_iota(jnp.int32, s.shape, 0)
    n = lax.broadcasted_iota(jnp.int32, s.shape, 1)
    tpos = i * tq + (row & (tq - 1))
    mask = (n * CMP_STRIDE + (CMP_LEN - 1) <= tpos) & (n < nc - 1)
    sm = jnp.where(mask, s, -1e30)
    m = _rowmax(sm)
    p = jnp.where(mask, jnp.exp(sm - m), 0.0)
    l = _rowsum(p)
    p = p / jnp.maximum(l, 1e-30)
    o = _mm(p, vc_ref[0, 0, 0])
    _store_gated(o_ref, o, gate_ref, g, 0, tq)

    ps = p[0:tq]
    for r in range(1, N_REP):
        ps = ps + p[r * tq:(r + 1) * tq]
    cn = lax.broadcasted_iota(jnp.int32, (nc, LANE), 0)
    cj = lax.broadcasted_iota(jnp.int32, (nc, LANE), 1)
    ratio = SLC_LEN // CMP_STRIDE
    agg = ((cn >= ratio * cj - (CMP_LEN // CMP_STRIDE - 1)) & (cn < ratio * cj + ratio)
           & (cn < nc - 1) & (cj < n_slc))
    agg = jnp.where(agg, 1.0, 0.0).astype(BF16)
    ps_hi = ps.astype(BF16)
    ps_lo = (ps - ps_hi.astype(F32)).astype(BF16)
    imp = (jnp.dot(ps_hi, agg, preferred_element_type=F32)
           + jnp.dot(ps_lo, agg, preferred_element_type=F32))

    j = lax.broadcasted_iota(jnp.int32, (tq, LANE), 1)
    tt = i * tq + lax.broadcasted_iota(jnp.int32, (tq, LANE), 0)
    cur = tt // SLC_LEN
    valid = j <= cur
    forced = (j == 0) | (j == cur) | (j == cur - 1)
    vals = jnp.where(forced, 1e30, jnp.where(valid, imp, -1.0))
    vt = vals.T
    jb = lax.broadcasted_iota(jnp.int32, (n_slc, tq), 0)
    vb = vt[0:n_slc]
    cnt = jnp.zeros((n_slc, tq), F32)
    for jp in range(n_slc):
        cand = vt[jp:jp + 1, :]
        ge = jnp.where(cand >= vb, 1.0, 0.0)
        gt = jnp.where(cand > vb, 1.0, 0.0)
        cnt = cnt + jnp.where(jb > jp, ge, gt)
    keep = jnp.where(cnt < float(min(SLC_TOP, n_slc)), 0.0, NEG_BIAS)
    if n_slc < LANE:
        keep = jnp.concatenate([keep, jnp.full((LANE - n_slc, tq), NEG_BIAS, F32)], axis=0)
    sel_ref[0, 0] = jnp.where(valid, keep.T, NEG_BIAS).astype(BF16)


def _nsa_cmp_attn(proj3, small3, tabs, kvc, tq=128):
    b, t, _ = proj3.shape
    g = N_KV_NSA
    nc = t // CMP_STRIDE
    n_slc = t // SLC_LEN
    rw = N_REP * LANE
    tab_spec = pl.BlockSpec((1, tq, LANE), lambda bi, gi, i: (bi, i, 0))
    return pl.pallas_call(
        functools.partial(_cmp_attn_kernel, n_slc=n_slc),
        grid=(b, g, t // tq),
        in_specs=[pl.BlockSpec((1, tq, rw), lambda bi, gi, i: (bi, i, gi)),
                  tab_spec, tab_spec, tab_spec,
                  pl.BlockSpec((1, 1, 1, nc, LANE), lambda bi, gi, i: (bi, 0, gi, 0, 0)),
                  pl.BlockSpec((1, 1, 1, nc, LANE), lambda bi, gi, i: (bi, 1, gi, 0, 0)),
                  pl.BlockSpec((1, tq, LANE), lambda bi, gi, i: (bi, i, 0))],
        out_specs=[pl.BlockSpec((1, tq, rw), lambda bi, gi, i: (bi, i, gi)),
                   pl.BlockSpec((1, 1, tq, LANE), lambda bi, gi, i: (bi, gi, i, 0))],
        out_shape=[jax.ShapeDtypeStruct((b, t, N_HEADS_NSA * LANE), BF16),
                   jax.ShapeDtypeStruct((b, g, t, LANE), BF16)],
        compiler_params=_cparams(("parallel", "parallel", "parallel")),
        name="nsa_cmp",
    )(proj3, *tabs, kvc, kvc, small3)


def _sel_attn_kernel(q_ref, c_ref, sa_ref, sb_ref, cf_ref, saf_ref, sbf_ref, k_ref, v_ref,
                     sel_ref, gate_ref, o_ref, ka_ref, vs_ref, m_ref, l_ref, acc_ref, s0_ref, s1_ref,
                     *, tk):
    g = pl.program_id(1)
    i = pl.program_id(2)
    tq = q_ref.shape[1]
    t = k_ref.shape[1]
    rows = N_REP * tq

    @pl.when(i == 0)
    def _():
        kr = _rope(k_ref[0].astype(F32), cf_ref[0], saf_ref[0], sbf_ref[0])
        ka_ref[:, 0:LANE] = kr.astype(BF16)
        pos = lax.broadcasted_iota(jnp.int32, (t, LANE), 0)
        lane = lax.broadcasted_iota(jnp.int32, (t, LANE), 1)
        ka_ref[:, LANE:2 * LANE] = jnp.where(lane == pos // SLC_LEN, 1.0, 0.0).astype(BF16)
        vs_ref[...] = v_ref[0].astype(BF16)

    q4 = _load_q(q_ref, c_ref[0], sa_ref[0], sb_ref[0])
    bias = sel_ref[0, 0]
    qa = jnp.concatenate([q4, jnp.concatenate([bias] * N_REP, axis=0)], axis=1)

    m_ref[...] = jnp.full((rows, LANE), -1e30, F32)
    l_ref[...] = jnp.zeros((rows, LANE), F32)
    acc_ref[...] = jnp.zeros((rows, LANE), F32)

    half = tk // 2

    def scores(kt, hlf):
        k0 = pl.multiple_of(kt * tk + hlf * half, half)
        return _mm_nt(qa, ka_ref[pl.ds(k0, half), :])

    def update(s_ref, kt, hlf, causal):
        k0 = pl.multiple_of(kt * tk + hlf * half, half)
        s = s_ref[...]
        if causal:
            r = lax.broadcasted_iota(jnp.int32, s.shape, 0)
            kp = k0 + lax.broadcasted_iota(jnp.int32, s.shape, 1)
            s = jnp.where(kp <= i * tq + (r & (tq - 1)), s, -1e30)
        tiles = _lane_tiles(s)
        m_old = m_ref[...]
        m_new = jnp.maximum(m_old, _rowmax(s))
        alpha = jnp.exp(m_old - m_new)
        ps = [jnp.exp(tl - m_new) for tl in tiles]
        l_ref[...] = alpha * l_ref[...] + jnp.sum(functools.reduce(jnp.add, ps), axis=1, keepdims=True)
        p = jnp.concatenate([x.astype(BF16) for x in ps], axis=1)
        acc_ref[...] = alpha * acc_ref[...] + jnp.dot(p, vs_ref[pl.ds(k0, half), :],
                                                      preferred_element_type=F32)
        m_ref[...] = m_new

    diag = (i * tq) // tk
    s0_ref[...] = scores(0, 0)

    def body(kt, carry):
        s1_ref[...] = scores(kt, 1)
        update(s0_ref, kt, 0, False)
        s0_ref[...] = scores(kt + 1, 0)
        update(s1_ref, kt, 1, False)
        return carry

    lax.fori_loop(0, diag, body, 0)
    s1_ref[...] = scores(diag, 1)
    update(s0_ref, diag, 0, True)
    update(s1_ref, diag, 1, True)
    o = acc_ref[...] / l_ref[...]
    _store_gated(o_ref, o, gate_ref, g, 1, tq)


def _nsa_sel_attn(proj3, small3, tabs, selbias, tq=128, tk=512):
    b, t, _ = proj3.shape
    g = N_KV_NSA
    rw = N_REP * LANE
    tk = min(tk, t)
    tab_spec = pl.BlockSpec((1, tq, LANE), lambda bi, gi, i: (bi, i, 0))
    tabf_spec = pl.BlockSpec((1, t, LANE), lambda bi, gi, i: (bi, 0, 0))
    kcb = CB_KV + 1 * 2 * g
    return pl.pallas_call(
        functools.partial(_sel_attn_kernel, tk=tk),
        grid=(b, g, t // tq),
        in_specs=[pl.BlockSpec((1, tq, rw), lambda bi, gi, i: (bi, i, gi)),
                  tab_spec, tab_spec, tab_spec, tabf_spec, tabf_spec, tabf_spec,
                  pl.BlockSpec((1, t, LANE), lambda bi, gi, i: (bi, 0, kcb + gi)),
                  pl.BlockSpec((1, t, LANE), lambda bi, gi, i: (bi, 0, kcb + g + gi)),
                  pl.BlockSpec((1, 1, tq, LANE), lambda bi, gi, i: (bi, gi, i, 0)),
                  pl.BlockSpec((1, tq, LANE), lambda bi, gi, i: (bi, i, 0))],
        out_specs=pl.BlockSpec((1, tq, rw), lambda bi, gi, i: (bi, i, gi)),
        out_shape=jax.ShapeDtypeStruct((b, t, N_HEADS_NSA * LANE), BF16),
        scratch_shapes=[pltpu.VMEM((t, 2 * LANE), BF16), pltpu.VMEM((t, LANE), BF16),
                        pltpu.VMEM((N_REP * tq, LANE), F32), pltpu.VMEM((N_REP * tq, LANE), F32),
                        pltpu.VMEM((N_REP * tq, LANE), F32),
                        pltpu.VMEM((N_REP * tq, tk // 2), F32), pltpu.VMEM((N_REP * tq, tk // 2), F32)],
        compiler_params=_cparams(("parallel", "parallel", "arbitrary")),
        name="nsa_sel",
    )(proj3, *tabs, *tabs, proj3, proj3, selbias, small3)


def _win_attn_kernel(q_ref, c_ref, sa_ref, sb_ref, cf_ref, saf_ref, sbf_ref, k_ref, v_ref,
                     gate_ref, o_ref, ks_ref, vs_ref, *, span):
    g = pl.program_id(1)
    i = pl.program_id(2)
    tq = q_ref.shape[1]

    t = k_ref.shape[1]

    @pl.when(i == 0)
    def _():
        zeros = jnp.zeros((WINDOW, LANE), BF16)
        ks_ref[pl.ds(0, WINDOW), :] = zeros
        vs_ref[pl.ds(0, WINDOW), :] = zeros
        ks_ref[pl.ds(WINDOW, t), :] = _rope(k_ref[0].astype(F32), cf_ref[0], saf_ref[0],
                                            sbf_ref[0]).astype(BF16)
        vs_ref[pl.ds(WINDOW, t), :] = v_ref[0].astype(BF16)

    q4 = _load_q(q_ref, c_ref[0], sa_ref[0], sb_ref[0])
    k0 = pl.multiple_of(i * tq, tq)
    s = _mm_nt(q4, ks_ref[pl.ds(k0, span), :])
    tiles = _lane_tiles(s)
    r = lax.broadcasted_iota(jnp.int32, tiles[0].shape, 0) & (tq - 1)
    c = lax.broadcasted_iota(jnp.int32, tiles[0].shape, 1)
    first_block = WINDOW // tq - i
    masked = []
    for b, tl in enumerate(tiles):
        if b == 0:
            tl = jnp.where(c > r, tl, -1e30)
        if b == len(tiles) - 1:
            tl = jnp.where(c <= r, tl, -1e30)
        else:
            tl = jnp.where(b >= first_block, tl, -1e30)
        masked.append(tl)
    m = jnp.max(functools.reduce(jnp.maximum, masked), axis=1, keepdims=True)
    ps = [jnp.exp(tl - m) for tl in masked]
    l = jnp.sum(functools.reduce(jnp.add, ps), axis=1, keepdims=True)
    p = jnp.concatenate([x.astype(BF16) for x in ps], axis=1)
    o = jnp.dot(p, vs_ref[pl.ds(k0, span), :], preferred_element_type=F32) / l
    _store_gated(o_ref, o, gate_ref, g, 2, tq)


def _nsa_win_attn(proj3, small3, tabs, tq=128):
    b, t, _ = proj3.shape
    g = N_KV_NSA
    rw = N_REP * LANE
    span = WINDOW + tq
    assert tq == LANE
    tab_spec = pl.BlockSpec((1, tq, LANE), lambda bi, gi, i: (bi, i, 0))
    tabf_spec = pl.BlockSpec((1, t, LANE), lambda bi, gi, i: (bi, 0, 0))
    kcb = CB_KV + 2 * 2 * g
    return pl.pallas_call(
        functools.partial(_win_attn_kernel, span=span),
        grid=(b, g, t // tq),
        in_specs=[pl.BlockSpec((1, tq, rw), lambda bi, gi, i: (bi, i, gi)),
                  tab_spec, tab_spec, tab_spec, tabf_spec, tabf_spec, tabf_spec,
                  pl.BlockSpec((1, t, LANE), lambda bi, gi, i: (bi, 0, kcb + gi)),
                  pl.BlockSpec((1, t, LANE), lambda bi, gi, i: (bi, 0, kcb + g + gi)),
                  pl.BlockSpec((1, tq, LANE), lambda bi, gi, i: (bi, i, 0))],
        out_specs=pl.BlockSpec((1, tq, rw), lambda bi, gi, i: (bi, i, gi)),
        out_shape=jax.ShapeDtypeStruct((b, t, N_HEADS_NSA * LANE), BF16),
        scratch_shapes=[pltpu.VMEM((t + WINDOW, LANE), BF16), pltpu.VMEM((t + WINDOW, LANE), BF16)],
        compiler_params=_cparams(("parallel", "parallel", "arbitrary")),
        name="nsa_win",
    )(proj3, *tabs, *tabs, proj3, proj3, small3)


def _gdn_kernel(q_ref, k_ref, v_ref, z_ref, small_ref, cwq_ref, cwk_ref, cwv_ref, alog_ref, dtb_ref,
                nw_ref, o_ref, q_s, k_s, kb_s, vb_s, gc_s, qp_s, op_s, gl_s, mm_s, nn_s):
    h = pl.program_id(1)
    t = q_ref.shape[1]
    c = GDN_CHUNK
    nchunk = t // c
    rowi = lax.broadcasted_iota(jnp.int32, (t, LANE), 0)
    lanei = lax.broadcasted_iota(jnp.int32, (t, LANE), 1)

    def conv_silu(x, cw):
        y = x * cw[CONV_WIDTH - 1:CONV_WIDTH]
        for sft in range(1, CONV_WIDTH):
            xs = jnp.where(rowi >= sft, pltpu.roll(x, sft, 0), 0.0)
            y = y + xs * cw[CONV_WIDTH - 1 - sft:CONV_WIDTH - sft]
        return y * _sigmoid(y)

    def l2n(x):
        return x * lax.rsqrt(jnp.sum(x * x, axis=1, keepdims=True) + EPS)

    q = l2n(conv_silu(q_ref[0].astype(F32), cwq_ref[...])) * (HEAD_DIM ** -0.5)
    k = l2n(conv_silu(k_ref[0].astype(F32), cwk_ref[...]))
    v = conv_silu(v_ref[0].astype(F32), cwv_ref[...])

    small = small_ref[0]
    b_raw = jnp.sum(jnp.where(lanei == SMALL_BETA + h, small, 0.0), axis=1, keepdims=True)
    a_raw = jnp.sum(jnp.where(lanei == SMALL_DECAY + h, small, 0.0), axis=1, keepdims=True)
    beta = jnp.broadcast_to(_sigmoid(b_raw), (t, LANE))
    xa = jnp.broadcast_to(a_raw, (t, LANE)) + dtb_ref[0]
    softplus = jnp.maximum(xa, 0.0) + jnp.log(1.0 + jnp.exp(-jnp.abs(xa)))
    gstep = -jnp.exp(alog_ref[0]) * softplus
    gc = gstep
    sft = 1
    while sft < c:
        gc = gc + jnp.where((rowi & (c - 1)) >= sft, pltpu.roll(gc, sft, 0), 0.0)
        sft *= 2
    q_s[...] = q
    k_s[...] = k
    kb_s[...] = k * beta
    vb_s[...] = v * beta
    gc_s[...] = gc

    c2 = 2 * c
    ci = lax.broadcasted_iota(jnp.int32, (c2, c2), 0)
    cj = lax.broadcasted_iota(jnp.int32, (c2, c2), 1)
    same = (ci // c) == (cj // c)
    tril = same & (cj <= ci)
    strict = same & (cj < ci)
    eye = jnp.where(ci == cj, 1.0, 0.0)
    first = ci < c

    pairs_per_iter = 8
    rng = range(pairs_per_iter)

    def prep(it, carry):
        n2 = [it * pairs_per_iter + p for p in rng]
        sl = [pl.ds(pl.multiple_of(n * c2, c2), c2) for n in n2]
        gc2 = [gc_s[s_, :] for s_ in sl]
        kn = [k_s[s_, :] for s_ in sl]
        kbn = [kb_s[s_, :] for s_ in sl]
        decay = [jnp.exp(jnp.where(tril, g_ - g_.T[0:1, :], -1e30)) for g_ in gc2]
        kk = [_mm_nt(kbn[p], kn[p]) for p in rng]
        y = [-jnp.where(strict, kk[p] * decay[p], 0.0) for p in rng]
        pm = [eye + y_ for y_ in y]
        pw = 1
        while 2 * pw < c:
            y = [_mm(y_, y_) for y_ in y]
            pm = [pm[p] + _mm(pm[p], y[p]) for p in rng]
            pw *= 2
        egn = [jnp.exp(g_) for g_ in gc2]
        u = [_mm(pm[p], vb_s[sl[p], :]) for p in rng]
        w = [_mm(pm[p], kbn[p] * egn[p]) for p in rng]
        qn = [q_s[s_, :] for s_ in sl]
        qk = [_mm_nt(qn[p], kn[p]) for p in rng]
        attn = [jnp.where(tril, qk[p] * decay[p], 0.0) for p in rng]
        aw = [_mm(attn[p], w[p]) for p in rng]
        au = [_mm(attn[p], u[p]) for p in rng]
        for p in rng:
            qp_s[sl[p], :] = qn[p] * egn[p] - aw[p]
            op_s[sl[p], :] = au[p]
        gl = [jnp.where(first, g_[c - 1:c, :], g_[c2 - 1:c2, :]) for g_ in gc2]
        kdt = [(kn[p] * jnp.exp(gl[p] - gc2[p])).T for p in rng]
        for p in rng:
            for hlf in range(2):
                lo = hlf * c
                mm_s[2 * n2[p] + hlf] = _mm(kdt[p][:, lo:lo + c], w[p][lo:lo + c]).astype(BF16)
                nn_s[2 * n2[p] + hlf] = _mm(kdt[p][:, lo:lo + c], u[p][lo:lo + c])
                gl_s[pl.ds(2 * n2[p] + hlf, 1), :] = jnp.exp(gc2[p][lo + c - 1:lo + c, :])
        return carry

    lax.fori_loop(0, nchunk // (2 * pairs_per_iter), prep, 0)

    def scan(it, s):
        for j in range(2):
            n = it * 2 + j
            rs = pl.ds(pl.multiple_of(n * c, c), c)
            sb = s.astype(BF16)
            o = jnp.dot(qp_s[rs, :].astype(BF16), sb, preferred_element_type=F32) + op_s[rs, :]
            s = (s * gl_s[pl.ds(n, 1), :] - jnp.dot(mm_s[n], sb, preferred_element_type=F32)
                 + nn_s[n])
            vb_s[rs, :] = o
        return s

    lax.fori_loop(0, nchunk // 2, scan, jnp.zeros((LANE, LANE), F32))

    o = vb_s[...]
    on = o * lax.rsqrt(jnp.mean(o * o, axis=1, keepdims=True) + EPS) * nw_ref[...]
    z = z_ref[0].astype(F32)
    o_ref[0] = (on * (z * _sigmoid(z))).astype(o_ref.dtype)


def _gdn(proj3, small3, conv_w, a_log, dt_bias, norm_w):
    b, t, _ = proj3.shape
    hh = N_HEADS_GDN
    c = GDN_CHUNK
    col = lambda cb: pl.BlockSpec((1, t, LANE), lambda bi, hi: (bi, 0, cb + hi))
    cw = lambda off: pl.BlockSpec((CONV_WIDTH, LANE), lambda bi, hi: (0, off + hi))
    hrow = pl.BlockSpec((1, 1, LANE), lambda bi, hi: (hi, 0, 0))
    alog_b = jnp.broadcast_to(a_log.astype(F32)[:, None, None], (hh, 1, LANE))
    dtb_b = jnp.broadcast_to(dt_bias.astype(F32)[:, None, None], (hh, 1, LANE))
    big = pltpu.VMEM((t, LANE), F32)
    return pl.pallas_call(
        _gdn_kernel,
        grid=(b, hh),
        in_specs=[col(CB_GQKV), col(CB_GQKV + hh), col(CB_GQKV + 2 * hh), col(CB_Z),
                  pl.BlockSpec((1, t, LANE), lambda bi, hi: (bi, 0, 0)),
                  cw(0), cw(hh), cw(2 * hh), hrow, hrow,
                  pl.BlockSpec((1, LANE), lambda bi, hi: (0, 0))],
        out_specs=pl.BlockSpec((1, t, LANE), lambda bi, hi: (bi, 0, hi)),
        out_shape=jax.ShapeDtypeStruct((b, t, hh * LANE), BF16),
        scratch_shapes=[big, big, big, big, big, big, big,
                        pltpu.VMEM((t // c, LANE), F32),
                        pltpu.VMEM((t // c, LANE, LANE), BF16),
                        pltpu.VMEM((t // c, LANE, LANE), F32)],
        compiler_params=_cparams(("parallel", "parallel")),
        name="gdn",
    )(proj3, proj3, proj3, proj3, small3, conv_w, conv_w, conv_w, alog_b, dtb_b,
      norm_w.reshape(1, LANE))


def _split3(a):
    hi = a.astype(BF16)
    lo = (a - hi.astype(F32)).astype(BF16)
    return hi, lo


def _out_proj_kernel(oc_ref, os_ref, ow_ref, ob_ref, x_ref, wo_ref, fw_ref, wr_ref, br_ref,
                     h_ref, hn_ref, ids_ref, wts_ref):
    half = oc_ref.shape[1]
    oa = (oc_ref[...].astype(F32) + os_ref[...].astype(F32) + ow_ref[...].astype(F32)).astype(BF16)
    h1 = (x_ref[...] + jnp.dot(oa, wo_ref[0:half, :], preferred_element_type=F32)
          + jnp.dot(ob_ref[...], wo_ref[half:2 * half, :], preferred_element_type=F32))
    h_ref[...] = h1
    hn = h1 * lax.rsqrt(jnp.mean(h1 * h1, axis=1, keepdims=True) + EPS) * fw_ref[...]
    _store_token_major(hn_ref, hn)

    a_hi, a_lo = _split3(hn)
    w_hi, w_lo = _split3(wr_ref[...])
    dot = lambda a, b: jnp.dot(a, b, preferred_element_type=F32)
    logits = dot(a_hi, w_hi) + dot(a_hi, w_lo) + dot(a_lo, w_hi) + br_ref[...]
    lane = lax.broadcasted_iota(jnp.int32, logits.shape, 1)
    big = 1e30
    is_g = lane < N_GROUPS
    lg = jnp.where(is_g, logits, -big)
    gm = jnp.max(lg, axis=1, keepdims=True)
    grp = jnp.min(jnp.where(lg == gm, lane, LANE), axis=1, keepdims=True)
    p_grp = 1.0 / jnp.sum(jnp.where(is_g, jnp.exp(lg - gm), 0.0), axis=1, keepdims=True)
    e_id = lane - N_GROUPS
    in_g = (e_id >= 0) & (e_id < N_EXPERTS) & ((e_id // EXPERTS_PER_GROUP) == grp)
    le = jnp.where(in_g, logits, -big)
    em = jnp.max(le, axis=1, keepdims=True)
    pe = jnp.where(in_g, jnp.exp(le - em), 0.0)
    pe = pe / jnp.sum(pe, axis=1, keepdims=True)
    pm = jnp.where(in_g, pe, -1.0)
    p1 = jnp.max(pm, axis=1, keepdims=True)
    i1 = jnp.min(jnp.where(pm == p1, lane, LANE), axis=1, keepdims=True)
    pm2 = jnp.where(lane == i1, -1.0, pm)
    p2 = jnp.max(pm2, axis=1, keepdims=True)
    i2 = jnp.min(jnp.where(pm2 == p2, lane, LANE), axis=1, keepdims=True)
    den = p1 + p2
    ids_ref[...] = jnp.where(lane == 0, i1 - N_GROUPS, jnp.where(lane == 1, i2 - N_GROUPS, 0))
    wts_ref[...] = jnp.where(lane == 0, p1 / den * p_grp, jnp.where(lane == 1, p2 / den * p_grp, 0.0))


def _out_proj(oc, os_, ow, ob, x2, w_out, ffn_w, wr, br):
    n, d = x2.shape
    half = oc.shape[1]
    tm = min(256, n)
    row = lambda w: pl.BlockSpec((tm, w), lambda i: (i, 0))
    full = lambda a: pl.BlockSpec(a.shape, lambda i: (0,) * a.ndim)
    fw = ffn_w.reshape(1, d)
    return pl.pallas_call(
        _out_proj_kernel,
        grid=(n // tm,),
        in_specs=[row(half), row(half), row(half), row(half), row(d), full(w_out), full(fw),
                  full(wr), full(br)],
        out_specs=[row(d), pl.BlockSpec((tm * (d // LANE), LANE), lambda i: (i, 0)), row(LANE), row(LANE)],
        out_shape=[jax.ShapeDtypeStruct((n, d), F32), jax.ShapeDtypeStruct((n * (d // LANE), LANE), F32),
                   jax.ShapeDtypeStruct((n, LANE), jnp.int32), jax.ShapeDtypeStruct((n, LANE), F32)],
        compiler_params=_cparams(("parallel",)),
        name="out_proj",
    )(oc, os_, ow, ob, x2, w_out, fw, wr, br)


def _moe_kernel(bexp_ref, bact_ref, rtok_ref, rtokn_ref, rslot_ref, hn_hbm, wg_ref, wu_ref,
                wd_ref, y_hbm, xbuf, ybuf, wgb, wub, wdb, cached, gsem, ssem, *, n_real):
    i = pl.program_id(0)
    nb = pl.num_programs(0)
    rb = rtok_ref.shape[2]
    seg = ybuf.shape[1] // rb
    slot = i & 1
    other = 1 - slot
    active = bact_ref[i] > 0
    prev_active = jnp.logical_and(i > 0, bact_ref[jnp.maximum(i - 1, 0)] > 0)

    def gather_copy(tok, r, buf):
        return pltpu.make_async_copy(hn_hbm.at[pl.ds(pl.multiple_of(tok, seg), seg), :],
                                     xbuf.at[buf, pl.ds(r * seg, seg), :], gsem.at[buf])

    def wait_gather(buf):
        pltpu.make_async_copy(hn_hbm.at[pl.ds(0, rb * seg), :], xbuf.at[buf], gsem.at[buf]).wait()

    def wait_scatter(buf):
        pltpu.make_async_copy(ybuf.at[buf], y_hbm.at[pl.ds(0, rb * seg), :], ssem.at[buf]).wait()

    @pl.when(i == 0)
    def _():
        cached[0] = -1
        ybuf[0] = jnp.zeros(ybuf.shape[1:], F32)
        spare = pltpu.make_async_copy(ybuf.at[0], y_hbm.at[pl.ds(n_real * seg, rb * seg), :], ssem.at[0])
        spare.start()
        spare.wait()

        def first(r, carry):
            gather_copy(rtok_ref[0, 0, r], r, 0).start()
            return carry
        lax.fori_loop(0, rb, first, 0)

    @pl.when(jnp.logical_or(i == 0, prev_active))
    def _():
        wait_gather(slot)

    @pl.when(active)
    def _():
        e = bexp_ref[i]

        @pl.when(cached[0] != e)
        def _():
            wgb[...] = wg_ref[0].astype(BF16)
            wub[...] = wu_ref[0].astype(BF16)
            wdb[...] = wd_ref[0].astype(BF16)
            cached[0] = e

        for r in range(rb):
            gather_copy(rtokn_ref[0, 0, r], r, other).start()
        x = _load_token_major(xbuf.at[slot], rb).astype(BF16)
        hg = jnp.dot(x, wgb[...], preferred_element_type=F32)
        hu = jnp.dot(x, wub[...], preferred_element_type=F32)
        act = (hg * _sigmoid(hg) * hu).astype(BF16)
        y = jnp.dot(act, wdb[...], preferred_element_type=F32)
        _store_token_major(ybuf.at[slot], y)
        for r in range(rb):
            dst = pl.multiple_of(rslot_ref[0, 0, r], seg)
            pltpu.make_async_copy(ybuf.at[slot, pl.ds(r * seg, seg), :],
                                  y_hbm.at[pl.ds(dst, seg), :], ssem.at[slot]).start(priority=1)

    @pl.when(prev_active)
    def _():
        wait_scatter(other)

    @pl.when(jnp.logical_and(i == nb - 1, active))
    def _():
        wait_scatter(slot)
        wait_gather(other)


def _moe(hn, bexp, bact, rtok, rslot, w_gate, w_up, w_down, n_real):
    d, de = w_gate.shape[1], w_gate.shape[2]
    seg = d // LANE
    nb = bexp.shape[0]
    rb = MOE_ROW_BLOCK
    smem_rows = pl.BlockSpec((1, 1, rb), lambda i, be, ba: (i, 0, 0), memory_space=pltpu.SMEM)
    smem_next = pl.BlockSpec((1, 1, rb), lambda i, be, ba: (jnp.minimum(i + 1, nb - 1), 0, 0),
                             memory_space=pltpu.SMEM)
    grid_spec = pltpu.PrefetchScalarGridSpec(
        num_scalar_prefetch=2,
        grid=(nb,),
        in_specs=[smem_rows, smem_next, smem_rows,
                  pl.BlockSpec(memory_space=pl.ANY),
                  pl.BlockSpec((1, d, de), lambda i, be, ba: (be[i], 0, 0)),
                  pl.BlockSpec((1, d, de), lambda i, be, ba: (be[i], 0, 0)),
                  pl.BlockSpec((1, de, d), lambda i, be, ba: (be[i], 0, 0))],
        out_specs=pl.BlockSpec(memory_space=pl.ANY),
        scratch_shapes=[pltpu.VMEM((2, rb * seg, LANE), F32), pltpu.VMEM((2, rb * seg, LANE), F32),
                        pltpu.VMEM((d, de), BF16), pltpu.VMEM((d, de), BF16), pltpu.VMEM((de, d), BF16),
                        pltpu.SMEM((1,), jnp.int32),
                        pltpu.SemaphoreType.DMA((2,)), pltpu.SemaphoreType.DMA((2,))])
    return pl.pallas_call(
        functools.partial(_moe_kernel, n_real=n_real),
        grid_spec=grid_spec,
        out_shape=jax.ShapeDtypeStruct(((n_real + rb) * seg, LANE), F32),
        compiler_params=_cparams(("arbitrary",)),
        name="moe",
    )(bexp, bact, rtok, rtok, rslot, hn, w_gate, w_up, w_down)


def _dispatch(ids, n, seg):
    k = 2
    m = n * k
    rb = MOE_ROW_BLOCK
    e_flat = ids[:, :k].reshape(m)
    onehot = (e_flat[:, None] == jnp.arange(N_EXPERTS, dtype=jnp.int32)[None, :]).astype(jnp.int32)
    csum = jnp.cumsum(onehot, axis=0)
    counts = csum[-1]
    rank = jnp.take_along_axis(csum, e_flat[:, None], axis=1)[:, 0] - 1
    padded = (counts + rb - 1) // rb * rb
    pad_end = jnp.cumsum(padded)
    pad_start = pad_end - padded
    dest = pad_start[e_flat] + rank
    nb = (m + N_EXPERTS * (rb - 1) + rb - 1) // rb
    p = nb * rb
    row_m = jnp.full((p,), -1, jnp.int32).at[dest].set(jnp.arange(m, dtype=jnp.int32))
    real = row_m >= 0
    rtok = jnp.where(real, row_m // k, 0)
    rslot = jnp.where(real, (row_m % k) * n + row_m // k, m + jnp.arange(p, dtype=jnp.int32) % rb)
    starts = jnp.arange(nb, dtype=jnp.int32) * rb
    bexp = jnp.minimum(jnp.sum((pad_end[None, :] <= starts[:, None]).astype(jnp.int32), axis=1),
                       N_EXPERTS - 1)
    bact = jnp.sum(real.reshape(nb, rb).astype(jnp.int32), axis=1)
    last_e = jnp.max(jnp.where(bact > 0, bexp, 0))
    bexp = jnp.where(bact > 0, bexp, last_e)
    return (bexp, bact, (rtok * seg).reshape(nb, 1, rb), (rslot * seg).reshape(nb, 1, rb),
            m)


def _combine_kernel(h_ref, y0_ref, y1_ref, wts_ref, fw_ref, o_ref):
    tm = h_ref.shape[0]
    wts = wts_ref[...]
    moe = (_load_token_major(y0_ref, tm) * wts[:, 0:1] + _load_token_major(y1_ref, tm) * wts[:, 1:2])
    h = h_ref[...] + moe
    o_ref[...] = h * lax.rsqrt(jnp.mean(h * h, axis=1, keepdims=True) + EPS) * fw_ref[...]


def _combine(h1, y, wts, final_w):
    n, d = h1.shape
    seg = d // LANE
    tm = min(256, n)
    nt = n // tm
    return pl.pallas_call(
        _combine_kernel,
        grid=(nt,),
        in_specs=[pl.BlockSpec((tm, d), lambda i: (i, 0)),
                  pl.BlockSpec((tm * seg, LANE), lambda i: (i, 0)),
                  pl.BlockSpec((tm * seg, LANE), lambda i: (i + nt, 0)),
                  pl.BlockSpec((tm, LANE), lambda i: (i, 0)),
                  pl.BlockSpec((1, d), lambda i: (0, 0))],
        out_specs=pl.BlockSpec((tm, d), lambda i: (i, 0)),
        out_shape=jax.ShapeDtypeStruct((n, d), F32),
        compiler_params=_cparams(("parallel",)),
        name="combine",
    )(h1, y, y, wts, final_w.reshape(1, d))


def _rope_tables(positions):
    half = ROT_DIM // 2
    inv_freq = ROPE_THETA ** (-jnp.arange(0, ROT_DIM, 2, dtype=F32) / ROT_DIM)
    ang = positions.astype(F32)[..., None] * inv_freq
    cos, sin = jnp.cos(ang), jnp.sin(ang)
    b, t = positions.shape
    ones = jnp.ones((b, t, LANE - ROT_DIM), F32)
    zeros = jnp.zeros((b, t, LANE - half), F32)
    c = jnp.concatenate([cos, cos, ones], axis=-1)
    sa = jnp.concatenate([-sin, zeros], axis=-1)
    sb = jnp.concatenate([jnp.zeros((b, t, half), F32), sin, zeros[..., :LANE - ROT_DIM]], axis=-1)
    return c, sa, sb


def _arrange_w_in(w_in):
    d = w_in.shape[0]
    sizes = (N_HEADS_NSA * HEAD_DIM, 3 * 2 * N_KV_NSA * HEAD_DIM, 3 * N_HEADS_NSA,
             3 * N_HEADS_GDN * HEAD_DIM, N_HEADS_GDN, N_HEADS_GDN, N_HEADS_GDN * HEAD_DIM)
    offs = np.cumsum((0,) + sizes)
    seg = [w_in[:, offs[i]:offs[i + 1]] for i in range(len(sizes))]
    q, kv, gate, gqkv, gb, ga, gz = seg
    used = sum(sizes)
    pad = jnp.zeros((d, N_CB * LANE - used), w_in.dtype)
    return jnp.concatenate([q, kv, gqkv, gz, gate, gb, ga, pad], axis=1).astype(BF16)


def kernel(x, positions, attn_norm_w, w_in, cmp_wk, cmp_pek, cmp_wv, cmp_pev, gdn_conv_w, gdn_a_log,
           gdn_dt_bias, gdn_norm_w, w_out, ffn_norm_w, router_group_w, router_group_b,
           router_expert_w, router_expert_b, moe_w_gate, moe_w_up, moe_w_down, final_norm_w):
    b, t, d = x.shape
    n = b * t
    tabs = _rope_tables(positions)
    h = x.reshape(n, d)
    assert w_in.shape[0] == 1, "single-layer block only"
    for l in range(1):
        proj, small = _in_proj(h, attn_norm_w[l], _arrange_w_in(w_in[l]))
        proj3 = proj.reshape(b, t, N_CB * LANE)
        small3 = small.reshape(b, t, LANE)
        cmp_w = jnp.stack([cmp_wk[l], cmp_wv[l]])
        cmp_pe = jnp.stack([cmp_pek[l], cmp_pev[l]])
        kvc = _nsa_compress(proj3, tabs, cmp_w, cmp_pe)
        o_c, selbias = _nsa_cmp_attn(proj3, small3, tabs, kvc)
        o_s = _nsa_sel_attn(proj3, small3, tabs, selbias)
        o_w = _nsa_win_attn(proj3, small3, tabs)
        o_b = _gdn(proj3, small3, gdn_conv_w[l], gdn_a_log[l], gdn_dt_bias[l], gdn_norm_w[l])
        half = N_HEADS_NSA * HEAD_DIM
        wr = jnp.concatenate([router_group_w[l], router_expert_w[l],
                              jnp.zeros((d, LANE - N_GROUPS - N_EXPERTS), F32)], axis=1)
        br = jnp.concatenate([router_group_b[l], router_expert_b[l],
                              jnp.zeros((LANE - N_GROUPS - N_EXPERTS,), F32)]).reshape(1, LANE)
        h1, hn2, ids, wts = _out_proj(o_c.reshape(n, half), o_s.reshape(n, half), o_w.reshape(n, half),
                                      o_b.reshape(n, half), h, w_out[l].astype(BF16), ffn_norm_w[l], wr, br)
        bexp, bact, rtok, rslot, n_slots = _dispatch(ids, n, d // LANE)
        y = _moe(hn2, bexp, bact, rtok, rslot, moe_w_gate[l], moe_w_up[l], moe_w_down[l], n_slots)
        out = _combine(h1, y, wts, final_norm_w)
    return out.reshape(b, t, d)
```

```python
import functools

import numpy as np
import jax
import jax.numpy as jnp
from jax import lax
from jax.experimental import pallas as pl
from jax.experimental.pallas import tpu as pltpu

F32 = jnp.float32
BF16 = jnp.bfloat16

HEAD_DIM = 128
N_HEADS_NSA = 8
N_KV_NSA = 2
N_REP = N_HEADS_NSA // N_KV_NSA
N_HEADS_GDN = 8
ROT_DIM = 32
ROPE_THETA = 500000.0
CMP_LEN = 32
CMP_STRIDE = 16
SLC_LEN = 64
SLC_TOP = 16
WINDOW = 512
CONV_WIDTH = 4
GDN_CHUNK = 64
N_GROUPS = 8
EXPERTS_PER_GROUP = 8
N_EXPERTS = 64
MOE_ROW_BLOCK = 256
EPS = 1e-6
LANE = 128

CB_Q = 0
CB_KV = 8
CB_GQKV = 20
CB_Z = 44
CB_SMALL = 52
N_CB = 54
SMALL_BETA = 24
SMALL_DECAY = 32

NEG_BIAS = -32768.0
VMEM_LIMIT = 56 * 1024 * 1024


def _cparams(sem):
    return pltpu.CompilerParams(dimension_semantics=sem, vmem_limit_bytes=VMEM_LIMIT)


def _mm(a, b):
    return jnp.dot(a.astype(BF16), b.astype(BF16), preferred_element_type=F32)


def _mm_nt(a, b):
    return lax.dot_general(a.astype(BF16), b.astype(BF16), (((1,), (1,)), ((), ())),
                           preferred_element_type=F32)


def _rope(x, c, sa, sb):
    return (x * c + pltpu.roll(x, LANE - ROT_DIM // 2, 1) * sa
            + pltpu.roll(x, ROT_DIM // 2, 1) * sb)


def _sigmoid(x):
    return 0.5 * jnp.tanh(0.5 * x) + 0.5


def _lane_tiles(x):
    return [x[:, i:i + LANE] for i in range(0, x.shape[1], LANE)]


def _store_token_major(ref, x):
    rows, d = x.shape
    seg = d // LANE
    for s in range(seg):
        ref[pl.ds(s, rows, stride=seg), :] = x[:, s * LANE:(s + 1) * LANE]


def _load_token_major(ref, rows):
    seg = ref.shape[0] // rows
    return jnp.concatenate([ref[pl.ds(s, rows, stride=seg), :] for s in range(seg)], axis=1)


def _rowmax(x):
    return jnp.max(functools.reduce(jnp.maximum, _lane_tiles(x)), axis=1, keepdims=True)


def _rowsum(x):
    return jnp.sum(functools.reduce(jnp.add, _lane_tiles(x)), axis=1, keepdims=True)


def _in_proj_kernel(x_ref, nw_ref, w_ref, o_ref, small_ref, hn_ref, *, small_off):
    j = pl.program_id(1)

    @pl.when(j == 0)
    def _():
        x = x_ref[...]
        ms = jnp.mean(x * x, axis=-1, keepdims=True)
        hn_ref[...] = (x * lax.rsqrt(ms + EPS) * nw_ref[...]).astype(BF16)
    acc = jnp.dot(hn_ref[...], w_ref[...], preferred_element_type=F32)
    o_ref[...] = acc.astype(o_ref.dtype)

    @pl.when(j == pl.num_programs(1) - 1)
    def _():
        small_ref[...] = acc[:, small_off:small_off + LANE]


def _in_proj(x2, norm_w, w):
    n, d = x2.shape
    ncol = w.shape[1]
    tm = min(1024, n)
    tn = 768
    small_off = CB_SMALL * LANE - (ncol // tn - 1) * tn
    assert 0 <= small_off <= tn - LANE
    return pl.pallas_call(
        functools.partial(_in_proj_kernel, small_off=small_off),
        grid=(n // tm, ncol // tn),
        in_specs=[pl.BlockSpec((tm, d), lambda i, j: (i, 0)),
                  pl.BlockSpec((1, d), lambda i, j: (0, 0)),
                  pl.BlockSpec((d, tn), lambda i, j: (0, j))],
        out_specs=[pl.BlockSpec((tm, tn), lambda i, j: (i, j)),
                   pl.BlockSpec((tm, LANE), lambda i, j: (i, 0))],
        out_shape=[jax.ShapeDtypeStruct((n, ncol), BF16), jax.ShapeDtypeStruct((n, LANE), F32)],
        scratch_shapes=[pltpu.VMEM((tm, d), BF16)],
        compiler_params=_cparams(("parallel", "arbitrary")),
        name="in_proj",
    )(x2, norm_w.reshape(1, d), w)


def _compress_kernel(a_ref, c_ref, sa_ref, sb_ref, w_ref, pe_ref, o_ref, xs_ref):
    kv = pl.program_id(1)
    t = a_ref.shape[1]
    nc = t // CMP_STRIDE
    x = a_ref[0].astype(F32)
    xr = _rope(x, c_ref[0], sa_ref[0], sb_ref[0])
    x = jnp.where(kv == 0, xr, x)
    xs_ref[pl.ds(0, t), :] = x
    xs_ref[pl.ds(t, CMP_STRIDE), :] = jnp.zeros((CMP_STRIDE, LANE), F32)
    acc = jnp.zeros((nc, LANE), F32)
    for l in range(CMP_LEN):
        rows = xs_ref[pl.ds(l, nc, stride=CMP_STRIDE), :] + pe_ref[0, pl.ds(l, 1), :]
        acc = acc + _mm(rows, w_ref[0, l])
    o_ref[0, 0, 0] = acc.astype(BF16)


def _nsa_compress(proj3, tabs, cmp_w, cmp_pe):
    b, t, _ = proj3.shape
    g = N_KV_NSA
    nc = t // CMP_STRIDE
    tab_spec = pl.BlockSpec((1, t, LANE), lambda bi, kv, gi: (bi, 0, 0))
    return pl.pallas_call(
        _compress_kernel,
        grid=(b, 2, g),
        in_specs=[pl.BlockSpec((1, t, LANE), lambda bi, kv, gi: (bi, 0, CB_KV + kv * g + gi)),
                  tab_spec, tab_spec, tab_spec,
                  pl.BlockSpec((1, CMP_LEN, LANE, LANE), lambda bi, kv, gi: (kv, 0, 0, 0)),
                  pl.BlockSpec((1, CMP_LEN, LANE), lambda bi, kv, gi: (kv, 0, 0))],
        out_specs=pl.BlockSpec((1, 1, 1, nc, LANE), lambda bi, kv, gi: (bi, kv, gi, 0, 0)),
        out_shape=jax.ShapeDtypeStruct((b, 2, g, nc, LANE), BF16),
        scratch_shapes=[pltpu.VMEM((t + CMP_STRIDE, LANE), F32)],
        compiler_params=_cparams(("parallel", "arbitrary", "arbitrary")),
        name="nsa_compress",
    )(proj3, *tabs, cmp_w, cmp_pe)


def _load_q(q_ref, c, sa, sb):
    scale = HEAD_DIM ** -0.5
    qs = [(_rope(q_ref[0, :, r * LANE:(r + 1) * LANE].astype(F32), c, sa, sb) * scale).astype(BF16)
          for r in range(N_REP)]
    return jnp.concatenate(qs, axis=0)


def _store_gated(o_ref, o, gate_ref, g, branch, tq):
    gt = gate_ref[0]
    for r in range(N_REP):
        col = ((g * N_REP + r) * 3 + branch)
        lane = lax.broadcasted_iota(jnp.int32, gt.shape, 1)
        gcol = jnp.sum(jnp.where(lane == col, gt, 0.0), axis=1, keepdims=True)
        o_ref[0, :, r * LANE:(r + 1) * LANE] = (o[r * tq:(r + 1) * tq] * _sigmoid(gcol)).astype(o_ref.dtype)


def _cmp_attn_kernel(q_ref, c_ref, sa_ref, sb_ref, kc_ref, vc_ref, gate_ref, o_ref, sel_ref, *, n_slc):
    g = pl.program_id(1)
    i = pl.program_id(2)
    tq = q_ref.shape[1]
    nc = kc_ref.shape[3]
    q4 = _load_q(q_ref, c_ref[0], sa_ref[0], sb_ref[0])
    s = _mm_nt(q4, kc_ref[0, 0, 0])
    row = lax.broadcasted_iota(jnp.int32, s.shape, 0)
    n = lax.broadcasted_iota(jnp.int32, s.shape, 1)
    tpos = i * tq + (row & (tq - 1))
    mask = (n * CMP_STRIDE + (CMP_LEN - 1) <= tpos) & (n < nc - 1)
    sm = jnp.where(mask, s, -1e30)
    m = _rowmax(sm)
    p = jnp.where(mask, jnp.exp(sm - m), 0.0)
    l = _rowsum(p)
    p = p / jnp.maximum(l, 1e-30)
    o = _mm(p, vc_ref[0, 0, 0])
    _store_gated(o_ref, o, gate_ref, g, 0, tq)

    ps = p[0:tq]
    for r in range(1, N_REP):
        ps = ps + p[r * tq:(r + 1) * tq]
    cn = lax.broadcasted_iota(jnp.int32, (nc, LANE), 0)
    cj = lax.broadcasted_iota(jnp.int32, (nc, LANE), 1)
    ratio = SLC_LEN // CMP_STRIDE
    agg = ((cn >= ratio * cj - (CMP_LEN // CMP_STRIDE - 1)) & (cn < ratio * cj + ratio)
           & (cn < nc - 1) & (cj < n_slc))
    agg = jnp.where(agg, 1.0, 0.0).astype(BF16)
    ps_hi = ps.astype(BF16)
    ps_lo = (ps - ps_hi.astype(F32)).astype(BF16)
    imp = (jnp.dot(ps_hi, agg, preferred_element_type=F32)
           + jnp.dot(ps_lo, agg, preferred_element_type=F32))

    j = lax.broadcasted_iota(jnp.int32, (tq, LANE), 1)
    tt = i * tq + lax.broadcasted_iota(jnp.int32, (tq, LANE), 0)
    cur = tt // SLC_LEN
    valid = j <= cur
    forced = (j == 0) | (j == cur) | (j == cur - 1)
    vals = jnp.where(forced, 1e30, jnp.where(valid, imp, -1.0))
    vt = vals.T
    jb = lax.broadcasted_iota(jnp.int32, (n_slc, tq), 0)
    vb = vt[0:n_slc]
    cnt = jnp.zeros((n_slc, tq), F32)
    for jp in range(n_slc):
        cand = vt[jp:jp + 1, :]
        ge = jnp.where(cand >= vb, 1.0, 0.0)
        gt = jnp.where(cand > vb, 1.0, 0.0)
        cnt = cnt + jnp.where(jb > jp, ge, gt)
    keep = jnp.where(cnt < float(min(SLC_TOP, n_slc)), 0.0, NEG_BIAS)
    if n_slc < LANE:
        keep = jnp.concatenate([keep, jnp.full((LANE - n_slc, tq), NEG_BIAS, F32)], axis=0)
    sel_ref[0, 0] = jnp.where(valid, keep.T, NEG_BIAS).astype(BF16)


def _nsa_cmp_attn(proj3, small3, tabs, kvc, tq=128):
    b, t, _ = proj3.shape
    g = N_KV_NSA
    nc = t // CMP_STRIDE
    n_slc = t // SLC_LEN
    rw = N_REP * LANE
    tab_spec = pl.BlockSpec((1, tq, LANE), lambda bi, gi, i: (bi, i, 0))
    return pl.pallas_call(
        functools.partial(_cmp_attn_kernel, n_slc=n_slc),
        grid=(b, g, t // tq),
        in_specs=[pl.BlockSpec((1, tq, rw), lambda bi, gi, i: (bi, i, gi)),
                  tab_spec, tab_spec, tab_spec,
                  pl.BlockSpec((1, 1, 1, nc, LANE), lambda bi, gi, i: (bi, 0, gi, 0, 0)),
                  pl.BlockSpec((1, 1, 1, nc, LANE), lambda bi, gi, i: (bi, 1, gi, 0, 0)),
                  pl.BlockSpec((1, tq, LANE), lambda bi, gi, i: (bi, i, 0))],
        out_specs=[pl.BlockSpec((1, tq, rw), lambda bi, gi, i: (bi, i, gi)),
                   pl.BlockSpec((1, 1, tq, LANE), lambda bi, gi, i: (bi, gi, i, 0))],
        out_shape=[jax.ShapeDtypeStruct((b, t, N_HEADS_NSA * LANE), BF16),
                   jax.ShapeDtypeStruct((b, g, t, LANE), BF16)],
        compiler_params=_cparams(("parallel", "parallel", "parallel")),
        name="nsa_cmp",
    )(proj3, *tabs, kvc, kvc, small3)


def _sel_attn_kernel(q_ref, c_ref, sa_ref, sb_ref, cf_ref, saf_ref, sbf_ref, k_ref, v_ref,
                     sel_ref, gate_ref, o_ref, ka_ref, vs_ref, m_ref, l_ref, acc_ref, s0_ref, s1_ref,
                     *, tk):
    g = pl.program_id(1)
    i = pl.program_id(2)
    tq = q_ref.shape[1]
    t = k_ref.shape[1]
    rows = N_REP * tq

    @pl.when(i == 0)
    def _():
        kr = _rope(k_ref[0].astype(F32), cf_ref[0], saf_ref[0], sbf_ref[0])
        ka_ref[:, 0:LANE] = kr.astype(BF16)
        pos = lax.broadcasted_iota(jnp.int32, (t, LANE), 0)
        lane = lax.broadcasted_iota(jnp.int32, (t, LANE), 1)
        ka_ref[:, LANE:2 * LANE] = jnp.where(lane == pos // SLC_LEN, 1.0, 0.0).astype(BF16)
        vs_ref[...] = v_ref[0].astype(BF16)

    q4 = _load_q(q_ref, c_ref[0], sa_ref[0], sb_ref[0])
    bias = sel_ref[0, 0]
    qa = jnp.concatenate([q4, jnp.concatenate([bias] * N_REP, axis=0)], axis=1)

    m_ref[...] = jnp.full((rows, LANE), -1e30, F32)
    l_ref[...] = jnp.zeros((rows, LANE), F32)
    acc_ref[...] = jnp.zeros((rows, LANE), F32)

    half = tk // 2

    def scores(kt, hlf):
        k0 = pl.multiple_of(kt * tk + hlf * half, half)
        return _mm_nt(qa, ka_ref[pl.ds(k0, half), :])

    def update(s_ref, kt, hlf, causal):
        k0 = pl.multiple_of(kt * tk + hlf * half, half)
        s = s_ref[...]
        if causal:
            r = lax.broadcasted_iota(jnp.int32, s.shape, 0)
            kp = k0 + lax.broadcasted_iota(jnp.int32, s.shape, 1)
            s = jnp.where(kp <= i * tq + (r & (tq - 1)), s, -1e30)
        tiles = _lane_tiles(s)
        m_old = m_ref[...]
        m_new = jnp.maximum(m_old, _rowmax(s))
        alpha = jnp.exp(m_old - m_new)
        ps = [jnp.exp(tl - m_new) for tl in tiles]
        l_ref[...] = alpha * l_ref[...] + jnp.sum(functools.reduce(jnp.add, ps), axis=1, keepdims=True)
        p = jnp.concatenate([x.astype(BF16) for x in ps], axis=1)
        acc_ref[...] = alpha * acc_ref[...] + jnp.dot(p, vs_ref[pl.ds(k0, half), :],
                                                      preferred_element_type=F32)
        m_ref[...] = m_new

    diag = (i * tq) // tk
    s0_ref[...] = scores(0, 0)

    def body(kt, carry):
        s1_ref[...] = scores(kt, 1)
        update(s0_ref, kt, 0, False)
        s0_ref[...] = scores(kt + 1, 0)
        update(s1_ref, kt, 1, False)
        return carry

    lax.fori_loop(0, diag, body, 0)
    update(s0_ref, diag, 0, True)

    @pl.when(i * tq - diag * tk >= half)
    def _():
        s1_ref[...] = scores(diag, 1)
        update(s1_ref, diag, 1, True)

    o = acc_ref[...] / l_ref[...]
    _store_gated(o_ref, o, gate_ref, g, 1, tq)


def _nsa_sel_attn(proj3, small3, tabs, selbias, tq=128, tk=512):
    b, t, _ = proj3.shape
    g = N_KV_NSA
    rw = N_REP * LANE
    tk = min(tk, t)
    tab_spec = pl.BlockSpec((1, tq, LANE), lambda bi, gi, i: (bi, i, 0))
    tabf_spec = pl.BlockSpec((1, t, LANE), lambda bi, gi, i: (bi, 0, 0))
    kcb = CB_KV + 1 * 2 * g
    return pl.pallas_call(
        functools.partial(_sel_attn_kernel, tk=tk),
        grid=(b, g, t // tq),
        in_specs=[pl.BlockSpec((1, tq, rw), lambda bi, gi, i: (bi, i, gi)),
                  tab_spec, tab_spec, tab_spec, tabf_spec, tabf_spec, tabf_spec,
                  pl.BlockSpec((1, t, LANE), lambda bi, gi, i: (bi, 0, kcb + gi)),
                  pl.BlockSpec((1, t, LANE), lambda bi, gi, i: (bi, 0, kcb + g + gi)),
                  pl.BlockSpec((1, 1, tq, LANE), lambda bi, gi, i: (bi, gi, i, 0)),
                  pl.BlockSpec((1, tq, LANE), lambda bi, gi, i: (bi, i, 0))],
        out_specs=pl.BlockSpec((1, tq, rw), lambda bi, gi, i: (bi, i, gi)),
        out_shape=jax.ShapeDtypeStruct((b, t, N_HEADS_NSA * LANE), BF16),
        scratch_shapes=[pltpu.VMEM((t, 2 * LANE), BF16), pltpu.VMEM((t, LANE), BF16),
                        pltpu.VMEM((N_REP * tq, LANE), F32), pltpu.VMEM((N_REP * tq, LANE), F32),
                        pltpu.VMEM((N_REP * tq, LANE), F32),
                        pltpu.VMEM((N_REP * tq, tk // 2), F32), pltpu.VMEM((N_REP * tq, tk // 2), F32)],
        compiler_params=_cparams(("parallel", "parallel", "arbitrary")),
        name="nsa_sel",
    )(proj3, *tabs, *tabs, proj3, proj3, selbias, small3)


def _win_attn_kernel(q_ref, c_ref, sa_ref, sb_ref, cf_ref, saf_ref, sbf_ref, k_ref, v_ref,
                     gate_ref, o_ref, ks_ref, vs_ref, *, span):
    g = pl.program_id(1)
    i = pl.program_id(2)
    tq = q_ref.shape[1]

    t = k_ref.shape[1]

    @pl.when(i == 0)
    def _():
        zeros = jnp.zeros((WINDOW, LANE), BF16)
        ks_ref[pl.ds(0, WINDOW), :] = zeros
        vs_ref[pl.ds(0, WINDOW), :] = zeros
        ks_ref[pl.ds(WINDOW, t), :] = _rope(k_ref[0].astype(F32), cf_ref[0], saf_ref[0],
                                            sbf_ref[0]).astype(BF16)
        vs_ref[pl.ds(WINDOW, t), :] = v_ref[0].astype(BF16)

    q4 = _load_q(q_ref, c_ref[0], sa_ref[0], sb_ref[0])
    k0 = pl.multiple_of(i * tq, tq)
    s = _mm_nt(q4, ks_ref[pl.ds(k0, span), :])
    tiles = _lane_tiles(s)
    r = lax.broadcasted_iota(jnp.int32, tiles[0].shape, 0) & (tq - 1)
    c = lax.broadcasted_iota(jnp.int32, tiles[0].shape, 1)
    first_block = WINDOW // tq - i
    masked = []
    for b, tl in enumerate(tiles):
        if b == 0:
            tl = jnp.where(c > r, tl, -1e30)
        if b == len(tiles) - 1:
            tl = jnp.where(c <= r, tl, -1e30)
        else:
            tl = jnp.where(b >= first_block, tl, -1e30)
        masked.append(tl)
    m = jnp.max(functools.reduce(jnp.maximum, masked), axis=1, keepdims=True)
    ps = [jnp.exp(tl - m) for tl in masked]
    l = jnp.sum(functools.reduce(jnp.add, ps), axis=1, keepdims=True)
    p = jnp.concatenate([x.astype(BF16) for x in ps], axis=1)
    o = jnp.dot(p, vs_ref[pl.ds(k0, span), :], preferred_element_type=F32) / l
    _store_gated(o_ref, o, gate_ref, g, 2, tq)


def _nsa_win_attn(proj3, small3, tabs, tq=128):
    b, t, _ = proj3.shape
    g = N_KV_NSA
    rw = N_REP * LANE
    span = WINDOW + tq
    assert tq == LANE
    tab_spec = pl.BlockSpec((1, tq, LANE), lambda bi, gi, i: (bi, i, 0))
    tabf_spec = pl.BlockSpec((1, t, LANE), lambda bi, gi, i: (bi, 0, 0))
    kcb = CB_KV + 2 * 2 * g
    return pl.pallas_call(
        functools.partial(_win_attn_kernel, span=span),
        grid=(b, g, t // tq),
        in_specs=[pl.BlockSpec((1, tq, rw), lambda bi, gi, i: (bi, i, gi)),
                  tab_spec, tab_spec, tab_spec, tabf_spec, tabf_spec, tabf_spec,
                  pl.BlockSpec((1, t, LANE), lambda bi, gi, i: (bi, 0, kcb + gi)),
                  pl.BlockSpec((1, t, LANE), lambda bi, gi, i: (bi, 0, kcb + g + gi)),
                  pl.BlockSpec((1, tq, LANE), lambda bi, gi, i: (bi, i, 0))],
        out_specs=pl.BlockSpec((1, tq, rw), lambda bi, gi, i: (bi, i, gi)),
        out_shape=jax.ShapeDtypeStruct((b, t, N_HEADS_NSA * LANE), BF16),
        scratch_shapes=[pltpu.VMEM((t + WINDOW, LANE), BF16), pltpu.VMEM((t + WINDOW, LANE), BF16)],
        compiler_params=_cparams(("parallel", "parallel", "arbitrary")),
        name="nsa_win",
    )(proj3, *tabs, *tabs, proj3, proj3, small3)


def _gdn_kernel(q_ref, k_ref, v_ref, z_ref, braw_ref, araw_ref, cwq_ref, cwk_ref, cwv_ref, alog_ref,
                dtb_ref, nw_ref, o_ref, q_s, k_s, v_s, o_s, qp_s, op_s, xp_s, bd_s, gcd_s, gl_s, mm_s,
                nn_s):
    t = q_ref.shape[1]
    c = GDN_CHUNK
    nchunk = t // c
    xp_s[pl.ds(0, 8), :] = jnp.zeros((8, LANE), F32)

    def conv_silu(x, cw):
        xp_s[pl.ds(8, t), :] = x
        y = x * cw[CONV_WIDTH - 1:CONV_WIDTH]
        for sft in range(1, CONV_WIDTH):
            y = y + xp_s[pl.ds(8 - sft, t), :] * cw[CONV_WIDTH - 1 - sft:CONV_WIDTH - sft]
        return y * _sigmoid(y)

    def l2n(x):
        return x * lax.rsqrt(jnp.sum(x * x, axis=1, keepdims=True) + EPS)

    q_s[...] = l2n(conv_silu(q_ref[0].astype(F32), cwq_ref[...])) * (HEAD_DIM ** -0.5)
    k_s[...] = l2n(conv_silu(k_ref[0].astype(F32), cwk_ref[...]))
    v_s[...] = conv_silu(v_ref[0].astype(F32), cwv_ref[...])

    bd_s[...] = _sigmoid(braw_ref[0, 0])
    xa = araw_ref[0, 0] + dtb_ref[0]
    softplus = jnp.maximum(xa, 0.0) + jnp.log(1.0 + jnp.exp(-jnp.abs(xa)))
    gc = -jnp.exp(alog_ref[0]) * softplus
    lane = lax.broadcasted_iota(jnp.int32, gc.shape, 1)
    sft = 1
    while sft < c:
        gc = gc + jnp.where((lane & (c - 1)) >= sft, pltpu.roll(gc, sft, 1), 0.0)
        sft *= 2
    gcd_s[...] = gc

    c2 = 2 * c
    ci = lax.broadcasted_iota(jnp.int32, (c2, c2), 0)
    cj = lax.broadcasted_iota(jnp.int32, (c2, c2), 1)
    same = (ci // c) == (cj // c)
    tril = same & (cj <= ci)
    strict = same & (cj < ci)
    eye = jnp.where(ci == cj, 1.0, 0.0)
    first = ci < c

    pairs_per_iter = 8
    rng = range(pairs_per_iter)

    def prep(it, carry):
        n2 = [it * pairs_per_iter + p for p in rng]
        sl = [pl.ds(pl.multiple_of(n * c2, c2), c2) for n in n2]
        grow = [gcd_s[pl.ds(n, 1), :] for n in n2]
        gc2 = [jnp.broadcast_to(g_, (c2, c2)).T for g_ in grow]
        betac = [jnp.broadcast_to(bd_s[pl.ds(n, 1), :], (c2, c2)).T for n in n2]
        kn = [k_s[s_, :] for s_ in sl]
        kbn = [kn[p] * betac[p] for p in rng]
        decay = [jnp.exp(jnp.where(tril, gc2[p] - grow[p], -1e30)) for p in rng]
        kk = [_mm_nt(kbn[p], kn[p]) for p in rng]
        y = [-jnp.where(strict, kk[p] * decay[p], 0.0) for p in rng]
        pm = [eye + y_ for y_ in y]
        pw = 1
        while 2 * pw < c:
            y = [_mm(y_, y_) for y_ in y]
            pm = [pm[p] + _mm(pm[p], y[p]) for p in rng]
            pw *= 2
        egn = [jnp.exp(g_) for g_ in gc2]
        u = [_mm(pm[p], v_s[sl[p], :] * betac[p]) for p in rng]
        w = [_mm(pm[p], kbn[p] * egn[p]) for p in rng]
        qn = [q_s[s_, :] for s_ in sl]
        qk = [_mm_nt(qn[p], kn[p]) for p in rng]
        attn = [jnp.where(tril, qk[p] * decay[p], 0.0) for p in rng]
        aw = [_mm(attn[p], w[p]) for p in rng]
        au = [_mm(attn[p], u[p]) for p in rng]
        for p in rng:
            qp_s[sl[p], :] = qn[p] * egn[p] - aw[p]
            op_s[sl[p], :] = au[p]
        gl = [jnp.where(first, g_[c - 1:c, :], g_[c2 - 1:c2, :]) for g_ in gc2]
        kdt = [(kn[p] * jnp.exp(gl[p] - gc2[p])).T for p in rng]
        ma = [_mm(kdt[p][:, 0:c], w[p][0:c]) for p in rng]
        mb = [_mm(kdt[p][:, c:c2], w[p][c:c2]) for p in rng]
        na = [_mm(kdt[p][:, 0:c], u[p][0:c]) for p in rng]
        nb_ = [_mm(kdt[p][:, c:c2], u[p][c:c2]) for p in rng]
        mba = [_mm(mb[p], ma[p]) for p in rng]
        mbn = [_mm(mb[p], na[p]) for p in rng]
        for p in rng:
            ga = jnp.exp(gc2[p][c - 1:c, :])
            gb = jnp.exp(gc2[p][c2 - 1:c2, :])
            mm_s[2 * n2[p]] = ma[p].astype(BF16)
            nn_s[2 * n2[p]] = na[p]
            gl_s[pl.ds(2 * n2[p], 1), :] = ga
            mm_s[2 * n2[p] + 1] = (gb * ma[p] + ga * mb[p] - mba[p]).astype(BF16)
            nn_s[2 * n2[p] + 1] = gb * na[p] - mbn[p] + nb_[p]
            gl_s[pl.ds(2 * n2[p] + 1, 1), :] = ga * gb
        return carry

    lax.fori_loop(0, nchunk // (2 * pairs_per_iter), prep, 0)

    def second_out(n2, s_mid):
        rows = pl.ds(pl.multiple_of(n2 * c2, c2) + c, c)
        o_s[rows, :] = (jnp.dot(qp_s[rows, :].astype(BF16), s_mid.astype(BF16),
                                 preferred_element_type=F32) + op_s[rows, :])

    def scan(n2, carry):
        s, s_mid_prev = carry
        second_out(jnp.maximum(n2 - 1, 0), s_mid_prev)
        ra = pl.ds(pl.multiple_of(n2 * c2, c2), c)
        sb = s.astype(BF16)
        s_mid = (s * gl_s[pl.ds(2 * n2, 1), :] - jnp.dot(mm_s[2 * n2], sb, preferred_element_type=F32)
                 + nn_s[2 * n2])
        s_new = (s * gl_s[pl.ds(2 * n2 + 1, 1), :]
                 - jnp.dot(mm_s[2 * n2 + 1], sb, preferred_element_type=F32) + nn_s[2 * n2 + 1])
        o_s[ra, :] = jnp.dot(qp_s[ra, :].astype(BF16), sb, preferred_element_type=F32) + op_s[ra, :]
        return s_new, s_mid

    zero_state = jnp.zeros((LANE, LANE), F32)
    _, s_mid_last = lax.fori_loop(0, nchunk // 2, scan, (zero_state, zero_state))
    second_out(jnp.int32(nchunk // 2 - 1), s_mid_last)

    o = o_s[...]
    on = o * lax.rsqrt(jnp.mean(o * o, axis=1, keepdims=True) + EPS) * nw_ref[...]
    z = z_ref[0].astype(F32)
    o_ref[0] = (on * (z * _sigmoid(z))).astype(o_ref.dtype)


def _gdn(proj3, small3, conv_w, a_log, dt_bias, norm_w):
    b, t, _ = proj3.shape
    hh = N_HEADS_GDN
    c = GDN_CHUNK
    col = lambda cb: pl.BlockSpec((1, t, LANE), lambda bi, hi: (bi, 0, cb + hi))
    cw = lambda off: pl.BlockSpec((CONV_WIDTH, LANE), lambda bi, hi: (0, off + hi))
    hrow = pl.BlockSpec((1, 1, LANE), lambda bi, hi: (hi, 0, 0))
    alog_b = jnp.broadcast_to(a_log.astype(F32)[:, None, None], (hh, 1, LANE))
    dtb_b = jnp.broadcast_to(dt_bias.astype(F32)[:, None, None], (hh, 1, LANE))
    big = pltpu.VMEM((t, LANE), F32)
    nrow = t // LANE
    ba = small3[:, :, SMALL_BETA:SMALL_BETA + 2 * hh].transpose(0, 2, 1).reshape(b, 2 * hh, nrow, LANE)
    dense = lambda off: pl.BlockSpec((1, 1, nrow, LANE), lambda bi, hi: (bi, off + hi, 0, 0))
    return pl.pallas_call(
        _gdn_kernel,
        grid=(b, hh),
        in_specs=[col(CB_GQKV), col(CB_GQKV + hh), col(CB_GQKV + 2 * hh), col(CB_Z),
                  dense(0), dense(hh),
                  cw(0), cw(hh), cw(2 * hh), hrow, hrow,
                  pl.BlockSpec((1, LANE), lambda bi, hi: (0, 0))],
        out_specs=pl.BlockSpec((1, t, LANE), lambda bi, hi: (bi, 0, hi)),
        out_shape=jax.ShapeDtypeStruct((b, t, hh * LANE), BF16),
        scratch_shapes=[big, big, big, big, big, big, pltpu.VMEM((t + 8, LANE), F32),
                        pltpu.VMEM((nrow, LANE), F32), pltpu.VMEM((nrow, LANE), F32),
                        pltpu.VMEM((t // c, LANE), F32),
                        pltpu.VMEM((t // c, LANE, LANE), BF16),
                        pltpu.VMEM((t // c, LANE, LANE), F32)],
        compiler_params=_cparams(("parallel", "parallel")),
        name="gdn",
    )(proj3, proj3, proj3, proj3, ba, ba, conv_w, conv_w, conv_w, alog_b, dtb_b,
      norm_w.reshape(1, LANE))


def _split3(a):
    hi = a.astype(BF16)
    lo = (a - hi.astype(F32)).astype(BF16)
    return hi, lo


def _out_proj_kernel(oc_ref, os_ref, ow_ref, ob_ref, x_ref, wo_ref, fw_ref, wr_ref, br_ref,
                     h_ref, hn_ref, ids_ref, wts_ref):
    half = oc_ref.shape[1]
    oa = (oc_ref[...].astype(F32) + os_ref[...].astype(F32) + ow_ref[...].astype(F32)).astype(BF16)
    h1 = (x_ref[...] + jnp.dot(oa, wo_ref[0:half, :], preferred_element_type=F32)
          + jnp.dot(ob_ref[...], wo_ref[half:2 * half, :], preferred_element_type=F32))
    h_ref[...] = h1
    hn = h1 * lax.rsqrt(jnp.mean(h1 * h1, axis=1, keepdims=True) + EPS) * fw_ref[...]
    _store_token_major(hn_ref, hn)

    a_hi, a_lo = _split3(hn)
    w_hi, w_lo = _split3(wr_ref[...])
    dot = lambda a, b: jnp.dot(a, b, preferred_element_type=F32)
    logits = dot(a_hi, w_hi) + dot(a_hi, w_lo) + dot(a_lo, w_hi) + br_ref[...]
    lane = lax.broadcasted_iota(jnp.int32, logits.shape, 1)
    big = 1e30
    is_g = lane < N_GROUPS
    lg = jnp.where(is_g, logits, -big)
    gm = jnp.max(lg, axis=1, keepdims=True)
    grp = jnp.min(jnp.where(lg == gm, lane, LANE), axis=1, keepdims=True)
    p_grp = 1.0 / jnp.sum(jnp.where(is_g, jnp.exp(lg - gm), 0.0), axis=1, keepdims=True)
    e_id = lane - N_GROUPS
    in_g = (e_id >= 0) & (e_id < N_EXPERTS) & ((e_id // EXPERTS_PER_GROUP) == grp)
    le = jnp.where(in_g, logits, -big)
    em = jnp.max(le, axis=1, keepdims=True)
    pe = jnp.where(in_g, jnp.exp(le - em), 0.0)
    pe = pe / jnp.sum(pe, axis=1, keepdims=True)
    pm = jnp.where(in_g, pe, -1.0)
    p1 = jnp.max(pm, axis=1, keepdims=True)
    i1 = jnp.min(jnp.where(pm == p1, lane, LANE), axis=1, keepdims=True)
    pm2 = jnp.where(lane == i1, -1.0, pm)
    p2 = jnp.max(pm2, axis=1, keepdims=True)
    i2 = jnp.min(jnp.where(pm2 == p2, lane, LANE), axis=1, keepdims=True)
    den = p1 + p2
    ids_ref[...] = jnp.where(lane == 0, i1 - N_GROUPS, jnp.where(lane == 1, i2 - N_GROUPS, 0))
    wts_ref[...] = jnp.where(lane == 0, p1 / den * p_grp, jnp.where(lane == 1, p2 / den * p_grp, 0.0))


def _out_proj(oc, os_, ow, ob, x2, w_out, ffn_w, wr, br):
    n, d = x2.shape
    half = oc.shape[1]
    tm = min(256, n)
    row = lambda w: pl.BlockSpec((tm, w), lambda i: (i, 0))
    full = lambda a: pl.BlockSpec(a.shape, lambda i: (0,) * a.ndim)
    fw = ffn_w.reshape(1, d)
    return pl.pallas_call(
        _out_proj_kernel,
        grid=(n // tm,),
        in_specs=[row(half), row(half), row(half), row(half), row(d), full(w_out), full(fw),
                  full(wr), full(br)],
        out_specs=[row(d), pl.BlockSpec((tm * (d // LANE), LANE), lambda i: (i, 0)), row(LANE), row(LANE)],
        out_shape=[jax.ShapeDtypeStruct((n, d), F32), jax.ShapeDtypeStruct((n * (d // LANE), LANE), F32),
                   jax.ShapeDtypeStruct((n, LANE), jnp.int32), jax.ShapeDtypeStruct((n, LANE), F32)],
        compiler_params=_cparams(("parallel",)),
        name="out_proj",
    )(oc, os_, ow, ob, x2, w_out, fw, wr, br)


def _moe_kernel(bexp_ref, bact_ref, rtok_ref, rtokn_ref, rslotp_ref, rslot_ref, hn_hbm, wg_ref, wu_ref,
                wd_ref, y_hbm, xbuf, ybuf, wgb, wub, wdb, cached, gsem, ssem, *, n_real):
    i = pl.program_id(0)
    nb = pl.num_programs(0)
    rb = rtok_ref.shape[2]
    seg = ybuf.shape[1] // rb
    slot = i & 1
    other = 1 - slot
    active = bact_ref[i] > 0
    prev_active = jnp.logical_and(i > 0, bact_ref[jnp.maximum(i - 1, 0)] > 0)

    def gather_copy(tok, r, buf):
        return pltpu.make_async_copy(hn_hbm.at[pl.ds(pl.multiple_of(tok, seg), seg), :],
                                     xbuf.at[buf, pl.ds(r * seg, seg), :], gsem.at[buf])

    def scatter_copy(dst, r, buf):
        return pltpu.make_async_copy(ybuf.at[buf, pl.ds(r * seg, seg), :],
                                     y_hbm.at[pl.ds(pl.multiple_of(dst, seg), seg), :], ssem.at[buf])

    def wait_gather(buf):
        pltpu.make_async_copy(hn_hbm.at[pl.ds(0, rb * seg), :], xbuf.at[buf], gsem.at[buf]).wait()

    def wait_scatter(buf):
        pltpu.make_async_copy(ybuf.at[buf], y_hbm.at[pl.ds(0, rb * seg), :], ssem.at[buf]).wait()

    @pl.when(i == 0)
    def _():
        cached[0] = -1
        ybuf[...] = jnp.zeros(ybuf.shape, F32)
        pltpu.make_async_copy(ybuf.at[0], y_hbm.at[pl.ds(n_real * seg, rb * seg), :],
                              ssem.at[0]).start()

        def first(r, carry):
            gather_copy(rtok_ref[0, 0, r], r, 0).start()
            return carry
        lax.fori_loop(0, rb, first, 0)

    @pl.when(jnp.logical_or(i == 0, prev_active))
    def _():
        wait_gather(slot)

    @pl.when(active)
    def _():
        e = bexp_ref[i]

        @pl.when(cached[0] != e)
        def _():
            wgb[...] = wg_ref[0].astype(BF16)
            wub[...] = wu_ref[0].astype(BF16)
            wdb[...] = wd_ref[0].astype(BF16)
            cached[0] = e

        n_piece = 8
        per = rb // n_piece

        def issue(piece):
            for r in range(piece * per, (piece + 1) * per):
                gather_copy(rtokn_ref[0, 0, r], r, other).start()
                scatter_copy(rslotp_ref[0, 0, r], r, other).start()

        x = _load_token_major(xbuf.at[slot], rb).astype(BF16)
        de = wgb.shape[1]
        d = wdb.shape[1]
        hc = de // 2
        acts = []
        for c in range(2):
            issue(2 * c)
            hg = jnp.dot(x, wgb[:, c * hc:(c + 1) * hc], preferred_element_type=F32)
            issue(2 * c + 1)
            hu = jnp.dot(x, wub[:, c * hc:(c + 1) * hc], preferred_element_type=F32)
            acts.append((hg * _sigmoid(hg) * hu).astype(BF16))
        act = jnp.concatenate(acts, axis=1)
        dc = d // 4
        ys = []
        for j in range(4):
            issue(4 + j)
            ys.append(jnp.dot(act, wdb[:, j * dc:(j + 1) * dc], preferred_element_type=F32))
        wait_scatter(slot)
        _store_token_major(ybuf.at[slot], jnp.concatenate(ys, axis=1))

    def scatter_all(idx_ref, buf):
        def body(r, carry):
            scatter_copy(idx_ref[0, 0, r], r, buf).start()
            return carry
        lax.fori_loop(0, rb, body, 0)

    @pl.when(jnp.logical_and(jnp.logical_not(active), prev_active))
    def _():
        wait_scatter(slot)
        scatter_all(rslotp_ref, other)
        wait_scatter(other)

    @pl.when(jnp.logical_and(i == nb - 1, active))
    def _():
        wait_scatter(other)
        scatter_all(rslot_ref, slot)
        wait_scatter(slot)
        wait_gather(other)


def _moe(hn, bexp, bact, rtok, rslot, w_gate, w_up, w_down, n_real):
    d, de = w_gate.shape[1], w_gate.shape[2]
    seg = d // LANE
    nb = bexp.shape[0]
    rb = MOE_ROW_BLOCK
    smem_rows = pl.BlockSpec((1, 1, rb), lambda i, be, ba: (i, 0, 0), memory_space=pltpu.SMEM)
    smem_next = pl.BlockSpec((1, 1, rb), lambda i, be, ba: (jnp.minimum(i + 1, nb - 1), 0, 0),
                             memory_space=pltpu.SMEM)
    smem_cur = pl.BlockSpec((1, 1, rb), lambda i, be, ba: (i + 1, 0, 0), memory_space=pltpu.SMEM)
    grid_spec = pltpu.PrefetchScalarGridSpec(
        num_scalar_prefetch=2,
        grid=(nb,),
        in_specs=[smem_rows, smem_next, smem_rows, smem_cur,
                  pl.BlockSpec(memory_space=pl.ANY),
                  pl.BlockSpec((1, d, de), lambda i, be, ba: (be[i], 0, 0)),
                  pl.BlockSpec((1, d, de), lambda i, be, ba: (be[i], 0, 0)),
                  pl.BlockSpec((1, de, d), lambda i, be, ba: (be[i], 0, 0))],
        out_specs=pl.BlockSpec(memory_space=pl.ANY),
        scratch_shapes=[pltpu.VMEM((2, rb * seg, LANE), F32), pltpu.VMEM((2, rb * seg, LANE), F32),
                        pltpu.VMEM((d, de), BF16), pltpu.VMEM((d, de), BF16), pltpu.VMEM((de, d), BF16),
                        pltpu.SMEM((1,), jnp.int32),
                        pltpu.SemaphoreType.DMA((2,)), pltpu.SemaphoreType.DMA((2,))])
    return pl.pallas_call(
        functools.partial(_moe_kernel, n_real=n_real),
        grid_spec=grid_spec,
        out_shape=jax.ShapeDtypeStruct(((n_real + 2 * rb) * seg, LANE), F32),
        compiler_params=_cparams(("arbitrary",)),
        name="moe",
    )(bexp, bact, rtok, rtok, rslot, rslot, hn, w_gate, w_up, w_down)


def _dispatch(ids, n, seg):
    k = 2
    m = n * k
    rb = MOE_ROW_BLOCK
    e_flat = ids[:, :k].reshape(m)
    onehot = (e_flat[:, None] == jnp.arange(N_EXPERTS, dtype=jnp.int32)[None, :]).astype(jnp.int32)
    csum = jnp.cumsum(onehot, axis=0)
    counts = csum[-1]
    rank = jnp.take_along_axis(csum, e_flat[:, None], axis=1)[:, 0] - 1
    padded = (counts + rb - 1) // rb * rb
    pad_end = jnp.cumsum(padded)
    pad_start = pad_end - padded
    dest = pad_start[e_flat] + rank
    nb = (m + N_EXPERTS * (rb - 1) + rb - 1) // rb
    p = nb * rb
    row_m = jnp.full((p,), -1, jnp.int32).at[dest].set(jnp.arange(m, dtype=jnp.int32))
    real = row_m >= 0
    rtok = jnp.where(real, row_m // k, 0)
    pidx = jnp.arange(p, dtype=jnp.int32)
    rslot = jnp.where(real, (row_m % k) * n + row_m // k, m + ((pidx // rb) % 2) * rb + pidx % rb)
    starts = jnp.arange(nb, dtype=jnp.int32) * rb
    bexp = jnp.minimum(jnp.sum((pad_end[None, :] <= starts[:, None]).astype(jnp.int32), axis=1),
                       N_EXPERTS - 1)
    bact = jnp.sum(real.reshape(nb, rb).astype(jnp.int32), axis=1)
    last_e = jnp.max(jnp.where(bact > 0, bexp, 0))
    bexp = jnp.where(bact > 0, bexp, last_e)
    rslot = jnp.concatenate([m + rb + jnp.arange(rb, dtype=jnp.int32), rslot])
    return (bexp, bact, (rtok * seg).reshape(nb, 1, rb), (rslot * seg).reshape(nb + 1, 1, rb),
            m)


def _combine_kernel(h_ref, y0_ref, y1_ref, wts_ref, fw_ref, o_ref):
    tm = h_ref.shape[0]
    wts = wts_ref[...]
    moe = (_load_token_major(y0_ref, tm) * wts[:, 0:1] + _load_token_major(y1_ref, tm) * wts[:, 1:2])
    h = h_ref[...] + moe
    o_ref[...] = h * lax.rsqrt(jnp.mean(h * h, axis=1, keepdims=True) + EPS) * fw_ref[...]


def _combine(h1, y, wts, final_w):
    n, d = h1.shape
    seg = d // LANE
    tm = min(256, n)
    nt = n // tm
    return pl.pallas_call(
        _combine_kernel,
        grid=(nt,),
        in_specs=[pl.BlockSpec((tm, d), lambda i: (i, 0)),
                  pl.BlockSpec((tm * seg, LANE), lambda i: (i, 0)),
                  pl.BlockSpec((tm * seg, LANE), lambda i: (i + nt, 0)),
                  pl.BlockSpec((tm, LANE), lambda i: (i, 0)),
                  pl.BlockSpec((1, d), lambda i: (0, 0))],
        out_specs=pl.BlockSpec((tm, d), lambda i: (i, 0)),
        out_shape=jax.ShapeDtypeStruct((n, d), F32),
        compiler_params=_cparams(("parallel",)),
        name="combine",
    )(h1, y, y, wts, final_w.reshape(1, d))


def _rope_tables(positions):
    half = ROT_DIM // 2
    inv_freq = ROPE_THETA ** (-jnp.arange(0, ROT_DIM, 2, dtype=F32) / ROT_DIM)
    ang = positions.astype(F32)[..., None] * inv_freq
    cos, sin = jnp.cos(ang), jnp.sin(ang)
    b, t = positions.shape
    ones = jnp.ones((b, t, LANE - ROT_DIM), F32)
    zeros = jnp.zeros((b, t, LANE - half), F32)
    c = jnp.concatenate([cos, cos, ones], axis=-1)
    sa = jnp.concatenate([-sin, zeros], axis=-1)
    sb = jnp.concatenate([jnp.zeros((b, t, half), F32), sin, zeros[..., :LANE - ROT_DIM]], axis=-1)
    return c, sa, sb


def _arrange_w_in(w_in):
    d = w_in.shape[0]
    sizes = (N_HEADS_NSA * HEAD_DIM, 3 * 2 * N_KV_NSA * HEAD_DIM, 3 * N_HEADS_NSA,
             3 * N_HEADS_GDN * HEAD_DIM, N_HEADS_GDN, N_HEADS_GDN, N_HEADS_GDN * HEAD_DIM)
    offs = np.cumsum((0,) + sizes)
    seg = [w_in[:, offs[i]:offs[i + 1]] for i in range(len(sizes))]
    q, kv, gate, gqkv, gb, ga, gz = seg
    used = sum(sizes)
    pad = jnp.zeros((d, N_CB * LANE - used), w_in.dtype)
    return jnp.concatenate([q, kv, gqkv, gz, gate, gb, ga, pad], axis=1).astype(BF16)


def kernel(x, positions, attn_norm_w, w_in, cmp_wk, cmp_pek, cmp_wv, cmp_pev, gdn_conv_w, gdn_a_log,
           gdn_dt_bias, gdn_norm_w, w_out, ffn_norm_w, router_group_w, router_group_b,
           router_expert_w, router_expert_b, moe_w_gate, moe_w_up, moe_w_down, final_norm_w):
    b, t, d = x.shape
    n = b * t
    tabs = _rope_tables(positions)
    h = x.reshape(n, d)
    assert w_in.shape[0] == 1, "single-layer block only"
    for l in range(1):
        proj, small = _in_proj(h, attn_norm_w[l], _arrange_w_in(w_in[l]))
        proj3 = proj.reshape(b, t, N_CB * LANE)
        small3 = small.reshape(b, t, LANE)
        cmp_w = jnp.stack([cmp_wk[l], cmp_wv[l]])
        cmp_pe = jnp.stack([cmp_pek[l], cmp_pev[l]])
        kvc = _nsa_compress(proj3, tabs, cmp_w, cmp_pe)
        o_c, selbias = _nsa_cmp_attn(proj3, small3, tabs, kvc)
        o_s = _nsa_sel_attn(proj3, small3, tabs, selbias)
        o_w = _nsa_win_attn(proj3, small3, tabs)
        o_b = _gdn(proj3, small3, gdn_conv_w[l], gdn_a_log[l], gdn_dt_bias[l], gdn_norm_w[l])
        half = N_HEADS_NSA * HEAD_DIM
        wr = jnp.concatenate([router_group_w[l], router_expert_w[l],
                              jnp.zeros((d, LANE - N_GROUPS - N_EXPERTS), F32)], axis=1)
        br = jnp.concatenate([router_group_b[l], router_expert_b[l],
                              jnp.zeros((LANE - N_GROUPS - N_EXPERTS,), F32)]).reshape(1, LANE)
        h1, hn2, ids, wts = _out_proj(o_c.reshape(n, half), o_s.reshape(n, half), o_w.reshape(n, half),
                                      o_b.reshape(n, half), h, w_out[l].astype(BF16), ffn_norm_w[l], wr, br)
        bexp, bact, rtok, rslot, n_slots = _dispatch(ids, n, d // LANE)
        y = _moe(hn2, bexp, bact, rtok, rslot, moe_w_gate[l], moe_w_up[l], moe_w_down[l], n_slots)
        out = _combine(h1, y, wts, final_norm_w)
    return out.reshape(b, t, d)
```

```python
import functools

import numpy as np
import jax
import jax.numpy as jnp
from jax import lax
from jax.experimental import pallas as pl
from jax.experimental.pallas import tpu as pltpu

F32 = jnp.float32
BF16 = jnp.bfloat16

HEAD_DIM = 128
N_HEADS_NSA = 8
N_KV_NSA = 2
N_REP = N_HEADS_NSA // N_KV_NSA
N_HEADS_GDN = 8
ROT_DIM = 32
ROPE_THETA = 500000.0
CMP_LEN = 32
CMP_STRIDE = 16
SLC_LEN = 64
SLC_TOP = 16
WINDOW = 512
CONV_WIDTH = 4
GDN_CHUNK = 64
N_GROUPS = 8
EXPERTS_PER_GROUP = 8
N_EXPERTS = 64
MOE_ROW_BLOCK = 256
EPS = 1e-6
LANE = 128

CB_Q = 0
CB_KV = 8
CB_GQKV = 20
CB_Z = 44
CB_SMALL = 52
N_CB = 54
SMALL_BETA = 24
SMALL_DECAY = 32

NEG_BIAS = -32768.0
VMEM_LIMIT = 56 * 1024 * 1024


def _cparams(sem):
    return pltpu.CompilerParams(dimension_semantics=sem, vmem_limit_bytes=VMEM_LIMIT)


def _mm(a, b):
    return jnp.dot(a.astype(BF16), b.astype(BF16), preferred_element_type=F32)


def _mm_nt(a, b):
    return lax.dot_general(a.astype(BF16), b.astype(BF16), (((1,), (1,)), ((), ())),
                           preferred_element_type=F32)


def _rope(x, c, sa, sb):
    return (x * c + pltpu.roll(x, LANE - ROT_DIM // 2, 1) * sa
            + pltpu.roll(x, ROT_DIM // 2, 1) * sb)


def _sigmoid(x):
    return 0.5 * jnp.tanh(0.5 * x) + 0.5


def _lane_tiles(x):
    return [x[:, i:i + LANE] for i in range(0, x.shape[1], LANE)]


def _store_token_major(ref, x, tmp_ref):
    rows, d = x.shape
    seg = d // LANE
    for s in range(seg):
        tmp_ref[pl.ds(s, rows, stride=seg), :] = x[:, s * LANE:(s + 1) * LANE]
    ref[...] = tmp_ref[...].astype(ref.dtype)


def _load_token_major(ref, rows, tmp_ref):
    seg = ref.shape[0] // rows
    tmp_ref[...] = ref[...].astype(F32)
    return jnp.concatenate([tmp_ref[pl.ds(s, rows, stride=seg), :] for s in range(seg)], axis=1)


def _rowmax(x):
    return jnp.max(functools.reduce(jnp.maximum, _lane_tiles(x)), axis=1, keepdims=True)


def _rowsum(x):
    return jnp.sum(functools.reduce(jnp.add, _lane_tiles(x)), axis=1, keepdims=True)


def _in_proj_kernel(x_ref, nw_ref, w_ref, o_ref, small_ref, hn_ref, *, small_off):
    j = pl.program_id(1)

    @pl.when(j == 0)
    def _():
        x = x_ref[...]
        ms = jnp.mean(x * x, axis=-1, keepdims=True)
        hn_ref[...] = (x * lax.rsqrt(ms + EPS) * nw_ref[...]).astype(BF16)
    acc = jnp.dot(hn_ref[...], w_ref[...], preferred_element_type=F32)
    o_ref[...] = acc.astype(o_ref.dtype)

    @pl.when(j == pl.num_programs(1) - 1)
    def _():
        small_ref[...] = acc[:, small_off:small_off + LANE]


def _in_proj(x2, norm_w, w):
    n, d = x2.shape
    ncol = w.shape[1]
    tm = min(1024, n)
    tn = 768
    small_off = CB_SMALL * LANE - (ncol // tn - 1) * tn
    assert 0 <= small_off <= tn - LANE
    return pl.pallas_call(
        functools.partial(_in_proj_kernel, small_off=small_off),
        grid=(n // tm, ncol // tn),
        in_specs=[pl.BlockSpec((tm, d), lambda i, j: (i, 0)),
                  pl.BlockSpec((1, d), lambda i, j: (0, 0)),
                  pl.BlockSpec((d, tn), lambda i, j: (0, j))],
        out_specs=[pl.BlockSpec((tm, tn), lambda i, j: (i, j)),
                   pl.BlockSpec((tm, LANE), lambda i, j: (i, 0))],
        out_shape=[jax.ShapeDtypeStruct((n, ncol), BF16), jax.ShapeDtypeStruct((n, LANE), F32)],
        scratch_shapes=[pltpu.VMEM((tm, d), BF16)],
        compiler_params=_cparams(("parallel", "arbitrary")),
        name="in_proj",
    )(x2, norm_w.reshape(1, d), w)


def _compress_kernel(a_ref, c_ref, sa_ref, sb_ref, w_ref, pe_ref, o_ref, xs_ref):
    kv = pl.program_id(1)
    t = a_ref.shape[1]
    nc = t // CMP_STRIDE
    x = a_ref[0].astype(F32)
    xr = _rope(x, c_ref[0], sa_ref[0], sb_ref[0])
    x = jnp.where(kv == 0, xr, x)
    xs_ref[pl.ds(0, t), :] = x
    xs_ref[pl.ds(t, CMP_STRIDE), :] = jnp.zeros((CMP_STRIDE, LANE), F32)
    acc = jnp.zeros((nc, LANE), F32)
    for l in range(CMP_LEN):
        rows = xs_ref[pl.ds(l, nc, stride=CMP_STRIDE), :] + pe_ref[0, pl.ds(l, 1), :]
        acc = acc + _mm(rows, w_ref[0, l])
    o_ref[0, 0, 0] = acc.astype(BF16)


def _nsa_compress(proj3, tabs, cmp_w, cmp_pe):
    b, t, _ = proj3.shape
    g = N_KV_NSA
    nc = t // CMP_STRIDE
    tab_spec = pl.BlockSpec((1, t, LANE), lambda bi, kv, gi: (bi, 0, 0))
    return pl.pallas_call(
        _compress_kernel,
        grid=(b, 2, g),
        in_specs=[pl.BlockSpec((1, t, LANE), lambda bi, kv, gi: (bi, 0, CB_KV + kv * g + gi)),
                  tab_spec, tab_spec, tab_spec,
                  pl.BlockSpec((1, CMP_LEN, LANE, LANE), lambda bi, kv, gi: (kv, 0, 0, 0)),
                  pl.BlockSpec((1, CMP_LEN, LANE), lambda bi, kv, gi: (kv, 0, 0))],
        out_specs=pl.BlockSpec((1, 1, 1, nc, LANE), lambda bi, kv, gi: (bi, kv, gi, 0, 0)),
        out_shape=jax.ShapeDtypeStruct((b, 2, g, nc, LANE), BF16),
        scratch_shapes=[pltpu.VMEM((t + CMP_STRIDE, LANE), F32)],
        compiler_params=_cparams(("parallel", "arbitrary", "arbitrary")),
        name="nsa_compress",
    )(proj3, *tabs, cmp_w, cmp_pe)


def _load_q(q_ref, c, sa, sb):
    scale = HEAD_DIM ** -0.5
    qs = [(_rope(q_ref[0, :, r * LANE:(r + 1) * LANE].astype(F32), c, sa, sb) * scale).astype(BF16)
          for r in range(N_REP)]
    return jnp.concatenate(qs, axis=0)


def _store_gated(o_ref, o, gate_ref, g, branch, tq):
    gt = gate_ref[0]
    for r in range(N_REP):
        col = ((g * N_REP + r) * 3 + branch)
        lane = lax.broadcasted_iota(jnp.int32, gt.shape, 1)
        gcol = jnp.sum(jnp.where(lane == col, gt, 0.0), axis=1, keepdims=True)
        o_ref[0, :, r * LANE:(r + 1) * LANE] = (o[r * tq:(r + 1) * tq] * _sigmoid(gcol)).astype(o_ref.dtype)


def _cmp_attn_kernel(q_ref, c_ref, sa_ref, sb_ref, kc_ref, vc_ref, gate_ref, o_ref, sel_ref, *, n_slc):
    g = pl.program_id(1)
    i = pl.program_id(2)
    tq = q_ref.shape[1]
    nc = kc_ref.shape[3]
    q4 = _load_q(q_ref, c_ref[0], sa_ref[0], sb_ref[0])
    s = _mm_nt(q4, kc_ref[0, 0, 0])
    row = lax.broadcasted_iota(jnp.int32, s.shape, 0)
    n = lax.broadcasted_iota(jnp.int32, s.shape, 1)
    tpos = i * tq + (row & (tq - 1))
    mask = (n * CMP_STRIDE + (CMP_LEN - 1) <= tpos) & (n < nc - 1)
    sm = jnp.where(mask, s, -1e30)
    m = _rowmax(sm)
    p = jnp.where(mask, jnp.exp(sm - m), 0.0)
    l = _rowsum(p)
    p = p / jnp.maximum(l, 1e-30)
    o = _mm(p, vc_ref[0, 0, 0])
    _store_gated(o_ref, o, gate_ref, g, 0, tq)

    ps = p[0:tq]
    for r in range(1, N_REP):
        ps = ps + p[r * tq:(r + 1) * tq]
    cn = lax.broadcasted_iota(jnp.int32, (nc, LANE), 0)
    cj = lax.broadcasted_iota(jnp.int32, (nc, LANE), 1)
    ratio = SLC_LEN // CMP_STRIDE
    agg = ((cn >= ratio * cj - (CMP_LEN // CMP_STRIDE - 1)) & (cn < ratio * cj + ratio)
           & (cn < nc - 1) & (cj < n_slc))
    agg = jnp.where(agg, 1.0, 0.0).astype(BF16)
    ps_hi = ps.astype(BF16)
    ps_lo = (ps - ps_hi.astype(F32)).astype(BF16)
    imp = (jnp.dot(ps_hi, agg, preferred_element_type=F32)
           + jnp.dot(ps_lo, agg, preferred_element_type=F32))

    j = lax.broadcasted_iota(jnp.int32, (tq, LANE), 1)
    tt = i * tq + lax.broadcasted_iota(jnp.int32, (tq, LANE), 0)
    cur = tt // SLC_LEN
    valid = j <= cur
    forced = (j == 0) | (j == cur) | (j == cur - 1)
    vals = jnp.where(forced, 1e30, jnp.where(valid, imp, -1.0))
    vt = vals.T
    jb = lax.broadcasted_iota(jnp.int32, (n_slc, tq), 0)
    vb = vt[0:n_slc]
    cnt = jnp.zeros((n_slc, tq), F32)
    for jp in range(n_slc):
        cand = vt[jp:jp + 1, :]
        ge = jnp.where(cand >= vb, 1.0, 0.0)
        gt = jnp.where(cand > vb, 1.0, 0.0)
        cnt = cnt + jnp.where(jb > jp, ge, gt)
    keep = jnp.where(cnt < float(min(SLC_TOP, n_slc)), 0.0, NEG_BIAS)
    if n_slc < LANE:
        keep = jnp.concatenate([keep, jnp.full((LANE - n_slc, tq), NEG_BIAS, F32)], axis=0)
    sel_ref[0, 0] = jnp.where(valid, keep.T, NEG_BIAS).astype(BF16)


def _nsa_cmp_attn(proj3, small3, tabs, kvc, tq=128):
    b, t, _ = proj3.shape
    g = N_KV_NSA
    nc = t // CMP_STRIDE
    n_slc = t // SLC_LEN
    rw = N_REP * LANE
    tab_spec = pl.BlockSpec((1, tq, LANE), lambda bi, gi, i: (bi, i, 0))
    return pl.pallas_call(
        functools.partial(_cmp_attn_kernel, n_slc=n_slc),
        grid=(b, g, t // tq),
        in_specs=[pl.BlockSpec((1, tq, rw), lambda bi, gi, i: (bi, i, gi)),
                  tab_spec, tab_spec, tab_spec,
                  pl.BlockSpec((1, 1, 1, nc, LANE), lambda bi, gi, i: (bi, 0, gi, 0, 0)),
                  pl.BlockSpec((1, 1, 1, nc, LANE), lambda bi, gi, i: (bi, 1, gi, 0, 0)),
                  pl.BlockSpec((1, tq, LANE), lambda bi, gi, i: (bi, i, 0))],
        out_specs=[pl.BlockSpec((1, tq, rw), lambda bi, gi, i: (bi, i, gi)),
                   pl.BlockSpec((1, 1, tq, LANE), lambda bi, gi, i: (bi, gi, i, 0))],
        out_shape=[jax.ShapeDtypeStruct((b, t, N_HEADS_NSA * LANE), BF16),
                   jax.ShapeDtypeStruct((b, g, t, LANE), BF16)],
        compiler_params=_cparams(("parallel", "parallel", "parallel")),
        name="nsa_cmp",
    )(proj3, *tabs, kvc, kvc, small3)


def _sel_attn_kernel(q_ref, c_ref, sa_ref, sb_ref, cf_ref, saf_ref, sbf_ref, k_ref, v_ref,
                     sel_ref, gate_ref, o_ref, ka_ref, vs_ref, m_ref, l_ref, acc_ref, s0_ref, s1_ref,
                     *, tk):
    g = pl.program_id(1)
    i = pl.program_id(2)
    tq = q_ref.shape[1]
    t = k_ref.shape[1]
    rows = N_REP * tq

    @pl.when(i == 0)
    def _():
        kr = _rope(k_ref[0].astype(F32), cf_ref[0], saf_ref[0], sbf_ref[0])
        ka_ref[:, 0:LANE] = kr.astype(BF16)
        pos = lax.broadcasted_iota(jnp.int32, (t, LANE), 0)
        lane = lax.broadcasted_iota(jnp.int32, (t, LANE), 1)
        ka_ref[:, LANE:2 * LANE] = jnp.where(lane == pos // SLC_LEN, 1.0, 0.0).astype(BF16)
        vs_ref[...] = v_ref[0].astype(BF16)

    q4 = _load_q(q_ref, c_ref[0], sa_ref[0], sb_ref[0])
    bias = sel_ref[0, 0]
    qa = jnp.concatenate([q4, jnp.concatenate([bias] * N_REP, axis=0)], axis=1)

    m_ref[...] = jnp.full((rows, LANE), -1e30, F32)
    l_ref[...] = jnp.zeros((rows, LANE), F32)
    acc_ref[...] = jnp.zeros((rows, LANE), F32)

    half = tk // 2

    def scores(kt, hlf):
        k0 = pl.multiple_of(kt * tk + hlf * half, half)
        return _mm_nt(qa, ka_ref[pl.ds(k0, half), :])

    def update(s_ref, kt, hlf, causal):
        k0 = pl.multiple_of(kt * tk + hlf * half, half)
        s = s_ref[...]
        if causal:
            r = lax.broadcasted_iota(jnp.int32, s.shape, 0)
            kp = k0 + lax.broadcasted_iota(jnp.int32, s.shape, 1)
            s = jnp.where(kp <= i * tq + (r & (tq - 1)), s, -1e30)
        tiles = _lane_tiles(s)
        m_old = m_ref[...]
        m_new = jnp.maximum(m_old, _rowmax(s))
        alpha = jnp.exp(m_old - m_new)
        ps = [jnp.exp(tl - m_new) for tl in tiles]
        l_ref[...] = alpha * l_ref[...] + jnp.sum(functools.reduce(jnp.add, ps), axis=1, keepdims=True)
        p = jnp.concatenate([x.astype(BF16) for x in ps], axis=1)
        acc_ref[...] = alpha * acc_ref[...] + jnp.dot(p, vs_ref[pl.ds(k0, half), :],
                                                      preferred_element_type=F32)
        m_ref[...] = m_new

    diag = (i * tq) // tk
    s0_ref[...] = scores(0, 0)

    def body(kt, carry):
        s1_ref[...] = scores(kt, 1)
        update(s0_ref, kt, 0, False)
        s0_ref[...] = scores(kt + 1, 0)
        update(s1_ref, kt, 1, False)
        return carry

    lax.fori_loop(0, diag, body, 0)
    s1_ref[...] = scores(diag, 1)
    update(s0_ref, diag, 0, True)
    update(s1_ref, diag, 1, True)
    o = acc_ref[...] / l_ref[...]
    _store_gated(o_ref, o, gate_ref, g, 1, tq)


def _nsa_sel_attn(proj3, small3, tabs, selbias, tq=128, tk=512):
    b, t, _ = proj3.shape
    g = N_KV_NSA
    rw = N_REP * LANE
    tk = min(tk, t)
    tab_spec = pl.BlockSpec((1, tq, LANE), lambda bi, gi, i: (bi, i, 0))
    tabf_spec = pl.BlockSpec((1, t, LANE), lambda bi, gi, i: (bi, 0, 0))
    kcb = CB_KV + 1 * 2 * g
    return pl.pallas_call(
        functools.partial(_sel_attn_kernel, tk=tk),
        grid=(b, g, t // tq),
        in_specs=[pl.BlockSpec((1, tq, rw), lambda bi, gi, i: (bi, i, gi)),
                  tab_spec, tab_spec, tab_spec, tabf_spec, tabf_spec, tabf_spec,
                  pl.BlockSpec((1, t, LANE), lambda bi, gi, i: (bi, 0, kcb + gi)),
                  pl.BlockSpec((1, t, LANE), lambda bi, gi, i: (bi, 0, kcb + g + gi)),
                  pl.BlockSpec((1, 1, tq, LANE), lambda bi, gi, i: (bi, gi, i, 0)),
                  pl.BlockSpec((1, tq, LANE), lambda bi, gi, i: (bi, i, 0))],
        out_specs=pl.BlockSpec((1, tq, rw), lambda bi, gi, i: (bi, i, gi)),
        out_shape=jax.ShapeDtypeStruct((b, t, N_HEADS_NSA * LANE), BF16),
        scratch_shapes=[pltpu.VMEM((t, 2 * LANE), BF16), pltpu.VMEM((t, LANE), BF16),
                        pltpu.VMEM((N_REP * tq, LANE), F32), pltpu.VMEM((N_REP * tq, LANE), F32),
                        pltpu.VMEM((N_REP * tq, LANE), F32),
                        pltpu.VMEM((N_REP * tq, tk // 2), F32), pltpu.VMEM((N_REP * tq, tk // 2), F32)],
        compiler_params=_cparams(("parallel", "parallel", "arbitrary")),
        name="nsa_sel",
    )(proj3, *tabs, *tabs, proj3, proj3, selbias, small3)


def _win_attn_kernel(q_ref, c_ref, sa_ref, sb_ref, cf_ref, saf_ref, sbf_ref, k_ref, v_ref,
                     gate_ref, o_ref, ks_ref, vs_ref, *, span):
    g = pl.program_id(1)
    i = pl.program_id(2)
    tq = q_ref.shape[1]

    t = k_ref.shape[1]

    @pl.when(i == 0)
    def _():
        zeros = jnp.zeros((WINDOW, LANE), BF16)
        ks_ref[pl.ds(0, WINDOW), :] = zeros
        vs_ref[pl.ds(0, WINDOW), :] = zeros
        ks_ref[pl.ds(WINDOW, t), :] = _rope(k_ref[0].astype(F32), cf_ref[0], saf_ref[0],
                                            sbf_ref[0]).astype(BF16)
        vs_ref[pl.ds(WINDOW, t), :] = v_ref[0].astype(BF16)

    q4 = _load_q(q_ref, c_ref[0], sa_ref[0], sb_ref[0])
    k0 = pl.multiple_of(i * tq, tq)
    s = _mm_nt(q4, ks_ref[pl.ds(k0, span), :])
    tiles = _lane_tiles(s)
    r = lax.broadcasted_iota(jnp.int32, tiles[0].shape, 0) & (tq - 1)
    c = lax.broadcasted_iota(jnp.int32, tiles[0].shape, 1)
    first_block = WINDOW // tq - i
    masked = []
    for b, tl in enumerate(tiles):
        if b == 0:
            tl = jnp.where(c > r, tl, -1e30)
        if b == len(tiles) - 1:
            tl = jnp.where(c <= r, tl, -1e30)
        else:
            tl = jnp.where(b >= first_block, tl, -1e30)
        masked.append(tl)
    m = jnp.max(functools.reduce(jnp.maximum, masked), axis=1, keepdims=True)
    ps = [jnp.exp(tl - m) for tl in masked]
    l = jnp.sum(functools.reduce(jnp.add, ps), axis=1, keepdims=True)
    p = jnp.concatenate([x.astype(BF16) for x in ps], axis=1)
    o = jnp.dot(p, vs_ref[pl.ds(k0, span), :], preferred_element_type=F32) / l
    _store_gated(o_ref, o, gate_ref, g, 2, tq)


def _nsa_win_attn(proj3, small3, tabs, tq=128):
    b, t, _ = proj3.shape
    g = N_KV_NSA
    rw = N_REP * LANE
    span = WINDOW + tq
    assert tq == LANE
    tab_spec = pl.BlockSpec((1, tq, LANE), lambda bi, gi, i: (bi, i, 0))
    tabf_spec = pl.BlockSpec((1, t, LANE), lambda bi, gi, i: (bi, 0, 0))
    kcb = CB_KV + 2 * 2 * g
    return pl.pallas_call(
        functools.partial(_win_attn_kernel, span=span),
        grid=(b, g, t // tq),
        in_specs=[pl.BlockSpec((1, tq, rw), lambda bi, gi, i: (bi, i, gi)),
                  tab_spec, tab_spec, tab_spec, tabf_spec, tabf_spec, tabf_spec,
                  pl.BlockSpec((1, t, LANE), lambda bi, gi, i: (bi, 0, kcb + gi)),
                  pl.BlockSpec((1, t, LANE), lambda bi, gi, i: (bi, 0, kcb + g + gi)),
                  pl.BlockSpec((1, tq, LANE), lambda bi, gi, i: (bi, i, 0))],
        out_specs=pl.BlockSpec((1, tq, rw), lambda bi, gi, i: (bi, i, gi)),
        out_shape=jax.ShapeDtypeStruct((b, t, N_HEADS_NSA * LANE), BF16),
        scratch_shapes=[pltpu.VMEM((t + WINDOW, LANE), BF16), pltpu.VMEM((t + WINDOW, LANE), BF16)],
        compiler_params=_cparams(("parallel", "parallel", "arbitrary")),
        name="nsa_win",
    )(proj3, *tabs, *tabs, proj3, proj3, small3)


def _gdn_kernel(q_ref, k_ref, v_ref, z_ref, braw_ref, araw_ref, cwq_ref, cwk_ref, cwv_ref, alog_ref,
                dtb_ref, nw_ref, o_ref, q_s, k_s, v_s, o_s, qp_s, op_s, xp_s, bd_s, gcd_s, gl_s, mm_s,
                nn_s):
    t = q_ref.shape[1]
    c = GDN_CHUNK
    nchunk = t // c
    xp_s[pl.ds(0, 8), :] = jnp.zeros((8, LANE), F32)

    def conv_silu(x, cw):
        xp_s[pl.ds(8, t), :] = x
        y = x * cw[CONV_WIDTH - 1:CONV_WIDTH]
        for sft in range(1, CONV_WIDTH):
            y = y + xp_s[pl.ds(8 - sft, t), :] * cw[CONV_WIDTH - 1 - sft:CONV_WIDTH - sft]
        return y * _sigmoid(y)

    def l2n(x):
        return x * lax.rsqrt(jnp.sum(x * x, axis=1, keepdims=True) + EPS)

    q_s[...] = l2n(conv_silu(q_ref[0].astype(F32), cwq_ref[...])) * (HEAD_DIM ** -0.5)
    k_s[...] = l2n(conv_silu(k_ref[0].astype(F32), cwk_ref[...]))
    v_s[...] = conv_silu(v_ref[0].astype(F32), cwv_ref[...])

    bd_s[...] = _sigmoid(braw_ref[0, 0])
    xa = araw_ref[0, 0] + dtb_ref[0]
    softplus = jnp.maximum(xa, 0.0) + jnp.log(1.0 + jnp.exp(-jnp.abs(xa)))
    gc = -jnp.exp(alog_ref[0]) * softplus
    lane = lax.broadcasted_iota(jnp.int32, gc.shape, 1)
    sft = 1
    while sft < c:
        gc = gc + jnp.where((lane & (c - 1)) >= sft, pltpu.roll(gc, sft, 1), 0.0)
        sft *= 2
    gcd_s[...] = gc

    c2 = 2 * c
    ci = lax.broadcasted_iota(jnp.int32, (c2, c2), 0)
    cj = lax.broadcasted_iota(jnp.int32, (c2, c2), 1)
    same = (ci // c) == (cj // c)
    tril = same & (cj <= ci)
    strict = same & (cj < ci)
    eye = jnp.where(ci == cj, 1.0, 0.0)
    first = ci < c

    pairs_per_iter = 8
    rng = range(pairs_per_iter)

    def prep(it, carry):
        n2 = [it * pairs_per_iter + p for p in rng]
        sl = [pl.ds(pl.multiple_of(n * c2, c2), c2) for n in n2]
        grow = [gcd_s[pl.ds(n, 1), :] for n in n2]
        gc2 = [jnp.broadcast_to(g_, (c2, c2)).T for g_ in grow]
        betac = [jnp.broadcast_to(bd_s[pl.ds(n, 1), :], (c2, c2)).T for n in n2]
        kn = [k_s[s_, :] for s_ in sl]
        kbn = [kn[p] * betac[p] for p in rng]
        decay = [jnp.exp(jnp.where(tril, gc2[p] - grow[p], -1e30)) for p in rng]
        kk = [_mm_nt(kbn[p], kn[p]) for p in rng]
        y = [-jnp.where(strict, kk[p] * decay[p], 0.0) for p in rng]
        pm = [eye + y_ for y_ in y]
        pw = 1
        while 2 * pw < c:
            y = [_mm(y_, y_) for y_ in y]
            pm = [pm[p] + _mm(pm[p], y[p]) for p in rng]
            pw *= 2
        egn = [jnp.exp(g_) for g_ in gc2]
        u = [_mm(pm[p], v_s[sl[p], :] * betac[p]) for p in rng]
        w = [_mm(pm[p], kbn[p] * egn[p]) for p in rng]
        qn = [q_s[s_, :] for s_ in sl]
        qk = [_mm_nt(qn[p], kn[p]) for p in rng]
        attn = [jnp.where(tril, qk[p] * decay[p], 0.0) for p in rng]
        aw = [_mm(attn[p], w[p]) for p in rng]
        au = [_mm(attn[p], u[p]) for p in rng]
        for p in rng:
            qp_s[sl[p], :] = qn[p] * egn[p] - aw[p]
            op_s[sl[p], :] = au[p]
        gl = [jnp.where(first, g_[c - 1:c, :], g_[c2 - 1:c2, :]) for g_ in gc2]
        kdt = [(kn[p] * jnp.exp(gl[p] - gc2[p])).T for p in rng]
        ma = [_mm(kdt[p][:, 0:c], w[p][0:c]) for p in rng]
        mb = [_mm(kdt[p][:, c:c2], w[p][c:c2]) for p in rng]
        na = [_mm(kdt[p][:, 0:c], u[p][0:c]) for p in rng]
        nb_ = [_mm(kdt[p][:, c:c2], u[p][c:c2]) for p in rng]
        mba = [_mm(mb[p], ma[p]) for p in rng]
        mbn = [_mm(mb[p], na[p]) for p in rng]
        for p in rng:
            ga = jnp.exp(gc2[p][c - 1:c, :])
            gb = jnp.exp(gc2[p][c2 - 1:c2, :])
            mm_s[2 * n2[p]] = ma[p].astype(BF16)
            nn_s[2 * n2[p]] = na[p]
            gl_s[pl.ds(2 * n2[p], 1), :] = ga
            mm_s[2 * n2[p] + 1] = (gb * ma[p] + ga * mb[p] - mba[p]).astype(BF16)
            nn_s[2 * n2[p] + 1] = gb * na[p] - mbn[p] + nb_[p]
            gl_s[pl.ds(2 * n2[p] + 1, 1), :] = ga * gb
        return carry

    lax.fori_loop(0, nchunk // (2 * pairs_per_iter), prep, 0)

    def second_out(n2, s_mid):
        rows = pl.ds(pl.multiple_of(n2 * c2, c2) + c, c)
        o_s[rows, :] = (jnp.dot(qp_s[rows, :].astype(BF16), s_mid.astype(BF16),
                                 preferred_element_type=F32) + op_s[rows, :])

    def scan(n2, carry):
        s, s_mid_prev = carry
        second_out(jnp.maximum(n2 - 1, 0), s_mid_prev)
        ra = pl.ds(pl.multiple_of(n2 * c2, c2), c)
        sb = s.astype(BF16)
        s_mid = (s * gl_s[pl.ds(2 * n2, 1), :] - jnp.dot(mm_s[2 * n2], sb, preferred_element_type=F32)
                 + nn_s[2 * n2])
        s_new = (s * gl_s[pl.ds(2 * n2 + 1, 1), :]
                 - jnp.dot(mm_s[2 * n2 + 1], sb, preferred_element_type=F32) + nn_s[2 * n2 + 1])
        o_s[ra, :] = jnp.dot(qp_s[ra, :].astype(BF16), sb, preferred_element_type=F32) + op_s[ra, :]
        return s_new, s_mid

    zero_state = jnp.zeros((LANE, LANE), F32)
    _, s_mid_last = lax.fori_loop(0, nchunk // 2, scan, (zero_state, zero_state))
    second_out(jnp.int32(nchunk // 2 - 1), s_mid_last)

    o = o_s[...]
    on = o * lax.rsqrt(jnp.mean(o * o, axis=1, keepdims=True) + EPS) * nw_ref[...]
    z = z_ref[0].astype(F32)
    o_ref[0] = (on * (z * _sigmoid(z))).astype(o_ref.dtype)


def _gdn(proj3, small3, conv_w, a_log, dt_bias, norm_w):
    b, t, _ = proj3.shape
    hh = N_HEADS_GDN
    c = GDN_CHUNK
    col = lambda cb: pl.BlockSpec((1, t, LANE), lambda bi, hi: (bi, 0, cb + hi))
    cw = lambda off: pl.BlockSpec((CONV_WIDTH, LANE), lambda bi, hi: (0, off + hi))
    hrow = pl.BlockSpec((1, 1, LANE), lambda bi, hi: (hi, 0, 0))
    alog_b = jnp.broadcast_to(a_log.astype(F32)[:, None, None], (hh, 1, LANE))
    dtb_b = jnp.broadcast_to(dt_bias.astype(F32)[:, None, None], (hh, 1, LANE))
    big = pltpu.VMEM((t, LANE), F32)
    nrow = t // LANE
    ba = small3[:, :, SMALL_BETA:SMALL_BETA + 2 * hh].transpose(0, 2, 1).reshape(b, 2 * hh, nrow, LANE)
    dense = lambda off: pl.BlockSpec((1, 1, nrow, LANE), lambda bi, hi: (bi, off + hi, 0, 0))
    return pl.pallas_call(
        _gdn_kernel,
        grid=(b, hh),
        in_specs=[col(CB_GQKV), col(CB_GQKV + hh), col(CB_GQKV + 2 * hh), col(CB_Z),
                  dense(0), dense(hh),
                  cw(0), cw(hh), cw(2 * hh), hrow, hrow,
                  pl.BlockSpec((1, LANE), lambda bi, hi: (0, 0))],
        out_specs=pl.BlockSpec((1, t, LANE), lambda bi, hi: (bi, 0, hi)),
        out_shape=jax.ShapeDtypeStruct((b, t, hh * LANE), BF16),
        scratch_shapes=[big, big, big, big, big, big, pltpu.VMEM((t + 8, LANE), F32),
                        pltpu.VMEM((nrow, LANE), F32), pltpu.VMEM((nrow, LANE), F32),
                        pltpu.VMEM((t // c, LANE), F32),
                        pltpu.VMEM((t // c, LANE, LANE), BF16),
                        pltpu.VMEM((t // c, LANE, LANE), F32)],
        compiler_params=_cparams(("parallel", "parallel")),
        name="gdn",
    )(proj3, proj3, proj3, proj3, ba, ba, conv_w, conv_w, conv_w, alog_b, dtb_b,
      norm_w.reshape(1, LANE))


def _split3(a):
    hi = a.astype(BF16)
    lo = (a - hi.astype(F32)).astype(BF16)
    return hi, lo


def _out_proj_kernel(oc_ref, os_ref, ow_ref, ob_ref, x_ref, wo_ref, fw_ref, wr_ref, br_ref,
                     h_ref, hn_ref, ids_ref, wts_ref, tmp_ref):
    half = oc_ref.shape[1]
    oa = (oc_ref[...].astype(F32) + os_ref[...].astype(F32) + ow_ref[...].astype(F32)).astype(BF16)
    h1 = (x_ref[...] + jnp.dot(oa, wo_ref[0:half, :], preferred_element_type=F32)
          + jnp.dot(ob_ref[...], wo_ref[half:2 * half, :], preferred_element_type=F32))
    h_ref[...] = h1
    hn = h1 * lax.rsqrt(jnp.mean(h1 * h1, axis=1, keepdims=True) + EPS) * fw_ref[...]
    _store_token_major(hn_ref, hn, tmp_ref)

    a_hi, a_lo = _split3(hn)
    w_hi, w_lo = _split3(wr_ref[...])
    dot = lambda a, b: jnp.dot(a, b, preferred_element_type=F32)
    logits = dot(a_hi, w_hi) + dot(a_hi, w_lo) + dot(a_lo, w_hi) + br_ref[...]
    lane = lax.broadcasted_iota(jnp.int32, logits.shape, 1)
    big = 1e30
    is_g = lane < N_GROUPS
    lg = jnp.where(is_g, logits, -big)
    gm = jnp.max(lg, axis=1, keepdims=True)
    grp = jnp.min(jnp.where(lg == gm, lane, LANE), axis=1, keepdims=True)
    p_grp = 1.0 / jnp.sum(jnp.where(is_g, jnp.exp(lg - gm), 0.0), axis=1, keepdims=True)
    e_id = lane - N_GROUPS
    in_g = (e_id >= 0) & (e_id < N_EXPERTS) & ((e_id // EXPERTS_PER_GROUP) == grp)
    le = jnp.where(in_g, logits, -big)
    em = jnp.max(le, axis=1, keepdims=True)
    pe = jnp.where(in_g, jnp.exp(le - em), 0.0)
    pe = pe / jnp.sum(pe, axis=1, keepdims=True)
    pm = jnp.where(in_g, pe, -1.0)
    p1 = jnp.max(pm, axis=1, keepdims=True)
    i1 = jnp.min(jnp.where(pm == p1, lane, LANE), axis=1, keepdims=True)
    pm2 = jnp.where(lane == i1, -1.0, pm)
    p2 = jnp.max(pm2, axis=1, keepdims=True)
    i2 = jnp.min(jnp.where(pm2 == p2, lane, LANE), axis=1, keepdims=True)
    den = p1 + p2
    ids_ref[...] = jnp.where(lane == 0, i1 - N_GROUPS, jnp.where(lane == 1, i2 - N_GROUPS, 0))
    wts_ref[...] = jnp.where(lane == 0, p1 / den * p_grp, jnp.where(lane == 1, p2 / den * p_grp, 0.0))


def _out_proj(oc, os_, ow, ob, x2, w_out, ffn_w, wr, br):
    n, d = x2.shape
    half = oc.shape[1]
    seg = d // LANE
    tm = min(256, n)
    row = lambda w: pl.BlockSpec((tm, w), lambda i: (i, 0))
    full = lambda a: pl.BlockSpec(a.shape, lambda i: (0,) * a.ndim)
    fw = ffn_w.reshape(1, d)
    return pl.pallas_call(
        _out_proj_kernel,
        grid=(n // tm,),
        in_specs=[row(half), row(half), row(half), row(half), row(d), full(w_out), full(fw),
                  full(wr), full(br)],
        out_specs=[row(d), pl.BlockSpec((tm * seg, LANE), lambda i: (i, 0)), row(LANE), row(LANE)],
        out_shape=[jax.ShapeDtypeStruct((n, d), F32), jax.ShapeDtypeStruct((n * seg, LANE), BF16),
                   jax.ShapeDtypeStruct((n, LANE), jnp.int32), jax.ShapeDtypeStruct((n, LANE), F32)],
        scratch_shapes=[pltpu.VMEM((tm * seg, LANE), F32)],
        compiler_params=_cparams(("parallel",)),
        name="out_proj",
    )(oc, os_, ow, ob, x2, w_out, fw, wr, br)


def _moe_kernel(bexp_ref, bact_ref, rtok_ref, rtokn_ref, rslotp_ref, rslot_ref, hn_hbm, wg_ref, wu_ref,
                wd_ref, y_hbm, xbuf, ybuf, tmp, wgb, wub, wdb, cached, gsem, ssem, *, n_real):
    i = pl.program_id(0)
    nb = pl.num_programs(0)
    rb = rtok_ref.shape[2]
    seg = ybuf.shape[1] // rb
    slot = i & 1
    other = 1 - slot
    active = bact_ref[i] > 0
    prev_active = jnp.logical_and(i > 0, bact_ref[jnp.maximum(i - 1, 0)] > 0)

    def gather_copy(tok, r, buf):
        return pltpu.make_async_copy(hn_hbm.at[pl.ds(pl.multiple_of(tok, seg), seg), :],
                                     xbuf.at[buf, pl.ds(r * seg, seg), :], gsem.at[buf])

    def scatter_copy(dst, r, buf):
        return pltpu.make_async_copy(ybuf.at[buf, pl.ds(r * seg, seg), :],
                                     y_hbm.at[pl.ds(pl.multiple_of(dst, seg), seg), :], ssem.at[buf])

    def wait_gather(buf):
        pltpu.make_async_copy(hn_hbm.at[pl.ds(0, rb * seg), :], xbuf.at[buf], gsem.at[buf]).wait()

    def wait_scatter(buf):
        pltpu.make_async_copy(ybuf.at[buf], y_hbm.at[pl.ds(0, rb * seg), :], ssem.at[buf]).wait()

    @pl.when(i == 0)
    def _():
        cached[0] = -1
        ybuf[...] = jnp.zeros(ybuf.shape, ybuf.dtype)
        pltpu.make_async_copy(ybuf.at[0], y_hbm.at[pl.ds(n_real * seg, rb * seg), :],
                              ssem.at[0]).start()

        def first(r, carry):
            gather_copy(rtok_ref[0, 0, r], r, 0).start()
            return carry
        lax.fori_loop(0, rb, first, 0)

    @pl.when(jnp.logical_or(i == 0, prev_active))
    def _():
        wait_gather(slot)

    @pl.when(active)
    def _():
        e = bexp_ref[i]

        @pl.when(cached[0] != e)
        def _():
            wgb[...] = wg_ref[0].astype(BF16)
            wub[...] = wu_ref[0].astype(BF16)
            wdb[...] = wd_ref[0].astype(BF16)
            cached[0] = e

        n_piece = 8
        per = rb // n_piece

        def issue(piece):
            for r in range(piece * per, (piece + 1) * per):
                gather_copy(rtokn_ref[0, 0, r], r, other).start()
                scatter_copy(rslotp_ref[0, 0, r], r, other).start()

        x = _load_token_major(xbuf.at[slot], rb, tmp).astype(BF16)
        de = wgb.shape[1]
        d = wdb.shape[1]
        hc = de // 2
        acts = []
        for c in range(2):
            issue(2 * c)
            hg = jnp.dot(x, wgb[:, c * hc:(c + 1) * hc], preferred_element_type=F32)
            issue(2 * c + 1)
            hu = jnp.dot(x, wub[:, c * hc:(c + 1) * hc], preferred_element_type=F32)
            acts.append((hg * _sigmoid(hg) * hu).astype(BF16))
        act = jnp.concatenate(acts, axis=1)
        dc = d // 4
        ys = []
        for j in range(4):
            issue(4 + j)
            ys.append(jnp.dot(act, wdb[:, j * dc:(j + 1) * dc], preferred_element_type=F32))
        wait_scatter(slot)
        _store_token_major(ybuf.at[slot], jnp.concatenate(ys, axis=1), tmp)

    def scatter_all(idx_ref, buf):
        def body(r, carry):
            scatter_copy(idx_ref[0, 0, r], r, buf).start()
            return carry
        lax.fori_loop(0, rb, body, 0)

    @pl.when(jnp.logical_and(jnp.logical_not(active), prev_active))
    def _():
        wait_scatter(slot)
        scatter_all(rslotp_ref, other)
        wait_scatter(other)

    @pl.when(jnp.logical_and(i == nb - 1, active))
    def _():
        wait_scatter(other)
        scatter_all(rslot_ref, slot)
        wait_scatter(slot)
        wait_gather(other)


def _moe(hn, bexp, bact, rtok, rslot, w_gate, w_up, w_down, n_real):
    d, de = w_gate.shape[1], w_gate.shape[2]
    seg = d // LANE
    nb = bexp.shape[0]
    rb = MOE_ROW_BLOCK
    smem_rows = pl.BlockSpec((1, 1, rb), lambda i, be, ba: (i, 0, 0), memory_space=pltpu.SMEM)
    smem_next = pl.BlockSpec((1, 1, rb), lambda i, be, ba: (jnp.minimum(i + 1, nb - 1), 0, 0),
                             memory_space=pltpu.SMEM)
    smem_cur = pl.BlockSpec((1, 1, rb), lambda i, be, ba: (i + 1, 0, 0), memory_space=pltpu.SMEM)
    grid_spec = pltpu.PrefetchScalarGridSpec(
        num_scalar_prefetch=2,
        grid=(nb,),
        in_specs=[smem_rows, smem_next, smem_rows, smem_cur,
                  pl.BlockSpec(memory_space=pl.ANY),
                  pl.BlockSpec((1, d, de), lambda i, be, ba: (be[i], 0, 0)),
                  pl.BlockSpec((1, d, de), lambda i, be, ba: (be[i], 0, 0)),
                  pl.BlockSpec((1, de, d), lambda i, be, ba: (be[i], 0, 0))],
        out_specs=pl.BlockSpec(memory_space=pl.ANY),
        scratch_shapes=[pltpu.VMEM((2, rb * seg, LANE), BF16), pltpu.VMEM((2, rb * seg, LANE), BF16),
                        pltpu.VMEM((rb * seg, LANE), F32),
                        pltpu.VMEM((d, de), BF16), pltpu.VMEM((d, de), BF16), pltpu.VMEM((de, d), BF16),
                        pltpu.SMEM((1,), jnp.int32),
                        pltpu.SemaphoreType.DMA((2,)), pltpu.SemaphoreType.DMA((2,))])
    return pl.pallas_call(
        functools.partial(_moe_kernel, n_real=n_real),
        grid_spec=grid_spec,
        out_shape=jax.ShapeDtypeStruct(((n_real + 2 * rb) * seg, LANE), BF16),
        compiler_params=_cparams(("arbitrary",)),
        name="moe",
    )(bexp, bact, rtok, rtok, rslot, rslot, hn, w_gate, w_up, w_down)


def _dispatch(ids, n, seg):
    k = 2
    m = n * k
    rb = MOE_ROW_BLOCK
    e_flat = ids[:, :k].reshape(m)
    onehot = (e_flat[:, None] == jnp.arange(N_EXPERTS, dtype=jnp.int32)[None, :]).astype(jnp.int32)
    csum = jnp.cumsum(onehot, axis=0)
    counts = csum[-1]
    rank = jnp.take_along_axis(csum, e_flat[:, None], axis=1)[:, 0] - 1
    padded = (counts + rb - 1) // rb * rb
    pad_end = jnp.cumsum(padded)
    pad_start = pad_end - padded
    dest = pad_start[e_flat] + rank
    nb = (m + N_EXPERTS * (rb - 1) + rb - 1) // rb
    p = nb * rb
    row_m = jnp.full((p,), -1, jnp.int32).at[dest].set(jnp.arange(m, dtype=jnp.int32))
    real = row_m >= 0
    rtok = jnp.where(real, row_m // k, 0)
    pidx = jnp.arange(p, dtype=jnp.int32)
    rslot = jnp.where(real, (row_m % k) * n + row_m // k, m + ((pidx // rb) % 2) * rb + pidx % rb)
    starts = jnp.arange(nb, dtype=jnp.int32) * rb
    bexp = jnp.minimum(jnp.sum((pad_end[None, :] <= starts[:, None]).astype(jnp.int32), axis=1),
                       N_EXPERTS - 1)
    bact = jnp.sum(real.reshape(nb, rb).astype(jnp.int32), axis=1)
    last_e = jnp.max(jnp.where(bact > 0, bexp, 0))
    bexp = jnp.where(bact > 0, bexp, last_e)
    rslot = jnp.concatenate([m + rb + jnp.arange(rb, dtype=jnp.int32), rslot])
    return (bexp, bact, (rtok * seg).reshape(nb, 1, rb), (rslot * seg).reshape(nb + 1, 1, rb),
            m)


def _combine_kernel(h_ref, y0_ref, y1_ref, wts_ref, fw_ref, o_ref, tmp_ref):
    tm = h_ref.shape[0]
    wts = wts_ref[...]
    y0 = _load_token_major(y0_ref, tm, tmp_ref)
    y1 = _load_token_major(y1_ref, tm, tmp_ref)
    moe = y0 * wts[:, 0:1] + y1 * wts[:, 1:2]
    h = h_ref[...] + moe
    o_ref[...] = h * lax.rsqrt(jnp.mean(h * h, axis=1, keepdims=True) + EPS) * fw_ref[...]


def _combine(h1, y, wts, final_w):
    n, d = h1.shape
    seg = d // LANE
    tm = min(256, n)
    nt = n // tm
    return pl.pallas_call(
        _combine_kernel,
        grid=(nt,),
        in_specs=[pl.BlockSpec((tm, d), lambda i: (i, 0)),
                  pl.BlockSpec((tm * seg, LANE), lambda i: (i, 0)),
                  pl.BlockSpec((tm * seg, LANE), lambda i: (i + nt, 0)),
                  pl.BlockSpec((tm, LANE), lambda i: (i, 0)),
                  pl.BlockSpec((1, d), lambda i: (0, 0))],
        out_specs=pl.BlockSpec((tm, d), lambda i: (i, 0)),
        out_shape=jax.ShapeDtypeStruct((n, d), F32),
        scratch_shapes=[pltpu.VMEM((tm * seg, LANE), F32)],
        compiler_params=_cparams(("parallel",)),
        name="combine",
    )(h1, y, y, wts, final_w.reshape(1, d))


def _rope_tables(positions):
    half = ROT_DIM // 2
    inv_freq = ROPE_THETA ** (-jnp.arange(0, ROT_DIM, 2, dtype=F32) / ROT_DIM)
    ang = positions.astype(F32)[..., None] * inv_freq
    cos, sin = jnp.cos(ang), jnp.sin(ang)
    b, t = positions.shape
    ones = jnp.ones((b, t, LANE - ROT_DIM), F32)
    zeros = jnp.zeros((b, t, LANE - half), F32)
    c = jnp.concatenate([cos, cos, ones], axis=-1)
    sa = jnp.concatenate([-sin, zeros], axis=-1)
    sb = jnp.concatenate([jnp.zeros((b, t, half), F32), sin, zeros[..., :LANE - ROT_DIM]], axis=-1)
    return c, sa, sb


def _arrange_w_in(w_in):
    d = w_in.shape[0]
    sizes = (N_HEADS_NSA * HEAD_DIM, 3 * 2 * N_KV_NSA * HEAD_DIM, 3 * N_HEADS_NSA,
             3 * N_HEADS_GDN * HEAD_DIM, N_HEADS_GDN, N_HEADS_GDN, N_HEADS_GDN * HEAD_DIM)
    offs = np.cumsum((0,) + sizes)
    seg = [w_in[:, offs[i]:offs[i + 1]] for i in range(len(sizes))]
    q, kv, gate, gqkv, gb, ga, gz = seg
    used = sum(sizes)
    pad = jnp.zeros((d, N_CB * LANE - used), w_in.dtype)
    return jnp.concatenate([q, kv, gqkv, gz, gate, gb, ga, pad], axis=1).astype(BF16)


def kernel(x, positions, attn_norm_w, w_in, cmp_wk, cmp_pek, cmp_wv, cmp_pev, gdn_conv_w, gdn_a_log,
           gdn_dt_bias, gdn_norm_w, w_out, ffn_norm_w, router_group_w, router_group_b,
           router_expert_w, router_expert_b, moe_w_gate, moe_w_up, moe_w_down, final_norm_w):
    b, t, d = x.shape
    n = b * t
    tabs = _rope_tables(positions)
    h = x.reshape(n, d)
    assert w_in.shape[0] == 1, "single-layer block only"
    for l in range(1):
        proj, small = _in_proj(h, attn_norm_w[l], _arrange_w_in(w_in[l]))
        proj3 = proj.reshape(b, t, N_CB * LANE)
        small3 = small.reshape(b, t, LANE)
        cmp_w = jnp.stack([cmp_wk[l], cmp_wv[l]])
        cmp_pe = jnp.stack([cmp_pek[l], cmp_pev[l]])
        kvc = _nsa_compress(proj3, tabs, cmp_w, cmp_pe)
        o_c, selbias = _nsa_cmp_attn(proj3, small3, tabs, kvc)
        o_s = _nsa_sel_attn(proj3, small3, tabs, selbias)
        o_w = _nsa_win_attn(proj3, small3, tabs)
        o_b = _gdn(proj3, small3, gdn_conv_w[l], gdn_a_log[l], gdn_dt_bias[l], gdn_norm_w[l])
        half = N_HEADS_NSA * HEAD_DIM
        wr = jnp.concatenate([router_group_w[l], router_expert_w[l],
                              jnp.zeros((d, LANE - N_GROUPS - N_EXPERTS), F32)], axis=1)
        br = jnp.concatenate([router_group_b[l], router_expert_b[l],
                              jnp.zeros((LANE - N_GROUPS - N_EXPERTS,), F32)]).reshape(1, LANE)
        h1, hn2, ids, wts = _out_proj(o_c.reshape(n, half), o_s.reshape(n, half), o_w.reshape(n, half),
                                      o_b.reshape(n, half), h, w_out[l].astype(BF16), ffn_norm_w[l], wr, br)
        bexp, bact, rtok, rslot, n_slots = _dispatch(ids, n, d // LANE)
        y = _moe(hn2, bexp, bact, rtok, rslot, moe_w_gate[l], moe_w_up[l], moe_w_down[l], n_slots)
        out = _combine(h1, y, wts, final_norm_w)
    return out.reshape(b, t, d)
```

```python
import functools

import numpy as np
import jax
import jax.numpy as jnp
from jax import lax
from jax.experimental import pallas as pl
from jax.experimental.pallas import tpu as pltpu

F32 = jnp.float32
BF16 = jnp.bfloat16

HEAD_DIM = 128
N_HEADS_NSA = 8
N_KV_NSA = 2
N_REP = N_HEADS_NSA // N_KV_NSA
N_HEADS_GDN = 8
ROT_DIM = 32
ROPE_THETA = 500000.0
CMP_LEN = 32
CMP_STRIDE = 16
SLC_LEN = 64
SLC_TOP = 16
WINDOW = 512
CONV_WIDTH = 4
GDN_CHUNK = 64
N_GROUPS = 8
EXPERTS_PER_GROUP = 8
N_EXPERTS = 64
MOE_ROW_BLOCK = 256
EPS = 1e-6
LANE = 128

CB_Q = 0
CB_KV = 8
CB_GQKV = 20
CB_Z = 44
CB_SMALL = 52
N_CB = 54
SMALL_BETA = 24
SMALL_DECAY = 32

NEG_BIAS = -32768.0
VMEM_LIMIT = 56 * 1024 * 1024


def _cparams(sem):
    return pltpu.CompilerParams(dimension_semantics=sem, vmem_limit_bytes=VMEM_LIMIT)


def _mm(a, b):
    return jnp.dot(a.astype(BF16), b.astype(BF16), preferred_element_type=F32)


def _mm_nt(a, b):
    return lax.dot_general(a.astype(BF16), b.astype(BF16), (((1,), (1,)), ((), ())),
                           preferred_element_type=F32)


def _rope(x, c, sa, sb):
    return (x * c + pltpu.roll(x, LANE - ROT_DIM // 2, 1) * sa
            + pltpu.roll(x, ROT_DIM // 2, 1) * sb)


def _sigmoid(x):
    return 0.5 * jnp.tanh(0.5 * x) + 0.5


def _lane_tiles(x):
    return [x[:, i:i + LANE] for i in range(0, x.shape[1], LANE)]


def _store_token_major(ref, x, tmp_ref):
    rows, d = x.shape
    seg = d // LANE
    for s in range(seg):
        tmp_ref[pl.ds(s, rows, stride=seg), :] = x[:, s * LANE:(s + 1) * LANE]
    ref[...] = tmp_ref[...].astype(ref.dtype)


def _load_token_major(ref, rows, tmp_ref):
    seg = ref.shape[0] // rows
    tmp_ref[...] = ref[...].astype(F32)
    return jnp.concatenate([tmp_ref[pl.ds(s, rows, stride=seg), :] for s in range(seg)], axis=1)


def _rowmax(x):
    return jnp.max(functools.reduce(jnp.maximum, _lane_tiles(x)), axis=1, keepdims=True)


def _rowsum(x):
    return jnp.sum(functools.reduce(jnp.add, _lane_tiles(x)), axis=1, keepdims=True)


def _in_proj_kernel(x_ref, nw_ref, w_ref, o_ref, small_ref, hn_ref, *, small_off):
    j = pl.program_id(1)

    @pl.when(j == 0)
    def _():
        x = x_ref[...]
        ms = jnp.mean(x * x, axis=-1, keepdims=True)
        hn_ref[...] = (x * lax.rsqrt(ms + EPS) * nw_ref[...]).astype(BF16)
    acc = jnp.dot(hn_ref[...], w_ref[...], preferred_element_type=F32)
    o_ref[...] = acc.astype(o_ref.dtype)

    @pl.when(j == pl.num_programs(1) - 1)
    def _():
        small_ref[...] = acc[:, small_off:small_off + LANE]


def _in_proj(x2, norm_w, w):
    n, d = x2.shape
    ncol = w.shape[1]
    tm = min(1024, n)
    tn = 768
    small_off = CB_SMALL * LANE - (ncol // tn - 1) * tn
    assert 0 <= small_off <= tn - LANE
    return pl.pallas_call(
        functools.partial(_in_proj_kernel, small_off=small_off),
        grid=(n // tm, ncol // tn),
        in_specs=[pl.BlockSpec((tm, d), lambda i, j: (i, 0)),
                  pl.BlockSpec((1, d), lambda i, j: (0, 0)),
                  pl.BlockSpec((d, tn), lambda i, j: (0, j))],
        out_specs=[pl.BlockSpec((tm, tn), lambda i, j: (i, j)),
                   pl.BlockSpec((tm, LANE), lambda i, j: (i, 0))],
        out_shape=[jax.ShapeDtypeStruct((n, ncol), BF16), jax.ShapeDtypeStruct((n, LANE), F32)],
        scratch_shapes=[pltpu.VMEM((tm, d), BF16)],
        compiler_params=_cparams(("parallel", "arbitrary")),
        name="in_proj",
    )(x2, norm_w.reshape(1, d), w)


def _compress_kernel(a_ref, c_ref, sa_ref, sb_ref, w_ref, pe_ref, o_ref, xs_ref):
    kv = pl.program_id(1)
    t = a_ref.shape[1]
    nc = t // CMP_STRIDE
    x = a_ref[0].astype(F32)
    xr = _rope(x, c_ref[0], sa_ref[0], sb_ref[0])
    x = jnp.where(kv == 0, xr, x)
    xs_ref[pl.ds(0, t), :] = x
    xs_ref[pl.ds(t, CMP_STRIDE), :] = jnp.zeros((CMP_STRIDE, LANE), F32)
    acc = jnp.zeros((nc, LANE), F32)
    for l in range(CMP_LEN):
        rows = xs_ref[pl.ds(l, nc, stride=CMP_STRIDE), :] + pe_ref[0, pl.ds(l, 1), :]
        acc = acc + _mm(rows, w_ref[0, l])
    o_ref[0, 0, 0] = acc.astype(BF16)


def _nsa_compress(proj3, tabs, cmp_w, cmp_pe):
    b, t, _ = proj3.shape
    g = N_KV_NSA
    nc = t // CMP_STRIDE
    tab_spec = pl.BlockSpec((1, t, LANE), lambda bi, kv, gi: (bi, 0, 0))
    return pl.pallas_call(
        _compress_kernel,
        grid=(b, 2, g),
        in_specs=[pl.BlockSpec((1, t, LANE), lambda bi, kv, gi: (bi, 0, CB_KV + kv * g + gi)),
                  tab_spec, tab_spec, tab_spec,
                  pl.BlockSpec((1, CMP_LEN, LANE, LANE), lambda bi, kv, gi: (kv, 0, 0, 0)),
                  pl.BlockSpec((1, CMP_LEN, LANE), lambda bi, kv, gi: (kv, 0, 0))],
        out_specs=pl.BlockSpec((1, 1, 1, nc, LANE), lambda bi, kv, gi: (bi, kv, gi, 0, 0)),
        out_shape=jax.ShapeDtypeStruct((b, 2, g, nc, LANE), BF16),
        scratch_shapes=[pltpu.VMEM((t + CMP_STRIDE, LANE), F32)],
        compiler_params=_cparams(("parallel", "arbitrary", "arbitrary")),
        name="nsa_compress",
    )(proj3, *tabs, cmp_w, cmp_pe)


def _load_q(q_ref, c, sa, sb):
    scale = HEAD_DIM ** -0.5
    qs = [(_rope(q_ref[0, :, r * LANE:(r + 1) * LANE].astype(F32), c, sa, sb) * scale).astype(BF16)
          for r in range(N_REP)]
    return jnp.concatenate(qs, axis=0)


def _store_gated(o_ref, o, gate_ref, g, branch, tq):
    gt = gate_ref[0]
    for r in range(N_REP):
        col = ((g * N_REP + r) * 3 + branch)
        lane = lax.broadcasted_iota(jnp.int32, gt.shape, 1)
        gcol = jnp.sum(jnp.where(lane == col, gt, 0.0), axis=1, keepdims=True)
        o_ref[0, :, r * LANE:(r + 1) * LANE] = (o[r * tq:(r + 1) * tq] * _sigmoid(gcol)).astype(o_ref.dtype)


def _cmp_attn_kernel(q_ref, c_ref, sa_ref, sb_ref, kc_ref, vc_ref, gate_ref, o_ref, sel_ref, *, n_slc):
    g = pl.program_id(1)
    i = pl.program_id(2)
    tq = q_ref.shape[1]
    nc = kc_ref.shape[3]
    q4 = _load_q(q_ref, c_ref[0], sa_ref[0], sb_ref[0])
    s = _mm_nt(q4, kc_ref[0, 0, 0])
    row = lax.broadcasted_iota(jnp.int32, s.shape, 0)
    n = lax.broadcasted_iota(jnp.int32, s.shape, 1)
    tpos = i * tq + (row & (tq - 1))
    mask = (n * CMP_STRIDE + (CMP_LEN - 1) <= tpos) & (n < nc - 1)
    sm = jnp.where(mask, s, -1e30)
    m = _rowmax(sm)
    p = jnp.where(mask, jnp.exp(sm - m), 0.0)
    l = _rowsum(p)
    p = p / jnp.maximum(l, 1e-30)
    o = _mm(p, vc_ref[0, 0, 0])
    _store_gated(o_ref, o, gate_ref, g, 0, tq)

    ps = p[0:tq]
    for r in range(1, N_REP):
        ps = ps + p[r * tq:(r + 1) * tq]
    cn = lax.broadcasted_iota(jnp.int32, (nc, LANE), 0)
    cj = lax.broadcasted_iota(jnp.int32, (nc, LANE), 1)
    ratio = SLC_LEN // CMP_STRIDE
    agg = ((cn >= ratio * cj - (CMP_LEN // CMP_STRIDE - 1)) & (cn < ratio * cj + ratio)
           & (cn < nc - 1) & (cj < n_slc))
    agg = jnp.where(agg, 1.0, 0.0).astype(BF16)
    ps_hi = ps.astype(BF16)
    ps_lo = (ps - ps_hi.astype(F32)).astype(BF16)
    imp = (jnp.dot(ps_hi, agg, preferred_element_type=F32)
           + jnp.dot(ps_lo, agg, preferred_element_type=F32))

    j = lax.broadcasted_iota(jnp.int32, (tq, LANE), 1)
    tt = i * tq + lax.broadcasted_iota(jnp.int32, (tq, LANE), 0)
    cur = tt // SLC_LEN
    valid = j <= cur
    forced = (j == 0) | (j == cur) | (j == cur - 1)
    vals = jnp.where(forced, 1e30, jnp.where(valid, imp, -1.0))
    vt = vals.T
    jb = lax.broadcasted_iota(jnp.int32, (n_slc, tq), 0)
    vb = vt[0:n_slc]
    cnt = jnp.zeros((n_slc, tq), F32)
    for jp in range(n_slc):
        cand = vt[jp:jp + 1, :]
        ge = jnp.where(cand >= vb, 1.0, 0.0)
        gt = jnp.where(cand > vb, 1.0, 0.0)
        cnt = cnt + jnp.where(jb > jp, ge, gt)
    keep = jnp.where(cnt < float(min(SLC_TOP, n_slc)), 0.0, NEG_BIAS)
    if n_slc < LANE:
        keep = jnp.concatenate([keep, jnp.full((LANE - n_slc, tq), NEG_BIAS, F32)], axis=0)
    sel_ref[0, 0] = jnp.where(valid, keep.T, NEG_BIAS).astype(BF16)


def _nsa_cmp_attn(proj3, small3, tabs, kvc, tq=128):
    b, t, _ = proj3.shape
    g = N_KV_NSA
    nc = t // CMP_STRIDE
    n_slc = t // SLC_LEN
    rw = N_REP * LANE
    tab_spec = pl.BlockSpec((1, tq, LANE), lambda bi, gi, i: (bi, i, 0))
    return pl.pallas_call(
        functools.partial(_cmp_attn_kernel, n_slc=n_slc),
        grid=(b, g, t // tq),
        in_specs=[pl.BlockSpec((1, tq, rw), lambda bi, gi, i: (bi, i, gi)),
                  tab_spec, tab_spec, tab_spec,
                  pl.BlockSpec((1, 1, 1, nc, LANE), lambda bi, gi, i: (bi, 0, gi, 0, 0)),
                  pl.BlockSpec((1, 1, 1, nc, LANE), lambda bi, gi, i: (bi, 1, gi, 0, 0)),
                  pl.BlockSpec((1, tq, LANE), lambda bi, gi, i: (bi, i, 0))],
        out_specs=[pl.BlockSpec((1, tq, rw), lambda bi, gi, i: (bi, i, gi)),
                   pl.BlockSpec((1, 1, tq, LANE), lambda bi, gi, i: (bi, gi, i, 0))],
        out_shape=[jax.ShapeDtypeStruct((b, t, N_HEADS_NSA * LANE), BF16),
                   jax.ShapeDtypeStruct((b, g, t, LANE), BF16)],
        compiler_params=_cparams(("parallel", "parallel", "parallel")),
        name="nsa_cmp",
    )(proj3, *tabs, kvc, kvc, small3)


def _sel_attn_kernel(q_ref, c_ref, sa_ref, sb_ref, cf_ref, saf_ref, sbf_ref, k_ref, v_ref,
                     sel_ref, gate_ref, o_ref, ka_ref, vs_ref, m_ref, l_ref, acc_ref, s0_ref, s1_ref,
                     *, tk):
    g = pl.program_id(1)
    i = pl.program_id(2)
    tq = q_ref.shape[1]
    t = k_ref.shape[1]
    rows = N_REP * tq

    @pl.when(i == 0)
    def _():
        kr = _rope(k_ref[0].astype(F32), cf_ref[0], saf_ref[0], sbf_ref[0])
        ka_ref[:, 0:LANE] = kr.astype(BF16)
        pos = lax.broadcasted_iota(jnp.int32, (t, LANE), 0)
        lane = lax.broadcasted_iota(jnp.int32, (t, LANE), 1)
        ka_ref[:, LANE:2 * LANE] = jnp.where(lane == pos // SLC_LEN, 1.0, 0.0).astype(BF16)
        vs_ref[...] = v_ref[0].astype(BF16)

    q4 = _load_q(q_ref, c_ref[0], sa_ref[0], sb_ref[0])
    bias = sel_ref[0, 0]
    qa = jnp.concatenate([q4, jnp.concatenate([bias] * N_REP, axis=0)], axis=1)

    m_ref[...] = jnp.full((rows, LANE), -1e30, F32)
    l_ref[...] = jnp.zeros((rows, LANE), F32)
    acc_ref[...] = jnp.zeros((rows, LANE), F32)

    half = tk // 2

    def scores(kt, hlf):
        k0 = pl.multiple_of(kt * tk + hlf * half, half)
        return _mm_nt(qa, ka_ref[pl.ds(k0, half), :])

    def update(s_ref, kt, hlf, causal):
        k0 = pl.multiple_of(kt * tk + hlf * half, half)
        s = s_ref[...]
        if causal:
            r = lax.broadcasted_iota(jnp.int32, s.shape, 0)
            kp = k0 + lax.broadcasted_iota(jnp.int32, s.shape, 1)
            s = jnp.where(kp <= i * tq + (r & (tq - 1)), s, -1e30)
        tiles = _lane_tiles(s)
        m_old = m_ref[...]
        m_new = jnp.maximum(m_old, _rowmax(s))
        alpha = jnp.exp(m_old - m_new)
        ps = [jnp.exp(tl - m_new) for tl in tiles]
        l_ref[...] = alpha * l_ref[...] + jnp.sum(functools.reduce(jnp.add, ps), axis=1, keepdims=True)
        p = jnp.concatenate([x.astype(BF16) for x in ps], axis=1)
        acc_ref[...] = alpha * acc_ref[...] + jnp.dot(p, vs_ref[pl.ds(k0, half), :],
                                                      preferred_element_type=F32)
        m_ref[...] = m_new

    diag = (i * tq) // tk
    s0_ref[...] = scores(0, 0)

    def body(kt, carry):
        s1_ref[...] = scores(kt, 1)
        update(s0_ref, kt, 0, False)
        s0_ref[...] = scores(kt + 1, 0)
        update(s1_ref, kt, 1, False)
        return carry

    lax.fori_loop(0, diag, body, 0)
    s1_ref[...] = scores(diag, 1)
    update(s0_ref, diag, 0, True)
    update(s1_ref, diag, 1, True)
    o = acc_ref[...] / l_ref[...]
    _store_gated(o_ref, o, gate_ref, g, 1, tq)


def _nsa_sel_attn(proj3, small3, tabs, selbias, tq=128, tk=512):
    b, t, _ = proj3.shape
    g = N_KV_NSA
    rw = N_REP * LANE
    tk = min(tk, t)
    tab_spec = pl.BlockSpec((1, tq, LANE), lambda bi, gi, i: (bi, i, 0))
    tabf_spec = pl.BlockSpec((1, t, LANE), lambda bi, gi, i: (bi, 0, 0))
    kcb = CB_KV + 1 * 2 * g
    return pl.pallas_call(
        functools.partial(_sel_attn_kernel, tk=tk),
        grid=(b, g, t // tq),
        in_specs=[pl.BlockSpec((1, tq, rw), lambda bi, gi, i: (bi, i, gi)),
                  tab_spec, tab_spec, tab_spec, tabf_spec, tabf_spec, tabf_spec,
                  pl.BlockSpec((1, t, LANE), lambda bi, gi, i: (bi, 0, kcb + gi)),
                  pl.BlockSpec((1, t, LANE), lambda bi, gi, i: (bi, 0, kcb + g + gi)),
                  pl.BlockSpec((1, 1, tq, LANE), lambda bi, gi, i: (bi, gi, i, 0)),
                  pl.BlockSpec((1, tq, LANE), lambda bi, gi, i: (bi, i, 0))],
        out_specs=pl.BlockSpec((1, tq, rw), lambda bi, gi, i: (bi, i, gi)),
        out_shape=jax.ShapeDtypeStruct((b, t, N_HEADS_NSA * LANE), BF16),
        scratch_shapes=[pltpu.VMEM((t, 2 * LANE), BF16), pltpu.VMEM((t, LANE), BF16),
                        pltpu.VMEM((N_REP * tq, LANE), F32), pltpu.VMEM((N_REP * tq, LANE), F32),
                        pltpu.VMEM((N_REP * tq, LANE), F32),
                        pltpu.VMEM((N_REP * tq, tk // 2), F32), pltpu.VMEM((N_REP * tq, tk // 2), F32)],
        compiler_params=_cparams(("parallel", "parallel", "arbitrary")),
        name="nsa_sel",
    )(proj3, *tabs, *tabs, proj3, proj3, selbias, small3)


def _win_attn_kernel(q_ref, c_ref, sa_ref, sb_ref, cf_ref, saf_ref, sbf_ref, k_ref, v_ref,
                     gate_ref, o_ref, ks_ref, vs_ref, *, span):
    g = pl.program_id(1)
    i = pl.program_id(2)
    tq = q_ref.shape[1]

    t = k_ref.shape[1]

    @pl.when(i == 0)
    def _():
        zeros = jnp.zeros((WINDOW, LANE), BF16)
        ks_ref[pl.ds(0, WINDOW), :] = zeros
        vs_ref[pl.ds(0, WINDOW), :] = zeros
        ks_ref[pl.ds(WINDOW, t), :] = _rope(k_ref[0].astype(F32), cf_ref[0], saf_ref[0],
                                            sbf_ref[0]).astype(BF16)
        vs_ref[pl.ds(WINDOW, t), :] = v_ref[0].astype(BF16)

    q4 = _load_q(q_ref, c_ref[0], sa_ref[0], sb_ref[0])
    k0 = pl.multiple_of(i * tq, tq)
    s = _mm_nt(q4, ks_ref[pl.ds(k0, span), :])
    tiles = _lane_tiles(s)
    r = lax.broadcasted_iota(jnp.int32, tiles[0].shape, 0) & (tq - 1)
    c = lax.broadcasted_iota(jnp.int32, tiles[0].shape, 1)
    first_block = WINDOW // tq - i
    masked = []
    for b, tl in enumerate(tiles):
        if b == 0:
            tl = jnp.where(c > r, tl, -1e30)
        if b == len(tiles) - 1:
            tl = jnp.where(c <= r, tl, -1e30)
        else:
            tl = jnp.where(b >= first_block, tl, -1e30)
        masked.append(tl)
    m = jnp.max(functools.reduce(jnp.maximum, masked), axis=1, keepdims=True)
    ps = [jnp.exp(tl - m) for tl in masked]
    l = jnp.sum(functools.reduce(jnp.add, ps), axis=1, keepdims=True)
    p = jnp.concatenate([x.astype(BF16) for x in ps], axis=1)
    o = jnp.dot(p, vs_ref[pl.ds(k0, span), :], preferred_element_type=F32) / l
    _store_gated(o_ref, o, gate_ref, g, 2, tq)


def _nsa_win_attn(proj3, small3, tabs, tq=128):
    b, t, _ = proj3.shape
    g = N_KV_NSA
    rw = N_REP * LANE
    span = WINDOW + tq
    assert tq == LANE
    tab_spec = pl.BlockSpec((1, tq, LANE), lambda bi, gi, i: (bi, i, 0))
    tabf_spec = pl.BlockSpec((1, t, LANE), lambda bi, gi, i: (bi, 0, 0))
    kcb = CB_KV + 2 * 2 * g
    return pl.pallas_call(
        functools.partial(_win_attn_kernel, span=span),
        grid=(b, g, t // tq),
        in_specs=[pl.BlockSpec((1, tq, rw), lambda bi, gi, i: (bi, i, gi)),
                  tab_spec, tab_spec, tab_spec, tabf_spec, tabf_spec, tabf_spec,
                  pl.BlockSpec((1, t, LANE), lambda bi, gi, i: (bi, 0, kcb + gi)),
                  pl.BlockSpec((1, t, LANE), lambda bi, gi, i: (bi, 0, kcb + g + gi)),
                  pl.BlockSpec((1, tq, LANE), lambda bi, gi, i: (bi, i, 0))],
        out_specs=pl.BlockSpec((1, tq, rw), lambda bi, gi, i: (bi, i, gi)),
        out_shape=jax.ShapeDtypeStruct((b, t, N_HEADS_NSA * LANE), BF16),
        scratch_shapes=[pltpu.VMEM((t + WINDOW, LANE), BF16), pltpu.VMEM((t + WINDOW, LANE), BF16)],
        compiler_params=_cparams(("parallel", "parallel", "arbitrary")),
        name="nsa_win",
    )(proj3, *tabs, *tabs, proj3, proj3, small3)


def _gdn_kernel(q_ref, k_ref, v_ref, z_ref, braw_ref, araw_ref, cwq_ref, cwk_ref, cwv_ref, alog_ref,
                dtb_ref, nw_ref, o_ref, q_s, k_s, v_s, o_s, qp_s, op_s, xp_s, bd_s, gcd_s, gl_s, mm_s,
                nn_s):
    t = q_ref.shape[1]
    c = GDN_CHUNK
    nchunk = t // c
    xp_s[pl.ds(0, 8), :] = jnp.zeros((8, LANE), F32)

    def conv_silu(x, cw):
        xp_s[pl.ds(8, t), :] = x
        y = x * cw[CONV_WIDTH - 1:CONV_WIDTH]
        for sft in range(1, CONV_WIDTH):
            y = y + xp_s[pl.ds(8 - sft, t), :] * cw[CONV_WIDTH - 1 - sft:CONV_WIDTH - sft]
        return y * _sigmoid(y)

    def l2n(x):
        return x * lax.rsqrt(jnp.sum(x * x, axis=1, keepdims=True) + EPS)

    q_s[...] = l2n(conv_silu(q_ref[0].astype(F32), cwq_ref[...])) * (HEAD_DIM ** -0.5)
    k_s[...] = l2n(conv_silu(k_ref[0].astype(F32), cwk_ref[...]))
    v_s[...] = conv_silu(v_ref[0].astype(F32), cwv_ref[...])

    bd_s[...] = _sigmoid(braw_ref[0, 0])
    xa = araw_ref[0, 0] + dtb_ref[0]
    softplus = jnp.maximum(xa, 0.0) + jnp.log(1.0 + jnp.exp(-jnp.abs(xa)))
    gc = -jnp.exp(alog_ref[0]) * softplus
    lane = lax.broadcasted_iota(jnp.int32, gc.shape, 1)
    sft = 1
    while sft < c:
        gc = gc + jnp.where((lane & (c - 1)) >= sft, pltpu.roll(gc, sft, 1), 0.0)
        sft *= 2
    gcd_s[...] = gc

    c2 = 2 * c
    ci = lax.broadcasted_iota(jnp.int32, (c2, c2), 0)
    cj = lax.broadcasted_iota(jnp.int32, (c2, c2), 1)
    same = (ci // c) == (cj // c)
    tril = same & (cj <= ci)
    strict = same & (cj < ci)
    eye = jnp.where(ci == cj, 1.0, 0.0)
    first = ci < c

    pairs_per_iter = 8
    rng = range(pairs_per_iter)

    def prep(it, carry):
        n2 = [it * pairs_per_iter + p for p in rng]
        sl = [pl.ds(pl.multiple_of(n * c2, c2), c2) for n in n2]
        grow = [gcd_s[pl.ds(n, 1), :] for n in n2]
        gc2 = [jnp.broadcast_to(g_, (c2, c2)).T for g_ in grow]
        betac = [jnp.broadcast_to(bd_s[pl.ds(n, 1), :], (c2, c2)).T for n in n2]
        kn = [k_s[s_, :] for s_ in sl]
        kbn = [kn[p] * betac[p] for p in rng]
        decay = [jnp.exp(jnp.where(tril, gc2[p] - grow[p], -1e30)) for p in rng]
        kk = [_mm_nt(kbn[p], kn[p]) for p in rng]
        y = [-jnp.where(strict, kk[p] * decay[p], 0.0) for p in rng]
        pm = [eye + y_ for y_ in y]
        pw = 1
        while 2 * pw < c:
            y = [_mm(y_, y_) for y_ in y]
            pm = [pm[p] + _mm(pm[p], y[p]) for p in rng]
            pw *= 2
        egn = [jnp.exp(g_) for g_ in gc2]
        u = [_mm(pm[p], v_s[sl[p], :] * betac[p]) for p in rng]
        w = [_mm(pm[p], kbn[p] * egn[p]) for p in rng]
        qn = [q_s[s_, :] for s_ in sl]
        qk = [_mm_nt(qn[p], kn[p]) for p in rng]
        attn = [jnp.where(tril, qk[p] * decay[p], 0.0) for p in rng]
        aw = [_mm(attn[p], w[p]) for p in rng]
        au = [_mm(attn[p], u[p]) for p in rng]
        for p in rng:
            qp_s[sl[p], :] = qn[p] * egn[p] - aw[p]
            op_s[sl[p], :] = au[p]
        gl = [jnp.where(first, g_[c - 1:c, :], g_[c2 - 1:c2, :]) for g_ in gc2]
        kdt = [(kn[p] * jnp.exp(gl[p] - gc2[p])).T for p in rng]
        ma = [_mm(kdt[p][:, 0:c], w[p][0:c]) for p in rng]
        mb = [_mm(kdt[p][:, c:c2], w[p][c:c2]) for p in rng]
        na = [_mm(kdt[p][:, 0:c], u[p][0:c]) for p in rng]
        nb_ = [_mm(kdt[p][:, c:c2], u[p][c:c2]) for p in rng]
        mba = [_mm(mb[p], ma[p]) for p in rng]
        mbn = [_mm(mb[p], na[p]) for p in rng]
        for p in rng:
            ga = jnp.exp(gc2[p][c - 1:c, :])
            gb = jnp.exp(gc2[p][c2 - 1:c2, :])
            mm_s[2 * n2[p]] = ma[p].astype(BF16)
            nn_s[2 * n2[p]] = na[p]
            gl_s[pl.ds(2 * n2[p], 1), :] = ga
            mm_s[2 * n2[p] + 1] = (gb * ma[p] + ga * mb[p] - mba[p]).astype(BF16)
            nn_s[2 * n2[p] + 1] = gb * na[p] - mbn[p] + nb_[p]
            gl_s[pl.ds(2 * n2[p] + 1, 1), :] = ga * gb
        return carry

    lax.fori_loop(0, nchunk // (2 * pairs_per_iter), prep, 0)

    def second_out(n2, s_mid):
        rows = pl.ds(pl.multiple_of(n2 * c2, c2) + c, c)
        o_s[rows, :] = (jnp.dot(qp_s[rows, :].astype(BF16), s_mid.astype(BF16),
                                 preferred_element_type=F32) + op_s[rows, :])

    def scan(n2, carry):
        s, s_mid_prev = carry
        second_out(jnp.maximum(n2 - 1, 0), s_mid_prev)
        ra = pl.ds(pl.multiple_of(n2 * c2, c2), c)
        sb = s.astype(BF16)
        s_mid = (s * gl_s[pl.ds(2 * n2, 1), :] - jnp.dot(mm_s[2 * n2], sb, preferred_element_type=F32)
                 + nn_s[2 * n2])
        s_new = (s * gl_s[pl.ds(2 * n2 + 1, 1), :]
                 - jnp.dot(mm_s[2 * n2 + 1], sb, preferred_element_type=F32) + nn_s[2 * n2 + 1])
        o_s[ra, :] = jnp.dot(qp_s[ra, :].astype(BF16), sb, preferred_element_type=F32) + op_s[ra, :]
        return s_new, s_mid

    zero_state = jnp.zeros((LANE, LANE), F32)
    _, s_mid_last = lax.fori_loop(0, nchunk // 2, scan, (zero_state, zero_state))
    second_out(jnp.int32(nchunk // 2 - 1), s_mid_last)

    o = o_s[...]
    on = o * lax.rsqrt(jnp.mean(o * o, axis=1, keepdims=True) + EPS) * nw_ref[...]
    z = z_ref[0].astype(F32)
    o_ref[0] = (on * (z * _sigmoid(z))).astype(o_ref.dtype)


def _gdn(proj3, small3, conv_w, a_log, dt_bias, norm_w):
    b, t, _ = proj3.shape
    hh = N_HEADS_GDN
    c = GDN_CHUNK
    col = lambda cb: pl.BlockSpec((1, t, LANE), lambda bi, hi: (bi, 0, cb + hi))
    cw = lambda off: pl.BlockSpec((CONV_WIDTH, LANE), lambda bi, hi: (0, off + hi))
    hrow = pl.BlockSpec((1, 1, LANE), lambda bi, hi: (hi, 0, 0))
    alog_b = jnp.broadcast_to(a_log.astype(F32)[:, None, None], (hh, 1, LANE))
    dtb_b = jnp.broadcast_to(dt_bias.astype(F32)[:, None, None], (hh, 1, LANE))
    big = pltpu.VMEM((t, LANE), F32)
    nrow = t // LANE
    ba = small3[:, :, SMALL_BETA:SMALL_BETA + 2 * hh].transpose(0, 2, 1).reshape(b, 2 * hh, nrow, LANE)
    dense = lambda off: pl.BlockSpec((1, 1, nrow, LANE), lambda bi, hi: (bi, off + hi, 0, 0))
    return pl.pallas_call(
        _gdn_kernel,
        grid=(b, hh),
        in_specs=[col(CB_GQKV), col(CB_GQKV + hh), col(CB_GQKV + 2 * hh), col(CB_Z),
                  dense(0), dense(hh),
                  cw(0), cw(hh), cw(2 * hh), hrow, hrow,
                  pl.BlockSpec((1, LANE), lambda bi, hi: (0, 0))],
        out_specs=pl.BlockSpec((1, t, LANE), lambda bi, hi: (bi, 0, hi)),
        out_shape=jax.ShapeDtypeStruct((b, t, hh * LANE), BF16),
        scratch_shapes=[big, big, big, big, big, big, pltpu.VMEM((t + 8, LANE), F32),
                        pltpu.VMEM((nrow, LANE), F32), pltpu.VMEM((nrow, LANE), F32),
                        pltpu.VMEM((t // c, LANE), F32),
                        pltpu.VMEM((t // c, LANE, LANE), BF16),
                        pltpu.VMEM((t // c, LANE, LANE), F32)],
        compiler_params=_cparams(("parallel", "parallel")),
        name="gdn",
    )(proj3, proj3, proj3, proj3, ba, ba, conv_w, conv_w, conv_w, alog_b, dtb_b,
      norm_w.reshape(1, LANE))


def _split3(a):
    hi = a.astype(BF16)
    lo = (a - hi.astype(F32)).astype(BF16)
    return hi, lo


def _out_proj_kernel(oc_ref, os_ref, ow_ref, ob_ref, x_ref, wo_ref, fw_ref, wr_ref, br_ref,
                     h_ref, hn_ref, ids_ref, wts_ref, tmp_ref):
    half = oc_ref.shape[1]
    oa = (oc_ref[...].astype(F32) + os_ref[...].astype(F32) + ow_ref[...].astype(F32)).astype(BF16)
    h1 = (x_ref[...] + jnp.dot(oa, wo_ref[0:half, :], preferred_element_type=F32)
          + jnp.dot(ob_ref[...], wo_ref[half:2 * half, :], preferred_element_type=F32))
    h_ref[...] = h1
    hn = h1 * lax.rsqrt(jnp.mean(h1 * h1, axis=1, keepdims=True) + EPS) * fw_ref[...]
    _store_token_major(hn_ref, hn, tmp_ref)

    a_hi, a_lo = _split3(hn)
    w_hi, w_lo = _split3(wr_ref[...])
    dot = lambda a, b: jnp.dot(a, b, preferred_element_type=F32)
    logits = dot(a_hi, w_hi) + dot(a_hi, w_lo) + dot(a_lo, w_hi) + br_ref[...]
    lane = lax.broadcasted_iota(jnp.int32, logits.shape, 1)
    big = 1e30
    is_g = lane < N_GROUPS
    lg = jnp.where(is_g, logits, -big)
    gm = jnp.max(lg, axis=1, keepdims=True)
    grp = jnp.min(jnp.where(lg == gm, lane, LANE), axis=1, keepdims=True)
    p_grp = 1.0 / jnp.sum(jnp.where(is_g, jnp.exp(lg - gm), 0.0), axis=1, keepdims=True)
    e_id = lane - N_GROUPS
    in_g = (e_id >= 0) & (e_id < N_EXPERTS) & ((e_id // EXPERTS_PER_GROUP) == grp)
    le = jnp.where(in_g, logits, -big)
    em = jnp.max(le, axis=1, keepdims=True)
    pe = jnp.where(in_g, jnp.exp(le - em), 0.0)
    pe = pe / jnp.sum(pe, axis=1, keepdims=True)
    pm = jnp.where(in_g, pe, -1.0)
    p1 = jnp.max(pm, axis=1, keepdims=True)
    i1 = jnp.min(jnp.where(pm == p1, lane, LANE), axis=1, keepdims=True)
    pm2 = jnp.where(lane == i1, -1.0, pm)
    p2 = jnp.max(pm2, axis=1, keepdims=True)
    i2 = jnp.min(jnp.where(pm2 == p2, lane, LANE), axis=1, keepdims=True)
    den = p1 + p2
    ids_ref[...] = jnp.where(lane == 0, i1 - N_GROUPS, jnp.where(lane == 1, i2 - N_GROUPS, 0))
    wts_ref[...] = jnp.where(lane == 0, p1 / den * p_grp, jnp.where(lane == 1, p2 / den * p_grp, 0.0))


def _out_proj(oc, os_, ow, ob, x2, w_out, ffn_w, wr, br):
    n, d = x2.shape
    half = oc.shape[1]
    seg = d // LANE
    tm = min(256, n)
    row = lambda w: pl.BlockSpec((tm, w), lambda i: (i, 0))
    full = lambda a: pl.BlockSpec(a.shape, lambda i: (0,) * a.ndim)
    fw = ffn_w.reshape(1, d)
    return pl.pallas_call(
        _out_proj_kernel,
        grid=(n // tm,),
        in_specs=[row(half), row(half), row(half), row(half), row(d), full(w_out), full(fw),
                  full(wr), full(br)],
        out_specs=[row(d), pl.BlockSpec((tm * seg, LANE), lambda i: (i, 0)), row(LANE), row(LANE)],
        out_shape=[jax.ShapeDtypeStruct((n, d), F32), jax.ShapeDtypeStruct((n * seg, LANE), BF16),
                   jax.ShapeDtypeStruct((n, LANE), jnp.int32), jax.ShapeDtypeStruct((n, LANE), F32)],
        scratch_shapes=[pltpu.VMEM((tm * seg, LANE), F32)],
        compiler_params=_cparams(("parallel",)),
        name="out_proj",
    )(oc, os_, ow, ob, x2, w_out, fw, wr, br)


def _moe_kernel(bexp_ref, bact_ref, bfirst_ref, bord_ref, bnext_ref, rtok_ref, rtokn_ref, rslotp_ref,
                rslot_ref, hn_hbm, wg_hbm, wu_hbm, wd_hbm, y_hbm, xbuf, ybuf, tmp, wg_st, wu_st, wd_st,
                wgb, wub, wdb, gsem, ssem, wsem, *, n_real):
    i = pl.program_id(0)
    nb = pl.num_programs(0)
    rb = rtok_ref.shape[2]
    seg = ybuf.shape[1] // rb
    slot = i & 1
    other = 1 - slot
    active = bact_ref[i] > 0
    prev_active = jnp.logical_and(i > 0, bact_ref[jnp.maximum(i - 1, 0)] > 0)

    def gather_copy(tok, r, buf):
        return pltpu.make_async_copy(hn_hbm.at[pl.ds(pl.multiple_of(tok, seg), seg), :],
                                     xbuf.at[buf, pl.ds(r * seg, seg), :], gsem.at[buf])

    def scatter_copy(dst, r, buf):
        return pltpu.make_async_copy(ybuf.at[buf, pl.ds(r * seg, seg), :],
                                     y_hbm.at[pl.ds(pl.multiple_of(dst, seg), seg), :], ssem.at[buf])

    def wait_gather(buf):
        pltpu.make_async_copy(hn_hbm.at[pl.ds(0, rb * seg), :], xbuf.at[buf], gsem.at[buf]).wait()

    def wait_scatter(buf):
        pltpu.make_async_copy(ybuf.at[buf], y_hbm.at[pl.ds(0, rb * seg), :], ssem.at[buf]).wait()

    def weight_copies(e, ws):
        return (pltpu.make_async_copy(wg_hbm.at[e], wg_st.at[ws], wsem.at[ws]),
                pltpu.make_async_copy(wu_hbm.at[e], wu_st.at[ws], wsem.at[ws]),
                pltpu.make_async_copy(wd_hbm.at[e], wd_st.at[ws], wsem.at[ws]))

    @pl.when(i == 0)
    def _():
        for cp in weight_copies(bexp_ref[0], 0):
            cp.start()
        ybuf[...] = jnp.zeros(ybuf.shape, ybuf.dtype)
        pltpu.make_async_copy(ybuf.at[0], y_hbm.at[pl.ds(n_real * seg, rb * seg), :],
                              ssem.at[0]).start()

        def first(r, carry):
            gather_copy(rtok_ref[0, 0, r], r, 0).start()
            return carry
        lax.fori_loop(0, rb, first, 0)

    @pl.when(jnp.logical_or(i == 0, prev_active))
    def _():
        wait_gather(slot)

    @pl.when(active)
    def _():
        @pl.when(bfirst_ref[i] > 0)
        def _():
            ws = bord_ref[i] & 1
            for cp in weight_copies(bexp_ref[i], ws):
                cp.wait()

            @pl.when(bnext_ref[i] >= 0)
            def _():
                for cp in weight_copies(bnext_ref[i], 1 - ws):
                    cp.start()

            wgb[...] = wg_st[ws].astype(BF16)
            wub[...] = wu_st[ws].astype(BF16)
            wdb[...] = wd_st[ws].astype(BF16)

        n_piece = 8
        per = rb // n_piece

        def issue(piece):
            for r in range(piece * per, (piece + 1) * per):
                gather_copy(rtokn_ref[0, 0, r], r, other).start()
                scatter_copy(rslotp_ref[0, 0, r], r, other).start()

        x = _load_token_major(xbuf.at[slot], rb, tmp).astype(BF16)
        de = wgb.shape[1]
        d = wdb.shape[1]
        hc = de // 2
        acts = []
        for c in range(2):
            issue(2 * c)
            hg = jnp.dot(x, wgb[:, c * hc:(c + 1) * hc], preferred_element_type=F32)
            issue(2 * c + 1)
            hu = jnp.dot(x, wub[:, c * hc:(c + 1) * hc], preferred_element_type=F32)
            acts.append((hg * _sigmoid(hg) * hu).astype(BF16))
        act = jnp.concatenate(acts, axis=1)
        dc = d // 4
        ys = []
        for j in range(4):
            issue(4 + j)
            ys.append(jnp.dot(act, wdb[:, j * dc:(j + 1) * dc], preferred_element_type=F32))
        wait_scatter(slot)
        _store_token_major(ybuf.at[slot], jnp.concatenate(ys, axis=1), tmp)

    def scatter_all(idx_ref, buf):
        def body(r, carry):
            scatter_copy(idx_ref[0, 0, r], r, buf).start()
            return carry
        lax.fori_loop(0, rb, body, 0)

    @pl.when(jnp.logical_and(jnp.logical_not(active), prev_active))
    def _():
        wait_scatter(slot)
        scatter_all(rslotp_ref, other)
        wait_scatter(other)

    @pl.when(jnp.logical_and(i == nb - 1, active))
    def _():
        wait_scatter(other)
        scatter_all(rslot_ref, slot)
        wait_scatter(slot)
        wait_gather(other)


def _moe(hn, bexp, bact, rtok, rslot, w_gate, w_up, w_down, n_real):
    d, de = w_gate.shape[1], w_gate.shape[2]
    seg = d // LANE
    nb = bexp.shape[0]
    rb = MOE_ROW_BLOCK
    idx = jnp.arange(nb, dtype=jnp.int32)
    prev_e = jnp.concatenate([jnp.full((1,), -1, jnp.int32), bexp[:-1]])
    bfirst = jnp.logical_and(bact > 0, jnp.logical_or(idx == 0, bexp != prev_e)).astype(jnp.int32)
    bord = jnp.cumsum(bfirst) - 1
    first_at = jnp.where(bfirst > 0, idx, nb)
    next_first = jnp.concatenate([lax.cummin(first_at[::-1])[::-1][1:], jnp.full((1,), nb, jnp.int32)])
    bnext = jnp.where(next_first < nb, bexp[jnp.minimum(next_first, nb - 1)], -1).astype(jnp.int32)

    smem_rows = pl.BlockSpec((1, 1, rb), lambda i, *_: (i, 0, 0), memory_space=pltpu.SMEM)
    smem_next = pl.BlockSpec((1, 1, rb), lambda i, *_: (jnp.minimum(i + 1, nb - 1), 0, 0),
                             memory_space=pltpu.SMEM)
    smem_cur = pl.BlockSpec((1, 1, rb), lambda i, *_: (i + 1, 0, 0), memory_space=pltpu.SMEM)
    hbm = pl.BlockSpec(memory_space=pl.ANY)
    grid_spec = pltpu.PrefetchScalarGridSpec(
        num_scalar_prefetch=5,
        grid=(nb,),
        in_specs=[smem_rows, smem_next, smem_rows, smem_cur, hbm, hbm, hbm, hbm],
        out_specs=hbm,
        scratch_shapes=[pltpu.VMEM((2, rb * seg, LANE), BF16), pltpu.VMEM((2, rb * seg, LANE), BF16),
                        pltpu.VMEM((rb * seg, LANE), F32),
                        pltpu.VMEM((2, d, de), F32), pltpu.VMEM((2, d, de), F32), pltpu.VMEM((2, de, d), F32),
                        pltpu.VMEM((d, de), BF16), pltpu.VMEM((d, de), BF16), pltpu.VMEM((de, d), BF16),
                        pltpu.SemaphoreType.DMA((2,)), pltpu.SemaphoreType.DMA((2,)),
                        pltpu.SemaphoreType.DMA((2,))])
    return pl.pallas_call(
        functools.partial(_moe_kernel, n_real=n_real),
        grid_spec=grid_spec,
        out_shape=jax.ShapeDtypeStruct(((n_real + 2 * rb) * seg, LANE), BF16),
        compiler_params=_cparams(("arbitrary",)),
        name="moe",
    )(bexp, bact, bfirst, bord, bnext, rtok, rtok, rslot, rslot, hn, w_gate, w_up, w_down)


def _dispatch(ids, n, seg):
    k = 2
    m = n * k
    rb = MOE_ROW_BLOCK
    e_flat = ids[:, :k].reshape(m)
    onehot = (e_flat[:, None] == jnp.arange(N_EXPERTS, dtype=jnp.int32)[None, :]).astype(jnp.int32)
    csum = jnp.cumsum(onehot, axis=0)
    counts = csum[-1]
    rank = jnp.take_along_axis(csum, e_flat[:, None], axis=1)[:, 0] - 1
    padded = (counts + rb - 1) // rb * rb
    pad_end = jnp.cumsum(padded)
    pad_start = pad_end - padded
    dest = pad_start[e_flat] + rank
    nb = (m + N_EXPERTS * (rb - 1) + rb - 1) // rb
    p = nb * rb
    row_m = jnp.full((p,), -1, jnp.int32).at[dest].set(jnp.arange(m, dtype=jnp.int32))
    real = row_m >= 0
    rtok = jnp.where(real, row_m // k, 0)
    pidx = jnp.arange(p, dtype=jnp.int32)
    rslot = jnp.where(real, (row_m % k) * n + row_m // k, m + ((pidx // rb) % 2) * rb + pidx % rb)
    starts = jnp.arange(nb, dtype=jnp.int32) * rb
    bexp = jnp.minimum(jnp.sum((pad_end[None, :] <= starts[:, None]).astype(jnp.int32), axis=1),
                       N_EXPERTS - 1)
    bact = jnp.sum(real.reshape(nb, rb).astype(jnp.int32), axis=1)
    last_e = jnp.max(jnp.where(bact > 0, bexp, 0))
    bexp = jnp.where(bact > 0, bexp, last_e)
    rslot = jnp.concatenate([m + rb + jnp.arange(rb, dtype=jnp.int32), rslot])
    return (bexp, bact, (rtok * seg).reshape(nb, 1, rb), (rslot * seg).reshape(nb + 1, 1, rb),
            m)


def _combine_kernel(h_ref, y0_ref, y1_ref, wts_ref, fw_ref, o_ref, tmp_ref):
    tm = h_ref.shape[0]
    wts = wts_ref[...]
    y0 = _load_token_major(y0_ref, tm, tmp_ref)
    y1 = _load_token_major(y1_ref, tm, tmp_ref)
    moe = y0 * wts[:, 0:1] + y1 * wts[:, 1:2]
    h = h_ref[...] + moe
    o_ref[...] = h * lax.rsqrt(jnp.mean(h * h, axis=1, keepdims=True) + EPS) * fw_ref[...]


def _combine(h1, y, wts, final_w):
    n, d = h1.shape
    seg = d // LANE
    tm = min(256, n)
    nt = n // tm
    return pl.pallas_call(
        _combine_kernel,
        grid=(nt,),
        in_specs=[pl.BlockSpec((tm, d), lambda i: (i, 0)),
                  pl.BlockSpec((tm * seg, LANE), lambda i: (i, 0)),
                  pl.BlockSpec((tm * seg, LANE), lambda i: (i + nt, 0)),
                  pl.BlockSpec((tm, LANE), lambda i: (i, 0)),
                  pl.BlockSpec((1, d), lambda i: (0, 0))],
        out_specs=pl.BlockSpec((tm, d), lambda i: (i, 0)),
        out_shape=jax.ShapeDtypeStruct((n, d), F32),
        scratch_shapes=[pltpu.VMEM((tm * seg, LANE), F32)],
        compiler_params=_cparams(("parallel",)),
        name="combine",
    )(h1, y, y, wts, final_w.reshape(1, d))


def _rope_tables(positions):
    half = ROT_DIM // 2
    inv_freq = ROPE_THETA ** (-jnp.arange(0, ROT_DIM, 2, dtype=F32) / ROT_DIM)
    ang = positions.astype(F32)[..., None] * inv_freq
    cos, sin = jnp.cos(ang), jnp.sin(ang)
    b, t = positions.shape
    ones = jnp.ones((b, t, LANE - ROT_DIM), F32)
    zeros = jnp.zeros((b, t, LANE - half), F32)
    c = jnp.concatenate([cos, cos, ones], axis=-1)
    sa = jnp.concatenate([-sin, zeros], axis=-1)
    sb = jnp.concatenate([jnp.zeros((b, t, half), F32), sin, zeros[..., :LANE - ROT_DIM]], axis=-1)
    return c, sa, sb


def _arrange_w_in(w_in):
    d = w_in.shape[0]
    sizes = (N_HEADS_NSA * HEAD_DIM, 3 * 2 * N_KV_NSA * HEAD_DIM, 3 * N_HEADS_NSA,
             3 * N_HEADS_GDN * HEAD_DIM, N_HEADS_GDN, N_HEADS_GDN, N_HEADS_GDN * HEAD_DIM)
    offs = np.cumsum((0,) + sizes)
    seg = [w_in[:, offs[i]:offs[i + 1]] for i in range(len(sizes))]
    q, kv, gate, gqkv, gb, ga, gz = seg
    used = sum(sizes)
    pad = jnp.zeros((d, N_CB * LANE - used), w_in.dtype)
    return jnp.concatenate([q, kv, gqkv, gz, gate, gb, ga, pad], axis=1).astype(BF16)


def kernel(x, positions, attn_norm_w, w_in, cmp_wk, cmp_pek, cmp_wv, cmp_pev, gdn_conv_w, gdn_a_log,
           gdn_dt_bias, gdn_norm_w, w_out, ffn_norm_w, router_group_w, router_group_b,
           router_expert_w, router_expert_b, moe_w_gate, moe_w_up, moe_w_down, final_norm_w):
    b, t, d = x.shape
    n = b * t
    tabs = _rope_tables(positions)
    h = x.reshape(n, d)
    assert w_in.shape[0] == 1, "single-layer block only"
    for l in range(1):
        proj, small = _in_proj(h, attn_norm_w[l], _arrange_w_in(w_in[l]))
        proj3 = proj.reshape(b, t, N_CB * LANE)
        small3 = small.reshape(b, t, LANE)
        cmp_w = jnp.stack([cmp_wk[l], cmp_wv[l]])
        cmp_pe = jnp.stack([cmp_pek[l], cmp_pev[l]])
        kvc = _nsa_compress(proj3, tabs, cmp_w, cmp_pe)
        o_c, selbias = _nsa_cmp_attn(proj3, small3, tabs, kvc)
        o_s = _nsa_sel_attn(proj3, small3, tabs, selbias)
        o_w = _nsa_win_attn(proj3, small3, tabs)
        o_b = _gdn(proj3, small3, gdn_conv_w[l], gdn_a_log[l], gdn_dt_bias[l], gdn_norm_w[l])
        half = N_HEADS_NSA * HEAD_DIM
        wr = jnp.concatenate([router_group_w[l], router_expert_w[l],
                              jnp.zeros((d, LANE - N_GROUPS - N_EXPERTS), F32)], axis=1)
        br = jnp.concatenate([router_group_b[l], router_expert_b[l],
                              jnp.zeros((LANE - N_GROUPS - N_EXPERTS,), F32)]).reshape(1, LANE)
        h1, hn2, ids, wts = _out_proj(o_c.reshape(n, half), o_s.reshape(n, half), o_w.reshape(n, half),
                                      o_b.reshape(n, half), h, w_out[l].astype(BF16), ffn_norm_w[l], wr, br)
        bexp, bact, rtok, rslot, n_slots = _dispatch(ids, n, d // LANE)
        y = _moe(hn2, bexp, bact, rtok, rslot, moe_w_gate[l], moe_w_up[l], moe_w_down[l], n_slots)
        out = _combine(h1, y, wts, final_norm_w)
    return out.reshape(b, t, d)
```

```python
import functools

import numpy as np
import jax
import jax.numpy as jnp
from jax import lax
from jax.experimental import pallas as pl
from jax.experimental.pallas import tpu as pltpu

F32 = jnp.float32
BF16 = jnp.bfloat16

HEAD_DIM = 128
N_HEADS_NSA = 8
N_KV_NSA = 2
N_REP = N_HEADS_NSA // N_KV_NSA
N_HEADS_GDN = 8
ROT_DIM = 32
ROPE_THETA = 500000.0
CMP_LEN = 32
CMP_STRIDE = 16
SLC_LEN = 64
SLC_TOP = 16
WINDOW = 512
CONV_WIDTH = 4
GDN_CHUNK = 64
N_GROUPS = 8
EXPERTS_PER_GROUP = 8
N_EXPERTS = 64
MOE_ROW_BLOCK = 256
EPS = 1e-6
LANE = 128

CB_Q = 0
CB_KV = 8
CB_GQKV = 20
CB_Z = 44
CB_SMALL = 52
N_CB = 54
SMALL_BETA = 24
SMALL_DECAY = 32

NEG_BIAS = -32768.0
VMEM_LIMIT = 56 * 1024 * 1024


def _cparams(sem):
    return pltpu.CompilerParams(dimension_semantics=sem, vmem_limit_bytes=VMEM_LIMIT)


def _mm(a, b):
    return jnp.dot(a.astype(BF16), b.astype(BF16), preferred_element_type=F32)


def _mm_nt(a, b):
    return lax.dot_general(a.astype(BF16), b.astype(BF16), (((1,), (1,)), ((), ())),
                           preferred_element_type=F32)


def _rope(x, c, sa, sb):
    return (x * c + pltpu.roll(x, LANE - ROT_DIM // 2, 1) * sa
            + pltpu.roll(x, ROT_DIM // 2, 1) * sb)


def _sigmoid(x):
    return 0.5 * jnp.tanh(0.5 * x) + 0.5


def _lane_tiles(x):
    return [x[:, i:i + LANE] for i in range(0, x.shape[1], LANE)]


def _store_token_major(ref, x, tmp_ref):
    rows, d = x.shape
    seg = d // LANE
    for s in range(seg):
        tmp_ref[pl.ds(s, rows, stride=seg), :] = x[:, s * LANE:(s + 1) * LANE]
    ref[...] = tmp_ref[...].astype(ref.dtype)


def _load_token_major(ref, rows, tmp_ref):
    seg = ref.shape[0] // rows
    tmp_ref[...] = ref[...].astype(F32)
    return jnp.concatenate([tmp_ref[pl.ds(s, rows, stride=seg), :] for s in range(seg)], axis=1)


def _rowmax(x):
    return jnp.max(functools.reduce(jnp.maximum, _lane_tiles(x)), axis=1, keepdims=True)


def _rowsum(x):
    return jnp.sum(functools.reduce(jnp.add, _lane_tiles(x)), axis=1, keepdims=True)


def _in_proj_kernel(x_ref, nw_ref, w_ref, o_ref, small_ref, hn_ref, *, small_off):
    j = pl.program_id(1)

    @pl.when(j == 0)
    def _():
        x = x_ref[...]
        ms = jnp.mean(x * x, axis=-1, keepdims=True)
        hn_ref[...] = (x * lax.rsqrt(ms + EPS) * nw_ref[...]).astype(BF16)
    acc = jnp.dot(hn_ref[...], w_ref[...], preferred_element_type=F32)
    o_ref[...] = acc.astype(o_ref.dtype)

    @pl.when(j == pl.num_programs(1) - 1)
    def _():
        small_ref[...] = acc[:, small_off:small_off + LANE]


def _in_proj(x2, norm_w, w):
    n, d = x2.shape
    ncol = w.shape[1]
    tm = min(1024, n)
    tn = 768
    small_off = CB_SMALL * LANE - (ncol // tn - 1) * tn
    assert 0 <= small_off <= tn - LANE
    return pl.pallas_call(
        functools.partial(_in_proj_kernel, small_off=small_off),
        grid=(n // tm, ncol // tn),
        in_specs=[pl.BlockSpec((tm, d), lambda i, j: (i, 0)),
                  pl.BlockSpec((1, d), lambda i, j: (0, 0)),
                  pl.BlockSpec((d, tn), lambda i, j: (0, j))],
        out_specs=[pl.BlockSpec((tm, tn), lambda i, j: (i, j)),
                   pl.BlockSpec((tm, LANE), lambda i, j: (i, 0))],
        out_shape=[jax.ShapeDtypeStruct((n, ncol), BF16), jax.ShapeDtypeStruct((n, LANE), F32)],
        scratch_shapes=[pltpu.VMEM((tm, d), BF16)],
        compiler_params=_cparams(("parallel", "arbitrary")),
        name="in_proj",
    )(x2, norm_w.reshape(1, d), w)


def _compress_kernel(a_ref, c_ref, sa_ref, sb_ref, w_ref, pe_ref, o_ref, xs_ref):
    kv = pl.program_id(1)
    t = a_ref.shape[1]
    nc = t // CMP_STRIDE
    x = a_ref[0].astype(F32)
    xr = _rope(x, c_ref[0], sa_ref[0], sb_ref[0])
    x = jnp.where(kv == 0, xr, x)
    xs_ref[pl.ds(0, t), :] = x
    xs_ref[pl.ds(t, CMP_STRIDE), :] = jnp.zeros((CMP_STRIDE, LANE), F32)
    acc = jnp.zeros((nc, LANE), F32)
    for l in range(CMP_LEN):
        rows = xs_ref[pl.ds(l, nc, stride=CMP_STRIDE), :] + pe_ref[0, pl.ds(l, 1), :]
        acc = acc + _mm(rows, w_ref[0, l])
    o_ref[0, 0, 0] = acc.astype(BF16)


def _nsa_compress(proj3, tabs, cmp_w, cmp_pe):
    b, t, _ = proj3.shape
    g = N_KV_NSA
    nc = t // CMP_STRIDE
    tab_spec = pl.BlockSpec((1, t, LANE), lambda bi, kv, gi: (bi, 0, 0))
    return pl.pallas_call(
        _compress_kernel,
        grid=(b, 2, g),
        in_specs=[pl.BlockSpec((1, t, LANE), lambda bi, kv, gi: (bi, 0, CB_KV + kv * g + gi)),
                  tab_spec, tab_spec, tab_spec,
                  pl.BlockSpec((1, CMP_LEN, LANE, LANE), lambda bi, kv, gi: (kv, 0, 0, 0)),
                  pl.BlockSpec((1, CMP_LEN, LANE), lambda bi, kv, gi: (kv, 0, 0))],
        out_specs=pl.BlockSpec((1, 1, 1, nc, LANE), lambda bi, kv, gi: (bi, kv, gi, 0, 0)),
        out_shape=jax.ShapeDtypeStruct((b, 2, g, nc, LANE), BF16),
        scratch_shapes=[pltpu.VMEM((t + CMP_STRIDE, LANE), F32)],
        compiler_params=_cparams(("parallel", "arbitrary", "arbitrary")),
        name="nsa_compress",
    )(proj3, *tabs, cmp_w, cmp_pe)


def _load_q(q_ref, c, sa, sb):
    scale = HEAD_DIM ** -0.5
    qs = [(_rope(q_ref[0, :, r * LANE:(r + 1) * LANE].astype(F32), c, sa, sb) * scale).astype(BF16)
          for r in range(N_REP)]
    return jnp.concatenate(qs, axis=0)


def _store_gated(o_ref, o, gate_ref, g, branch, tq):
    gt = gate_ref[0]
    for r in range(N_REP):
        col = ((g * N_REP + r) * 3 + branch)
        lane = lax.broadcasted_iota(jnp.int32, gt.shape, 1)
        gcol = jnp.sum(jnp.where(lane == col, gt, 0.0), axis=1, keepdims=True)
        o_ref[0, :, r * LANE:(r + 1) * LANE] = (o[r * tq:(r + 1) * tq] * _sigmoid(gcol)).astype(o_ref.dtype)


def _cmp_attn_kernel(q_ref, c_ref, sa_ref, sb_ref, kc_ref, vc_ref, gate_ref, o_ref, sel_ref, *, n_slc):
    g = pl.program_id(1)
    i = pl.program_id(2)
    tq = q_ref.shape[1]
    nc = kc_ref.shape[3]
    q4 = _load_q(q_ref, c_ref[0], sa_ref[0], sb_ref[0])
    s = _mm_nt(q4, kc_ref[0, 0, 0])
    row = lax.broadcasted_iota(jnp.int32, s.shape, 0)
    n = lax.broadcasted_iota(jnp.int32, s.shape, 1)
    tpos = i * tq + (row & (tq - 1))
    mask = (n * CMP_STRIDE + (CMP_LEN - 1) <= tpos) & (n < nc - 1)
    sm = jnp.where(mask, s, -1e30)
    m = _rowmax(sm)
    p = jnp.where(mask, jnp.exp(sm - m), 0.0)
    l = _rowsum(p)
    p = p / jnp.maximum(l, 1e-30)
    o = _mm(p, vc_ref[0, 0, 0])
    _store_gated(o_ref, o, gate_ref, g, 0, tq)

    ps = p[0:tq]
    for r in range(1, N_REP):
        ps = ps + p[r * tq:(r + 1) * tq]
    cn = lax.broadcasted_iota(jnp.int32, (nc, LANE), 0)
    cj = lax.broadcasted_iota(jnp.int32, (nc, LANE), 1)
    ratio = SLC_LEN // CMP_STRIDE
    agg = ((cn >= ratio * cj - (CMP_LEN // CMP_STRIDE - 1)) & (cn < ratio * cj + ratio)
           & (cn < nc - 1) & (cj < n_slc))
    agg = jnp.where(agg, 1.0, 0.0).astype(BF16)
    ps_hi = ps.astype(BF16)
    ps_lo = (ps - ps_hi.astype(F32)).astype(BF16)
    imp = (jnp.dot(ps_hi, agg, preferred_element_type=F32)
           + jnp.dot(ps_lo, agg, preferred_element_type=F32))

    j = lax.broadcasted_iota(jnp.int32, (tq, LANE), 1)
    tt = i * tq + lax.broadcasted_iota(jnp.int32, (tq, LANE), 0)
    cur = tt // SLC_LEN
    valid = j <= cur
    forced = (j == 0) | (j == cur) | (j == cur - 1)
    vals = jnp.where(forced, 1e30, jnp.where(valid, imp, -1.0))
    vt = vals.T
    jb = lax.broadcasted_iota(jnp.int32, (n_slc, tq), 0)
    vb = vt[0:n_slc]
    cnt = jnp.zeros((n_slc, tq), F32)
    for jp in range(n_slc):
        cand = vt[jp:jp + 1, :]
        ge = jnp.where(cand >= vb, 1.0, 0.0)
        gt = jnp.where(cand > vb, 1.0, 0.0)
        cnt = cnt + jnp.where(jb > jp, ge, gt)
    keep = jnp.where(cnt < float(min(SLC_TOP, n_slc)), 0.0, NEG_BIAS)
    if n_slc < LANE:
        keep = jnp.concatenate([keep, jnp.full((LANE - n_slc, tq), NEG_BIAS, F32)], axis=0)
    sel_ref[0, 0] = jnp.where(valid, keep.T, NEG_BIAS).astype(BF16)


def _nsa_cmp_attn(proj3, small3, tabs, kvc, tq=128):
    b, t, _ = proj3.shape
    g = N_KV_NSA
    nc = t // CMP_STRIDE
    n_slc = t // SLC_LEN
    rw = N_REP * LANE
    tab_spec = pl.BlockSpec((1, tq, LANE), lambda bi, gi, i: (bi, i, 0))
    return pl.pallas_call(
        functools.partial(_cmp_attn_kernel, n_slc=n_slc),
        grid=(b, g, t // tq),
        in_specs=[pl.BlockSpec((1, tq, rw), lambda bi, gi, i: (bi, i, gi)),
                  tab_spec, tab_spec, tab_spec,
                  pl.BlockSpec((1, 1, 1, nc, LANE), lambda bi, gi, i: (bi, 0, gi, 0, 0)),
                  pl.BlockSpec((1, 1, 1, nc, LANE), lambda bi, gi, i: (bi, 1, gi, 0, 0)),
                  pl.BlockSpec((1, tq, LANE), lambda bi, gi, i: (bi, i, 0))],
        out_specs=[pl.BlockSpec((1, tq, rw), lambda bi, gi, i: (bi, i, gi)),
                   pl.BlockSpec((1, 1, tq, LANE), lambda bi, gi, i: (bi, gi, i, 0))],
        out_shape=[jax.ShapeDtypeStruct((b, t, N_HEADS_NSA * LANE), BF16),
                   jax.ShapeDtypeStruct((b, g, t, LANE), BF16)],
        compiler_params=_cparams(("parallel", "parallel", "parallel")),
        name="nsa_cmp",
    )(proj3, *tabs, kvc, kvc, small3)


def _sel_attn_kernel(q_ref, c_ref, sa_ref, sb_ref, cf_ref, saf_ref, sbf_ref, k_ref, v_ref,
                     sel_ref, gate_ref, o_ref, ka_ref, vs_ref, m_ref, l_ref, acc_ref, s0_ref, s1_ref,
                     *, tk):
    g = pl.program_id(1)
    i = pl.program_id(2)
    tq = q_ref.shape[1]
    t = k_ref.shape[1]
    rows = N_REP * tq

    @pl.when(i == 0)
    def _():
        kr = _rope(k_ref[0].astype(F32), cf_ref[0], saf_ref[0], sbf_ref[0])
        ka_ref[:, 0:LANE] = kr.astype(BF16)
        pos = lax.broadcasted_iota(jnp.int32, (t, LANE), 0)
        lane = lax.broadcasted_iota(jnp.int32, (t, LANE), 1)
        ka_ref[:, LANE:2 * LANE] = jnp.where(lane == pos // SLC_LEN, 1.0, 0.0).astype(BF16)
        vs_ref[...] = v_ref[0].astype(BF16)

    q4 = _load_q(q_ref, c_ref[0], sa_ref[0], sb_ref[0])
    bias = sel_ref[0, 0]
    qa = jnp.concatenate([q4, jnp.concatenate([bias] * N_REP, axis=0)], axis=1)

    m_ref[...] = jnp.full((rows, LANE), -1e30, F32)
    l_ref[...] = jnp.zeros((rows, LANE), F32)
    acc_ref[...] = jnp.zeros((rows, LANE), F32)

    half = tk // 2

    def scores(kt, hlf):
        k0 = pl.multiple_of(kt * tk + hlf * half, half)
        return _mm_nt(qa, ka_ref[pl.ds(k0, half), :])

    def update(s_ref, kt, hlf, causal):
        k0 = pl.multiple_of(kt * tk + hlf * half, half)
        s = s_ref[...]
        if causal:
            r = lax.broadcasted_iota(jnp.int32, s.shape, 0)
            kp = k0 + lax.broadcasted_iota(jnp.int32, s.shape, 1)
            s = jnp.where(kp <= i * tq + (r & (tq - 1)), s, -1e30)
        tiles = _lane_tiles(s)
        m_old = m_ref[...]
        m_new = jnp.maximum(m_old, _rowmax(s))
        alpha = jnp.exp(m_old - m_new)
        ps = [jnp.exp(tl - m_new) for tl in tiles]
        l_ref[...] = alpha * l_ref[...] + jnp.sum(functools.reduce(jnp.add, ps), axis=1, keepdims=True)
        p = jnp.concatenate([x.astype(BF16) for x in ps], axis=1)
        acc_ref[...] = alpha * acc_ref[...] + jnp.dot(p, vs_ref[pl.ds(k0, half), :],
                                                      preferred_element_type=F32)
        m_ref[...] = m_new

    diag = (i * tq) // tk
    s0_ref[...] = scores(0, 0)

    def body(kt, carry):
        s1_ref[...] = scores(kt, 1)
        update(s0_ref, kt, 0, False)
        s0_ref[...] = scores(kt + 1, 0)
        update(s1_ref, kt, 1, False)
        return carry

    lax.fori_loop(0, diag, body, 0)
    s1_ref[...] = scores(diag, 1)
    update(s0_ref, diag, 0, True)
    update(s1_ref, diag, 1, True)
    o = acc_ref[...] / l_ref[...]
    _store_gated(o_ref, o, gate_ref, g, 1, tq)


def _nsa_sel_attn(proj3, small3, tabs, selbias, tq=128, tk=512):
    b, t, _ = proj3.shape
    g = N_KV_NSA
    rw = N_REP * LANE
    tk = min(tk, t)
    tab_spec = pl.BlockSpec((1, tq, LANE), lambda bi, gi, i: (bi, i, 0))
    tabf_spec = pl.BlockSpec((1, t, LANE), lambda bi, gi, i: (bi, 0, 0))
    kcb = CB_KV + 1 * 2 * g
    return pl.pallas_call(
        functools.partial(_sel_attn_kernel, tk=tk),
        grid=(b, g, t // tq),
        in_specs=[pl.BlockSpec((1, tq, rw), lambda bi, gi, i: (bi, i, gi)),
                  tab_spec, tab_spec, tab_spec, tabf_spec, tabf_spec, tabf_spec,
                  pl.BlockSpec((1, t, LANE), lambda bi, gi, i: (bi, 0, kcb + gi)),
                  pl.BlockSpec((1, t, LANE), lambda bi, gi, i: (bi, 0, kcb + g + gi)),
                  pl.BlockSpec((1, 1, tq, LANE), lambda bi, gi, i: (bi, gi, i, 0)),
                  pl.BlockSpec((1, tq, LANE), lambda bi, gi, i: (bi, i, 0))],
        out_specs=pl.BlockSpec((1, tq, rw), lambda bi, gi, i: (bi, i, gi)),
        out_shape=jax.ShapeDtypeStruct((b, t, N_HEADS_NSA * LANE), BF16),
        scratch_shapes=[pltpu.VMEM((t, 2 * LANE), BF16), pltpu.VMEM((t, LANE), BF16),
                        pltpu.VMEM((N_REP * tq, LANE), F32), pltpu.VMEM((N_REP * tq, LANE), F32),
                        pltpu.VMEM((N_REP * tq, LANE), F32),
                        pltpu.VMEM((N_REP * tq, tk // 2), F32), pltpu.VMEM((N_REP * tq, tk // 2), F32)],
        compiler_params=_cparams(("parallel", "parallel", "arbitrary")),
        name="nsa_sel",
    )(proj3, *tabs, *tabs, proj3, proj3, selbias, small3)


def _win_attn_kernel(q_ref, c_ref, sa_ref, sb_ref, cf_ref, saf_ref, sbf_ref, k_ref, v_ref,
                     gate_ref, o_ref, ks_ref, vs_ref, *, span):
    g = pl.program_id(1)
    i = pl.program_id(2)
    tq = q_ref.shape[1]

    t = k_ref.shape[1]

    @pl.when(i == 0)
    def _():
        zeros = jnp.zeros((WINDOW, LANE), BF16)
        ks_ref[pl.ds(0, WINDOW), :] = zeros
        vs_ref[pl.ds(0, WINDOW), :] = zeros
        ks_ref[pl.ds(WINDOW, t), :] = _rope(k_ref[0].astype(F32), cf_ref[0], saf_ref[0],
                                            sbf_ref[0]).astype(BF16)
        vs_ref[pl.ds(WINDOW, t), :] = v_ref[0].astype(BF16)

    q4 = _load_q(q_ref, c_ref[0], sa_ref[0], sb_ref[0])
    k0 = pl.multiple_of(i * tq, tq)
    s = _mm_nt(q4, ks_ref[pl.ds(k0, span), :])
    tiles = _lane_tiles(s)
    r = lax.broadcasted_iota(jnp.int32, tiles[0].shape, 0) & (tq - 1)
    c = lax.broadcasted_iota(jnp.int32, tiles[0].shape, 1)
    first_block = WINDOW // tq - i
    masked = []
    for b, tl in enumerate(tiles):
        if b == 0:
            tl = jnp.where(c > r, tl, -1e30)
        if b == len(tiles) - 1:
            tl = jnp.where(c <= r, tl, -1e30)
        else:
            tl = jnp.where(b >= first_block, tl, -1e30)
        masked.append(tl)
    m = jnp.max(functools.reduce(jnp.maximum, masked), axis=1, keepdims=True)
    ps = [jnp.exp(tl - m) for tl in masked]
    l = jnp.sum(functools.reduce(jnp.add, ps), axis=1, keepdims=True)
    p = jnp.concatenate([x.astype(BF16) for x in ps], axis=1)
    o = jnp.dot(p, vs_ref[pl.ds(k0, span), :], preferred_element_type=F32) / l
    _store_gated(o_ref, o, gate_ref, g, 2, tq)


def _nsa_win_attn(proj3, small3, tabs, tq=128):
    b, t, _ = proj3.shape
    g = N_KV_NSA
    rw = N_REP * LANE
    span = WINDOW + tq
    assert tq == LANE
    tab_spec = pl.BlockSpec((1, tq, LANE), lambda bi, gi, i: (bi, i, 0))
    tabf_spec = pl.BlockSpec((1, t, LANE), lambda bi, gi, i: (bi, 0, 0))
    kcb = CB_KV + 2 * 2 * g
    return pl.pallas_call(
        functools.partial(_win_attn_kernel, span=span),
        grid=(b, g, t // tq),
        in_specs=[pl.BlockSpec((1, tq, rw), lambda bi, gi, i: (bi, i, gi)),
                  tab_spec, tab_spec, tab_spec, tabf_spec, tabf_spec, tabf_spec,
                  pl.BlockSpec((1, t, LANE), lambda bi, gi, i: (bi, 0, kcb + gi)),
                  pl.BlockSpec((1, t, LANE), lambda bi, gi, i: (bi, 0, kcb + g + gi)),
                  pl.BlockSpec((1, tq, LANE), lambda bi, gi, i: (bi, i, 0))],
        out_specs=pl.BlockSpec((1, tq, rw), lambda bi, gi, i: (bi, i, gi)),
        out_shape=jax.ShapeDtypeStruct((b, t, N_HEADS_NSA * LANE), BF16),
        scratch_shapes=[pltpu.VMEM((t + WINDOW, LANE), BF16), pltpu.VMEM((t + WINDOW, LANE), BF16)],
        compiler_params=_cparams(("parallel", "parallel", "arbitrary")),
        name="nsa_win",
    )(proj3, *tabs, *tabs, proj3, proj3, small3)


def _gdn_kernel(q_ref, k_ref, v_ref, z_ref, braw_ref, araw_ref, cwq_ref, cwk_ref, cwv_ref, alog_ref,
                dtb_ref, nw_ref, o_ref, q_s, k_s, v_s, o_s, qp_s, op_s, xp_s, bd_s, gcd_s, gl_s, mm_s,
                nn_s):
    t = q_ref.shape[1]
    c = GDN_CHUNK
    nchunk = t // c
    xp_s[pl.ds(0, 8), :] = jnp.zeros((8, LANE), F32)

    def conv_silu(x, cw):
        xp_s[pl.ds(8, t), :] = x
        y = x * cw[CONV_WIDTH - 1:CONV_WIDTH]
        for sft in range(1, CONV_WIDTH):
            y = y + xp_s[pl.ds(8 - sft, t), :] * cw[CONV_WIDTH - 1 - sft:CONV_WIDTH - sft]
        return y * _sigmoid(y)

    def l2n(x):
        return x * lax.rsqrt(jnp.sum(x * x, axis=1, keepdims=True) + EPS)

    q_s[...] = l2n(conv_silu(q_ref[0].astype(F32), cwq_ref[...])) * (HEAD_DIM ** -0.5)
    k_s[...] = l2n(conv_silu(k_ref[0].astype(F32), cwk_ref[...]))
    v_s[...] = conv_silu(v_ref[0].astype(F32), cwv_ref[...])

    bd_s[...] = _sigmoid(braw_ref[0, 0])
    xa = araw_ref[0, 0] + dtb_ref[0]
    softplus = jnp.maximum(xa, 0.0) + jnp.log(1.0 + jnp.exp(-jnp.abs(xa)))
    gc = -jnp.exp(alog_ref[0]) * softplus
    lane = lax.broadcasted_iota(jnp.int32, gc.shape, 1)
    sft = 1
    while sft < c:
        gc = gc + jnp.where((lane & (c - 1)) >= sft, pltpu.roll(gc, sft, 1), 0.0)
        sft *= 2
    gcd_s[...] = gc

    c2 = 2 * c
    ci = lax.broadcasted_iota(jnp.int32, (c2, c2), 0)
    cj = lax.broadcasted_iota(jnp.int32, (c2, c2), 1)
    same = (ci // c) == (cj // c)
    tril = same & (cj <= ci)
    strict = same & (cj < ci)
    eye = jnp.where(ci == cj, 1.0, 0.0)
    first = ci < c

    pairs_per_iter = 8
    rng = range(pairs_per_iter)

    def prep(it, carry):
        n2 = [it * pairs_per_iter + p for p in rng]
        sl = [pl.ds(pl.multiple_of(n * c2, c2), c2) for n in n2]
        grow = [gcd_s[pl.ds(n, 1), :] for n in n2]
        gc2 = [jnp.broadcast_to(g_, (c2, c2)).T for g_ in grow]
        betac = [jnp.broadcast_to(bd_s[pl.ds(n, 1), :], (c2, c2)).T for n in n2]
        kn = [k_s[s_, :] for s_ in sl]
        kbn = [kn[p] * betac[p] for p in rng]
        decay = [jnp.exp(jnp.where(tril, gc2[p] - grow[p], -1e30)) for p in rng]
        kk = [_mm_nt(kbn[p], kn[p]) for p in rng]
        y = [-jnp.where(strict, kk[p] * decay[p], 0.0) for p in rng]
        pm = [eye + y_ for y_ in y]
        pw = 1
        while 2 * pw < c:
            y = [_mm(y_, y_) for y_ in y]
            pm = [pm[p] + _mm(pm[p], y[p]) for p in rng]
            pw *= 2
        egn = [jnp.exp(g_) for g_ in gc2]
        uw = [_mm(pm[p], jnp.concatenate([v_s[sl[p], :] * betac[p], kbn[p] * egn[p]], axis=1))
              for p in rng]
        qn = [q_s[s_, :] for s_ in sl]
        qk = [_mm_nt(qn[p], kn[p]) for p in rng]
        attn = [jnp.where(tril, qk[p] * decay[p], 0.0) for p in rng]
        auw = [_mm(attn[p], uw[p]) for p in rng]
        for p in rng:
            qp_s[sl[p], :] = qn[p] * egn[p] - auw[p][:, LANE:]
            op_s[sl[p], :] = auw[p][:, :LANE]
        gl = [jnp.where(first, g_[c - 1:c, :], g_[c2 - 1:c2, :]) for g_ in gc2]
        kdt = [(kn[p] * jnp.exp(gl[p] - gc2[p])).T for p in rng]
        nma = [_mm(kdt[p][:, 0:c], uw[p][0:c]) for p in rng]
        nmb = [_mm(kdt[p][:, c:c2], uw[p][c:c2]) for p in rng]
        mb = [x[:, LANE:] for x in nmb]
        mbnm = [_mm(mb[p], nma[p]) for p in rng]
        ma = [x[:, LANE:] for x in nma]
        na = [x[:, :LANE] for x in nma]
        nb_ = [x[:, :LANE] for x in nmb]
        mba = [x[:, LANE:] for x in mbnm]
        mbn = [x[:, :LANE] for x in mbnm]
        for p in rng:
            ga = jnp.exp(gc2[p][c - 1:c, :])
            gb = jnp.exp(gc2[p][c2 - 1:c2, :])
            mm_s[2 * n2[p]] = ma[p].astype(BF16)
            nn_s[2 * n2[p]] = na[p]
            gl_s[pl.ds(2 * n2[p], 1), :] = ga
            mm_s[2 * n2[p] + 1] = (gb * ma[p] + ga * mb[p] - mba[p]).astype(BF16)
            nn_s[2 * n2[p] + 1] = gb * na[p] - mbn[p] + nb_[p]
            gl_s[pl.ds(2 * n2[p] + 1, 1), :] = ga * gb
        return carry

    lax.fori_loop(0, nchunk // (2 * pairs_per_iter), prep, 0)

    def second_out(n2, s_mid):
        rows = pl.ds(pl.multiple_of(n2 * c2, c2) + c, c)
        o_s[rows, :] = (jnp.dot(qp_s[rows, :].astype(BF16), s_mid.astype(BF16),
                                 preferred_element_type=F32) + op_s[rows, :])

    def scan(n2, carry):
        s, s_mid_prev = carry
        second_out(jnp.maximum(n2 - 1, 0), s_mid_prev)
        ra = pl.ds(pl.multiple_of(n2 * c2, c2), c)
        sb = s.astype(BF16)
        s_mid = (s * gl_s[pl.ds(2 * n2, 1), :] - jnp.dot(mm_s[2 * n2], sb, preferred_element_type=F32)
                 + nn_s[2 * n2])
        s_new = (s * gl_s[pl.ds(2 * n2 + 1, 1), :]
                 - jnp.dot(mm_s[2 * n2 + 1], sb, preferred_element_type=F32) + nn_s[2 * n2 + 1])
        o_s[ra, :] = jnp.dot(qp_s[ra, :].astype(BF16), sb, preferred_element_type=F32) + op_s[ra, :]
        return s_new, s_mid

    zero_state = jnp.zeros((LANE, LANE), F32)
    _, s_mid_last = lax.fori_loop(0, nchunk // 2, scan, (zero_state, zero_state))
    second_out(jnp.int32(nchunk // 2 - 1), s_mid_last)

    o = o_s[...]
    on = o * lax.rsqrt(jnp.mean(o * o, axis=1, keepdims=True) + EPS) * nw_ref[...]
    z = z_ref[0].astype(F32)
    o_ref[0] = (on * (z * _sigmoid(z))).astype(o_ref.dtype)


def _gdn(proj3, small3, conv_w, a_log, dt_bias, norm_w):
    b, t, _ = proj3.shape
    hh = N_HEADS_GDN
    c = GDN_CHUNK
    col = lambda cb: pl.BlockSpec((1, t, LANE), lambda bi, hi: (bi, 0, cb + hi))
    cw = lambda off: pl.BlockSpec((CONV_WIDTH, LANE), lambda bi, hi: (0, off + hi))
    hrow = pl.BlockSpec((1, 1, LANE), lambda bi, hi: (hi, 0, 0))
    alog_b = jnp.broadcast_to(a_log.astype(F32)[:, None, None], (hh, 1, LANE))
    dtb_b = jnp.broadcast_to(dt_bias.astype(F32)[:, None, None], (hh, 1, LANE))
    big = pltpu.VMEM((t, LANE), F32)
    nrow = t // LANE
    ba = small3[:, :, SMALL_BETA:SMALL_BETA + 2 * hh].transpose(0, 2, 1).reshape(b, 2 * hh, nrow, LANE)
    dense = lambda off: pl.BlockSpec((1, 1, nrow, LANE), lambda bi, hi: (bi, off + hi, 0, 0))
    return pl.pallas_call(
        _gdn_kernel,
        grid=(b, hh),
        in_specs=[col(CB_GQKV), col(CB_GQKV + hh), col(CB_GQKV + 2 * hh), col(CB_Z),
                  dense(0), dense(hh),
                  cw(0), cw(hh), cw(2 * hh), hrow, hrow,
                  pl.BlockSpec((1, LANE), lambda bi, hi: (0, 0))],
        out_specs=pl.BlockSpec((1, t, LANE), lambda bi, hi: (bi, 0, hi)),
        out_shape=jax.ShapeDtypeStruct((b, t, hh * LANE), BF16),
        scratch_shapes=[big, big, big, big, big, big, pltpu.VMEM((t + 8, LANE), F32),
                        pltpu.VMEM((nrow, LANE), F32), pltpu.VMEM((nrow, LANE), F32),
                        pltpu.VMEM((t // c, LANE), F32),
                        pltpu.VMEM((t // c, LANE, LANE), BF16),
                        pltpu.VMEM((t // c, LANE, LANE), F32)],
        compiler_params=_cparams(("parallel", "parallel")),
        name="gdn",
    )(proj3, proj3, proj3, proj3, ba, ba, conv_w, conv_w, conv_w, alog_b, dtb_b,
      norm_w.reshape(1, LANE))


def _split3(a):
    hi = a.astype(BF16)
    lo = (a - hi.astype(F32)).astype(BF16)
    return hi, lo


def _out_proj_kernel(oc_ref, os_ref, ow_ref, ob_ref, x_ref, wo_ref, fw_ref, wr_ref, br_ref,
                     h_ref, hn_ref, ids_ref, wts_ref, tmp_ref):
    half = oc_ref.shape[1]
    oa = (oc_ref[...].astype(F32) + os_ref[...].astype(F32) + ow_ref[...].astype(F32)).astype(BF16)
    h1 = (x_ref[...] + jnp.dot(oa, wo_ref[0:half, :], preferred_element_type=F32)
          + jnp.dot(ob_ref[...], wo_ref[half:2 * half, :], preferred_element_type=F32))
    h_ref[...] = h1
    hn = h1 * lax.rsqrt(jnp.mean(h1 * h1, axis=1, keepdims=True) + EPS) * fw_ref[...]
    _store_token_major(hn_ref, hn, tmp_ref)

    a_hi, a_lo = _split3(hn)
    w_hi, w_lo = _split3(wr_ref[...])
    dot = lambda a, b: jnp.dot(a, b, preferred_element_type=F32)
    logits = dot(a_hi, w_hi) + dot(a_hi, w_lo) + dot(a_lo, w_hi) + br_ref[...]
    lane = lax.broadcasted_iota(jnp.int32, logits.shape, 1)
    big = 1e30
    is_g = lane < N_GROUPS
    lg = jnp.where(is_g, logits, -big)
    gm = jnp.max(lg, axis=1, keepdims=True)
    grp = jnp.min(jnp.where(lg == gm, lane, LANE), axis=1, keepdims=True)
    p_grp = 1.0 / jnp.sum(jnp.where(is_g, jnp.exp(lg - gm), 0.0), axis=1, keepdims=True)
    e_id = lane - N_GROUPS
    in_g = (e_id >= 0) & (e_id < N_EXPERTS) & ((e_id // EXPERTS_PER_GROUP) == grp)
    le = jnp.where(in_g, logits, -big)
    em = jnp.max(le, axis=1, keepdims=True)
    pe = jnp.where(in_g, jnp.exp(le - em), 0.0)
    pe = pe / jnp.sum(pe, axis=1, keepdims=True)
    pm = jnp.where(in_g, pe, -1.0)
    p1 = jnp.max(pm, axis=1, keepdims=True)
    i1 = jnp.min(jnp.where(pm == p1, lane, LANE), axis=1, keepdims=True)
    pm2 = jnp.where(lane == i1, -1.0, pm)
    p2 = jnp.max(pm2, axis=1, keepdims=True)
    i2 = jnp.min(jnp.where(pm2 == p2, lane, LANE), axis=1, keepdims=True)
    den = p1 + p2
    ids_ref[...] = jnp.where(lane == 0, i1 - N_GROUPS, jnp.where(lane == 1, i2 - N_GROUPS, 0))
    wts_ref[...] = jnp.where(lane == 0, p1 / den * p_grp, jnp.where(lane == 1, p2 / den * p_grp, 0.0))


def _out_proj(oc, os_, ow, ob, x2, w_out, ffn_w, wr, br):
    n, d = x2.shape
    half = oc.shape[1]
    seg = d // LANE
    tm = min(256, n)
    row = lambda w: pl.BlockSpec((tm, w), lambda i: (i, 0))
    full = lambda a: pl.BlockSpec(a.shape, lambda i: (0,) * a.ndim)
    fw = ffn_w.reshape(1, d)
    return pl.pallas_call(
        _out_proj_kernel,
        grid=(n // tm,),
        in_specs=[row(half), row(half), row(half), row(half), row(d), full(w_out), full(fw),
                  full(wr), full(br)],
        out_specs=[row(d), pl.BlockSpec((tm * seg, LANE), lambda i: (i, 0)), row(LANE), row(LANE)],
        out_shape=[jax.ShapeDtypeStruct((n, d), F32), jax.ShapeDtypeStruct((n * seg, LANE), BF16),
                   jax.ShapeDtypeStruct((n, LANE), jnp.int32), jax.ShapeDtypeStruct((n, LANE), F32)],
        scratch_shapes=[pltpu.VMEM((tm * seg, LANE), F32)],
        compiler_params=_cparams(("parallel",)),
        name="out_proj",
    )(oc, os_, ow, ob, x2, w_out, fw, wr, br)


def _moe_kernel(bexp_ref, bact_ref, bfirst_ref, bord_ref, bnext_ref, rtok_ref, rtokn_ref, rslotp_ref,
                rslot_ref, hn_hbm, wg_hbm, wu_hbm, wd_hbm, y_hbm, xbuf, ybuf, tmp, wg_st, wu_st, wd_st,
                wgb, wub, wdb, gsem, ssem, wsem, *, n_real):
    i = pl.program_id(0)
    nb = pl.num_programs(0)
    rb = rtok_ref.shape[2]
    seg = ybuf.shape[1] // rb
    slot = i & 1
    other = 1 - slot
    active = bact_ref[i] > 0
    prev_active = jnp.logical_and(i > 0, bact_ref[jnp.maximum(i - 1, 0)] > 0)

    def gather_copy(tok, r, buf):
        return pltpu.make_async_copy(hn_hbm.at[pl.ds(pl.multiple_of(tok, seg), seg), :],
                                     xbuf.at[buf, pl.ds(r * seg, seg), :], gsem.at[buf])

    def scatter_copy(dst, r, buf):
        return pltpu.make_async_copy(ybuf.at[buf, pl.ds(r * seg, seg), :],
                                     y_hbm.at[pl.ds(pl.multiple_of(dst, seg), seg), :], ssem.at[buf])

    def wait_gather(buf):
        pltpu.make_async_copy(hn_hbm.at[pl.ds(0, rb * seg), :], xbuf.at[buf], gsem.at[buf]).wait()

    def wait_scatter(buf):
        pltpu.make_async_copy(ybuf.at[buf], y_hbm.at[pl.ds(0, rb * seg), :], ssem.at[buf]).wait()

    def weight_copies(e, ws):
        return (pltpu.make_async_copy(wg_hbm.at[e], wg_st.at[ws], wsem.at[ws]),
                pltpu.make_async_copy(wu_hbm.at[e], wu_st.at[ws], wsem.at[ws]),
                pltpu.make_async_copy(wd_hbm.at[e], wd_st.at[ws], wsem.at[ws]))

    @pl.when(i == 0)
    def _():
        for cp in weight_copies(bexp_ref[0], 0):
            cp.start()
        ybuf[...] = jnp.zeros(ybuf.shape, ybuf.dtype)
        pltpu.make_async_copy(ybuf.at[0], y_hbm.at[pl.ds(n_real * seg, rb * seg), :],
                              ssem.at[0]).start()

        def first(r, carry):
            gather_copy(rtok_ref[0, 0, r], r, 0).start()
            return carry
        lax.fori_loop(0, rb, first, 0)

    @pl.when(jnp.logical_or(i == 0, prev_active))
    def _():
        wait_gather(slot)

    @pl.when(active)
    def _():
        @pl.when(bfirst_ref[i] > 0)
        def _():
            ws = bord_ref[i] & 1
            for cp in weight_copies(bexp_ref[i], ws):
                cp.wait()

            @pl.when(bnext_ref[i] >= 0)
            def _():
                for cp in weight_copies(bnext_ref[i], 1 - ws):
                    cp.start()

            wgb[...] = wg_st[ws].astype(BF16)
            wub[...] = wu_st[ws].astype(BF16)
            wdb[...] = wd_st[ws].astype(BF16)

        n_piece = 8
        per = rb // n_piece

        def issue(piece):
            for r in range(piece * per, (piece + 1) * per):
                gather_copy(rtokn_ref[0, 0, r], r, other).start()
                scatter_copy(rslotp_ref[0, 0, r], r, other).start(priority=1)

        x = _load_token_major(xbuf.at[slot], rb, tmp).astype(BF16)
        de = wgb.shape[1]
        d = wdb.shape[1]
        hc = de // 2
        acts = []
        for c in range(2):
            issue(2 * c)
            hg = jnp.dot(x, wgb[:, c * hc:(c + 1) * hc], preferred_element_type=F32)
            issue(2 * c + 1)
            hu = jnp.dot(x, wub[:, c * hc:(c + 1) * hc], preferred_element_type=F32)
            acts.append((hg * _sigmoid(hg) * hu).astype(BF16))
        act = jnp.concatenate(acts, axis=1)
        dc = d // 4
        ys = []
        for j in range(4):
            issue(4 + j)
            ys.append(jnp.dot(act, wdb[:, j * dc:(j + 1) * dc], preferred_element_type=F32))
        wait_scatter(slot)
        _store_token_major(ybuf.at[slot], jnp.concatenate(ys, axis=1), tmp)

    def scatter_all(idx_ref, buf):
        def body(r, carry):
            scatter_copy(idx_ref[0, 0, r], r, buf).start()
            return carry
        lax.fori_loop(0, rb, body, 0)

    @pl.when(jnp.logical_and(jnp.logical_not(active), prev_active))
    def _():
        wait_scatter(slot)
        scatter_all(rslotp_ref, other)
        wait_scatter(other)

    @pl.when(jnp.logical_and(i == nb - 1, active))
    def _():
        wait_scatter(other)
        scatter_all(rslot_ref, slot)
        wait_scatter(slot)
        wait_gather(other)


def _moe(hn, bexp, bact, rtok, rslot, w_gate, w_up, w_down, n_real):
    d, de = w_gate.shape[1], w_gate.shape[2]
    seg = d // LANE
    nb = bexp.shape[0]
    rb = MOE_ROW_BLOCK
    idx = jnp.arange(nb, dtype=jnp.int32)
    prev_e = jnp.concatenate([jnp.full((1,), -1, jnp.int32), bexp[:-1]])
    bfirst = jnp.logical_and(bact > 0, jnp.logical_or(idx == 0, bexp != prev_e)).astype(jnp.int32)
    bord = jnp.cumsum(bfirst) - 1
    first_at = jnp.where(bfirst > 0, idx, nb)
    next_first = jnp.concatenate([lax.cummin(first_at[::-1])[::-1][1:], jnp.full((1,), nb, jnp.int32)])
    bnext = jnp.where(next_first < nb, bexp[jnp.minimum(next_first, nb - 1)], -1).astype(jnp.int32)

    smem_rows = pl.BlockSpec((1, 1, rb), lambda i, *_: (i, 0, 0), memory_space=pltpu.SMEM)
    smem_next = pl.BlockSpec((1, 1, rb), lambda i, *_: (jnp.minimum(i + 1, nb - 1), 0, 0),
                             memory_space=pltpu.SMEM)
    smem_cur = pl.BlockSpec((1, 1, rb), lambda i, *_: (i + 1, 0, 0), memory_space=pltpu.SMEM)
    hbm = pl.BlockSpec(memory_space=pl.ANY)
    grid_spec = pltpu.PrefetchScalarGridSpec(
        num_scalar_prefetch=5,
        grid=(nb,),
        in_specs=[smem_rows, smem_next, smem_rows, smem_cur, hbm, hbm, hbm, hbm],
        out_specs=hbm,
        scratch_shapes=[pltpu.VMEM((2, rb * seg, LANE), BF16), pltpu.VMEM((2, rb * seg, LANE), BF16),
                        pltpu.VMEM((rb * seg, LANE), F32),
                        pltpu.VMEM((2, d, de), F32), pltpu.VMEM((2, d, de), F32), pltpu.VMEM((2, de, d), F32),
                        pltpu.VMEM((d, de), BF16), pltpu.VMEM((d, de), BF16), pltpu.VMEM((de, d), BF16),
                        pltpu.SemaphoreType.DMA((2,)), pltpu.SemaphoreType.DMA((2,)),
                        pltpu.SemaphoreType.DMA((2,))])
    return pl.pallas_call(
        functools.partial(_moe_kernel, n_real=n_real),
        grid_spec=grid_spec,
        out_shape=jax.ShapeDtypeStruct(((n_real + 2 * rb) * seg, LANE), BF16),
        compiler_params=_cparams(("arbitrary",)),
        name="moe",
    )(bexp, bact, bfirst, bord, bnext, rtok, rtok, rslot, rslot, hn, w_gate, w_up, w_down)


def _dispatch(ids, n, seg):
    k = 2
    m = n * k
    rb = MOE_ROW_BLOCK
    e_flat = ids[:, :k].reshape(m)
    onehot = (e_flat[:, None] == jnp.arange(N_EXPERTS, dtype=jnp.int32)[None, :]).astype(jnp.int32)
    csum = jnp.cumsum(onehot, axis=0)
    counts = csum[-1]
    rank = jnp.take_along_axis(csum, e_flat[:, None], axis=1)[:, 0] - 1
    padded = (counts + rb - 1) // rb * rb
    pad_end = jnp.cumsum(padded)
    pad_start = pad_end - padded
    dest = pad_start[e_flat] + rank
    nb = (m + N_EXPERTS * (rb - 1) + rb - 1) // rb
    p = nb * rb
    row_m = jnp.full((p,), -1, jnp.int32).at[dest].set(jnp.arange(m, dtype=jnp.int32))
    real = row_m >= 0
    rtok = jnp.where(real, row_m // k, 0)
    pidx = jnp.arange(p, dtype=jnp.int32)
    rslot = jnp.where(real, (row_m % k) * n + row_m // k, m + ((pidx // rb) % 2) * rb + pidx % rb)
    starts = jnp.arange(nb, dtype=jnp.int32) * rb
    bexp = jnp.minimum(jnp.sum((pad_end[None, :] <= starts[:, None]).astype(jnp.int32), axis=1),
                       N_EXPERTS - 1)
    bact = jnp.sum(real.reshape(nb, rb).astype(jnp.int32), axis=1)
    last_e = jnp.max(jnp.where(bact > 0, bexp, 0))
    bexp = jnp.where(bact > 0, bexp, last_e)
    rslot = jnp.concatenate([m + rb + jnp.arange(rb, dtype=jnp.int32), rslot])
    return (bexp, bact, (rtok * seg).reshape(nb, 1, rb), (rslot * seg).reshape(nb + 1, 1, rb),
            m)


def _combine_kernel(h_ref, y0_ref, y1_ref, wts_ref, fw_ref, o_ref, tmp_ref):
    tm = h_ref.shape[0]
    wts = wts_ref[...]
    y0 = _load_token_major(y0_ref, tm, tmp_ref)
    y1 = _load_token_major(y1_ref, tm, tmp_ref)
    moe = y0 * wts[:, 0:1] + y1 * wts[:, 1:2]
    h = h_ref[...] + moe
    o_ref[...] = h * lax.rsqrt(jnp.mean(h * h, axis=1, keepdims=True) + EPS) * fw_ref[...]


def _combine(h1, y, wts, final_w):
    n, d = h1.shape
    seg = d // LANE
    tm = min(256, n)
    nt = n // tm
    return pl.pallas_call(
        _combine_kernel,
        grid=(nt,),
        in_specs=[pl.BlockSpec((tm, d), lambda i: (i, 0)),
                  pl.BlockSpec((tm * seg, LANE), lambda i: (i, 0)),
                  pl.BlockSpec((tm * seg, LANE), lambda i: (i + nt, 0)),
                  pl.BlockSpec((tm, LANE), lambda i: (i, 0)),
                  pl.BlockSpec((1, d), lambda i: (0, 0))],
        out_specs=pl.BlockSpec((tm, d), lambda i: (i, 0)),
        out_shape=jax.ShapeDtypeStruct((n, d), F32),
        scratch_shapes=[pltpu.VMEM((tm * seg, LANE), F32)],
        compiler_params=_cparams(("parallel",)),
        name="combine",
    )(h1, y, y, wts, final_w.reshape(1, d))


def _rope_tables(positions):
    half = ROT_DIM // 2
    inv_freq = ROPE_THETA ** (-jnp.arange(0, ROT_DIM, 2, dtype=F32) / ROT_DIM)
    ang = positions.astype(F32)[..., None] * inv_freq
    cos, sin = jnp.cos(ang), jnp.sin(ang)
    b, t = positions.shape
    ones = jnp.ones((b, t, LANE - ROT_DIM), F32)
    zeros = jnp.zeros((b, t, LANE - half), F32)
    c = jnp.concatenate([cos, cos, ones], axis=-1)
    sa = jnp.concatenate([-sin, zeros], axis=-1)
    sb = jnp.concatenate([jnp.zeros((b, t, half), F32), sin, zeros[..., :LANE - ROT_DIM]], axis=-1)
    return c, sa, sb


def _arrange_w_in(w_in):
    d = w_in.shape[0]
    sizes = (N_HEADS_NSA * HEAD_DIM, 3 * 2 * N_KV_NSA * HEAD_DIM, 3 * N_HEADS_NSA,
             3 * N_HEADS_GDN * HEAD_DIM, N_HEADS_GDN, N_HEADS_GDN, N_HEADS_GDN * HEAD_DIM)
    offs = np.cumsum((0,) + sizes)
    seg = [w_in[:, offs[i]:offs[i + 1]] for i in range(len(sizes))]
    q, kv, gate, gqkv, gb, ga, gz = seg
    used = sum(sizes)
    pad = jnp.zeros((d, N_CB * LANE - used), w_in.dtype)
    return jnp.concatenate([q, kv, gqkv, gz, gate, gb, ga, pad], axis=1).astype(BF16)


def kernel(x, positions, attn_norm_w, w_in, cmp_wk, cmp_pek, cmp_wv, cmp_pev, gdn_conv_w, gdn_a_log,
           gdn_dt_bias, gdn_norm_w, w_out, ffn_norm_w, router_group_w, router_group_b,
           router_expert_w, router_expert_b, moe_w_gate, moe_w_up, moe_w_down, final_norm_w):
    b, t, d = x.shape
    n = b * t
    tabs = _rope_tables(positions)
    h = x.reshape(n, d)
    assert w_in.shape[0] == 1, "single-layer block only"
    for l in range(1):
        proj, small = _in_proj(h, attn_norm_w[l], _arrange_w_in(w_in[l]))
        proj3 = proj.reshape(b, t, N_CB * LANE)
        small3 = small.reshape(b, t, LANE)
        cmp_w = jnp.stack([cmp_wk[l], cmp_wv[l]])
        cmp_pe = jnp.stack([cmp_pek[l], cmp_pev[l]])
        kvc = _nsa_compress(proj3, tabs, cmp_w, cmp_pe)
        o_c, selbias = _nsa_cmp_attn(proj3, small3, tabs, kvc)
        o_s = _nsa_sel_attn(proj3, small3, tabs, selbias)
        o_w = _nsa_win_attn(proj3, small3, tabs)
        o_b = _gdn(proj3, small3, gdn_conv_w[l], gdn_a_log[l], gdn_dt_bias[l], gdn_norm_w[l])
        half = N_HEADS_NSA * HEAD_DIM
        wr = jnp.concatenate([router_group_w[l], router_expert_w[l],
                              jnp.zeros((d, LANE - N_GROUPS - N_EXPERTS), F32)], axis=1)
        br = jnp.concatenate([router_group_b[l], router_expert_b[l],
                              jnp.zeros((LANE - N_GROUPS - N_EXPERTS,), F32)]).reshape(1, LANE)
        h1, hn2, ids, wts = _out_proj(o_c.reshape(n, half), o_s.reshape(n, half), o_w.reshape(n, half),
                                      o_b.reshape(n, half), h, w_out[l].astype(BF16), ffn_norm_w[l], wr, br)
        bexp, bact, rtok, rslot, n_slots = _dispatch(ids, n, d // LANE)
        y = _moe(hn2, bexp, bact, rtok, rslot, moe_w_gate[l], moe_w_up[l], moe_w_down[l], n_slots)
        out = _combine(h1, y, wts, final_norm_w)
    return out.reshape(b, t, d)
```

```python
import functools

import numpy as np
import jax
import jax.numpy as jnp
from jax import lax
from jax.experimental import pallas as pl
from jax.experimental.pallas import tpu as pltpu

F32 = jnp.float32
BF16 = jnp.bfloat16

HEAD_DIM = 128
N_HEADS_NSA = 8
N_KV_NSA = 2
N_REP = N_HEADS_NSA // N_KV_NSA
N_HEADS_GDN = 8
ROT_DIM = 32
ROPE_THETA = 500000.0
CMP_LEN = 32
CMP_STRIDE = 16
SLC_LEN = 64
SLC_TOP = 16
WINDOW = 512
CONV_WIDTH = 4
GDN_CHUNK = 64
N_GROUPS = 8
EXPERTS_PER_GROUP = 8
N_EXPERTS = 64
MOE_ROW_BLOCK = 256
EPS = 1e-6
LANE = 128

CB_Q = 0
CB_KV = 8
CB_GQKV = 20
CB_Z = 44
CB_SMALL = 52
N_CB = 54
SMALL_BETA = 24
SMALL_DECAY = 32

NEG_BIAS = -32768.0
VMEM_LIMIT = 56 * 1024 * 1024


def _cparams(sem):
    return pltpu.CompilerParams(dimension_semantics=sem, vmem_limit_bytes=VMEM_LIMIT)


def _mm(a, b):
    return jnp.dot(a.astype(BF16), b.astype(BF16), preferred_element_type=F32)


def _mm_nt(a, b):
    return lax.dot_general(a.astype(BF16), b.astype(BF16), (((1,), (1,)), ((), ())),
                           preferred_element_type=F32)


def _rope(x, c, sa, sb):
    return (x * c + pltpu.roll(x, LANE - ROT_DIM // 2, 1) * sa
            + pltpu.roll(x, ROT_DIM // 2, 1) * sb)


def _sigmoid(x):
    return 0.5 * jnp.tanh(0.5 * x) + 0.5


def _lane_tiles(x):
    return [x[:, i:i + LANE] for i in range(0, x.shape[1], LANE)]


def _store_token_major(ref, x, tmp_ref):
    rows, d = x.shape
    seg = d // LANE
    for s in range(seg):
        tmp_ref[pl.ds(s, rows, stride=seg), :] = x[:, s * LANE:(s + 1) * LANE]
    ref[...] = tmp_ref[...].astype(ref.dtype)


def _load_token_major(ref, rows, tmp_ref):
    seg = ref.shape[0] // rows
    tmp_ref[...] = ref[...].astype(F32)
    return jnp.concatenate([tmp_ref[pl.ds(s, rows, stride=seg), :] for s in range(seg)], axis=1)


def _rowmax(x):
    return jnp.max(functools.reduce(jnp.maximum, _lane_tiles(x)), axis=1, keepdims=True)


def _rowsum(x):
    return jnp.sum(functools.reduce(jnp.add, _lane_tiles(x)), axis=1, keepdims=True)


def _in_proj_kernel(x_ref, nw_ref, w_ref, o_ref, small_ref, hn_ref, *, small_off):
    j = pl.program_id(1)

    @pl.when(j == 0)
    def _():
        x = x_ref[...]
        ms = jnp.mean(x * x, axis=-1, keepdims=True)
        hn_ref[...] = (x * lax.rsqrt(ms + EPS) * nw_ref[...]).astype(BF16)
    acc = jnp.dot(hn_ref[...], w_ref[...], preferred_element_type=F32)
    o_ref[...] = acc.astype(o_ref.dtype)

    @pl.when(j == pl.num_programs(1) - 1)
    def _():
        small_ref[...] = acc[:, small_off:small_off + LANE]


def _in_proj(x2, norm_w, w):
    n, d = x2.shape
    ncol = w.shape[1]
    tm = min(1024, n)
    tn = 768
    small_off = CB_SMALL * LANE - (ncol // tn - 1) * tn
    assert 0 <= small_off <= tn - LANE
    return pl.pallas_call(
        functools.partial(_in_proj_kernel, small_off=small_off),
        grid=(n // tm, ncol // tn),
        in_specs=[pl.BlockSpec((tm, d), lambda i, j: (i, 0)),
                  pl.BlockSpec((1, d), lambda i, j: (0, 0)),
                  pl.BlockSpec((d, tn), lambda i, j: (0, j))],
        out_specs=[pl.BlockSpec((tm, tn), lambda i, j: (i, j)),
                   pl.BlockSpec((tm, LANE), lambda i, j: (i, 0))],
        out_shape=[jax.ShapeDtypeStruct((n, ncol), BF16), jax.ShapeDtypeStruct((n, LANE), F32)],
        scratch_shapes=[pltpu.VMEM((tm, d), BF16)],
        compiler_params=_cparams(("parallel", "arbitrary")),
        name="in_proj",
    )(x2, norm_w.reshape(1, d), w)


def _compress_kernel(a_ref, c_ref, sa_ref, sb_ref, w_ref, pe_ref, o_ref, xs_ref):
    kv = pl.program_id(1)
    t = a_ref.shape[1]
    nc = t // CMP_STRIDE
    x = a_ref[0].astype(F32)
    xr = _rope(x, c_ref[0], sa_ref[0], sb_ref[0])
    x = jnp.where(kv == 0, xr, x)
    xs_ref[pl.ds(0, t), :] = x
    xs_ref[pl.ds(t, CMP_STRIDE), :] = jnp.zeros((CMP_STRIDE, LANE), F32)
    acc = jnp.zeros((nc, LANE), F32)
    for l in range(CMP_LEN):
        rows = xs_ref[pl.ds(l, nc, stride=CMP_STRIDE), :] + pe_ref[0, pl.ds(l, 1), :]
        acc = acc + _mm(rows, w_ref[0, l])
    o_ref[0, 0, 0] = acc.astype(BF16)


def _nsa_compress(proj3, tabs, cmp_w, cmp_pe):
    b, t, _ = proj3.shape
    g = N_KV_NSA
    nc = t // CMP_STRIDE
    tab_spec = pl.BlockSpec((1, t, LANE), lambda bi, kv, gi: (bi, 0, 0))
    return pl.pallas_call(
        _compress_kernel,
        grid=(b, 2, g),
        in_specs=[pl.BlockSpec((1, t, LANE), lambda bi, kv, gi: (bi, 0, CB_KV + kv * g + gi)),
                  tab_spec, tab_spec, tab_spec,
                  pl.BlockSpec((1, CMP_LEN, LANE, LANE), lambda bi, kv, gi: (kv, 0, 0, 0)),
                  pl.BlockSpec((1, CMP_LEN, LANE), lambda bi, kv, gi: (kv, 0, 0))],
        out_specs=pl.BlockSpec((1, 1, 1, nc, LANE), lambda bi, kv, gi: (bi, kv, gi, 0, 0)),
        out_shape=jax.ShapeDtypeStruct((b, 2, g, nc, LANE), BF16),
        scratch_shapes=[pltpu.VMEM((t + CMP_STRIDE, LANE), F32)],
        compiler_params=_cparams(("parallel", "arbitrary", "arbitrary")),
        name="nsa_compress",
    )(proj3, *tabs, cmp_w, cmp_pe)


def _load_q(q_ref, c, sa, sb):
    scale = HEAD_DIM ** -0.5
    qs = [(_rope(q_ref[0, :, r * LANE:(r + 1) * LANE].astype(F32), c, sa, sb) * scale).astype(BF16)
          for r in range(N_REP)]
    return jnp.concatenate(qs, axis=0)


def _store_gated(o_ref, o, gate_ref, g, branch, tq):
    gt = gate_ref[0]
    for r in range(N_REP):
        col = ((g * N_REP + r) * 3 + branch)
        lane = lax.broadcasted_iota(jnp.int32, gt.shape, 1)
        gcol = jnp.sum(jnp.where(lane == col, gt, 0.0), axis=1, keepdims=True)
        o_ref[0, :, r * LANE:(r + 1) * LANE] = (o[r * tq:(r + 1) * tq] * _sigmoid(gcol)).astype(o_ref.dtype)


def _cmp_attn_kernel(q_ref, c_ref, sa_ref, sb_ref, kc_ref, vc_ref, gate_ref, o_ref, sel_ref, *, n_slc):
    g = pl.program_id(1)
    i = pl.program_id(2)
    tq = q_ref.shape[1]
    nc = kc_ref.shape[3]
    q4 = _load_q(q_ref, c_ref[0], sa_ref[0], sb_ref[0])
    s = _mm_nt(q4, kc_ref[0, 0, 0])
    row = lax.broadcasted_iota(jnp.int32, s.shape, 0)
    n = lax.broadcasted_iota(jnp.int32, s.shape, 1)
    tpos = i * tq + (row & (tq - 1))
    mask = (n * CMP_STRIDE + (CMP_LEN - 1) <= tpos) & (n < nc - 1)
    sm = jnp.where(mask, s, -1e30)
    m = _rowmax(sm)
    p = jnp.where(mask, jnp.exp(sm - m), 0.0)
    l = _rowsum(p)
    p = p / jnp.maximum(l, 1e-30)
    o = _mm(p, vc_ref[0, 0, 0])
    _store_gated(o_ref, o, gate_ref, g, 0, tq)

    ps = p[0:tq]
    for r in range(1, N_REP):
        ps = ps + p[r * tq:(r + 1) * tq]
    cn = lax.broadcasted_iota(jnp.int32, (nc, LANE), 0)
    cj = lax.broadcasted_iota(jnp.int32, (nc, LANE), 1)
    ratio = SLC_LEN // CMP_STRIDE
    agg = ((cn >= ratio * cj - (CMP_LEN // CMP_STRIDE - 1)) & (cn < ratio * cj + ratio)
           & (cn < nc - 1) & (cj < n_slc))
    agg = jnp.where(agg, 1.0, 0.0).astype(BF16)
    ps_hi = ps.astype(BF16)
    ps_lo = (ps - ps_hi.astype(F32)).astype(BF16)
    imp = (jnp.dot(ps_hi, agg, preferred_element_type=F32)
           + jnp.dot(ps_lo, agg, preferred_element_type=F32))

    j = lax.broadcasted_iota(jnp.int32, (tq, LANE), 1)
    tt = i * tq + lax.broadcasted_iota(jnp.int32, (tq, LANE), 0)
    cur = tt // SLC_LEN
    valid = j <= cur
    forced = (j == 0) | (j == cur) | (j == cur - 1)
    vals = jnp.where(forced, 1e30, jnp.where(valid, imp, -1.0))
    vt = vals.T
    jb = lax.broadcasted_iota(jnp.int32, (n_slc, tq), 0)
    vb = vt[0:n_slc]
    cnt = jnp.zeros((n_slc, tq), F32)
    for jp in range(n_slc):
        cand = vt[jp:jp + 1, :]
        ge = jnp.where(cand >= vb, 1.0, 0.0)
        gt = jnp.where(cand > vb, 1.0, 0.0)
        cnt = cnt + jnp.where(jb > jp, ge, gt)
    keep = jnp.where(cnt < float(min(SLC_TOP, n_slc)), 0.0, NEG_BIAS)
    if n_slc < LANE:
        keep = jnp.concatenate([keep, jnp.full((LANE - n_slc, tq), NEG_BIAS, F32)], axis=0)
    sel_ref[0, 0] = jnp.where(valid, keep.T, NEG_BIAS).astype(BF16)


def _nsa_cmp_attn(proj3, small3, tabs, kvc, tq=128):
    b, t, _ = proj3.shape
    g = N_KV_NSA
    nc = t // CMP_STRIDE
    n_slc = t // SLC_LEN
    rw = N_REP * LANE
    tab_spec = pl.BlockSpec((1, tq, LANE), lambda bi, gi, i: (bi, i, 0))
    return pl.pallas_call(
        functools.partial(_cmp_attn_kernel, n_slc=n_slc),
        grid=(b, g, t // tq),
        in_specs=[pl.BlockSpec((1, tq, rw), lambda bi, gi, i: (bi, i, gi)),
                  tab_spec, tab_spec, tab_spec,
                  pl.BlockSpec((1, 1, 1, nc, LANE), lambda bi, gi, i: (bi, 0, gi, 0, 0)),
                  pl.BlockSpec((1, 1, 1, nc, LANE), lambda bi, gi, i: (bi, 1, gi, 0, 0)),
                  pl.BlockSpec((1, tq, LANE), lambda bi, gi, i: (bi, i, 0))],
        out_specs=[pl.BlockSpec((1, tq, rw), lambda bi, gi, i: (bi, i, gi)),
                   pl.BlockSpec((1, 1, tq, LANE), lambda bi, gi, i: (bi, gi, i, 0))],
        out_shape=[jax.ShapeDtypeStruct((b, t, N_HEADS_NSA * LANE), BF16),
                   jax.ShapeDtypeStruct((b, g, t, LANE), BF16)],
        compiler_params=_cparams(("parallel", "parallel", "parallel")),
        name="nsa_cmp",
    )(proj3, *tabs, kvc, kvc, small3)


def _sel_attn_kernel(q_ref, c_ref, sa_ref, sb_ref, cf_ref, saf_ref, sbf_ref, k_ref, v_ref,
                     sel_ref, gate_ref, o_ref, ka_ref, vs_ref, m_ref, l_ref, acc_ref, s0_ref, s1_ref,
                     *, tk):
    g = pl.program_id(1)
    i = pl.program_id(2)
    tq = q_ref.shape[1]
    t = k_ref.shape[1]
    rows = N_REP * tq

    @pl.when(i == 0)
    def _():
        kr = _rope(k_ref[0].astype(F32), cf_ref[0], saf_ref[0], sbf_ref[0])
        ka_ref[:, 0:LANE] = kr.astype(BF16)
        pos = lax.broadcasted_iota(jnp.int32, (t, LANE), 0)
        lane = lax.broadcasted_iota(jnp.int32, (t, LANE), 1)
        ka_ref[:, LANE:2 * LANE] = jnp.where(lane == pos // SLC_LEN, 1.0, 0.0).astype(BF16)
        vs_ref[...] = v_ref[0].astype(BF16)

    q4 = _load_q(q_ref, c_ref[0], sa_ref[0], sb_ref[0])
    bias = sel_ref[0, 0]
    qa = jnp.concatenate([q4, jnp.concatenate([bias] * N_REP, axis=0)], axis=1)

    m_ref[...] = jnp.full((rows, LANE), -1e30, F32)
    l_ref[...] = jnp.zeros((rows, LANE), F32)
    acc_ref[...] = jnp.zeros((rows, LANE), F32)

    half = tk // 2

    def scores(kt, hlf):
        k0 = pl.multiple_of(kt * tk + hlf * half, half)
        return _mm_nt(qa, ka_ref[pl.ds(k0, half), :])

    def update(s_ref, kt, hlf, causal):
        k0 = pl.multiple_of(kt * tk + hlf * half, half)
        s = s_ref[...]
        if causal:
            r = lax.broadcasted_iota(jnp.int32, s.shape, 0)
            kp = k0 + lax.broadcasted_iota(jnp.int32, s.shape, 1)
            s = jnp.where(kp <= i * tq + (r & (tq - 1)), s, -1e30)
        tiles = _lane_tiles(s)
        m_old = m_ref[...]
        m_new = jnp.maximum(m_old, _rowmax(s))
        alpha = jnp.exp(m_old - m_new)
        ps = [jnp.exp(tl - m_new) for tl in tiles]
        l_ref[...] = alpha * l_ref[...] + jnp.sum(functools.reduce(jnp.add, ps), axis=1, keepdims=True)
        p = jnp.concatenate([x.astype(BF16) for x in ps], axis=1)
        acc_ref[...] = alpha * acc_ref[...] + jnp.dot(p, vs_ref[pl.ds(k0, half), :],
                                                      preferred_element_type=F32)
        m_ref[...] = m_new

    diag = (i * tq) // tk
    s0_ref[...] = scores(0, 0)

    def body(kt, carry):
        s1_ref[...] = scores(kt, 1)
        update(s0_ref, kt, 0, False)
        s0_ref[...] = scores(kt + 1, 0)
        update(s1_ref, kt, 1, False)
        return carry

    lax.fori_loop(0, diag, body, 0)
    s1_ref[...] = scores(diag, 1)
    update(s0_ref, diag, 0, True)
    update(s1_ref, diag, 1, True)
    o = acc_ref[...] / l_ref[...]
    _store_gated(o_ref, o, gate_ref, g, 1, tq)


def _nsa_sel_attn(proj3, small3, tabs, selbias, tq=128, tk=512):
    b, t, _ = proj3.shape
    g = N_KV_NSA
    rw = N_REP * LANE
    tk = min(tk, t)
    tab_spec = pl.BlockSpec((1, tq, LANE), lambda bi, gi, i: (bi, i, 0))
    tabf_spec = pl.BlockSpec((1, t, LANE), lambda bi, gi, i: (bi, 0, 0))
    kcb = CB_KV + 1 * 2 * g
    return pl.pallas_call(
        functools.partial(_sel_attn_kernel, tk=tk),
        grid=(b, g, t // tq),
        in_specs=[pl.BlockSpec((1, tq, rw), lambda bi, gi, i: (bi, i, gi)),
                  tab_spec, tab_spec, tab_spec, tabf_spec, tabf_spec, tabf_spec,
                  pl.BlockSpec((1, t, LANE), lambda bi, gi, i: (bi, 0, kcb + gi)),
                  pl.BlockSpec((1, t, LANE), lambda bi, gi, i: (bi, 0, kcb + g + gi)),
                  pl.BlockSpec((1, 1, tq, LANE), lambda bi, gi, i: (bi, gi, i, 0)),
                  pl.BlockSpec((1, tq, LANE), lambda bi, gi, i: (bi, i, 0))],
        out_specs=pl.BlockSpec((1, tq, rw), lambda bi, gi, i: (bi, i, gi)),
        out_shape=jax.ShapeDtypeStruct((b, t, N_HEADS_NSA * LANE), BF16),
        scratch_shapes=[pltpu.VMEM((t, 2 * LANE), BF16), pltpu.VMEM((t, LANE), BF16),
                        pltpu.VMEM((N_REP * tq, LANE), F32), pltpu.VMEM((N_REP * tq, LANE), F32),
                        pltpu.VMEM((N_REP * tq, LANE), F32),
                        pltpu.VMEM((N_REP * tq, tk // 2), F32), pltpu.VMEM((N_REP * tq, tk // 2), F32)],
        compiler_params=_cparams(("parallel", "parallel", "arbitrary")),
        name="nsa_sel",
    )(proj3, *tabs, *tabs, proj3, proj3, selbias, small3)


def _win_attn_kernel(q_ref, c_ref, sa_ref, sb_ref, cf_ref, saf_ref, sbf_ref, k_ref, v_ref,
                     gate_ref, o_ref, ks_ref, vs_ref, *, span):
    g = pl.program_id(1)
    i = pl.program_id(2)
    tq = q_ref.shape[1]

    t = k_ref.shape[1]

    @pl.when(i == 0)
    def _():
        zeros = jnp.zeros((WINDOW, LANE), BF16)
        ks_ref[pl.ds(0, WINDOW), :] = zeros
        vs_ref[pl.ds(0, WINDOW), :] = zeros
        ks_ref[pl.ds(WINDOW, t), :] = _rope(k_ref[0].astype(F32), cf_ref[0], saf_ref[0],
                                            sbf_ref[0]).astype(BF16)
        vs_ref[pl.ds(WINDOW, t), :] = v_ref[0].astype(BF16)

    q4 = _load_q(q_ref, c_ref[0], sa_ref[0], sb_ref[0])
    k0 = pl.multiple_of(i * tq, tq)
    s = _mm_nt(q4, ks_ref[pl.ds(k0, span), :])
    tiles = _lane_tiles(s)
    r = lax.broadcasted_iota(jnp.int32, tiles[0].shape, 0) & (tq - 1)
    c = lax.broadcasted_iota(jnp.int32, tiles[0].shape, 1)
    first_block = WINDOW // tq - i
    masked = []
    for b, tl in enumerate(tiles):
        if b == 0:
            tl = jnp.where(c > r, tl, -1e30)
        if b == len(tiles) - 1:
            tl = jnp.where(c <= r, tl, -1e30)
        else:
            tl = jnp.where(b >= first_block, tl, -1e30)
        masked.append(tl)
    m = jnp.max(functools.reduce(jnp.maximum, masked), axis=1, keepdims=True)
    ps = [jnp.exp(tl - m) for tl in masked]
    l = jnp.sum(functools.reduce(jnp.add, ps), axis=1, keepdims=True)
    p = jnp.concatenate([x.astype(BF16) for x in ps], axis=1)
    o = jnp.dot(p, vs_ref[pl.ds(k0, span), :], preferred_element_type=F32) / l
    _store_gated(o_ref, o, gate_ref, g, 2, tq)


def _nsa_win_attn(proj3, small3, tabs, tq=128):
    b, t, _ = proj3.shape
    g = N_KV_NSA
    rw = N_REP * LANE
    span = WINDOW + tq
    assert tq == LANE
    tab_spec = pl.BlockSpec((1, tq, LANE), lambda bi, gi, i: (bi, i, 0))
    tabf_spec = pl.BlockSpec((1, t, LANE), lambda bi, gi, i: (bi, 0, 0))
    kcb = CB_KV + 2 * 2 * g
    return pl.pallas_call(
        functools.partial(_win_attn_kernel, span=span),
        grid=(b, g, t // tq),
        in_specs=[pl.BlockSpec((1, tq, rw), lambda bi, gi, i: (bi, i, gi)),
                  tab_spec, tab_spec, tab_spec, tabf_spec, tabf_spec, tabf_spec,
                  pl.BlockSpec((1, t, LANE), lambda bi, gi, i: (bi, 0, kcb + gi)),
                  pl.BlockSpec((1, t, LANE), lambda bi, gi, i: (bi, 0, kcb + g + gi)),
                  pl.BlockSpec((1, tq, LANE), lambda bi, gi, i: (bi, i, 0))],
        out_specs=pl.BlockSpec((1, tq, rw), lambda bi, gi, i: (bi, i, gi)),
        out_shape=jax.ShapeDtypeStruct((b, t, N_HEADS_NSA * LANE), BF16),
        scratch_shapes=[pltpu.VMEM((t + WINDOW, LANE), BF16), pltpu.VMEM((t + WINDOW, LANE), BF16)],
        compiler_params=_cparams(("parallel", "parallel", "arbitrary")),
        name="nsa_win",
    )(proj3, *tabs, *tabs, proj3, proj3, small3)


def _gdn_kernel(q_ref, k_ref, v_ref, z_ref, braw_ref, araw_ref, cwq_ref, cwk_ref, cwv_ref, alog_ref,
                dtb_ref, nw_ref, o_ref, q_s, k_s, v_s, o_s, qp_s, op_s, xp_s, bd_s, gcd_s, gl_s, mm_s,
                nn_s):
    t = q_ref.shape[1]
    c = GDN_CHUNK
    nchunk = t // c
    xp_s[pl.ds(0, 8), :] = jnp.zeros((8, LANE), F32)

    def conv_silu(x, cw):
        xp_s[pl.ds(8, t), :] = x
        y = x * cw[CONV_WIDTH - 1:CONV_WIDTH]
        for sft in range(1, CONV_WIDTH):
            y = y + xp_s[pl.ds(8 - sft, t), :] * cw[CONV_WIDTH - 1 - sft:CONV_WIDTH - sft]
        return y * _sigmoid(y)

    def l2n(x):
        return x * lax.rsqrt(jnp.sum(x * x, axis=1, keepdims=True) + EPS)

    q_s[...] = l2n(conv_silu(q_ref[0].astype(F32), cwq_ref[...])) * (HEAD_DIM ** -0.5)
    k_s[...] = l2n(conv_silu(k_ref[0].astype(F32), cwk_ref[...]))
    v_s[...] = conv_silu(v_ref[0].astype(F32), cwv_ref[...])

    bd_s[...] = _sigmoid(braw_ref[0, 0])
    xa = araw_ref[0, 0] + dtb_ref[0]
    softplus = jnp.maximum(xa, 0.0) + jnp.log(1.0 + jnp.exp(-jnp.abs(xa)))
    gc = -jnp.exp(alog_ref[0]) * softplus
    lane = lax.broadcasted_iota(jnp.int32, gc.shape, 1)
    sft = 1
    while sft < c:
        gc = gc + jnp.where((lane & (c - 1)) >= sft, pltpu.roll(gc, sft, 1), 0.0)
        sft *= 2
    gcd_s[...] = gc

    c2 = 2 * c
    ci = lax.broadcasted_iota(jnp.int32, (c2, c2), 0)
    cj = lax.broadcasted_iota(jnp.int32, (c2, c2), 1)
    same = (ci // c) == (cj // c)
    tril = same & (cj <= ci)
    strict = same & (cj < ci)
    eye = jnp.where(ci == cj, 1.0, 0.0)
    first = ci < c

    pairs_per_iter = 8
    rng = range(pairs_per_iter)

    def prep(it, carry):
        n2 = [it * pairs_per_iter + p for p in rng]
        sl = [pl.ds(pl.multiple_of(n * c2, c2), c2) for n in n2]
        grow = [gcd_s[pl.ds(n, 1), :] for n in n2]
        gc2 = [jnp.broadcast_to(g_, (c2, c2)).T for g_ in grow]
        betac = [jnp.broadcast_to(bd_s[pl.ds(n, 1), :], (c2, c2)).T for n in n2]
        kn = [k_s[s_, :] for s_ in sl]
        kbn = [kn[p] * betac[p] for p in rng]
        decay = [jnp.exp(jnp.where(tril, gc2[p] - grow[p], -1e30)) for p in rng]
        kk = [_mm_nt(kbn[p], kn[p]) for p in rng]
        y = [-jnp.where(strict, kk[p] * decay[p], 0.0) for p in rng]
        pm = [eye + y_ for y_ in y]
        pw = 1
        while 2 * pw < c:
            y = [_mm(y_, y_) for y_ in y]
            pm = [pm[p] + _mm(pm[p], y[p]) for p in rng]
            pw *= 2
        egn = [jnp.exp(g_) for g_ in gc2]
        uw = [_mm(pm[p], jnp.concatenate([v_s[sl[p], :] * betac[p], kbn[p] * egn[p]], axis=1))
              for p in rng]
        qn = [q_s[s_, :] for s_ in sl]
        qk = [_mm_nt(qn[p], kn[p]) for p in rng]
        attn = [jnp.where(tril, qk[p] * decay[p], 0.0) for p in rng]
        auw = [_mm(attn[p], uw[p]) for p in rng]
        for p in rng:
            qp_s[sl[p], :] = qn[p] * egn[p] - auw[p][:, LANE:]
            op_s[sl[p], :] = auw[p][:, :LANE]
        gl = [jnp.where(first, g_[c - 1:c, :], g_[c2 - 1:c2, :]) for g_ in gc2]
        kdt = [(kn[p] * jnp.exp(gl[p] - gc2[p])).T for p in rng]
        nma = [_mm(kdt[p][:, 0:c], uw[p][0:c]) for p in rng]
        nmb = [_mm(kdt[p][:, c:c2], uw[p][c:c2]) for p in rng]
        mb = [x[:, LANE:] for x in nmb]
        mbnm = [_mm(mb[p], nma[p]) for p in rng]
        ma = [x[:, LANE:] for x in nma]
        na = [x[:, :LANE] for x in nma]
        nb_ = [x[:, :LANE] for x in nmb]
        mba = [x[:, LANE:] for x in mbnm]
        mbn = [x[:, :LANE] for x in mbnm]
        for p in rng:
            ga = jnp.exp(gc2[p][c - 1:c, :])
            gb = jnp.exp(gc2[p][c2 - 1:c2, :])
            mm_s[2 * n2[p]] = ma[p].astype(BF16)
            nn_s[2 * n2[p]] = na[p]
            gl_s[pl.ds(2 * n2[p], 1), :] = ga
            mm_s[2 * n2[p] + 1] = (gb * ma[p] + ga * mb[p] - mba[p]).astype(BF16)
            nn_s[2 * n2[p] + 1] = gb * na[p] - mbn[p] + nb_[p]
            gl_s[pl.ds(2 * n2[p] + 1, 1), :] = ga * gb
        return carry

    lax.fori_loop(0, nchunk // (2 * pairs_per_iter), prep, 0)

    def second_out(n2, s_mid):
        rows = pl.ds(pl.multiple_of(n2 * c2, c2) + c, c)
        o_s[rows, :] = (jnp.dot(qp_s[rows, :].astype(BF16), s_mid.astype(BF16),
                                 preferred_element_type=F32) + op_s[rows, :])

    def scan(n2, carry):
        s, s_mid_prev = carry
        second_out(jnp.maximum(n2 - 1, 0), s_mid_prev)
        ra = pl.ds(pl.multiple_of(n2 * c2, c2), c)
        sb = s.astype(BF16)
        s_mid = (s * gl_s[pl.ds(2 * n2, 1), :] - jnp.dot(mm_s[2 * n2], sb, preferred_element_type=F32)
                 + nn_s[2 * n2])
        s_new = (s * gl_s[pl.ds(2 * n2 + 1, 1), :]
                 - jnp.dot(mm_s[2 * n2 + 1], sb, preferred_element_type=F32) + nn_s[2 * n2 + 1])
        o_s[ra, :] = jnp.dot(qp_s[ra, :].astype(BF16), sb, preferred_element_type=F32) + op_s[ra, :]
        return s_new, s_mid

    zero_state = jnp.zeros((LANE, LANE), F32)
    _, s_mid_last = lax.fori_loop(0, nchunk // 2, scan, (zero_state, zero_state))
    second_out(jnp.int32(nchunk // 2 - 1), s_mid_last)

    o = o_s[...]
    on = o * lax.rsqrt(jnp.mean(o * o, axis=1, keepdims=True) + EPS) * nw_ref[...]
    z = z_ref[0].astype(F32)
    o_ref[0] = (on * (z * _sigmoid(z))).astype(o_ref.dtype)


def _gdn(proj3, small3, conv_w, a_log, dt_bias, norm_w):
    b, t, _ = proj3.shape
    hh = N_HEADS_GDN
    c = GDN_CHUNK
    col = lambda cb: pl.BlockSpec((1, t, LANE), lambda bi, hi: (bi, 0, cb + hi))
    cw = lambda off: pl.BlockSpec((CONV_WIDTH, LANE), lambda bi, hi: (0, off + hi))
    hrow = pl.BlockSpec((1, 1, LANE), lambda bi, hi: (hi, 0, 0))
    alog_b = jnp.broadcast_to(a_log.astype(F32)[:, None, None], (hh, 1, LANE))
    dtb_b = jnp.broadcast_to(dt_bias.astype(F32)[:, None, None], (hh, 1, LANE))
    big = pltpu.VMEM((t, LANE), F32)
    nrow = t // LANE
    ba = small3[:, :, SMALL_BETA:SMALL_BETA + 2 * hh].transpose(0, 2, 1).reshape(b, 2 * hh, nrow, LANE)
    dense = lambda off: pl.BlockSpec((1, 1, nrow, LANE), lambda bi, hi: (bi, off + hi, 0, 0))
    return pl.pallas_call(
        _gdn_kernel,
        grid=(b, hh),
        in_specs=[col(CB_GQKV), col(CB_GQKV + hh), col(CB_GQKV + 2 * hh), col(CB_Z),
                  dense(0), dense(hh),
                  cw(0), cw(hh), cw(2 * hh), hrow, hrow,
                  pl.BlockSpec((1, LANE), lambda bi, hi: (0, 0))],
        out_specs=pl.BlockSpec((1, t, LANE), lambda bi, hi: (bi, 0, hi)),
        out_shape=jax.ShapeDtypeStruct((b, t, hh * LANE), BF16),
        scratch_shapes=[big, big, big, big, big, big, pltpu.VMEM((t + 8, LANE), F32),
                        pltpu.VMEM((nrow, LANE), F32), pltpu.VMEM((nrow, LANE), F32),
                        pltpu.VMEM((t // c, LANE), F32),
                        pltpu.VMEM((t // c, LANE, LANE), BF16),
                        pltpu.VMEM((t // c, LANE, LANE), F32)],
        compiler_params=_cparams(("parallel", "parallel")),
        name="gdn",
    )(proj3, proj3, proj3, proj3, ba, ba, conv_w, conv_w, conv_w, alog_b, dtb_b,
      norm_w.reshape(1, LANE))


def _split3(a):
    hi = a.astype(BF16)
    lo = (a - hi.astype(F32)).astype(BF16)
    return hi, lo


def _out_proj_kernel(oc_ref, os_ref, ow_ref, ob_ref, x_ref, wo_ref, fw_ref, wr_ref, br_ref,
                     h_ref, hn_ref, ids_ref, wts_ref, tmp_ref):
    half = oc_ref.shape[1]
    oa = (oc_ref[...].astype(F32) + os_ref[...].astype(F32) + ow_ref[...].astype(F32)).astype(BF16)
    h1 = (x_ref[...] + jnp.dot(oa, wo_ref[0:half, :], preferred_element_type=F32)
          + jnp.dot(ob_ref[...], wo_ref[half:2 * half, :], preferred_element_type=F32))
    h_ref[...] = h1
    hn = h1 * lax.rsqrt(jnp.mean(h1 * h1, axis=1, keepdims=True) + EPS) * fw_ref[...]
    _store_token_major(hn_ref, hn, tmp_ref)

    a_hi, a_lo = _split3(hn)
    w_hi, w_lo = _split3(wr_ref[...])
    dot = lambda a, b: jnp.dot(a, b, preferred_element_type=F32)
    logits = dot(a_hi, w_hi) + dot(a_hi, w_lo) + dot(a_lo, w_hi) + br_ref[...]
    lane = lax.broadcasted_iota(jnp.int32, logits.shape, 1)
    big = 1e30
    is_g = lane < N_GROUPS
    lg = jnp.where(is_g, logits, -big)
    gm = jnp.max(lg, axis=1, keepdims=True)
    grp = jnp.min(jnp.where(lg == gm, lane, LANE), axis=1, keepdims=True)
    p_grp = 1.0 / jnp.sum(jnp.where(is_g, jnp.exp(lg - gm), 0.0), axis=1, keepdims=True)
    e_id = lane - N_GROUPS
    in_g = (e_id >= 0) & (e_id < N_EXPERTS) & ((e_id // EXPERTS_PER_GROUP) == grp)
    le = jnp.where(in_g, logits, -big)
    em = jnp.max(le, axis=1, keepdims=True)
    pe = jnp.where(in_g, jnp.exp(le - em), 0.0)
    pe = pe / jnp.sum(pe, axis=1, keepdims=True)
    pm = jnp.where(in_g, pe, -1.0)
    p1 = jnp.max(pm, axis=1, keepdims=True)
    i1 = jnp.min(jnp.where(pm == p1, lane, LANE), axis=1, keepdims=True)
    pm2 = jnp.where(lane == i1, -1.0, pm)
    p2 = jnp.max(pm2, axis=1, keepdims=True)
    i2 = jnp.min(jnp.where(pm2 == p2, lane, LANE), axis=1, keepdims=True)
    den = p1 + p2
    ids_ref[...] = jnp.where(lane == 0, i1 - N_GROUPS, jnp.where(lane == 1, i2 - N_GROUPS, 0))
    wts_ref[...] = jnp.where(lane == 0, p1 / den * p_grp, jnp.where(lane == 1, p2 / den * p_grp, 0.0))


def _out_proj(oc, os_, ow, ob, x2, w_out, ffn_w, wr, br):
    n, d = x2.shape
    half = oc.shape[1]
    seg = d // LANE
    tm = min(256, n)
    row = lambda w: pl.BlockSpec((tm, w), lambda i: (i, 0))
    full = lambda a: pl.BlockSpec(a.shape, lambda i: (0,) * a.ndim)
    fw = ffn_w.reshape(1, d)
    return pl.pallas_call(
        _out_proj_kernel,
        grid=(n // tm,),
        in_specs=[row(half), row(half), row(half), row(half), row(d), full(w_out), full(fw),
                  full(wr), full(br)],
        out_specs=[row(d), pl.BlockSpec((tm * seg, LANE), lambda i: (i, 0)), row(LANE), row(LANE)],
        out_shape=[jax.ShapeDtypeStruct((n, d), F32), jax.ShapeDtypeStruct((n * seg, LANE), BF16),
                   jax.ShapeDtypeStruct((n, LANE), jnp.int32), jax.ShapeDtypeStruct((n, LANE), F32)],
        scratch_shapes=[pltpu.VMEM((tm * seg, LANE), F32)],
        compiler_params=_cparams(("parallel",)),
        name="out_proj",
    )(oc, os_, ow, ob, x2, w_out, fw, wr, br)


def _moe_kernel(bexp_ref, bact_ref, bfirst_ref, bord_ref, bnext_ref, rtok_ref, rtokn_ref, rslotp_ref,
                rslot_ref, hn_hbm, wg_hbm, wu_hbm, wd_hbm, y_hbm, xbuf, ybuf, tmp, wg_st, wu_st, wd_st,
                wgb, wub, wdb, gsem, ssem, wsem, *, n_real):
    i = pl.program_id(0)
    nb = pl.num_programs(0)
    rb = rtok_ref.shape[2]
    seg = ybuf.shape[1] // rb
    slot = i & 1
    other = 1 - slot
    active = bact_ref[i] > 0
    prev_active = jnp.logical_and(i > 0, bact_ref[jnp.maximum(i - 1, 0)] > 0)

    def gather_copy(tok, r, buf):
        return pltpu.make_async_copy(hn_hbm.at[pl.ds(pl.multiple_of(tok, seg), seg), :],
                                     xbuf.at[buf, pl.ds(r * seg, seg), :], gsem.at[buf])

    def scatter_copy(dst, r, buf):
        return pltpu.make_async_copy(ybuf.at[buf, pl.ds(r * seg, seg), :],
                                     y_hbm.at[pl.ds(pl.multiple_of(dst, seg), seg), :], ssem.at[buf])

    def wait_gather(buf):
        pltpu.make_async_copy(hn_hbm.at[pl.ds(0, rb * seg), :], xbuf.at[buf], gsem.at[buf]).wait()

    def wait_scatter(buf):
        pltpu.make_async_copy(ybuf.at[buf], y_hbm.at[pl.ds(0, rb * seg), :], ssem.at[buf]).wait()

    def weight_copies(e, ws):
        return (pltpu.make_async_copy(wg_hbm.at[e], wg_st.at[ws], wsem.at[ws]),
                pltpu.make_async_copy(wu_hbm.at[e], wu_st.at[ws], wsem.at[ws]),
                pltpu.make_async_copy(wd_hbm.at[e], wd_st.at[ws], wsem.at[ws]))

    @pl.when(i == 0)
    def _():
        for cp in weight_copies(bexp_ref[0], 0):
            cp.start(priority=1)
        ybuf[...] = jnp.zeros(ybuf.shape, ybuf.dtype)
        pltpu.make_async_copy(ybuf.at[0], y_hbm.at[pl.ds(n_real * seg, rb * seg), :],
                              ssem.at[0]).start()

        def first(r, carry):
            gather_copy(rtok_ref[0, 0, r], r, 0).start()
            return carry
        lax.fori_loop(0, rb, first, 0)

    @pl.when(jnp.logical_or(i == 0, prev_active))
    def _():
        wait_gather(slot)

    @pl.when(active)
    def _():
        @pl.when(bfirst_ref[i] > 0)
        def _():
            ws = bord_ref[i] & 1
            for cp in weight_copies(bexp_ref[i], ws):
                cp.wait()

            @pl.when(bnext_ref[i] >= 0)
            def _():
                for cp in weight_copies(bnext_ref[i], 1 - ws):
                    cp.start(priority=1)

            wgb[...] = wg_st[ws].astype(BF16)
            wub[...] = wu_st[ws].astype(BF16)
            wdb[...] = wd_st[ws].astype(BF16)

        n_piece = 8
        per = rb // n_piece

        def issue(piece):
            for r in range(piece * per, (piece + 1) * per):
                gather_copy(rtokn_ref[0, 0, r], r, other).start()
                scatter_copy(rslotp_ref[0, 0, r], r, other).start()

        x = _load_token_major(xbuf.at[slot], rb, tmp).astype(BF16)
        de = wgb.shape[1]
        d = wdb.shape[1]
        hc = de // 2
        acts = []
        for c in range(2):
            issue(2 * c)
            hg = jnp.dot(x, wgb[:, c * hc:(c + 1) * hc], preferred_element_type=F32)
            issue(2 * c + 1)
            hu = jnp.dot(x, wub[:, c * hc:(c + 1) * hc], preferred_element_type=F32)
            acts.append((hg * _sigmoid(hg) * hu).astype(BF16))
        act = jnp.concatenate(acts, axis=1)
        dc = d // 4
        ys = []
        for j in range(4):
            issue(4 + j)
            ys.append(jnp.dot(act, wdb[:, j * dc:(j + 1) * dc], preferred_element_type=F32))
        wait_scatter(slot)
        _store_token_major(ybuf.at[slot], jnp.concatenate(ys, axis=1), tmp)

    def scatter_all(idx_ref, buf):
        def body(r, carry):
            scatter_copy(idx_ref[0, 0, r], r, buf).start()
            return carry
        lax.fori_loop(0, rb, body, 0)

    @pl.when(jnp.logical_and(jnp.logical_not(active), prev_active))
    def _():
        wait_scatter(slot)
        scatter_all(rslotp_ref, other)
        wait_scatter(other)

    @pl.when(jnp.logical_and(i == nb - 1, active))
    def _():
        wait_scatter(other)
        scatter_all(rslot_ref, slot)
        wait_scatter(slot)
        wait_gather(other)


def _moe(hn, bexp, bact, rtok, rslot, w_gate, w_up, w_down, n_real):
    d, de = w_gate.shape[1], w_gate.shape[2]
    seg = d // LANE
    nb = bexp.shape[0]
    rb = MOE_ROW_BLOCK
    idx = jnp.arange(nb, dtype=jnp.int32)
    prev_e = jnp.concatenate([jnp.full((1,), -1, jnp.int32), bexp[:-1]])
    bfirst = jnp.logical_and(bact > 0, jnp.logical_or(idx == 0, bexp != prev_e)).astype(jnp.int32)
    bord = jnp.cumsum(bfirst) - 1
    first_at = jnp.where(bfirst > 0, idx, nb)
    next_first = jnp.concatenate([lax.cummin(first_at[::-1])[::-1][1:], jnp.full((1,), nb, jnp.int32)])
    bnext = jnp.where(next_first < nb, bexp[jnp.minimum(next_first, nb - 1)], -1).astype(jnp.int32)

    smem_rows = pl.BlockSpec((1, 1, rb), lambda i, *_: (i, 0, 0), memory_space=pltpu.SMEM)
    smem_next = pl.BlockSpec((1, 1, rb), lambda i, *_: (jnp.minimum(i + 1, nb - 1), 0, 0),
                             memory_space=pltpu.SMEM)
    smem_cur = pl.BlockSpec((1, 1, rb), lambda i, *_: (i + 1, 0, 0), memory_space=pltpu.SMEM)
    hbm = pl.BlockSpec(memory_space=pl.ANY)
    grid_spec = pltpu.PrefetchScalarGridSpec(
        num_scalar_prefetch=5,
        grid=(nb,),
        in_specs=[smem_rows, smem_next, smem_rows, smem_cur, hbm, hbm, hbm, hbm],
        out_specs=hbm,
        scratch_shapes=[pltpu.VMEM((2, rb * seg, LANE), BF16), pltpu.VMEM((2, rb * seg, LANE), BF16),
                        pltpu.VMEM((rb * seg, LANE), F32),
                        pltpu.VMEM((2, d, de), F32), pltpu.VMEM((2, d, de), F32), pltpu.VMEM((2, de, d), F32),
                        pltpu.VMEM((d, de), BF16), pltpu.VMEM((d, de), BF16), pltpu.VMEM((de, d), BF16),
                        pltpu.SemaphoreType.DMA((2,)), pltpu.SemaphoreType.DMA((2,)),
                        pltpu.SemaphoreType.DMA((2,))])
    return pl.pallas_call(
        functools.partial(_moe_kernel, n_real=n_real),
        grid_spec=grid_spec,
        out_shape=jax.ShapeDtypeStruct(((n_real + 2 * rb) * seg, LANE), BF16),
        compiler_params=_cparams(("arbitrary",)),
        name="moe",
    )(bexp, bact, bfirst, bord, bnext, rtok, rtok, rslot, rslot, hn, w_gate, w_up, w_down)


def _dispatch(ids, n, seg):
    k = 2
    m = n * k
    rb = MOE_ROW_BLOCK
    e_flat = ids[:, :k].reshape(m)
    onehot = (e_flat[:, None] == jnp.arange(N_EXPERTS, dtype=jnp.int32)[None, :]).astype(jnp.int32)
    csum = jnp.cumsum(onehot, axis=0)
    counts = csum[-1]
    rank = jnp.take_along_axis(csum, e_flat[:, None], axis=1)[:, 0] - 1
    padded = (counts + rb - 1) // rb * rb
    pad_end = jnp.cumsum(padded)
    pad_start = pad_end - padded
    dest = pad_start[e_flat] + rank
    nb = (m + N_EXPERTS * (rb - 1) + rb - 1) // rb
    p = nb * rb
    row_m = jnp.full((p,), -1, jnp.int32).at[dest].set(jnp.arange(m, dtype=jnp.int32))
    real = row_m >= 0
    rtok = jnp.where(real, row_m // k, 0)
    pidx = jnp.arange(p, dtype=jnp.int32)
    rslot = jnp.where(real, (row_m % k) * n + row_m // k, m + ((pidx // rb) % 2) * rb + pidx % rb)
    starts = jnp.arange(nb, dtype=jnp.int32) * rb
    bexp = jnp.minimum(jnp.sum((pad_end[None, :] <= starts[:, None]).astype(jnp.int32), axis=1),
                       N_EXPERTS - 1)
    bact = jnp.sum(real.reshape(nb, rb).astype(jnp.int32), axis=1)
    last_e = jnp.max(jnp.where(bact > 0, bexp, 0))
    bexp = jnp.where(bact > 0, bexp, last_e)
    rslot = jnp.concatenate([m + rb + jnp.arange(rb, dtype=jnp.int32), rslot])
    return (bexp, bact, (rtok * seg).reshape(nb, 1, rb), (rslot * seg).reshape(nb + 1, 1, rb),
            m)


def _combine_kernel(h_ref, y0_ref, y1_ref, wts_ref, fw_ref, o_ref, tmp_ref):
    tm = h_ref.shape[0]
    wts = wts_ref[...]
    y0 = _load_token_major(y0_ref, tm, tmp_ref)
    y1 = _load_token_major(y1_ref, tm, tmp_ref)
    moe = y0 * wts[:, 0:1] + y1 * wts[:, 1:2]
    h = h_ref[...] + moe
    o_ref[...] = h * lax.rsqrt(jnp.mean(h * h, axis=1, keepdims=True) + EPS) * fw_ref[...]


def _combine(h1, y, wts, final_w):
    n, d = h1.shape
    seg = d // LANE
    tm = min(256, n)
    nt = n // tm
    return pl.pallas_call(
        _combine_kernel,
        grid=(nt,),
        in_specs=[pl.BlockSpec((tm, d), lambda i: (i, 0)),
                  pl.BlockSpec((tm * seg, LANE), lambda i: (i, 0)),
                  pl.BlockSpec((tm * seg, LANE), lambda i: (i + nt, 0)),
                  pl.BlockSpec((tm, LANE), lambda i: (i, 0)),
                  pl.BlockSpec((1, d), lambda i: (0, 0))],
        out_specs=pl.BlockSpec((tm, d), lambda i: (i, 0)),
        out_shape=jax.ShapeDtypeStruct((n, d), F32),
        scratch_shapes=[pltpu.VMEM((tm * seg, LANE), F32)],
        compiler_params=_cparams(("parallel",)),
        name="combine",
    )(h1, y, y, wts, final_w.reshape(1, d))


def _rope_tables(positions):
    half = ROT_DIM // 2
    inv_freq = ROPE_THETA ** (-jnp.arange(0, ROT_DIM, 2, dtype=F32) / ROT_DIM)
    ang = positions.astype(F32)[..., None] * inv_freq
    cos, sin = jnp.cos(ang), jnp.sin(ang)
    b, t = positions.shape
    ones = jnp.ones((b, t, LANE - ROT_DIM), F32)
    zeros = jnp.zeros((b, t, LANE - half), F32)
    c = jnp.concatenate([cos, cos, ones], axis=-1)
    sa = jnp.concatenate([-sin, zeros], axis=-1)
    sb = jnp.concatenate([jnp.zeros((b, t, half), F32), sin, zeros[..., :LANE - ROT_DIM]], axis=-1)
    return c, sa, sb


def _arrange_w_in(w_in):
    d = w_in.shape[0]
    sizes = (N_HEADS_NSA * HEAD_DIM, 3 * 2 * N_KV_NSA * HEAD_DIM, 3 * N_HEADS_NSA,
             3 * N_HEADS_GDN * HEAD_DIM, N_HEADS_GDN, N_HEADS_GDN, N_HEADS_GDN * HEAD_DIM)
    offs = np.cumsum((0,) + sizes)
    seg = [w_in[:, offs[i]:offs[i + 1]] for i in range(len(sizes))]
    q, kv, gate, gqkv, gb, ga, gz = seg
    used = sum(sizes)
    pad = jnp.zeros((d, N_CB * LANE - used), w_in.dtype)
    return jnp.concatenate([q, kv, gqkv, gz, gate, gb, ga, pad], axis=1).astype(BF16)


def kernel(x, positions, attn_norm_w, w_in, cmp_wk, cmp_pek, cmp_wv, cmp_pev, gdn_conv_w, gdn_a_log,
           gdn_dt_bias, gdn_norm_w, w_out, ffn_norm_w, router_group_w, router_group_b,
           router_expert_w, router_expert_b, moe_w_gate, moe_w_up, moe_w_down, final_norm_w):
    b, t, d = x.shape
    n = b * t
    tabs = _rope_tables(positions)
    h = x.reshape(n, d)
    assert w_in.shape[0] == 1, "single-layer block only"
    for l in range(1):
        proj, small = _in_proj(h, attn_norm_w[l], _arrange_w_in(w_in[l]))
        proj3 = proj.reshape(b, t, N_CB * LANE)
        small3 = small.reshape(b, t, LANE)
        cmp_w = jnp.stack([cmp_wk[l], cmp_wv[l]])
        cmp_pe = jnp.stack([cmp_pek[l], cmp_pev[l]])
        kvc = _nsa_compress(proj3, tabs, cmp_w, cmp_pe)
        o_c, selbias = _nsa_cmp_attn(proj3, small3, tabs, kvc)
        o_s = _nsa_sel_attn(proj3, small3, tabs, selbias)
        o_w = _nsa_win_attn(proj3, small3, tabs)
        o_b = _gdn(proj3, small3, gdn_conv_w[l], gdn_a_log[l], gdn_dt_bias[l], gdn_norm_w[l])
        half = N_HEADS_NSA * HEAD_DIM
        wr = jnp.concatenate([router_group_w[l], router_expert_w[l],
                              jnp.zeros((d, LANE - N_GROUPS - N_EXPERTS), F32)], axis=1)
        br = jnp.concatenate([router_group_b[l], router_expert_b[l],
                              jnp.zeros((LANE - N_GROUPS - N_EXPERTS,), F32)]).reshape(1, LANE)
        h1, hn2, ids, wts = _out_proj(o_c.reshape(n, half), o_s.reshape(n, half), o_w.reshape(n, half),
                                      o_b.reshape(n, half), h, w_out[l].astype(BF16), ffn_norm_w[l], wr, br)
        bexp, bact, rtok, rslot, n_slots = _dispatch(ids, n, d // LANE)
        y = _moe(hn2, bexp, bact, rtok, rslot, moe_w_gate[l], moe_w_up[l], moe_w_down[l], n_slots)
        out = _combine(h1, y, wts, final_norm_w)
    return out.reshape(b, t, d)
```

```python
import functools

import numpy as np
import jax
import jax.numpy as jnp
from jax import lax
from jax.experimental import pallas as pl
from jax.experimental.pallas import tpu as pltpu

F32 = jnp.float32
BF16 = jnp.bfloat16

HEAD_DIM = 128
N_HEADS_NSA = 8
N_KV_NSA = 2
N_REP = N_HEADS_NSA // N_KV_NSA
N_HEADS_GDN = 8
ROT_DIM = 32
ROPE_THETA = 500000.0
CMP_LEN = 32
CMP_STRIDE = 16
SLC_LEN = 64
SLC_TOP = 16
WINDOW = 512
CONV_WIDTH = 4
GDN_CHUNK = 64
N_GROUPS = 8
EXPERTS_PER_GROUP = 8
N_EXPERTS = 64
MOE_ROW_BLOCK = 256
EPS = 1e-6
LANE = 128

CB_Q = 0
CB_KV = 8
CB_GQKV = 20
CB_Z = 44
CB_SMALL = 52
N_CB = 54
SMALL_BETA = 24
SMALL_DECAY = 32

NEG_BIAS = -32768.0
VMEM_LIMIT = 56 * 1024 * 1024


def _cparams(sem):
    return pltpu.CompilerParams(dimension_semantics=sem, vmem_limit_bytes=VMEM_LIMIT)


def _mm(a, b):
    return jnp.dot(a.astype(BF16), b.astype(BF16), preferred_element_type=F32)


def _mm_nt(a, b):
    return lax.dot_general(a.astype(BF16), b.astype(BF16), (((1,), (1,)), ((), ())),
                           preferred_element_type=F32)


def _rope(x, c, sa, sb):
    return (x * c + pltpu.roll(x, LANE - ROT_DIM // 2, 1) * sa
            + pltpu.roll(x, ROT_DIM // 2, 1) * sb)


def _sigmoid(x):
    return 0.5 * jnp.tanh(0.5 * x) + 0.5


def _lane_tiles(x):
    return [x[:, i:i + LANE] for i in range(0, x.shape[1], LANE)]


def _store_token_major(ref, x, tmp_ref):
    rows, d = x.shape
    seg = d // LANE
    for s in range(seg):
        tmp_ref[pl.ds(s, rows, stride=seg), :] = x[:, s * LANE:(s + 1) * LANE]
    ref[...] = tmp_ref[...].astype(ref.dtype)


def _load_token_major(ref, rows, tmp_ref):
    seg = ref.shape[0] // rows
    tmp_ref[...] = ref[...].astype(F32)
    return jnp.concatenate([tmp_ref[pl.ds(s, rows, stride=seg), :] for s in range(seg)], axis=1)


def _rowmax(x):
    return jnp.max(functools.reduce(jnp.maximum, _lane_tiles(x)), axis=1, keepdims=True)


def _rowsum(x):
    return jnp.sum(functools.reduce(jnp.add, _lane_tiles(x)), axis=1, keepdims=True)


def _in_proj_kernel(x_ref, nw_ref, w_ref, o_ref, small_ref, hn_ref, *, small_off):
    j = pl.program_id(1)

    @pl.when(j == 0)
    def _():
        x = x_ref[...]
        ms = jnp.mean(x * x, axis=-1, keepdims=True)
        hn_ref[...] = (x * lax.rsqrt(ms + EPS) * nw_ref[...]).astype(BF16)
    acc = jnp.dot(hn_ref[...], w_ref[...], preferred_element_type=F32)
    o_ref[...] = acc.astype(o_ref.dtype)

    @pl.when(j == pl.num_programs(1) - 1)
    def _():
        small_ref[...] = acc[:, small_off:small_off + LANE]


def _in_proj(x2, norm_w, w):
    n, d = x2.shape
    ncol = w.shape[1]
    tm = min(1024, n)
    tn = 768
    small_off = CB_SMALL * LANE - (ncol // tn - 1) * tn
    assert 0 <= small_off <= tn - LANE
    return pl.pallas_call(
        functools.partial(_in_proj_kernel, small_off=small_off),
        grid=(n // tm, ncol // tn),
        in_specs=[pl.BlockSpec((tm, d), lambda i, j: (i, 0)),
                  pl.BlockSpec((1, d), lambda i, j: (0, 0)),
                  pl.BlockSpec((d, tn), lambda i, j: (0, j))],
        out_specs=[pl.BlockSpec((tm, tn), lambda i, j: (i, j)),
                   pl.BlockSpec((tm, LANE), lambda i, j: (i, 0))],
        out_shape=[jax.ShapeDtypeStruct((n, ncol), BF16), jax.ShapeDtypeStruct((n, LANE), F32)],
        scratch_shapes=[pltpu.VMEM((tm, d), BF16)],
        compiler_params=_cparams(("parallel", "arbitrary")),
        name="in_proj",
    )(x2, norm_w.reshape(1, d), w)


def _compress_kernel(a_ref, c_ref, sa_ref, sb_ref, w_ref, pe_ref, o_ref, xs_ref):
    kv = pl.program_id(1)
    t = a_ref.shape[1]
    nc = t // CMP_STRIDE
    x = a_ref[0].astype(F32)
    xr = _rope(x, c_ref[0], sa_ref[0], sb_ref[0])
    x = jnp.where(kv == 0, xr, x)
    xs_ref[pl.ds(0, t), :] = x
    xs_ref[pl.ds(t, CMP_STRIDE), :] = jnp.zeros((CMP_STRIDE, LANE), F32)
    acc = jnp.zeros((nc, LANE), F32)
    for l in range(CMP_LEN):
        rows = xs_ref[pl.ds(l, nc, stride=CMP_STRIDE), :] + pe_ref[0, pl.ds(l, 1), :]
        acc = acc + _mm(rows, w_ref[0, l])
    o_ref[0, 0, 0] = acc.astype(BF16)


def _nsa_compress(proj3, tabs, cmp_w, cmp_pe):
    b, t, _ = proj3.shape
    g = N_KV_NSA
    nc = t // CMP_STRIDE
    tab_spec = pl.BlockSpec((1, t, LANE), lambda bi, kv, gi: (bi, 0, 0))
    return pl.pallas_call(
        _compress_kernel,
        grid=(b, 2, g),
        in_specs=[pl.BlockSpec((1, t, LANE), lambda bi, kv, gi: (bi, 0, CB_KV + kv * g + gi)),
                  tab_spec, tab_spec, tab_spec,
                  pl.BlockSpec((1, CMP_LEN, LANE, LANE), lambda bi, kv, gi: (kv, 0, 0, 0)),
                  pl.BlockSpec((1, CMP_LEN, LANE), lambda bi, kv, gi: (kv, 0, 0))],
        out_specs=pl.BlockSpec((1, 1, 1, nc, LANE), lambda bi, kv, gi: (bi, kv, gi, 0, 0)),
        out_shape=jax.ShapeDtypeStruct((b, 2, g, nc, LANE), BF16),
        scratch_shapes=[pltpu.VMEM((t + CMP_STRIDE, LANE), F32)],
        compiler_params=_cparams(("parallel", "arbitrary", "arbitrary")),
        name="nsa_compress",
    )(proj3, *tabs, cmp_w, cmp_pe)


def _load_q(q_ref, c, sa, sb):
    scale = HEAD_DIM ** -0.5
    qs = [(_rope(q_ref[0, :, r * LANE:(r + 1) * LANE].astype(F32), c, sa, sb) * scale).astype(BF16)
          for r in range(N_REP)]
    return jnp.concatenate(qs, axis=0)


def _store_gated(o_ref, o, gate_ref, g, branch, tq):
    gt = gate_ref[0]
    for r in range(N_REP):
        col = ((g * N_REP + r) * 3 + branch)
        lane = lax.broadcasted_iota(jnp.int32, gt.shape, 1)
        gcol = jnp.sum(jnp.where(lane == col, gt, 0.0), axis=1, keepdims=True)
        o_ref[0, :, r * LANE:(r + 1) * LANE] = (o[r * tq:(r + 1) * tq] * _sigmoid(gcol)).astype(o_ref.dtype)


def _cmp_attn_kernel(q_ref, c_ref, sa_ref, sb_ref, kc_ref, vc_ref, gate_ref, o_ref, sel_ref, *, n_slc):
    g = pl.program_id(1)
    i = pl.program_id(2)
    tq = q_ref.shape[1]
    nc = kc_ref.shape[3]
    q4 = _load_q(q_ref, c_ref[0], sa_ref[0], sb_ref[0])
    s = _mm_nt(q4, kc_ref[0, 0, 0])
    row = lax.broadcasted_iota(jnp.int32, s.shape, 0)
    n = lax.broadcasted_iota(jnp.int32, s.shape, 1)
    tpos = i * tq + (row & (tq - 1))
    mask = (n * CMP_STRIDE + (CMP_LEN - 1) <= tpos) & (n < nc - 1)
    sm = jnp.where(mask, s, -1e30)
    m = _rowmax(sm)
    p = jnp.where(mask, jnp.exp(sm - m), 0.0)
    l = _rowsum(p)
    p = p / jnp.maximum(l, 1e-30)
    o = _mm(p, vc_ref[0, 0, 0])
    _store_gated(o_ref, o, gate_ref, g, 0, tq)

    ps = p[0:tq]
    for r in range(1, N_REP):
        ps = ps + p[r * tq:(r + 1) * tq]
    cn = lax.broadcasted_iota(jnp.int32, (nc, LANE), 0)
    cj = lax.broadcasted_iota(jnp.int32, (nc, LANE), 1)
    ratio = SLC_LEN // CMP_STRIDE
    agg = ((cn >= ratio * cj - (CMP_LEN // CMP_STRIDE - 1)) & (cn < ratio * cj + ratio)
           & (cn < nc - 1) & (cj < n_slc))
    agg = jnp.where(agg, 1.0, 0.0).astype(BF16)
    ps_hi = ps.astype(BF16)
    ps_lo = (ps - ps_hi.astype(F32)).astype(BF16)
    imp = (jnp.dot(ps_hi, agg, preferred_element_type=F32)
           + jnp.dot(ps_lo, agg, preferred_element_type=F32))

    j = lax.broadcasted_iota(jnp.int32, (tq, LANE), 1)
    tt = i * tq + lax.broadcasted_iota(jnp.int32, (tq, LANE), 0)
    cur = tt // SLC_LEN
    valid = j <= cur
    forced = (j == 0) | (j == cur) | (j == cur - 1)
    vals = jnp.where(forced, 1e30, jnp.where(valid, imp, -1.0))
    vt = vals.T
    jb = lax.broadcasted_iota(jnp.int32, (n_slc, tq), 0)
    vb = vt[0:n_slc]
    cnt = jnp.zeros((n_slc, tq), F32)
    for jp in range(n_slc):
        cand = vt[jp:jp + 1, :]
        ge = jnp.where(cand >= vb, 1.0, 0.0)
        gt = jnp.where(cand > vb, 1.0, 0.0)
        cnt = cnt + jnp.where(jb > jp, ge, gt)
    keep = jnp.where(cnt < float(min(SLC_TOP, n_slc)), 0.0, NEG_BIAS)
    if n_slc < LANE:
        keep = jnp.concatenate([keep, jnp.full((LANE - n_slc, tq), NEG_BIAS, F32)], axis=0)
    sel_ref[0, 0] = jnp.where(valid, keep.T, NEG_BIAS).astype(BF16)


def _nsa_cmp_attn(proj3, small3, tabs, kvc, tq=128):
    b, t, _ = proj3.shape
    g = N_KV_NSA
    nc = t // CMP_STRIDE
    n_slc = t // SLC_LEN
    rw = N_REP * LANE
    tab_spec = pl.BlockSpec((1, tq, LANE), lambda bi, gi, i: (bi, i, 0))
    return pl.pallas_call(
        functools.partial(_cmp_attn_kernel, n_slc=n_slc),
        grid=(b, g, t // tq),
        in_specs=[pl.BlockSpec((1, tq, rw), lambda bi, gi, i: (bi, i, gi)),
                  tab_spec, tab_spec, tab_spec,
                  pl.BlockSpec((1, 1, 1, nc, LANE), lambda bi, gi, i: (bi, 0, gi, 0, 0)),
                  pl.BlockSpec((1, 1, 1, nc, LANE), lambda bi, gi, i: (bi, 1, gi, 0, 0)),
                  pl.BlockSpec((1, tq, LANE), lambda bi, gi, i: (bi, i, 0))],
        out_specs=[pl.BlockSpec((1, tq, rw), lambda bi, gi, i: (bi, i, gi)),
                   pl.BlockSpec((1, 1, tq, LANE), lambda bi, gi, i: (bi, gi, i, 0))],
        out_shape=[jax.ShapeDtypeStruct((b, t, N_HEADS_NSA * LANE), BF16),
                   jax.ShapeDtypeStruct((b, g, t, LANE), BF16)],
        compiler_params=_cparams(("parallel", "parallel", "parallel")),
        name="nsa_cmp",
    )(proj3, *tabs, kvc, kvc, small3)


def _sel_attn_kernel(q_ref, c_ref, sa_ref, sb_ref, cf_ref, saf_ref, sbf_ref, k_ref, v_ref,
                     sel_ref, gate_ref, o_ref, ka_ref, vs_ref, m_ref, l_ref, acc_ref, s0_ref, s1_ref,
                     *, tk):
    g = pl.program_id(1)
    i = pl.program_id(2)
    tq = q_ref.shape[1]
    t = k_ref.shape[1]
    rows = N_REP * tq

    @pl.when(i == 0)
    def _():
        kr = _rope(k_ref[0].astype(F32), cf_ref[0], saf_ref[0], sbf_ref[0])
        ka_ref[:, 0:LANE] = kr.astype(BF16)
        pos = lax.broadcasted_iota(jnp.int32, (t, LANE), 0)
        lane = lax.broadcasted_iota(jnp.int32, (t, LANE), 1)
        ka_ref[:, LANE:2 * LANE] = jnp.where(lane == pos // SLC_LEN, 1.0, 0.0).astype(BF16)
        vs_ref[...] = v_ref[0].astype(BF16)

    q4 = _load_q(q_ref, c_ref[0], sa_ref[0], sb_ref[0])
    bias = sel_ref[0, 0]
    qa = jnp.concatenate([q4, jnp.concatenate([bias] * N_REP, axis=0)], axis=1)

    m_ref[...] = jnp.full((rows, LANE), -1e30, F32)
    l_ref[...] = jnp.zeros((rows, LANE), F32)
    acc_ref[...] = jnp.zeros((rows, LANE), F32)

    half = tk // 2

    def scores(kt, hlf):
        k0 = pl.multiple_of(kt * tk + hlf * half, half)
        return _mm_nt(qa, ka_ref[pl.ds(k0, half), :])

    def update(s_ref, kt, hlf, causal):
        k0 = pl.multiple_of(kt * tk + hlf * half, half)
        s = s_ref[...]
        if causal:
            r = lax.broadcasted_iota(jnp.int32, s.shape, 0)
            kp = k0 + lax.broadcasted_iota(jnp.int32, s.shape, 1)
            s = jnp.where(kp <= i * tq + (r & (tq - 1)), s, -1e30)
        tiles = _lane_tiles(s)
        m_old = m_ref[...]
        m_new = jnp.maximum(m_old, _rowmax(s))
        alpha = jnp.exp(m_old - m_new)
        ps = [jnp.exp(tl - m_new) for tl in tiles]
        l_ref[...] = alpha * l_ref[...] + jnp.sum(functools.reduce(jnp.add, ps), axis=1, keepdims=True)
        p = jnp.concatenate([x.astype(BF16) for x in ps], axis=1)
        acc_ref[...] = alpha * acc_ref[...] + jnp.dot(p, vs_ref[pl.ds(k0, half), :],
                                                      preferred_element_type=F32)
        m_ref[...] = m_new

    diag = (i * tq) // tk
    s0_ref[...] = scores(0, 0)

    def body(kt, carry):
        s1_ref[...] = scores(kt, 1)
        update(s0_ref, kt, 0, False)
        s0_ref[...] = scores(kt + 1, 0)
        update(s1_ref, kt, 1, False)
        return carry

    def body2(j, carry):
        body(2 * j, carry)
        body(2 * j + 1, carry)
        return carry

    lax.fori_loop(0, diag // 2, body2, 0)
    lax.fori_loop((diag // 2) * 2, diag, body, 0)
    s1_ref[...] = scores(diag, 1)
    update(s0_ref, diag, 0, True)
    update(s1_ref, diag, 1, True)
    o = acc_ref[...] / l_ref[...]
    _store_gated(o_ref, o, gate_ref, g, 1, tq)


def _nsa_sel_attn(proj3, small3, tabs, selbias, tq=128, tk=512):
    b, t, _ = proj3.shape
    g = N_KV_NSA
    rw = N_REP * LANE
    tk = min(tk, t)
    tab_spec = pl.BlockSpec((1, tq, LANE), lambda bi, gi, i: (bi, i, 0))
    tabf_spec = pl.BlockSpec((1, t, LANE), lambda bi, gi, i: (bi, 0, 0))
    kcb = CB_KV + 1 * 2 * g
    return pl.pallas_call(
        functools.partial(_sel_attn_kernel, tk=tk),
        grid=(b, g, t // tq),
        in_specs=[pl.BlockSpec((1, tq, rw), lambda bi, gi, i: (bi, i, gi)),
                  tab_spec, tab_spec, tab_spec, tabf_spec, tabf_spec, tabf_spec,
                  pl.BlockSpec((1, t, LANE), lambda bi, gi, i: (bi, 0, kcb + gi)),
                  pl.BlockSpec((1, t, LANE), lambda bi, gi, i: (bi, 0, kcb + g + gi)),
                  pl.BlockSpec((1, 1, tq, LANE), lambda bi, gi, i: (bi, gi, i, 0)),
                  pl.BlockSpec((1, tq, LANE), lambda bi, gi, i: (bi, i, 0))],
        out_specs=pl.BlockSpec((1, tq, rw), lambda bi, gi, i: (bi, i, gi)),
        out_shape=jax.ShapeDtypeStruct((b, t, N_HEADS_NSA * LANE), BF16),
        scratch_shapes=[pltpu.VMEM((t, 2 * LANE), BF16), pltpu.VMEM((t, LANE), BF16),
                        pltpu.VMEM((N_REP * tq, LANE), F32), pltpu.VMEM((N_REP * tq, LANE), F32),
                        pltpu.VMEM((N_REP * tq, LANE), F32),
                        pltpu.VMEM((N_REP * tq, tk // 2), F32), pltpu.VMEM((N_REP * tq, tk // 2), F32)],
        compiler_params=_cparams(("parallel", "parallel", "arbitrary")),
        name="nsa_sel",
    )(proj3, *tabs, *tabs, proj3, proj3, selbias, small3)


def _win_attn_kernel(q_ref, c_ref, sa_ref, sb_ref, cf_ref, saf_ref, sbf_ref, k_ref, v_ref,
                     gate_ref, o_ref, ks_ref, vs_ref, *, span):
    g = pl.program_id(1)
    i = pl.program_id(2)
    tq = q_ref.shape[1]

    t = k_ref.shape[1]

    @pl.when(i == 0)
    def _():
        zeros = jnp.zeros((WINDOW, LANE), BF16)
        ks_ref[pl.ds(0, WINDOW), :] = zeros
        vs_ref[pl.ds(0, WINDOW), :] = zeros
        ks_ref[pl.ds(WINDOW, t), :] = _rope(k_ref[0].astype(F32), cf_ref[0], saf_ref[0],
                                            sbf_ref[0]).astype(BF16)
        vs_ref[pl.ds(WINDOW, t), :] = v_ref[0].astype(BF16)

    q4 = _load_q(q_ref, c_ref[0], sa_ref[0], sb_ref[0])
    k0 = pl.multiple_of(i * tq, tq)
    s = _mm_nt(q4, ks_ref[pl.ds(k0, span), :])
    tiles = _lane_tiles(s)
    r = lax.broadcasted_iota(jnp.int32, tiles[0].shape, 0) & (tq - 1)
    c = lax.broadcasted_iota(jnp.int32, tiles[0].shape, 1)
    first_block = WINDOW // tq - i
    masked = []
    for b, tl in enumerate(tiles):
        if b == 0:
            tl = jnp.where(c > r, tl, -1e30)
        if b == len(tiles) - 1:
            tl = jnp.where(c <= r, tl, -1e30)
        else:
            tl = jnp.where(b >= first_block, tl, -1e30)
        masked.append(tl)
    m = jnp.max(functools.reduce(jnp.maximum, masked), axis=1, keepdims=True)
    ps = [jnp.exp(tl - m) for tl in masked]
    l = jnp.sum(functools.reduce(jnp.add, ps), axis=1, keepdims=True)
    p = jnp.concatenate([x.astype(BF16) for x in ps], axis=1)
    o = jnp.dot(p, vs_ref[pl.ds(k0, span), :], preferred_element_type=F32) / l
    _store_gated(o_ref, o, gate_ref, g, 2, tq)


def _nsa_win_attn(proj3, small3, tabs, tq=128):
    b, t, _ = proj3.shape
    g = N_KV_NSA
    rw = N_REP * LANE
    span = WINDOW + tq
    assert tq == LANE
    tab_spec = pl.BlockSpec((1, tq, LANE), lambda bi, gi, i: (bi, i, 0))
    tabf_spec = pl.BlockSpec((1, t, LANE), lambda bi, gi, i: (bi, 0, 0))
    kcb = CB_KV + 2 * 2 * g
    return pl.pallas_call(
        functools.partial(_win_attn_kernel, span=span),
        grid=(b, g, t // tq),
        in_specs=[pl.BlockSpec((1, tq, rw), lambda bi, gi, i: (bi, i, gi)),
                  tab_spec, tab_spec, tab_spec, tabf_spec, tabf_spec, tabf_spec,
                  pl.BlockSpec((1, t, LANE), lambda bi, gi, i: (bi, 0, kcb + gi)),
                  pl.BlockSpec((1, t, LANE), lambda bi, gi, i: (bi, 0, kcb + g + gi)),
                  pl.BlockSpec((1, tq, LANE), lambda bi, gi, i: (bi, i, 0))],
        out_specs=pl.BlockSpec((1, tq, rw), lambda bi, gi, i: (bi, i, gi)),
        out_shape=jax.ShapeDtypeStruct((b, t, N_HEADS_NSA * LANE), BF16),
        scratch_shapes=[pltpu.VMEM((t + WINDOW, LANE), BF16), pltpu.VMEM((t + WINDOW, LANE), BF16)],
        compiler_params=_cparams(("parallel", "parallel", "arbitrary")),
        name="nsa_win",
    )(proj3, *tabs, *tabs, proj3, proj3, small3)


def _gdn_kernel(q_ref, k_ref, v_ref, z_ref, braw_ref, araw_ref, cwq_ref, cwk_ref, cwv_ref, alog_ref,
                dtb_ref, nw_ref, o_ref, q_s, k_s, v_s, o_s, qp_s, op_s, xp_s, bd_s, gcd_s, gl_s, mm_s,
                nn_s):
    t = q_ref.shape[1]
    c = GDN_CHUNK
    nchunk = t // c
    xp_s[pl.ds(0, 8), :] = jnp.zeros((8, LANE), F32)

    def conv_silu(x, cw):
        xp_s[pl.ds(8, t), :] = x
        y = x * cw[CONV_WIDTH - 1:CONV_WIDTH]
        for sft in range(1, CONV_WIDTH):
            y = y + xp_s[pl.ds(8 - sft, t), :] * cw[CONV_WIDTH - 1 - sft:CONV_WIDTH - sft]
        return y * _sigmoid(y)

    def l2n(x):
        return x * lax.rsqrt(jnp.sum(x * x, axis=1, keepdims=True) + EPS)

    q_s[...] = l2n(conv_silu(q_ref[0].astype(F32), cwq_ref[...])) * (HEAD_DIM ** -0.5)
    k_s[...] = l2n(conv_silu(k_ref[0].astype(F32), cwk_ref[...]))
    v_s[...] = conv_silu(v_ref[0].astype(F32), cwv_ref[...])

    bd_s[...] = _sigmoid(braw_ref[0, 0])
    xa = araw_ref[0, 0] + dtb_ref[0]
    softplus = jnp.maximum(xa, 0.0) + jnp.log(1.0 + jnp.exp(-jnp.abs(xa)))
    gc = -jnp.exp(alog_ref[0]) * softplus
    lane = lax.broadcasted_iota(jnp.int32, gc.shape, 1)
    sft = 1
    while sft < c:
        gc = gc + jnp.where((lane & (c - 1)) >= sft, pltpu.roll(gc, sft, 1), 0.0)
        sft *= 2
    gcd_s[...] = gc

    c2 = 2 * c
    ci = lax.broadcasted_iota(jnp.int32, (c2, c2), 0)
    cj = lax.broadcasted_iota(jnp.int32, (c2, c2), 1)
    same = (ci // c) == (cj // c)
    tril = same & (cj <= ci)
    strict = same & (cj < ci)
    eye = jnp.where(ci == cj, 1.0, 0.0)
    first = ci < c

    pairs_per_iter = 8
    rng = range(pairs_per_iter)

    def prep(it, carry):
        n2 = [it * pairs_per_iter + p for p in rng]
        sl = [pl.ds(pl.multiple_of(n * c2, c2), c2) for n in n2]
        grow = [gcd_s[pl.ds(n, 1), :] for n in n2]
        gc2 = [jnp.broadcast_to(g_, (c2, c2)).T for g_ in grow]
        betac = [jnp.broadcast_to(bd_s[pl.ds(n, 1), :], (c2, c2)).T for n in n2]
        kn = [k_s[s_, :] for s_ in sl]
        kbn = [kn[p] * betac[p] for p in rng]
        decay = [jnp.exp(jnp.where(tril, gc2[p] - grow[p], -1e30)) for p in rng]
        kk = [_mm_nt(kbn[p], kn[p]) for p in rng]
        y = [-jnp.where(strict, kk[p] * decay[p], 0.0) for p in rng]
        pm = [eye + y_ for y_ in y]
        pw = 1
        while 2 * pw < c:
            y = [_mm(y_, y_) for y_ in y]
            pm = [pm[p] + _mm(pm[p], y[p]) for p in rng]
            pw *= 2
        egn = [jnp.exp(g_) for g_ in gc2]
        uw = [_mm(pm[p], jnp.concatenate([v_s[sl[p], :] * betac[p], kbn[p] * egn[p]], axis=1))
              for p in rng]
        qn = [q_s[s_, :] for s_ in sl]
        qk = [_mm_nt(qn[p], kn[p]) for p in rng]
        attn = [jnp.where(tril, qk[p] * decay[p], 0.0) for p in rng]
        auw = [_mm(attn[p], uw[p]) for p in rng]
        for p in rng:
            qp_s[sl[p], :] = qn[p] * egn[p] - auw[p][:, LANE:]
            op_s[sl[p], :] = auw[p][:, :LANE]
        gl = [jnp.where(first, g_[c - 1:c, :], g_[c2 - 1:c2, :]) for g_ in gc2]
        kdt = [(kn[p] * jnp.exp(gl[p] - gc2[p])).T for p in rng]
        nma = [_mm(kdt[p][:, 0:c], uw[p][0:c]) for p in rng]
        nmb = [_mm(kdt[p][:, c:c2], uw[p][c:c2]) for p in rng]
        mb = [x[:, LANE:] for x in nmb]
        mbnm = [_mm(mb[p], nma[p]) for p in rng]
        ma = [x[:, LANE:] for x in nma]
        na = [x[:, :LANE] for x in nma]
        nb_ = [x[:, :LANE] for x in nmb]
        mba = [x[:, LANE:] for x in mbnm]
        mbn = [x[:, :LANE] for x in mbnm]
        for p in rng:
            ga = jnp.exp(gc2[p][c - 1:c, :])
            gb = jnp.exp(gc2[p][c2 - 1:c2, :])
            mm_s[2 * n2[p]] = ma[p].astype(BF16)
            nn_s[2 * n2[p]] = na[p]
            gl_s[pl.ds(2 * n2[p], 1), :] = ga
            mm_s[2 * n2[p] + 1] = (gb * ma[p] + ga * mb[p] - mba[p]).astype(BF16)
            nn_s[2 * n2[p] + 1] = gb * na[p] - mbn[p] + nb_[p]
            gl_s[pl.ds(2 * n2[p] + 1, 1), :] = ga * gb
        return carry

    lax.fori_loop(0, nchunk // (2 * pairs_per_iter), prep, 0)

    def second_out(n2, s_mid):
        rows = pl.ds(pl.multiple_of(n2 * c2, c2) + c, c)
        o_s[rows, :] = (jnp.dot(qp_s[rows, :].astype(BF16), s_mid.astype(BF16),
                                 preferred_element_type=F32) + op_s[rows, :])

    def scan(n2, carry):
        s, s_mid_prev = carry
        second_out(jnp.maximum(n2 - 1, 0), s_mid_prev)
        ra = pl.ds(pl.multiple_of(n2 * c2, c2), c)
        sb = s.astype(BF16)
        s_mid = (s * gl_s[pl.ds(2 * n2, 1), :] - jnp.dot(mm_s[2 * n2], sb, preferred_element_type=F32)
                 + nn_s[2 * n2])
        s_new = (s * gl_s[pl.ds(2 * n2 + 1, 1), :]
                 - jnp.dot(mm_s[2 * n2 + 1], sb, preferred_element_type=F32) + nn_s[2 * n2 + 1])
        o_s[ra, :] = jnp.dot(qp_s[ra, :].astype(BF16), sb, preferred_element_type=F32) + op_s[ra, :]
        return s_new, s_mid

    zero_state = jnp.zeros((LANE, LANE), F32)
    _, s_mid_last = lax.fori_loop(0, nchunk // 2, scan, (zero_state, zero_state))
    second_out(jnp.int32(nchunk // 2 - 1), s_mid_last)

    o = o_s[...]
    on = o * lax.rsqrt(jnp.mean(o * o, axis=1, keepdims=True) + EPS) * nw_ref[...]
    z = z_ref[0].astype(F32)
    o_ref[0] = (on * (z * _sigmoid(z))).astype(o_ref.dtype)


def _gdn(proj3, small3, conv_w, a_log, dt_bias, norm_w):
    b, t, _ = proj3.shape
    hh = N_HEADS_GDN
    c = GDN_CHUNK
    col = lambda cb: pl.BlockSpec((1, t, LANE), lambda bi, hi: (bi, 0, cb + hi))
    cw = lambda off: pl.BlockSpec((CONV_WIDTH, LANE), lambda bi, hi: (0, off + hi))
    hrow = pl.BlockSpec((1, 1, LANE), lambda bi, hi: (hi, 0, 0))
    alog_b = jnp.broadcast_to(a_log.astype(F32)[:, None, None], (hh, 1, LANE))
    dtb_b = jnp.broadcast_to(dt_bias.astype(F32)[:, None, None], (hh, 1, LANE))
    big = pltpu.VMEM((t, LANE), F32)
    nrow = t // LANE
    ba = small3[:, :, SMALL_BETA:SMALL_BETA + 2 * hh].transpose(0, 2, 1).reshape(b, 2 * hh, nrow, LANE)
    dense = lambda off: pl.BlockSpec((1, 1, nrow, LANE), lambda bi, hi: (bi, off + hi, 0, 0))
    return pl.pallas_call(
        _gdn_kernel,
        grid=(b, hh),
        in_specs=[col(CB_GQKV), col(CB_GQKV + hh), col(CB_GQKV + 2 * hh), col(CB_Z),
                  dense(0), dense(hh),
                  cw(0), cw(hh), cw(2 * hh), hrow, hrow,
                  pl.BlockSpec((1, LANE), lambda bi, hi: (0, 0))],
        out_specs=pl.BlockSpec((1, t, LANE), lambda bi, hi: (bi, 0, hi)),
        out_shape=jax.ShapeDtypeStruct((b, t, hh * LANE), BF16),
        scratch_shapes=[big, big, big, big, big, big, pltpu.VMEM((t + 8, LANE), F32),
                        pltpu.VMEM((nrow, LANE), F32), pltpu.VMEM((nrow, LANE), F32),
                        pltpu.VMEM((t // c, LANE), F32),
                        pltpu.VMEM((t // c, LANE, LANE), BF16),
                        pltpu.VMEM((t // c, LANE, LANE), F32)],
        compiler_params=_cparams(("parallel", "parallel")),
        name="gdn",
    )(proj3, proj3, proj3, proj3, ba, ba, conv_w, conv_w, conv_w, alog_b, dtb_b,
      norm_w.reshape(1, LANE))


def _split3(a):
    hi = a.astype(BF16)
    lo = (a - hi.astype(F32)).astype(BF16)
    return hi, lo


def _out_proj_kernel(oc_ref, os_ref, ow_ref, ob_ref, x_ref, wo_ref, fw_ref, wr_ref, br_ref,
                     h_ref, hn_ref, ids_ref, wts_ref, tmp_ref):
    half = oc_ref.shape[1]
    oa = (oc_ref[...].astype(F32) + os_ref[...].astype(F32) + ow_ref[...].astype(F32)).astype(BF16)
    h1 = (x_ref[...] + jnp.dot(oa, wo_ref[0:half, :], preferred_element_type=F32)
          + jnp.dot(ob_ref[...], wo_ref[half:2 * half, :], preferred_element_type=F32))
    h_ref[...] = h1
    hn = h1 * lax.rsqrt(jnp.mean(h1 * h1, axis=1, keepdims=True) + EPS) * fw_ref[...]
    _store_token_major(hn_ref, hn, tmp_ref)

    a_hi, a_lo = _split3(hn)
    w_hi, w_lo = _split3(wr_ref[...])
    dot = lambda a, b: jnp.dot(a, b, preferred_element_type=F32)
    logits = dot(a_hi, w_hi) + dot(a_hi, w_lo) + dot(a_lo, w_hi) + br_ref[...]
    lane = lax.broadcasted_iota(jnp.int32, logits.shape, 1)
    big = 1e30
    is_g = lane < N_GROUPS
    lg = jnp.where(is_g, logits, -big)
    gm = jnp.max(lg, axis=1, keepdims=True)
    grp = jnp.min(jnp.where(lg == gm, lane, LANE), axis=1, keepdims=True)
    p_grp = 1.0 / jnp.sum(jnp.where(is_g, jnp.exp(lg - gm), 0.0), axis=1, keepdims=True)
    e_id = lane - N_GROUPS
    in_g = (e_id >= 0) & (e_id < N_EXPERTS) & ((e_id // EXPERTS_PER_GROUP) == grp)
    le = jnp.where(in_g, logits, -big)
    em = jnp.max(le, axis=1, keepdims=True)
    pe = jnp.where(in_g, jnp.exp(le - em), 0.0)
    pe = pe / jnp.sum(pe, axis=1, keepdims=True)
    pm = jnp.where(in_g, pe, -1.0)
    p1 = jnp.max(pm, axis=1, keepdims=True)
    i1 = jnp.min(jnp.where(pm == p1, lane, LANE), axis=1, keepdims=True)
    pm2 = jnp.where(lane == i1, -1.0, pm)
    p2 = jnp.max(pm2, axis=1, keepdims=True)
    i2 = jnp.min(jnp.where(pm2 == p2, lane, LANE), axis=1, keepdims=True)
    den = p1 + p2
    ids_ref[...] = jnp.where(lane == 0, i1 - N_GROUPS, jnp.where(lane == 1, i2 - N_GROUPS, 0))
    wts_ref[...] = jnp.where(lane == 0, p1 / den * p_grp, jnp.where(lane == 1, p2 / den * p_grp, 0.0))


def _out_proj(oc, os_, ow, ob, x2, w_out, ffn_w, wr, br):
    n, d = x2.shape
    half = oc.shape[1]
    seg = d // LANE
    tm = min(256, n)
    row = lambda w: pl.BlockSpec((tm, w), lambda i: (i, 0))
    full = lambda a: pl.BlockSpec(a.shape, lambda i: (0,) * a.ndim)
    fw = ffn_w.reshape(1, d)
    return pl.pallas_call(
        _out_proj_kernel,
        grid=(n // tm,),
        in_specs=[row(half), row(half), row(half), row(half), row(d), full(w_out), full(fw),
                  full(wr), full(br)],
        out_specs=[row(d), pl.BlockSpec((tm * seg, LANE), lambda i: (i, 0)), row(LANE), row(LANE)],
        out_shape=[jax.ShapeDtypeStruct((n, d), F32), jax.ShapeDtypeStruct((n * seg, LANE), BF16),
                   jax.ShapeDtypeStruct((n, LANE), jnp.int32), jax.ShapeDtypeStruct((n, LANE), F32)],
        scratch_shapes=[pltpu.VMEM((tm * seg, LANE), F32)],
        compiler_params=_cparams(("parallel",)),
        name="out_proj",
    )(oc, os_, ow, ob, x2, w_out, fw, wr, br)


def _moe_kernel(bexp_ref, bact_ref, bfirst_ref, bord_ref, bnext_ref, rtok_ref, rtokn_ref, rslotp_ref,
                rslot_ref, hn_hbm, wg_hbm, wu_hbm, wd_hbm, y_hbm, xbuf, ybuf, tmp, wg_st, wu_st, wd_st,
                wgb, wub, wdb, gsem, ssem, wsem, *, n_real):
    i = pl.program_id(0)
    nb = pl.num_programs(0)
    rb = rtok_ref.shape[2]
    seg = ybuf.shape[1] // rb
    slot = i & 1
    other = 1 - slot
    active = bact_ref[i] > 0
    prev_active = jnp.logical_and(i > 0, bact_ref[jnp.maximum(i - 1, 0)] > 0)

    def gather_copy(tok, r, buf):
        return pltpu.make_async_copy(hn_hbm.at[pl.ds(pl.multiple_of(tok, seg), seg), :],
                                     xbuf.at[buf, pl.ds(r * seg, seg), :], gsem.at[buf])

    def scatter_copy(dst, r, buf):
        return pltpu.make_async_copy(ybuf.at[buf, pl.ds(r * seg, seg), :],
                                     y_hbm.at[pl.ds(pl.multiple_of(dst, seg), seg), :], ssem.at[buf])

    def wait_gather(buf):
        pltpu.make_async_copy(hn_hbm.at[pl.ds(0, rb * seg), :], xbuf.at[buf], gsem.at[buf]).wait()

    def wait_scatter(buf):
        pltpu.make_async_copy(ybuf.at[buf], y_hbm.at[pl.ds(0, rb * seg), :], ssem.at[buf]).wait()

    def weight_copies(e, ws):
        return (pltpu.make_async_copy(wg_hbm.at[e], wg_st.at[ws], wsem.at[ws]),
                pltpu.make_async_copy(wu_hbm.at[e], wu_st.at[ws], wsem.at[ws]),
                pltpu.make_async_copy(wd_hbm.at[e], wd_st.at[ws], wsem.at[ws]))

    @pl.when(i == 0)
    def _():
        for cp in weight_copies(bexp_ref[0], 0):
            cp.start(priority=1)
        ybuf[...] = jnp.zeros(ybuf.shape, ybuf.dtype)
        pltpu.make_async_copy(ybuf.at[0], y_hbm.at[pl.ds(n_real * seg, rb * seg), :],
                              ssem.at[0]).start()

        def first(r, carry):
            gather_copy(rtok_ref[0, 0, r], r, 0).start()
            return carry
        lax.fori_loop(0, rb, first, 0)

    @pl.when(jnp.logical_or(i == 0, prev_active))
    def _():
        wait_gather(slot)

    @pl.when(active)
    def _():
        @pl.when(bfirst_ref[i] > 0)
        def _():
            ws = bord_ref[i] & 1
            for cp in weight_copies(bexp_ref[i], ws):
                cp.wait()

            @pl.when(bnext_ref[i] >= 0)
            def _():
                for cp in weight_copies(bnext_ref[i], 1 - ws):
                    cp.start(priority=1)

            wgb[...] = wg_st[ws].astype(BF16)
            wub[...] = wu_st[ws].astype(BF16)
            wdb[...] = wd_st[ws].astype(BF16)

        n_piece = 8
        per = rb // n_piece

        def issue(piece):
            for r in range(piece * per, (piece + 1) * per):
                gather_copy(rtokn_ref[0, 0, r], r, other).start()
                scatter_copy(rslotp_ref[0, 0, r], r, other).start()

        x = _load_token_major(xbuf.at[slot], rb, tmp).astype(BF16)
        de = wgb.shape[1]
        d = wdb.shape[1]
        hc = de // 2
        acts = []
        for c in range(2):
            issue(2 * c)
            hg = jnp.dot(x, wgb[:, c * hc:(c + 1) * hc], preferred_element_type=F32)
            issue(2 * c + 1)
            hu = jnp.dot(x, wub[:, c * hc:(c + 1) * hc], preferred_element_type=F32)
            acts.append((hg * _sigmoid(hg) * hu).astype(BF16))
        act = jnp.concatenate(acts, axis=1)
        dc = d // 4
        ys = []
        for j in range(4):
            issue(4 + j)
            ys.append(jnp.dot(act, wdb[:, j * dc:(j + 1) * dc], preferred_element_type=F32))
        wait_scatter(slot)
        _store_token_major(ybuf.at[slot], jnp.concatenate(ys, axis=1), tmp)

    def scatter_all(idx_ref, buf):
        def body(r, carry):
            scatter_copy(idx_ref[0, 0, r], r, buf).start()
            return carry
        lax.fori_loop(0, rb, body, 0)

    @pl.when(jnp.logical_and(jnp.logical_not(active), prev_active))
    def _():
        wait_scatter(slot)
        scatter_all(rslotp_ref, other)
        wait_scatter(other)

    @pl.when(jnp.logical_and(i == nb - 1, active))
    def _():
        wait_scatter(other)
        scatter_all(rslot_ref, slot)
        wait_scatter(slot)
        wait_gather(other)


def _moe(hn, bexp, bact, rtok, rslot, w_gate, w_up, w_down, n_real):
    d, de = w_gate.shape[1], w_gate.shape[2]
    seg = d // LANE
    nb = bexp.shape[0]
    rb = MOE_ROW_BLOCK
    idx = jnp.arange(nb, dtype=jnp.int32)
    prev_e = jnp.concatenate([jnp.full((1,), -1, jnp.int32), bexp[:-1]])
    bfirst = jnp.logical_and(bact > 0, jnp.logical_or(idx == 0, bexp != prev_e)).astype(jnp.int32)
    bord = jnp.cumsum(bfirst) - 1
    first_at = jnp.where(bfirst > 0, idx, nb)
    next_first = jnp.concatenate([lax.cummin(first_at[::-1])[::-1][1:], jnp.full((1,), nb, jnp.int32)])
    bnext = jnp.where(next_first < nb, bexp[jnp.minimum(next_first, nb - 1)], -1).astype(jnp.int32)

    smem_rows = pl.BlockSpec((1, 1, rb), lambda i, *_: (i, 0, 0), memory_space=pltpu.SMEM)
    smem_next = pl.BlockSpec((1, 1, rb), lambda i, *_: (jnp.minimum(i + 1, nb - 1), 0, 0),
                             memory_space=pltpu.SMEM)
    smem_cur = pl.BlockSpec((1, 1, rb), lambda i, *_: (i + 1, 0, 0), memory_space=pltpu.SMEM)
    hbm = pl.BlockSpec(memory_space=pl.ANY)
    grid_spec = pltpu.PrefetchScalarGridSpec(
        num_scalar_prefetch=5,
        grid=(nb,),
        in_specs=[smem_rows, smem_next, smem_rows, smem_cur, hbm, hbm, hbm, hbm],
        out_specs=hbm,
        scratch_shapes=[pltpu.VMEM((2, rb * seg, LANE), BF16), pltpu.VMEM((2, rb * seg, LANE), BF16),
                        pltpu.VMEM((rb * seg, LANE), F32),
                        pltpu.VMEM((2, d, de), F32), pltpu.VMEM((2, d, de), F32), pltpu.VMEM((2, de, d), F32),
                        pltpu.VMEM((d, de), BF16), pltpu.VMEM((d, de), BF16), pltpu.VMEM((de, d), BF16),
                        pltpu.SemaphoreType.DMA((2,)), pltpu.SemaphoreType.DMA((2,)),
                        pltpu.SemaphoreType.DMA((2,))])
    return pl.pallas_call(
        functools.partial(_moe_kernel, n_real=n_real),
        grid_spec=grid_spec,
        out_shape=jax.ShapeDtypeStruct(((n_real + 2 * rb) * seg, LANE), BF16),
        compiler_params=_cparams(("arbitrary",)),
        name="moe",
    )(bexp, bact, bfirst, bord, bnext, rtok, rtok, rslot, rslot, hn, w_gate, w_up, w_down)


def _dispatch(ids, n, seg):
    k = 2
    m = n * k
    rb = MOE_ROW_BLOCK
    e_flat = ids[:, :k].reshape(m)
    onehot = (e_flat[:, None] == jnp.arange(N_EXPERTS, dtype=jnp.int32)[None, :]).astype(jnp.int32)
    csum = jnp.cumsum(onehot, axis=0)
    counts = csum[-1]
    rank = jnp.take_along_axis(csum, e_flat[:, None], axis=1)[:, 0] - 1
    padded = (counts + rb - 1) // rb * rb
    pad_end = jnp.cumsum(padded)
    pad_start = pad_end - padded
    dest = pad_start[e_flat] + rank
    nb = (m + N_EXPERTS * (rb - 1) + rb - 1) // rb
    p = nb * rb
    row_m = jnp.full((p,), -1, jnp.int32).at[dest].set(jnp.arange(m, dtype=jnp.int32))
    real = row_m >= 0
    rtok = jnp.where(real, row_m // k, 0)
    pidx = jnp.arange(p, dtype=jnp.int32)
    rslot = jnp.where(real, (row_m % k) * n + row_m // k, m + ((pidx // rb) % 2) * rb + pidx % rb)
    starts = jnp.arange(nb, dtype=jnp.int32) * rb
    bexp = jnp.minimum(jnp.sum((pad_end[None, :] <= starts[:, None]).astype(jnp.int32), axis=1),
                       N_EXPERTS - 1)
    bact = jnp.sum(real.reshape(nb, rb).astype(jnp.int32), axis=1)
    last_e = jnp.max(jnp.where(bact > 0, bexp, 0))
    bexp = jnp.where(bact > 0, bexp, last_e)
    rslot = jnp.concatenate([m + rb + jnp.arange(rb, dtype=jnp.int32), rslot])
    return (bexp, bact, (rtok * seg).reshape(nb, 1, rb), (rslot * seg).reshape(nb + 1, 1, rb),
            m)


def _combine_kernel(h_ref, y0_ref, y1_ref, wts_ref, fw_ref, o_ref, tmp_ref):
    tm = h_ref.shape[0]
    wts = wts_ref[...]
    y0 = _load_token_major(y0_ref, tm, tmp_ref)
    y1 = _load_token_major(y1_ref, tm, tmp_ref)
    moe = y0 * wts[:, 0:1] + y1 * wts[:, 1:2]
    h = h_ref[...] + moe
    o_ref[...] = h * lax.rsqrt(jnp.mean(h * h, axis=1, keepdims=True) + EPS) * fw_ref[...]


def _combine(h1, y, wts, final_w):
    n, d = h1.shape
    seg = d // LANE
    tm = min(256, n)
    nt = n // tm
    return pl.pallas_call(
        _combine_kernel,
        grid=(nt,),
        in_specs=[pl.BlockSpec((tm, d), lambda i: (i, 0)),
                  pl.BlockSpec((tm * seg, LANE), lambda i: (i, 0)),
                  pl.BlockSpec((tm * seg, LANE), lambda i: (i + nt, 0)),
                  pl.BlockSpec((tm, LANE), lambda i: (i, 0)),
                  pl.BlockSpec((1, d), lambda i: (0, 0))],
        out_specs=pl.BlockSpec((tm, d), lambda i: (i, 0)),
        out_shape=jax.ShapeDtypeStruct((n, d), F32),
        scratch_shapes=[pltpu.VMEM((tm * seg, LANE), F32)],
        compiler_params=_cparams(("parallel",)),
        name="combine",
    )(h1, y, y, wts, final_w.reshape(1, d))


def _rope_tables(positions):
    half = ROT_DIM // 2
    inv_freq = ROPE_THETA ** (-jnp.arange(0, ROT_DIM, 2, dtype=F32) / ROT_DIM)
    ang = positions.astype(F32)[..., None] * inv_freq
    cos, sin = jnp.cos(ang), jnp.sin(ang)
    b, t = positions.shape
    ones = jnp.ones((b, t, LANE - ROT_DIM), F32)
    zeros = jnp.zeros((b, t, LANE - half), F32)
    c = jnp.concatenate([cos, cos, ones], axis=-1)
    sa = jnp.concatenate([-sin, zeros], axis=-1)
    sb = jnp.concatenate([jnp.zeros((b, t, half), F32), sin, zeros[..., :LANE - ROT_DIM]], axis=-1)
    return c, sa, sb


def _arrange_w_in(w_in):
    d = w_in.shape[0]
    sizes = (N_HEADS_NSA * HEAD_DIM, 3 * 2 * N_KV_NSA * HEAD_DIM, 3 * N_HEADS_NSA,
             3 * N_HEADS_GDN * HEAD_DIM, N_HEADS_GDN, N_HEADS_GDN, N_HEADS_GDN * HEAD_DIM)
    offs = np.cumsum((0,) + sizes)
    seg = [w_in[:, offs[i]:offs[i + 1]] for i in range(len(sizes))]
    q, kv, gate, gqkv, gb, ga, gz = seg
    used = sum(sizes)
    pad = jnp.zeros((d, N_CB * LANE - used), w_in.dtype)
    return jnp.concatenate([q, kv, gqkv, gz, gate, gb, ga, pad], axis=1).astype(BF16)


def kernel(x, positions, attn_norm_w, w_in, cmp_wk, cmp_pek, cmp_wv, cmp_pev, gdn_conv_w, gdn_a_log,
           gdn_dt_bias, gdn_norm_w, w_out, ffn_norm_w, router_group_w, router_group_b,
           router_expert_w, router_expert_b, moe_w_gate, moe_w_up, moe_w_down, final_norm_w):
    b, t, d = x.shape
    n = b * t
    tabs = _rope_tables(positions)
    h = x.reshape(n, d)
    assert w_in.shape[0] == 1, "single-layer block only"
    for l in range(1):
        proj, small = _in_proj(h, attn_norm_w[l], _arrange_w_in(w_in[l]))
        proj3 = proj.reshape(b, t, N_CB * LANE)
        small3 = small.reshape(b, t, LANE)
        cmp_w = jnp.stack([cmp_wk[l], cmp_wv[l]])
        cmp_pe = jnp.stack([cmp_pek[l], cmp_pev[l]])
        kvc = _nsa_compress(proj3, tabs, cmp_w, cmp_pe)
        o_c, selbias = _nsa_cmp_attn(proj3, small3, tabs, kvc)
        o_s = _nsa_sel_attn(proj3, small3, tabs, selbias)
        o_w = _nsa_win_attn(proj3, small3, tabs)
        o_b = _gdn(proj3, small3, gdn_conv_w[l], gdn_a_log[l], gdn_dt_bias[l], gdn_norm_w[l])
        half = N_HEADS_NSA * HEAD_DIM
        wr = jnp.concatenate([router_group_w[l], router_expert_w[l],
                              jnp.zeros((d, LANE - N_GROUPS - N_EXPERTS), F32)], axis=1)
        br = jnp.concatenate([router_group_b[l], router_expert_b[l],
                              jnp.zeros((LANE - N_GROUPS - N_EXPERTS,), F32)]).reshape(1, LANE)
        h1, hn2, ids, wts = _out_proj(o_c.reshape(n, half), o_s.reshape(n, half), o_w.reshape(n, half),
                                      o_b.reshape(n, half), h, w_out[l].astype(BF16), ffn_norm_w[l], wr, br)
        bexp, bact, rtok, rslot, n_slots = _dispatch(ids, n, d // LANE)
        y = _moe(hn2, bexp, bact, rtok, rslot, moe_w_gate[l], moe_w_up[l], moe_w_down[l], n_slots)
        out = _combine(h1, y, wts, final_norm_w)
    return out.reshape(b, t, d)
```

```python
import functools

import numpy as np
import jax
import jax.numpy as jnp
from jax import lax
from jax.experimental import pallas as pl
from jax.experimental.pallas import tpu as pltpu

F32 = jnp.float32
BF16 = jnp.bfloat16

HEAD_DIM = 128
N_HEADS_NSA = 8
N_KV_NSA = 2
N_REP = N_HEADS_NSA // N_KV_NSA
N_HEADS_GDN = 8
ROT_DIM = 32
ROPE_THETA = 500000.0
CMP_LEN = 32
CMP_STRIDE = 16
SLC_LEN = 64
SLC_TOP = 16
WINDOW = 512
CONV_WIDTH = 4
GDN_CHUNK = 64
N_GROUPS = 8
EXPERTS_PER_GROUP = 8
N_EXPERTS = 64
MOE_ROW_BLOCK = 256
EPS = 1e-6
LANE = 128

CB_Q = 0
CB_KV = 8
CB_GQKV = 20
CB_Z = 44
CB_SMALL = 52
N_CB = 54
SMALL_BETA = 24
SMALL_DECAY = 32

NEG_BIAS = -32768.0
VMEM_LIMIT = 56 * 1024 * 1024


def _cparams(sem):
    return pltpu.CompilerParams(dimension_semantics=sem, vmem_limit_bytes=VMEM_LIMIT)


def _mm(a, b):
    return jnp.dot(a.astype(BF16), b.astype(BF16), preferred_element_type=F32)


def _mm_nt(a, b):
    return lax.dot_general(a.astype(BF16), b.astype(BF16), (((1,), (1,)), ((), ())),
                           preferred_element_type=F32)


def _rope(x, c, sa, sb):
    return (x * c + pltpu.roll(x, LANE - ROT_DIM // 2, 1) * sa
            + pltpu.roll(x, ROT_DIM // 2, 1) * sb)


def _sigmoid(x):
    return 0.5 * jnp.tanh(0.5 * x) + 0.5


def _lane_tiles(x):
    return [x[:, i:i + LANE] for i in range(0, x.shape[1], LANE)]


def _store_token_major(ref, x, tmp_ref):
    rows, d = x.shape
    seg = d // LANE
    for s in range(seg):
        tmp_ref[pl.ds(s, rows, stride=seg), :] = x[:, s * LANE:(s + 1) * LANE]
    ref[...] = tmp_ref[...].astype(ref.dtype)


def _load_token_major(ref, rows, tmp_ref):
    seg = ref.shape[0] // rows
    tmp_ref[...] = ref[...].astype(F32)
    return jnp.concatenate([tmp_ref[pl.ds(s, rows, stride=seg), :] for s in range(seg)], axis=1)


def _rowmax(x):
    return jnp.max(functools.reduce(jnp.maximum, _lane_tiles(x)), axis=1, keepdims=True)


def _rowsum(x):
    return jnp.sum(functools.reduce(jnp.add, _lane_tiles(x)), axis=1, keepdims=True)


def _in_proj_kernel(x_ref, nw_ref, w_ref, o_ref, small_ref, hn_ref, *, small_off):
    j = pl.program_id(1)

    @pl.when(j == 0)
    def _():
        x = x_ref[...]
        ms = jnp.mean(x * x, axis=-1, keepdims=True)
        hn_ref[...] = (x * lax.rsqrt(ms + EPS) * nw_ref[...]).astype(BF16)
    acc = jnp.dot(hn_ref[...], w_ref[...], preferred_element_type=F32)
    o_ref[...] = acc.astype(o_ref.dtype)

    @pl.when(j == pl.num_programs(1) - 1)
    def _():
        small_ref[...] = acc[:, small_off:small_off + LANE]


def _in_proj(x2, norm_w, w):
    n, d = x2.shape
    ncol = w.shape[1]
    tm = min(1024, n)
    tn = 768
    small_off = CB_SMALL * LANE - (ncol // tn - 1) * tn
    assert 0 <= small_off <= tn - LANE
    return pl.pallas_call(
        functools.partial(_in_proj_kernel, small_off=small_off),
        grid=(n // tm, ncol // tn),
        in_specs=[pl.BlockSpec((tm, d), lambda i, j: (i, 0)),
                  pl.BlockSpec((1, d), lambda i, j: (0, 0)),
                  pl.BlockSpec((d, tn), lambda i, j: (0, j))],
        out_specs=[pl.BlockSpec((tm, tn), lambda i, j: (i, j)),
                   pl.BlockSpec((tm, LANE), lambda i, j: (i, 0))],
        out_shape=[jax.ShapeDtypeStruct((n, ncol), BF16), jax.ShapeDtypeStruct((n, LANE), F32)],
        scratch_shapes=[pltpu.VMEM((tm, d), BF16)],
        compiler_params=_cparams(("parallel", "arbitrary")),
        name="in_proj",
    )(x2, norm_w.reshape(1, d), w)


def _compress_kernel(a_ref, c_ref, sa_ref, sb_ref, w_ref, pe_ref, o_ref, xs_ref):
    kv = pl.program_id(1)
    t = a_ref.shape[1]
    nc = t // CMP_STRIDE
    x = a_ref[0].astype(F32)
    xr = _rope(x, c_ref[0], sa_ref[0], sb_ref[0])
    x = jnp.where(kv == 0, xr, x)
    xs_ref[pl.ds(0, t), :] = x
    xs_ref[pl.ds(t, CMP_STRIDE), :] = jnp.zeros((CMP_STRIDE, LANE), F32)
    acc = jnp.zeros((nc, LANE), F32)
    for l in range(CMP_LEN):
        rows = xs_ref[pl.ds(l, nc, stride=CMP_STRIDE), :] + pe_ref[0, pl.ds(l, 1), :]
        acc = acc + _mm(rows, w_ref[0, l])
    o_ref[0, 0, 0] = acc.astype(BF16)


def _nsa_compress(proj3, tabs, cmp_w, cmp_pe):
    b, t, _ = proj3.shape
    g = N_KV_NSA
    nc = t // CMP_STRIDE
    tab_spec = pl.BlockSpec((1, t, LANE), lambda bi, kv, gi: (bi, 0, 0))
    return pl.pallas_call(
        _compress_kernel,
        grid=(b, 2, g),
        in_specs=[pl.BlockSpec((1, t, LANE), lambda bi, kv, gi: (bi, 0, CB_KV + kv * g + gi)),
                  tab_spec, tab_spec, tab_spec,
                  pl.BlockSpec((1, CMP_LEN, LANE, LANE), lambda bi, kv, gi: (kv, 0, 0, 0)),
                  pl.BlockSpec((1, CMP_LEN, LANE), lambda bi, kv, gi: (kv, 0, 0))],
        out_specs=pl.BlockSpec((1, 1, 1, nc, LANE), lambda bi, kv, gi: (bi, kv, gi, 0, 0)),
        out_shape=jax.ShapeDtypeStruct((b, 2, g, nc, LANE), BF16),
        scratch_shapes=[pltpu.VMEM((t + CMP_STRIDE, LANE), F32)],
        compiler_params=_cparams(("parallel", "arbitrary", "arbitrary")),
        name="nsa_compress",
    )(proj3, *tabs, cmp_w, cmp_pe)


def _load_q(q_ref, c, sa, sb):
    scale = HEAD_DIM ** -0.5
    qs = [(_rope(q_ref[0, :, r * LANE:(r + 1) * LANE].astype(F32), c, sa, sb) * scale).astype(BF16)
          for r in range(N_REP)]
    return jnp.concatenate(qs, axis=0)


def _store_gated(o_ref, o, gate_ref, g, branch, tq):
    gt = gate_ref[0]
    for r in range(N_REP):
        col = ((g * N_REP + r) * 3 + branch)
        lane = lax.broadcasted_iota(jnp.int32, gt.shape, 1)
        gcol = jnp.sum(jnp.where(lane == col, gt, 0.0), axis=1, keepdims=True)
        o_ref[0, :, r * LANE:(r + 1) * LANE] = (o[r * tq:(r + 1) * tq] * _sigmoid(gcol)).astype(o_ref.dtype)


Q_TILES_PER_STEP = 2


def _row_views(refs, sub, tq):
    rows = pl.ds(sub * tq, tq)
    return [r.at[:, rows, :] if len(r.shape) == 3 else r.at[:, :, rows, :] for r in refs]


def _cmp_attn_kernel(q_ref, c_ref, sa_ref, sb_ref, kc_ref, vc_ref, gate_ref, o_ref, sel_ref, *, n_slc):
    tq = q_ref.shape[1] // Q_TILES_PER_STEP
    for sub in range(Q_TILES_PER_STEP):
        q, c, sa, sb, gate, o, sel = _row_views(
            (q_ref, c_ref, sa_ref, sb_ref, gate_ref, o_ref, sel_ref), sub, tq)
        _cmp_attn_tile(pl.program_id(2) * Q_TILES_PER_STEP + sub, q, c, sa, sb, kc_ref, vc_ref, gate, o,
                       sel, n_slc=n_slc)


def _cmp_attn_tile(i, q_ref, c_ref, sa_ref, sb_ref, kc_ref, vc_ref, gate_ref, o_ref, sel_ref, *, n_slc):
    g = pl.program_id(1)
    tq = q_ref.shape[1]
    nc = kc_ref.shape[3]
    q4 = _load_q(q_ref, c_ref[0], sa_ref[0], sb_ref[0])
    s = _mm_nt(q4, kc_ref[0, 0, 0])
    row = lax.broadcasted_iota(jnp.int32, s.shape, 0)
    n = lax.broadcasted_iota(jnp.int32, s.shape, 1)
    tpos = i * tq + (row & (tq - 1))
    mask = (n * CMP_STRIDE + (CMP_LEN - 1) <= tpos) & (n < nc - 1)
    sm = jnp.where(mask, s, -1e30)
    m = _rowmax(sm)
    p = jnp.where(mask, jnp.exp(sm - m), 0.0)
    l = _rowsum(p)
    p = p / jnp.maximum(l, 1e-30)
    o = _mm(p, vc_ref[0, 0, 0])
    _store_gated(o_ref, o, gate_ref, g, 0, tq)

    ps = p[0:tq]
    for r in range(1, N_REP):
        ps = ps + p[r * tq:(r + 1) * tq]
    cn = lax.broadcasted_iota(jnp.int32, (nc, LANE), 0)
    cj = lax.broadcasted_iota(jnp.int32, (nc, LANE), 1)
    ratio = SLC_LEN // CMP_STRIDE
    agg = ((cn >= ratio * cj - (CMP_LEN // CMP_STRIDE - 1)) & (cn < ratio * cj + ratio)
           & (cn < nc - 1) & (cj < n_slc))
    agg = jnp.where(agg, 1.0, 0.0).astype(BF16)
    ps_hi = ps.astype(BF16)
    ps_lo = (ps - ps_hi.astype(F32)).astype(BF16)
    imp = (jnp.dot(ps_hi, agg, preferred_element_type=F32)
           + jnp.dot(ps_lo, agg, preferred_element_type=F32))

    j = lax.broadcasted_iota(jnp.int32, (tq, LANE), 1)
    tt = i * tq + lax.broadcasted_iota(jnp.int32, (tq, LANE), 0)
    cur = tt // SLC_LEN
    valid = j <= cur
    forced = (j == 0) | (j == cur) | (j == cur - 1)
    vals = jnp.where(forced, 1e30, jnp.where(valid, imp, -1.0))
    vt = vals.T
    jb = lax.broadcasted_iota(jnp.int32, (n_slc, tq), 0)
    vb = vt[0:n_slc]
    cnt = jnp.zeros((n_slc, tq), F32)
    for jp in range(n_slc):
        cand = vt[jp:jp + 1, :]
        ge = jnp.where(cand >= vb, 1.0, 0.0)
        gt = jnp.where(cand > vb, 1.0, 0.0)
        cnt = cnt + jnp.where(jb > jp, ge, gt)
    keep = jnp.where(cnt < float(min(SLC_TOP, n_slc)), 0.0, NEG_BIAS)
    if n_slc < LANE:
        keep = jnp.concatenate([keep, jnp.full((LANE - n_slc, tq), NEG_BIAS, F32)], axis=0)
    sel_ref[0, 0] = jnp.where(valid, keep.T, NEG_BIAS).astype(BF16)


def _nsa_cmp_attn(proj3, small3, tabs, kvc, tq=128):
    b, t, _ = proj3.shape
    g = N_KV_NSA
    nc = t // CMP_STRIDE
    n_slc = t // SLC_LEN
    rw = N_REP * LANE
    bq = tq * Q_TILES_PER_STEP
    tab_spec = pl.BlockSpec((1, bq, LANE), lambda bi, gi, i: (bi, i, 0))
    return pl.pallas_call(
        functools.partial(_cmp_attn_kernel, n_slc=n_slc),
        grid=(b, g, t // bq),
        in_specs=[pl.BlockSpec((1, bq, rw), lambda bi, gi, i: (bi, i, gi)),
                  tab_spec, tab_spec, tab_spec,
                  pl.BlockSpec((1, 1, 1, nc, LANE), lambda bi, gi, i: (bi, 0, gi, 0, 0)),
                  pl.BlockSpec((1, 1, 1, nc, LANE), lambda bi, gi, i: (bi, 1, gi, 0, 0)),
                  pl.BlockSpec((1, bq, LANE), lambda bi, gi, i: (bi, i, 0))],
        out_specs=[pl.BlockSpec((1, bq, rw), lambda bi, gi, i: (bi, i, gi)),
                   pl.BlockSpec((1, 1, bq, LANE), lambda bi, gi, i: (bi, gi, i, 0))],
        out_shape=[jax.ShapeDtypeStruct((b, t, N_HEADS_NSA * LANE), BF16),
                   jax.ShapeDtypeStruct((b, g, t, LANE), BF16)],
        compiler_params=_cparams(("parallel", "parallel", "parallel")),
        name="nsa_cmp",
    )(proj3, *tabs, kvc, kvc, small3)


def _sel_attn_kernel(q_ref, c_ref, sa_ref, sb_ref, cf_ref, saf_ref, sbf_ref, k_ref, v_ref,
                     sel_ref, gate_ref, o_ref, *scratch, tk):
    tq = q_ref.shape[1] // Q_TILES_PER_STEP
    for sub in range(Q_TILES_PER_STEP):
        q, c, sa, sb, sel, gate, o = _row_views(
            (q_ref, c_ref, sa_ref, sb_ref, sel_ref, gate_ref, o_ref), sub, tq)
        _sel_attn_tile(pl.program_id(2) * Q_TILES_PER_STEP + sub, q, c, sa, sb, cf_ref, saf_ref, sbf_ref,
                       k_ref, v_ref, sel, gate, o, *scratch, tk=tk)


def _sel_attn_tile(i, q_ref, c_ref, sa_ref, sb_ref, cf_ref, saf_ref, sbf_ref, k_ref, v_ref,
                   sel_ref, gate_ref, o_ref, ka_ref, vs_ref, m_ref, l_ref, acc_ref, s0_ref, s1_ref,
                   *, tk):
    g = pl.program_id(1)
    tq = q_ref.shape[1]
    t = k_ref.shape[1]
    rows = N_REP * tq

    @pl.when(i == 0)
    def _():
        kr = _rope(k_ref[0].astype(F32), cf_ref[0], saf_ref[0], sbf_ref[0])
        ka_ref[:, 0:LANE] = kr.astype(BF16)
        pos = lax.broadcasted_iota(jnp.int32, (t, LANE), 0)
        lane = lax.broadcasted_iota(jnp.int32, (t, LANE), 1)
        ka_ref[:, LANE:2 * LANE] = jnp.where(lane == pos // SLC_LEN, 1.0, 0.0).astype(BF16)
        vs_ref[...] = v_ref[0].astype(BF16)

    q4 = _load_q(q_ref, c_ref[0], sa_ref[0], sb_ref[0])
    bias = sel_ref[0, 0]
    qa = jnp.concatenate([q4, jnp.concatenate([bias] * N_REP, axis=0)], axis=1)

    m_ref[...] = jnp.full((rows, LANE), -1e30, F32)
    l_ref[...] = jnp.zeros((rows, LANE), F32)
    acc_ref[...] = jnp.zeros((rows, LANE), F32)

    half = tk // 2

    def scores(kt, hlf):
        k0 = pl.multiple_of(kt * tk + hlf * half, half)
        return _mm_nt(qa, ka_ref[pl.ds(k0, half), :])

    def update(s_ref, kt, hlf, causal):
        k0 = pl.multiple_of(kt * tk + hlf * half, half)
        s = s_ref[...]
        if causal:
            r = lax.broadcasted_iota(jnp.int32, s.shape, 0)
            kp = k0 + lax.broadcasted_iota(jnp.int32, s.shape, 1)
            s = jnp.where(kp <= i * tq + (r & (tq - 1)), s, -1e30)
        tiles = _lane_tiles(s)
        m_old = m_ref[...]
        m_new = jnp.maximum(m_old, _rowmax(s))
        alpha = jnp.exp(m_old - m_new)
        ps = [jnp.exp(tl - m_new) for tl in tiles]
        l_ref[...] = alpha * l_ref[...] + jnp.sum(functools.reduce(jnp.add, ps), axis=1, keepdims=True)
        p = jnp.concatenate([x.astype(BF16) for x in ps], axis=1)
        acc_ref[...] = alpha * acc_ref[...] + jnp.dot(p, vs_ref[pl.ds(k0, half), :],
                                                      preferred_element_type=F32)
        m_ref[...] = m_new

    diag = (i * tq) // tk
    s0_ref[...] = scores(0, 0)

    def body(kt, carry):
        s1_ref[...] = scores(kt, 1)
        update(s0_ref, kt, 0, False)
        s0_ref[...] = scores(kt + 1, 0)
        update(s1_ref, kt, 1, False)
        return carry

    def body2(j, carry):
        body(2 * j, carry)
        body(2 * j + 1, carry)
        return carry

    lax.fori_loop(0, diag // 2, body2, 0)
    lax.fori_loop((diag // 2) * 2, diag, body, 0)
    s1_ref[...] = scores(diag, 1)
    update(s0_ref, diag, 0, True)
    update(s1_ref, diag, 1, True)
    o = acc_ref[...] / l_ref[...]
    _store_gated(o_ref, o, gate_ref, g, 1, tq)


def _nsa_sel_attn(proj3, small3, tabs, selbias, tq=128, tk=512):
    b, t, _ = proj3.shape
    g = N_KV_NSA
    rw = N_REP * LANE
    bq = tq * Q_TILES_PER_STEP
    tk = min(tk, t)
    tab_spec = pl.BlockSpec((1, bq, LANE), lambda bi, gi, i: (bi, i, 0))
    tabf_spec = pl.BlockSpec((1, t, LANE), lambda bi, gi, i: (bi, 0, 0))
    kcb = CB_KV + 1 * 2 * g
    return pl.pallas_call(
        functools.partial(_sel_attn_kernel, tk=tk),
        grid=(b, g, t // bq),
        in_specs=[pl.BlockSpec((1, bq, rw), lambda bi, gi, i: (bi, i, gi)),
                  tab_spec, tab_spec, tab_spec, tabf_spec, tabf_spec, tabf_spec,
                  pl.BlockSpec((1, t, LANE), lambda bi, gi, i: (bi, 0, kcb + gi)),
                  pl.BlockSpec((1, t, LANE), lambda bi, gi, i: (bi, 0, kcb + g + gi)),
                  pl.BlockSpec((1, 1, bq, LANE), lambda bi, gi, i: (bi, gi, i, 0)),
                  pl.BlockSpec((1, bq, LANE), lambda bi, gi, i: (bi, i, 0))],
        out_specs=pl.BlockSpec((1, bq, rw), lambda bi, gi, i: (bi, i, gi)),
        out_shape=jax.ShapeDtypeStruct((b, t, N_HEADS_NSA * LANE), BF16),
        scratch_shapes=[pltpu.VMEM((t, 2 * LANE), BF16), pltpu.VMEM((t, LANE), BF16),
                        pltpu.VMEM((N_REP * tq, LANE), F32), pltpu.VMEM((N_REP * tq, LANE), F32),
                        pltpu.VMEM((N_REP * tq, LANE), F32),
                        pltpu.VMEM((N_REP * tq, tk // 2), F32), pltpu.VMEM((N_REP * tq, tk // 2), F32)],
        compiler_params=_cparams(("parallel", "parallel", "arbitrary")),
        name="nsa_sel",
    )(proj3, *tabs, *tabs, proj3, proj3, selbias, small3)


def _win_attn_kernel(q_ref, c_ref, sa_ref, sb_ref, cf_ref, saf_ref, sbf_ref, k_ref, v_ref,
                     gate_ref, o_ref, ks_ref, vs_ref, *, span):
    tq = q_ref.shape[1] // Q_TILES_PER_STEP
    for sub in range(Q_TILES_PER_STEP):
        q, c, sa, sb, gate, o = _row_views((q_ref, c_ref, sa_ref, sb_ref, gate_ref, o_ref), sub, tq)
        _win_attn_tile(pl.program_id(2) * Q_TILES_PER_STEP + sub, q, c, sa, sb, cf_ref, saf_ref, sbf_ref,
                       k_ref, v_ref, gate, o, ks_ref, vs_ref, span=span)


def _win_attn_tile(i, q_ref, c_ref, sa_ref, sb_ref, cf_ref, saf_ref, sbf_ref, k_ref, v_ref,
                   gate_ref, o_ref, ks_ref, vs_ref, *, span):
    g = pl.program_id(1)
    tq = q_ref.shape[1]

    t = k_ref.shape[1]

    @pl.when(i == 0)
    def _():
        zeros = jnp.zeros((WINDOW, LANE), BF16)
        ks_ref[pl.ds(0, WINDOW), :] = zeros
        vs_ref[pl.ds(0, WINDOW), :] = zeros
        ks_ref[pl.ds(WINDOW, t), :] = _rope(k_ref[0].astype(F32), cf_ref[0], saf_ref[0],
                                            sbf_ref[0]).astype(BF16)
        vs_ref[pl.ds(WINDOW, t), :] = v_ref[0].astype(BF16)

    q4 = _load_q(q_ref, c_ref[0], sa_ref[0], sb_ref[0])
    k0 = pl.multiple_of(i * tq, tq)
    s = _mm_nt(q4, ks_ref[pl.ds(k0, span), :])
    tiles = _lane_tiles(s)
    r = lax.broadcasted_iota(jnp.int32, tiles[0].shape, 0) & (tq - 1)
    c = lax.broadcasted_iota(jnp.int32, tiles[0].shape, 1)
    first_block = WINDOW // tq - i
    masked = []
    for b, tl in enumerate(tiles):
        if b == 0:
            tl = jnp.where(c > r, tl, -1e30)
        if b == len(tiles) - 1:
            tl = jnp.where(c <= r, tl, -1e30)
        else:
            tl = jnp.where(b >= first_block, tl, -1e30)
        masked.append(tl)
    m = jnp.max(functools.reduce(jnp.maximum, masked), axis=1, keepdims=True)
    ps = [jnp.exp(tl - m) for tl in masked]
    l = jnp.sum(functools.reduce(jnp.add, ps), axis=1, keepdims=True)
    p = jnp.concatenate([x.astype(BF16) for x in ps], axis=1)
    o = jnp.dot(p, vs_ref[pl.ds(k0, span), :], preferred_element_type=F32) / l
    _store_gated(o_ref, o, gate_ref, g, 2, tq)


def _nsa_win_attn(proj3, small3, tabs, tq=128):
    b, t, _ = proj3.shape
    g = N_KV_NSA
    rw = N_REP * LANE
    bq = tq * Q_TILES_PER_STEP
    span = WINDOW + tq
    assert tq == LANE
    tab_spec = pl.BlockSpec((1, bq, LANE), lambda bi, gi, i: (bi, i, 0))
    tabf_spec = pl.BlockSpec((1, t, LANE), lambda bi, gi, i: (bi, 0, 0))
    kcb = CB_KV + 2 * 2 * g
    return pl.pallas_call(
        functools.partial(_win_attn_kernel, span=span),
        grid=(b, g, t // bq),
        in_specs=[pl.BlockSpec((1, bq, rw), lambda bi, gi, i: (bi, i, gi)),
                  tab_spec, tab_spec, tab_spec, tabf_spec, tabf_spec, tabf_spec,
                  pl.BlockSpec((1, t, LANE), lambda bi, gi, i: (bi, 0, kcb + gi)),
                  pl.BlockSpec((1, t, LANE), lambda bi, gi, i: (bi, 0, kcb + g + gi)),
                  pl.BlockSpec((1, bq, LANE), lambda bi, gi, i: (bi, i, 0))],
        out_specs=pl.BlockSpec((1, bq, rw), lambda bi, gi, i: (bi, i, gi)),
        out_shape=jax.ShapeDtypeStruct((b, t, N_HEADS_NSA * LANE), BF16),
        scratch_shapes=[pltpu.VMEM((t + WINDOW, LANE), BF16), pltpu.VMEM((t + WINDOW, LANE), BF16)],
        compiler_params=_cparams(("parallel", "parallel", "arbitrary")),
        name="nsa_win",
    )(proj3, *tabs, *tabs, proj3, proj3, small3)


def _gdn_kernel(q_ref, k_ref, v_ref, z_ref, braw_ref, araw_ref, cwq_ref, cwk_ref, cwv_ref, alog_ref,
                dtb_ref, nw_ref, o_ref, q_s, k_s, v_s, o_s, qp_s, op_s, xp_s, bd_s, gcd_s, gl_s, mm_s,
                nn_s):
    t = q_ref.shape[1]
    c = GDN_CHUNK
    nchunk = t // c
    xp_s[pl.ds(0, 8), :] = jnp.zeros((8, LANE), F32)

    def conv_silu(x, cw):
        xp_s[pl.ds(8, t), :] = x
        y = x * cw[CONV_WIDTH - 1:CONV_WIDTH]
        for sft in range(1, CONV_WIDTH):
            y = y + xp_s[pl.ds(8 - sft, t), :] * cw[CONV_WIDTH - 1 - sft:CONV_WIDTH - sft]
        return y * _sigmoid(y)

    def l2n(x):
        return x * lax.rsqrt(jnp.sum(x * x, axis=1, keepdims=True) + EPS)

    q_s[...] = l2n(conv_silu(q_ref[0].astype(F32), cwq_ref[...])) * (HEAD_DIM ** -0.5)
    k_s[...] = l2n(conv_silu(k_ref[0].astype(F32), cwk_ref[...]))
    v_s[...] = conv_silu(v_ref[0].astype(F32), cwv_ref[...])

    bd_s[...] = _sigmoid(braw_ref[0, 0])
    xa = araw_ref[0, 0] + dtb_ref[0]
    softplus = jnp.maximum(xa, 0.0) + jnp.log(1.0 + jnp.exp(-jnp.abs(xa)))
    gc = -jnp.exp(alog_ref[0]) * softplus
    lane = lax.broadcasted_iota(jnp.int32, gc.shape, 1)
    sft = 1
    while sft < c:
        gc = gc + jnp.where((lane & (c - 1)) >= sft, pltpu.roll(gc, sft, 1), 0.0)
        sft *= 2
    gcd_s[...] = gc

    c2 = 2 * c
    ci = lax.broadcasted_iota(jnp.int32, (c2, c2), 0)
    cj = lax.broadcasted_iota(jnp.int32, (c2, c2), 1)
    same = (ci // c) == (cj // c)
    tril = same & (cj <= ci)
    strict = same & (cj < ci)
    eye = jnp.where(ci == cj, 1.0, 0.0)
    first = ci < c

    pairs_per_iter = 8
    rng = range(pairs_per_iter)

    def prep(it, carry):
        n2 = [it * pairs_per_iter + p for p in rng]
        sl = [pl.ds(pl.multiple_of(n * c2, c2), c2) for n in n2]
        grow = [gcd_s[pl.ds(n, 1), :] for n in n2]
        gc2 = [jnp.broadcast_to(g_, (c2, c2)).T for g_ in grow]
        betac = [jnp.broadcast_to(bd_s[pl.ds(n, 1), :], (c2, c2)).T for n in n2]
        kn = [k_s[s_, :] for s_ in sl]
        kbn = [kn[p] * betac[p] for p in rng]
        decay = [jnp.exp(jnp.where(tril, gc2[p] - grow[p], -1e30)) for p in rng]
        kk = [_mm_nt(kbn[p], kn[p]) for p in rng]
        y = [-jnp.where(strict, kk[p] * decay[p], 0.0) for p in rng]
        pm = [eye + y_ for y_ in y]
        pw = 1
        while 2 * pw < c:
            y = [_mm(y_, y_) for y_ in y]
            pm = [pm[p] + _mm(pm[p], y[p]) for p in rng]
            pw *= 2
        egn = [jnp.exp(g_) for g_ in gc2]
        uw = [_mm(pm[p], jnp.concatenate([v_s[sl[p], :] * betac[p], kbn[p] * egn[p]], axis=1))
              for p in rng]
        qn = [q_s[s_, :] for s_ in sl]
        qk = [_mm_nt(qn[p], kn[p]) for p in rng]
        attn = [jnp.where(tril, qk[p] * decay[p], 0.0) for p in rng]
        auw = [_mm(attn[p], uw[p]) for p in rng]
        for p in rng:
            qp_s[sl[p], :] = qn[p] * egn[p] - auw[p][:, LANE:]
            op_s[sl[p], :] = auw[p][:, :LANE]
        gl = [jnp.where(first, g_[c - 1:c, :], g_[c2 - 1:c2, :]) for g_ in gc2]
        kdt = [(kn[p] * jnp.exp(gl[p] - gc2[p])).T for p in rng]
        nma = [_mm(kdt[p][:, 0:c], uw[p][0:c]) for p in rng]
        nmb = [_mm(kdt[p][:, c:c2], uw[p][c:c2]) for p in rng]
        mb = [x[:, LANE:] for x in nmb]
        mbnm = [_mm(mb[p], nma[p]) for p in rng]
        ma = [x[:, LANE:] for x in nma]
        na = [x[:, :LANE] for x in nma]
        nb_ = [x[:, :LANE] for x in nmb]
        mba = [x[:, LANE:] for x in mbnm]
        mbn = [x[:, :LANE] for x in mbnm]
        for p in rng:
            ga = jnp.exp(gc2[p][c - 1:c, :])
            gb = jnp.exp(gc2[p][c2 - 1:c2, :])
            mm_s[2 * n2[p]] = ma[p].astype(BF16)
            nn_s[2 * n2[p]] = na[p]
            gl_s[pl.ds(2 * n2[p], 1), :] = ga
            mm_s[2 * n2[p] + 1] = (gb * ma[p] + ga * mb[p] - mba[p]).astype(BF16)
            nn_s[2 * n2[p] + 1] = gb * na[p] - mbn[p] + nb_[p]
            gl_s[pl.ds(2 * n2[p] + 1, 1), :] = ga * gb
        return carry

    lax.fori_loop(0, nchunk // (2 * pairs_per_iter), prep, 0)

    def second_out(n2, s_mid):
        rows = pl.ds(pl.multiple_of(n2 * c2, c2) + c, c)
        o_s[rows, :] = (jnp.dot(qp_s[rows, :].astype(BF16), s_mid.astype(BF16),
                                 preferred_element_type=F32) + op_s[rows, :])

    def scan(n2, carry):
        s, s_mid_prev = carry
        second_out(jnp.maximum(n2 - 1, 0), s_mid_prev)
        ra = pl.ds(pl.multiple_of(n2 * c2, c2), c)
        sb = s.astype(BF16)
        s_mid = (s * gl_s[pl.ds(2 * n2, 1), :] - jnp.dot(mm_s[2 * n2], sb, preferred_element_type=F32)
                 + nn_s[2 * n2])
        s_new = (s * gl_s[pl.ds(2 * n2 + 1, 1), :]
                 - jnp.dot(mm_s[2 * n2 + 1], sb, preferred_element_type=F32) + nn_s[2 * n2 + 1])
        o_s[ra, :] = jnp.dot(qp_s[ra, :].astype(BF16), sb, preferred_element_type=F32) + op_s[ra, :]
        return s_new, s_mid

    zero_state = jnp.zeros((LANE, LANE), F32)
    _, s_mid_last = lax.fori_loop(0, nchunk // 2, scan, (zero_state, zero_state))
    second_out(jnp.int32(nchunk // 2 - 1), s_mid_last)

    o = o_s[...]
    on = o * lax.rsqrt(jnp.mean(o * o, axis=1, keepdims=True) + EPS) * nw_ref[...]
    z = z_ref[0].astype(F32)
    o_ref[0] = (on * (z * _sigmoid(z))).astype(o_ref.dtype)


def _gdn(proj3, small3, conv_w, a_log, dt_bias, norm_w):
    b, t, _ = proj3.shape
    hh = N_HEADS_GDN
    c = GDN_CHUNK
    col = lambda cb: pl.BlockSpec((1, t, LANE), lambda bi, hi: (bi, 0, cb + hi))
    cw = lambda off: pl.BlockSpec((CONV_WIDTH, LANE), lambda bi, hi: (0, off + hi))
    hrow = pl.BlockSpec((1, 1, LANE), lambda bi, hi: (hi, 0, 0))
    alog_b = jnp.broadcast_to(a_log.astype(F32)[:, None, None], (hh, 1, LANE))
    dtb_b = jnp.broadcast_to(dt_bias.astype(F32)[:, None, None], (hh, 1, LANE))
    big = pltpu.VMEM((t, LANE), F32)
    nrow = t // LANE
    ba = small3[:, :, SMALL_BETA:SMALL_BETA + 2 * hh].transpose(0, 2, 1).reshape(b, 2 * hh, nrow, LANE)
    dense = lambda off: pl.BlockSpec((1, 1, nrow, LANE), lambda bi, hi: (bi, off + hi, 0, 0))
    return pl.pallas_call(
        _gdn_kernel,
        grid=(b, hh),
        in_specs=[col(CB_GQKV), col(CB_GQKV + hh), col(CB_GQKV + 2 * hh), col(CB_Z),
                  dense(0), dense(hh),
                  cw(0), cw(hh), cw(2 * hh), hrow, hrow,
                  pl.BlockSpec((1, LANE), lambda bi, hi: (0, 0))],
        out_specs=pl.BlockSpec((1, t, LANE), lambda bi, hi: (bi, 0, hi)),
        out_shape=jax.ShapeDtypeStruct((b, t, hh * LANE), BF16),
        scratch_shapes=[big, big, big, big, big, big, pltpu.VMEM((t + 8, LANE), F32),
                        pltpu.VMEM((nrow, LANE), F32), pltpu.VMEM((nrow, LANE), F32),
                        pltpu.VMEM((t // c, LANE), F32),
                        pltpu.VMEM((t // c, LANE, LANE), BF16),
                        pltpu.VMEM((t // c, LANE, LANE), F32)],
        compiler_params=_cparams(("parallel", "parallel")),
        name="gdn",
    )(proj3, proj3, proj3, proj3, ba, ba, conv_w, conv_w, conv_w, alog_b, dtb_b,
      norm_w.reshape(1, LANE))


def _split3(a):
    hi = a.astype(BF16)
    lo = (a - hi.astype(F32)).astype(BF16)
    return hi, lo


def _out_proj_kernel(oc_ref, os_ref, ow_ref, ob_ref, x_ref, wo_ref, fw_ref, wr_ref, br_ref,
                     h_ref, hn_ref, ids_ref, wts_ref, tmp_ref):
    half = oc_ref.shape[1]
    oa = (oc_ref[...].astype(F32) + os_ref[...].astype(F32) + ow_ref[...].astype(F32)).astype(BF16)
    h1 = (x_ref[...] + jnp.dot(oa, wo_ref[0:half, :], preferred_element_type=F32)
          + jnp.dot(ob_ref[...], wo_ref[half:2 * half, :], preferred_element_type=F32))
    h_ref[...] = h1
    hn = h1 * lax.rsqrt(jnp.mean(h1 * h1, axis=1, keepdims=True) + EPS) * fw_ref[...]
    _store_token_major(hn_ref, hn, tmp_ref)

    a_hi, a_lo = _split3(hn)
    w_hi, w_lo = _split3(wr_ref[...])
    dot = lambda a, b: jnp.dot(a, b, preferred_element_type=F32)
    logits = dot(a_hi, w_hi) + dot(a_hi, w_lo) + dot(a_lo, w_hi) + br_ref[...]
    lane = lax.broadcasted_iota(jnp.int32, logits.shape, 1)
    big = 1e30
    is_g = lane < N_GROUPS
    lg = jnp.where(is_g, logits, -big)
    gm = jnp.max(lg, axis=1, keepdims=True)
    grp = jnp.min(jnp.where(lg == gm, lane, LANE), axis=1, keepdims=True)
    p_grp = 1.0 / jnp.sum(jnp.where(is_g, jnp.exp(lg - gm), 0.0), axis=1, keepdims=True)
    e_id = lane - N_GROUPS
    in_g = (e_id >= 0) & (e_id < N_EXPERTS) & ((e_id // EXPERTS_PER_GROUP) == grp)
    le = jnp.where(in_g, logits, -big)
    em = jnp.max(le, axis=1, keepdims=True)
    pe = jnp.where(in_g, jnp.exp(le - em), 0.0)
    pe = pe / jnp.sum(pe, axis=1, keepdims=True)
    pm = jnp.where(in_g, pe, -1.0)
    p1 = jnp.max(pm, axis=1, keepdims=True)
    i1 = jnp.min(jnp.where(pm == p1, lane, LANE), axis=1, keepdims=True)
    pm2 = jnp.where(lane == i1, -1.0, pm)
    p2 = jnp.max(pm2, axis=1, keepdims=True)
    i2 = jnp.min(jnp.where(pm2 == p2, lane, LANE), axis=1, keepdims=True)
    den = p1 + p2
    ids_ref[...] = jnp.where(lane == 0, i1 - N_GROUPS, jnp.where(lane == 1, i2 - N_GROUPS, 0))
    wts_ref[...] = jnp.where(lane == 0, p1 / den * p_grp, jnp.where(lane == 1, p2 / den * p_grp, 0.0))


def _out_proj(oc, os_, ow, ob, x2, w_out, ffn_w, wr, br):
    n, d = x2.shape
    half = oc.shape[1]
    seg = d // LANE
    tm = min(256, n)
    row = lambda w: pl.BlockSpec((tm, w), lambda i: (i, 0))
    full = lambda a: pl.BlockSpec(a.shape, lambda i: (0,) * a.ndim)
    fw = ffn_w.reshape(1, d)
    return pl.pallas_call(
        _out_proj_kernel,
        grid=(n // tm,),
        in_specs=[row(half), row(half), row(half), row(half), row(d), full(w_out), full(fw),
                  full(wr), full(br)],
        out_specs=[row(d), pl.BlockSpec((tm * seg, LANE), lambda i: (i, 0)), row(LANE), row(LANE)],
        out_shape=[jax.ShapeDtypeStruct((n, d), F32), jax.ShapeDtypeStruct((n * seg, LANE), BF16),
                   jax.ShapeDtypeStruct((n, LANE), jnp.int32), jax.ShapeDtypeStruct((n, LANE), F32)],
        scratch_shapes=[pltpu.VMEM((tm * seg, LANE), F32)],
        compiler_params=_cparams(("parallel",)),
        name="out_proj",
    )(oc, os_, ow, ob, x2, w_out, fw, wr, br)


def _moe_kernel(bexp_ref, bact_ref, bfirst_ref, bord_ref, bnext_ref, rtok_ref, rtokn_ref, rslotp_ref,
                rslot_ref, hn_hbm, wg_hbm, wu_hbm, wd_hbm, y_hbm, xbuf, ybuf, tmp, wg_st, wu_st, wd_st,
                wgb, wub, wdb, gsem, ssem, wsem, *, n_real):
    i = pl.program_id(0)
    nb = pl.num_programs(0)
    rb = rtok_ref.shape[2]
    seg = ybuf.shape[1] // rb
    slot = i & 1
    other = 1 - slot
    active = bact_ref[i] > 0
    prev_active = jnp.logical_and(i > 0, bact_ref[jnp.maximum(i - 1, 0)] > 0)

    def gather_copy(tok, r, buf):
        return pltpu.make_async_copy(hn_hbm.at[pl.ds(pl.multiple_of(tok, seg), seg), :],
                                     xbuf.at[buf, pl.ds(r * seg, seg), :], gsem.at[buf])

    def scatter_copy(dst, r, buf):
        return pltpu.make_async_copy(ybuf.at[buf, pl.ds(r * seg, seg), :],
                                     y_hbm.at[pl.ds(pl.multiple_of(dst, seg), seg), :], ssem.at[buf])

    def wait_gather(buf):
        pltpu.make_async_copy(hn_hbm.at[pl.ds(0, rb * seg), :], xbuf.at[buf], gsem.at[buf]).wait()

    def wait_scatter(buf):
        pltpu.make_async_copy(ybuf.at[buf], y_hbm.at[pl.ds(0, rb * seg), :], ssem.at[buf]).wait()

    def weight_copies(e, ws):
        return (pltpu.make_async_copy(wg_hbm.at[e], wg_st.at[ws], wsem.at[ws]),
                pltpu.make_async_copy(wu_hbm.at[e], wu_st.at[ws], wsem.at[ws]),
                pltpu.make_async_copy(wd_hbm.at[e], wd_st.at[ws], wsem.at[ws]))

    @pl.when(i == 0)
    def _():
        for cp in weight_copies(bexp_ref[0], 0):
            cp.start(priority=1)
        ybuf[...] = jnp.zeros(ybuf.shape, ybuf.dtype)
        pltpu.make_async_copy(ybuf.at[0], y_hbm.at[pl.ds(n_real * seg, rb * seg), :],
                              ssem.at[0]).start()

        def first(r, carry):
            gather_copy(rtok_ref[0, 0, r], r, 0).start()
            return carry
        lax.fori_loop(0, rb, first, 0)

    @pl.when(jnp.logical_or(i == 0, prev_active))
    def _():
        wait_gather(slot)

    @pl.when(active)
    def _():
        @pl.when(bfirst_ref[i] > 0)
        def _():
            ws = bord_ref[i] & 1
            for cp in weight_copies(bexp_ref[i], ws):
                cp.wait()

            @pl.when(bnext_ref[i] >= 0)
            def _():
                for cp in weight_copies(bnext_ref[i], 1 - ws):
                    cp.start(priority=1)

            wgb[...] = wg_st[ws].astype(BF16)
            wub[...] = wu_st[ws].astype(BF16)
            wdb[...] = wd_st[ws].astype(BF16)

        n_piece = 8
        per = rb // n_piece

        def issue(piece):
            for r in range(piece * per, (piece + 1) * per):
                gather_copy(rtokn_ref[0, 0, r], r, other).start()
                scatter_copy(rslotp_ref[0, 0, r], r, other).start()

        x = _load_token_major(xbuf.at[slot], rb, tmp).astype(BF16)
        de = wgb.shape[1]
        d = wdb.shape[1]
        hc = de // 2
        acts = []
        for c in range(2):
            issue(2 * c)
            hg = jnp.dot(x, wgb[:, c * hc:(c + 1) * hc], preferred_element_type=F32)
            issue(2 * c + 1)
            hu = jnp.dot(x, wub[:, c * hc:(c + 1) * hc], preferred_element_type=F32)
            acts.append((hg * _sigmoid(hg) * hu).astype(BF16))
        act = jnp.concatenate(acts, axis=1)
        dc = d // 4
        ys = []
        for j in range(4):
            issue(4 + j)
            ys.append(jnp.dot(act, wdb[:, j * dc:(j + 1) * dc], preferred_element_type=F32))
        wait_scatter(slot)
        _store_token_major(ybuf.at[slot], jnp.concatenate(ys, axis=1), tmp)

    def scatter_all(idx_ref, buf):
        def body(r, carry):
            scatter_copy(idx_ref[0, 0, r], r, buf).start()
            return carry
        lax.fori_loop(0, rb, body, 0)

    @pl.when(jnp.logical_and(jnp.logical_not(active), prev_active))
    def _():
        wait_scatter(slot)
        scatter_all(rslotp_ref, other)
        wait_scatter(other)

    @pl.when(jnp.logical_and(i == nb - 1, active))
    def _():
        wait_scatter(other)
        scatter_all(rslot_ref, slot)
        wait_scatter(slot)
        wait_gather(other)


def _moe(hn, bexp, bact, rtok, rslot, w_gate, w_up, w_down, n_real):
    d, de = w_gate.shape[1], w_gate.shape[2]
    seg = d // LANE
    nb = bexp.shape[0]
    rb = MOE_ROW_BLOCK
    idx = jnp.arange(nb, dtype=jnp.int32)
    prev_e = jnp.concatenate([jnp.full((1,), -1, jnp.int32), bexp[:-1]])
    bfirst = jnp.logical_and(bact > 0, jnp.logical_or(idx == 0, bexp != prev_e)).astype(jnp.int32)
    bord = jnp.cumsum(bfirst) - 1
    first_at = jnp.where(bfirst > 0, idx, nb)
    next_first = jnp.concatenate([lax.cummin(first_at[::-1])[::-1][1:], jnp.full((1,), nb, jnp.int32)])
    bnext = jnp.where(next_first < nb, bexp[jnp.minimum(next_first, nb - 1)], -1).astype(jnp.int32)

    smem_rows = pl.BlockSpec((1, 1, rb), lambda i, *_: (i, 0, 0), memory_space=pltpu.SMEM)
    smem_next = pl.BlockSpec((1, 1, rb), lambda i, *_: (jnp.minimum(i + 1, nb - 1), 0, 0),
                             memory_space=pltpu.SMEM)
    smem_cur = pl.BlockSpec((1, 1, rb), lambda i, *_: (i + 1, 0, 0), memory_space=pltpu.SMEM)
    hbm = pl.BlockSpec(memory_space=pl.ANY)
    grid_spec = pltpu.PrefetchScalarGridSpec(
        num_scalar_prefetch=5,
        grid=(nb,),
        in_specs=[smem_rows, smem_next, smem_rows, smem_cur, hbm, hbm, hbm, hbm],
        out_specs=hbm,
        scratch_shapes=[pltpu.VMEM((2, rb * seg, LANE), BF16), pltpu.VMEM((2, rb * seg, LANE), BF16),
                        pltpu.VMEM((rb * seg, LANE), F32),
                        pltpu.VMEM((2, d, de), F32), pltpu.VMEM((2, d, de), F32), pltpu.VMEM((2, de, d), F32),
                        pltpu.VMEM((d, de), BF16), pltpu.VMEM((d, de), BF16), pltpu.VMEM((de, d), BF16),
                        pltpu.SemaphoreType.DMA((2,)), pltpu.SemaphoreType.DMA((2,)),
                        pltpu.SemaphoreType.DMA((2,))])
    return pl.pallas_call(
        functools.partial(_moe_kernel, n_real=n_real),
        grid_spec=grid_spec,
        out_shape=jax.ShapeDtypeStruct(((n_real + 2 * rb) * seg, LANE), BF16),
        compiler_params=_cparams(("arbitrary",)),
        name="moe",
    )(bexp, bact, bfirst, bord, bnext, rtok, rtok, rslot, rslot, hn, w_gate, w_up, w_down)


def _dispatch(ids, n, seg):
    k = 2
    m = n * k
    rb = MOE_ROW_BLOCK
    e_flat = ids[:, :k].reshape(m)
    onehot = (e_flat[:, None] == jnp.arange(N_EXPERTS, dtype=jnp.int32)[None, :]).astype(jnp.int32)
    csum = jnp.cumsum(onehot, axis=0)
    counts = csum[-1]
    rank = jnp.take_along_axis(csum, e_flat[:, None], axis=1)[:, 0] - 1
    padded = (counts + rb - 1) // rb * rb
    pad_end = jnp.cumsum(padded)
    pad_start = pad_end - padded
    dest = pad_start[e_flat] + rank
    nb = (m + N_EXPERTS * (rb - 1) + rb - 1) // rb
    p = nb * rb
    row_m = jnp.full((p,), -1, jnp.int32).at[dest].set(jnp.arange(m, dtype=jnp.int32))
    real = row_m >= 0
    rtok = jnp.where(real, row_m // k, 0)
    pidx = jnp.arange(p, dtype=jnp.int32)
    rslot = jnp.where(real, (row_m % k) * n + row_m // k, m + ((pidx // rb) % 2) * rb + pidx % rb)
    starts = jnp.arange(nb, dtype=jnp.int32) * rb
    bexp = jnp.minimum(jnp.sum((pad_end[None, :] <= starts[:, None]).astype(jnp.int32), axis=1),
                       N_EXPERTS - 1)
    bact = jnp.sum(real.reshape(nb, rb).astype(jnp.int32), axis=1)
    last_e = jnp.max(jnp.where(bact > 0, bexp, 0))
    bexp = jnp.where(bact > 0, bexp, last_e)
    rslot = jnp.concatenate([m + rb + jnp.arange(rb, dtype=jnp.int32), rslot])
    return (bexp, bact, (rtok * seg).reshape(nb, 1, rb), (rslot * seg).reshape(nb + 1, 1, rb),
            m)


def _combine_kernel(h_ref, y0_ref, y1_ref, wts_ref, fw_ref, o_ref, tmp_ref):
    tm = h_ref.shape[0]
    wts = wts_ref[...]
    y0 = _load_token_major(y0_ref, tm, tmp_ref)
    y1 = _load_token_major(y1_ref, tm, tmp_ref)
    moe = y0 * wts[:, 0:1] + y1 * wts[:, 1:2]
    h = h_ref[...] + moe
    o_ref[...] = h * lax.rsqrt(jnp.mean(h * h, axis=1, keepdims=True) + EPS) * fw_ref[...]


def _combine(h1, y, wts, final_w):
    n, d = h1.shape
    seg = d // LANE
    tm = min(256, n)
    nt = n // tm
    return pl.pallas_call(
        _combine_kernel,
        grid=(nt,),
        in_specs=[pl.BlockSpec((tm, d), lambda i: (i, 0)),
                  pl.BlockSpec((tm * seg, LANE), lambda i: (i, 0)),
                  pl.BlockSpec((tm * seg, LANE), lambda i: (i + nt, 0)),
                  pl.BlockSpec((tm, LANE), lambda i: (i, 0)),
                  pl.BlockSpec((1, d), lambda i: (0, 0))],
        out_specs=pl.BlockSpec((tm, d), lambda i: (i, 0)),
        out_shape=jax.ShapeDtypeStruct((n, d), F32),
        scratch_shapes=[pltpu.VMEM((tm * seg, LANE), F32)],
        compiler_params=_cparams(("parallel",)),
        name="combine",
    )(h1, y, y, wts, final_w.reshape(1, d))


def _rope_tables(positions):
    half = ROT_DIM // 2
    inv_freq = ROPE_THETA ** (-jnp.arange(0, ROT_DIM, 2, dtype=F32) / ROT_DIM)
    ang = positions.astype(F32)[..., None] * inv_freq
    cos, sin = jnp.cos(ang), jnp.sin(ang)
    b, t = positions.shape
    ones = jnp.ones((b, t, LANE - ROT_DIM), F32)
    zeros = jnp.zeros((b, t, LANE - half), F32)
    c = jnp.concatenate([cos, cos, ones], axis=-1)
    sa = jnp.concatenate([-sin, zeros], axis=-1)
    sb = jnp.concatenate([jnp.zeros((b, t, half), F32), sin, zeros[..., :LANE - ROT_DIM]], axis=-1)
    return c, sa, sb


def _arrange_w_in(w_in):
    d = w_in.shape[0]
    sizes = (N_HEADS_NSA * HEAD_DIM, 3 * 2 * N_KV_NSA * HEAD_DIM, 3 * N_HEADS_NSA,
             3 * N_HEADS_GDN * HEAD_DIM, N_HEADS_GDN, N_HEADS_GDN, N_HEADS_GDN * HEAD_DIM)
    offs = np.cumsum((0,) + sizes)
    seg = [w_in[:, offs[i]:offs[i + 1]] for i in range(len(sizes))]
    q, kv, gate, gqkv, gb, ga, gz = seg
    used = sum(sizes)
    pad = jnp.zeros((d, N_CB * LANE - used), w_in.dtype)
    return jnp.concatenate([q, kv, gqkv, gz, gate, gb, ga, pad], axis=1).astype(BF16)


def kernel(x, positions, attn_norm_w, w_in, cmp_wk, cmp_pek, cmp_wv, cmp_pev, gdn_conv_w, gdn_a_log,
           gdn_dt_bias, gdn_norm_w, w_out, ffn_norm_w, router_group_w, router_group_b,
           router_expert_w, router_expert_b, moe_w_gate, moe_w_up, moe_w_down, final_norm_w):
    b, t, d = x.shape
    n = b * t
    tabs = _rope_tables(positions)
    h = x.reshape(n, d)
    assert w_in.shape[0] == 1, "single-layer block only"
    for l in range(1):
        proj, small = _in_proj(h, attn_norm_w[l], _arrange_w_in(w_in[l]))
        proj3 = proj.reshape(b, t, N_CB * LANE)
        small3 = small.reshape(b, t, LANE)
        cmp_w = jnp.stack([cmp_wk[l], cmp_wv[l]])
        cmp_pe = jnp.stack([cmp_pek[l], cmp_pev[l]])
        kvc = _nsa_compress(proj3, tabs, cmp_w, cmp_pe)
        o_c, selbias = _nsa_cmp_attn(proj3, small3, tabs, kvc)
        o_s = _nsa_sel_attn(proj3, small3, tabs, selbias)
        o_w = _nsa_win_attn(proj3, small3, tabs)
        o_b = _gdn(proj3, small3, gdn_conv_w[l], gdn_a_log[l], gdn_dt_bias[l], gdn_norm_w[l])
        half = N_HEADS_NSA * HEAD_DIM
        wr = jnp.concatenate([router_group_w[l], router_expert_w[l],
                              jnp.zeros((d, LANE - N_GROUPS - N_EXPERTS), F32)], axis=1)
        br = jnp.concatenate([router_group_b[l], router_expert_b[l],
                              jnp.zeros((LANE - N_GROUPS - N_EXPERTS,), F32)]).reshape(1, LANE)
        h1, hn2, ids, wts = _out_proj(o_c.reshape(n, half), o_s.reshape(n, half), o_w.reshape(n, half),
                                      o_b.reshape(n, half), h, w_out[l].astype(BF16), ffn_norm_w[l], wr, br)
        bexp, bact, rtok, rslot, n_slots = _dispatch(ids, n, d // LANE)
        y = _moe(hn2, bexp, bact, rtok, rslot, moe_w_gate[l], moe_w_up[l], moe_w_down[l], n_slots)
        out = _combine(h1, y, wts, final_norm_w)
    return out.reshape(b, t, d)
```

```python
import functools

import numpy as np
import jax
import jax.numpy as jnp
from jax import lax
from jax.experimental import pallas as pl
from jax.experimental.pallas import tpu as pltpu

F32 = jnp.float32
BF16 = jnp.bfloat16

HEAD_DIM = 128
N_HEADS_NSA = 8
N_KV_NSA = 2
N_REP = N_HEADS_NSA // N_KV_NSA
N_HEADS_GDN = 8
ROT_DIM = 32
ROPE_THETA = 500000.0
CMP_LEN = 32
CMP_STRIDE = 16
SLC_LEN = 64
SLC_TOP = 16
WINDOW = 512
CONV_WIDTH = 4
GDN_CHUNK = 64
N_GROUPS = 8
EXPERTS_PER_GROUP = 8
N_EXPERTS = 64
MOE_ROW_BLOCK = 256
EPS = 1e-6
LANE = 128

CB_Q = 0
CB_KV = 8
CB_GQKV = 20
CB_Z = 44
CB_SMALL = 52
N_CB = 54
SMALL_BETA = 24
SMALL_DECAY = 32

NEG_BIAS = -32768.0
VMEM_LIMIT = 56 * 1024 * 1024


def _cparams(sem):
    return pltpu.CompilerParams(dimension_semantics=sem, vmem_limit_bytes=VMEM_LIMIT)


def _mm(a, b):
    return jnp.dot(a.astype(BF16), b.astype(BF16), preferred_element_type=F32)


def _mm_nt(a, b):
    return lax.dot_general(a.astype(BF16), b.astype(BF16), (((1,), (1,)), ((), ())),
                           preferred_element_type=F32)


def _rope(x, c, sa, sb):
    return (x * c + pltpu.roll(x, LANE - ROT_DIM // 2, 1) * sa
            + pltpu.roll(x, ROT_DIM // 2, 1) * sb)


def _sigmoid(x):
    return 0.5 * jnp.tanh(0.5 * x) + 0.5


def _lane_tiles(x):
    return [x[:, i:i + LANE] for i in range(0, x.shape[1], LANE)]


def _store_token_major(ref, x, tmp_ref):
    rows, d = x.shape
    seg = d // LANE
    for s in range(seg):
        tmp_ref[pl.ds(s, rows, stride=seg), :] = x[:, s * LANE:(s + 1) * LANE]
    ref[...] = tmp_ref[...].astype(ref.dtype)


def _load_token_major(ref, rows, tmp_ref):
    seg = ref.shape[0] // rows
    tmp_ref[...] = ref[...].astype(F32)
    return jnp.concatenate([tmp_ref[pl.ds(s, rows, stride=seg), :] for s in range(seg)], axis=1)


def _rowmax(x):
    return jnp.max(functools.reduce(jnp.maximum, _lane_tiles(x)), axis=1, keepdims=True)


def _rowsum(x):
    return jnp.sum(functools.reduce(jnp.add, _lane_tiles(x)), axis=1, keepdims=True)


def _in_proj_kernel(x_ref, nw_ref, w_ref, o_ref, small_ref, hn_ref, *, small_off):
    j = pl.program_id(1)

    @pl.when(j == 0)
    def _():
        x = x_ref[...]
        ms = jnp.mean(x * x, axis=-1, keepdims=True)
        hn_ref[...] = (x * lax.rsqrt(ms + EPS) * nw_ref[...]).astype(BF16)
    acc = jnp.dot(hn_ref[...], w_ref[...], preferred_element_type=F32)
    o_ref[...] = acc.astype(o_ref.dtype)

    @pl.when(j == pl.num_programs(1) - 1)
    def _():
        small_ref[...] = acc[:, small_off:small_off + LANE]


def _in_proj(x2, norm_w, w):
    n, d = x2.shape
    ncol = w.shape[1]
    tm = min(1024, n)
    tn = 768
    small_off = CB_SMALL * LANE - (ncol // tn - 1) * tn
    assert 0 <= small_off <= tn - LANE
    return pl.pallas_call(
        functools.partial(_in_proj_kernel, small_off=small_off),
        grid=(n // tm, ncol // tn),
        in_specs=[pl.BlockSpec((tm, d), lambda i, j: (i, 0)),
                  pl.BlockSpec((1, d), lambda i, j: (0, 0)),
                  pl.BlockSpec((d, tn), lambda i, j: (0, j))],
        out_specs=[pl.BlockSpec((tm, tn), lambda i, j: (i, j)),
                   pl.BlockSpec((tm, LANE), lambda i, j: (i, 0))],
        out_shape=[jax.ShapeDtypeStruct((n, ncol), BF16), jax.ShapeDtypeStruct((n, LANE), F32)],
        scratch_shapes=[pltpu.VMEM((tm, d), BF16)],
        compiler_params=_cparams(("parallel", "arbitrary")),
        name="in_proj",
    )(x2, norm_w.reshape(1, d), w)


def _compress_kernel(a_ref, c_ref, sa_ref, sb_ref, w_ref, pe_ref, o_ref, xs_ref):
    kv = pl.program_id(1)
    t = a_ref.shape[1]
    nc = t // CMP_STRIDE
    x = a_ref[0].astype(F32)
    xr = _rope(x, c_ref[0], sa_ref[0], sb_ref[0])
    x = jnp.where(kv == 0, xr, x)
    xs_ref[pl.ds(0, t), :] = x
    xs_ref[pl.ds(t, CMP_STRIDE), :] = jnp.zeros((CMP_STRIDE, LANE), F32)
    acc = jnp.zeros((nc, LANE), F32)
    for l in range(CMP_LEN):
        rows = xs_ref[pl.ds(l, nc, stride=CMP_STRIDE), :] + pe_ref[0, pl.ds(l, 1), :]
        acc = acc + _mm(rows, w_ref[0, l])
    o_ref[0, 0, 0] = acc.astype(BF16)


def _nsa_compress(proj3, tabs, cmp_w, cmp_pe):
    b, t, _ = proj3.shape
    g = N_KV_NSA
    nc = t // CMP_STRIDE
    tab_spec = pl.BlockSpec((1, t, LANE), lambda bi, kv, gi: (bi, 0, 0))
    return pl.pallas_call(
        _compress_kernel,
        grid=(b, 2, g),
        in_specs=[pl.BlockSpec((1, t, LANE), lambda bi, kv, gi: (bi, 0, CB_KV + kv * g + gi)),
                  tab_spec, tab_spec, tab_spec,
                  pl.BlockSpec((1, CMP_LEN, LANE, LANE), lambda bi, kv, gi: (kv, 0, 0, 0)),
                  pl.BlockSpec((1, CMP_LEN, LANE), lambda bi, kv, gi: (kv, 0, 0))],
        out_specs=pl.BlockSpec((1, 1, 1, nc, LANE), lambda bi, kv, gi: (bi, kv, gi, 0, 0)),
        out_shape=jax.ShapeDtypeStruct((b, 2, g, nc, LANE), BF16),
        scratch_shapes=[pltpu.VMEM((t + CMP_STRIDE, LANE), F32)],
        compiler_params=_cparams(("parallel", "arbitrary", "arbitrary")),
        name="nsa_compress",
    )(proj3, *tabs, cmp_w, cmp_pe)


def _load_q(q_ref, c, sa, sb):
    scale = HEAD_DIM ** -0.5
    qs = [(_rope(q_ref[0, :, r * LANE:(r + 1) * LANE].astype(F32), c, sa, sb) * scale).astype(BF16)
          for r in range(N_REP)]
    return jnp.concatenate(qs, axis=0)


def _store_gated(o_ref, o, gate_ref, g, branch, tq):
    gt = gate_ref[0]
    for r in range(N_REP):
        col = ((g * N_REP + r) * 3 + branch)
        lane = lax.broadcasted_iota(jnp.int32, gt.shape, 1)
        gcol = jnp.sum(jnp.where(lane == col, gt, 0.0), axis=1, keepdims=True)
        o_ref[0, :, r * LANE:(r + 1) * LANE] = (o[r * tq:(r + 1) * tq] * _sigmoid(gcol)).astype(o_ref.dtype)


Q_TILES_PER_STEP = 4


def _row_views(refs, sub, tq):
    rows = pl.ds(sub * tq, tq)
    return [r.at[:, rows, :] if len(r.shape) == 3 else r.at[:, :, rows, :] for r in refs]


def _cmp_attn_kernel(q_ref, c_ref, sa_ref, sb_ref, kc_ref, vc_ref, gate_ref, o_ref, sel_ref, *, n_slc):
    tq = q_ref.shape[1] // Q_TILES_PER_STEP
    for sub in range(Q_TILES_PER_STEP):
        q, c, sa, sb, gate, o, sel = _row_views(
            (q_ref, c_ref, sa_ref, sb_ref, gate_ref, o_ref, sel_ref), sub, tq)
        _cmp_attn_tile(pl.program_id(2) * Q_TILES_PER_STEP + sub, q, c, sa, sb, kc_ref, vc_ref, gate, o,
                       sel, n_slc=n_slc)


def _cmp_attn_tile(i, q_ref, c_ref, sa_ref, sb_ref, kc_ref, vc_ref, gate_ref, o_ref, sel_ref, *, n_slc):
    g = pl.program_id(1)
    tq = q_ref.shape[1]
    nc = kc_ref.shape[3]
    q4 = _load_q(q_ref, c_ref[0], sa_ref[0], sb_ref[0])
    s = _mm_nt(q4, kc_ref[0, 0, 0])
    row = lax.broadcasted_iota(jnp.int32, s.shape, 0)
    n = lax.broadcasted_iota(jnp.int32, s.shape, 1)
    tpos = i * tq + (row & (tq - 1))
    mask = (n * CMP_STRIDE + (CMP_LEN - 1) <= tpos) & (n < nc - 1)
    sm = jnp.where(mask, s, -1e30)
    m = _rowmax(sm)
    p = jnp.where(mask, jnp.exp(sm - m), 0.0)
    l = _rowsum(p)
    p = p / jnp.maximum(l, 1e-30)
    o = _mm(p, vc_ref[0, 0, 0])
    _store_gated(o_ref, o, gate_ref, g, 0, tq)

    ps = p[0:tq]
    for r in range(1, N_REP):
        ps = ps + p[r * tq:(r + 1) * tq]
    cn = lax.broadcasted_iota(jnp.int32, (nc, LANE), 0)
    cj = lax.broadcasted_iota(jnp.int32, (nc, LANE), 1)
    ratio = SLC_LEN // CMP_STRIDE
    agg = ((cn >= ratio * cj - (CMP_LEN // CMP_STRIDE - 1)) & (cn < ratio * cj + ratio)
           & (cn < nc - 1) & (cj < n_slc))
    agg = jnp.where(agg, 1.0, 0.0).astype(BF16)
    ps_hi = ps.astype(BF16)
    ps_lo = (ps - ps_hi.astype(F32)).astype(BF16)
    imp = (jnp.dot(ps_hi, agg, preferred_element_type=F32)
           + jnp.dot(ps_lo, agg, preferred_element_type=F32))

    j = lax.broadcasted_iota(jnp.int32, (tq, LANE), 1)
    tt = i * tq + lax.broadcasted_iota(jnp.int32, (tq, LANE), 0)
    cur = tt // SLC_LEN
    valid = j <= cur
    forced = (j == 0) | (j == cur) | (j == cur - 1)
    vals = jnp.where(forced, 1e30, jnp.where(valid, imp, -1.0))
    vt = vals.T
    jb = lax.broadcasted_iota(jnp.int32, (n_slc, tq), 0)
    vb = vt[0:n_slc]
    cnt = jnp.zeros((n_slc, tq), F32)
    for jp in range(n_slc):
        cand = vt[jp:jp + 1, :]
        ge = jnp.where(cand >= vb, 1.0, 0.0)
        gt = jnp.where(cand > vb, 1.0, 0.0)
        cnt = cnt + jnp.where(jb > jp, ge, gt)
    keep = jnp.where(cnt < float(min(SLC_TOP, n_slc)), 0.0, NEG_BIAS)
    if n_slc < LANE:
        keep = jnp.concatenate([keep, jnp.full((LANE - n_slc, tq), NEG_BIAS, F32)], axis=0)
    sel_ref[0, 0] = jnp.where(valid, keep.T, NEG_BIAS).astype(BF16)


def _nsa_cmp_attn(proj3, small3, tabs, kvc, tq=128):
    b, t, _ = proj3.shape
    g = N_KV_NSA
    nc = t // CMP_STRIDE
    n_slc = t // SLC_LEN
    rw = N_REP * LANE
    bq = tq * Q_TILES_PER_STEP
    tab_spec = pl.BlockSpec((1, bq, LANE), lambda bi, gi, i: (bi, i, 0))
    return pl.pallas_call(
        functools.partial(_cmp_attn_kernel, n_slc=n_slc),
        grid=(b, g, t // bq),
        in_specs=[pl.BlockSpec((1, bq, rw), lambda bi, gi, i: (bi, i, gi)),
                  tab_spec, tab_spec, tab_spec,
                  pl.BlockSpec((1, 1, 1, nc, LANE), lambda bi, gi, i: (bi, 0, gi, 0, 0)),
                  pl.BlockSpec((1, 1, 1, nc, LANE), lambda bi, gi, i: (bi, 1, gi, 0, 0)),
                  pl.BlockSpec((1, bq, LANE), lambda bi, gi, i: (bi, i, 0))],
        out_specs=[pl.BlockSpec((1, bq, rw), lambda bi, gi, i: (bi, i, gi)),
                   pl.BlockSpec((1, 1, bq, LANE), lambda bi, gi, i: (bi, gi, i, 0))],
        out_shape=[jax.ShapeDtypeStruct((b, t, N_HEADS_NSA * LANE), BF16),
                   jax.ShapeDtypeStruct((b, g, t, LANE), BF16)],
        compiler_params=_cparams(("parallel", "parallel", "parallel")),
        name="nsa_cmp",
    )(proj3, *tabs, kvc, kvc, small3)


def _sel_attn_kernel(q_ref, c_ref, sa_ref, sb_ref, cf_ref, saf_ref, sbf_ref, k_ref, v_ref,
                     sel_ref, gate_ref, o_ref, *scratch, tk):
    tq = q_ref.shape[1] // Q_TILES_PER_STEP
    for sub in range(Q_TILES_PER_STEP):
        q, c, sa, sb, sel, gate, o = _row_views(
            (q_ref, c_ref, sa_ref, sb_ref, sel_ref, gate_ref, o_ref), sub, tq)
        _sel_attn_tile(pl.program_id(2) * Q_TILES_PER_STEP + sub, q, c, sa, sb, cf_ref, saf_ref, sbf_ref,
                       k_ref, v_ref, sel, gate, o, *scratch, tk=tk)


def _sel_attn_tile(i, q_ref, c_ref, sa_ref, sb_ref, cf_ref, saf_ref, sbf_ref, k_ref, v_ref,
                   sel_ref, gate_ref, o_ref, ka_ref, vs_ref, m_ref, l_ref, acc_ref, s0_ref, s1_ref,
                   *, tk):
    g = pl.program_id(1)
    tq = q_ref.shape[1]
    t = k_ref.shape[1]
    rows = N_REP * tq

    @pl.when(i == 0)
    def _():
        kr = _rope(k_ref[0].astype(F32), cf_ref[0], saf_ref[0], sbf_ref[0])
        ka_ref[:, 0:LANE] = kr.astype(BF16)
        pos = lax.broadcasted_iota(jnp.int32, (t, LANE), 0)
        lane = lax.broadcasted_iota(jnp.int32, (t, LANE), 1)
        ka_ref[:, LANE:2 * LANE] = jnp.where(lane == pos // SLC_LEN, 1.0, 0.0).astype(BF16)
        vs_ref[...] = v_ref[0].astype(BF16)

    q4 = _load_q(q_ref, c_ref[0], sa_ref[0], sb_ref[0])
    bias = sel_ref[0, 0]
    qa = jnp.concatenate([q4, jnp.concatenate([bias] * N_REP, axis=0)], axis=1)

    m_ref[...] = jnp.full((rows, LANE), -1e30, F32)
    l_ref[...] = jnp.zeros((rows, LANE), F32)
    acc_ref[...] = jnp.zeros((rows, LANE), F32)

    half = tk // 2

    def scores(kt, hlf):
        k0 = pl.multiple_of(kt * tk + hlf * half, half)
        return _mm_nt(qa, ka_ref[pl.ds(k0, half), :])

    def update(s_ref, kt, hlf, causal):
        k0 = pl.multiple_of(kt * tk + hlf * half, half)
        s = s_ref[...]
        if causal:
            r = lax.broadcasted_iota(jnp.int32, s.shape, 0)
            kp = k0 + lax.broadcasted_iota(jnp.int32, s.shape, 1)
            s = jnp.where(kp <= i * tq + (r & (tq - 1)), s, -1e30)
        tiles = _lane_tiles(s)
        m_old = m_ref[...]
        m_new = jnp.maximum(m_old, _rowmax(s))
        alpha = jnp.exp(m_old - m_new)
        ps = [jnp.exp(tl - m_new) for tl in tiles]
        l_ref[...] = alpha * l_ref[...] + jnp.sum(functools.reduce(jnp.add, ps), axis=1, keepdims=True)
        p = jnp.concatenate([x.astype(BF16) for x in ps], axis=1)
        acc_ref[...] = alpha * acc_ref[...] + jnp.dot(p, vs_ref[pl.ds(k0, half), :],
                                                      preferred_element_type=F32)
        m_ref[...] = m_new

    diag = (i * tq) // tk
    s0_ref[...] = scores(0, 0)

    def body(kt, carry):
        s1_ref[...] = scores(kt, 1)
        update(s0_ref, kt, 0, False)
        s0_ref[...] = scores(kt + 1, 0)
        update(s1_ref, kt, 1, False)
        return carry

    def body2(j, carry):
        body(2 * j, carry)
        body(2 * j + 1, carry)
        return carry

    lax.fori_loop(0, diag // 2, body2, 0)
    lax.fori_loop((diag // 2) * 2, diag, body, 0)
    s1_ref[...] = scores(diag, 1)
    update(s0_ref, diag, 0, True)
    update(s1_ref, diag, 1, True)
    o = acc_ref[...] / l_ref[...]
    _store_gated(o_ref, o, gate_ref, g, 1, tq)


def _nsa_sel_attn(proj3, small3, tabs, selbias, tq=128, tk=512):
    b, t, _ = proj3.shape
    g = N_KV_NSA
    rw = N_REP * LANE
    bq = tq * Q_TILES_PER_STEP
    tk = min(tk, t)
    tab_spec = pl.BlockSpec((1, bq, LANE), lambda bi, gi, i: (bi, i, 0))
    tabf_spec = pl.BlockSpec((1, t, LANE), lambda bi, gi, i: (bi, 0, 0))
    kcb = CB_KV + 1 * 2 * g
    return pl.pallas_call(
        functools.partial(_sel_attn_kernel, tk=tk),
        grid=(b, g, t // bq),
        in_specs=[pl.BlockSpec((1, bq, rw), lambda bi, gi, i: (bi, i, gi)),
                  tab_spec, tab_spec, tab_spec, tabf_spec, tabf_spec, tabf_spec,
                  pl.BlockSpec((1, t, LANE), lambda bi, gi, i: (bi, 0, kcb + gi)),
                  pl.BlockSpec((1, t, LANE), lambda bi, gi, i: (bi, 0, kcb + g + gi)),
                  pl.BlockSpec((1, 1, bq, LANE), lambda bi, gi, i: (bi, gi, i, 0)),
                  pl.BlockSpec((1, bq, LANE), lambda bi, gi, i: (bi, i, 0))],
        out_specs=pl.BlockSpec((1, bq, rw), lambda bi, gi, i: (bi, i, gi)),
        out_shape=jax.ShapeDtypeStruct((b, t, N_HEADS_NSA * LANE), BF16),
        scratch_shapes=[pltpu.VMEM((t, 2 * LANE), BF16), pltpu.VMEM((t, LANE), BF16),
                        pltpu.VMEM((N_REP * tq, LANE), F32), pltpu.VMEM((N_REP * tq, LANE), F32),
                        pltpu.VMEM((N_REP * tq, LANE), F32),
                        pltpu.VMEM((N_REP * tq, tk // 2), F32), pltpu.VMEM((N_REP * tq, tk // 2), F32)],
        compiler_params=_cparams(("parallel", "parallel", "arbitrary")),
        name="nsa_sel",
    )(proj3, *tabs, *tabs, proj3, proj3, selbias, small3)


def _win_attn_kernel(q_ref, c_ref, sa_ref, sb_ref, cf_ref, saf_ref, sbf_ref, k_ref, v_ref,
                     gate_ref, o_ref, ks_ref, vs_ref, *, span):
    tq = q_ref.shape[1] // Q_TILES_PER_STEP
    for sub in range(Q_TILES_PER_STEP):
        q, c, sa, sb, gate, o = _row_views((q_ref, c_ref, sa_ref, sb_ref, gate_ref, o_ref), sub, tq)
        _win_attn_tile(pl.program_id(2) * Q_TILES_PER_STEP + sub, q, c, sa, sb, cf_ref, saf_ref, sbf_ref,
                       k_ref, v_ref, gate, o, ks_ref, vs_ref, span=span)


def _win_attn_tile(i, q_ref, c_ref, sa_ref, sb_ref, cf_ref, saf_ref, sbf_ref, k_ref, v_ref,
                   gate_ref, o_ref, ks_ref, vs_ref, *, span):
    g = pl.program_id(1)
    tq = q_ref.shape[1]

    t = k_ref.shape[1]

    @pl.when(i == 0)
    def _():
        zeros = jnp.zeros((WINDOW, LANE), BF16)
        ks_ref[pl.ds(0, WINDOW), :] = zeros
        vs_ref[pl.ds(0, WINDOW), :] = zeros
        ks_ref[pl.ds(WINDOW, t), :] = _rope(k_ref[0].astype(F32), cf_ref[0], saf_ref[0],
                                            sbf_ref[0]).astype(BF16)
        vs_ref[pl.ds(WINDOW, t), :] = v_ref[0].astype(BF16)

    q4 = _load_q(q_ref, c_ref[0], sa_ref[0], sb_ref[0])
    k0 = pl.multiple_of(i * tq, tq)
    s = _mm_nt(q4, ks_ref[pl.ds(k0, span), :])
    tiles = _lane_tiles(s)
    r = lax.broadcasted_iota(jnp.int32, tiles[0].shape, 0) & (tq - 1)
    c = lax.broadcasted_iota(jnp.int32, tiles[0].shape, 1)
    first_block = WINDOW // tq - i
    masked = []
    for b, tl in enumerate(tiles):
        if b == 0:
            tl = jnp.where(c > r, tl, -1e30)
        if b == len(tiles) - 1:
            tl = jnp.where(c <= r, tl, -1e30)
        else:
            tl = jnp.where(b >= first_block, tl, -1e30)
        masked.append(tl)
    m = jnp.max(functools.reduce(jnp.maximum, masked), axis=1, keepdims=True)
    ps = [jnp.exp(tl - m) for tl in masked]
    l = jnp.sum(functools.reduce(jnp.add, ps), axis=1, keepdims=True)
    p = jnp.concatenate([x.astype(BF16) for x in ps], axis=1)
    o = jnp.dot(p, vs_ref[pl.ds(k0, span), :], preferred_element_type=F32) / l
    _store_gated(o_ref, o, gate_ref, g, 2, tq)


def _nsa_win_attn(proj3, small3, tabs, tq=128):
    b, t, _ = proj3.shape
    g = N_KV_NSA
    rw = N_REP * LANE
    bq = tq * Q_TILES_PER_STEP
    span = WINDOW + tq
    assert tq == LANE
    tab_spec = pl.BlockSpec((1, bq, LANE), lambda bi, gi, i: (bi, i, 0))
    tabf_spec = pl.BlockSpec((1, t, LANE), lambda bi, gi, i: (bi, 0, 0))
    kcb = CB_KV + 2 * 2 * g
    return pl.pallas_call(
        functools.partial(_win_attn_kernel, span=span),
        grid=(b, g, t // bq),
        in_specs=[pl.BlockSpec((1, bq, rw), lambda bi, gi, i: (bi, i, gi)),
                  tab_spec, tab_spec, tab_spec, tabf_spec, tabf_spec, tabf_spec,
                  pl.BlockSpec((1, t, LANE), lambda bi, gi, i: (bi, 0, kcb + gi)),
                  pl.BlockSpec((1, t, LANE), lambda bi, gi, i: (bi, 0, kcb + g + gi)),
                  pl.BlockSpec((1, bq, LANE), lambda bi, gi, i: (bi, i, 0))],
        out_specs=pl.BlockSpec((1, bq, rw), lambda bi, gi, i: (bi, i, gi)),
        out_shape=jax.ShapeDtypeStruct((b, t, N_HEADS_NSA * LANE), BF16),
        scratch_shapes=[pltpu.VMEM((t + WINDOW, LANE), BF16), pltpu.VMEM((t + WINDOW, LANE), BF16)],
        compiler_params=_cparams(("parallel", "parallel", "arbitrary")),
        name="nsa_win",
    )(proj3, *tabs, *tabs, proj3, proj3, small3)


def _gdn_kernel(q_ref, k_ref, v_ref, z_ref, braw_ref, araw_ref, cwq_ref, cwk_ref, cwv_ref, alog_ref,
                dtb_ref, nw_ref, o_ref, q_s, k_s, v_s, o_s, qp_s, op_s, xp_s, bd_s, gcd_s, gl_s, mm_s,
                nn_s):
    t = q_ref.shape[1]
    c = GDN_CHUNK
    nchunk = t // c
    xp_s[pl.ds(0, 8), :] = jnp.zeros((8, LANE), F32)

    def conv_silu(x, cw):
        xp_s[pl.ds(8, t), :] = x
        y = x * cw[CONV_WIDTH - 1:CONV_WIDTH]
        for sft in range(1, CONV_WIDTH):
            y = y + xp_s[pl.ds(8 - sft, t), :] * cw[CONV_WIDTH - 1 - sft:CONV_WIDTH - sft]
        return y * _sigmoid(y)

    def l2n(x):
        return x * lax.rsqrt(jnp.sum(x * x, axis=1, keepdims=True) + EPS)

    q_s[...] = l2n(conv_silu(q_ref[0].astype(F32), cwq_ref[...])) * (HEAD_DIM ** -0.5)
    k_s[...] = l2n(conv_silu(k_ref[0].astype(F32), cwk_ref[...]))
    v_s[...] = conv_silu(v_ref[0].astype(F32), cwv_ref[...])

    bd_s[...] = _sigmoid(braw_ref[0, 0])
    xa = araw_ref[0, 0] + dtb_ref[0]
    softplus = jnp.maximum(xa, 0.0) + jnp.log(1.0 + jnp.exp(-jnp.abs(xa)))
    gc = -jnp.exp(alog_ref[0]) * softplus
    lane = lax.broadcasted_iota(jnp.int32, gc.shape, 1)
    sft = 1
    while sft < c:
        gc = gc + jnp.where((lane & (c - 1)) >= sft, pltpu.roll(gc, sft, 1), 0.0)
        sft *= 2
    gcd_s[...] = gc

    c2 = 2 * c
    ci = lax.broadcasted_iota(jnp.int32, (c2, c2), 0)
    cj = lax.broadcasted_iota(jnp.int32, (c2, c2), 1)
    same = (ci // c) == (cj // c)
    tril = same & (cj <= ci)
    strict = same & (cj < ci)
    eye = jnp.where(ci == cj, 1.0, 0.0)
    first = ci < c

    pairs_per_iter = 8
    rng = range(pairs_per_iter)

    def prep(it, carry):
        n2 = [it * pairs_per_iter + p for p in rng]
        sl = [pl.ds(pl.multiple_of(n * c2, c2), c2) for n in n2]
        grow = [gcd_s[pl.ds(n, 1), :] for n in n2]
        gc2 = [jnp.broadcast_to(g_, (c2, c2)).T for g_ in grow]
        betac = [jnp.broadcast_to(bd_s[pl.ds(n, 1), :], (c2, c2)).T for n in n2]
        kn = [k_s[s_, :] for s_ in sl]
        kbn = [kn[p] * betac[p] for p in rng]
        decay = [jnp.exp(jnp.where(tril, gc2[p] - grow[p], -1e30)) for p in rng]
        kk = [_mm_nt(kbn[p], kn[p]) for p in rng]
        y = [-jnp.where(strict, kk[p] * decay[p], 0.0) for p in rng]
        pm = [eye + y_ for y_ in y]
        pw = 1
        while 2 * pw < c:
            y = [_mm(y_, y_) for y_ in y]
            pm = [pm[p] + _mm(pm[p], y[p]) for p in rng]
            pw *= 2
        egn = [jnp.exp(g_) for g_ in gc2]
        uw = [_mm(pm[p], jnp.concatenate([v_s[sl[p], :] * betac[p], kbn[p] * egn[p]], axis=1))
              for p in rng]
        qn = [q_s[s_, :] for s_ in sl]
        qk = [_mm_nt(qn[p], kn[p]) for p in rng]
        attn = [jnp.where(tril, qk[p] * decay[p], 0.0) for p in rng]
        auw = [_mm(attn[p], uw[p]) for p in rng]
        for p in rng:
            qp_s[sl[p], :] = qn[p] * egn[p] - auw[p][:, LANE:]
            op_s[sl[p], :] = auw[p][:, :LANE]
        gl = [jnp.where(first, g_[c - 1:c, :], g_[c2 - 1:c2, :]) for g_ in gc2]
        kdt = [(kn[p] * jnp.exp(gl[p] - gc2[p])).T for p in rng]
        nma = [_mm(kdt[p][:, 0:c], uw[p][0:c]) for p in rng]
        nmb = [_mm(kdt[p][:, c:c2], uw[p][c:c2]) for p in rng]
        mb = [x[:, LANE:] for x in nmb]
        mbnm = [_mm(mb[p], nma[p]) for p in rng]
        ma = [x[:, LANE:] for x in nma]
        na = [x[:, :LANE] for x in nma]
        nb_ = [x[:, :LANE] for x in nmb]
        mba = [x[:, LANE:] for x in mbnm]
        mbn = [x[:, :LANE] for x in mbnm]
        for p in rng:
            ga = jnp.exp(gc2[p][c - 1:c, :])
            gb = jnp.exp(gc2[p][c2 - 1:c2, :])
            mm_s[2 * n2[p]] = ma[p].astype(BF16)
            nn_s[2 * n2[p]] = na[p]
            gl_s[pl.ds(2 * n2[p], 1), :] = ga
            mm_s[2 * n2[p] + 1] = (gb * ma[p] + ga * mb[p] - mba[p]).astype(BF16)
            nn_s[2 * n2[p] + 1] = gb * na[p] - mbn[p] + nb_[p]
            gl_s[pl.ds(2 * n2[p] + 1, 1), :] = ga * gb
        return carry

    lax.fori_loop(0, nchunk // (2 * pairs_per_iter), prep, 0)

    def second_out(n2, s_mid):
        rows = pl.ds(pl.multiple_of(n2 * c2, c2) + c, c)
        o_s[rows, :] = (jnp.dot(qp_s[rows, :].astype(BF16), s_mid.astype(BF16),
                                 preferred_element_type=F32) + op_s[rows, :])

    def scan(n2, carry):
        s, s_mid_prev = carry
        second_out(jnp.maximum(n2 - 1, 0), s_mid_prev)
        ra = pl.ds(pl.multiple_of(n2 * c2, c2), c)
        sb = s.astype(BF16)
        s_mid = (s * gl_s[pl.ds(2 * n2, 1), :] - jnp.dot(mm_s[2 * n2], sb, preferred_element_type=F32)
                 + nn_s[2 * n2])
        s_new = (s * gl_s[pl.ds(2 * n2 + 1, 1), :]
                 - jnp.dot(mm_s[2 * n2 + 1], sb, preferred_element_type=F32) + nn_s[2 * n2 + 1])
        o_s[ra, :] = jnp.dot(qp_s[ra, :].astype(BF16), sb, preferred_element_type=F32) + op_s[ra, :]
        return s_new, s_mid

    zero_state = jnp.zeros((LANE, LANE), F32)
    _, s_mid_last = lax.fori_loop(0, nchunk // 2, scan, (zero_state, zero_state))
    second_out(jnp.int32(nchunk // 2 - 1), s_mid_last)

    o = o_s[...]
    on = o * lax.rsqrt(jnp.mean(o * o, axis=1, keepdims=True) + EPS) * nw_ref[...]
    z = z_ref[0].astype(F32)
    o_ref[0] = (on * (z * _sigmoid(z))).astype(o_ref.dtype)


def _gdn(proj3, small3, conv_w, a_log, dt_bias, norm_w):
    b, t, _ = proj3.shape
    hh = N_HEADS_GDN
    c = GDN_CHUNK
    col = lambda cb: pl.BlockSpec((1, t, LANE), lambda bi, hi: (bi, 0, cb + hi))
    cw = lambda off: pl.BlockSpec((CONV_WIDTH, LANE), lambda bi, hi: (0, off + hi))
    hrow = pl.BlockSpec((1, 1, LANE), lambda bi, hi: (hi, 0, 0))
    alog_b = jnp.broadcast_to(a_log.astype(F32)[:, None, None], (hh, 1, LANE))
    dtb_b = jnp.broadcast_to(dt_bias.astype(F32)[:, None, None], (hh, 1, LANE))
    big = pltpu.VMEM((t, LANE), F32)
    nrow = t // LANE
    ba = small3[:, :, SMALL_BETA:SMALL_BETA + 2 * hh].transpose(0, 2, 1).reshape(b, 2 * hh, nrow, LANE)
    dense = lambda off: pl.BlockSpec((1, 1, nrow, LANE), lambda bi, hi: (bi, off + hi, 0, 0))
    return pl.pallas_call(
        _gdn_kernel,
        grid=(b, hh),
        in_specs=[col(CB_GQKV), col(CB_GQKV + hh), col(CB_GQKV + 2 * hh), col(CB_Z),
                  dense(0), dense(hh),
                  cw(0), cw(hh), cw(2 * hh), hrow, hrow,
                  pl.BlockSpec((1, LANE), lambda bi, hi: (0, 0))],
        out_specs=pl.BlockSpec((1, t, LANE), lambda bi, hi: (bi, 0, hi)),
        out_shape=jax.ShapeDtypeStruct((b, t, hh * LANE), BF16),
        scratch_shapes=[big, big, big, big, big, big, pltpu.VMEM((t + 8, LANE), F32),
                        pltpu.VMEM((nrow, LANE), F32), pltpu.VMEM((nrow, LANE), F32),
                        pltpu.VMEM((t // c, LANE), F32),
                        pltpu.VMEM((t // c, LANE, LANE), BF16),
                        pltpu.VMEM((t // c, LANE, LANE), F32)],
        compiler_params=_cparams(("parallel", "parallel")),
        name="gdn",
    )(proj3, proj3, proj3, proj3, ba, ba, conv_w, conv_w, conv_w, alog_b, dtb_b,
      norm_w.reshape(1, LANE))


def _split3(a):
    hi = a.astype(BF16)
    lo = (a - hi.astype(F32)).astype(BF16)
    return hi, lo


def _out_proj_kernel(oc_ref, os_ref, ow_ref, ob_ref, x_ref, wo_ref, fw_ref, wr_ref, br_ref,
                     h_ref, hn_ref, ids_ref, wts_ref, tmp_ref):
    half = oc_ref.shape[1]
    oa = (oc_ref[...].astype(F32) + os_ref[...].astype(F32) + ow_ref[...].astype(F32)).astype(BF16)
    h1 = (x_ref[...] + jnp.dot(oa, wo_ref[0:half, :], preferred_element_type=F32)
          + jnp.dot(ob_ref[...], wo_ref[half:2 * half, :], preferred_element_type=F32))
    h_ref[...] = h1
    hn = h1 * lax.rsqrt(jnp.mean(h1 * h1, axis=1, keepdims=True) + EPS) * fw_ref[...]
    _store_token_major(hn_ref, hn, tmp_ref)

    a_hi, a_lo = _split3(hn)
    w_hi, w_lo = _split3(wr_ref[...])
    dot = lambda a, b: jnp.dot(a, b, preferred_element_type=F32)
    logits = dot(a_hi, w_hi) + dot(a_hi, w_lo) + dot(a_lo, w_hi) + br_ref[...]
    lane = lax.broadcasted_iota(jnp.int32, logits.shape, 1)
    big = 1e30
    is_g = lane < N_GROUPS
    lg = jnp.where(is_g, logits, -big)
    gm = jnp.max(lg, axis=1, keepdims=True)
    grp = jnp.min(jnp.where(lg == gm, lane, LANE), axis=1, keepdims=True)
    p_grp = 1.0 / jnp.sum(jnp.where(is_g, jnp.exp(lg - gm), 0.0), axis=1, keepdims=True)
    e_id = lane - N_GROUPS
    in_g = (e_id >= 0) & (e_id < N_EXPERTS) & ((e_id // EXPERTS_PER_GROUP) == grp)
    le = jnp.where(in_g, logits, -big)
    em = jnp.max(le, axis=1, keepdims=True)
    pe = jnp.where(in_g, jnp.exp(le - em), 0.0)
    pe = pe / jnp.sum(pe, axis=1, keepdims=True)
    pm = jnp.where(in_g, pe, -1.0)
    p1 = jnp.max(pm, axis=1, keepdims=True)
    i1 = jnp.min(jnp.where(pm == p1, lane, LANE), axis=1, keepdims=True)
    pm2 = jnp.where(lane == i1, -1.0, pm)
    p2 = jnp.max(pm2, axis=1, keepdims=True)
    i2 = jnp.min(jnp.where(pm2 == p2, lane, LANE), axis=1, keepdims=True)
    den = p1 + p2
    ids_ref[...] = jnp.where(lane == 0, i1 - N_GROUPS, jnp.where(lane == 1, i2 - N_GROUPS, 0))
    wts_ref[...] = jnp.where(lane == 0, p1 / den * p_grp, jnp.where(lane == 1, p2 / den * p_grp, 0.0))


def _out_proj(oc, os_, ow, ob, x2, w_out, ffn_w, wr, br):
    n, d = x2.shape
    half = oc.shape[1]
    seg = d // LANE
    tm = min(256, n)
    row = lambda w: pl.BlockSpec((tm, w), lambda i: (i, 0))
    full = lambda a: pl.BlockSpec(a.shape, lambda i: (0,) * a.ndim)
    fw = ffn_w.reshape(1, d)
    return pl.pallas_call(
        _out_proj_kernel,
        grid=(n // tm,),
        in_specs=[row(half), row(half), row(half), row(half), row(d), full(w_out), full(fw),
                  full(wr), full(br)],
        out_specs=[row(d), pl.BlockSpec((tm * seg, LANE), lambda i: (i, 0)), row(LANE), row(LANE)],
        out_shape=[jax.ShapeDtypeStruct((n, d), F32), jax.ShapeDtypeStruct((n * seg, LANE), BF16),
                   jax.ShapeDtypeStruct((n, LANE), jnp.int32), jax.ShapeDtypeStruct((n, LANE), F32)],
        scratch_shapes=[pltpu.VMEM((tm * seg, LANE), F32)],
        compiler_params=_cparams(("parallel",)),
        name="out_proj",
    )(oc, os_, ow, ob, x2, w_out, fw, wr, br)


def _moe_kernel(bexp_ref, bact_ref, bfirst_ref, bord_ref, bnext_ref, rtok_ref, rtokn_ref, rslotp_ref,
                rslot_ref, hn_hbm, wg_hbm, wu_hbm, wd_hbm, y_hbm, xbuf, ybuf, tmp, wg_st, wu_st, wd_st,
                wgb, wub, wdb, gsem, ssem, wsem, *, n_real):
    i = pl.program_id(0)
    nb = pl.num_programs(0)
    rb = rtok_ref.shape[2]
    seg = ybuf.shape[1] // rb
    slot = i & 1
    other = 1 - slot
    active = bact_ref[i] > 0
    prev_active = jnp.logical_and(i > 0, bact_ref[jnp.maximum(i - 1, 0)] > 0)

    def gather_copy(tok, r, buf):
        return pltpu.make_async_copy(hn_hbm.at[pl.ds(pl.multiple_of(tok, seg), seg), :],
                                     xbuf.at[buf, pl.ds(r * seg, seg), :], gsem.at[buf])

    def scatter_copy(dst, r, buf):
        return pltpu.make_async_copy(ybuf.at[buf, pl.ds(r * seg, seg), :],
                                     y_hbm.at[pl.ds(pl.multiple_of(dst, seg), seg), :], ssem.at[buf])

    def wait_gather(buf):
        pltpu.make_async_copy(hn_hbm.at[pl.ds(0, rb * seg), :], xbuf.at[buf], gsem.at[buf]).wait()

    def wait_scatter(buf):
        pltpu.make_async_copy(ybuf.at[buf], y_hbm.at[pl.ds(0, rb * seg), :], ssem.at[buf]).wait()

    def weight_copies(e, ws):
        return (pltpu.make_async_copy(wg_hbm.at[e], wg_st.at[ws], wsem.at[ws]),
                pltpu.make_async_copy(wu_hbm.at[e], wu_st.at[ws], wsem.at[ws]),
                pltpu.make_async_copy(wd_hbm.at[e], wd_st.at[ws], wsem.at[ws]))

    @pl.when(i == 0)
    def _():
        for cp in weight_copies(bexp_ref[0], 0):
            cp.start(priority=1)
        ybuf[...] = jnp.zeros(ybuf.shape, ybuf.dtype)
        pltpu.make_async_copy(ybuf.at[0], y_hbm.at[pl.ds(n_real * seg, rb * seg), :],
                              ssem.at[0]).start()

        def first(r, carry):
            gather_copy(rtok_ref[0, 0, r], r, 0).start()
            return carry
        lax.fori_loop(0, rb, first, 0)

    @pl.when(jnp.logical_or(i == 0, prev_active))
    def _():
        wait_gather(slot)

    @pl.when(active)
    def _():
        @pl.when(bfirst_ref[i] > 0)
        def _():
            ws = bord_ref[i] & 1
            for cp in weight_copies(bexp_ref[i], ws):
                cp.wait()

            @pl.when(bnext_ref[i] >= 0)
            def _():
                for cp in weight_copies(bnext_ref[i], 1 - ws):
                    cp.start(priority=1)

            wgb[...] = wg_st[ws].astype(BF16)
            wub[...] = wu_st[ws].astype(BF16)
            wdb[...] = wd_st[ws].astype(BF16)

        n_piece = 8
        per = rb // n_piece

        def issue(piece):
            half_n = n_piece // 2
            if piece < half_n:
                for r in range(piece * 2 * per, (piece + 1) * 2 * per):
                    gather_copy(rtokn_ref[0, 0, r], r, other).start()
            else:
                for r in range((piece - half_n) * 2 * per, (piece - half_n + 1) * 2 * per):
                    scatter_copy(rslotp_ref[0, 0, r], r, other).start()

        x = _load_token_major(xbuf.at[slot], rb, tmp).astype(BF16)
        de = wgb.shape[1]
        d = wdb.shape[1]
        hc = de // 2
        acts = []
        for c in range(2):
            issue(2 * c)
            hg = jnp.dot(x, wgb[:, c * hc:(c + 1) * hc], preferred_element_type=F32)
            issue(2 * c + 1)
            hu = jnp.dot(x, wub[:, c * hc:(c + 1) * hc], preferred_element_type=F32)
            acts.append((hg * _sigmoid(hg) * hu).astype(BF16))
        act = jnp.concatenate(acts, axis=1)
        dc = d // 4
        ys = []
        for j in range(4):
            issue(4 + j)
            ys.append(jnp.dot(act, wdb[:, j * dc:(j + 1) * dc], preferred_element_type=F32))
        wait_scatter(slot)
        _store_token_major(ybuf.at[slot], jnp.concatenate(ys, axis=1), tmp)

    def scatter_all(idx_ref, buf):
        def body(r, carry):
            scatter_copy(idx_ref[0, 0, r], r, buf).start()
            return carry
        lax.fori_loop(0, rb, body, 0)

    @pl.when(jnp.logical_and(jnp.logical_not(active), prev_active))
    def _():
        wait_scatter(slot)
        scatter_all(rslotp_ref, other)
        wait_scatter(other)

    @pl.when(jnp.logical_and(i == nb - 1, active))
    def _():
        wait_scatter(other)
        scatter_all(rslot_ref, slot)
        wait_scatter(slot)
        wait_gather(other)


def _moe(hn, bexp, bact, rtok, rslot, w_gate, w_up, w_down, n_real):
    d, de = w_gate.shape[1], w_gate.shape[2]
    seg = d // LANE
    nb = bexp.shape[0]
    rb = MOE_ROW_BLOCK
    idx = jnp.arange(nb, dtype=jnp.int32)
    prev_e = jnp.concatenate([jnp.full((1,), -1, jnp.int32), bexp[:-1]])
    bfirst = jnp.logical_and(bact > 0, jnp.logical_or(idx == 0, bexp != prev_e)).astype(jnp.int32)
    bord = jnp.cumsum(bfirst) - 1
    first_at = jnp.where(bfirst > 0, idx, nb)
    next_first = jnp.concatenate([lax.cummin(first_at[::-1])[::-1][1:], jnp.full((1,), nb, jnp.int32)])
    bnext = jnp.where(next_first < nb, bexp[jnp.minimum(next_first, nb - 1)], -1).astype(jnp.int32)

    smem_rows = pl.BlockSpec((1, 1, rb), lambda i, *_: (i, 0, 0), memory_space=pltpu.SMEM)
    smem_next = pl.BlockSpec((1, 1, rb), lambda i, *_: (jnp.minimum(i + 1, nb - 1), 0, 0),
                             memory_space=pltpu.SMEM)
    smem_cur = pl.BlockSpec((1, 1, rb), lambda i, *_: (i + 1, 0, 0), memory_space=pltpu.SMEM)
    hbm = pl.BlockSpec(memory_space=pl.ANY)
    grid_spec = pltpu.PrefetchScalarGridSpec(
        num_scalar_prefetch=5,
        grid=(nb,),
        in_specs=[smem_rows, smem_next, smem_rows, smem_cur, hbm, hbm, hbm, hbm],
        out_specs=hbm,
        scratch_shapes=[pltpu.VMEM((2, rb * seg, LANE), BF16), pltpu.VMEM((2, rb * seg, LANE), BF16),
                        pltpu.VMEM((rb * seg, LANE), F32),
                        pltpu.VMEM((2, d, de), F32), pltpu.VMEM((2, d, de), F32), pltpu.VMEM((2, de, d), F32),
                        pltpu.VMEM((d, de), BF16), pltpu.VMEM((d, de), BF16), pltpu.VMEM((de, d), BF16),
                        pltpu.SemaphoreType.DMA((2,)), pltpu.SemaphoreType.DMA((2,)),
                        pltpu.SemaphoreType.DMA((2,))])
    return pl.pallas_call(
        functools.partial(_moe_kernel, n_real=n_real),
        grid_spec=grid_spec,
        out_shape=jax.ShapeDtypeStruct(((n_real + 2 * rb) * seg, LANE), BF16),
        compiler_params=_cparams(("arbitrary",)),
        name="moe",
    )(bexp, bact, bfirst, bord, bnext, rtok, rtok, rslot, rslot, hn, w_gate, w_up, w_down)


def _dispatch(ids, n, seg):
    k = 2
    m = n * k
    rb = MOE_ROW_BLOCK
    e_flat = ids[:, :k].reshape(m)
    onehot = (e_flat[:, None] == jnp.arange(N_EXPERTS, dtype=jnp.int32)[None, :]).astype(jnp.int32)
    csum = jnp.cumsum(onehot, axis=0)
    counts = csum[-1]
    rank = jnp.take_along_axis(csum, e_flat[:, None], axis=1)[:, 0] - 1
    padded = (counts + rb - 1) // rb * rb
    pad_end = jnp.cumsum(padded)
    pad_start = pad_end - padded
    dest = pad_start[e_flat] + rank
    nb = (m + N_EXPERTS * (rb - 1) + rb - 1) // rb
    p = nb * rb
    row_m = jnp.full((p,), -1, jnp.int32).at[dest].set(jnp.arange(m, dtype=jnp.int32))
    real = row_m >= 0
    rtok = jnp.where(real, row_m // k, 0)
    pidx = jnp.arange(p, dtype=jnp.int32)
    rslot = jnp.where(real, (row_m % k) * n + row_m // k, m + ((pidx // rb) % 2) * rb + pidx % rb)
    starts = jnp.arange(nb, dtype=jnp.int32) * rb
    bexp = jnp.minimum(jnp.sum((pad_end[None, :] <= starts[:, None]).astype(jnp.int32), axis=1),
                       N_EXPERTS - 1)
    bact = jnp.sum(real.reshape(nb, rb).astype(jnp.int32), axis=1)
    last_e = jnp.max(jnp.where(bact > 0, bexp, 0))
    bexp = jnp.where(bact > 0, bexp, last_e)
    rslot = jnp.concatenate([m + rb + jnp.arange(rb, dtype=jnp.int32), rslot])
    return (bexp, bact, (rtok * seg).reshape(nb, 1, rb), (rslot * seg).reshape(nb + 1, 1, rb),
            m)


def _combine_kernel(h_ref, y0_ref, y1_ref, wts_ref, fw_ref, o_ref, tmp_ref):
    tm = h_ref.shape[0]
    wts = wts_ref[...]
    y0 = _load_token_major(y0_ref, tm, tmp_ref)
    y1 = _load_token_major(y1_ref, tm, tmp_ref)
    moe = y0 * wts[:, 0:1] + y1 * wts[:, 1:2]
    h = h_ref[...] + moe
    o_ref[...] = h * lax.rsqrt(jnp.mean(h * h, axis=1, keepdims=True) + EPS) * fw_ref[...]


def _combine(h1, y, wts, final_w):
    n, d = h1.shape
    seg = d // LANE
    tm = min(512, n)
    nt = n // tm
    return pl.pallas_call(
        _combine_kernel,
        grid=(nt,),
        in_specs=[pl.BlockSpec((tm, d), lambda i: (i, 0)),
                  pl.BlockSpec((tm * seg, LANE), lambda i: (i, 0)),
                  pl.BlockSpec((tm * seg, LANE), lambda i: (i + nt, 0)),
                  pl.BlockSpec((tm, LANE), lambda i: (i, 0)),
                  pl.BlockSpec((1, d), lambda i: (0, 0))],
        out_specs=pl.BlockSpec((tm, d), lambda i: (i, 0)),
        out_shape=jax.ShapeDtypeStruct((n, d), F32),
        scratch_shapes=[pltpu.VMEM((tm * seg, LANE), F32)],
        compiler_params=_cparams(("parallel",)),
        name="combine",
    )(h1, y, y, wts, final_w.reshape(1, d))


def _rope_tables(positions):
    half = ROT_DIM // 2
    inv_freq = ROPE_THETA ** (-jnp.arange(0, ROT_DIM, 2, dtype=F32) / ROT_DIM)
    ang = positions.astype(F32)[..., None] * inv_freq
    cos, sin = jnp.cos(ang), jnp.sin(ang)
    b, t = positions.shape
    ones = jnp.ones((b, t, LANE - ROT_DIM), F32)
    zeros = jnp.zeros((b, t, LANE - half), F32)
    c = jnp.concatenate([cos, cos, ones], axis=-1)
    sa = jnp.concatenate([-sin, zeros], axis=-1)
    sb = jnp.concatenate([jnp.zeros((b, t, half), F32), sin, zeros[..., :LANE - ROT_DIM]], axis=-1)
    return c, sa, sb


def _arrange_w_in(w_in):
    d = w_in.shape[0]
    sizes = (N_HEADS_NSA * HEAD_DIM, 3 * 2 * N_KV_NSA * HEAD_DIM, 3 * N_HEADS_NSA,
             3 * N_HEADS_GDN * HEAD_DIM, N_HEADS_GDN, N_HEADS_GDN, N_HEADS_GDN * HEAD_DIM)
    offs = np.cumsum((0,) + sizes)
    seg = [w_in[:, offs[i]:offs[i + 1]] for i in range(len(sizes))]
    q, kv, gate, gqkv, gb, ga, gz = seg
    used = sum(sizes)
    pad = jnp.zeros((d, N_CB * LANE - used), w_in.dtype)
    return jnp.concatenate([q, kv, gqkv, gz, gate, gb, ga, pad], axis=1).astype(BF16)


def kernel(x, positions, attn_norm_w, w_in, cmp_wk, cmp_pek, cmp_wv, cmp_pev, gdn_conv_w, gdn_a_log,
           gdn_dt_bias, gdn_norm_w, w_out, ffn_norm_w, router_group_w, router_group_b,
           router_expert_w, router_expert_b, moe_w_gate, moe_w_up, moe_w_down, final_norm_w):
    b, t, d = x.shape
    n = b * t
    tabs = _rope_tables(positions)
    h = x.reshape(n, d)
    assert w_in.shape[0] == 1, "single-layer block only"
    for l in range(1):
        proj, small = _in_proj(h, attn_norm_w[l], _arrange_w_in(w_in[l]))
        proj3 = proj.reshape(b, t, N_CB * LANE)
        small3 = small.reshape(b, t, LANE)
        cmp_w = jnp.stack([cmp_wk[l], cmp_wv[l]])
        cmp_pe = jnp.stack([cmp_pek[l], cmp_pev[l]])
        kvc = _nsa_compress(proj3, tabs, cmp_w, cmp_pe)
        o_c, selbias = _nsa_cmp_attn(proj3, small3, tabs, kvc)
        o_s = _nsa_sel_attn(proj3, small3, tabs, selbias)
        o_w = _nsa_win_attn(proj3, small3, tabs)
        o_b = _gdn(proj3, small3, gdn_conv_w[l], gdn_a_log[l], gdn_dt_bias[l], gdn_norm_w[l])
        half = N_HEADS_NSA * HEAD_DIM
        wr = jnp.concatenate([router_group_w[l], router_expert_w[l],
                              jnp.zeros((d, LANE - N_GROUPS - N_EXPERTS), F32)], axis=1)
        br = jnp.concatenate([router_group_b[l], router_expert_b[l],
                              jnp.zeros((LANE - N_GROUPS - N_EXPERTS,), F32)]).reshape(1, LANE)
        h1, hn2, ids, wts = _out_proj(o_c.reshape(n, half), o_s.reshape(n, half), o_w.reshape(n, half),
                                      o_b.reshape(n, half), h, w_out[l].astype(BF16), ffn_norm_w[l], wr, br)
        bexp, bact, rtok, rslot, n_slots = _dispatch(ids, n, d // LANE)
        y = _moe(hn2, bexp, bact, rtok, rslot, moe_w_gate[l], moe_w_up[l], moe_w_down[l], n_slots)
        out = _combine(h1, y, wts, final_norm_w)
    return out.reshape(b, t, d)
```

```python
import functools

import numpy as np
import jax
import jax.numpy as jnp
from jax import lax
from jax.experimental import pallas as pl
from jax.experimental.pallas import tpu as pltpu

F32 = jnp.float32
BF16 = jnp.bfloat16

HEAD_DIM = 128
N_HEADS_NSA = 8
N_KV_NSA = 2
N_REP = N_HEADS_NSA // N_KV_NSA
N_HEADS_GDN = 8
ROT_DIM = 32
ROPE_THETA = 500000.0
CMP_LEN = 32
CMP_STRIDE = 16
SLC_LEN = 64
SLC_TOP = 16
WINDOW = 512
CONV_WIDTH = 4
GDN_CHUNK = 64
N_GROUPS = 8
EXPERTS_PER_GROUP = 8
N_EXPERTS = 64
MOE_ROW_BLOCK = 256
EPS = 1e-6
LANE = 128

CB_Q = 0
CB_KV = 8
CB_GQKV = 20
CB_Z = 44
CB_SMALL = 52
N_CB = 54
SMALL_BETA = 24
SMALL_DECAY = 32

NEG_BIAS = -32768.0
VMEM_LIMIT = 56 * 1024 * 1024


def _cparams(sem):
    return pltpu.CompilerParams(dimension_semantics=sem, vmem_limit_bytes=VMEM_LIMIT)


def _mm(a, b):
    return jnp.dot(a.astype(BF16), b.astype(BF16), preferred_element_type=F32)


def _mm_nt(a, b):
    return lax.dot_general(a.astype(BF16), b.astype(BF16), (((1,), (1,)), ((), ())),
                           preferred_element_type=F32)


def _rope(x, c, sa, sb):
    return (x * c + pltpu.roll(x, LANE - ROT_DIM // 2, 1) * sa
            + pltpu.roll(x, ROT_DIM // 2, 1) * sb)


def _sigmoid(x):
    return 0.5 * jnp.tanh(0.5 * x) + 0.5


def _lane_tiles(x):
    return [x[:, i:i + LANE] for i in range(0, x.shape[1], LANE)]


def _store_token_major(ref, x, tmp_ref):
    rows, d = x.shape
    seg = d // LANE
    for s in range(seg):
        tmp_ref[pl.ds(s, rows, stride=seg), :] = x[:, s * LANE:(s + 1) * LANE]
    ref[...] = tmp_ref[...].astype(ref.dtype)


def _load_token_major(ref, rows, tmp_ref):
    seg = ref.shape[0] // rows
    tmp_ref[...] = ref[...].astype(F32)
    return jnp.concatenate([tmp_ref[pl.ds(s, rows, stride=seg), :] for s in range(seg)], axis=1)


def _rowmax(x):
    return jnp.max(functools.reduce(jnp.maximum, _lane_tiles(x)), axis=1, keepdims=True)


def _rowsum(x):
    return jnp.sum(functools.reduce(jnp.add, _lane_tiles(x)), axis=1, keepdims=True)


def _in_proj_kernel(x_ref, nw_ref, w_ref, o_ref, small_ref, hn_ref, *, small_off):
    j = pl.program_id(1)

    @pl.when(j == 0)
    def _():
        x = x_ref[...]
        ms = jnp.mean(x * x, axis=-1, keepdims=True)
        hn_ref[...] = (x * lax.rsqrt(ms + EPS) * nw_ref[...]).astype(BF16)
    acc = jnp.dot(hn_ref[...], w_ref[...], preferred_element_type=F32)
    o_ref[...] = acc.astype(o_ref.dtype)

    @pl.when(j == pl.num_programs(1) - 1)
    def _():
        small_ref[...] = acc[:, small_off:small_off + LANE]


def _in_proj(x2, norm_w, w):
    n, d = x2.shape
    ncol = w.shape[1]
    tm = min(1024, n)
    tn = 768
    small_off = CB_SMALL * LANE - (ncol // tn - 1) * tn
    assert 0 <= small_off <= tn - LANE
    return pl.pallas_call(
        functools.partial(_in_proj_kernel, small_off=small_off),
        grid=(n // tm, ncol // tn),
        in_specs=[pl.BlockSpec((tm, d), lambda i, j: (i, 0)),
                  pl.BlockSpec((1, d), lambda i, j: (0, 0)),
                  pl.BlockSpec((d, tn), lambda i, j: (0, j))],
        out_specs=[pl.BlockSpec((tm, tn), lambda i, j: (i, j)),
                   pl.BlockSpec((tm, LANE), lambda i, j: (i, 0))],
        out_shape=[jax.ShapeDtypeStruct((n, ncol), BF16), jax.ShapeDtypeStruct((n, LANE), F32)],
        scratch_shapes=[pltpu.VMEM((tm, d), BF16)],
        compiler_params=_cparams(("parallel", "arbitrary")),
        name="in_proj",
    )(x2, norm_w.reshape(1, d), w)


def _compress_kernel(a_ref, c_ref, sa_ref, sb_ref, w_ref, pe_ref, o_ref, xs_ref):
    kv = pl.program_id(1)
    t = a_ref.shape[1]
    nc = t // CMP_STRIDE
    x = a_ref[0].astype(F32)
    xr = _rope(x, c_ref[0], sa_ref[0], sb_ref[0])
    x = jnp.where(kv == 0, xr, x)
    xs_ref[pl.ds(0, t), :] = x
    xs_ref[pl.ds(t, CMP_STRIDE), :] = jnp.zeros((CMP_STRIDE, LANE), F32)
    acc = jnp.zeros((nc, LANE), F32)
    for l in range(CMP_LEN):
        rows = xs_ref[pl.ds(l, nc, stride=CMP_STRIDE), :] + pe_ref[0, pl.ds(l, 1), :]
        acc = acc + _mm(rows, w_ref[0, l])
    o_ref[0, 0, 0] = acc.astype(BF16)


def _nsa_compress(proj3, tabs, cmp_w, cmp_pe):
    b, t, _ = proj3.shape
    g = N_KV_NSA
    nc = t // CMP_STRIDE
    tab_spec = pl.BlockSpec((1, t, LANE), lambda bi, kv, gi: (bi, 0, 0))
    return pl.pallas_call(
        _compress_kernel,
        grid=(b, 2, g),
        in_specs=[pl.BlockSpec((1, t, LANE), lambda bi, kv, gi: (bi, 0, CB_KV + kv * g + gi)),
                  tab_spec, tab_spec, tab_spec,
                  pl.BlockSpec((1, CMP_LEN, LANE, LANE), lambda bi, kv, gi: (kv, 0, 0, 0)),
                  pl.BlockSpec((1, CMP_LEN, LANE), lambda bi, kv, gi: (kv, 0, 0))],
        out_specs=pl.BlockSpec((1, 1, 1, nc, LANE), lambda bi, kv, gi: (bi, kv, gi, 0, 0)),
        out_shape=jax.ShapeDtypeStruct((b, 2, g, nc, LANE), BF16),
        scratch_shapes=[pltpu.VMEM((t + CMP_STRIDE, LANE), F32)],
        compiler_params=_cparams(("parallel", "arbitrary", "arbitrary")),
        name="nsa_compress",
    )(proj3, *tabs, cmp_w, cmp_pe)


def _load_q(q_ref, c, sa, sb):
    scale = HEAD_DIM ** -0.5
    qs = [(_rope(q_ref[0, :, r * LANE:(r + 1) * LANE].astype(F32), c, sa, sb) * scale).astype(BF16)
          for r in range(N_REP)]
    return jnp.concatenate(qs, axis=0)


def _store_gated(o_ref, o, gate_ref, g, branch, tq):
    gt = gate_ref[0]
    for r in range(N_REP):
        col = ((g * N_REP + r) * 3 + branch)
        lane = lax.broadcasted_iota(jnp.int32, gt.shape, 1)
        gcol = jnp.sum(jnp.where(lane == col, gt, 0.0), axis=1, keepdims=True)
        o_ref[0, :, r * LANE:(r + 1) * LANE] = (o[r * tq:(r + 1) * tq] * _sigmoid(gcol)).astype(o_ref.dtype)


Q_TILES_PER_STEP = 4


def _row_views(refs, sub, tq):
    rows = pl.ds(sub * tq, tq)
    return [r.at[:, rows, :] if len(r.shape) == 3 else r.at[:, :, rows, :] for r in refs]


def _cmp_attn_kernel(q_ref, c_ref, sa_ref, sb_ref, kc_ref, vc_ref, gate_ref, o_ref, sel_ref, *, n_slc):
    tq = q_ref.shape[1] // Q_TILES_PER_STEP
    for sub in range(Q_TILES_PER_STEP):
        q, c, sa, sb, gate, o, sel = _row_views(
            (q_ref, c_ref, sa_ref, sb_ref, gate_ref, o_ref, sel_ref), sub, tq)
        _cmp_attn_tile(pl.program_id(2) * Q_TILES_PER_STEP + sub, q, c, sa, sb, kc_ref, vc_ref, gate, o,
                       sel, n_slc=n_slc)


def _cmp_attn_tile(i, q_ref, c_ref, sa_ref, sb_ref, kc_ref, vc_ref, gate_ref, o_ref, sel_ref, *, n_slc):
    g = pl.program_id(1)
    tq = q_ref.shape[1]
    nc = kc_ref.shape[3]
    q4 = _load_q(q_ref, c_ref[0], sa_ref[0], sb_ref[0])
    s = _mm_nt(q4, kc_ref[0, 0, 0])
    row = lax.broadcasted_iota(jnp.int32, s.shape, 0)
    n = lax.broadcasted_iota(jnp.int32, s.shape, 1)
    tpos = i * tq + (row & (tq - 1))
    mask = (n * CMP_STRIDE + (CMP_LEN - 1) <= tpos) & (n < nc - 1)
    sm = jnp.where(mask, s, -1e30)
    m = _rowmax(sm)
    p = jnp.where(mask, jnp.exp(sm - m), 0.0)
    l = _rowsum(p)
    p = p / jnp.maximum(l, 1e-30)
    o = _mm(p, vc_ref[0, 0, 0])
    _store_gated(o_ref, o, gate_ref, g, 0, tq)

    ps = p[0:tq]
    for r in range(1, N_REP):
        ps = ps + p[r * tq:(r + 1) * tq]
    cn = lax.broadcasted_iota(jnp.int32, (nc, LANE), 0)
    cj = lax.broadcasted_iota(jnp.int32, (nc, LANE), 1)
    ratio = SLC_LEN // CMP_STRIDE
    agg = ((cn >= ratio * cj - (CMP_LEN // CMP_STRIDE - 1)) & (cn < ratio * cj + ratio)
           & (cn < nc - 1) & (cj < n_slc))
    agg = jnp.where(agg, 1.0, 0.0).astype(BF16)
    ps_hi = ps.astype(BF16)
    ps_lo = (ps - ps_hi.astype(F32)).astype(BF16)
    imp = (jnp.dot(ps_hi, agg, preferred_element_type=F32)
           + jnp.dot(ps_lo, agg, preferred_element_type=F32))

    j = lax.broadcasted_iota(jnp.int32, (tq, LANE), 1)
    tt = i * tq + lax.broadcasted_iota(jnp.int32, (tq, LANE), 0)
    cur = tt // SLC_LEN
    valid = j <= cur
    forced = (j == 0) | (j == cur) | (j == cur - 1)
    vals = jnp.where(forced, 1e30, jnp.where(valid, imp, -1.0))
    vt = vals.T
    jb = lax.broadcasted_iota(jnp.int32, (n_slc, tq), 0)
    vb = vt[0:n_slc]
    cnt = jnp.zeros((n_slc, tq), F32)
    for jp in range(n_slc):
        cand = vt[jp:jp + 1, :]
        ge = jnp.where(cand >= vb, 1.0, 0.0)
        gt = jnp.where(cand > vb, 1.0, 0.0)
        cnt = cnt + jnp.where(jb > jp, ge, gt)
    keep = jnp.where(cnt < float(min(SLC_TOP, n_slc)), 0.0, NEG_BIAS)
    if n_slc < LANE:
        keep = jnp.concatenate([keep, jnp.full((LANE - n_slc, tq), NEG_BIAS, F32)], axis=0)
    sel_ref[0, 0] = jnp.where(valid, keep.T, NEG_BIAS).astype(BF16)


def _nsa_cmp_attn(proj3, small3, tabs, kvc, tq=128):
    b, t, _ = proj3.shape
    g = N_KV_NSA
    nc = t // CMP_STRIDE
    n_slc = t // SLC_LEN
    rw = N_REP * LANE
    bq = tq * Q_TILES_PER_STEP
    tab_spec = pl.BlockSpec((1, bq, LANE), lambda bi, gi, i: (bi, i, 0))
    return pl.pallas_call(
        functools.partial(_cmp_attn_kernel, n_slc=n_slc),
        grid=(b, g, t // bq),
        in_specs=[pl.BlockSpec((1, bq, rw), lambda bi, gi, i: (bi, i, gi)),
                  tab_spec, tab_spec, tab_spec,
                  pl.BlockSpec((1, 1, 1, nc, LANE), lambda bi, gi, i: (bi, 0, gi, 0, 0)),
                  pl.BlockSpec((1, 1, 1, nc, LANE), lambda bi, gi, i: (bi, 1, gi, 0, 0)),
                  pl.BlockSpec((1, bq, LANE), lambda bi, gi, i: (bi, i, 0))],
        out_specs=[pl.BlockSpec((1, bq, rw), lambda bi, gi, i: (bi, i, gi)),
                   pl.BlockSpec((1, 1, bq, LANE), lambda bi, gi, i: (bi, gi, i, 0))],
        out_shape=[jax.ShapeDtypeStruct((b, t, N_HEADS_NSA * LANE), BF16),
                   jax.ShapeDtypeStruct((b, g, t, LANE), BF16)],
        compiler_params=_cparams(("parallel", "parallel", "parallel")),
        name="nsa_cmp",
    )(proj3, *tabs, kvc, kvc, small3)


def _sel_attn_kernel(q_ref, c_ref, sa_ref, sb_ref, cf_ref, saf_ref, sbf_ref, k_ref, v_ref,
                     sel_ref, gate_ref, o_ref, *scratch, tk):
    tq = q_ref.shape[1] // Q_TILES_PER_STEP
    for sub in range(Q_TILES_PER_STEP):
        q, c, sa, sb, sel, gate, o = _row_views(
            (q_ref, c_ref, sa_ref, sb_ref, sel_ref, gate_ref, o_ref), sub, tq)
        _sel_attn_tile(pl.program_id(2) * Q_TILES_PER_STEP + sub, q, c, sa, sb, cf_ref, saf_ref, sbf_ref,
                       k_ref, v_ref, sel, gate, o, *scratch, tk=tk)


def _sel_attn_tile(i, q_ref, c_ref, sa_ref, sb_ref, cf_ref, saf_ref, sbf_ref, k_ref, v_ref,
                   sel_ref, gate_ref, o_ref, ka_ref, vs_ref, m_ref, l_ref, acc_ref, s0_ref, s1_ref,
                   *, tk):
    g = pl.program_id(1)
    tq = q_ref.shape[1]
    t = k_ref.shape[1]
    rows = N_REP * tq

    @pl.when(i == 0)
    def _():
        kr = _rope(k_ref[0].astype(F32), cf_ref[0], saf_ref[0], sbf_ref[0])
        ka_ref[:, 0:LANE] = kr.astype(BF16)
        pos = lax.broadcasted_iota(jnp.int32, (t, LANE), 0)
        lane = lax.broadcasted_iota(jnp.int32, (t, LANE), 1)
        ka_ref[:, LANE:2 * LANE] = jnp.where(lane == pos // SLC_LEN, 1.0, 0.0).astype(BF16)
        vs_ref[...] = v_ref[0].astype(BF16)

    q4 = _load_q(q_ref, c_ref[0], sa_ref[0], sb_ref[0])
    bias = sel_ref[0, 0]
    qa = jnp.concatenate([q4, jnp.concatenate([bias] * N_REP, axis=0)], axis=1)

    m_ref[...] = jnp.full((rows, LANE), -1e30, F32)
    l_ref[...] = jnp.zeros((rows, LANE), F32)
    acc_ref[...] = jnp.zeros((rows, LANE), F32)

    half = tk // 2

    def scores(kt, hlf):
        k0 = pl.multiple_of(kt * tk + hlf * half, half)
        return _mm_nt(qa, ka_ref[pl.ds(k0, half), :])

    def update(s_ref, kt, hlf, causal):
        k0 = pl.multiple_of(kt * tk + hlf * half, half)
        s = s_ref[...]
        if causal:
            r = lax.broadcasted_iota(jnp.int32, s.shape, 0)
            kp = k0 + lax.broadcasted_iota(jnp.int32, s.shape, 1)
            s = jnp.where(kp <= i * tq + (r & (tq - 1)), s, -1e30)
        tiles = _lane_tiles(s)
        m_old = m_ref[...]
        m_new = jnp.maximum(m_old, _rowmax(s))
        alpha = jnp.exp(m_old - m_new)
        ps = [jnp.exp(tl - m_new) for tl in tiles]
        l_ref[...] = alpha * l_ref[...] + jnp.sum(functools.reduce(jnp.add, ps), axis=1, keepdims=True)
        p = jnp.concatenate([x.astype(BF16) for x in ps], axis=1)
        acc_ref[...] = alpha * acc_ref[...] + jnp.dot(p, vs_ref[pl.ds(k0, half), :],
                                                      preferred_element_type=F32)
        m_ref[...] = m_new

    diag = (i * tq) // tk
    s0_ref[...] = scores(0, 0)

    def body(kt, carry):
        s1_ref[...] = scores(kt, 1)
        update(s0_ref, kt, 0, False)
        s0_ref[...] = scores(kt + 1, 0)
        update(s1_ref, kt, 1, False)
        return carry

    def body2(j, carry):
        body(2 * j, carry)
        body(2 * j + 1, carry)
        return carry

    lax.fori_loop(0, diag // 2, body2, 0)
    lax.fori_loop((diag // 2) * 2, diag, body, 0)
    s1_ref[...] = scores(diag, 1)
    update(s0_ref, diag, 0, True)
    update(s1_ref, diag, 1, True)
    o = acc_ref[...] / l_ref[...]
    _store_gated(o_ref, o, gate_ref, g, 1, tq)


def _nsa_sel_attn(proj3, small3, tabs, selbias, tq=128, tk=512):
    b, t, _ = proj3.shape
    g = N_KV_NSA
    rw = N_REP * LANE
    bq = tq * Q_TILES_PER_STEP
    tk = min(tk, t)
    tab_spec = pl.BlockSpec((1, bq, LANE), lambda bi, gi, i: (bi, i, 0))
    tabf_spec = pl.BlockSpec((1, t, LANE), lambda bi, gi, i: (bi, 0, 0))
    kcb = CB_KV + 1 * 2 * g
    return pl.pallas_call(
        functools.partial(_sel_attn_kernel, tk=tk),
        grid=(b, g, t // bq),
        in_specs=[pl.BlockSpec((1, bq, rw), lambda bi, gi, i: (bi, i, gi)),
                  tab_spec, tab_spec, tab_spec, tabf_spec, tabf_spec, tabf_spec,
                  pl.BlockSpec((1, t, LANE), lambda bi, gi, i: (bi, 0, kcb + gi)),
                  pl.BlockSpec((1, t, LANE), lambda bi, gi, i: (bi, 0, kcb + g + gi)),
                  pl.BlockSpec((1, 1, bq, LANE), lambda bi, gi, i: (bi, gi, i, 0)),
                  pl.BlockSpec((1, bq, LANE), lambda bi, gi, i: (bi, i, 0))],
        out_specs=pl.BlockSpec((1, bq, rw), lambda bi, gi, i: (bi, i, gi)),
        out_shape=jax.ShapeDtypeStruct((b, t, N_HEADS_NSA * LANE), BF16),
        scratch_shapes=[pltpu.VMEM((t, 2 * LANE), BF16), pltpu.VMEM((t, LANE), BF16),
                        pltpu.VMEM((N_REP * tq, LANE), F32), pltpu.VMEM((N_REP * tq, LANE), F32),
                        pltpu.VMEM((N_REP * tq, LANE), F32),
                        pltpu.VMEM((N_REP * tq, tk // 2), F32), pltpu.VMEM((N_REP * tq, tk // 2), F32)],
        compiler_params=_cparams(("parallel", "parallel", "arbitrary")),
        name="nsa_sel",
    )(proj3, *tabs, *tabs, proj3, proj3, selbias, small3)


def _win_attn_kernel(q_ref, c_ref, sa_ref, sb_ref, cf_ref, saf_ref, sbf_ref, k_ref, v_ref,
                     gate_ref, o_ref, ks_ref, vs_ref, *, span):
    tq = q_ref.shape[1] // Q_TILES_PER_STEP
    for sub in range(Q_TILES_PER_STEP):
        q, c, sa, sb, gate, o = _row_views((q_ref, c_ref, sa_ref, sb_ref, gate_ref, o_ref), sub, tq)
        _win_attn_tile(pl.program_id(2) * Q_TILES_PER_STEP + sub, q, c, sa, sb, cf_ref, saf_ref, sbf_ref,
                       k_ref, v_ref, gate, o, ks_ref, vs_ref, span=span)


def _win_attn_tile(i, q_ref, c_ref, sa_ref, sb_ref, cf_ref, saf_ref, sbf_ref, k_ref, v_ref,
                   gate_ref, o_ref, ks_ref, vs_ref, *, span):
    g = pl.program_id(1)
    tq = q_ref.shape[1]

    t = k_ref.shape[1]

    @pl.when(i == 0)
    def _():
        zeros = jnp.zeros((WINDOW, LANE), BF16)
        ks_ref[pl.ds(0, WINDOW), :] = zeros
        vs_ref[pl.ds(0, WINDOW), :] = zeros
        ks_ref[pl.ds(WINDOW, t), :] = _rope(k_ref[0].astype(F32), cf_ref[0], saf_ref[0],
                                            sbf_ref[0]).astype(BF16)
        vs_ref[pl.ds(WINDOW, t), :] = v_ref[0].astype(BF16)

    q4 = _load_q(q_ref, c_ref[0], sa_ref[0], sb_ref[0])
    k0 = pl.multiple_of(i * tq, tq)
    s = _mm_nt(q4, ks_ref[pl.ds(k0, span), :])
    tiles = _lane_tiles(s)
    r = lax.broadcasted_iota(jnp.int32, tiles[0].shape, 0) & (tq - 1)
    c = lax.broadcasted_iota(jnp.int32, tiles[0].shape, 1)
    first_block = WINDOW // tq - i
    masked = []
    for b, tl in enumerate(tiles):
        if b == 0:
            tl = jnp.where(c > r, tl, -1e30)
        if b == len(tiles) - 1:
            tl = jnp.where(c <= r, tl, -1e30)
        else:
            tl = jnp.where(b >= first_block, tl, -1e30)
        masked.append(tl)
    m = jnp.max(functools.reduce(jnp.maximum, masked), axis=1, keepdims=True)
    ps = [jnp.exp(tl - m) for tl in masked]
    l = jnp.sum(functools.reduce(jnp.add, ps), axis=1, keepdims=True)
    p = jnp.concatenate([x.astype(BF16) for x in ps], axis=1)
    o = jnp.dot(p, vs_ref[pl.ds(k0, span), :], preferred_element_type=F32) / l
    _store_gated(o_ref, o, gate_ref, g, 2, tq)


def _nsa_win_attn(proj3, small3, tabs, tq=128):
    b, t, _ = proj3.shape
    g = N_KV_NSA
    rw = N_REP * LANE
    bq = tq * Q_TILES_PER_STEP
    span = WINDOW + tq
    assert tq == LANE
    tab_spec = pl.BlockSpec((1, bq, LANE), lambda bi, gi, i: (bi, i, 0))
    tabf_spec = pl.BlockSpec((1, t, LANE), lambda bi, gi, i: (bi, 0, 0))
    kcb = CB_KV + 2 * 2 * g
    return pl.pallas_call(
        functools.partial(_win_attn_kernel, span=span),
        grid=(b, g, t // bq),
        in_specs=[pl.BlockSpec((1, bq, rw), lambda bi, gi, i: (bi, i, gi)),
                  tab_spec, tab_spec, tab_spec, tabf_spec, tabf_spec, tabf_spec,
                  pl.BlockSpec((1, t, LANE), lambda bi, gi, i: (bi, 0, kcb + gi)),
                  pl.BlockSpec((1, t, LANE), lambda bi, gi, i: (bi, 0, kcb + g + gi)),
                  pl.BlockSpec((1, bq, LANE), lambda bi, gi, i: (bi, i, 0))],
        out_specs=pl.BlockSpec((1, bq, rw), lambda bi, gi, i: (bi, i, gi)),
        out_shape=jax.ShapeDtypeStruct((b, t, N_HEADS_NSA * LANE), BF16),
        scratch_shapes=[pltpu.VMEM((t + WINDOW, LANE), BF16), pltpu.VMEM((t + WINDOW, LANE), BF16)],
        compiler_params=_cparams(("parallel", "parallel", "arbitrary")),
        name="nsa_win",
    )(proj3, *tabs, *tabs, proj3, proj3, small3)


def _gdn_kernel(q_ref, k_ref, v_ref, z_ref, braw_ref, araw_ref, cwq_ref, cwk_ref, cwv_ref, alog_ref,
                dtb_ref, nw_ref, o_ref, q_s, k_s, v_s, o_s, qp_s, op_s, xp_s, bd_s, gcd_s, gl_s, mm_s,
                nn_s):
    t = q_ref.shape[1]
    c = GDN_CHUNK
    nchunk = t // c
    xp_s[pl.ds(0, 8), :] = jnp.zeros((8, LANE), F32)

    def conv_silu(x, cw):
        xp_s[pl.ds(8, t), :] = x
        y = x * cw[CONV_WIDTH - 1:CONV_WIDTH]
        for sft in range(1, CONV_WIDTH):
            y = y + xp_s[pl.ds(8 - sft, t), :] * cw[CONV_WIDTH - 1 - sft:CONV_WIDTH - sft]
        return y * _sigmoid(y)

    def l2n(x):
        return x * lax.rsqrt(jnp.sum(x * x, axis=1, keepdims=True) + EPS)

    q_s[...] = l2n(conv_silu(q_ref[0].astype(F32), cwq_ref[...])) * (HEAD_DIM ** -0.5)
    k_s[...] = l2n(conv_silu(k_ref[0].astype(F32), cwk_ref[...]))
    v_s[...] = conv_silu(v_ref[0].astype(F32), cwv_ref[...])

    bd_s[...] = _sigmoid(braw_ref[0, 0])
    xa = araw_ref[0, 0] + dtb_ref[0]
    softplus = jnp.maximum(xa, 0.0) + jnp.log(1.0 + jnp.exp(-jnp.abs(xa)))
    gc = -jnp.exp(alog_ref[0]) * softplus
    lane = lax.broadcasted_iota(jnp.int32, gc.shape, 1)
    sft = 1
    while sft < c:
        gc = gc + jnp.where((lane & (c - 1)) >= sft, pltpu.roll(gc, sft, 1), 0.0)
        sft *= 2
    gcd_s[...] = gc

    c2 = 2 * c
    ci = lax.broadcasted_iota(jnp.int32, (c2, c2), 0)
    cj = lax.broadcasted_iota(jnp.int32, (c2, c2), 1)
    same = (ci // c) == (cj // c)
    tril = same & (cj <= ci)
    strict = same & (cj < ci)
    eye = jnp.where(ci == cj, 1.0, 0.0)
    first = ci < c

    pairs_per_iter = 8
    rng = range(pairs_per_iter)

    def prep(it, carry):
        n2 = [it * pairs_per_iter + p for p in rng]
        sl = [pl.ds(pl.multiple_of(n * c2, c2), c2) for n in n2]
        grow = [gcd_s[pl.ds(n, 1), :] for n in n2]
        gc2 = [jnp.broadcast_to(g_, (c2, c2)).T for g_ in grow]
        betac = [jnp.broadcast_to(bd_s[pl.ds(n, 1), :], (c2, c2)).T for n in n2]
        kn = [k_s[s_, :] for s_ in sl]
        kbn = [kn[p] * betac[p] for p in rng]
        decay = [jnp.exp(jnp.where(tril, gc2[p] - grow[p], -1e30)) for p in rng]
        kk = [_mm_nt(kbn[p], kn[p]) for p in rng]
        y = [-jnp.where(strict, kk[p] * decay[p], 0.0) for p in rng]
        pm = [eye + y_ for y_ in y]
        pw = 1
        while 2 * pw < c:
            y = [_mm(y_, y_) for y_ in y]
            pm = [pm[p] + _mm(pm[p], y[p]) for p in rng]
            pw *= 2
        egn = [jnp.exp(g_) for g_ in gc2]
        uw = [_mm(pm[p], jnp.concatenate([v_s[sl[p], :] * betac[p], kbn[p] * egn[p]], axis=1))
              for p in rng]
        qn = [q_s[s_, :] for s_ in sl]
        qk = [_mm_nt(qn[p], kn[p]) for p in rng]
        attn = [jnp.where(tril, qk[p] * decay[p], 0.0) for p in rng]
        auw = [_mm(attn[p], uw[p]) for p in rng]
        for p in rng:
            qp_s[sl[p], :] = qn[p] * egn[p] - auw[p][:, LANE:]
            op_s[sl[p], :] = auw[p][:, :LANE]
        gl = [jnp.where(first, g_[c - 1:c, :], g_[c2 - 1:c2, :]) for g_ in gc2]
        kdt = [(kn[p] * jnp.exp(gl[p] - gc2[p])).T for p in rng]
        nma = [_mm(kdt[p][:, 0:c], uw[p][0:c]) for p in rng]
        nmb = [_mm(kdt[p][:, c:c2], uw[p][c:c2]) for p in rng]
        mb = [x[:, LANE:] for x in nmb]
        mbnm = [_mm(mb[p], nma[p]) for p in rng]
        ma = [x[:, LANE:] for x in nma]
        na = [x[:, :LANE] for x in nma]
        nb_ = [x[:, :LANE] for x in nmb]
        mba = [x[:, LANE:] for x in mbnm]
        mbn = [x[:, :LANE] for x in mbnm]
        for p in rng:
            ga = jnp.exp(gc2[p][c - 1:c, :])
            gb = jnp.exp(gc2[p][c2 - 1:c2, :])
            mm_s[2 * n2[p]] = ma[p].astype(BF16)
            nn_s[2 * n2[p]] = na[p]
            gl_s[pl.ds(2 * n2[p], 1), :] = ga
            mm_s[2 * n2[p] + 1] = (gb * ma[p] + ga * mb[p] - mba[p]).astype(BF16)
            nn_s[2 * n2[p] + 1] = gb * na[p] - mbn[p] + nb_[p]
            gl_s[pl.ds(2 * n2[p] + 1, 1), :] = ga * gb
        return carry

    lax.fori_loop(0, nchunk // (2 * pairs_per_iter), prep, 0)

    def second_out(n2, s_mid):
        rows = pl.ds(pl.multiple_of(n2 * c2, c2) + c, c)
        o_s[rows, :] = (jnp.dot(qp_s[rows, :].astype(BF16), s_mid.astype(BF16),
                                 preferred_element_type=F32) + op_s[rows, :])

    def scan(n2, carry):
        s, s_mid_prev = carry
        second_out(jnp.maximum(n2 - 1, 0), s_mid_prev)
        ra = pl.ds(pl.multiple_of(n2 * c2, c2), c)
        sb = s.astype(BF16)
        s_mid = (s * gl_s[pl.ds(2 * n2, 1), :] - jnp.dot(mm_s[2 * n2], sb, preferred_element_type=F32)
                 + nn_s[2 * n2])
        s_new = (s * gl_s[pl.ds(2 * n2 + 1, 1), :]
                 - jnp.dot(mm_s[2 * n2 + 1], sb, preferred_element_type=F32) + nn_s[2 * n2 + 1])
        o_s[ra, :] = jnp.dot(qp_s[ra, :].astype(BF16), sb, preferred_element_type=F32) + op_s[ra, :]
        return s_new, s_mid

    zero_state = jnp.zeros((LANE, LANE), F32)
    _, s_mid_last = lax.fori_loop(0, nchunk // 2, scan, (zero_state, zero_state))
    second_out(jnp.int32(nchunk // 2 - 1), s_mid_last)

    o = o_s[...]
    on = o * lax.rsqrt(jnp.mean(o * o, axis=1, keepdims=True) + EPS) * nw_ref[...]
    z = z_ref[0].astype(F32)
    o_ref[0] = (on * (z * _sigmoid(z))).astype(o_ref.dtype)


def _gdn(proj3, small3, conv_w, a_log, dt_bias, norm_w):
    b, t, _ = proj3.shape
    hh = N_HEADS_GDN
    c = GDN_CHUNK
    col = lambda cb: pl.BlockSpec((1, t, LANE), lambda bi, hi: (bi, 0, cb + hi))
    cw = lambda off: pl.BlockSpec((CONV_WIDTH, LANE), lambda bi, hi: (0, off + hi))
    hrow = pl.BlockSpec((1, 1, LANE), lambda bi, hi: (hi, 0, 0))
    alog_b = jnp.broadcast_to(a_log.astype(F32)[:, None, None], (hh, 1, LANE))
    dtb_b = jnp.broadcast_to(dt_bias.astype(F32)[:, None, None], (hh, 1, LANE))
    big = pltpu.VMEM((t, LANE), F32)
    nrow = t // LANE
    ba = small3[:, :, SMALL_BETA:SMALL_BETA + 2 * hh].transpose(0, 2, 1).reshape(b, 2 * hh, nrow, LANE)
    dense = lambda off: pl.BlockSpec((1, 1, nrow, LANE), lambda bi, hi: (bi, off + hi, 0, 0))
    return pl.pallas_call(
        _gdn_kernel,
        grid=(b, hh),
        in_specs=[col(CB_GQKV), col(CB_GQKV + hh), col(CB_GQKV + 2 * hh), col(CB_Z),
                  dense(0), dense(hh),
                  cw(0), cw(hh), cw(2 * hh), hrow, hrow,
                  pl.BlockSpec((1, LANE), lambda bi, hi: (0, 0))],
        out_specs=pl.BlockSpec((1, t, LANE), lambda bi, hi: (bi, 0, hi)),
        out_shape=jax.ShapeDtypeStruct((b, t, hh * LANE), BF16),
        scratch_shapes=[big, big, big, big, big, big, pltpu.VMEM((t + 8, LANE), F32),
                        pltpu.VMEM((nrow, LANE), F32), pltpu.VMEM((nrow, LANE), F32),
                        pltpu.VMEM((t // c, LANE), F32),
                        pltpu.VMEM((t // c, LANE, LANE), BF16),
                        pltpu.VMEM((t // c, LANE, LANE), F32)],
        compiler_params=_cparams(("parallel", "parallel")),
        name="gdn",
    )(proj3, proj3, proj3, proj3, ba, ba, conv_w, conv_w, conv_w, alog_b, dtb_b,
      norm_w.reshape(1, LANE))


def _split3(a):
    hi = a.astype(BF16)
    lo = (a - hi.astype(F32)).astype(BF16)
    return hi, lo


def _out_proj_kernel(oc_ref, os_ref, ow_ref, ob_ref, x_ref, wo_ref, fw_ref, wr_ref, br_ref,
                     h_ref, hn_ref, ids_ref, wts_ref, tmp_ref):
    half = oc_ref.shape[1]
    oa = (oc_ref[...].astype(F32) + os_ref[...].astype(F32) + ow_ref[...].astype(F32)).astype(BF16)
    h1 = (x_ref[...] + jnp.dot(oa, wo_ref[0:half, :], preferred_element_type=F32)
          + jnp.dot(ob_ref[...], wo_ref[half:2 * half, :], preferred_element_type=F32))
    h_ref[...] = h1
    hn = h1 * lax.rsqrt(jnp.mean(h1 * h1, axis=1, keepdims=True) + EPS) * fw_ref[...]
    _store_token_major(hn_ref, hn, tmp_ref)

    a_hi, a_lo = _split3(hn)
    w_hi, w_lo = _split3(wr_ref[...])
    dot = lambda a, b: jnp.dot(a, b, preferred_element_type=F32)
    logits = dot(a_hi, w_hi) + dot(a_hi, w_lo) + dot(a_lo, w_hi) + br_ref[...]
    lane = lax.broadcasted_iota(jnp.int32, logits.shape, 1)
    big = 1e30
    is_g = lane < N_GROUPS
    lg = jnp.where(is_g, logits, -big)
    gm = jnp.max(lg, axis=1, keepdims=True)
    grp = jnp.min(jnp.where(lg == gm, lane, LANE), axis=1, keepdims=True)
    p_grp = 1.0 / jnp.sum(jnp.where(is_g, jnp.exp(lg - gm), 0.0), axis=1, keepdims=True)
    e_id = lane - N_GROUPS
    in_g = (e_id >= 0) & (e_id < N_EXPERTS) & ((e_id // EXPERTS_PER_GROUP) == grp)
    le = jnp.where(in_g, logits, -big)
    em = jnp.max(le, axis=1, keepdims=True)
    pe = jnp.where(in_g, jnp.exp(le - em), 0.0)
    pe = pe / jnp.sum(pe, axis=1, keepdims=True)
    pm = jnp.where(in_g, pe, -1.0)
    p1 = jnp.max(pm, axis=1, keepdims=True)
    i1 = jnp.min(jnp.where(pm == p1, lane, LANE), axis=1, keepdims=True)
    pm2 = jnp.where(lane == i1, -1.0, pm)
    p2 = jnp.max(pm2, axis=1, keepdims=True)
    i2 = jnp.min(jnp.where(pm2 == p2, lane, LANE), axis=1, keepdims=True)
    den = p1 + p2
    ids_ref[...] = jnp.where(lane == 0, i1 - N_GROUPS, jnp.where(lane == 1, i2 - N_GROUPS, 0))
    wts_ref[...] = jnp.where(lane == 0, p1 / den * p_grp, jnp.where(lane == 1, p2 / den * p_grp, 0.0))


def _out_proj(oc, os_, ow, ob, x2, w_out, ffn_w, wr, br):
    n, d = x2.shape
    half = oc.shape[1]
    seg = d // LANE
    tm = min(256, n)
    row = lambda w: pl.BlockSpec((tm, w), lambda i: (i, 0))
    full = lambda a: pl.BlockSpec(a.shape, lambda i: (0,) * a.ndim)
    fw = ffn_w.reshape(1, d)
    return pl.pallas_call(
        _out_proj_kernel,
        grid=(n // tm,),
        in_specs=[row(half), row(half), row(half), row(half), row(d), full(w_out), full(fw),
                  full(wr), full(br)],
        out_specs=[row(d), pl.BlockSpec((tm * seg, LANE), lambda i: (i, 0)), row(LANE), row(LANE)],
        out_shape=[jax.ShapeDtypeStruct((n, d), F32), jax.ShapeDtypeStruct((n * seg, LANE), BF16),
                   jax.ShapeDtypeStruct((n, LANE), jnp.int32), jax.ShapeDtypeStruct((n, LANE), F32)],
        scratch_shapes=[pltpu.VMEM((tm * seg, LANE), F32)],
        compiler_params=_cparams(("parallel",)),
        name="out_proj",
    )(oc, os_, ow, ob, x2, w_out, fw, wr, br)


def _moe_kernel(bexp_ref, bact_ref, bfirst_ref, bord_ref, bnext_ref, rtok_ref, rtokn_ref, rslotp_ref,
                rslot_ref, hn_hbm, wg_hbm, wu_hbm, wd_hbm, y_hbm, xbuf, ybuf, tmp, wg_st, wu_st, wd_st,
                wgb, wub, wdb, gsem, ssem, wsem, *, n_real):
    i = pl.program_id(0)
    nb = pl.num_programs(0)
    rb = rtok_ref.shape[2]
    seg = ybuf.shape[1] // rb
    slot = i & 1
    other = 1 - slot
    active = bact_ref[i] > 0
    prev_active = jnp.logical_and(i > 0, bact_ref[jnp.maximum(i - 1, 0)] > 0)

    def gather_copy(tok, r, buf):
        return pltpu.make_async_copy(hn_hbm.at[pl.ds(pl.multiple_of(tok, seg), seg), :],
                                     xbuf.at[buf, pl.ds(r * seg, seg), :], gsem.at[buf])

    def scatter_copy(dst, r, buf):
        return pltpu.make_async_copy(ybuf.at[buf, pl.ds(r * seg, seg), :],
                                     y_hbm.at[pl.ds(pl.multiple_of(dst, seg), seg), :], ssem.at[buf])

    def wait_gather(buf):
        pltpu.make_async_copy(hn_hbm.at[pl.ds(0, rb * seg), :], xbuf.at[buf], gsem.at[buf]).wait()

    def wait_scatter(buf):
        pltpu.make_async_copy(ybuf.at[buf], y_hbm.at[pl.ds(0, rb * seg), :], ssem.at[buf]).wait()

    def weight_copies(e, ws):
        return (pltpu.make_async_copy(wg_hbm.at[e], wg_st.at[ws], wsem.at[ws]),
                pltpu.make_async_copy(wu_hbm.at[e], wu_st.at[ws], wsem.at[ws]),
                pltpu.make_async_copy(wd_hbm.at[e], wd_st.at[ws], wsem.at[ws]))

    @pl.when(i == 0)
    def _():
        for cp in weight_copies(bexp_ref[0], 0):
            cp.start(priority=1)
        ybuf[...] = jnp.zeros(ybuf.shape, ybuf.dtype)
        pltpu.make_async_copy(ybuf.at[0], y_hbm.at[pl.ds(n_real * seg, rb * seg), :],
                              ssem.at[0]).start()

        def first(r, carry):
            gather_copy(rtok_ref[0, 0, r], r, 0).start()
            return carry
        lax.fori_loop(0, rb, first, 0)

    @pl.when(jnp.logical_or(i == 0, prev_active))
    def _():
        wait_gather(slot)

    @pl.when(active)
    def _():
        @pl.when(bfirst_ref[i] > 0)
        def _():
            ws = bord_ref[i] & 1
            for cp in weight_copies(bexp_ref[i], ws):
                cp.wait()

            @pl.when(bnext_ref[i] >= 0)
            def _():
                for cp in weight_copies(bnext_ref[i], 1 - ws):
                    cp.start(priority=1)

            wgb[...] = wg_st[ws].astype(BF16)
            wub[...] = wu_st[ws].astype(BF16)
            wdb[...] = wd_st[ws].astype(BF16)

        n_piece = 8
        per = rb // n_piece

        def issue(piece):
            half_n = n_piece // 2
            if piece < half_n:
                for r in range(piece * 2 * per, (piece + 1) * 2 * per):
                    gather_copy(rtokn_ref[0, 0, r], r, other).start()
            else:
                for r in range((piece - half_n) * 2 * per, (piece - half_n + 1) * 2 * per):
                    scatter_copy(rslotp_ref[0, 0, r], r, other).start()

        x = _load_token_major(xbuf.at[slot], rb, tmp).astype(BF16)
        de = wgb.shape[1]
        d = wdb.shape[1]
        hc = de // 2
        acts = []
        for c in range(2):
            issue(2 * c)
            hg = jnp.dot(x, wgb[:, c * hc:(c + 1) * hc], preferred_element_type=F32)
            issue(2 * c + 1)
            hu = jnp.dot(x, wub[:, c * hc:(c + 1) * hc], preferred_element_type=F32)
            acts.append((hg * _sigmoid(hg) * hu).astype(BF16))
        act = jnp.concatenate(acts, axis=1)
        dc = d // 4
        ys = []
        for j in range(4):
            issue(4 + j)
            ys.append(jnp.dot(act, wdb[:, j * dc:(j + 1) * dc], preferred_element_type=F32))
        wait_scatter(slot)
        _store_token_major(ybuf.at[slot], jnp.concatenate(ys, axis=1), tmp)

    def scatter_all(idx_ref, buf):
        def body(r, carry):
            scatter_copy(idx_ref[0, 0, r], r, buf).start()
            return carry
        lax.fori_loop(0, rb, body, 0)

    @pl.when(jnp.logical_and(jnp.logical_not(active), prev_active))
    def _():
        wait_scatter(slot)
        scatter_all(rslotp_ref, other)
        wait_scatter(other)

    @pl.when(jnp.logical_and(i == nb - 1, active))
    def _():
        wait_scatter(other)
        scatter_all(rslot_ref, slot)
        wait_scatter(slot)
        wait_gather(other)


def _moe(hn, bexp, bact, rtok, rslot, w_gate, w_up, w_down, n_real):
    d, de = w_gate.shape[1], w_gate.shape[2]
    seg = d // LANE
    nb = bexp.shape[0]
    rb = MOE_ROW_BLOCK
    idx = jnp.arange(nb, dtype=jnp.int32)
    prev_e = jnp.concatenate([jnp.full((1,), -1, jnp.int32), bexp[:-1]])
    bfirst = jnp.logical_and(bact > 0, jnp.logical_or(idx == 0, bexp != prev_e)).astype(jnp.int32)
    bord = jnp.cumsum(bfirst) - 1
    first_at = jnp.where(bfirst > 0, idx, nb)
    next_first = jnp.concatenate([lax.cummin(first_at[::-1])[::-1][1:], jnp.full((1,), nb, jnp.int32)])
    bnext = jnp.where(next_first < nb, bexp[jnp.minimum(next_first, nb - 1)], -1).astype(jnp.int32)

    smem_rows = pl.BlockSpec((1, 1, rb), lambda i, *_: (i, 0, 0), memory_space=pltpu.SMEM)
    smem_next = pl.BlockSpec((1, 1, rb), lambda i, *_: (jnp.minimum(i + 1, nb - 1), 0, 0),
                             memory_space=pltpu.SMEM)
    smem_cur = pl.BlockSpec((1, 1, rb), lambda i, *_: (i + 1, 0, 0), memory_space=pltpu.SMEM)
    hbm = pl.BlockSpec(memory_space=pl.ANY)
    grid_spec = pltpu.PrefetchScalarGridSpec(
        num_scalar_prefetch=5,
        grid=(nb,),
        in_specs=[smem_rows, smem_next, smem_rows, smem_cur, hbm, hbm, hbm, hbm],
        out_specs=hbm,
        scratch_shapes=[pltpu.VMEM((2, rb * seg, LANE), BF16), pltpu.VMEM((2, rb * seg, LANE), BF16),
                        pltpu.VMEM((rb * seg, LANE), F32),
                        pltpu.VMEM((2, d, de), F32), pltpu.VMEM((2, d, de), F32), pltpu.VMEM((2, de, d), F32),
                        pltpu.VMEM((d, de), BF16), pltpu.VMEM((d, de), BF16), pltpu.VMEM((de, d), BF16),
                        pltpu.SemaphoreType.DMA((2,)), pltpu.SemaphoreType.DMA((2,)),
                        pltpu.SemaphoreType.DMA((2,))])
    return pl.pallas_call(
        functools.partial(_moe_kernel, n_real=n_real),
        grid_spec=grid_spec,
        out_shape=jax.ShapeDtypeStruct(((n_real + 2 * rb) * seg, LANE), BF16),
        compiler_params=_cparams(("arbitrary",)),
        name="moe",
    )(bexp, bact, bfirst, bord, bnext, rtok, rtok, rslot, rslot, hn, w_gate, w_up, w_down)


def _dispatch(ids, n, seg):
    k = 2
    m = n * k
    rb = MOE_ROW_BLOCK
    e_flat = ids[:, :k].reshape(m)
    ch = 256
    onehot = (e_flat[:, None] == jnp.arange(N_EXPERTS, dtype=jnp.int32)[None, :]).astype(F32)
    ohc = onehot.reshape(m // ch, ch, N_EXPERTS)
    tri = jnp.tril(jnp.ones((ch, ch), F32))
    within = jnp.einsum('ij,bjk->bik', tri, ohc)
    tot = within[:, -1, :]
    before = jnp.cumsum(tot, axis=0) - tot
    rank = (jnp.sum((within + before[:, None, :]) * ohc, axis=-1).reshape(m) - 1.0).astype(jnp.int32)
    counts = (before[-1] + tot[-1]).astype(jnp.int32)
    padded = (counts + rb - 1) // rb * rb
    pad_end = jnp.cumsum(padded)
    pad_start = pad_end - padded
    dest = jnp.sum(onehot * pad_start.astype(F32)[None, :], axis=-1).astype(jnp.int32) + rank
    nb = (m + N_EXPERTS * (rb - 1) + rb - 1) // rb
    p = nb * rb
    row_m = jnp.full((p,), -1, jnp.int32).at[dest].set(jnp.arange(m, dtype=jnp.int32))
    real = row_m >= 0
    rtok = jnp.where(real, row_m // k, 0)
    pidx = jnp.arange(p, dtype=jnp.int32)
    rslot = jnp.where(real, (row_m % k) * n + row_m // k, m + ((pidx // rb) % 2) * rb + pidx % rb)
    starts = jnp.arange(nb, dtype=jnp.int32) * rb
    bexp = jnp.minimum(jnp.sum((pad_end[None, :] <= starts[:, None]).astype(jnp.int32), axis=1),
                       N_EXPERTS - 1)
    bact = jnp.sum(real.reshape(nb, rb).astype(jnp.int32), axis=1)
    last_e = jnp.max(jnp.where(bact > 0, bexp, 0))
    bexp = jnp.where(bact > 0, bexp, last_e)
    rslot = jnp.concatenate([m + rb + jnp.arange(rb, dtype=jnp.int32), rslot])
    return (bexp, bact, (rtok * seg).reshape(nb, 1, rb), (rslot * seg).reshape(nb + 1, 1, rb),
            m)


def _combine_kernel(h_ref, y0_ref, y1_ref, wts_ref, fw_ref, o_ref, tmp_ref):
    tm = h_ref.shape[0]
    wts = wts_ref[...]
    y0 = _load_token_major(y0_ref, tm, tmp_ref)
    y1 = _load_token_major(y1_ref, tm, tmp_ref)
    moe = y0 * wts[:, 0:1] + y1 * wts[:, 1:2]
    h = h_ref[...] + moe
    o_ref[...] = h * lax.rsqrt(jnp.mean(h * h, axis=1, keepdims=True) + EPS) * fw_ref[...]


def _combine(h1, y, wts, final_w):
    n, d = h1.shape
    seg = d // LANE
    tm = min(512, n)
    nt = n // tm
    return pl.pallas_call(
        _combine_kernel,
        grid=(nt,),
        in_specs=[pl.BlockSpec((tm, d), lambda i: (i, 0)),
                  pl.BlockSpec((tm * seg, LANE), lambda i: (i, 0)),
                  pl.BlockSpec((tm * seg, LANE), lambda i: (i + nt, 0)),
                  pl.BlockSpec((tm, LANE), lambda i: (i, 0)),
                  pl.BlockSpec((1, d), lambda i: (0, 0))],
        out_specs=pl.BlockSpec((tm, d), lambda i: (i, 0)),
        out_shape=jax.ShapeDtypeStruct((n, d), F32),
        scratch_shapes=[pltpu.VMEM((tm * seg, LANE), F32)],
        compiler_params=_cparams(("parallel",)),
        name="combine",
    )(h1, y, y, wts, final_w.reshape(1, d))


def _rope_tables(positions):
    half = ROT_DIM // 2
    inv_freq = ROPE_THETA ** (-jnp.arange(0, ROT_DIM, 2, dtype=F32) / ROT_DIM)
    ang = positions.astype(F32)[..., None] * inv_freq
    cos, sin = jnp.cos(ang), jnp.sin(ang)
    b, t = positions.shape
    ones = jnp.ones((b, t, LANE - ROT_DIM), F32)
    zeros = jnp.zeros((b, t, LANE - half), F32)
    c = jnp.concatenate([cos, cos, ones], axis=-1)
    sa = jnp.concatenate([-sin, zeros], axis=-1)
    sb = jnp.concatenate([jnp.zeros((b, t, half), F32), sin, zeros[..., :LANE - ROT_DIM]], axis=-1)
    return c, sa, sb


def _arrange_w_in(w_in):
    d = w_in.shape[0]
    sizes = (N_HEADS_NSA * HEAD_DIM, 3 * 2 * N_KV_NSA * HEAD_DIM, 3 * N_HEADS_NSA,
             3 * N_HEADS_GDN * HEAD_DIM, N_HEADS_GDN, N_HEADS_GDN, N_HEADS_GDN * HEAD_DIM)
    offs = np.cumsum((0,) + sizes)
    seg = [w_in[:, offs[i]:offs[i + 1]].astype(BF16) for i in range(len(sizes))]
    q, kv, gate, gqkv, gb, ga, gz = seg
    used = sum(sizes)
    pad = jnp.zeros((d, N_CB * LANE - used), BF16)
    return jnp.concatenate([q, kv, gqkv, gz, gate, gb, ga, pad], axis=1)


def kernel(x, positions, attn_norm_w, w_in, cmp_wk, cmp_pek, cmp_wv, cmp_pev, gdn_conv_w, gdn_a_log,
           gdn_dt_bias, gdn_norm_w, w_out, ffn_norm_w, router_group_w, router_group_b,
           router_expert_w, router_expert_b, moe_w_gate, moe_w_up, moe_w_down, final_norm_w):
    b, t, d = x.shape
    n = b * t
    tabs = _rope_tables(positions)
    h = x.reshape(n, d)
    assert w_in.shape[0] == 1, "single-layer block only"
    for l in range(1):
        proj, small = _in_proj(h, attn_norm_w[l], _arrange_w_in(w_in[l]))
        proj3 = proj.reshape(b, t, N_CB * LANE)
        small3 = small.reshape(b, t, LANE)
        cmp_w = jnp.stack([cmp_wk[l], cmp_wv[l]])
        cmp_pe = jnp.stack([cmp_pek[l], cmp_pev[l]])
        kvc = _nsa_compress(proj3, tabs, cmp_w, cmp_pe)
        o_c, selbias = _nsa_cmp_attn(proj3, small3, tabs, kvc)
        o_s = _nsa_sel_attn(proj3, small3, tabs, selbias)
        o_w = _nsa_win_attn(proj3, small3, tabs)
        o_b = _gdn(proj3, small3, gdn_conv_w[l], gdn_a_log[l], gdn_dt_bias[l], gdn_norm_w[l])
        half = N_HEADS_NSA * HEAD_DIM
        wr = jnp.concatenate([router_group_w[l], router_expert_w[l],
                              jnp.zeros((d, LANE - N_GROUPS - N_EXPERTS), F32)], axis=1)
        br = jnp.concatenate([router_group_b[l], router_expert_b[l],
                              jnp.zeros((LANE - N_GROUPS - N_EXPERTS,), F32)]).reshape(1, LANE)
        h1, hn2, ids, wts = _out_proj(o_c.reshape(n, half), o_s.reshape(n, half), o_w.reshape(n, half),
                                      o_b.reshape(n, half), h, w_out[l].astype(BF16), ffn_norm_w[l], wr, br)
        bexp, bact, rtok, rslot, n_slots = _dispatch(ids, n, d // LANE)
        y = _moe(hn2, bexp, bact, rtok, rslot, moe_w_gate[l], moe_w_up[l], moe_w_down[l], n_slots)
        out = _combine(h1, y, wts, final_norm_w)
    return out.reshape(b, t, d)
```

```python
import functools

import numpy as np
import jax
import jax.numpy as jnp
from jax import lax
from jax.experimental import pallas as pl
from jax.experimental.pallas import tpu as pltpu

F32 = jnp.float32
BF16 = jnp.bfloat16

HEAD_DIM = 128
N_HEADS_NSA = 8
N_KV_NSA = 2
N_REP = N_HEADS_NSA // N_KV_NSA
N_HEADS_GDN = 8
ROT_DIM = 32
ROPE_THETA = 500000.0
CMP_LEN = 32
CMP_STRIDE = 16
SLC_LEN = 64
SLC_TOP = 16
WINDOW = 512
CONV_WIDTH = 4
GDN_CHUNK = 64
N_GROUPS = 8
EXPERTS_PER_GROUP = 8
N_EXPERTS = 64
MOE_ROW_BLOCK = 256
EPS = 1e-6
LANE = 128

CB_Q = 0
CB_KV = 8
CB_GQKV = 20
CB_Z = 44
CB_SMALL = 52
N_CB = 54
SMALL_BETA = 24
SMALL_DECAY = 32

NEG_BIAS = -32768.0
VMEM_LIMIT = 56 * 1024 * 1024


def _cparams(sem):
    return pltpu.CompilerParams(dimension_semantics=sem, vmem_limit_bytes=VMEM_LIMIT)


def _mm(a, b):
    return jnp.dot(a.astype(BF16), b.astype(BF16), preferred_element_type=F32)


def _mm_nt(a, b):
    return lax.dot_general(a.astype(BF16), b.astype(BF16), (((1,), (1,)), ((), ())),
                           preferred_element_type=F32)


def _rope(x, c, sa, sb):
    return (x * c + pltpu.roll(x, LANE - ROT_DIM // 2, 1) * sa
            + pltpu.roll(x, ROT_DIM // 2, 1) * sb)


def _sigmoid(x):
    return 0.5 * jnp.tanh(0.5 * x) + 0.5


def _lane_tiles(x):
    return [x[:, i:i + LANE] for i in range(0, x.shape[1], LANE)]


def _store_token_major(ref, x, tmp_ref):
    rows, d = x.shape
    seg = d // LANE
    for s in range(seg):
        tmp_ref[pl.ds(s, rows, stride=seg), :] = x[:, s * LANE:(s + 1) * LANE]
    ref[...] = tmp_ref[...].astype(ref.dtype)


def _load_token_major(ref, rows, tmp_ref):
    seg = ref.shape[0] // rows
    tmp_ref[...] = ref[...].astype(F32)
    return jnp.concatenate([tmp_ref[pl.ds(s, rows, stride=seg), :] for s in range(seg)], axis=1)


def _rowmax(x):
    return jnp.max(functools.reduce(jnp.maximum, _lane_tiles(x)), axis=1, keepdims=True)


def _rowsum(x):
    return jnp.sum(functools.reduce(jnp.add, _lane_tiles(x)), axis=1, keepdims=True)


def _in_proj_kernel(x_ref, nw_ref, w_ref, o_ref, small_ref, hn_ref, *, small_off):
    j = pl.program_id(1)

    @pl.when(j == 0)
    def _():
        x = x_ref[...]
        ms = jnp.mean(x * x, axis=-1, keepdims=True)
        hn_ref[...] = (x * lax.rsqrt(ms + EPS) * nw_ref[...]).astype(BF16)
    acc = jnp.dot(hn_ref[...], w_ref[...], preferred_element_type=F32)
    o_ref[...] = acc.astype(o_ref.dtype)

    @pl.when(j == pl.num_programs(1) - 1)
    def _():
        small_ref[...] = acc[:, small_off:small_off + LANE]


def _in_proj(x2, norm_w, w):
    n, d = x2.shape
    ncol = w.shape[1]
    tm = min(1024, n)
    tn = 2304
    small_off = CB_SMALL * LANE - (ncol // tn - 1) * tn
    assert 0 <= small_off <= tn - LANE
    return pl.pallas_call(
        functools.partial(_in_proj_kernel, small_off=small_off),
        grid=(n // tm, ncol // tn),
        in_specs=[pl.BlockSpec((tm, d), lambda i, j: (i, 0)),
                  pl.BlockSpec((1, d), lambda i, j: (0, 0)),
                  pl.BlockSpec((d, tn), lambda i, j: (0, j))],
        out_specs=[pl.BlockSpec((tm, tn), lambda i, j: (i, j)),
                   pl.BlockSpec((tm, LANE), lambda i, j: (i, 0))],
        out_shape=[jax.ShapeDtypeStruct((n, ncol), BF16), jax.ShapeDtypeStruct((n, LANE), F32)],
        scratch_shapes=[pltpu.VMEM((tm, d), BF16)],
        compiler_params=_cparams(("parallel", "arbitrary")),
        name="in_proj",
    )(x2, norm_w.reshape(1, d), w)


def _compress_kernel(a_ref, c_ref, sa_ref, sb_ref, w_ref, pe_ref, o_ref, xs_ref):
    kv = pl.program_id(1)
    t = a_ref.shape[1]
    nc = t // CMP_STRIDE
    x = a_ref[0].astype(F32)
    xr = _rope(x, c_ref[0], sa_ref[0], sb_ref[0])
    x = jnp.where(kv == 0, xr, x)
    xs_ref[pl.ds(0, t), :] = x
    xs_ref[pl.ds(t, CMP_STRIDE), :] = jnp.zeros((CMP_STRIDE, LANE), F32)
    acc = jnp.zeros((nc, LANE), F32)
    for l in range(CMP_LEN):
        rows = xs_ref[pl.ds(l, nc, stride=CMP_STRIDE), :] + pe_ref[0, pl.ds(l, 1), :]
        acc = acc + _mm(rows, w_ref[0, l])
    o_ref[0, 0, 0] = acc.astype(BF16)


def _nsa_compress(proj3, tabs, cmp_w, cmp_pe):
    b, t, _ = proj3.shape
    g = N_KV_NSA
    nc = t // CMP_STRIDE
    tab_spec = pl.BlockSpec((1, t, LANE), lambda bi, kv, gi: (bi, 0, 0))
    return pl.pallas_call(
        _compress_kernel,
        grid=(b, 2, g),
        in_specs=[pl.BlockSpec((1, t, LANE), lambda bi, kv, gi: (bi, 0, CB_KV + kv * g + gi)),
                  tab_spec, tab_spec, tab_spec,
                  pl.BlockSpec((1, CMP_LEN, LANE, LANE), lambda bi, kv, gi: (kv, 0, 0, 0)),
                  pl.BlockSpec((1, CMP_LEN, LANE), lambda bi, kv, gi: (kv, 0, 0))],
        out_specs=pl.BlockSpec((1, 1, 1, nc, LANE), lambda bi, kv, gi: (bi, kv, gi, 0, 0)),
        out_shape=jax.ShapeDtypeStruct((b, 2, g, nc, LANE), BF16),
        scratch_shapes=[pltpu.VMEM((t + CMP_STRIDE, LANE), F32)],
        compiler_params=_cparams(("parallel", "arbitrary", "arbitrary")),
        name="nsa_compress",
    )(proj3, *tabs, cmp_w, cmp_pe)


def _load_q(q_ref, c, sa, sb):
    scale = HEAD_DIM ** -0.5
    qs = [(_rope(q_ref[0, :, r * LANE:(r + 1) * LANE].astype(F32), c, sa, sb) * scale).astype(BF16)
          for r in range(N_REP)]
    return jnp.concatenate(qs, axis=0)


def _store_gated(o_ref, o, gate_ref, g, branch, tq):
    gt = gate_ref[0]
    for r in range(N_REP):
        col = ((g * N_REP + r) * 3 + branch)
        lane = lax.broadcasted_iota(jnp.int32, gt.shape, 1)
        gcol = jnp.sum(jnp.where(lane == col, gt, 0.0), axis=1, keepdims=True)
        o_ref[0, :, r * LANE:(r + 1) * LANE] = (o[r * tq:(r + 1) * tq] * _sigmoid(gcol)).astype(o_ref.dtype)


Q_TILES_PER_STEP = 4


def _row_views(refs, sub, tq):
    rows = pl.ds(sub * tq, tq)
    return [r.at[:, rows, :] if len(r.shape) == 3 else r.at[:, :, rows, :] for r in refs]


def _cmp_attn_kernel(q_ref, c_ref, sa_ref, sb_ref, kc_ref, vc_ref, gate_ref, o_ref, sel_ref, *, n_slc):
    tq = q_ref.shape[1] // Q_TILES_PER_STEP
    for sub in range(Q_TILES_PER_STEP):
        q, c, sa, sb, gate, o, sel = _row_views(
            (q_ref, c_ref, sa_ref, sb_ref, gate_ref, o_ref, sel_ref), sub, tq)
        _cmp_attn_tile(pl.program_id(2) * Q_TILES_PER_STEP + sub, q, c, sa, sb, kc_ref, vc_ref, gate, o,
                       sel, n_slc=n_slc)


def _cmp_attn_tile(i, q_ref, c_ref, sa_ref, sb_ref, kc_ref, vc_ref, gate_ref, o_ref, sel_ref, *, n_slc):
    g = pl.program_id(1)
    tq = q_ref.shape[1]
    nc = kc_ref.shape[3]
    q4 = _load_q(q_ref, c_ref[0], sa_ref[0], sb_ref[0])
    s = _mm_nt(q4, kc_ref[0, 0, 0])
    row = lax.broadcasted_iota(jnp.int32, s.shape, 0)
    n = lax.broadcasted_iota(jnp.int32, s.shape, 1)
    tpos = i * tq + (row & (tq - 1))
    mask = (n * CMP_STRIDE + (CMP_LEN - 1) <= tpos) & (n < nc - 1)
    sm = jnp.where(mask, s, -1e30)
    m = _rowmax(sm)
    p = jnp.where(mask, jnp.exp(sm - m), 0.0)
    l = _rowsum(p)
    p = p / jnp.maximum(l, 1e-30)
    o = _mm(p, vc_ref[0, 0, 0])
    _store_gated(o_ref, o, gate_ref, g, 0, tq)

    ps = p[0:tq]
    for r in range(1, N_REP):
        ps = ps + p[r * tq:(r + 1) * tq]
    cn = lax.broadcasted_iota(jnp.int32, (nc, LANE), 0)
    cj = lax.broadcasted_iota(jnp.int32, (nc, LANE), 1)
    ratio = SLC_LEN // CMP_STRIDE
    agg = ((cn >= ratio * cj - (CMP_LEN // CMP_STRIDE - 1)) & (cn < ratio * cj + ratio)
           & (cn < nc - 1) & (cj < n_slc))
    agg = jnp.where(agg, 1.0, 0.0).astype(BF16)
    ps_hi = ps.astype(BF16)
    ps_lo = (ps - ps_hi.astype(F32)).astype(BF16)
    imp = (jnp.dot(ps_hi, agg, preferred_element_type=F32)
           + jnp.dot(ps_lo, agg, preferred_element_type=F32))

    j = lax.broadcasted_iota(jnp.int32, (tq, LANE), 1)
    tt = i * tq + lax.broadcasted_iota(jnp.int32, (tq, LANE), 0)
    cur = tt // SLC_LEN
    valid = j <= cur
    forced = (j == 0) | (j == cur) | (j == cur - 1)
    vals = jnp.where(forced, 1e30, jnp.where(valid, imp, -1.0))
    vt = vals.T
    jb = lax.broadcasted_iota(jnp.int32, (n_slc, tq), 0)
    vb = vt[0:n_slc]
    cnt = jnp.zeros((n_slc, tq), F32)
    for jp in range(n_slc):
        cand = vt[jp:jp + 1, :]
        ge = jnp.where(cand >= vb, 1.0, 0.0)
        gt = jnp.where(cand > vb, 1.0, 0.0)
        cnt = cnt + jnp.where(jb > jp, ge, gt)
    keep = jnp.where(cnt < float(min(SLC_TOP, n_slc)), 0.0, NEG_BIAS)
    if n_slc < LANE:
        keep = jnp.concatenate([keep, jnp.full((LANE - n_slc, tq), NEG_BIAS, F32)], axis=0)
    sel_ref[0, 0] = jnp.where(valid, keep.T, NEG_BIAS).astype(BF16)


def _nsa_cmp_attn(proj3, small3, tabs, kvc, tq=128):
    b, t, _ = proj3.shape
    g = N_KV_NSA
    nc = t // CMP_STRIDE
    n_slc = t // SLC_LEN
    rw = N_REP * LANE
    bq = tq * Q_TILES_PER_STEP
    tab_spec = pl.BlockSpec((1, bq, LANE), lambda bi, gi, i: (bi, i, 0))
    return pl.pallas_call(
        functools.partial(_cmp_attn_kernel, n_slc=n_slc),
        grid=(b, g, t // bq),
        in_specs=[pl.BlockSpec((1, bq, rw), lambda bi, gi, i: (bi, i, gi)),
                  tab_spec, tab_spec, tab_spec,
                  pl.BlockSpec((1, 1, 1, nc, LANE), lambda bi, gi, i: (bi, 0, gi, 0, 0)),
                  pl.BlockSpec((1, 1, 1, nc, LANE), lambda bi, gi, i: (bi, 1, gi, 0, 0)),
                  pl.BlockSpec((1, bq, LANE), lambda bi, gi, i: (bi, i, 0))],
        out_specs=[pl.BlockSpec((1, bq, rw), lambda bi, gi, i: (bi, i, gi)),
                   pl.BlockSpec((1, 1, bq, LANE), lambda bi, gi, i: (bi, gi, i, 0))],
        out_shape=[jax.ShapeDtypeStruct((b, t, N_HEADS_NSA * LANE), BF16),
                   jax.ShapeDtypeStruct((b, g, t, LANE), BF16)],
        compiler_params=_cparams(("parallel", "parallel", "parallel")),
        name="nsa_cmp",
    )(proj3, *tabs, kvc, kvc, small3)


def _sel_attn_kernel(q_ref, c_ref, sa_ref, sb_ref, cf_ref, saf_ref, sbf_ref, k_ref, v_ref,
                     sel_ref, gate_ref, o_ref, *scratch, tk):
    tq = q_ref.shape[1] // Q_TILES_PER_STEP
    for sub in range(Q_TILES_PER_STEP):
        q, c, sa, sb, sel, gate, o = _row_views(
            (q_ref, c_ref, sa_ref, sb_ref, sel_ref, gate_ref, o_ref), sub, tq)
        _sel_attn_tile(pl.program_id(2) * Q_TILES_PER_STEP + sub, q, c, sa, sb, cf_ref, saf_ref, sbf_ref,
                       k_ref, v_ref, sel, gate, o, *scratch, tk=tk)


def _sel_attn_tile(i, q_ref, c_ref, sa_ref, sb_ref, cf_ref, saf_ref, sbf_ref, k_ref, v_ref,
                   sel_ref, gate_ref, o_ref, ka_ref, vs_ref, m_ref, l_ref, acc_ref, s0_ref, s1_ref,
                   *, tk):
    g = pl.program_id(1)
    tq = q_ref.shape[1]
    t = k_ref.shape[1]
    rows = N_REP * tq

    @pl.when(i == 0)
    def _():
        kr = _rope(k_ref[0].astype(F32), cf_ref[0], saf_ref[0], sbf_ref[0])
        ka_ref[:, 0:LANE] = kr.astype(BF16)
        pos = lax.broadcasted_iota(jnp.int32, (t, LANE), 0)
        lane = lax.broadcasted_iota(jnp.int32, (t, LANE), 1)
        ka_ref[:, LANE:2 * LANE] = jnp.where(lane == pos // SLC_LEN, 1.0, 0.0).astype(BF16)
        vs_ref[...] = v_ref[0].astype(BF16)

    q4 = _load_q(q_ref, c_ref[0], sa_ref[0], sb_ref[0])
    bias = sel_ref[0, 0]
    qa = jnp.concatenate([q4, jnp.concatenate([bias] * N_REP, axis=0)], axis=1)

    m_ref[...] = jnp.full((rows, LANE), -1e30, F32)
    l_ref[...] = jnp.zeros((rows, LANE), F32)
    acc_ref[...] = jnp.zeros((rows, LANE), F32)

    half = tk // 2

    def scores(kt, hlf):
        k0 = pl.multiple_of(kt * tk + hlf * half, half)
        return _mm_nt(qa, ka_ref[pl.ds(k0, half), :])

    def update(s_ref, kt, hlf, causal):
        k0 = pl.multiple_of(kt * tk + hlf * half, half)
        s = s_ref[...]
        if causal:
            r = lax.broadcasted_iota(jnp.int32, s.shape, 0)
            kp = k0 + lax.broadcasted_iota(jnp.int32, s.shape, 1)
            s = jnp.where(kp <= i * tq + (r & (tq - 1)), s, -1e30)
        tiles = _lane_tiles(s)
        m_old = m_ref[...]
        m_new = jnp.maximum(m_old, _rowmax(s))
        alpha = jnp.exp(m_old - m_new)
        ps = [jnp.exp(tl - m_new) for tl in tiles]
        l_ref[...] = alpha * l_ref[...] + jnp.sum(functools.reduce(jnp.add, ps), axis=1, keepdims=True)
        p = jnp.concatenate([x.astype(BF16) for x in ps], axis=1)
        acc_ref[...] = alpha * acc_ref[...] + jnp.dot(p, vs_ref[pl.ds(k0, half), :],
                                                      preferred_element_type=F32)
        m_ref[...] = m_new

    diag = (i * tq) // tk
    s0_ref[...] = scores(0, 0)

    def body(kt, carry):
        s1_ref[...] = scores(kt, 1)
        update(s0_ref, kt, 0, False)
        s0_ref[...] = scores(kt + 1, 0)
        update(s1_ref, kt, 1, False)
        return carry

    def body2(j, carry):
        body(2 * j, carry)
        body(2 * j + 1, carry)
        return carry

    lax.fori_loop(0, diag // 2, body2, 0)
    lax.fori_loop((diag // 2) * 2, diag, body, 0)
    s1_ref[...] = scores(diag, 1)
    update(s0_ref, diag, 0, True)
    update(s1_ref, diag, 1, True)
    o = acc_ref[...] / l_ref[...]
    _store_gated(o_ref, o, gate_ref, g, 1, tq)


def _nsa_sel_attn(proj3, small3, tabs, selbias, tq=128, tk=512):
    b, t, _ = proj3.shape
    g = N_KV_NSA
    rw = N_REP * LANE
    bq = tq * Q_TILES_PER_STEP
    tk = min(tk, t)
    tab_spec = pl.BlockSpec((1, bq, LANE), lambda bi, gi, i: (bi, i, 0))
    tabf_spec = pl.BlockSpec((1, t, LANE), lambda bi, gi, i: (bi, 0, 0))
    kcb = CB_KV + 1 * 2 * g
    return pl.pallas_call(
        functools.partial(_sel_attn_kernel, tk=tk),
        grid=(b, g, t // bq),
        in_specs=[pl.BlockSpec((1, bq, rw), lambda bi, gi, i: (bi, i, gi)),
                  tab_spec, tab_spec, tab_spec, tabf_spec, tabf_spec, tabf_spec,
                  pl.BlockSpec((1, t, LANE), lambda bi, gi, i: (bi, 0, kcb + gi)),
                  pl.BlockSpec((1, t, LANE), lambda bi, gi, i: (bi, 0, kcb + g + gi)),
                  pl.BlockSpec((1, 1, bq, LANE), lambda bi, gi, i: (bi, gi, i, 0)),
                  pl.BlockSpec((1, bq, LANE), lambda bi, gi, i: (bi, i, 0))],
        out_specs=pl.BlockSpec((1, bq, rw), lambda bi, gi, i: (bi, i, gi)),
        out_shape=jax.ShapeDtypeStruct((b, t, N_HEADS_NSA * LANE), BF16),
        scratch_shapes=[pltpu.VMEM((t, 2 * LANE), BF16), pltpu.VMEM((t, LANE), BF16),
                        pltpu.VMEM((N_REP * tq, LANE), F32), pltpu.VMEM((N_REP * tq, LANE), F32),
                        pltpu.VMEM((N_REP * tq, LANE), F32),
                        pltpu.VMEM((N_REP * tq, tk // 2), F32), pltpu.VMEM((N_REP * tq, tk // 2), F32)],
        compiler_params=_cparams(("parallel", "parallel", "arbitrary")),
        name="nsa_sel",
    )(proj3, *tabs, *tabs, proj3, proj3, selbias, small3)


def _win_attn_kernel(q_ref, c_ref, sa_ref, sb_ref, cf_ref, saf_ref, sbf_ref, k_ref, v_ref,
                     gate_ref, o_ref, ks_ref, vs_ref, *, span):
    tq = q_ref.shape[1] // Q_TILES_PER_STEP
    for sub in range(Q_TILES_PER_STEP):
        q, c, sa, sb, gate, o = _row_views((q_ref, c_ref, sa_ref, sb_ref, gate_ref, o_ref), sub, tq)
        _win_attn_tile(pl.program_id(2) * Q_TILES_PER_STEP + sub, q, c, sa, sb, cf_ref, saf_ref, sbf_ref,
                       k_ref, v_ref, gate, o, ks_ref, vs_ref, span=span)


def _win_attn_tile(i, q_ref, c_ref, sa_ref, sb_ref, cf_ref, saf_ref, sbf_ref, k_ref, v_ref,
                   gate_ref, o_ref, ks_ref, vs_ref, *, span):
    g = pl.program_id(1)
    tq = q_ref.shape[1]

    t = k_ref.shape[1]

    @pl.when(i == 0)
    def _():
        zeros = jnp.zeros((WINDOW, LANE), BF16)
        ks_ref[pl.ds(0, WINDOW), :] = zeros
        vs_ref[pl.ds(0, WINDOW), :] = zeros
        ks_ref[pl.ds(WINDOW, t), :] = _rope(k_ref[0].astype(F32), cf_ref[0], saf_ref[0],
                                            sbf_ref[0]).astype(BF16)
        vs_ref[pl.ds(WINDOW, t), :] = v_ref[0].astype(BF16)

    q4 = _load_q(q_ref, c_ref[0], sa_ref[0], sb_ref[0])
    k0 = pl.multiple_of(i * tq, tq)
    s = _mm_nt(q4, ks_ref[pl.ds(k0, span), :])
    tiles = _lane_tiles(s)
    r = lax.broadcasted_iota(jnp.int32, tiles[0].shape, 0) & (tq - 1)
    c = lax.broadcasted_iota(jnp.int32, tiles[0].shape, 1)
    first_block = WINDOW // tq - i
    masked = []
    for b, tl in enumerate(tiles):
        if b == 0:
            tl = jnp.where(c > r, tl, -1e30)
        if b == len(tiles) - 1:
            tl = jnp.where(c <= r, tl, -1e30)
        else:
            tl = jnp.where(b >= first_block, tl, -1e30)
        masked.append(tl)
    m = jnp.max(functools.reduce(jnp.maximum, masked), axis=1, keepdims=True)
    ps = [jnp.exp(tl - m) for tl in masked]
    l = jnp.sum(functools.reduce(jnp.add, ps), axis=1, keepdims=True)
    p = jnp.concatenate([x.astype(BF16) for x in ps], axis=1)
    o = jnp.dot(p, vs_ref[pl.ds(k0, span), :], preferred_element_type=F32) / l
    _store_gated(o_ref, o, gate_ref, g, 2, tq)


def _nsa_win_attn(proj3, small3, tabs, tq=128):
    b, t, _ = proj3.shape
    g = N_KV_NSA
    rw = N_REP * LANE
    bq = tq * Q_TILES_PER_STEP
    span = WINDOW + tq
    assert tq == LANE
    tab_spec = pl.BlockSpec((1, bq, LANE), lambda bi, gi, i: (bi, i, 0))
    tabf_spec = pl.BlockSpec((1, t, LANE), lambda bi, gi, i: (bi, 0, 0))
    kcb = CB_KV + 2 * 2 * g
    return pl.pallas_call(
        functools.partial(_win_attn_kernel, span=span),
        grid=(b, g, t // bq),
        in_specs=[pl.BlockSpec((1, bq, rw), lambda bi, gi, i: (bi, i, gi)),
                  tab_spec, tab_spec, tab_spec, tabf_spec, tabf_spec, tabf_spec,
                  pl.BlockSpec((1, t, LANE), lambda bi, gi, i: (bi, 0, kcb + gi)),
                  pl.BlockSpec((1, t, LANE), lambda bi, gi, i: (bi, 0, kcb + g + gi)),
                  pl.BlockSpec((1, bq, LANE), lambda bi, gi, i: (bi, i, 0))],
        out_specs=pl.BlockSpec((1, bq, rw), lambda bi, gi, i: (bi, i, gi)),
        out_shape=jax.ShapeDtypeStruct((b, t, N_HEADS_NSA * LANE), BF16),
        scratch_shapes=[pltpu.VMEM((t + WINDOW, LANE), BF16), pltpu.VMEM((t + WINDOW, LANE), BF16)],
        compiler_params=_cparams(("parallel", "parallel", "arbitrary")),
        name="nsa_win",
    )(proj3, *tabs, *tabs, proj3, proj3, small3)


def _gdn_kernel(q_ref, k_ref, v_ref, z_ref, braw_ref, araw_ref, cwq_ref, cwk_ref, cwv_ref, alog_ref,
                dtb_ref, nw_ref, o_ref, q_s, k_s, v_s, o_s, qp_s, op_s, xp_s, bd_s, gcd_s, gl_s, mm_s,
                nn_s):
    t = q_ref.shape[1]
    c = GDN_CHUNK
    nchunk = t // c
    xp_s[pl.ds(0, 8), :] = jnp.zeros((8, LANE), F32)

    def conv_silu(x, cw):
        xp_s[pl.ds(8, t), :] = x
        y = x * cw[CONV_WIDTH - 1:CONV_WIDTH]
        for sft in range(1, CONV_WIDTH):
            y = y + xp_s[pl.ds(8 - sft, t), :] * cw[CONV_WIDTH - 1 - sft:CONV_WIDTH - sft]
        return y * _sigmoid(y)

    def l2n(x):
        return x * lax.rsqrt(jnp.sum(x * x, axis=1, keepdims=True) + EPS)

    q_s[...] = l2n(conv_silu(q_ref[0].astype(F32), cwq_ref[...])) * (HEAD_DIM ** -0.5)
    k_s[...] = l2n(conv_silu(k_ref[0].astype(F32), cwk_ref[...]))
    v_s[...] = conv_silu(v_ref[0].astype(F32), cwv_ref[...])

    bd_s[...] = _sigmoid(braw_ref[0, 0])
    xa = araw_ref[0, 0] + dtb_ref[0]
    softplus = jnp.maximum(xa, 0.0) + jnp.log(1.0 + jnp.exp(-jnp.abs(xa)))
    gc = -jnp.exp(alog_ref[0]) * softplus
    lane = lax.broadcasted_iota(jnp.int32, gc.shape, 1)
    sft = 1
    while sft < c:
        gc = gc + jnp.where((lane & (c - 1)) >= sft, pltpu.roll(gc, sft, 1), 0.0)
        sft *= 2
    gcd_s[...] = gc

    c2 = 2 * c
    ci = lax.broadcasted_iota(jnp.int32, (c2, c2), 0)
    cj = lax.broadcasted_iota(jnp.int32, (c2, c2), 1)
    same = (ci // c) == (cj // c)
    tril = same & (cj <= ci)
    strict = same & (cj < ci)
    eye = jnp.where(ci == cj, 1.0, 0.0)
    first = ci < c

    pairs_per_iter = 8
    rng = range(pairs_per_iter)

    def prep(it, carry):
        n2 = [it * pairs_per_iter + p for p in rng]
        sl = [pl.ds(pl.multiple_of(n * c2, c2), c2) for n in n2]
        grow = [gcd_s[pl.ds(n, 1), :] for n in n2]
        gc2 = [jnp.broadcast_to(g_, (c2, c2)).T for g_ in grow]
        betac = [jnp.broadcast_to(bd_s[pl.ds(n, 1), :], (c2, c2)).T for n in n2]
        kn = [k_s[s_, :] for s_ in sl]
        kbn = [kn[p] * betac[p] for p in rng]
        decay = [jnp.exp(jnp.where(tril, gc2[p] - grow[p], -1e30)) for p in rng]
        kk = [_mm_nt(kbn[p], kn[p]) for p in rng]
        y = [-jnp.where(strict, kk[p] * decay[p], 0.0) for p in rng]
        pm = [eye + y_ for y_ in y]
        pw = 1
        while 2 * pw < c:
            y = [_mm(y_, y_) for y_ in y]
            pm = [pm[p] + _mm(pm[p], y[p]) for p in rng]
            pw *= 2
        egn = [jnp.exp(g_) for g_ in gc2]
        uw = [_mm(pm[p], jnp.concatenate([v_s[sl[p], :] * betac[p], kbn[p] * egn[p]], axis=1))
              for p in rng]
        qn = [q_s[s_, :] for s_ in sl]
        qk = [_mm_nt(qn[p], kn[p]) for p in rng]
        attn = [jnp.where(tril, qk[p] * decay[p], 0.0) for p in rng]
        auw = [_mm(attn[p], uw[p]) for p in rng]
        for p in rng:
            qp_s[sl[p], :] = qn[p] * egn[p] - auw[p][:, LANE:]
            op_s[sl[p], :] = auw[p][:, :LANE]
        gl = [jnp.where(first, g_[c - 1:c, :], g_[c2 - 1:c2, :]) for g_ in gc2]
        kdt = [(kn[p] * jnp.exp(gl[p] - gc2[p])).T for p in rng]
        nma = [_mm(kdt[p][:, 0:c], uw[p][0:c]) for p in rng]
        nmb = [_mm(kdt[p][:, c:c2], uw[p][c:c2]) for p in rng]
        mb = [x[:, LANE:] for x in nmb]
        mbnm = [_mm(mb[p], nma[p]) for p in rng]
        ma = [x[:, LANE:] for x in nma]
        na = [x[:, :LANE] for x in nma]
        nb_ = [x[:, :LANE] for x in nmb]
        mba = [x[:, LANE:] for x in mbnm]
        mbn = [x[:, :LANE] for x in mbnm]
        for p in rng:
            ga = jnp.exp(gc2[p][c - 1:c, :])
            gb = jnp.exp(gc2[p][c2 - 1:c2, :])
            mm_s[2 * n2[p]] = ma[p].astype(BF16)
            nn_s[2 * n2[p]] = na[p]
            gl_s[pl.ds(2 * n2[p], 1), :] = ga
            mm_s[2 * n2[p] + 1] = (gb * ma[p] + ga * mb[p] - mba[p]).astype(BF16)
            nn_s[2 * n2[p] + 1] = gb * na[p] - mbn[p] + nb_[p]
            gl_s[pl.ds(2 * n2[p] + 1, 1), :] = ga * gb
        return carry

    lax.fori_loop(0, nchunk // (2 * pairs_per_iter), prep, 0)

    def second_out(n2, s_mid):
        rows = pl.ds(pl.multiple_of(n2 * c2, c2) + c, c)
        o_s[rows, :] = (jnp.dot(qp_s[rows, :].astype(BF16), s_mid.astype(BF16),
                                 preferred_element_type=F32) + op_s[rows, :])

    def scan(n2, carry):
        s, s_mid_prev = carry
        second_out(jnp.maximum(n2 - 1, 0), s_mid_prev)
        ra = pl.ds(pl.multiple_of(n2 * c2, c2), c)
        sb = s.astype(BF16)
        s_mid = (s * gl_s[pl.ds(2 * n2, 1), :] - jnp.dot(mm_s[2 * n2], sb, preferred_element_type=F32)
                 + nn_s[2 * n2])
        s_new = (s * gl_s[pl.ds(2 * n2 + 1, 1), :]
                 - jnp.dot(mm_s[2 * n2 + 1], sb, preferred_element_type=F32) + nn_s[2 * n2 + 1])
        o_s[ra, :] = jnp.dot(qp_s[ra, :].astype(BF16), sb, preferred_element_type=F32) + op_s[ra, :]
        return s_new, s_mid

    zero_state = jnp.zeros((LANE, LANE), F32)
    _, s_mid_last = lax.fori_loop(0, nchunk // 2, scan, (zero_state, zero_state))
    second_out(jnp.int32(nchunk // 2 - 1), s_mid_last)

    o = o_s[...]
    on = o * lax.rsqrt(jnp.mean(o * o, axis=1, keepdims=True) + EPS) * nw_ref[...]
    z = z_ref[0].astype(F32)
    o_ref[0] = (on * (z * _sigmoid(z))).astype(o_ref.dtype)


def _gdn(proj3, small3, conv_w, a_log, dt_bias, norm_w):
    b, t, _ = proj3.shape
    hh = N_HEADS_GDN
    c = GDN_CHUNK
    col = lambda cb: pl.BlockSpec((1, t, LANE), lambda bi, hi: (bi, 0, cb + hi))
    cw = lambda off: pl.BlockSpec((CONV_WIDTH, LANE), lambda bi, hi: (0, off + hi))
    hrow = pl.BlockSpec((1, 1, LANE), lambda bi, hi: (hi, 0, 0))
    alog_b = jnp.broadcast_to(a_log.astype(F32)[:, None, None], (hh, 1, LANE))
    dtb_b = jnp.broadcast_to(dt_bias.astype(F32)[:, None, None], (hh, 1, LANE))
    big = pltpu.VMEM((t, LANE), F32)
    nrow = t // LANE
    ba = small3[:, :, SMALL_BETA:SMALL_BETA + 2 * hh].transpose(0, 2, 1).reshape(b, 2 * hh, nrow, LANE)
    dense = lambda off: pl.BlockSpec((1, 1, nrow, LANE), lambda bi, hi: (bi, off + hi, 0, 0))
    return pl.pallas_call(
        _gdn_kernel,
        grid=(b, hh),
        in_specs=[col(CB_GQKV), col(CB_GQKV + hh), col(CB_GQKV + 2 * hh), col(CB_Z),
                  dense(0), dense(hh),
                  cw(0), cw(hh), cw(2 * hh), hrow, hrow,
                  pl.BlockSpec((1, LANE), lambda bi, hi: (0, 0))],
        out_specs=pl.BlockSpec((1, t, LANE), lambda bi, hi: (bi, 0, hi)),
        out_shape=jax.ShapeDtypeStruct((b, t, hh * LANE), BF16),
        scratch_shapes=[big, big, big, big, big, big, pltpu.VMEM((t + 8, LANE), F32),
                        pltpu.VMEM((nrow, LANE), F32), pltpu.VMEM((nrow, LANE), F32),
                        pltpu.VMEM((t // c, LANE), F32),
                        pltpu.VMEM((t // c, LANE, LANE), BF16),
                        pltpu.VMEM((t // c, LANE, LANE), F32)],
        compiler_params=_cparams(("parallel", "parallel")),
        name="gdn",
    )(proj3, proj3, proj3, proj3, ba, ba, conv_w, conv_w, conv_w, alog_b, dtb_b,
      norm_w.reshape(1, LANE))


def _split3(a):
    hi = a.astype(BF16)
    lo = (a - hi.astype(F32)).astype(BF16)
    return hi, lo


def _out_proj_kernel(oc_ref, os_ref, ow_ref, ob_ref, x_ref, wo_ref, fw_ref, wr_ref, br_ref,
                     h_ref, hn_ref, ids_ref, wts_ref, tmp_ref):
    half = oc_ref.shape[1]
    oa = (oc_ref[...].astype(F32) + os_ref[...].astype(F32) + ow_ref[...].astype(F32)).astype(BF16)
    h1 = (x_ref[...] + jnp.dot(oa, wo_ref[0:half, :], preferred_element_type=F32)
          + jnp.dot(ob_ref[...], wo_ref[half:2 * half, :], preferred_element_type=F32))
    h_ref[...] = h1
    hn = h1 * lax.rsqrt(jnp.mean(h1 * h1, axis=1, keepdims=True) + EPS) * fw_ref[...]
    _store_token_major(hn_ref, hn, tmp_ref)

    a_hi, a_lo = _split3(hn)
    w_hi, w_lo = _split3(wr_ref[...])
    dot = lambda a, b: jnp.dot(a, b, preferred_element_type=F32)
    logits = dot(a_hi, w_hi) + dot(a_hi, w_lo) + dot(a_lo, w_hi) + br_ref[...]
    lane = lax.broadcasted_iota(jnp.int32, logits.shape, 1)
    big = 1e30
    is_g = lane < N_GROUPS
    lg = jnp.where(is_g, logits, -big)
    gm = jnp.max(lg, axis=1, keepdims=True)
    grp = jnp.min(jnp.where(lg == gm, lane, LANE), axis=1, keepdims=True)
    p_grp = 1.0 / jnp.sum(jnp.where(is_g, jnp.exp(lg - gm), 0.0), axis=1, keepdims=True)
    e_id = lane - N_GROUPS
    in_g = (e_id >= 0) & (e_id < N_EXPERTS) & ((e_id // EXPERTS_PER_GROUP) == grp)
    le = jnp.where(in_g, logits, -big)
    em = jnp.max(le, axis=1, keepdims=True)
    pe = jnp.where(in_g, jnp.exp(le - em), 0.0)
    pe = pe / jnp.sum(pe, axis=1, keepdims=True)
    pm = jnp.where(in_g, pe, -1.0)
    p1 = jnp.max(pm, axis=1, keepdims=True)
    i1 = jnp.min(jnp.where(pm == p1, lane, LANE), axis=1, keepdims=True)
    pm2 = jnp.where(lane == i1, -1.0, pm)
    p2 = jnp.max(pm2, axis=1, keepdims=True)
    i2 = jnp.min(jnp.where(pm2 == p2, lane, LANE), axis=1, keepdims=True)
    den = p1 + p2
    ids_ref[...] = jnp.where(lane == 0, i1 - N_GROUPS, jnp.where(lane == 1, i2 - N_GROUPS, 0))
    wts_ref[...] = jnp.where(lane == 0, p1 / den * p_grp, jnp.where(lane == 1, p2 / den * p_grp, 0.0))


def _out_proj(oc, os_, ow, ob, x2, w_out, ffn_w, wr, br):
    n, d = x2.shape
    half = oc.shape[1]
    seg = d // LANE
    tm = min(256, n)
    row = lambda w: pl.BlockSpec((tm, w), lambda i: (i, 0))
    full = lambda a: pl.BlockSpec(a.shape, lambda i: (0,) * a.ndim)
    fw = ffn_w.reshape(1, d)
    return pl.pallas_call(
        _out_proj_kernel,
        grid=(n // tm,),
        in_specs=[row(half), row(half), row(half), row(half), row(d), full(w_out), full(fw),
                  full(wr), full(br)],
        out_specs=[row(d), pl.BlockSpec((tm * seg, LANE), lambda i: (i, 0)), row(LANE), row(LANE)],
        out_shape=[jax.ShapeDtypeStruct((n, d), F32), jax.ShapeDtypeStruct((n * seg, LANE), BF16),
                   jax.ShapeDtypeStruct((n, LANE), jnp.int32), jax.ShapeDtypeStruct((n, LANE), F32)],
        scratch_shapes=[pltpu.VMEM((tm * seg, LANE), F32)],
        compiler_params=_cparams(("parallel",)),
        name="out_proj",
    )(oc, os_, ow, ob, x2, w_out, fw, wr, br)


def _moe_kernel(bexp_ref, bact_ref, bfirst_ref, bord_ref, bnext_ref, rtok_ref, rtokn_ref, rslotp_ref,
                rslot_ref, hn_hbm, wg_hbm, wu_hbm, wd_hbm, y_hbm, xbuf, ybuf, tmp, wg_st, wu_st, wd_st,
                wgb, wub, wdb, gsem, ssem, wsem, *, n_real):
    i = pl.program_id(0)
    nb = pl.num_programs(0)
    rb = rtok_ref.shape[2]
    seg = ybuf.shape[1] // rb
    slot = i & 1
    other = 1 - slot
    active = bact_ref[i] > 0
    prev_active = jnp.logical_and(i > 0, bact_ref[jnp.maximum(i - 1, 0)] > 0)

    def gather_copy(tok, r, buf):
        return pltpu.make_async_copy(hn_hbm.at[pl.ds(pl.multiple_of(tok, seg), seg), :],
                                     xbuf.at[buf, pl.ds(r * seg, seg), :], gsem.at[buf])

    def scatter_copy(dst, r, buf):
        return pltpu.make_async_copy(ybuf.at[buf, pl.ds(r * seg, seg), :],
                                     y_hbm.at[pl.ds(pl.multiple_of(dst, seg), seg), :], ssem.at[buf])

    def wait_gather(buf):
        pltpu.make_async_copy(hn_hbm.at[pl.ds(0, rb * seg), :], xbuf.at[buf], gsem.at[buf]).wait()

    def wait_scatter(buf):
        pltpu.make_async_copy(ybuf.at[buf], y_hbm.at[pl.ds(0, rb * seg), :], ssem.at[buf]).wait()

    def weight_copies(e, ws):
        return (pltpu.make_async_copy(wg_hbm.at[e], wg_st.at[ws], wsem.at[ws]),
                pltpu.make_async_copy(wu_hbm.at[e], wu_st.at[ws], wsem.at[ws]),
                pltpu.make_async_copy(wd_hbm.at[e], wd_st.at[ws], wsem.at[ws]))

    @pl.when(i == 0)
    def _():
        for cp in weight_copies(bexp_ref[0], 0):
            cp.start(priority=1)
        ybuf[...] = jnp.zeros(ybuf.shape, ybuf.dtype)
        pltpu.make_async_copy(ybuf.at[0], y_hbm.at[pl.ds(n_real * seg, rb * seg), :],
                              ssem.at[0]).start()

        def first(r, carry):
            gather_copy(rtok_ref[0, 0, r], r, 0).start()
            return carry
        lax.fori_loop(0, rb, first, 0)

    @pl.when(jnp.logical_or(i == 0, prev_active))
    def _():
        wait_gather(slot)

    @pl.when(active)
    def _():
        @pl.when(bfirst_ref[i] > 0)
        def _():
            ws = bord_ref[i] & 1
            for cp in weight_copies(bexp_ref[i], ws):
                cp.wait()

            @pl.when(bnext_ref[i] >= 0)
            def _():
                for cp in weight_copies(bnext_ref[i], 1 - ws):
                    cp.start(priority=1)

            wgb[...] = wg_st[ws].astype(BF16)
            wub[...] = wu_st[ws].astype(BF16)
            wdb[...] = wd_st[ws].astype(BF16)

        n_piece = 8
        per = rb // n_piece

        def issue(piece):
            half_n = n_piece // 2
            if piece < half_n:
                for r in range(piece * 2 * per, (piece + 1) * 2 * per):
                    gather_copy(rtokn_ref[0, 0, r], r, other).start()
            else:
                for r in range((piece - half_n) * 2 * per, (piece - half_n + 1) * 2 * per):
                    scatter_copy(rslotp_ref[0, 0, r], r, other).start()

        x = _load_token_major(xbuf.at[slot], rb, tmp).astype(BF16)
        de = wgb.shape[1]
        d = wdb.shape[1]
        hc = de // 2
        acts = []
        for c in range(2):
            issue(2 * c)
            hg = jnp.dot(x, wgb[:, c * hc:(c + 1) * hc], preferred_element_type=F32)
            issue(2 * c + 1)
            hu = jnp.dot(x, wub[:, c * hc:(c + 1) * hc], preferred_element_type=F32)
            acts.append((hg * _sigmoid(hg) * hu).astype(BF16))
        act = jnp.concatenate(acts, axis=1)
        dc = d // 4
        ys = []
        for j in range(4):
            issue(4 + j)
            ys.append(jnp.dot(act, wdb[:, j * dc:(j + 1) * dc], preferred_element_type=F32))
        wait_scatter(slot)
        _store_token_major(ybuf.at[slot], jnp.concatenate(ys, axis=1), tmp)

    def scatter_all(idx_ref, buf):
        def body(r, carry):
            scatter_copy(idx_ref[0, 0, r], r, buf).start()
            return carry
        lax.fori_loop(0, rb, body, 0)

    @pl.when(jnp.logical_and(jnp.logical_not(active), prev_active))
    def _():
        wait_scatter(slot)
        scatter_all(rslotp_ref, other)
        wait_scatter(other)

    @pl.when(jnp.logical_and(i == nb - 1, active))
    def _():
        wait_scatter(other)
        scatter_all(rslot_ref, slot)
        wait_scatter(slot)
        wait_gather(other)


def _moe(hn, bexp, bact, rtok, rslot, w_gate, w_up, w_down, n_real):
    d, de = w_gate.shape[1], w_gate.shape[2]
    seg = d // LANE
    nb = bexp.shape[0]
    rb = MOE_ROW_BLOCK
    idx = jnp.arange(nb, dtype=jnp.int32)
    prev_e = jnp.concatenate([jnp.full((1,), -1, jnp.int32), bexp[:-1]])
    bfirst = jnp.logical_and(bact > 0, jnp.logical_or(idx == 0, bexp != prev_e)).astype(jnp.int32)
    bord = jnp.cumsum(bfirst) - 1
    first_at = jnp.where(bfirst > 0, idx, nb)
    next_first = jnp.concatenate([lax.cummin(first_at[::-1])[::-1][1:], jnp.full((1,), nb, jnp.int32)])
    bnext = jnp.where(next_first < nb, bexp[jnp.minimum(next_first, nb - 1)], -1).astype(jnp.int32)

    smem_rows = pl.BlockSpec((1, 1, rb), lambda i, *_: (i, 0, 0), memory_space=pltpu.SMEM)
    smem_next = pl.BlockSpec((1, 1, rb), lambda i, *_: (jnp.minimum(i + 1, nb - 1), 0, 0),
                             memory_space=pltpu.SMEM)
    smem_cur = pl.BlockSpec((1, 1, rb), lambda i, *_: (i + 1, 0, 0), memory_space=pltpu.SMEM)
    hbm = pl.BlockSpec(memory_space=pl.ANY)
    grid_spec = pltpu.PrefetchScalarGridSpec(
        num_scalar_prefetch=5,
        grid=(nb,),
        in_specs=[smem_rows, smem_next, smem_rows, smem_cur, hbm, hbm, hbm, hbm],
        out_specs=hbm,
        scratch_shapes=[pltpu.VMEM((2, rb * seg, LANE), BF16), pltpu.VMEM((2, rb * seg, LANE), BF16),
                        pltpu.VMEM((rb * seg, LANE), F32),
                        pltpu.VMEM((2, d, de), F32), pltpu.VMEM((2, d, de), F32), pltpu.VMEM((2, de, d), F32),
                        pltpu.VMEM((d, de), BF16), pltpu.VMEM((d, de), BF16), pltpu.VMEM((de, d), BF16),
                        pltpu.SemaphoreType.DMA((2,)), pltpu.SemaphoreType.DMA((2,)),
                        pltpu.SemaphoreType.DMA((2,))])
    return pl.pallas_call(
        functools.partial(_moe_kernel, n_real=n_real),
        grid_spec=grid_spec,
        out_shape=jax.ShapeDtypeStruct(((n_real + 2 * rb) * seg, LANE), BF16),
        compiler_params=_cparams(("arbitrary",)),
        name="moe",
    )(bexp, bact, bfirst, bord, bnext, rtok, rtok, rslot, rslot, hn, w_gate, w_up, w_down)


def _dispatch(ids, n, seg):
    k = 2
    m = n * k
    rb = MOE_ROW_BLOCK
    e_flat = ids[:, :k].reshape(m)
    ch = 256
    onehot = (e_flat[:, None] == jnp.arange(N_EXPERTS, dtype=jnp.int32)[None, :]).astype(F32)
    ohc = onehot.reshape(m // ch, ch, N_EXPERTS)
    tri = jnp.tril(jnp.ones((ch, ch), F32))
    within = jnp.einsum('ij,bjk->bik', tri, ohc)
    tot = within[:, -1, :]
    before = jnp.cumsum(tot, axis=0) - tot
    rank = (jnp.sum((within + before[:, None, :]) * ohc, axis=-1).reshape(m) - 1.0).astype(jnp.int32)
    counts = (before[-1] + tot[-1]).astype(jnp.int32)
    padded = (counts + rb - 1) // rb * rb
    pad_end = jnp.cumsum(padded)
    pad_start = pad_end - padded
    dest = jnp.sum(onehot * pad_start.astype(F32)[None, :], axis=-1).astype(jnp.int32) + rank
    nb = (m + N_EXPERTS * (rb - 1) + rb - 1) // rb
    p = nb * rb
    row_m = jnp.full((p,), -1, jnp.int32).at[dest].set(jnp.arange(m, dtype=jnp.int32))
    real = row_m >= 0
    rtok = jnp.where(real, row_m // k, 0)
    pidx = jnp.arange(p, dtype=jnp.int32)
    rslot = jnp.where(real, (row_m % k) * n + row_m // k, m + ((pidx // rb) % 2) * rb + pidx % rb)
    starts = jnp.arange(nb, dtype=jnp.int32) * rb
    bexp = jnp.minimum(jnp.sum((pad_end[None, :] <= starts[:, None]).astype(jnp.int32), axis=1),
                       N_EXPERTS - 1)
    bact = jnp.sum(real.reshape(nb, rb).astype(jnp.int32), axis=1)
    last_e = jnp.max(jnp.where(bact > 0, bexp, 0))
    bexp = jnp.where(bact > 0, bexp, last_e)
    rslot = jnp.concatenate([m + rb + jnp.arange(rb, dtype=jnp.int32), rslot])
    return (bexp, bact, (rtok * seg).reshape(nb, 1, rb), (rslot * seg).reshape(nb + 1, 1, rb),
            m)


def _combine_kernel(h_ref, y0_ref, y1_ref, wts_ref, fw_ref, o_ref, tmp_ref):
    tm = h_ref.shape[0]
    wts = wts_ref[...]
    y0 = _load_token_major(y0_ref, tm, tmp_ref)
    y1 = _load_token_major(y1_ref, tm, tmp_ref)
    moe = y0 * wts[:, 0:1] + y1 * wts[:, 1:2]
    h = h_ref[...] + moe
    o_ref[...] = h * lax.rsqrt(jnp.mean(h * h, axis=1, keepdims=True) + EPS) * fw_ref[...]


def _combine(h1, y, wts, final_w):
    n, d = h1.shape
    seg = d // LANE
    tm = min(512, n)
    nt = n // tm
    return pl.pallas_call(
        _combine_kernel,
        grid=(nt,),
        in_specs=[pl.BlockSpec((tm, d), lambda i: (i, 0)),
                  pl.BlockSpec((tm * seg, LANE), lambda i: (i, 0)),
                  pl.BlockSpec((tm * seg, LANE), lambda i: (i + nt, 0)),
                  pl.BlockSpec((tm, LANE), lambda i: (i, 0)),
                  pl.BlockSpec((1, d), lambda i: (0, 0))],
        out_specs=pl.BlockSpec((tm, d), lambda i: (i, 0)),
        out_shape=jax.ShapeDtypeStruct((n, d), F32),
        scratch_shapes=[pltpu.VMEM((tm * seg, LANE), F32)],
        compiler_params=_cparams(("parallel",)),
        name="combine",
    )(h1, y, y, wts, final_w.reshape(1, d))


def _rope_tables(positions):
    half = ROT_DIM // 2
    inv_freq = ROPE_THETA ** (-jnp.arange(0, ROT_DIM, 2, dtype=F32) / ROT_DIM)
    ang = positions.astype(F32)[..., None] * inv_freq
    cos, sin = jnp.cos(ang), jnp.sin(ang)
    b, t = positions.shape
    ones = jnp.ones((b, t, LANE - ROT_DIM), F32)
    zeros = jnp.zeros((b, t, LANE - half), F32)
    c = jnp.concatenate([cos, cos, ones], axis=-1)
    sa = jnp.concatenate([-sin, zeros], axis=-1)
    sb = jnp.concatenate([jnp.zeros((b, t, half), F32), sin, zeros[..., :LANE - ROT_DIM]], axis=-1)
    return c, sa, sb


def _arrange_w_in(w_in):
    d = w_in.shape[0]
    sizes = (N_HEADS_NSA * HEAD_DIM, 3 * 2 * N_KV_NSA * HEAD_DIM, 3 * N_HEADS_NSA,
             3 * N_HEADS_GDN * HEAD_DIM, N_HEADS_GDN, N_HEADS_GDN, N_HEADS_GDN * HEAD_DIM)
    offs = np.cumsum((0,) + sizes)
    seg = [w_in[:, offs[i]:offs[i + 1]].astype(BF16) for i in range(len(sizes))]
    q, kv, gate, gqkv, gb, ga, gz = seg
    used = sum(sizes)
    pad = jnp.zeros((d, N_CB * LANE - used), BF16)
    return jnp.concatenate([q, kv, gqkv, gz, gate, gb, ga, pad], axis=1)


def kernel(x, positions, attn_norm_w, w_in, cmp_wk, cmp_pek, cmp_wv, cmp_pev, gdn_conv_w, gdn_a_log,
           gdn_dt_bias, gdn_norm_w, w_out, ffn_norm_w, router_group_w, router_group_b,
           router_expert_w, router_expert_b, moe_w_gate, moe_w_up, moe_w_down, final_norm_w):
    b, t, d = x.shape
    n = b * t
    tabs = _rope_tables(positions)
    h = x.reshape(n, d)
    assert w_in.shape[0] == 1, "single-layer block only"
    for l in range(1):
        proj, small = _in_proj(h, attn_norm_w[l], _arrange_w_in(w_in[l]))
        proj3 = proj.reshape(b, t, N_CB * LANE)
        small3 = small.reshape(b, t, LANE)
        cmp_w = jnp.stack([cmp_wk[l], cmp_wv[l]])
        cmp_pe = jnp.stack([cmp_pek[l], cmp_pev[l]])
        kvc = _nsa_compress(proj3, tabs, cmp_w, cmp_pe)
        o_c, selbias = _nsa_cmp_attn(proj3, small3, tabs, kvc)
        o_s = _nsa_sel_attn(proj3, small3, tabs, selbias)
        o_w = _nsa_win_attn(proj3, small3, tabs)
        o_b = _gdn(proj3, small3, gdn_conv_w[l], gdn_a_log[l], gdn_dt_bias[l], gdn_norm_w[l])
        half = N_HEADS_NSA * HEAD_DIM
        wr = jnp.concatenate([router_group_w[l], router_expert_w[l],
                              jnp.zeros((d, LANE - N_GROUPS - N_EXPERTS), F32)], axis=1)
        br = jnp.concatenate([router_group_b[l], router_expert_b[l],
                              jnp.zeros((LANE - N_GROUPS - N_EXPERTS,), F32)]).reshape(1, LANE)
        h1, hn2, ids, wts = _out_proj(o_c.reshape(n, half), o_s.reshape(n, half), o_w.reshape(n, half),
                                      o_b.reshape(n, half), h, w_out[l].astype(BF16), ffn_norm_w[l], wr, br)
        bexp, bact, rtok, rslot, n_slots = _dispatch(ids, n, d // LANE)
        y = _moe(hn2, bexp, bact, rtok, rslot, moe_w_gate[l], moe_w_up[l], moe_w_down[l], n_slots)
        out = _combine(h1, y, wts, final_norm_w)
    return out.reshape(b, t, d)
```

```python
import functools

import numpy as np
import jax
import jax.numpy as jnp
from jax import lax
from jax.experimental import pallas as pl
from jax.experimental.pallas import tpu as pltpu

F32 = jnp.float32
BF16 = jnp.bfloat16

HEAD_DIM = 128
N_HEADS_NSA = 8
N_KV_NSA = 2
N_REP = N_HEADS_NSA // N_KV_NSA
N_HEADS_GDN = 8
ROT_DIM = 32
ROPE_THETA = 500000.0
CMP_LEN = 32
CMP_STRIDE = 16
SLC_LEN = 64
SLC_TOP = 16
WINDOW = 512
CONV_WIDTH = 4
GDN_CHUNK = 64
N_GROUPS = 8
EXPERTS_PER_GROUP = 8
N_EXPERTS = 64
MOE_ROW_BLOCK = 256
EPS = 1e-6
LANE = 128

CB_Q = 0
CB_KV = 8
CB_GQKV = 20
CB_Z = 44
CB_SMALL = 52
N_CB = 54
SMALL_BETA = 24
SMALL_DECAY = 32

NEG_BIAS = -32768.0
VMEM_LIMIT = 56 * 1024 * 1024


def _cparams(sem):
    return pltpu.CompilerParams(dimension_semantics=sem, vmem_limit_bytes=VMEM_LIMIT)


def _mm(a, b):
    return jnp.dot(a.astype(BF16), b.astype(BF16), preferred_element_type=F32)


def _mm_nt(a, b):
    return lax.dot_general(a.astype(BF16), b.astype(BF16), (((1,), (1,)), ((), ())),
                           preferred_element_type=F32)


def _rope(x, c, sa, sb):
    return (x * c + pltpu.roll(x, LANE - ROT_DIM // 2, 1) * sa
            + pltpu.roll(x, ROT_DIM // 2, 1) * sb)


def _sigmoid(x):
    return 0.5 * jnp.tanh(0.5 * x) + 0.5


def _lane_tiles(x):
    return [x[:, i:i + LANE] for i in range(0, x.shape[1], LANE)]


def _store_token_major(ref, x, tmp_ref):
    rows, d = x.shape
    seg = d // LANE
    for s in range(seg):
        tmp_ref[pl.ds(s, rows, stride=seg), :] = x[:, s * LANE:(s + 1) * LANE]
    ref[...] = tmp_ref[...].astype(ref.dtype)


def _load_token_major(ref, rows, tmp_ref):
    seg = ref.shape[0] // rows
    tmp_ref[...] = ref[...].astype(F32)
    return jnp.concatenate([tmp_ref[pl.ds(s, rows, stride=seg), :] for s in range(seg)], axis=1)


def _rowmax(x):
    return jnp.max(functools.reduce(jnp.maximum, _lane_tiles(x)), axis=1, keepdims=True)


def _rowsum(x):
    return jnp.sum(functools.reduce(jnp.add, _lane_tiles(x)), axis=1, keepdims=True)


def _in_proj_kernel(x_ref, nw_ref, w_ref, o_ref, small_ref, hn_ref, *, small_off):
    j = pl.program_id(1)

    @pl.when(j == 0)
    def _():
        x = x_ref[...]
        ms = jnp.mean(x * x, axis=-1, keepdims=True)
        hn_ref[...] = (x * lax.rsqrt(ms + EPS) * nw_ref[...]).astype(BF16)
    acc = jnp.dot(hn_ref[...], w_ref[...], preferred_element_type=F32)
    o_ref[...] = acc.astype(o_ref.dtype)

    @pl.when(j == pl.num_programs(1) - 1)
    def _():
        small_ref[...] = acc[:, small_off:small_off + LANE]


def _in_proj(x2, norm_w, w):
    n, d = x2.shape
    ncol = w.shape[1]
    tm = min(1024, n)
    tn = 2304
    small_off = CB_SMALL * LANE - (ncol // tn - 1) * tn
    assert 0 <= small_off <= tn - LANE
    return pl.pallas_call(
        functools.partial(_in_proj_kernel, small_off=small_off),
        grid=(n // tm, ncol // tn),
        in_specs=[pl.BlockSpec((tm, d), lambda i, j: (i, 0)),
                  pl.BlockSpec((1, d), lambda i, j: (0, 0)),
                  pl.BlockSpec((d, tn), lambda i, j: (0, j))],
        out_specs=[pl.BlockSpec((tm, tn), lambda i, j: (i, j)),
                   pl.BlockSpec((tm, LANE), lambda i, j: (i, 0))],
        out_shape=[jax.ShapeDtypeStruct((n, ncol), BF16), jax.ShapeDtypeStruct((n, LANE), F32)],
        scratch_shapes=[pltpu.VMEM((tm, d), BF16)],
        compiler_params=_cparams(("parallel", "arbitrary")),
        name="in_proj",
    )(x2, norm_w.reshape(1, d), w)


def _compress_kernel(a_ref, c_ref, sa_ref, sb_ref, w_ref, pe_ref, o_ref, xs_ref):
    kv = pl.program_id(1)
    t = a_ref.shape[1]
    nc = t // CMP_STRIDE
    x = a_ref[0].astype(F32)
    xr = _rope(x, c_ref[0], sa_ref[0], sb_ref[0])
    x = jnp.where(kv == 0, xr, x)
    xs_ref[pl.ds(0, t), :] = x
    xs_ref[pl.ds(t, CMP_STRIDE), :] = jnp.zeros((CMP_STRIDE, LANE), F32)
    acc = jnp.zeros((nc, LANE), F32)
    for l in range(CMP_LEN):
        rows = xs_ref[pl.ds(l, nc, stride=CMP_STRIDE), :] + pe_ref[0, pl.ds(l, 1), :]
        acc = acc + _mm(rows, w_ref[0, l])
    o_ref[0, 0, 0] = acc.astype(BF16)


def _nsa_compress(proj3, tabs, cmp_w, cmp_pe):
    b, t, _ = proj3.shape
    g = N_KV_NSA
    nc = t // CMP_STRIDE
    tab_spec = pl.BlockSpec((1, t, LANE), lambda bi, kv, gi: (bi, 0, 0))
    return pl.pallas_call(
        _compress_kernel,
        grid=(b, 2, g),
        in_specs=[pl.BlockSpec((1, t, LANE), lambda bi, kv, gi: (bi, 0, CB_KV + kv * g + gi)),
                  tab_spec, tab_spec, tab_spec,
                  pl.BlockSpec((1, CMP_LEN, LANE, LANE), lambda bi, kv, gi: (kv, 0, 0, 0)),
                  pl.BlockSpec((1, CMP_LEN, LANE), lambda bi, kv, gi: (kv, 0, 0))],
        out_specs=pl.BlockSpec((1, 1, 1, nc, LANE), lambda bi, kv, gi: (bi, kv, gi, 0, 0)),
        out_shape=jax.ShapeDtypeStruct((b, 2, g, nc, LANE), BF16),
        scratch_shapes=[pltpu.VMEM((t + CMP_STRIDE, LANE), F32)],
        compiler_params=_cparams(("parallel", "arbitrary", "arbitrary")),
        name="nsa_compress",
    )(proj3, *tabs, cmp_w, cmp_pe)


def _load_q(q_ref, c, sa, sb):
    scale = HEAD_DIM ** -0.5
    qs = [(_rope(q_ref[0, :, r * LANE:(r + 1) * LANE].astype(F32), c, sa, sb) * scale).astype(BF16)
          for r in range(N_REP)]
    return jnp.concatenate(qs, axis=0)


def _store_gated(o_ref, o, gate_ref, g, branch, tq):
    gt = gate_ref[0]
    for r in range(N_REP):
        col = ((g * N_REP + r) * 3 + branch)
        lane = lax.broadcasted_iota(jnp.int32, gt.shape, 1)
        gcol = jnp.sum(jnp.where(lane == col, gt, 0.0), axis=1, keepdims=True)
        o_ref[0, :, r * LANE:(r + 1) * LANE] = (o[r * tq:(r + 1) * tq] * _sigmoid(gcol)).astype(o_ref.dtype)


Q_TILES_PER_STEP = 4


def _row_views(refs, sub, tq):
    rows = pl.ds(sub * tq, tq)
    return [r.at[:, rows, :] if len(r.shape) == 3 else r.at[:, :, rows, :] for r in refs]


def _cmp_attn_kernel(q_ref, c_ref, sa_ref, sb_ref, kc_ref, vc_ref, gate_ref, o_ref, sel_ref, *, n_slc):
    tq = q_ref.shape[1] // Q_TILES_PER_STEP
    for sub in range(Q_TILES_PER_STEP):
        q, c, sa, sb, gate, o, sel = _row_views(
            (q_ref, c_ref, sa_ref, sb_ref, gate_ref, o_ref, sel_ref), sub, tq)
        _cmp_attn_tile(pl.program_id(2) * Q_TILES_PER_STEP + sub, q, c, sa, sb, kc_ref, vc_ref, gate, o,
                       sel, n_slc=n_slc)


def _cmp_attn_tile(i, q_ref, c_ref, sa_ref, sb_ref, kc_ref, vc_ref, gate_ref, o_ref, sel_ref, *, n_slc):
    g = pl.program_id(1)
    tq = q_ref.shape[1]
    nc = kc_ref.shape[3]
    q4 = _load_q(q_ref, c_ref[0], sa_ref[0], sb_ref[0])
    s = _mm_nt(q4, kc_ref[0, 0, 0])
    row = lax.broadcasted_iota(jnp.int32, s.shape, 0)
    n = lax.broadcasted_iota(jnp.int32, s.shape, 1)
    tpos = i * tq + (row & (tq - 1))
    mask = (n * CMP_STRIDE + (CMP_LEN - 1) <= tpos) & (n < nc - 1)
    sm = jnp.where(mask, s, -1e30)
    m = _rowmax(sm)
    p = jnp.where(mask, jnp.exp(sm - m), 0.0)
    l = _rowsum(p)
    p = p / jnp.maximum(l, 1e-30)
    o = _mm(p, vc_ref[0, 0, 0])
    _store_gated(o_ref, o, gate_ref, g, 0, tq)

    ps = p[0:tq]
    for r in range(1, N_REP):
        ps = ps + p[r * tq:(r + 1) * tq]
    cn = lax.broadcasted_iota(jnp.int32, (nc, LANE), 0)
    cj = lax.broadcasted_iota(jnp.int32, (nc, LANE), 1)
    ratio = SLC_LEN // CMP_STRIDE
    agg = ((cn >= ratio * cj - (CMP_LEN // CMP_STRIDE - 1)) & (cn < ratio * cj + ratio)
           & (cn < nc - 1) & (cj < n_slc))
    agg = jnp.where(agg, 1.0, 0.0).astype(BF16)
    ps_hi = ps.astype(BF16)
    ps_lo = (ps - ps_hi.astype(F32)).astype(BF16)
    imp = (jnp.dot(ps_hi, agg, preferred_element_type=F32)
           + jnp.dot(ps_lo, agg, preferred_element_type=F32))

    j = lax.broadcasted_iota(jnp.int32, (tq, LANE), 1)
    tt = i * tq + lax.broadcasted_iota(jnp.int32, (tq, LANE), 0)
    cur = tt // SLC_LEN
    valid = j <= cur
    forced = (j == 0) | (j == cur) | (j == cur - 1)
    vals = jnp.where(forced, 1e30, jnp.where(valid, imp, -1.0))
    vt = vals.T
    jb = lax.broadcasted_iota(jnp.int32, (n_slc, tq), 0)
    vb = vt[0:n_slc]
    cnt = jnp.zeros((n_slc, tq), F32)
    for jp in range(n_slc):
        cand = vt[jp:jp + 1, :]
        ge = jnp.where(cand >= vb, 1.0, 0.0)
        gt = jnp.where(cand > vb, 1.0, 0.0)
        cnt = cnt + jnp.where(jb > jp, ge, gt)
    keep = jnp.where(cnt < float(min(SLC_TOP, n_slc)), 0.0, NEG_BIAS)
    if n_slc < LANE:
        keep = jnp.concatenate([keep, jnp.full((LANE - n_slc, tq), NEG_BIAS, F32)], axis=0)
    sel_ref[0, 0] = jnp.where(valid, keep.T, NEG_BIAS).astype(BF16)


def _nsa_cmp_attn(proj3, small3, tabs, kvc, tq=128):
    b, t, _ = proj3.shape
    g = N_KV_NSA
    nc = t // CMP_STRIDE
    n_slc = t // SLC_LEN
    rw = N_REP * LANE
    bq = tq * Q_TILES_PER_STEP
    tab_spec = pl.BlockSpec((1, bq, LANE), lambda bi, gi, i: (bi, i, 0))
    return pl.pallas_call(
        functools.partial(_cmp_attn_kernel, n_slc=n_slc),
        grid=(b, g, t // bq),
        in_specs=[pl.BlockSpec((1, bq, rw), lambda bi, gi, i: (bi, i, gi)),
                  tab_spec, tab_spec, tab_spec,
                  pl.BlockSpec((1, 1, 1, nc, LANE), lambda bi, gi, i: (bi, 0, gi, 0, 0)),
                  pl.BlockSpec((1, 1, 1, nc, LANE), lambda bi, gi, i: (bi, 1, gi, 0, 0)),
                  pl.BlockSpec((1, bq, LANE), lambda bi, gi, i: (bi, i, 0))],
        out_specs=[pl.BlockSpec((1, bq, rw), lambda bi, gi, i: (bi, i, gi)),
                   pl.BlockSpec((1, 1, bq, LANE), lambda bi, gi, i: (bi, gi, i, 0))],
        out_shape=[jax.ShapeDtypeStruct((b, t, N_HEADS_NSA * LANE), BF16),
                   jax.ShapeDtypeStruct((b, g, t, LANE), BF16)],
        compiler_params=_cparams(("parallel", "parallel", "parallel")),
        name="nsa_cmp",
    )(proj3, *tabs, kvc, kvc, small3)


def _sel_attn_kernel(q_ref, c_ref, sa_ref, sb_ref, cf_ref, saf_ref, sbf_ref, k_ref, v_ref,
                     sel_ref, gate_ref, o_ref, *scratch, tk):
    tq = q_ref.shape[1] // Q_TILES_PER_STEP
    for sub in range(Q_TILES_PER_STEP):
        q, c, sa, sb, sel, gate, o = _row_views(
            (q_ref, c_ref, sa_ref, sb_ref, sel_ref, gate_ref, o_ref), sub, tq)
        _sel_attn_tile(pl.program_id(2) * Q_TILES_PER_STEP + sub, q, c, sa, sb, cf_ref, saf_ref, sbf_ref,
                       k_ref, v_ref, sel, gate, o, *scratch, tk=tk)


def _sel_attn_tile(i, q_ref, c_ref, sa_ref, sb_ref, cf_ref, saf_ref, sbf_ref, k_ref, v_ref,
                   sel_ref, gate_ref, o_ref, ka_ref, vs_ref, m_ref, l_ref, acc_ref, s0_ref, s1_ref,
                   *, tk):
    g = pl.program_id(1)
    tq = q_ref.shape[1]
    t = k_ref.shape[1]
    rows = N_REP * tq

    @pl.when(i == 0)
    def _():
        kr = _rope(k_ref[0].astype(F32), cf_ref[0], saf_ref[0], sbf_ref[0])
        ka_ref[:, 0:LANE] = kr.astype(BF16)
        pos = lax.broadcasted_iota(jnp.int32, (t, LANE), 0)
        lane = lax.broadcasted_iota(jnp.int32, (t, LANE), 1)
        ka_ref[:, LANE:2 * LANE] = jnp.where(lane == pos // SLC_LEN, 1.0, 0.0).astype(BF16)
        vs_ref[...] = v_ref[0].astype(BF16)

    q4 = _load_q(q_ref, c_ref[0], sa_ref[0], sb_ref[0])
    bias = sel_ref[0, 0]
    qa = jnp.concatenate([q4, jnp.concatenate([bias] * N_REP, axis=0)], axis=1)

    m_ref[...] = jnp.full((rows, LANE), -1e30, F32)
    l_ref[...] = jnp.zeros((rows, LANE), F32)
    acc_ref[...] = jnp.zeros((rows, LANE), F32)

    half = tk // 2

    def scores(kt, hlf):
        k0 = pl.multiple_of(kt * tk + hlf * half, half)
        return _mm_nt(qa, ka_ref[pl.ds(k0, half), :])

    def update(s_ref, kt, hlf, causal):
        k0 = pl.multiple_of(kt * tk + hlf * half, half)
        s = s_ref[...]
        if causal:
            r = lax.broadcasted_iota(jnp.int32, s.shape, 0)
            kp = k0 + lax.broadcasted_iota(jnp.int32, s.shape, 1)
            s = jnp.where(kp <= i * tq + (r & (tq - 1)), s, -1e30)
        tiles = _lane_tiles(s)
        m_old = m_ref[...]
        m_new = jnp.maximum(m_old, _rowmax(s))
        alpha = jnp.exp(m_old - m_new)
        ps = [jnp.exp(tl - m_new) for tl in tiles]
        l_ref[...] = alpha * l_ref[...] + jnp.sum(functools.reduce(jnp.add, ps), axis=1, keepdims=True)
        p = jnp.concatenate([x.astype(BF16) for x in ps], axis=1)
        acc_ref[...] = alpha * acc_ref[...] + jnp.dot(p, vs_ref[pl.ds(k0, half), :],
                                                      preferred_element_type=F32)
        m_ref[...] = m_new

    diag = (i * tq) // tk
    s0_ref[...] = scores(0, 0)

    def body(kt, carry):
        s1_ref[...] = scores(kt, 1)
        update(s0_ref, kt, 0, False)
        s0_ref[...] = scores(kt + 1, 0)
        update(s1_ref, kt, 1, False)
        return carry

    def body2(j, carry):
        body(2 * j, carry)
        body(2 * j + 1, carry)
        return carry

    lax.fori_loop(0, diag // 2, body2, 0)
    lax.fori_loop((diag // 2) * 2, diag, body, 0)
    s1_ref[...] = scores(diag, 1)
    update(s0_ref, diag, 0, True)
    update(s1_ref, diag, 1, True)
    o = acc_ref[...] / l_ref[...]
    _store_gated(o_ref, o, gate_ref, g, 1, tq)


def _nsa_sel_attn(proj3, small3, tabs, selbias, tq=128, tk=512):
    b, t, _ = proj3.shape
    g = N_KV_NSA
    rw = N_REP * LANE
    bq = tq * Q_TILES_PER_STEP
    tk = min(tk, t)
    tab_spec = pl.BlockSpec((1, bq, LANE), lambda bi, gi, i: (bi, i, 0))
    tabf_spec = pl.BlockSpec((1, t, LANE), lambda bi, gi, i: (bi, 0, 0))
    kcb = CB_KV + 1 * 2 * g
    return pl.pallas_call(
        functools.partial(_sel_attn_kernel, tk=tk),
        grid=(b, g, t // bq),
        in_specs=[pl.BlockSpec((1, bq, rw), lambda bi, gi, i: (bi, i, gi)),
                  tab_spec, tab_spec, tab_spec, tabf_spec, tabf_spec, tabf_spec,
                  pl.BlockSpec((1, t, LANE), lambda bi, gi, i: (bi, 0, kcb + gi)),
                  pl.BlockSpec((1, t, LANE), lambda bi, gi, i: (bi, 0, kcb + g + gi)),
                  pl.BlockSpec((1, 1, bq, LANE), lambda bi, gi, i: (bi, gi, i, 0)),
                  pl.BlockSpec((1, bq, LANE), lambda bi, gi, i: (bi, i, 0))],
        out_specs=pl.BlockSpec((1, bq, rw), lambda bi, gi, i: (bi, i, gi)),
        out_shape=jax.ShapeDtypeStruct((b, t, N_HEADS_NSA * LANE), BF16),
        scratch_shapes=[pltpu.VMEM((t, 2 * LANE), BF16), pltpu.VMEM((t, LANE), BF16),
                        pltpu.VMEM((N_REP * tq, LANE), F32), pltpu.VMEM((N_REP * tq, LANE), F32),
                        pltpu.VMEM((N_REP * tq, LANE), F32),
                        pltpu.VMEM((N_REP * tq, tk // 2), F32), pltpu.VMEM((N_REP * tq, tk // 2), F32)],
        compiler_params=_cparams(("parallel", "parallel", "arbitrary")),
        name="nsa_sel",
    )(proj3, *tabs, *tabs, proj3, proj3, selbias, small3)


def _win_attn_kernel(q_ref, c_ref, sa_ref, sb_ref, cf_ref, saf_ref, sbf_ref, k_ref, v_ref,
                     gate_ref, o_ref, ks_ref, vs_ref, *, span):
    tq = q_ref.shape[1] // Q_TILES_PER_STEP
    for sub in range(Q_TILES_PER_STEP):
        q, c, sa, sb, gate, o = _row_views((q_ref, c_ref, sa_ref, sb_ref, gate_ref, o_ref), sub, tq)
        _win_attn_tile(pl.program_id(2) * Q_TILES_PER_STEP + sub, q, c, sa, sb, cf_ref, saf_ref, sbf_ref,
                       k_ref, v_ref, gate, o, ks_ref, vs_ref, span=span)


def _win_attn_tile(i, q_ref, c_ref, sa_ref, sb_ref, cf_ref, saf_ref, sbf_ref, k_ref, v_ref,
                   gate_ref, o_ref, ks_ref, vs_ref, *, span):
    g = pl.program_id(1)
    tq = q_ref.shape[1]

    t = k_ref.shape[1]

    @pl.when(i == 0)
    def _():
        zeros = jnp.zeros((WINDOW, LANE), BF16)
        ks_ref[pl.ds(0, WINDOW), :] = zeros
        vs_ref[pl.ds(0, WINDOW), :] = zeros
        ks_ref[pl.ds(WINDOW, t), :] = _rope(k_ref[0].astype(F32), cf_ref[0], saf_ref[0],
                                            sbf_ref[0]).astype(BF16)
        vs_ref[pl.ds(WINDOW, t), :] = v_ref[0].astype(BF16)

    q4 = _load_q(q_ref, c_ref[0], sa_ref[0], sb_ref[0])
    k0 = pl.multiple_of(i * tq, tq)
    s = _mm_nt(q4, ks_ref[pl.ds(k0, span), :])
    tiles = _lane_tiles(s)
    r = lax.broadcasted_iota(jnp.int32, tiles[0].shape, 0) & (tq - 1)
    c = lax.broadcasted_iota(jnp.int32, tiles[0].shape, 1)
    first_block = WINDOW // tq - i
    masked = []
    for b, tl in enumerate(tiles):
        if b == 0:
            tl = jnp.where(c > r, tl, -1e30)
        if b == len(tiles) - 1:
            tl = jnp.where(c <= r, tl, -1e30)
        else:
            tl = jnp.where(b >= first_block, tl, -1e30)
        masked.append(tl)
    m = jnp.max(functools.reduce(jnp.maximum, masked), axis=1, keepdims=True)
    ps = [jnp.exp(tl - m) for tl in masked]
    l = jnp.sum(functools.reduce(jnp.add, ps), axis=1, keepdims=True)
    p = jnp.concatenate([x.astype(BF16) for x in ps], axis=1)
    o = jnp.dot(p, vs_ref[pl.ds(k0, span), :], preferred_element_type=F32) / l
    _store_gated(o_ref, o, gate_ref, g, 2, tq)


def _nsa_win_attn(proj3, small3, tabs, tq=128):
    b, t, _ = proj3.shape
    g = N_KV_NSA
    rw = N_REP * LANE
    bq = tq * Q_TILES_PER_STEP
    span = WINDOW + tq
    assert tq == LANE
    tab_spec = pl.BlockSpec((1, bq, LANE), lambda bi, gi, i: (bi, i, 0))
    tabf_spec = pl.BlockSpec((1, t, LANE), lambda bi, gi, i: (bi, 0, 0))
    kcb = CB_KV + 2 * 2 * g
    return pl.pallas_call(
        functools.partial(_win_attn_kernel, span=span),
        grid=(b, g, t // bq),
        in_specs=[pl.BlockSpec((1, bq, rw), lambda bi, gi, i: (bi, i, gi)),
                  tab_spec, tab_spec, tab_spec, tabf_spec, tabf_spec, tabf_spec,
                  pl.BlockSpec((1, t, LANE), lambda bi, gi, i: (bi, 0, kcb + gi)),
                  pl.BlockSpec((1, t, LANE), lambda bi, gi, i: (bi, 0, kcb + g + gi)),
                  pl.BlockSpec((1, bq, LANE), lambda bi, gi, i: (bi, i, 0))],
        out_specs=pl.BlockSpec((1, bq, rw), lambda bi, gi, i: (bi, i, gi)),
        out_shape=jax.ShapeDtypeStruct((b, t, N_HEADS_NSA * LANE), BF16),
        scratch_shapes=[pltpu.VMEM((t + WINDOW, LANE), BF16), pltpu.VMEM((t + WINDOW, LANE), BF16)],
        compiler_params=_cparams(("parallel", "parallel", "arbitrary")),
        name="nsa_win",
    )(proj3, *tabs, *tabs, proj3, proj3, small3)


def _gdn_kernel(q_ref, k_ref, v_ref, z_ref, braw_ref, araw_ref, cwq_ref, cwk_ref, cwv_ref, alog_ref,
                dtb_ref, nw_ref, o_ref, q_s, k_s, v_s, o_s, qp_s, op_s, xp_s, bd_s, gcd_s, gl_s, mm_s,
                nn_s):
    t = q_ref.shape[1]
    c = GDN_CHUNK
    nchunk = t // c
    xp_s[pl.ds(0, 8), :] = jnp.zeros((8, LANE), F32)

    def conv_silu(x, cw):
        xp_s[pl.ds(8, t), :] = x
        y = x * cw[CONV_WIDTH - 1:CONV_WIDTH]
        for sft in range(1, CONV_WIDTH):
            y = y + xp_s[pl.ds(8 - sft, t), :] * cw[CONV_WIDTH - 1 - sft:CONV_WIDTH - sft]
        return y * _sigmoid(y)

    def l2n(x):
        return x * lax.rsqrt(jnp.sum(x * x, axis=1, keepdims=True) + EPS)

    q_s[...] = l2n(conv_silu(q_ref[0].astype(F32), cwq_ref[...])) * (HEAD_DIM ** -0.5)
    k_s[...] = l2n(conv_silu(k_ref[0].astype(F32), cwk_ref[...]))
    v_s[...] = conv_silu(v_ref[0].astype(F32), cwv_ref[...])

    bd_s[...] = _sigmoid(braw_ref[0, 0])
    xa = araw_ref[0, 0] + dtb_ref[0]
    softplus = jnp.maximum(xa, 0.0) + jnp.log(1.0 + jnp.exp(-jnp.abs(xa)))
    gc = -jnp.exp(alog_ref[0]) * softplus
    lane = lax.broadcasted_iota(jnp.int32, gc.shape, 1)
    sft = 1
    while sft < c:
        gc = gc + jnp.where((lane & (c - 1)) >= sft, pltpu.roll(gc, sft, 1), 0.0)
        sft *= 2
    gcd_s[...] = gc

    c2 = 2 * c
    ci = lax.broadcasted_iota(jnp.int32, (c2, c2), 0)
    cj = lax.broadcasted_iota(jnp.int32, (c2, c2), 1)
    same = (ci // c) == (cj // c)
    tril = same & (cj <= ci)
    strict = same & (cj < ci)
    eye = jnp.where(ci == cj, 1.0, 0.0)
    first = ci < c

    pairs_per_iter = 8
    rng = range(pairs_per_iter)

    def prep(it, carry):
        n2 = [it * pairs_per_iter + p for p in rng]
        sl = [pl.ds(pl.multiple_of(n * c2, c2), c2) for n in n2]
        grow = [gcd_s[pl.ds(n, 1), :] for n in n2]
        gc2 = [jnp.broadcast_to(g_, (c2, c2)).T for g_ in grow]
        betac = [jnp.broadcast_to(bd_s[pl.ds(n, 1), :], (c2, c2)).T for n in n2]
        kn = [k_s[s_, :] for s_ in sl]
        kbn = [kn[p] * betac[p] for p in rng]
        decay = [jnp.exp(jnp.where(tril, gc2[p] - grow[p], -1e30)) for p in rng]
        kk = [_mm_nt(kbn[p], kn[p]) for p in rng]
        y = [-jnp.where(strict, kk[p] * decay[p], 0.0) for p in rng]
        pm = [eye + y_ for y_ in y]
        pw = 1
        while 2 * pw < c:
            y = [_mm(y_, y_) for y_ in y]
            pm = [pm[p] + _mm(pm[p], y[p]) for p in rng]
            pw *= 2
        egn = [jnp.exp(g_) for g_ in gc2]
        uw = [_mm(pm[p], jnp.concatenate([v_s[sl[p], :] * betac[p], kbn[p] * egn[p]], axis=1))
              for p in rng]
        qn = [q_s[s_, :] for s_ in sl]
        qk = [_mm_nt(qn[p], kn[p]) for p in rng]
        attn = [jnp.where(tril, qk[p] * decay[p], 0.0) for p in rng]
        auw = [_mm(attn[p], uw[p]) for p in rng]
        for p in rng:
            qp_s[sl[p], :] = qn[p] * egn[p] - auw[p][:, LANE:]
            op_s[sl[p], :] = auw[p][:, :LANE]
        gl = [jnp.where(first, g_[c - 1:c, :], g_[c2 - 1:c2, :]) for g_ in gc2]
        kdt = [(kn[p] * jnp.exp(gl[p] - gc2[p])).T for p in rng]
        nma = [_mm(kdt[p][:, 0:c], uw[p][0:c]) for p in rng]
        nmb = [_mm(kdt[p][:, c:c2], uw[p][c:c2]) for p in rng]
        mb = [x[:, LANE:] for x in nmb]
        mbnm = [_mm(mb[p], nma[p]) for p in rng]
        ma = [x[:, LANE:] for x in nma]
        na = [x[:, :LANE] for x in nma]
        nb_ = [x[:, :LANE] for x in nmb]
        mba = [x[:, LANE:] for x in mbnm]
        mbn = [x[:, :LANE] for x in mbnm]
        for p in rng:
            ga = jnp.exp(gc2[p][c - 1:c, :])
            gb = jnp.exp(gc2[p][c2 - 1:c2, :])
            mm_s[2 * n2[p]] = ma[p].astype(BF16)
            nn_s[2 * n2[p]] = na[p]
            gl_s[pl.ds(2 * n2[p], 1), :] = ga
            mm_s[2 * n2[p] + 1] = (gb * ma[p] + ga * mb[p] - mba[p]).astype(BF16)
            nn_s[2 * n2[p] + 1] = gb * na[p] - mbn[p] + nb_[p]
            gl_s[pl.ds(2 * n2[p] + 1, 1), :] = ga * gb
        return carry

    lax.fori_loop(0, nchunk // (2 * pairs_per_iter), prep, 0)

    def second_out(n2, s_mid):
        rows = pl.ds(pl.multiple_of(n2 * c2, c2) + c, c)
        o_s[rows, :] = (jnp.dot(qp_s[rows, :].astype(BF16), s_mid.astype(BF16),
                                 preferred_element_type=F32) + op_s[rows, :])

    def scan(n2, carry):
        s, s_mid_prev = carry
        second_out(jnp.maximum(n2 - 1, 0), s_mid_prev)
        ra = pl.ds(pl.multiple_of(n2 * c2, c2), c)
        sb = s.astype(BF16)
        s_mid = (s * gl_s[pl.ds(2 * n2, 1), :] - jnp.dot(mm_s[2 * n2], sb, preferred_element_type=F32)
                 + nn_s[2 * n2])
        s_new = (s * gl_s[pl.ds(2 * n2 + 1, 1), :]
                 - jnp.dot(mm_s[2 * n2 + 1], sb, preferred_element_type=F32) + nn_s[2 * n2 + 1])
        o_s[ra, :] = jnp.dot(qp_s[ra, :].astype(BF16), sb, preferred_element_type=F32) + op_s[ra, :]
        return s_new, s_mid

    zero_state = jnp.zeros((LANE, LANE), F32)
    _, s_mid_last = lax.fori_loop(0, nchunk // 2, scan, (zero_state, zero_state))
    second_out(jnp.int32(nchunk // 2 - 1), s_mid_last)

    o = o_s[...]
    on = o * lax.rsqrt(jnp.mean(o * o, axis=1, keepdims=True) + EPS) * nw_ref[...]
    z = z_ref[0].astype(F32)
    o_ref[0] = (on * (z * _sigmoid(z))).astype(o_ref.dtype)


def _gdn(proj3, small3, conv_w, a_log, dt_bias, norm_w):
    b, t, _ = proj3.shape
    hh = N_HEADS_GDN
    c = GDN_CHUNK
    col = lambda cb: pl.BlockSpec((1, t, LANE), lambda bi, hi: (bi, 0, cb + hi))
    cw = lambda off: pl.BlockSpec((CONV_WIDTH, LANE), lambda bi, hi: (0, off + hi))
    hrow = pl.BlockSpec((1, 1, LANE), lambda bi, hi: (hi, 0, 0))
    alog_b = jnp.broadcast_to(a_log.astype(F32)[:, None, None], (hh, 1, LANE))
    dtb_b = jnp.broadcast_to(dt_bias.astype(F32)[:, None, None], (hh, 1, LANE))
    big = pltpu.VMEM((t, LANE), F32)
    nrow = t // LANE
    ba = small3[:, :, SMALL_BETA:SMALL_BETA + 2 * hh].transpose(0, 2, 1).reshape(b, 2 * hh, nrow, LANE)
    dense = lambda off: pl.BlockSpec((1, 1, nrow, LANE), lambda bi, hi: (bi, off + hi, 0, 0))
    return pl.pallas_call(
        _gdn_kernel,
        grid=(b, hh),
        in_specs=[col(CB_GQKV), col(CB_GQKV + hh), col(CB_GQKV + 2 * hh), col(CB_Z),
                  dense(0), dense(hh),
                  cw(0), cw(hh), cw(2 * hh), hrow, hrow,
                  pl.BlockSpec((1, LANE), lambda bi, hi: (0, 0))],
        out_specs=pl.BlockSpec((1, t, LANE), lambda bi, hi: (bi, 0, hi)),
        out_shape=jax.ShapeDtypeStruct((b, t, hh * LANE), BF16),
        scratch_shapes=[big, big, big, big, big, big, pltpu.VMEM((t + 8, LANE), F32),
                        pltpu.VMEM((nrow, LANE), F32), pltpu.VMEM((nrow, LANE), F32),
                        pltpu.VMEM((t // c, LANE), F32),
                        pltpu.VMEM((t // c, LANE, LANE), BF16),
                        pltpu.VMEM((t // c, LANE, LANE), F32)],
        compiler_params=_cparams(("parallel", "parallel")),
        name="gdn",
    )(proj3, proj3, proj3, proj3, ba, ba, conv_w, conv_w, conv_w, alog_b, dtb_b,
      norm_w.reshape(1, LANE))


def _split3(a):
    hi = a.astype(BF16)
    lo = (a - hi.astype(F32)).astype(BF16)
    return hi, lo


def _out_proj_kernel(oc_ref, os_ref, ow_ref, ob_ref, x_ref, wo_ref, fw_ref, wr_ref, br_ref,
                     h_ref, hn_ref, ids_ref, wts_ref, tmp_ref):
    half = oc_ref.shape[1]
    oa = (oc_ref[...].astype(F32) + os_ref[...].astype(F32) + ow_ref[...].astype(F32)).astype(BF16)
    h1 = (x_ref[...] + jnp.dot(oa, wo_ref[0:half, :], preferred_element_type=F32)
          + jnp.dot(ob_ref[...], wo_ref[half:2 * half, :], preferred_element_type=F32))
    h_ref[...] = h1
    hn = h1 * lax.rsqrt(jnp.mean(h1 * h1, axis=1, keepdims=True) + EPS) * fw_ref[...]
    _store_token_major(hn_ref, hn, tmp_ref)

    a_hi, a_lo = _split3(hn)
    w_hi, w_lo = _split3(wr_ref[...])
    dot = lambda a, b: jnp.dot(a, b, preferred_element_type=F32)
    logits = dot(a_hi, w_hi) + dot(a_hi, w_lo) + dot(a_lo, w_hi) + br_ref[...]
    lane = lax.broadcasted_iota(jnp.int32, logits.shape, 1)
    big = 1e30
    is_g = lane < N_GROUPS
    lg = jnp.where(is_g, logits, -big)
    gm = jnp.max(lg, axis=1, keepdims=True)
    grp = jnp.min(jnp.where(lg == gm, lane, LANE), axis=1, keepdims=True)
    p_grp = 1.0 / jnp.sum(jnp.where(is_g, jnp.exp(lg - gm), 0.0), axis=1, keepdims=True)
    e_id = lane - N_GROUPS
    in_g = (e_id >= 0) & (e_id < N_EXPERTS) & ((e_id // EXPERTS_PER_GROUP) == grp)
    le = jnp.where(in_g, logits, -big)
    em = jnp.max(le, axis=1, keepdims=True)
    pe = jnp.where(in_g, jnp.exp(le - em), 0.0)
    pe = pe / jnp.sum(pe, axis=1, keepdims=True)
    pm = jnp.where(in_g, pe, -1.0)
    p1 = jnp.max(pm, axis=1, keepdims=True)
    i1 = jnp.min(jnp.where(pm == p1, lane, LANE), axis=1, keepdims=True)
    pm2 = jnp.where(lane == i1, -1.0, pm)
    p2 = jnp.max(pm2, axis=1, keepdims=True)
    i2 = jnp.min(jnp.where(pm2 == p2, lane, LANE), axis=1, keepdims=True)
    den = p1 + p2
    ids_ref[...] = jnp.where(lane == 0, i1 - N_GROUPS, jnp.where(lane == 1, i2 - N_GROUPS, 0))
    wts_ref[...] = jnp.where(lane == 0, p1 / den * p_grp, jnp.where(lane == 1, p2 / den * p_grp, 0.0))


def _out_proj(oc, os_, ow, ob, x2, w_out, ffn_w, wr, br):
    n, d = x2.shape
    half = oc.shape[1]
    seg = d // LANE
    tm = min(512, n)
    row = lambda w: pl.BlockSpec((tm, w), lambda i: (i, 0))
    full = lambda a: pl.BlockSpec(a.shape, lambda i: (0,) * a.ndim, pipeline_mode=pl.Buffered(1))
    fw = ffn_w.reshape(1, d)
    return pl.pallas_call(
        _out_proj_kernel,
        grid=(n // tm,),
        in_specs=[row(half), row(half), row(half), row(half), row(d), full(w_out), full(fw),
                  full(wr), full(br)],
        out_specs=[row(d), pl.BlockSpec((tm * seg, LANE), lambda i: (i, 0)), row(LANE), row(LANE)],
        out_shape=[jax.ShapeDtypeStruct((n, d), F32), jax.ShapeDtypeStruct((n * seg, LANE), BF16),
                   jax.ShapeDtypeStruct((n, LANE), jnp.int32), jax.ShapeDtypeStruct((n, LANE), F32)],
        scratch_shapes=[pltpu.VMEM((tm * seg, LANE), F32)],
        compiler_params=_cparams(("parallel",)),
        name="out_proj",
    )(oc, os_, ow, ob, x2, w_out, fw, wr, br)


def _moe_kernel(bexp_ref, bact_ref, bfirst_ref, bord_ref, bnext_ref, rtok_ref, rtokn_ref, rslotp_ref,
                rslot_ref, hn_hbm, wg_hbm, wu_hbm, wd_hbm, y_hbm, xbuf, ybuf, tmp, wg_st, wu_st, wd_st,
                wgb, wub, wdb, gsem, ssem, wsem, *, n_real):
    i = pl.program_id(0)
    nb = pl.num_programs(0)
    rb = rtok_ref.shape[2]
    seg = ybuf.shape[1] // rb
    slot = i & 1
    other = 1 - slot
    active = bact_ref[i] > 0
    prev_active = jnp.logical_and(i > 0, bact_ref[jnp.maximum(i - 1, 0)] > 0)

    def gather_copy(tok, r, buf):
        return pltpu.make_async_copy(hn_hbm.at[pl.ds(pl.multiple_of(tok, seg), seg), :],
                                     xbuf.at[buf, pl.ds(r * seg, seg), :], gsem.at[buf])

    def scatter_copy(dst, r, buf):
        return pltpu.make_async_copy(ybuf.at[buf, pl.ds(r * seg, seg), :],
                                     y_hbm.at[pl.ds(pl.multiple_of(dst, seg), seg), :], ssem.at[buf])

    def wait_gather(buf):
        pltpu.make_async_copy(hn_hbm.at[pl.ds(0, rb * seg), :], xbuf.at[buf], gsem.at[buf]).wait()

    def wait_scatter(buf):
        pltpu.make_async_copy(ybuf.at[buf], y_hbm.at[pl.ds(0, rb * seg), :], ssem.at[buf]).wait()

    def weight_copies(e, ws):
        return (pltpu.make_async_copy(wg_hbm.at[e], wg_st.at[ws], wsem.at[ws]),
                pltpu.make_async_copy(wu_hbm.at[e], wu_st.at[ws], wsem.at[ws]),
                pltpu.make_async_copy(wd_hbm.at[e], wd_st.at[ws], wsem.at[ws]))

    @pl.when(i == 0)
    def _():
        for cp in weight_copies(bexp_ref[0], 0):
            cp.start(priority=1)
        ybuf[...] = jnp.zeros(ybuf.shape, ybuf.dtype)
        pltpu.make_async_copy(ybuf.at[0], y_hbm.at[pl.ds(n_real * seg, rb * seg), :],
                              ssem.at[0]).start()

        def first(r, carry):
            gather_copy(rtok_ref[0, 0, r], r, 0).start()
            return carry
        lax.fori_loop(0, rb, first, 0)

    @pl.when(jnp.logical_or(i == 0, prev_active))
    def _():
        wait_gather(slot)

    @pl.when(active)
    def _():
        @pl.when(bfirst_ref[i] > 0)
        def _():
            ws = bord_ref[i] & 1
            for cp in weight_copies(bexp_ref[i], ws):
                cp.wait()

            @pl.when(bnext_ref[i] >= 0)
            def _():
                for cp in weight_copies(bnext_ref[i], 1 - ws):
                    cp.start(priority=1)

            wgb[...] = wg_st[ws].astype(BF16)
            wub[...] = wu_st[ws].astype(BF16)
            wdb[...] = wd_st[ws].astype(BF16)

        n_piece = 8
        per = rb // n_piece

        def issue(piece):
            half_n = n_piece // 2
            if piece < half_n:
                for r in range(piece * 2 * per, (piece + 1) * 2 * per):
                    gather_copy(rtokn_ref[0, 0, r], r, other).start()
            else:
                for r in range((piece - half_n) * 2 * per, (piece - half_n + 1) * 2 * per):
                    scatter_copy(rslotp_ref[0, 0, r], r, other).start()

        x = _load_token_major(xbuf.at[slot], rb, tmp).astype(BF16)
        de = wgb.shape[1]
        d = wdb.shape[1]
        hc = de // 2
        acts = []
        for c in range(2):
            issue(2 * c)
            hg = jnp.dot(x, wgb[:, c * hc:(c + 1) * hc], preferred_element_type=F32)
            issue(2 * c + 1)
            hu = jnp.dot(x, wub[:, c * hc:(c + 1) * hc], preferred_element_type=F32)
            acts.append((hg * _sigmoid(hg) * hu).astype(BF16))
        act = jnp.concatenate(acts, axis=1)
        dc = d // 4
        ys = []
        for j in range(4):
            issue(4 + j)
            ys.append(jnp.dot(act, wdb[:, j * dc:(j + 1) * dc], preferred_element_type=F32))
        wait_scatter(slot)
        _store_token_major(ybuf.at[slot], jnp.concatenate(ys, axis=1), tmp)

    def scatter_all(idx_ref, buf):
        def body(r, carry):
            scatter_copy(idx_ref[0, 0, r], r, buf).start()
            return carry
        lax.fori_loop(0, rb, body, 0)

    @pl.when(jnp.logical_and(jnp.logical_not(active), prev_active))
    def _():
        wait_scatter(slot)
        scatter_all(rslotp_ref, other)
        wait_scatter(other)

    @pl.when(jnp.logical_and(i == nb - 1, active))
    def _():
        wait_scatter(other)
        scatter_all(rslot_ref, slot)
        wait_scatter(slot)
        wait_gather(other)


def _moe(hn, bexp, bact, rtok, rslot, w_gate, w_up, w_down, n_real):
    d, de = w_gate.shape[1], w_gate.shape[2]
    seg = d // LANE
    nb = bexp.shape[0]
    rb = MOE_ROW_BLOCK
    idx = jnp.arange(nb, dtype=jnp.int32)
    prev_e = jnp.concatenate([jnp.full((1,), -1, jnp.int32), bexp[:-1]])
    bfirst = jnp.logical_and(bact > 0, jnp.logical_or(idx == 0, bexp != prev_e)).astype(jnp.int32)
    bord = jnp.cumsum(bfirst) - 1
    first_at = jnp.where(bfirst > 0, idx, nb)
    next_first = jnp.concatenate([lax.cummin(first_at[::-1])[::-1][1:], jnp.full((1,), nb, jnp.int32)])
    bnext = jnp.where(next_first < nb, bexp[jnp.minimum(next_first, nb - 1)], -1).astype(jnp.int32)

    smem_rows = pl.BlockSpec((1, 1, rb), lambda i, *_: (i, 0, 0), memory_space=pltpu.SMEM)
    smem_next = pl.BlockSpec((1, 1, rb), lambda i, *_: (jnp.minimum(i + 1, nb - 1), 0, 0),
                             memory_space=pltpu.SMEM)
    smem_cur = pl.BlockSpec((1, 1, rb), lambda i, *_: (i + 1, 0, 0), memory_space=pltpu.SMEM)
    hbm = pl.BlockSpec(memory_space=pl.ANY)
    grid_spec = pltpu.PrefetchScalarGridSpec(
        num_scalar_prefetch=5,
        grid=(nb,),
        in_specs=[smem_rows, smem_next, smem_rows, smem_cur, hbm, hbm, hbm, hbm],
        out_specs=hbm,
        scratch_shapes=[pltpu.VMEM((2, rb * seg, LANE), BF16), pltpu.VMEM((2, rb * seg, LANE), BF16),
                        pltpu.VMEM((rb * seg, LANE), F32),
                        pltpu.VMEM((2, d, de), F32), pltpu.VMEM((2, d, de), F32), pltpu.VMEM((2, de, d), F32),
                        pltpu.VMEM((d, de), BF16), pltpu.VMEM((d, de), BF16), pltpu.VMEM((de, d), BF16),
                        pltpu.SemaphoreType.DMA((2,)), pltpu.SemaphoreType.DMA((2,)),
                        pltpu.SemaphoreType.DMA((2,))])
    return pl.pallas_call(
        functools.partial(_moe_kernel, n_real=n_real),
        grid_spec=grid_spec,
        out_shape=jax.ShapeDtypeStruct(((n_real + 2 * rb) * seg, LANE), BF16),
        compiler_params=_cparams(("arbitrary",)),
        name="moe",
    )(bexp, bact, bfirst, bord, bnext, rtok, rtok, rslot, rslot, hn, w_gate, w_up, w_down)


def _dispatch(ids, n, seg):
    k = 2
    m = n * k
    rb = MOE_ROW_BLOCK
    e_flat = ids[:, :k].reshape(m)
    ch = 256
    onehot = (e_flat[:, None] == jnp.arange(N_EXPERTS, dtype=jnp.int32)[None, :]).astype(F32)
    ohc = onehot.reshape(m // ch, ch, N_EXPERTS)
    tri = jnp.tril(jnp.ones((ch, ch), F32))
    within = jnp.einsum('ij,bjk->bik', tri, ohc)
    tot = within[:, -1, :]
    before = jnp.cumsum(tot, axis=0) - tot
    rank = (jnp.sum((within + before[:, None, :]) * ohc, axis=-1).reshape(m) - 1.0).astype(jnp.int32)
    counts = (before[-1] + tot[-1]).astype(jnp.int32)
    padded = (counts + rb - 1) // rb * rb
    pad_end = jnp.cumsum(padded)
    pad_start = pad_end - padded
    dest = jnp.sum(onehot * pad_start.astype(F32)[None, :], axis=-1).astype(jnp.int32) + rank
    nb = (m + N_EXPERTS * (rb - 1) + rb - 1) // rb
    p = nb * rb
    row_m = jnp.full((p,), -1, jnp.int32).at[dest].set(jnp.arange(m, dtype=jnp.int32))
    real = row_m >= 0
    rtok = jnp.where(real, row_m // k, 0)
    pidx = jnp.arange(p, dtype=jnp.int32)
    rslot = jnp.where(real, (row_m % k) * n + row_m // k, m + ((pidx // rb) % 2) * rb + pidx % rb)
    starts = jnp.arange(nb, dtype=jnp.int32) * rb
    bexp = jnp.minimum(jnp.sum((pad_end[None, :] <= starts[:, None]).astype(jnp.int32), axis=1),
                       N_EXPERTS - 1)
    bact = jnp.sum(real.reshape(nb, rb).astype(jnp.int32), axis=1)
    last_e = jnp.max(jnp.where(bact > 0, bexp, 0))
    bexp = jnp.where(bact > 0, bexp, last_e)
    rslot = jnp.concatenate([m + rb + jnp.arange(rb, dtype=jnp.int32), rslot])
    return (bexp, bact, (rtok * seg).reshape(nb, 1, rb), (rslot * seg).reshape(nb + 1, 1, rb),
            m)


def _combine_kernel(h_ref, y0_ref, y1_ref, wts_ref, fw_ref, o_ref, tmp_ref):
    tm = h_ref.shape[0]
    wts = wts_ref[...]
    y0 = _load_token_major(y0_ref, tm, tmp_ref)
    y1 = _load_token_major(y1_ref, tm, tmp_ref)
    moe = y0 * wts[:, 0:1] + y1 * wts[:, 1:2]
    h = h_ref[...] + moe
    o_ref[...] = h * lax.rsqrt(jnp.mean(h * h, axis=1, keepdims=True) + EPS) * fw_ref[...]


def _combine(h1, y, wts, final_w):
    n, d = h1.shape
    seg = d // LANE
    tm = min(512, n)
    nt = n // tm
    return pl.pallas_call(
        _combine_kernel,
        grid=(nt,),
        in_specs=[pl.BlockSpec((tm, d), lambda i: (i, 0)),
                  pl.BlockSpec((tm * seg, LANE), lambda i: (i, 0)),
                  pl.BlockSpec((tm * seg, LANE), lambda i: (i + nt, 0)),
                  pl.BlockSpec((tm, LANE), lambda i: (i, 0)),
                  pl.BlockSpec((1, d), lambda i: (0, 0))],
        out_specs=pl.BlockSpec((tm, d), lambda i: (i, 0)),
        out_shape=jax.ShapeDtypeStruct((n, d), F32),
        scratch_shapes=[pltpu.VMEM((tm * seg, LANE), F32)],
        compiler_params=_cparams(("parallel",)),
        name="combine",
    )(h1, y, y, wts, final_w.reshape(1, d))


def _rope_tables(positions):
    half = ROT_DIM // 2
    inv_freq = ROPE_THETA ** (-jnp.arange(0, ROT_DIM, 2, dtype=F32) / ROT_DIM)
    ang = positions.astype(F32)[..., None] * inv_freq
    cos, sin = jnp.cos(ang), jnp.sin(ang)
    b, t = positions.shape
    ones = jnp.ones((b, t, LANE - ROT_DIM), F32)
    zeros = jnp.zeros((b, t, LANE - half), F32)
    c = jnp.concatenate([cos, cos, ones], axis=-1)
    sa = jnp.concatenate([-sin, zeros], axis=-1)
    sb = jnp.concatenate([jnp.zeros((b, t, half), F32), sin, zeros[..., :LANE - ROT_DIM]], axis=-1)
    return c, sa, sb


def _arrange_w_in(w_in):
    d = w_in.shape[0]
    sizes = (N_HEADS_NSA * HEAD_DIM, 3 * 2 * N_KV_NSA * HEAD_DIM, 3 * N_HEADS_NSA,
             3 * N_HEADS_GDN * HEAD_DIM, N_HEADS_GDN, N_HEADS_GDN, N_HEADS_GDN * HEAD_DIM)
    offs = np.cumsum((0,) + sizes)
    seg = [w_in[:, offs[i]:offs[i + 1]].astype(BF16) for i in range(len(sizes))]
    q, kv, gate, gqkv, gb, ga, gz = seg
    used = sum(sizes)
    pad = jnp.zeros((d, N_CB * LANE - used), BF16)
    return jnp.concatenate([q, kv, gqkv, gz, gate, gb, ga, pad], axis=1)


def kernel(x, positions, attn_norm_w, w_in, cmp_wk, cmp_pek, cmp_wv, cmp_pev, gdn_conv_w, gdn_a_log,
           gdn_dt_bias, gdn_norm_w, w_out, ffn_norm_w, router_group_w, router_group_b,
           router_expert_w, router_expert_b, moe_w_gate, moe_w_up, moe_w_down, final_norm_w):
    b, t, d = x.shape
    n = b * t
    tabs = _rope_tables(positions)
    h = x.reshape(n, d)
    assert w_in.shape[0] == 1, "single-layer block only"
    for l in range(1):
        proj, small = _in_proj(h, attn_norm_w[l], _arrange_w_in(w_in[l]))
        proj3 = proj.reshape(b, t, N_CB * LANE)
        small3 = small.reshape(b, t, LANE)
        cmp_w = jnp.stack([cmp_wk[l], cmp_wv[l]])
        cmp_pe = jnp.stack([cmp_pek[l], cmp_pev[l]])
        kvc = _nsa_compress(proj3, tabs, cmp_w, cmp_pe)
        o_c, selbias = _nsa_cmp_attn(proj3, small3, tabs, kvc)
        o_s = _nsa_sel_attn(proj3, small3, tabs, selbias)
        o_w = _nsa_win_attn(proj3, small3, tabs)
        o_b = _gdn(proj3, small3, gdn_conv_w[l], gdn_a_log[l], gdn_dt_bias[l], gdn_norm_w[l])
        half = N_HEADS_NSA * HEAD_DIM
        wr = jnp.concatenate([router_group_w[l], router_expert_w[l],
                              jnp.zeros((d, LANE - N_GROUPS - N_EXPERTS), F32)], axis=1)
        br = jnp.concatenate([router_group_b[l], router_expert_b[l],
                              jnp.zeros((LANE - N_GROUPS - N_EXPERTS,), F32)]).reshape(1, LANE)
        h1, hn2, ids, wts = _out_proj(o_c.reshape(n, half), o_s.reshape(n, half), o_w.reshape(n, half),
                                      o_b.reshape(n, half), h, w_out[l].astype(BF16), ffn_norm_w[l], wr, br)
        bexp, bact, rtok, rslot, n_slots = _dispatch(ids, n, d // LANE)
        y = _moe(hn2, bexp, bact, rtok, rslot, moe_w_gate[l], moe_w_up[l], moe_w_down[l], n_slots)
        out = _combine(h1, y, wts, final_norm_w)
    return out.reshape(b, t, d)
```

```python
import functools

import numpy as np
import jax
import jax.numpy as jnp
from jax import lax
from jax.experimental import pallas as pl
from jax.experimental.pallas import tpu as pltpu

F32 = jnp.float32
BF16 = jnp.bfloat16

HEAD_DIM = 128
N_HEADS_NSA = 8
N_KV_NSA = 2
N_REP = N_HEADS_NSA // N_KV_NSA
N_HEADS_GDN = 8
ROT_DIM = 32
ROPE_THETA = 500000.0
CMP_LEN = 32
CMP_STRIDE = 16
SLC_LEN = 64
SLC_TOP = 16
WINDOW = 512
CONV_WIDTH = 4
GDN_CHUNK = 64
N_GROUPS = 8
EXPERTS_PER_GROUP = 8
N_EXPERTS = 64
MOE_ROW_BLOCK = 512
EPS = 1e-6
LANE = 128

CB_Q = 0
CB_KV = 8
CB_GQKV = 20
CB_Z = 44
CB_SMALL = 52
N_CB = 54
SMALL_BETA = 24
SMALL_DECAY = 32

NEG_BIAS = -32768.0
VMEM_LIMIT = 56 * 1024 * 1024


def _cparams(sem):
    return pltpu.CompilerParams(dimension_semantics=sem, vmem_limit_bytes=VMEM_LIMIT)


def _mm(a, b):
    return jnp.dot(a.astype(BF16), b.astype(BF16), preferred_element_type=F32)


def _mm_nt(a, b):
    return lax.dot_general(a.astype(BF16), b.astype(BF16), (((1,), (1,)), ((), ())),
                           preferred_element_type=F32)


def _rope(x, c, sa, sb):
    return (x * c + pltpu.roll(x, LANE - ROT_DIM // 2, 1) * sa
            + pltpu.roll(x, ROT_DIM // 2, 1) * sb)


def _sigmoid(x):
    return 0.5 * jnp.tanh(0.5 * x) + 0.5


def _lane_tiles(x):
    return [x[:, i:i + LANE] for i in range(0, x.shape[1], LANE)]


def _store_token_major(ref, x, tmp_ref):
    rows, d = x.shape
    seg = d // LANE
    for s in range(seg):
        tmp_ref[pl.ds(s, rows, stride=seg), :] = x[:, s * LANE:(s + 1) * LANE]
    ref[...] = tmp_ref[...].astype(ref.dtype)


def _load_token_major(ref, rows, tmp_ref):
    seg = ref.shape[0] // rows
    tmp_ref[...] = ref[...].astype(F32)
    return jnp.concatenate([tmp_ref[pl.ds(s, rows, stride=seg), :] for s in range(seg)], axis=1)


def _rowmax(x):
    return jnp.max(functools.reduce(jnp.maximum, _lane_tiles(x)), axis=1, keepdims=True)


def _rowsum(x):
    return jnp.sum(functools.reduce(jnp.add, _lane_tiles(x)), axis=1, keepdims=True)


def _in_proj_kernel(x_ref, nw_ref, w_ref, o_ref, small_ref, hn_ref, *, small_off):
    j = pl.program_id(1)

    @pl.when(j == 0)
    def _():
        x = x_ref[...]
        ms = jnp.mean(x * x, axis=-1, keepdims=True)
        hn_ref[...] = (x * lax.rsqrt(ms + EPS) * nw_ref[...]).astype(BF16)
    acc = jnp.dot(hn_ref[...], w_ref[...], preferred_element_type=F32)
    o_ref[...] = acc.astype(o_ref.dtype)

    @pl.when(j == pl.num_programs(1) - 1)
    def _():
        small_ref[...] = acc[:, small_off:small_off + LANE]


def _in_proj(x2, norm_w, w):
    n, d = x2.shape
    ncol = w.shape[1]
    tm = min(1024, n)
    tn = 2304
    small_off = CB_SMALL * LANE - (ncol // tn - 1) * tn
    assert 0 <= small_off <= tn - LANE
    return pl.pallas_call(
        functools.partial(_in_proj_kernel, small_off=small_off),
        grid=(n // tm, ncol // tn),
        in_specs=[pl.BlockSpec((tm, d), lambda i, j: (i, 0)),
                  pl.BlockSpec((1, d), lambda i, j: (0, 0)),
                  pl.BlockSpec((d, tn), lambda i, j: (0, j))],
        out_specs=[pl.BlockSpec((tm, tn), lambda i, j: (i, j)),
                   pl.BlockSpec((tm, LANE), lambda i, j: (i, 0))],
        out_shape=[jax.ShapeDtypeStruct((n, ncol), BF16), jax.ShapeDtypeStruct((n, LANE), F32)],
        scratch_shapes=[pltpu.VMEM((tm, d), BF16)],
        compiler_params=_cparams(("parallel", "arbitrary")),
        name="in_proj",
    )(x2, norm_w.reshape(1, d), w)


def _compress_kernel(a_ref, c_ref, sa_ref, sb_ref, w_ref, pe_ref, o_ref, xs_ref):
    kv = pl.program_id(1)
    t = a_ref.shape[1]
    nc = t // CMP_STRIDE
    x = a_ref[0].astype(F32)
    xr = _rope(x, c_ref[0], sa_ref[0], sb_ref[0])
    x = jnp.where(kv == 0, xr, x)
    xs_ref[pl.ds(0, t), :] = x
    xs_ref[pl.ds(t, CMP_STRIDE), :] = jnp.zeros((CMP_STRIDE, LANE), F32)
    acc = jnp.zeros((nc, LANE), F32)
    for l in range(CMP_LEN):
        rows = xs_ref[pl.ds(l, nc, stride=CMP_STRIDE), :] + pe_ref[0, pl.ds(l, 1), :]
        acc = acc + _mm(rows, w_ref[0, l])
    o_ref[0, 0, 0] = acc.astype(BF16)


def _nsa_compress(proj3, tabs, cmp_w, cmp_pe):
    b, t, _ = proj3.shape
    g = N_KV_NSA
    nc = t // CMP_STRIDE
    tab_spec = pl.BlockSpec((1, t, LANE), lambda bi, kv, gi: (bi, 0, 0))
    return pl.pallas_call(
        _compress_kernel,
        grid=(b, 2, g),
        in_specs=[pl.BlockSpec((1, t, LANE), lambda bi, kv, gi: (bi, 0, CB_KV + kv * g + gi)),
                  tab_spec, tab_spec, tab_spec,
                  pl.BlockSpec((1, CMP_LEN, LANE, LANE), lambda bi, kv, gi: (kv, 0, 0, 0)),
                  pl.BlockSpec((1, CMP_LEN, LANE), lambda bi, kv, gi: (kv, 0, 0))],
        out_specs=pl.BlockSpec((1, 1, 1, nc, LANE), lambda bi, kv, gi: (bi, kv, gi, 0, 0)),
        out_shape=jax.ShapeDtypeStruct((b, 2, g, nc, LANE), BF16),
        scratch_shapes=[pltpu.VMEM((t + CMP_STRIDE, LANE), F32)],
        compiler_params=_cparams(("parallel", "arbitrary", "arbitrary")),
        name="nsa_compress",
    )(proj3, *tabs, cmp_w, cmp_pe)


def _load_q(q_ref, c, sa, sb):
    scale = HEAD_DIM ** -0.5
    qs = [(_rope(q_ref[0, :, r * LANE:(r + 1) * LANE].astype(F32), c, sa, sb) * scale).astype(BF16)
          for r in range(N_REP)]
    return jnp.concatenate(qs, axis=0)


def _store_gated(o_ref, o, gate_ref, g, branch, tq):
    gt = gate_ref[0]
    for r in range(N_REP):
        col = ((g * N_REP + r) * 3 + branch)
        lane = lax.broadcasted_iota(jnp.int32, gt.shape, 1)
        gcol = jnp.sum(jnp.where(lane == col, gt, 0.0), axis=1, keepdims=True)
        o_ref[0, :, r * LANE:(r + 1) * LANE] = (o[r * tq:(r + 1) * tq] * _sigmoid(gcol)).astype(o_ref.dtype)


Q_TILES_PER_STEP = 4


def _row_views(refs, sub, tq):
    rows = pl.ds(sub * tq, tq)
    return [r.at[:, rows, :] if len(r.shape) == 3 else r.at[:, :, rows, :] for r in refs]


def _cmp_attn_kernel(q_ref, c_ref, sa_ref, sb_ref, kc_ref, vc_ref, gate_ref, o_ref, sel_ref, *, n_slc):
    tq = q_ref.shape[1] // Q_TILES_PER_STEP
    for sub in range(Q_TILES_PER_STEP):
        q, c, sa, sb, gate, o, sel = _row_views(
            (q_ref, c_ref, sa_ref, sb_ref, gate_ref, o_ref, sel_ref), sub, tq)
        _cmp_attn_tile(pl.program_id(2) * Q_TILES_PER_STEP + sub, q, c, sa, sb, kc_ref, vc_ref, gate, o,
                       sel, n_slc=n_slc)


def _cmp_attn_tile(i, q_ref, c_ref, sa_ref, sb_ref, kc_ref, vc_ref, gate_ref, o_ref, sel_ref, *, n_slc):
    g = pl.program_id(1)
    tq = q_ref.shape[1]
    nc = kc_ref.shape[3]
    q4 = _load_q(q_ref, c_ref[0], sa_ref[0], sb_ref[0])
    s = _mm_nt(q4, kc_ref[0, 0, 0])
    row = lax.broadcasted_iota(jnp.int32, s.shape, 0)
    n = lax.broadcasted_iota(jnp.int32, s.shape, 1)
    tpos = i * tq + (row & (tq - 1))
    mask = (n * CMP_STRIDE + (CMP_LEN - 1) <= tpos) & (n < nc - 1)
    sm = jnp.where(mask, s, -1e30)
    m = _rowmax(sm)
    p = jnp.where(mask, jnp.exp(sm - m), 0.0)
    l = _rowsum(p)
    p = p / jnp.maximum(l, 1e-30)
    o = _mm(p, vc_ref[0, 0, 0])
    _store_gated(o_ref, o, gate_ref, g, 0, tq)

    ps = p[0:tq]
    for r in range(1, N_REP):
        ps = ps + p[r * tq:(r + 1) * tq]
    cn = lax.broadcasted_iota(jnp.int32, (nc, LANE), 0)
    cj = lax.broadcasted_iota(jnp.int32, (nc, LANE), 1)
    ratio = SLC_LEN // CMP_STRIDE
    agg = ((cn >= ratio * cj - (CMP_LEN // CMP_STRIDE - 1)) & (cn < ratio * cj + ratio)
           & (cn < nc - 1) & (cj < n_slc))
    agg = jnp.where(agg, 1.0, 0.0).astype(BF16)
    ps_hi = ps.astype(BF16)
    ps_lo = (ps - ps_hi.astype(F32)).astype(BF16)
    imp = (jnp.dot(ps_hi, agg, preferred_element_type=F32)
           + jnp.dot(ps_lo, agg, preferred_element_type=F32))

    j = lax.broadcasted_iota(jnp.int32, (tq, LANE), 1)
    tt = i * tq + lax.broadcasted_iota(jnp.int32, (tq, LANE), 0)
    cur = tt // SLC_LEN
    valid = j <= cur
    forced = (j == 0) | (j == cur) | (j == cur - 1)
    vals = jnp.where(forced, 1e30, jnp.where(valid, imp, -1.0))
    vt = vals.T
    jb = lax.broadcasted_iota(jnp.int32, (n_slc, tq), 0)
    vb = vt[0:n_slc]
    cnt = jnp.zeros((n_slc, tq), F32)
    for jp in range(n_slc):
        cand = vt[jp:jp + 1, :]
        ge = jnp.where(cand >= vb, 1.0, 0.0)
        gt = jnp.where(cand > vb, 1.0, 0.0)
        cnt = cnt + jnp.where(jb > jp, ge, gt)
    keep = jnp.where(cnt < float(min(SLC_TOP, n_slc)), 0.0, NEG_BIAS)
    if n_slc < LANE:
        keep = jnp.concatenate([keep, jnp.full((LANE - n_slc, tq), NEG_BIAS, F32)], axis=0)
    sel_ref[0, 0] = jnp.where(valid, keep.T, NEG_BIAS).astype(BF16)


def _nsa_cmp_attn(proj3, small3, tabs, kvc, tq=128):
    b, t, _ = proj3.shape
    g = N_KV_NSA
    nc = t // CMP_STRIDE
    n_slc = t // SLC_LEN
    rw = N_REP * LANE
    bq = tq * Q_TILES_PER_STEP
    tab_spec = pl.BlockSpec((1, bq, LANE), lambda bi, gi, i: (bi, i, 0))
    return pl.pallas_call(
        functools.partial(_cmp_attn_kernel, n_slc=n_slc),
        grid=(b, g, t // bq),
        in_specs=[pl.BlockSpec((1, bq, rw), lambda bi, gi, i: (bi, i, gi)),
                  tab_spec, tab_spec, tab_spec,
                  pl.BlockSpec((1, 1, 1, nc, LANE), lambda bi, gi, i: (bi, 0, gi, 0, 0)),
                  pl.BlockSpec((1, 1, 1, nc, LANE), lambda bi, gi, i: (bi, 1, gi, 0, 0)),
                  pl.BlockSpec((1, bq, LANE), lambda bi, gi, i: (bi, i, 0))],
        out_specs=[pl.BlockSpec((1, bq, rw), lambda bi, gi, i: (bi, i, gi)),
                   pl.BlockSpec((1, 1, bq, LANE), lambda bi, gi, i: (bi, gi, i, 0))],
        out_shape=[jax.ShapeDtypeStruct((b, t, N_HEADS_NSA * LANE), BF16),
                   jax.ShapeDtypeStruct((b, g, t, LANE), BF16)],
        compiler_params=_cparams(("parallel", "parallel", "parallel")),
        name="nsa_cmp",
    )(proj3, *tabs, kvc, kvc, small3)


def _sel_attn_kernel(q_ref, c_ref, sa_ref, sb_ref, cf_ref, saf_ref, sbf_ref, k_ref, v_ref,
                     sel_ref, gate_ref, o_ref, *scratch, tk):
    tq = q_ref.shape[1] // Q_TILES_PER_STEP
    for sub in range(Q_TILES_PER_STEP):
        q, c, sa, sb, sel, gate, o = _row_views(
            (q_ref, c_ref, sa_ref, sb_ref, sel_ref, gate_ref, o_ref), sub, tq)
        _sel_attn_tile(pl.program_id(2) * Q_TILES_PER_STEP + sub, q, c, sa, sb, cf_ref, saf_ref, sbf_ref,
                       k_ref, v_ref, sel, gate, o, *scratch, tk=tk)


def _sel_attn_tile(i, q_ref, c_ref, sa_ref, sb_ref, cf_ref, saf_ref, sbf_ref, k_ref, v_ref,
                   sel_ref, gate_ref, o_ref, ka_ref, vs_ref, m_ref, l_ref, acc_ref, s0_ref, s1_ref,
                   *, tk):
    g = pl.program_id(1)
    tq = q_ref.shape[1]
    t = k_ref.shape[1]
    rows = N_REP * tq

    @pl.when(i == 0)
    def _():
        kr = _rope(k_ref[0].astype(F32), cf_ref[0], saf_ref[0], sbf_ref[0])
        ka_ref[:, 0:LANE] = kr.astype(BF16)
        pos = lax.broadcasted_iota(jnp.int32, (t, LANE), 0)
        lane = lax.broadcasted_iota(jnp.int32, (t, LANE), 1)
        ka_ref[:, LANE:2 * LANE] = jnp.where(lane == pos // SLC_LEN, 1.0, 0.0).astype(BF16)
        vs_ref[...] = v_ref[0].astype(BF16)

    q4 = _load_q(q_ref, c_ref[0], sa_ref[0], sb_ref[0])
    bias = sel_ref[0, 0]
    qa = jnp.concatenate([q4, jnp.concatenate([bias] * N_REP, axis=0)], axis=1)

    m_ref[...] = jnp.full((rows, LANE), -1e30, F32)
    l_ref[...] = jnp.zeros((rows, LANE), F32)
    acc_ref[...] = jnp.zeros((rows, LANE), F32)

    half = tk // 2

    def scores(kt, hlf):
        k0 = pl.multiple_of(kt * tk + hlf * half, half)
        return _mm_nt(qa, ka_ref[pl.ds(k0, half), :])

    def update(s_ref, kt, hlf, causal):
        k0 = pl.multiple_of(kt * tk + hlf * half, half)
        s = s_ref[...]
        if causal:
            r = lax.broadcasted_iota(jnp.int32, s.shape, 0)
            kp = k0 + lax.broadcasted_iota(jnp.int32, s.shape, 1)
            s = jnp.where(kp <= i * tq + (r & (tq - 1)), s, -1e30)
        tiles = _lane_tiles(s)
        m_old = m_ref[...]
        m_new = jnp.maximum(m_old, _rowmax(s))
        alpha = jnp.exp(m_old - m_new)
        ps = [jnp.exp(tl - m_new) for tl in tiles]
        l_ref[...] = alpha * l_ref[...] + jnp.sum(functools.reduce(jnp.add, ps), axis=1, keepdims=True)
        p = jnp.concatenate([x.astype(BF16) for x in ps], axis=1)
        acc_ref[...] = alpha * acc_ref[...] + jnp.dot(p, vs_ref[pl.ds(k0, half), :],
                                                      preferred_element_type=F32)
        m_ref[...] = m_new

    diag = (i * tq) // tk
    s0_ref[...] = scores(0, 0)

    def body(kt, carry):
        s1_ref[...] = scores(kt, 1)
        update(s0_ref, kt, 0, False)
        s0_ref[...] = scores(kt + 1, 0)
        update(s1_ref, kt, 1, False)
        return carry

    def body2(j, carry):
        body(2 * j, carry)
        body(2 * j + 1, carry)
        return carry

    lax.fori_loop(0, diag // 2, body2, 0)
    lax.fori_loop((diag // 2) * 2, diag, body, 0)
    s1_ref[...] = scores(diag, 1)
    update(s0_ref, diag, 0, True)
    update(s1_ref, diag, 1, True)
    o = acc_ref[...] / l_ref[...]
    _store_gated(o_ref, o, gate_ref, g, 1, tq)


def _nsa_sel_attn(proj3, small3, tabs, selbias, tq=128, tk=512):
    b, t, _ = proj3.shape
    g = N_KV_NSA
    rw = N_REP * LANE
    bq = tq * Q_TILES_PER_STEP
    tk = min(tk, t)
    tab_spec = pl.BlockSpec((1, bq, LANE), lambda bi, gi, i: (bi, i, 0))
    tabf_spec = pl.BlockSpec((1, t, LANE), lambda bi, gi, i: (bi, 0, 0))
    kcb = CB_KV + 1 * 2 * g
    return pl.pallas_call(
        functools.partial(_sel_attn_kernel, tk=tk),
        grid=(b, g, t // bq),
        in_specs=[pl.BlockSpec((1, bq, rw), lambda bi, gi, i: (bi, i, gi)),
                  tab_spec, tab_spec, tab_spec, tabf_spec, tabf_spec, tabf_spec,
                  pl.BlockSpec((1, t, LANE), lambda bi, gi, i: (bi, 0, kcb + gi)),
                  pl.BlockSpec((1, t, LANE), lambda bi, gi, i: (bi, 0, kcb + g + gi)),
                  pl.BlockSpec((1, 1, bq, LANE), lambda bi, gi, i: (bi, gi, i, 0)),
                  pl.BlockSpec((1, bq, LANE), lambda bi, gi, i: (bi, i, 0))],
        out_specs=pl.BlockSpec((1, bq, rw), lambda bi, gi, i: (bi, i, gi)),
        out_shape=jax.ShapeDtypeStruct((b, t, N_HEADS_NSA * LANE), BF16),
        scratch_shapes=[pltpu.VMEM((t, 2 * LANE), BF16), pltpu.VMEM((t, LANE), BF16),
                        pltpu.VMEM((N_REP * tq, LANE), F32), pltpu.VMEM((N_REP * tq, LANE), F32),
                        pltpu.VMEM((N_REP * tq, LANE), F32),
                        pltpu.VMEM((N_REP * tq, tk // 2), F32), pltpu.VMEM((N_REP * tq, tk // 2), F32)],
        compiler_params=_cparams(("parallel", "parallel", "arbitrary")),
        name="nsa_sel",
    )(proj3, *tabs, *tabs, proj3, proj3, selbias, small3)


def _win_attn_kernel(q_ref, c_ref, sa_ref, sb_ref, cf_ref, saf_ref, sbf_ref, k_ref, v_ref,
                     gate_ref, o_ref, ks_ref, vs_ref, *, span):
    tq = q_ref.shape[1] // Q_TILES_PER_STEP
    for sub in range(Q_TILES_PER_STEP):
        q, c, sa, sb, gate, o = _row_views((q_ref, c_ref, sa_ref, sb_ref, gate_ref, o_ref), sub, tq)
        _win_attn_tile(pl.program_id(2) * Q_TILES_PER_STEP + sub, q, c, sa, sb, cf_ref, saf_ref, sbf_ref,
                       k_ref, v_ref, gate, o, ks_ref, vs_ref, span=span)


def _win_attn_tile(i, q_ref, c_ref, sa_ref, sb_ref, cf_ref, saf_ref, sbf_ref, k_ref, v_ref,
                   gate_ref, o_ref, ks_ref, vs_ref, *, span):
    g = pl.program_id(1)
    tq = q_ref.shape[1]

    t = k_ref.shape[1]

    @pl.when(i == 0)
    def _():
        zeros = jnp.zeros((WINDOW, LANE), BF16)
        ks_ref[pl.ds(0, WINDOW), :] = zeros
        vs_ref[pl.ds(0, WINDOW), :] = zeros
        ks_ref[pl.ds(WINDOW, t), :] = _rope(k_ref[0].astype(F32), cf_ref[0], saf_ref[0],
                                            sbf_ref[0]).astype(BF16)
        vs_ref[pl.ds(WINDOW, t), :] = v_ref[0].astype(BF16)

    q4 = _load_q(q_ref, c_ref[0], sa_ref[0], sb_ref[0])
    k0 = pl.multiple_of(i * tq, tq)
    s = _mm_nt(q4, ks_ref[pl.ds(k0, span), :])
    tiles = _lane_tiles(s)
    r = lax.broadcasted_iota(jnp.int32, tiles[0].shape, 0) & (tq - 1)
    c = lax.broadcasted_iota(jnp.int32, tiles[0].shape, 1)
    first_block = WINDOW // tq - i
    masked = []
    for b, tl in enumerate(tiles):
        if b == 0:
            tl = jnp.where(c > r, tl, -1e30)
        if b == len(tiles) - 1:
            tl = jnp.where(c <= r, tl, -1e30)
        else:
            tl = jnp.where(b >= first_block, tl, -1e30)
        masked.append(tl)
    m = jnp.max(functools.reduce(jnp.maximum, masked), axis=1, keepdims=True)
    ps = [jnp.exp(tl - m) for tl in masked]
    l = jnp.sum(functools.reduce(jnp.add, ps), axis=1, keepdims=True)
    p = jnp.concatenate([x.astype(BF16) for x in ps], axis=1)
    o = jnp.dot(p, vs_ref[pl.ds(k0, span), :], preferred_element_type=F32) / l
    _store_gated(o_ref, o, gate_ref, g, 2, tq)


def _nsa_win_attn(proj3, small3, tabs, tq=128):
    b, t, _ = proj3.shape
    g = N_KV_NSA
    rw = N_REP * LANE
    bq = tq * Q_TILES_PER_STEP
    span = WINDOW + tq
    assert tq == LANE
    tab_spec = pl.BlockSpec((1, bq, LANE), lambda bi, gi, i: (bi, i, 0))
    tabf_spec = pl.BlockSpec((1, t, LANE), lambda bi, gi, i: (bi, 0, 0))
    kcb = CB_KV + 2 * 2 * g
    return pl.pallas_call(
        functools.partial(_win_attn_kernel, span=span),
        grid=(b, g, t // bq),
        in_specs=[pl.BlockSpec((1, bq, rw), lambda bi, gi, i: (bi, i, gi)),
                  tab_spec, tab_spec, tab_spec, tabf_spec, tabf_spec, tabf_spec,
                  pl.BlockSpec((1, t, LANE), lambda bi, gi, i: (bi, 0, kcb + gi)),
                  pl.BlockSpec((1, t, LANE), lambda bi, gi, i: (bi, 0, kcb + g + gi)),
                  pl.BlockSpec((1, bq, LANE), lambda bi, gi, i: (bi, i, 0))],
        out_specs=pl.BlockSpec((1, bq, rw), lambda bi, gi, i: (bi, i, gi)),
        out_shape=jax.ShapeDtypeStruct((b, t, N_HEADS_NSA * LANE), BF16),
        scratch_shapes=[pltpu.VMEM((t + WINDOW, LANE), BF16), pltpu.VMEM((t + WINDOW, LANE), BF16)],
        compiler_params=_cparams(("parallel", "parallel", "arbitrary")),
        name="nsa_win",
    )(proj3, *tabs, *tabs, proj3, proj3, small3)


def _gdn_kernel(q_ref, k_ref, v_ref, z_ref, braw_ref, araw_ref, cwq_ref, cwk_ref, cwv_ref, alog_ref,
                dtb_ref, nw_ref, o_ref, q_s, k_s, v_s, o_s, qp_s, op_s, xp_s, bd_s, gcd_s, gl_s, mm_s,
                nn_s):
    t = q_ref.shape[1]
    c = GDN_CHUNK
    nchunk = t // c
    xp_s[pl.ds(0, 8), :] = jnp.zeros((8, LANE), F32)

    def conv_silu(x, cw):
        xp_s[pl.ds(8, t), :] = x
        y = x * cw[CONV_WIDTH - 1:CONV_WIDTH]
        for sft in range(1, CONV_WIDTH):
            y = y + xp_s[pl.ds(8 - sft, t), :] * cw[CONV_WIDTH - 1 - sft:CONV_WIDTH - sft]
        return y * _sigmoid(y)

    def l2n(x):
        return x * lax.rsqrt(jnp.sum(x * x, axis=1, keepdims=True) + EPS)

    q_s[...] = l2n(conv_silu(q_ref[0].astype(F32), cwq_ref[...])) * (HEAD_DIM ** -0.5)
    k_s[...] = l2n(conv_silu(k_ref[0].astype(F32), cwk_ref[...]))
    v_s[...] = conv_silu(v_ref[0].astype(F32), cwv_ref[...])

    bd_s[...] = _sigmoid(braw_ref[0, 0])
    xa = araw_ref[0, 0] + dtb_ref[0]
    softplus = jnp.maximum(xa, 0.0) + jnp.log(1.0 + jnp.exp(-jnp.abs(xa)))
    gc = -jnp.exp(alog_ref[0]) * softplus
    lane = lax.broadcasted_iota(jnp.int32, gc.shape, 1)
    sft = 1
    while sft < c:
        gc = gc + jnp.where((lane & (c - 1)) >= sft, pltpu.roll(gc, sft, 1), 0.0)
        sft *= 2
    gcd_s[...] = gc

    c2 = 2 * c
    ci = lax.broadcasted_iota(jnp.int32, (c2, c2), 0)
    cj = lax.broadcasted_iota(jnp.int32, (c2, c2), 1)
    same = (ci // c) == (cj // c)
    tril = same & (cj <= ci)
    strict = same & (cj < ci)
    eye = jnp.where(ci == cj, 1.0, 0.0)
    first = ci < c

    pairs_per_iter = 8
    rng = range(pairs_per_iter)

    def prep(it, carry):
        n2 = [it * pairs_per_iter + p for p in rng]
        sl = [pl.ds(pl.multiple_of(n * c2, c2), c2) for n in n2]
        grow = [gcd_s[pl.ds(n, 1), :] for n in n2]
        gc2 = [jnp.broadcast_to(g_, (c2, c2)).T for g_ in grow]
        betac = [jnp.broadcast_to(bd_s[pl.ds(n, 1), :], (c2, c2)).T for n in n2]
        kn = [k_s[s_, :] for s_ in sl]
        kbn = [kn[p] * betac[p] for p in rng]
        decay = [jnp.exp(jnp.where(tril, gc2[p] - grow[p], -1e30)) for p in rng]
        kk = [_mm_nt(kbn[p], kn[p]) for p in rng]
        y = [-jnp.where(strict, kk[p] * decay[p], 0.0) for p in rng]
        pm = [eye + y_ for y_ in y]
        pw = 1
        while 2 * pw < c:
            y = [_mm(y_, y_) for y_ in y]
            pm = [pm[p] + _mm(pm[p], y[p]) for p in rng]
            pw *= 2
        egn = [jnp.exp(g_) for g_ in gc2]
        uw = [_mm(pm[p], jnp.concatenate([v_s[sl[p], :] * betac[p], kbn[p] * egn[p]], axis=1))
              for p in rng]
        qn = [q_s[s_, :] for s_ in sl]
        qk = [_mm_nt(qn[p], kn[p]) for p in rng]
        attn = [jnp.where(tril, qk[p] * decay[p], 0.0) for p in rng]
        auw = [_mm(attn[p], uw[p]) for p in rng]
        for p in rng:
            qp_s[sl[p], :] = qn[p] * egn[p] - auw[p][:, LANE:]
            op_s[sl[p], :] = auw[p][:, :LANE]
        gl = [jnp.where(first, g_[c - 1:c, :], g_[c2 - 1:c2, :]) for g_ in gc2]
        kdt = [(kn[p] * jnp.exp(gl[p] - gc2[p])).T for p in rng]
        nma = [_mm(kdt[p][:, 0:c], uw[p][0:c]) for p in rng]
        nmb = [_mm(kdt[p][:, c:c2], uw[p][c:c2]) for p in rng]
        mb = [x[:, LANE:] for x in nmb]
        mbnm = [_mm(mb[p], nma[p]) for p in rng]
        ma = [x[:, LANE:] for x in nma]
        na = [x[:, :LANE] for x in nma]
        nb_ = [x[:, :LANE] for x in nmb]
        mba = [x[:, LANE:] for x in mbnm]
        mbn = [x[:, :LANE] for x in mbnm]
        for p in rng:
            ga = jnp.exp(gc2[p][c - 1:c, :])
            gb = jnp.exp(gc2[p][c2 - 1:c2, :])
            mm_s[2 * n2[p]] = ma[p].astype(BF16)
            nn_s[2 * n2[p]] = na[p]
            gl_s[pl.ds(2 * n2[p], 1), :] = ga
            mm_s[2 * n2[p] + 1] = (gb * ma[p] + ga * mb[p] - mba[p]).astype(BF16)
            nn_s[2 * n2[p] + 1] = gb * na[p] - mbn[p] + nb_[p]
            gl_s[pl.ds(2 * n2[p] + 1, 1), :] = ga * gb
        return carry

    lax.fori_loop(0, nchunk // (2 * pairs_per_iter), prep, 0)

    def second_out(n2, s_mid):
        rows = pl.ds(pl.multiple_of(n2 * c2, c2) + c, c)
        o_s[rows, :] = (jnp.dot(qp_s[rows, :].astype(BF16), s_mid.astype(BF16),
                                 preferred_element_type=F32) + op_s[rows, :])

    def scan(n2, carry):
        s, s_mid_prev = carry
        second_out(jnp.maximum(n2 - 1, 0), s_mid_prev)
        ra = pl.ds(pl.multiple_of(n2 * c2, c2), c)
        sb = s.astype(BF16)
        s_mid = (s * gl_s[pl.ds(2 * n2, 1), :] - jnp.dot(mm_s[2 * n2], sb, preferred_element_type=F32)
                 + nn_s[2 * n2])
        s_new = (s * gl_s[pl.ds(2 * n2 + 1, 1), :]
                 - jnp.dot(mm_s[2 * n2 + 1], sb, preferred_element_type=F32) + nn_s[2 * n2 + 1])
        o_s[ra, :] = jnp.dot(qp_s[ra, :].astype(BF16), sb, preferred_element_type=F32) + op_s[ra, :]
        return s_new, s_mid

    zero_state = jnp.zeros((LANE, LANE), F32)
    _, s_mid_last = lax.fori_loop(0, nchunk // 2, scan, (zero_state, zero_state))
    second_out(jnp.int32(nchunk // 2 - 1), s_mid_last)

    o = o_s[...]
    on = o * lax.rsqrt(jnp.mean(o * o, axis=1, keepdims=True) + EPS) * nw_ref[...]
    z = z_ref[0].astype(F32)
    o_ref[0] = (on * (z * _sigmoid(z))).astype(o_ref.dtype)


def _gdn(proj3, small3, conv_w, a_log, dt_bias, norm_w):
    b, t, _ = proj3.shape
    hh = N_HEADS_GDN
    c = GDN_CHUNK
    col = lambda cb: pl.BlockSpec((1, t, LANE), lambda bi, hi: (bi, 0, cb + hi))
    cw = lambda off: pl.BlockSpec((CONV_WIDTH, LANE), lambda bi, hi: (0, off + hi))
    hrow = pl.BlockSpec((1, 1, LANE), lambda bi, hi: (hi, 0, 0))
    alog_b = jnp.broadcast_to(a_log.astype(F32)[:, None, None], (hh, 1, LANE))
    dtb_b = jnp.broadcast_to(dt_bias.astype(F32)[:, None, None], (hh, 1, LANE))
    big = pltpu.VMEM((t, LANE), F32)
    nrow = t // LANE
    ba = small3[:, :, SMALL_BETA:SMALL_BETA + 2 * hh].transpose(0, 2, 1).reshape(b, 2 * hh, nrow, LANE)
    dense = lambda off: pl.BlockSpec((1, 1, nrow, LANE), lambda bi, hi: (bi, off + hi, 0, 0))
    return pl.pallas_call(
        _gdn_kernel,
        grid=(b, hh),
        in_specs=[col(CB_GQKV), col(CB_GQKV + hh), col(CB_GQKV + 2 * hh), col(CB_Z),
                  dense(0), dense(hh),
                  cw(0), cw(hh), cw(2 * hh), hrow, hrow,
                  pl.BlockSpec((1, LANE), lambda bi, hi: (0, 0))],
        out_specs=pl.BlockSpec((1, t, LANE), lambda bi, hi: (bi, 0, hi)),
        out_shape=jax.ShapeDtypeStruct((b, t, hh * LANE), BF16),
        scratch_shapes=[big, big, big, big, big, big, pltpu.VMEM((t + 8, LANE), F32),
                        pltpu.VMEM((nrow, LANE), F32), pltpu.VMEM((nrow, LANE), F32),
                        pltpu.VMEM((t // c, LANE), F32),
                        pltpu.VMEM((t // c, LANE, LANE), BF16),
                        pltpu.VMEM((t // c, LANE, LANE), F32)],
        compiler_params=_cparams(("parallel", "parallel")),
        name="gdn",
    )(proj3, proj3, proj3, proj3, ba, ba, conv_w, conv_w, conv_w, alog_b, dtb_b,
      norm_w.reshape(1, LANE))


def _split3(a):
    hi = a.astype(BF16)
    lo = (a - hi.astype(F32)).astype(BF16)
    return hi, lo


def _out_proj_kernel(oc_ref, os_ref, ow_ref, ob_ref, x_ref, wo_ref, fw_ref, wr_ref, br_ref,
                     h_ref, hn_ref, ids_ref, wts_ref, tmp_ref):
    half = oc_ref.shape[1]
    oa = (oc_ref[...].astype(F32) + os_ref[...].astype(F32) + ow_ref[...].astype(F32)).astype(BF16)
    h1 = (x_ref[...] + jnp.dot(oa, wo_ref[0:half, :], preferred_element_type=F32)
          + jnp.dot(ob_ref[...], wo_ref[half:2 * half, :], preferred_element_type=F32))
    h_ref[...] = h1
    hn = h1 * lax.rsqrt(jnp.mean(h1 * h1, axis=1, keepdims=True) + EPS) * fw_ref[...]
    _store_token_major(hn_ref, hn, tmp_ref)

    a_hi, a_lo = _split3(hn)
    w_hi, w_lo = _split3(wr_ref[...])
    dot = lambda a, b: jnp.dot(a, b, preferred_element_type=F32)
    logits = dot(a_hi, w_hi) + dot(a_hi, w_lo) + dot(a_lo, w_hi) + br_ref[...]
    lane = lax.broadcasted_iota(jnp.int32, logits.shape, 1)
    big = 1e30
    is_g = lane < N_GROUPS
    lg = jnp.where(is_g, logits, -big)
    gm = jnp.max(lg, axis=1, keepdims=True)
    grp = jnp.min(jnp.where(lg == gm, lane, LANE), axis=1, keepdims=True)
    p_grp = 1.0 / jnp.sum(jnp.where(is_g, jnp.exp(lg - gm), 0.0), axis=1, keepdims=True)
    e_id = lane - N_GROUPS
    in_g = (e_id >= 0) & (e_id < N_EXPERTS) & ((e_id // EXPERTS_PER_GROUP) == grp)
    le = jnp.where(in_g, logits, -big)
    em = jnp.max(le, axis=1, keepdims=True)
    pe = jnp.where(in_g, jnp.exp(le - em), 0.0)
    pe = pe / jnp.sum(pe, axis=1, keepdims=True)
    pm = jnp.where(in_g, pe, -1.0)
    p1 = jnp.max(pm, axis=1, keepdims=True)
    i1 = jnp.min(jnp.where(pm == p1, lane, LANE), axis=1, keepdims=True)
    pm2 = jnp.where(lane == i1, -1.0, pm)
    p2 = jnp.max(pm2, axis=1, keepdims=True)
    i2 = jnp.min(jnp.where(pm2 == p2, lane, LANE), axis=1, keepdims=True)
    den = p1 + p2
    ids_ref[...] = jnp.where(lane == 0, i1 - N_GROUPS, jnp.where(lane == 1, i2 - N_GROUPS, 0))
    wts_ref[...] = jnp.where(lane == 0, p1 / den * p_grp, jnp.where(lane == 1, p2 / den * p_grp, 0.0))


def _out_proj(oc, os_, ow, ob, x2, w_out, ffn_w, wr, br):
    n, d = x2.shape
    half = oc.shape[1]
    seg = d // LANE
    tm = min(512, n)
    row = lambda w: pl.BlockSpec((tm, w), lambda i: (i, 0))
    full = lambda a: pl.BlockSpec(a.shape, lambda i: (0,) * a.ndim, pipeline_mode=pl.Buffered(1))
    fw = ffn_w.reshape(1, d)
    return pl.pallas_call(
        _out_proj_kernel,
        grid=(n // tm,),
        in_specs=[row(half), row(half), row(half), row(half), row(d), full(w_out), full(fw),
                  full(wr), full(br)],
        out_specs=[row(d), pl.BlockSpec((tm * seg, LANE), lambda i: (i, 0)), row(LANE), row(LANE)],
        out_shape=[jax.ShapeDtypeStruct((n, d), F32), jax.ShapeDtypeStruct((n * seg, LANE), BF16),
                   jax.ShapeDtypeStruct((n, LANE), jnp.int32), jax.ShapeDtypeStruct((n, LANE), F32)],
        scratch_shapes=[pltpu.VMEM((tm * seg, LANE), F32)],
        compiler_params=_cparams(("parallel",)),
        name="out_proj",
    )(oc, os_, ow, ob, x2, w_out, fw, wr, br)


def _moe_kernel(bexp_ref, bact_ref, bfirst_ref, bord_ref, bnext_ref, rtok_ref, rtokn_ref, rslotp_ref,
                rslot_ref, hn_hbm, wg_hbm, wu_hbm, wd_hbm, y_hbm, xbuf, ybuf, tmp, wg_st, wu_st, wd_st,
                wgb, wub, wdb, gsem, ssem, wsem, *, n_real):
    i = pl.program_id(0)
    nb = pl.num_programs(0)
    rb = rtok_ref.shape[2]
    seg = ybuf.shape[1] // rb
    slot = i & 1
    other = 1 - slot
    active = bact_ref[i] > 0
    prev_active = jnp.logical_and(i > 0, bact_ref[jnp.maximum(i - 1, 0)] > 0)

    def gather_copy(tok, r, buf):
        return pltpu.make_async_copy(hn_hbm.at[pl.ds(pl.multiple_of(tok, seg), seg), :],
                                     xbuf.at[buf, pl.ds(r * seg, seg), :], gsem.at[buf])

    def scatter_copy(dst, r, buf):
        return pltpu.make_async_copy(ybuf.at[buf, pl.ds(r * seg, seg), :],
                                     y_hbm.at[pl.ds(pl.multiple_of(dst, seg), seg), :], ssem.at[buf])

    def wait_gather(buf):
        pltpu.make_async_copy(hn_hbm.at[pl.ds(0, rb * seg), :], xbuf.at[buf], gsem.at[buf]).wait()

    def wait_scatter(buf):
        pltpu.make_async_copy(ybuf.at[buf], y_hbm.at[pl.ds(0, rb * seg), :], ssem.at[buf]).wait()

    def weight_copies(e, ws):
        return (pltpu.make_async_copy(wg_hbm.at[e], wg_st.at[ws], wsem.at[ws]),
                pltpu.make_async_copy(wu_hbm.at[e], wu_st.at[ws], wsem.at[ws]),
                pltpu.make_async_copy(wd_hbm.at[e], wd_st.at[ws], wsem.at[ws]))

    @pl.when(i == 0)
    def _():
        for cp in weight_copies(bexp_ref[0], 0):
            cp.start(priority=1)
        ybuf[...] = jnp.zeros(ybuf.shape, ybuf.dtype)
        pltpu.make_async_copy(ybuf.at[0], y_hbm.at[pl.ds(n_real * seg, rb * seg), :],
                              ssem.at[0]).start()

        def first(r, carry):
            gather_copy(rtok_ref[0, 0, r], r, 0).start()
            return carry
        lax.fori_loop(0, rb, first, 0)

    @pl.when(jnp.logical_or(i == 0, prev_active))
    def _():
        wait_gather(slot)

    @pl.when(active)
    def _():
        @pl.when(bfirst_ref[i] > 0)
        def _():
            ws = bord_ref[i] & 1
            for cp in weight_copies(bexp_ref[i], ws):
                cp.wait()

            @pl.when(bnext_ref[i] >= 0)
            def _():
                for cp in weight_copies(bnext_ref[i], 1 - ws):
                    cp.start(priority=1)

            wgb[...] = wg_st[ws].astype(BF16)
            wub[...] = wu_st[ws].astype(BF16)
            wdb[...] = wd_st[ws].astype(BF16)

        n_piece = 8
        per = rb // n_piece

        def issue(piece):
            half_n = n_piece // 2
            if piece < half_n:
                for r in range(piece * 2 * per, (piece + 1) * 2 * per):
                    gather_copy(rtokn_ref[0, 0, r], r, other).start()
            else:
                for r in range((piece - half_n) * 2 * per, (piece - half_n + 1) * 2 * per):
                    scatter_copy(rslotp_ref[0, 0, r], r, other).start()

        x = _load_token_major(xbuf.at[slot], rb, tmp).astype(BF16)
        de = wgb.shape[1]
        d = wdb.shape[1]
        hc = de // 2
        acts = []
        for c in range(2):
            issue(2 * c)
            hg = jnp.dot(x, wgb[:, c * hc:(c + 1) * hc], preferred_element_type=F32)
            issue(2 * c + 1)
            hu = jnp.dot(x, wub[:, c * hc:(c + 1) * hc], preferred_element_type=F32)
            acts.append((hg * _sigmoid(hg) * hu).astype(BF16))
        act = jnp.concatenate(acts, axis=1)
        dc = d // 4
        ys = []
        for j in range(4):
            issue(4 + j)
            ys.append(jnp.dot(act, wdb[:, j * dc:(j + 1) * dc], preferred_element_type=F32))
        wait_scatter(slot)
        _store_token_major(ybuf.at[slot], jnp.concatenate(ys, axis=1), tmp)

    def scatter_all(idx_ref, buf):
        def body(r, carry):
            scatter_copy(idx_ref[0, 0, r], r, buf).start()
            return carry
        lax.fori_loop(0, rb, body, 0)

    @pl.when(jnp.logical_and(jnp.logical_not(active), prev_active))
    def _():
        wait_scatter(slot)
        scatter_all(rslotp_ref, other)
        wait_scatter(other)

    @pl.when(jnp.logical_and(i == nb - 1, active))
    def _():
        wait_scatter(other)
        scatter_all(rslot_ref, slot)
        wait_scatter(slot)
        wait_gather(other)


def _moe(hn, bexp, bact, rtok, rslot, w_gate, w_up, w_down, n_real):
    d, de = w_gate.shape[1], w_gate.shape[2]
    seg = d // LANE
    nb = bexp.shape[0]
    rb = MOE_ROW_BLOCK
    idx = jnp.arange(nb, dtype=jnp.int32)
    prev_e = jnp.concatenate([jnp.full((1,), -1, jnp.int32), bexp[:-1]])
    bfirst = jnp.logical_and(bact > 0, jnp.logical_or(idx == 0, bexp != prev_e)).astype(jnp.int32)
    bord = jnp.cumsum(bfirst) - 1
    first_at = jnp.where(bfirst > 0, idx, nb)
    next_first = jnp.concatenate([lax.cummin(first_at[::-1])[::-1][1:], jnp.full((1,), nb, jnp.int32)])
    bnext = jnp.where(next_first < nb, bexp[jnp.minimum(next_first, nb - 1)], -1).astype(jnp.int32)

    smem_rows = pl.BlockSpec((1, 1, rb), lambda i, *_: (i, 0, 0), memory_space=pltpu.SMEM)
    smem_next = pl.BlockSpec((1, 1, rb), lambda i, *_: (jnp.minimum(i + 1, nb - 1), 0, 0),
                             memory_space=pltpu.SMEM)
    smem_cur = pl.BlockSpec((1, 1, rb), lambda i, *_: (i + 1, 0, 0), memory_space=pltpu.SMEM)
    hbm = pl.BlockSpec(memory_space=pl.ANY)
    grid_spec = pltpu.PrefetchScalarGridSpec(
        num_scalar_prefetch=5,
        grid=(nb,),
        in_specs=[smem_rows, smem_next, smem_rows, smem_cur, hbm, hbm, hbm, hbm],
        out_specs=hbm,
        scratch_shapes=[pltpu.VMEM((2, rb * seg, LANE), BF16), pltpu.VMEM((2, rb * seg, LANE), BF16),
                        pltpu.VMEM((rb * seg, LANE), F32),
                        pltpu.VMEM((2, d, de), F32), pltpu.VMEM((2, d, de), F32), pltpu.VMEM((2, de, d), F32),
                        pltpu.VMEM((d, de), BF16), pltpu.VMEM((d, de), BF16), pltpu.VMEM((de, d), BF16),
                        pltpu.SemaphoreType.DMA((2,)), pltpu.SemaphoreType.DMA((2,)),
                        pltpu.SemaphoreType.DMA((2,))])
    return pl.pallas_call(
        functools.partial(_moe_kernel, n_real=n_real),
        grid_spec=grid_spec,
        out_shape=jax.ShapeDtypeStruct(((n_real + 2 * rb) * seg, LANE), BF16),
        compiler_params=_cparams(("arbitrary",)),
        name="moe",
    )(bexp, bact, bfirst, bord, bnext, rtok, rtok, rslot, rslot, hn, w_gate, w_up, w_down)


def _dispatch(ids, n, seg):
    k = 2
    m = n * k
    rb = MOE_ROW_BLOCK
    e_flat = ids[:, :k].reshape(m)
    ch = 256
    onehot = (e_flat[:, None] == jnp.arange(N_EXPERTS, dtype=jnp.int32)[None, :]).astype(F32)
    ohc = onehot.reshape(m // ch, ch, N_EXPERTS)
    tri = jnp.tril(jnp.ones((ch, ch), F32))
    within = jnp.einsum('ij,bjk->bik', tri, ohc)
    tot = within[:, -1, :]
    before = jnp.cumsum(tot, axis=0) - tot
    rank = (jnp.sum((within + before[:, None, :]) * ohc, axis=-1).reshape(m) - 1.0).astype(jnp.int32)
    counts = (before[-1] + tot[-1]).astype(jnp.int32)
    padded = (counts + rb - 1) // rb * rb
    pad_end = jnp.cumsum(padded)
    pad_start = pad_end - padded
    dest = jnp.sum(onehot * pad_start.astype(F32)[None, :], axis=-1).astype(jnp.int32) + rank
    nb = (m + N_EXPERTS * (rb - 1) + rb - 1) // rb
    p = nb * rb
    row_m = jnp.full((p,), -1, jnp.int32).at[dest].set(jnp.arange(m, dtype=jnp.int32))
    real = row_m >= 0
    rtok = jnp.where(real, row_m // k, 0)
    pidx = jnp.arange(p, dtype=jnp.int32)
    rslot = jnp.where(real, (row_m % k) * n + row_m // k, m + ((pidx // rb) % 2) * rb + pidx % rb)
    starts = jnp.arange(nb, dtype=jnp.int32) * rb
    bexp = jnp.minimum(jnp.sum((pad_end[None, :] <= starts[:, None]).astype(jnp.int32), axis=1),
                       N_EXPERTS - 1)
    bact = jnp.sum(real.reshape(nb, rb).astype(jnp.int32), axis=1)
    last_e = jnp.max(jnp.where(bact > 0, bexp, 0))
    bexp = jnp.where(bact > 0, bexp, last_e)
    rslot = jnp.concatenate([m + rb + jnp.arange(rb, dtype=jnp.int32), rslot])
    return (bexp, bact, (rtok * seg).reshape(nb, 1, rb), (rslot * seg).reshape(nb + 1, 1, rb),
            m)


def _combine_kernel(h_ref, y0_ref, y1_ref, wts_ref, fw_ref, o_ref, tmp_ref):
    tm = h_ref.shape[0]
    wts = wts_ref[...]
    y0 = _load_token_major(y0_ref, tm, tmp_ref)
    y1 = _load_token_major(y1_ref, tm, tmp_ref)
    moe = y0 * wts[:, 0:1] + y1 * wts[:, 1:2]
    h = h_ref[...] + moe
    o_ref[...] = h * lax.rsqrt(jnp.mean(h * h, axis=1, keepdims=True) + EPS) * fw_ref[...]


def _combine(h1, y, wts, final_w):
    n, d = h1.shape
    seg = d // LANE
    tm = min(512, n)
    nt = n // tm
    return pl.pallas_call(
        _combine_kernel,
        grid=(nt,),
        in_specs=[pl.BlockSpec((tm, d), lambda i: (i, 0)),
                  pl.BlockSpec((tm * seg, LANE), lambda i: (i, 0)),
                  pl.BlockSpec((tm * seg, LANE), lambda i: (i + nt, 0)),
                  pl.BlockSpec((tm, LANE), lambda i: (i, 0)),
                  pl.BlockSpec((1, d), lambda i: (0, 0))],
        out_specs=pl.BlockSpec((tm, d), lambda i: (i, 0)),
        out_shape=jax.ShapeDtypeStruct((n, d), F32),
        scratch_shapes=[pltpu.VMEM((tm * seg, LANE), F32)],
        compiler_params=_cparams(("parallel",)),
        name="combine",
    )(h1, y, y, wts, final_w.reshape(1, d))


def _rope_tables(positions):
    half = ROT_DIM // 2
    inv_freq = ROPE_THETA ** (-jnp.arange(0, ROT_DIM, 2, dtype=F32) / ROT_DIM)
    ang = positions.astype(F32)[..., None] * inv_freq
    cos, sin = jnp.cos(ang), jnp.sin(ang)
    b, t = positions.shape
    ones = jnp.ones((b, t, LANE - ROT_DIM), F32)
    zeros = jnp.zeros((b, t, LANE - half), F32)
    c = jnp.concatenate([cos, cos, ones], axis=-1)
    sa = jnp.concatenate([-sin, zeros], axis=-1)
    sb = jnp.concatenate([jnp.zeros((b, t, half), F32), sin, zeros[..., :LANE - ROT_DIM]], axis=-1)
    return c, sa, sb


def _arrange_w_in(w_in):
    d = w_in.shape[0]
    sizes = (N_HEADS_NSA * HEAD_DIM, 3 * 2 * N_KV_NSA * HEAD_DIM, 3 * N_HEADS_NSA,
             3 * N_HEADS_GDN * HEAD_DIM, N_HEADS_GDN, N_HEADS_GDN, N_HEADS_GDN * HEAD_DIM)
    offs = np.cumsum((0,) + sizes)
    seg = [w_in[:, offs[i]:offs[i + 1]].astype(BF16) for i in range(len(sizes))]
    q, kv, gate, gqkv, gb, ga, gz = seg
    used = sum(sizes)
    pad = jnp.zeros((d, N_CB * LANE - used), BF16)
    return jnp.concatenate([q, kv, gqkv, gz, gate, gb, ga, pad], axis=1)


def kernel(x, positions, attn_norm_w, w_in, cmp_wk, cmp_pek, cmp_wv, cmp_pev, gdn_conv_w, gdn_a_log,
           gdn_dt_bias, gdn_norm_w, w_out, ffn_norm_w, router_group_w, router_group_b,
           router_expert_w, router_expert_b, moe_w_gate, moe_w_up, moe_w_down, final_norm_w):
    b, t, d = x.shape
    n = b * t
    tabs = _rope_tables(positions)
    h = x.reshape(n, d)
    assert w_in.shape[0] == 1, "single-layer block only"
    for l in range(1):
        proj, small = _in_proj(h, attn_norm_w[l], _arrange_w_in(w_in[l]))
        proj3 = proj.reshape(b, t, N_CB * LANE)
        small3 = small.reshape(b, t, LANE)
        cmp_w = jnp.stack([cmp_wk[l], cmp_wv[l]])
        cmp_pe = jnp.stack([cmp_pek[l], cmp_pev[l]])
        kvc = _nsa_compress(proj3, tabs, cmp_w, cmp_pe)
        o_c, selbias = _nsa_cmp_attn(proj3, small3, tabs, kvc)
        o_s = _nsa_sel_attn(proj3, small3, tabs, selbias)
        o_w = _nsa_win_attn(proj3, small3, tabs)
        o_b = _gdn(proj3, small3, gdn_conv_w[l], gdn_a_log[l], gdn_dt_bias[l], gdn_norm_w[l])
        half = N_HEADS_NSA * HEAD_DIM
        wr = jnp.concatenate([router_group_w[l], router_expert_w[l],
                              jnp.zeros((d, LANE - N_GROUPS - N_EXPERTS), F32)], axis=1)
        br = jnp.concatenate([router_group_b[l], router_expert_b[l],
                              jnp.zeros((LANE - N_GROUPS - N_EXPERTS,), F32)]).reshape(1, LANE)
        h1, hn2, ids, wts = _out_proj(o_c.reshape(n, half), o_s.reshape(n, half), o_w.reshape(n, half),
                                      o_b.reshape(n, half), h, w_out[l].astype(BF16), ffn_norm_w[l], wr, br)
        bexp, bact, rtok, rslot, n_slots = _dispatch(ids, n, d // LANE)
        y = _moe(hn2, bexp, bact, rtok, rslot, moe_w_gate[l], moe_w_up[l], moe_w_down[l], n_slots)
        out = _combine(h1, y, wts, final_norm_w)
    return out.reshape(b, t, d)
```

```python
import functools

import numpy as np
import jax
import jax.numpy as jnp
from jax import lax
from jax.experimental import pallas as pl
from jax.experimental.pallas import tpu as pltpu

F32 = jnp.float32
BF16 = jnp.bfloat16

HEAD_DIM = 128
N_HEADS_NSA = 8
N_KV_NSA = 2
N_REP = N_HEADS_NSA // N_KV_NSA
N_HEADS_GDN = 8
ROT_DIM = 32
ROPE_THETA = 500000.0
CMP_LEN = 32
CMP_STRIDE = 16
SLC_LEN = 64
SLC_TOP = 16
WINDOW = 512
CONV_WIDTH = 4
GDN_CHUNK = 64
N_GROUPS = 8
EXPERTS_PER_GROUP = 8
N_EXPERTS = 64
MOE_ROW_BLOCK = 256
EPS = 1e-6
LANE = 128

CB_Q = 0
CB_KV = 8
CB_GQKV = 20
CB_Z = 44
CB_SMALL = 52
N_CB = 54
SMALL_BETA = 24
SMALL_DECAY = 32

NEG_BIAS = -32768.0
VMEM_LIMIT = 56 * 1024 * 1024


def _cparams(sem):
    return pltpu.CompilerParams(dimension_semantics=sem, vmem_limit_bytes=VMEM_LIMIT)


def _mm(a, b):
    return jnp.dot(a.astype(BF16), b.astype(BF16), preferred_element_type=F32)


def _mm_nt(a, b):
    return lax.dot_general(a.astype(BF16), b.astype(BF16), (((1,), (1,)), ((), ())),
                           preferred_element_type=F32)


def _rope(x, c, sa, sb):
    return (x * c + pltpu.roll(x, LANE - ROT_DIM // 2, 1) * sa
            + pltpu.roll(x, ROT_DIM // 2, 1) * sb)


def _sigmoid(x):
    return 0.5 * jnp.tanh(0.5 * x) + 0.5


def _lane_tiles(x):
    return [x[:, i:i + LANE] for i in range(0, x.shape[1], LANE)]


def _store_token_major(ref, x, tmp_ref):
    rows, d = x.shape
    seg = d // LANE
    for s in range(seg):
        tmp_ref[pl.ds(s, rows, stride=seg), :] = x[:, s * LANE:(s + 1) * LANE]
    ref[...] = tmp_ref[...].astype(ref.dtype)


def _load_token_major(ref, rows, tmp_ref):
    seg = ref.shape[0] // rows
    tmp_ref[...] = ref[...].astype(F32)
    return jnp.concatenate([tmp_ref[pl.ds(s, rows, stride=seg), :] for s in range(seg)], axis=1)


def _rowmax(x):
    return jnp.max(functools.reduce(jnp.maximum, _lane_tiles(x)), axis=1, keepdims=True)


def _rowsum(x):
    return jnp.sum(functools.reduce(jnp.add, _lane_tiles(x)), axis=1, keepdims=True)


def _in_proj_kernel(x_ref, nw_ref, w_ref, o_ref, small_ref, hn_ref, *, small_off):
    j = pl.program_id(1)

    @pl.when(j == 0)
    def _():
        x = x_ref[...]
        ms = jnp.mean(x * x, axis=-1, keepdims=True)
        hn_ref[...] = (x * lax.rsqrt(ms + EPS) * nw_ref[...]).astype(BF16)
    acc = jnp.dot(hn_ref[...], w_ref[...], preferred_element_type=F32)
    o_ref[...] = acc.astype(o_ref.dtype)

    @pl.when(j == pl.num_programs(1) - 1)
    def _():
        small_ref[...] = acc[:, small_off:small_off + LANE]


def _in_proj(x2, norm_w, w):
    n, d = x2.shape
    ncol = w.shape[1]
    tm = min(1024, n)
    tn = 2304
    small_off = CB_SMALL * LANE - (ncol // tn - 1) * tn
    assert 0 <= small_off <= tn - LANE
    return pl.pallas_call(
        functools.partial(_in_proj_kernel, small_off=small_off),
        grid=(n // tm, ncol // tn),
        in_specs=[pl.BlockSpec((tm, d), lambda i, j: (i, 0)),
                  pl.BlockSpec((1, d), lambda i, j: (0, 0)),
                  pl.BlockSpec((d, tn), lambda i, j: (0, j))],
        out_specs=[pl.BlockSpec((tm, tn), lambda i, j: (i, j)),
                   pl.BlockSpec((tm, LANE), lambda i, j: (i, 0))],
        out_shape=[jax.ShapeDtypeStruct((n, ncol), BF16), jax.ShapeDtypeStruct((n, LANE), F32)],
        scratch_shapes=[pltpu.VMEM((tm, d), BF16)],
        compiler_params=_cparams(("parallel", "arbitrary")),
        name="in_proj",
    )(x2, norm_w.reshape(1, d), w)


def _compress_kernel(a_ref, c_ref, sa_ref, sb_ref, w_ref, pe_ref, o_ref, xs_ref):
    kv = pl.program_id(1)
    t = a_ref.shape[1]
    nc = t // CMP_STRIDE
    x = a_ref[0].astype(F32)
    xr = _rope(x, c_ref[0], sa_ref[0], sb_ref[0])
    x = jnp.where(kv == 0, xr, x)
    xs_ref[pl.ds(0, t), :] = x
    xs_ref[pl.ds(t, CMP_STRIDE), :] = jnp.zeros((CMP_STRIDE, LANE), F32)
    acc = jnp.zeros((nc, LANE), F32)
    for l in range(CMP_LEN):
        rows = xs_ref[pl.ds(l, nc, stride=CMP_STRIDE), :] + pe_ref[0, pl.ds(l, 1), :]
        acc = acc + _mm(rows, w_ref[0, l])
    o_ref[0, 0, 0] = acc.astype(BF16)


def _nsa_compress(proj3, tabs, cmp_w, cmp_pe):
    b, t, _ = proj3.shape
    g = N_KV_NSA
    nc = t // CMP_STRIDE
    tab_spec = pl.BlockSpec((1, t, LANE), lambda bi, kv, gi: (bi, 0, 0))
    return pl.pallas_call(
        _compress_kernel,
        grid=(b, 2, g),
        in_specs=[pl.BlockSpec((1, t, LANE), lambda bi, kv, gi: (bi, 0, CB_KV + kv * g + gi)),
                  tab_spec, tab_spec, tab_spec,
                  pl.BlockSpec((1, CMP_LEN, LANE, LANE), lambda bi, kv, gi: (kv, 0, 0, 0)),
                  pl.BlockSpec((1, CMP_LEN, LANE), lambda bi, kv, gi: (kv, 0, 0))],
        out_specs=pl.BlockSpec((1, 1, 1, nc, LANE), lambda bi, kv, gi: (bi, kv, gi, 0, 0)),
        out_shape=jax.ShapeDtypeStruct((b, 2, g, nc, LANE), BF16),
        scratch_shapes=[pltpu.VMEM((t + CMP_STRIDE, LANE), F32)],
        compiler_params=_cparams(("parallel", "arbitrary", "arbitrary")),
        name="nsa_compress",
    )(proj3, *tabs, cmp_w, cmp_pe)


def _load_q(q_ref, c, sa, sb):
    scale = HEAD_DIM ** -0.5
    qs = [(_rope(q_ref[0, :, r * LANE:(r + 1) * LANE].astype(F32), c, sa, sb) * scale).astype(BF16)
          for r in range(N_REP)]
    return jnp.concatenate(qs, axis=0)


def _store_gated(o_ref, o, gate_ref, g, branch, tq):
    gt = gate_ref[0]
    for r in range(N_REP):
        col = ((g * N_REP + r) * 3 + branch)
        lane = lax.broadcasted_iota(jnp.int32, gt.shape, 1)
        gcol = jnp.sum(jnp.where(lane == col, gt, 0.0), axis=1, keepdims=True)
        o_ref[0, :, r * LANE:(r + 1) * LANE] = (o[r * tq:(r + 1) * tq] * _sigmoid(gcol)).astype(o_ref.dtype)


Q_TILES_PER_STEP = 4


def _row_views(refs, sub, tq):
    rows = pl.ds(sub * tq, tq)
    return [r.at[:, rows, :] if len(r.shape) == 3 else r.at[:, :, rows, :] for r in refs]


def _cmp_attn_kernel(q_ref, c_ref, sa_ref, sb_ref, kc_ref, vc_ref, gate_ref, o_ref, sel_ref, *, n_slc):
    tq = q_ref.shape[1] // Q_TILES_PER_STEP
    for sub in range(Q_TILES_PER_STEP):
        q, c, sa, sb, gate, o, sel = _row_views(
            (q_ref, c_ref, sa_ref, sb_ref, gate_ref, o_ref, sel_ref), sub, tq)
        _cmp_attn_tile(pl.program_id(2) * Q_TILES_PER_STEP + sub, q, c, sa, sb, kc_ref, vc_ref, gate, o,
                       sel, n_slc=n_slc)


def _cmp_attn_tile(i, q_ref, c_ref, sa_ref, sb_ref, kc_ref, vc_ref, gate_ref, o_ref, sel_ref, *, n_slc):
    g = pl.program_id(1)
    tq = q_ref.shape[1]
    nc = kc_ref.shape[3]
    q4 = _load_q(q_ref, c_ref[0], sa_ref[0], sb_ref[0])
    s = _mm_nt(q4, kc_ref[0, 0, 0])
    row = lax.broadcasted_iota(jnp.int32, s.shape, 0)
    n = lax.broadcasted_iota(jnp.int32, s.shape, 1)
    tpos = i * tq + (row & (tq - 1))
    mask = (n * CMP_STRIDE + (CMP_LEN - 1) <= tpos) & (n < nc - 1)
    sm = jnp.where(mask, s, -1e30)
    m = _rowmax(sm)
    p = jnp.where(mask, jnp.exp(sm - m), 0.0)
    l = _rowsum(p)
    p = p / jnp.maximum(l, 1e-30)
    o = _mm(p, vc_ref[0, 0, 0])
    _store_gated(o_ref, o, gate_ref, g, 0, tq)

    ps = p[0:tq]
    for r in range(1, N_REP):
        ps = ps + p[r * tq:(r + 1) * tq]
    cn = lax.broadcasted_iota(jnp.int32, (nc, LANE), 0)
    cj = lax.broadcasted_iota(jnp.int32, (nc, LANE), 1)
    ratio = SLC_LEN // CMP_STRIDE
    agg = ((cn >= ratio * cj - (CMP_LEN // CMP_STRIDE - 1)) & (cn < ratio * cj + ratio)
           & (cn < nc - 1) & (cj < n_slc))
    agg = jnp.where(agg, 1.0, 0.0).astype(BF16)
    ps_hi = ps.astype(BF16)
    ps_lo = (ps - ps_hi.astype(F32)).astype(BF16)
    imp = (jnp.dot(ps_hi, agg, preferred_element_type=F32)
           + jnp.dot(ps_lo, agg, preferred_element_type=F32))

    j = lax.broadcasted_iota(jnp.int32, (tq, LANE), 1)
    tt = i * tq + lax.broadcasted_iota(jnp.int32, (tq, LANE), 0)
    cur = tt // SLC_LEN
    valid = j <= cur
    forced = (j == 0) | (j == cur) | (j == cur - 1)
    vals = jnp.where(forced, 1e30, jnp.where(valid, imp, -1.0))
    vt = vals.T
    jb = lax.broadcasted_iota(jnp.int32, (n_slc, tq), 0)
    vb = vt[0:n_slc]
    cnt = jnp.zeros((n_slc, tq), F32)
    for jp in range(n_slc):
        cand = vt[jp:jp + 1, :]
        ge = jnp.where(cand >= vb, 1.0, 0.0)
        gt = jnp.where(cand > vb, 1.0, 0.0)
        cnt = cnt + jnp.where(jb > jp, ge, gt)
    keep = jnp.where(cnt < float(min(SLC_TOP, n_slc)), 0.0, NEG_BIAS)
    if n_slc < LANE:
        keep = jnp.concatenate([keep, jnp.full((LANE - n_slc, tq), NEG_BIAS, F32)], axis=0)
    sel_ref[0, 0] = jnp.where(valid, keep.T, NEG_BIAS).astype(BF16)


def _nsa_cmp_attn(proj3, small3, tabs, kvc, tq=128):
    b, t, _ = proj3.shape
    g = N_KV_NSA
    nc = t // CMP_STRIDE
    n_slc = t // SLC_LEN
    rw = N_REP * LANE
    bq = tq * Q_TILES_PER_STEP
    tab_spec = pl.BlockSpec((1, bq, LANE), lambda bi, gi, i: (bi, i, 0))
    return pl.pallas_call(
        functools.partial(_cmp_attn_kernel, n_slc=n_slc),
        grid=(b, g, t // bq),
        in_specs=[pl.BlockSpec((1, bq, rw), lambda bi, gi, i: (bi, i, gi)),
                  tab_spec, tab_spec, tab_spec,
                  pl.BlockSpec((1, 1, 1, nc, LANE), lambda bi, gi, i: (bi, 0, gi, 0, 0)),
                  pl.BlockSpec((1, 1, 1, nc, LANE), lambda bi, gi, i: (bi, 1, gi, 0, 0)),
                  pl.BlockSpec((1, bq, LANE), lambda bi, gi, i: (bi, i, 0))],
        out_specs=[pl.BlockSpec((1, bq, rw), lambda bi, gi, i: (bi, i, gi)),
                   pl.BlockSpec((1, 1, bq, LANE), lambda bi, gi, i: (bi, gi, i, 0))],
        out_shape=[jax.ShapeDtypeStruct((b, t, N_HEADS_NSA * LANE), BF16),
                   jax.ShapeDtypeStruct((b, g, t, LANE), BF16)],
        compiler_params=_cparams(("parallel", "parallel", "parallel")),
        name="nsa_cmp",
    )(proj3, *tabs, kvc, kvc, small3)


def _sel_attn_kernel(q_ref, c_ref, sa_ref, sb_ref, cf_ref, saf_ref, sbf_ref, k_ref, v_ref,
                     sel_ref, gate_ref, o_ref, *scratch, tk):
    tq = q_ref.shape[1] // Q_TILES_PER_STEP
    for sub in range(Q_TILES_PER_STEP):
        q, c, sa, sb, sel, gate, o = _row_views(
            (q_ref, c_ref, sa_ref, sb_ref, sel_ref, gate_ref, o_ref), sub, tq)
        _sel_attn_tile(pl.program_id(2) * Q_TILES_PER_STEP + sub, q, c, sa, sb, cf_ref, saf_ref, sbf_ref,
                       k_ref, v_ref, sel, gate, o, *scratch, tk=tk)


def _sel_attn_tile(i, q_ref, c_ref, sa_ref, sb_ref, cf_ref, saf_ref, sbf_ref, k_ref, v_ref,
                   sel_ref, gate_ref, o_ref, ka_ref, vs_ref, m_ref, l_ref, acc_ref, s0_ref, s1_ref,
                   *, tk):
    g = pl.program_id(1)
    tq = q_ref.shape[1]
    t = k_ref.shape[1]
    rows = N_REP * tq

    @pl.when(i == 0)
    def _():
        kr = _rope(k_ref[0].astype(F32), cf_ref[0], saf_ref[0], sbf_ref[0])
        ka_ref[:, 0:LANE] = kr.astype(BF16)
        pos = lax.broadcasted_iota(jnp.int32, (t, LANE), 0)
        lane = lax.broadcasted_iota(jnp.int32, (t, LANE), 1)
        ka_ref[:, LANE:2 * LANE] = jnp.where(lane == pos // SLC_LEN, 1.0, 0.0).astype(BF16)
        vs_ref[...] = v_ref[0].astype(BF16)

    q4 = _load_q(q_ref, c_ref[0], sa_ref[0], sb_ref[0])
    bias = sel_ref[0, 0]
    qa = jnp.concatenate([q4, jnp.concatenate([bias] * N_REP, axis=0)], axis=1)

    m_ref[...] = jnp.full((rows, LANE), -1e30, F32)
    l_ref[...] = jnp.zeros((rows, LANE), F32)
    acc_ref[...] = jnp.zeros((rows, LANE), F32)

    half = tk // 2

    def scores(kt, hlf):
        k0 = pl.multiple_of(kt * tk + hlf * half, half)
        return _mm_nt(qa, ka_ref[pl.ds(k0, half), :])

    def update(s_ref, kt, hlf, causal):
        k0 = pl.multiple_of(kt * tk + hlf * half, half)
        s = s_ref[...]
        if causal:
            r = lax.broadcasted_iota(jnp.int32, s.shape, 0)
            kp = k0 + lax.broadcasted_iota(jnp.int32, s.shape, 1)
            s = jnp.where(kp <= i * tq + (r & (tq - 1)), s, -1e30)
        tiles = _lane_tiles(s)
        m_old = m_ref[...]
        m_new = jnp.maximum(m_old, _rowmax(s))
        alpha = jnp.exp(m_old - m_new)
        ps = [jnp.exp(tl - m_new) for tl in tiles]
        l_ref[...] = alpha * l_ref[...] + jnp.sum(functools.reduce(jnp.add, ps), axis=1, keepdims=True)
        p = jnp.concatenate([x.astype(BF16) for x in ps], axis=1)
        acc_ref[...] = alpha * acc_ref[...] + jnp.dot(p, vs_ref[pl.ds(k0, half), :],
                                                      preferred_element_type=F32)
        m_ref[...] = m_new

    diag = (i * tq) // tk
    s0_ref[...] = scores(0, 0)

    def body(kt, carry):
        s1_ref[...] = scores(kt, 1)
        update(s0_ref, kt, 0, False)
        s0_ref[...] = scores(kt + 1, 0)
        update(s1_ref, kt, 1, False)
        return carry

    def body2(j, carry):
        body(2 * j, carry)
        body(2 * j + 1, carry)
        return carry

    lax.fori_loop(0, diag // 2, body2, 0)
    lax.fori_loop((diag // 2) * 2, diag, body, 0)
    s1_ref[...] = scores(diag, 1)
    update(s0_ref, diag, 0, True)
    update(s1_ref, diag, 1, True)
    o = acc_ref[...] / l_ref[...]
    _store_gated(o_ref, o, gate_ref, g, 1, tq)


def _nsa_sel_attn(proj3, small3, tabs, selbias, tq=128, tk=512):
    b, t, _ = proj3.shape
    g = N_KV_NSA
    rw = N_REP * LANE
    bq = tq * Q_TILES_PER_STEP
    tk = min(tk, t)
    tab_spec = pl.BlockSpec((1, bq, LANE), lambda bi, gi, i: (bi, i, 0))
    tabf_spec = pl.BlockSpec((1, t, LANE), lambda bi, gi, i: (bi, 0, 0))
    kcb = CB_KV + 1 * 2 * g
    return pl.pallas_call(
        functools.partial(_sel_attn_kernel, tk=tk),
        grid=(b, g, t // bq),
        in_specs=[pl.BlockSpec((1, bq, rw), lambda bi, gi, i: (bi, i, gi)),
                  tab_spec, tab_spec, tab_spec, tabf_spec, tabf_spec, tabf_spec,
                  pl.BlockSpec((1, t, LANE), lambda bi, gi, i: (bi, 0, kcb + gi)),
                  pl.BlockSpec((1, t, LANE), lambda bi, gi, i: (bi, 0, kcb + g + gi)),
                  pl.BlockSpec((1, 1, bq, LANE), lambda bi, gi, i: (bi, gi, i, 0)),
                  pl.BlockSpec((1, bq, LANE), lambda bi, gi, i: (bi, i, 0))],
        out_specs=pl.BlockSpec((1, bq, rw), lambda bi, gi, i: (bi, i, gi)),
        out_shape=jax.ShapeDtypeStruct((b, t, N_HEADS_NSA * LANE), BF16),
        scratch_shapes=[pltpu.VMEM((t, 2 * LANE), BF16), pltpu.VMEM((t, LANE), BF16),
                        pltpu.VMEM((N_REP * tq, LANE), F32), pltpu.VMEM((N_REP * tq, LANE), F32),
                        pltpu.VMEM((N_REP * tq, LANE), F32),
                        pltpu.VMEM((N_REP * tq, tk // 2), F32), pltpu.VMEM((N_REP * tq, tk // 2), F32)],
        compiler_params=_cparams(("parallel", "parallel", "arbitrary")),
        name="nsa_sel",
    )(proj3, *tabs, *tabs, proj3, proj3, selbias, small3)


def _win_attn_kernel(q_ref, c_ref, sa_ref, sb_ref, cf_ref, saf_ref, sbf_ref, k_ref, v_ref,
                     gate_ref, o_ref, ks_ref, vs_ref, *, span):
    tq = q_ref.shape[1] // Q_TILES_PER_STEP
    for sub in range(Q_TILES_PER_STEP):
        q, c, sa, sb, gate, o = _row_views((q_ref, c_ref, sa_ref, sb_ref, gate_ref, o_ref), sub, tq)
        _win_attn_tile(pl.program_id(2) * Q_TILES_PER_STEP + sub, q, c, sa, sb, cf_ref, saf_ref, sbf_ref,
                       k_ref, v_ref, gate, o, ks_ref, vs_ref, span=span)


def _win_attn_tile(i, q_ref, c_ref, sa_ref, sb_ref, cf_ref, saf_ref, sbf_ref, k_ref, v_ref,
                   gate_ref, o_ref, ks_ref, vs_ref, *, span):
    g = pl.program_id(1)
    tq = q_ref.shape[1]

    t = k_ref.shape[1]

    @pl.when(i == 0)
    def _():
        zeros = jnp.zeros((WINDOW, LANE), BF16)
        ks_ref[pl.ds(0, WINDOW), :] = zeros
        vs_ref[pl.ds(0, WINDOW), :] = zeros
        ks_ref[pl.ds(WINDOW, t), :] = _rope(k_ref[0].astype(F32), cf_ref[0], saf_ref[0],
                                            sbf_ref[0]).astype(BF16)
        vs_ref[pl.ds(WINDOW, t), :] = v_ref[0].astype(BF16)

    q4 = _load_q(q_ref, c_ref[0], sa_ref[0], sb_ref[0])
    k0 = pl.multiple_of(i * tq, tq)
    s = _mm_nt(q4, ks_ref[pl.ds(k0, span), :])
    tiles = _lane_tiles(s)
    r = lax.broadcasted_iota(jnp.int32, tiles[0].shape, 0) & (tq - 1)
    c = lax.broadcasted_iota(jnp.int32, tiles[0].shape, 1)
    first_block = WINDOW // tq - i
    masked = []
    for b, tl in enumerate(tiles):
        if b == 0:
            tl = jnp.where(c > r, tl, -1e30)
        if b == len(tiles) - 1:
            tl = jnp.where(c <= r, tl, -1e30)
        else:
            tl = jnp.where(b >= first_block, tl, -1e30)
        masked.append(tl)
    m = jnp.max(functools.reduce(jnp.maximum, masked), axis=1, keepdims=True)
    ps = [jnp.exp(tl - m) for tl in masked]
    l = jnp.sum(functools.reduce(jnp.add, ps), axis=1, keepdims=True)
    p = jnp.concatenate([x.astype(BF16) for x in ps], axis=1)
    o = jnp.dot(p, vs_ref[pl.ds(k0, span), :], preferred_element_type=F32) / l
    _store_gated(o_ref, o, gate_ref, g, 2, tq)


def _nsa_win_attn(proj3, small3, tabs, tq=128):
    b, t, _ = proj3.shape
    g = N_KV_NSA
    rw = N_REP * LANE
    bq = tq * Q_TILES_PER_STEP
    span = WINDOW + tq
    assert tq == LANE
    tab_spec = pl.BlockSpec((1, bq, LANE), lambda bi, gi, i: (bi, i, 0))
    tabf_spec = pl.BlockSpec((1, t, LANE), lambda bi, gi, i: (bi, 0, 0))
    kcb = CB_KV + 2 * 2 * g
    return pl.pallas_call(
        functools.partial(_win_attn_kernel, span=span),
        grid=(b, g, t // bq),
        in_specs=[pl.BlockSpec((1, bq, rw), lambda bi, gi, i: (bi, i, gi)),
                  tab_spec, tab_spec, tab_spec, tabf_spec, tabf_spec, tabf_spec,
                  pl.BlockSpec((1, t, LANE), lambda bi, gi, i: (bi, 0, kcb + gi)),
                  pl.BlockSpec((1, t, LANE), lambda bi, gi, i: (bi, 0, kcb + g + gi)),
                  pl.BlockSpec((1, bq, LANE), lambda bi, gi, i: (bi, i, 0))],
        out_specs=pl.BlockSpec((1, bq, rw), lambda bi, gi, i: (bi, i, gi)),
        out_shape=jax.ShapeDtypeStruct((b, t, N_HEADS_NSA * LANE), BF16),
        scratch_shapes=[pltpu.VMEM((t + WINDOW, LANE), BF16), pltpu.VMEM((t + WINDOW, LANE), BF16)],
        compiler_params=_cparams(("parallel", "parallel", "arbitrary")),
        name="nsa_win",
    )(proj3, *tabs, *tabs, proj3, proj3, small3)


def _gdn_kernel(q_ref, k_ref, v_ref, z_ref, braw_ref, araw_ref, cwq_ref, cwk_ref, cwv_ref, alog_ref,
                dtb_ref, nw_ref, o_ref, q_s, k_s, v_s, o_s, qp_s, op_s, xp_s, bd_s, gcd_s, gl_s, mm_s,
                nn_s):
    t = q_ref.shape[1]
    c = GDN_CHUNK
    nchunk = t // c
    xp_s[pl.ds(0, 8), :] = jnp.zeros((8, LANE), F32)

    def conv_silu(x, cw):
        xp_s[pl.ds(8, t), :] = x
        y = x * cw[CONV_WIDTH - 1:CONV_WIDTH]
        for sft in range(1, CONV_WIDTH):
            y = y + xp_s[pl.ds(8 - sft, t), :] * cw[CONV_WIDTH - 1 - sft:CONV_WIDTH - sft]
        return y * _sigmoid(y)

    def l2n(x):
        return x * lax.rsqrt(jnp.sum(x * x, axis=1, keepdims=True) + EPS)

    q_s[...] = l2n(conv_silu(q_ref[0].astype(F32), cwq_ref[...])) * (HEAD_DIM ** -0.5)
    k_s[...] = l2n(conv_silu(k_ref[0].astype(F32), cwk_ref[...]))
    v_s[...] = conv_silu(v_ref[0].astype(F32), cwv_ref[...])

    bd_s[...] = _sigmoid(braw_ref[0, 0])
    xa = araw_ref[0, 0] + dtb_ref[0]
    softplus = jnp.maximum(xa, 0.0) + jnp.log(1.0 + jnp.exp(-jnp.abs(xa)))
    gc = -jnp.exp(alog_ref[0]) * softplus
    lane = lax.broadcasted_iota(jnp.int32, gc.shape, 1)
    sft = 1
    while sft < c:
        gc = gc + jnp.where((lane & (c - 1)) >= sft, pltpu.roll(gc, sft, 1), 0.0)
        sft *= 2
    gcd_s[...] = gc

    c2 = 2 * c
    ci = lax.broadcasted_iota(jnp.int32, (c2, c2), 0)
    cj = lax.broadcasted_iota(jnp.int32, (c2, c2), 1)
    same = (ci // c) == (cj // c)
    tril = same & (cj <= ci)
    strict = same & (cj < ci)
    eye = jnp.where(ci == cj, 1.0, 0.0)
    first = ci < c

    pairs_per_iter = 16
    rng = range(pairs_per_iter)

    def prep(it, carry):
        n2 = [it * pairs_per_iter + p for p in rng]
        sl = [pl.ds(pl.multiple_of(n * c2, c2), c2) for n in n2]
        grow = [gcd_s[pl.ds(n, 1), :] for n in n2]
        gc2 = [jnp.broadcast_to(g_, (c2, c2)).T for g_ in grow]
        betac = [jnp.broadcast_to(bd_s[pl.ds(n, 1), :], (c2, c2)).T for n in n2]
        kn = [k_s[s_, :] for s_ in sl]
        kbn = [kn[p] * betac[p] for p in rng]
        decay = [jnp.exp(jnp.where(tril, gc2[p] - grow[p], -1e30)) for p in rng]
        kk = [_mm_nt(kbn[p], kn[p]) for p in rng]
        y = [-jnp.where(strict, kk[p] * decay[p], 0.0) for p in rng]
        pm = [eye + y_ for y_ in y]
        pw = 1
        while 2 * pw < c:
            y = [_mm(y_, y_) for y_ in y]
            pm = [pm[p] + _mm(pm[p], y[p]) for p in rng]
            pw *= 2
        egn = [jnp.exp(g_) for g_ in gc2]
        uw = [_mm(pm[p], jnp.concatenate([v_s[sl[p], :] * betac[p], kbn[p] * egn[p]], axis=1))
              for p in rng]
        qn = [q_s[s_, :] for s_ in sl]
        qk = [_mm_nt(qn[p], kn[p]) for p in rng]
        attn = [jnp.where(tril, qk[p] * decay[p], 0.0) for p in rng]
        auw = [_mm(attn[p], uw[p]) for p in rng]
        for p in rng:
            qp_s[sl[p], :] = qn[p] * egn[p] - auw[p][:, LANE:]
            op_s[sl[p], :] = auw[p][:, :LANE]
        gl = [jnp.where(first, g_[c - 1:c, :], g_[c2 - 1:c2, :]) for g_ in gc2]
        kdt = [(kn[p] * jnp.exp(gl[p] - gc2[p])).T for p in rng]
        nma = [_mm(kdt[p][:, 0:c], uw[p][0:c]) for p in rng]
        nmb = [_mm(kdt[p][:, c:c2], uw[p][c:c2]) for p in rng]
        mb = [x[:, LANE:] for x in nmb]
        mbnm = [_mm(mb[p], nma[p]) for p in rng]
        ma = [x[:, LANE:] for x in nma]
        na = [x[:, :LANE] for x in nma]
        nb_ = [x[:, :LANE] for x in nmb]
        mba = [x[:, LANE:] for x in mbnm]
        mbn = [x[:, :LANE] for x in mbnm]
        for p in rng:
            ga = jnp.exp(gc2[p][c - 1:c, :])
            gb = jnp.exp(gc2[p][c2 - 1:c2, :])
            mm_s[2 * n2[p]] = ma[p].astype(BF16)
            nn_s[2 * n2[p]] = na[p]
            gl_s[pl.ds(2 * n2[p], 1), :] = ga
            mm_s[2 * n2[p] + 1] = (gb * ma[p] + ga * mb[p] - mba[p]).astype(BF16)
            nn_s[2 * n2[p] + 1] = gb * na[p] - mbn[p] + nb_[p]
            gl_s[pl.ds(2 * n2[p] + 1, 1), :] = ga * gb
        return carry

    lax.fori_loop(0, nchunk // (2 * pairs_per_iter), prep, 0)

    def second_out(n2, s_mid):
        rows = pl.ds(pl.multiple_of(n2 * c2, c2) + c, c)
        o_s[rows, :] = (jnp.dot(qp_s[rows, :].astype(BF16), s_mid.astype(BF16),
                                 preferred_element_type=F32) + op_s[rows, :])

    def scan(n2, carry):
        s, s_mid_prev = carry
        second_out(jnp.maximum(n2 - 1, 0), s_mid_prev)
        ra = pl.ds(pl.multiple_of(n2 * c2, c2), c)
        sb = s.astype(BF16)
        s_mid = (s * gl_s[pl.ds(2 * n2, 1), :] - jnp.dot(mm_s[2 * n2], sb, preferred_element_type=F32)
                 + nn_s[2 * n2])
        s_new = (s * gl_s[pl.ds(2 * n2 + 1, 1), :]
                 - jnp.dot(mm_s[2 * n2 + 1], sb, preferred_element_type=F32) + nn_s[2 * n2 + 1])
        o_s[ra, :] = jnp.dot(qp_s[ra, :].astype(BF16), sb, preferred_element_type=F32) + op_s[ra, :]
        return s_new, s_mid

    zero_state = jnp.zeros((LANE, LANE), F32)
    _, s_mid_last = lax.fori_loop(0, nchunk // 2, scan, (zero_state, zero_state))
    second_out(jnp.int32(nchunk // 2 - 1), s_mid_last)

    o = o_s[...]
    on = o * lax.rsqrt(jnp.mean(o * o, axis=1, keepdims=True) + EPS) * nw_ref[...]
    z = z_ref[0].astype(F32)
    o_ref[0] = (on * (z * _sigmoid(z))).astype(o_ref.dtype)


def _gdn(proj3, small3, conv_w, a_log, dt_bias, norm_w):
    b, t, _ = proj3.shape
    hh = N_HEADS_GDN
    c = GDN_CHUNK
    col = lambda cb: pl.BlockSpec((1, t, LANE), lambda bi, hi: (bi, 0, cb + hi))
    cw = lambda off: pl.BlockSpec((CONV_WIDTH, LANE), lambda bi, hi: (0, off + hi))
    hrow = pl.BlockSpec((1, 1, LANE), lambda bi, hi: (hi, 0, 0))
    alog_b = jnp.broadcast_to(a_log.astype(F32)[:, None, None], (hh, 1, LANE))
    dtb_b = jnp.broadcast_to(dt_bias.astype(F32)[:, None, None], (hh, 1, LANE))
    big = pltpu.VMEM((t, LANE), F32)
    nrow = t // LANE
    ba = small3[:, :, SMALL_BETA:SMALL_BETA + 2 * hh].transpose(0, 2, 1).reshape(b, 2 * hh, nrow, LANE)
    dense = lambda off: pl.BlockSpec((1, 1, nrow, LANE), lambda bi, hi: (bi, off + hi, 0, 0))
    return pl.pallas_call(
        _gdn_kernel,
        grid=(b, hh),
        in_specs=[col(CB_GQKV), col(CB_GQKV + hh), col(CB_GQKV + 2 * hh), col(CB_Z),
                  dense(0), dense(hh),
                  cw(0), cw(hh), cw(2 * hh), hrow, hrow,
                  pl.BlockSpec((1, LANE), lambda bi, hi: (0, 0))],
        out_specs=pl.BlockSpec((1, t, LANE), lambda bi, hi: (bi, 0, hi)),
        out_shape=jax.ShapeDtypeStruct((b, t, hh * LANE), BF16),
        scratch_shapes=[big, big, big, big, big, big, pltpu.VMEM((t + 8, LANE), F32),
                        pltpu.VMEM((nrow, LANE), F32), pltpu.VMEM((nrow, LANE), F32),
                        pltpu.VMEM((t // c, LANE), F32),
                        pltpu.VMEM((t // c, LANE, LANE), BF16),
                        pltpu.VMEM((t // c, LANE, LANE), F32)],
        compiler_params=_cparams(("parallel", "parallel")),
        name="gdn",
    )(proj3, proj3, proj3, proj3, ba, ba, conv_w, conv_w, conv_w, alog_b, dtb_b,
      norm_w.reshape(1, LANE))


def _split3(a):
    hi = a.astype(BF16)
    lo = (a - hi.astype(F32)).astype(BF16)
    return hi, lo


def _out_proj_kernel(oc_ref, os_ref, ow_ref, ob_ref, x_ref, wo_ref, fw_ref, wr_ref, br_ref,
                     h_ref, hn_ref, ids_ref, wts_ref, tmp_ref):
    half = oc_ref.shape[1]
    oa = (oc_ref[...].astype(F32) + os_ref[...].astype(F32) + ow_ref[...].astype(F32)).astype(BF16)
    h1 = (x_ref[...] + jnp.dot(oa, wo_ref[0:half, :], preferred_element_type=F32)
          + jnp.dot(ob_ref[...], wo_ref[half:2 * half, :], preferred_element_type=F32))
    h_ref[...] = h1
    hn = h1 * lax.rsqrt(jnp.mean(h1 * h1, axis=1, keepdims=True) + EPS) * fw_ref[...]
    _store_token_major(hn_ref, hn, tmp_ref)

    a_hi, a_lo = _split3(hn)
    w_hi, w_lo = _split3(wr_ref[...])
    dot = lambda a, b: jnp.dot(a, b, preferred_element_type=F32)
    logits = dot(a_hi, w_hi) + dot(a_hi, w_lo) + dot(a_lo, w_hi) + br_ref[...]
    lane = lax.broadcasted_iota(jnp.int32, logits.shape, 1)
    big = 1e30
    is_g = lane < N_GROUPS
    lg = jnp.where(is_g, logits, -big)
    gm = jnp.max(lg, axis=1, keepdims=True)
    grp = jnp.min(jnp.where(lg == gm, lane, LANE), axis=1, keepdims=True)
    p_grp = 1.0 / jnp.sum(jnp.where(is_g, jnp.exp(lg - gm), 0.0), axis=1, keepdims=True)
    e_id = lane - N_GROUPS
    in_g = (e_id >= 0) & (e_id < N_EXPERTS) & ((e_id // EXPERTS_PER_GROUP) == grp)
    le = jnp.where(in_g, logits, -big)
    em = jnp.max(le, axis=1, keepdims=True)
    pe = jnp.where(in_g, jnp.exp(le - em), 0.0)
    pe = pe / jnp.sum(pe, axis=1, keepdims=True)
    pm = jnp.where(in_g, pe, -1.0)
    p1 = jnp.max(pm, axis=1, keepdims=True)
    i1 = jnp.min(jnp.where(pm == p1, lane, LANE), axis=1, keepdims=True)
    pm2 = jnp.where(lane == i1, -1.0, pm)
    p2 = jnp.max(pm2, axis=1, keepdims=True)
    i2 = jnp.min(jnp.where(pm2 == p2, lane, LANE), axis=1, keepdims=True)
    den = p1 + p2
    ids_ref[...] = jnp.where(lane == 0, i1 - N_GROUPS, jnp.where(lane == 1, i2 - N_GROUPS, 0))
    wts_ref[...] = jnp.where(lane == 0, p1 / den * p_grp, jnp.where(lane == 1, p2 / den * p_grp, 0.0))


def _out_proj(oc, os_, ow, ob, x2, w_out, ffn_w, wr, br):
    n, d = x2.shape
    half = oc.shape[1]
    seg = d // LANE
    tm = min(512, n)
    row = lambda w: pl.BlockSpec((tm, w), lambda i: (i, 0))
    full = lambda a: pl.BlockSpec(a.shape, lambda i: (0,) * a.ndim, pipeline_mode=pl.Buffered(1))
    fw = ffn_w.reshape(1, d)
    return pl.pallas_call(
        _out_proj_kernel,
        grid=(n // tm,),
        in_specs=[row(half), row(half), row(half), row(half), row(d), full(w_out), full(fw),
                  full(wr), full(br)],
        out_specs=[row(d), pl.BlockSpec((tm * seg, LANE), lambda i: (i, 0)), row(LANE), row(LANE)],
        out_shape=[jax.ShapeDtypeStruct((n, d), F32), jax.ShapeDtypeStruct((n * seg, LANE), BF16),
                   jax.ShapeDtypeStruct((n, LANE), jnp.int32), jax.ShapeDtypeStruct((n, LANE), F32)],
        scratch_shapes=[pltpu.VMEM((tm * seg, LANE), F32)],
        compiler_params=_cparams(("parallel",)),
        name="out_proj",
    )(oc, os_, ow, ob, x2, w_out, fw, wr, br)


def _moe_kernel(bexp_ref, bact_ref, bfirst_ref, bord_ref, bnext_ref, rtok_ref, rtokn_ref, rslotp_ref,
                rslot_ref, hn_hbm, wg_hbm, wu_hbm, wd_hbm, y_hbm, xbuf, ybuf, tmp, wg_st, wu_st, wd_st,
                wgb, wub, wdb, gsem, ssem, wsem, *, n_real):
    i = pl.program_id(0)
    nb = pl.num_programs(0)
    rb = rtok_ref.shape[2]
    seg = ybuf.shape[1] // rb
    slot = i & 1
    other = 1 - slot
    active = bact_ref[i] > 0
    prev_active = jnp.logical_and(i > 0, bact_ref[jnp.maximum(i - 1, 0)] > 0)

    def gather_copy(tok, r, buf):
        return pltpu.make_async_copy(hn_hbm.at[pl.ds(pl.multiple_of(tok, seg), seg), :],
                                     xbuf.at[buf, pl.ds(r * seg, seg), :], gsem.at[buf])

    def scatter_copy(dst, r, buf):
        return pltpu.make_async_copy(ybuf.at[buf, pl.ds(r * seg, seg), :],
                                     y_hbm.at[pl.ds(pl.multiple_of(dst, seg), seg), :], ssem.at[buf])

    def wait_gather(buf):
        pltpu.make_async_copy(hn_hbm.at[pl.ds(0, rb * seg), :], xbuf.at[buf], gsem.at[buf]).wait()

    def wait_scatter(buf):
        pltpu.make_async_copy(ybuf.at[buf], y_hbm.at[pl.ds(0, rb * seg), :], ssem.at[buf]).wait()

    def weight_copies(e, ws):
        return (pltpu.make_async_copy(wg_hbm.at[e], wg_st.at[ws], wsem.at[ws]),
                pltpu.make_async_copy(wu_hbm.at[e], wu_st.at[ws], wsem.at[ws]),
                pltpu.make_async_copy(wd_hbm.at[e], wd_st.at[ws], wsem.at[ws]))

    @pl.when(i == 0)
    def _():
        for cp in weight_copies(bexp_ref[0], 0):
            cp.start(priority=1)
        ybuf[...] = jnp.zeros(ybuf.shape, ybuf.dtype)
        pltpu.make_async_copy(ybuf.at[0], y_hbm.at[pl.ds(n_real * seg, rb * seg), :],
                              ssem.at[0]).start()

        def first(r, carry):
            gather_copy(rtok_ref[0, 0, r], r, 0).start()
            return carry
        lax.fori_loop(0, rb, first, 0)

    @pl.when(jnp.logical_or(i == 0, prev_active))
    def _():
        wait_gather(slot)

    @pl.when(active)
    def _():
        @pl.when(bfirst_ref[i] > 0)
        def _():
            ws = bord_ref[i] & 1
            for cp in weight_copies(bexp_ref[i], ws):
                cp.wait()

            @pl.when(bnext_ref[i] >= 0)
            def _():
                for cp in weight_copies(bnext_ref[i], 1 - ws):
                    cp.start(priority=1)

            wgb[...] = wg_st[ws].astype(BF16)
            wub[...] = wu_st[ws].astype(BF16)
            wdb[...] = wd_st[ws].astype(BF16)

        n_piece = 8
        per = rb // n_piece

        def issue(piece):
            half_n = n_piece // 2
            if piece < half_n:
                for r in range(piece * 2 * per, (piece + 1) * 2 * per):
                    gather_copy(rtokn_ref[0, 0, r], r, other).start()
            else:
                for r in range((piece - half_n) * 2 * per, (piece - half_n + 1) * 2 * per):
                    scatter_copy(rslotp_ref[0, 0, r], r, other).start()

        x = _load_token_major(xbuf.at[slot], rb, tmp).astype(BF16)
        de = wgb.shape[1]
        d = wdb.shape[1]
        hc = de // 2
        acts = []
        for c in range(2):
            issue(2 * c)
            hg = jnp.dot(x, wgb[:, c * hc:(c + 1) * hc], preferred_element_type=F32)
            issue(2 * c + 1)
            hu = jnp.dot(x, wub[:, c * hc:(c + 1) * hc], preferred_element_type=F32)
            acts.append((hg * _sigmoid(hg) * hu).astype(BF16))
        act = jnp.concatenate(acts, axis=1)
        dc = d // 4
        ys = []
        for j in range(4):
            issue(4 + j)
            ys.append(jnp.dot(act, wdb[:, j * dc:(j + 1) * dc], preferred_element_type=F32))
        wait_scatter(slot)
        _store_token_major(ybuf.at[slot], jnp.concatenate(ys, axis=1), tmp)

    def scatter_all(idx_ref, buf):
        def body(r, carry):
            scatter_copy(idx_ref[0, 0, r], r, buf).start()
            return carry
        lax.fori_loop(0, rb, body, 0)

    @pl.when(jnp.logical_and(jnp.logical_not(active), prev_active))
    def _():
        wait_scatter(slot)
        scatter_all(rslotp_ref, other)
        wait_scatter(other)

    @pl.when(jnp.logical_and(i == nb - 1, active))
    def _():
        wait_scatter(other)
        scatter_all(rslot_ref, slot)
        wait_scatter(slot)
        wait_gather(other)


def _moe(hn, bexp, bact, rtok, rslot, w_gate, w_up, w_down, n_real):
    d, de = w_gate.shape[1], w_gate.shape[2]
    seg = d // LANE
    nb = bexp.shape[0]
    rb = MOE_ROW_BLOCK
    idx = jnp.arange(nb, dtype=jnp.int32)
    prev_e = jnp.concatenate([jnp.full((1,), -1, jnp.int32), bexp[:-1]])
    bfirst = jnp.logical_and(bact > 0, jnp.logical_or(idx == 0, bexp != prev_e)).astype(jnp.int32)
    bord = jnp.cumsum(bfirst) - 1
    first_at = jnp.where(bfirst > 0, idx, nb)
    next_first = jnp.concatenate([lax.cummin(first_at[::-1])[::-1][1:], jnp.full((1,), nb, jnp.int32)])
    bnext = jnp.where(next_first < nb, bexp[jnp.minimum(next_first, nb - 1)], -1).astype(jnp.int32)

    smem_rows = pl.BlockSpec((1, 1, rb), lambda i, *_: (i, 0, 0), memory_space=pltpu.SMEM)
    smem_next = pl.BlockSpec((1, 1, rb), lambda i, *_: (jnp.minimum(i + 1, nb - 1), 0, 0),
                             memory_space=pltpu.SMEM)
    smem_cur = pl.BlockSpec((1, 1, rb), lambda i, *_: (i + 1, 0, 0), memory_space=pltpu.SMEM)
    hbm = pl.BlockSpec(memory_space=pl.ANY)
    grid_spec = pltpu.PrefetchScalarGridSpec(
        num_scalar_prefetch=5,
        grid=(nb,),
        in_specs=[smem_rows, smem_next, smem_rows, smem_cur, hbm, hbm, hbm, hbm],
        out_specs=hbm,
        scratch_shapes=[pltpu.VMEM((2, rb * seg, LANE), BF16), pltpu.VMEM((2, rb * seg, LANE), BF16),
                        pltpu.VMEM((rb * seg, LANE), F32),
                        pltpu.VMEM((2, d, de), F32), pltpu.VMEM((2, d, de), F32), pltpu.VMEM((2, de, d), F32),
                        pltpu.VMEM((d, de), BF16), pltpu.VMEM((d, de), BF16), pltpu.VMEM((de, d), BF16),
                        pltpu.SemaphoreType.DMA((2,)), pltpu.SemaphoreType.DMA((2,)),
                        pltpu.SemaphoreType.DMA((2,))])
    return pl.pallas_call(
        functools.partial(_moe_kernel, n_real=n_real),
        grid_spec=grid_spec,
        out_shape=jax.ShapeDtypeStruct(((n_real + 2 * rb) * seg, LANE), BF16),
        compiler_params=_cparams(("arbitrary",)),
        name="moe",
    )(bexp, bact, bfirst, bord, bnext, rtok, rtok, rslot, rslot, hn, w_gate, w_up, w_down)


def _dispatch(ids, n, seg):
    k = 2
    m = n * k
    rb = MOE_ROW_BLOCK
    e_flat = ids[:, :k].reshape(m)
    ch = 256
    onehot = (e_flat[:, None] == jnp.arange(N_EXPERTS, dtype=jnp.int32)[None, :]).astype(F32)
    ohc = onehot.reshape(m // ch, ch, N_EXPERTS)
    tri = jnp.tril(jnp.ones((ch, ch), F32))
    within = jnp.einsum('ij,bjk->bik', tri, ohc)
    tot = within[:, -1, :]
    before = jnp.cumsum(tot, axis=0) - tot
    rank = (jnp.sum((within + before[:, None, :]) * ohc, axis=-1).reshape(m) - 1.0).astype(jnp.int32)
    counts = (before[-1] + tot[-1]).astype(jnp.int32)
    padded = (counts + rb - 1) // rb * rb
    pad_end = jnp.cumsum(padded)
    pad_start = pad_end - padded
    dest = jnp.sum(onehot * pad_start.astype(F32)[None, :], axis=-1).astype(jnp.int32) + rank
    nb = (m + N_EXPERTS * (rb - 1) + rb - 1) // rb
    p = nb * rb
    row_m = jnp.full((p,), -1, jnp.int32).at[dest].set(jnp.arange(m, dtype=jnp.int32))
    real = row_m >= 0
    rtok = jnp.where(real, row_m // k, 0)
    pidx = jnp.arange(p, dtype=jnp.int32)
    rslot = jnp.where(real, (row_m % k) * n + row_m // k, m + ((pidx // rb) % 2) * rb + pidx % rb)
    starts = jnp.arange(nb, dtype=jnp.int32) * rb
    bexp = jnp.minimum(jnp.sum((pad_end[None, :] <= starts[:, None]).astype(jnp.int32), axis=1),
                       N_EXPERTS - 1)
    bact = jnp.sum(real.reshape(nb, rb).astype(jnp.int32), axis=1)
    last_e = jnp.max(jnp.where(bact > 0, bexp, 0))
    bexp = jnp.where(bact > 0, bexp, last_e)
    rslot = jnp.concatenate([m + rb + jnp.arange(rb, dtype=jnp.int32), rslot])
    return (bexp, bact, (rtok * seg).reshape(nb, 1, rb), (rslot * seg).reshape(nb + 1, 1, rb),
            m)


def _combine_kernel(h_ref, y0_ref, y1_ref, wts_ref, fw_ref, o_ref, tmp_ref):
    tm = h_ref.shape[0]
    wts = wts_ref[...]
    y0 = _load_token_major(y0_ref, tm, tmp_ref)
    y1 = _load_token_major(y1_ref, tm, tmp_ref)
    moe = y0 * wts[:, 0:1] + y1 * wts[:, 1:2]
    h = h_ref[...] + moe
    o_ref[...] = h * lax.rsqrt(jnp.mean(h * h, axis=1, keepdims=True) + EPS) * fw_ref[...]


def _combine(h1, y, wts, final_w):
    n, d = h1.shape
    seg = d // LANE
    tm = min(512, n)
    nt = n // tm
    return pl.pallas_call(
        _combine_kernel,
        grid=(nt,),
        in_specs=[pl.BlockSpec((tm, d), lambda i: (i, 0)),
                  pl.BlockSpec((tm * seg, LANE), lambda i: (i, 0)),
                  pl.BlockSpec((tm * seg, LANE), lambda i: (i + nt, 0)),
                  pl.BlockSpec((tm, LANE), lambda i: (i, 0)),
                  pl.BlockSpec((1, d), lambda i: (0, 0))],
        out_specs=pl.BlockSpec((tm, d), lambda i: (i, 0)),
        out_shape=jax.ShapeDtypeStruct((n, d), F32),
        scratch_shapes=[pltpu.VMEM((tm * seg, LANE), F32)],
        compiler_params=_cparams(("parallel",)),
        name="combine",
    )(h1, y, y, wts, final_w.reshape(1, d))


def _rope_tables(positions):
    half = ROT_DIM // 2
    inv_freq = ROPE_THETA ** (-jnp.arange(0, ROT_DIM, 2, dtype=F32) / ROT_DIM)
    ang = positions.astype(F32)[..., None] * inv_freq
    cos, sin = jnp.cos(ang), jnp.sin(ang)
    b, t = positions.shape
    ones = jnp.ones((b, t, LANE - ROT_DIM), F32)
    zeros = jnp.zeros((b, t, LANE - half), F32)
    c = jnp.concatenate([cos, cos, ones], axis=-1)
    sa = jnp.concatenate([-sin, zeros], axis=-1)
    sb = jnp.concatenate([jnp.zeros((b, t, half), F32), sin, zeros[..., :LANE - ROT_DIM]], axis=-1)
    return c, sa, sb


def _arrange_w_in(w_in):
    d = w_in.shape[0]
    sizes = (N_HEADS_NSA * HEAD_DIM, 3 * 2 * N_KV_NSA * HEAD_DIM, 3 * N_HEADS_NSA,
             3 * N_HEADS_GDN * HEAD_DIM, N_HEADS_GDN, N_HEADS_GDN, N_HEADS_GDN * HEAD_DIM)
    offs = np.cumsum((0,) + sizes)
    seg = [w_in[:, offs[i]:offs[i + 1]].astype(BF16) for i in range(len(sizes))]
    q, kv, gate, gqkv, gb, ga, gz = seg
    used = sum(sizes)
    pad = jnp.zeros((d, N_CB * LANE - used), BF16)
    return jnp.concatenate([q, kv, gqkv, gz, gate, gb, ga, pad], axis=1)


def kernel(x, positions, attn_norm_w, w_in, cmp_wk, cmp_pek, cmp_wv, cmp_pev, gdn_conv_w, gdn_a_log,
           gdn_dt_bias, gdn_norm_w, w_out, ffn_norm_w, router_group_w, router_group_b,
           router_expert_w, router_expert_b, moe_w_gate, moe_w_up, moe_w_down, final_norm_w):
    b, t, d = x.shape
    n = b * t
    tabs = _rope_tables(positions)
    h = x.reshape(n, d)
    assert w_in.shape[0] == 1, "single-layer block only"
    for l in range(1):
        proj, small = _in_proj(h, attn_norm_w[l], _arrange_w_in(w_in[l]))
        proj3 = proj.reshape(b, t, N_CB * LANE)
        small3 = small.reshape(b, t, LANE)
        cmp_w = jnp.stack([cmp_wk[l], cmp_wv[l]])
        cmp_pe = jnp.stack([cmp_pek[l], cmp_pev[l]])
        kvc = _nsa_compress(proj3, tabs, cmp_w, cmp_pe)
        o_c, selbias = _nsa_cmp_attn(proj3, small3, tabs, kvc)
        o_s = _nsa_sel_attn(proj3, small3, tabs, selbias)
        o_w = _nsa_win_attn(proj3, small3, tabs)
        o_b = _gdn(proj3, small3, gdn_conv_w[l], gdn_a_log[l], gdn_dt_bias[l], gdn_norm_w[l])
        half = N_HEADS_NSA * HEAD_DIM
        wr = jnp.concatenate([router_group_w[l], router_expert_w[l],
                              jnp.zeros((d, LANE - N_GROUPS - N_EXPERTS), F32)], axis=1)
        br = jnp.concatenate([router_group_b[l], router_expert_b[l],
                              jnp.zeros((LANE - N_GROUPS - N_EXPERTS,), F32)]).reshape(1, LANE)
        h1, hn2, ids, wts = _out_proj(o_c.reshape(n, half), o_s.reshape(n, half), o_w.reshape(n, half),
                                      o_b.reshape(n, half), h, w_out[l].astype(BF16), ffn_norm_w[l], wr, br)
        bexp, bact, rtok, rslot, n_slots = _dispatch(ids, n, d // LANE)
        y = _moe(hn2, bexp, bact, rtok, rslot, moe_w_gate[l], moe_w_up[l], moe_w_down[l], n_slots)
        out = _combine(h1, y, wts, final_norm_w)
    return out.reshape(b, t, d)
```
